```python
import math
import jax, jax.numpy as jnp
from jax import lax
import numpy as np

D_MODEL = 1024
BATCH = 8
SEQ = 4096
DEPTH = 2

SSM_HEADS = 16
SSM_HEAD_DIM = 64
SSM_D_INNER = SSM_HEADS * SSM_HEAD_DIM
SSM_GROUPS = 2
SSM_STATE = 128
SSM_CONV_DIM = SSM_D_INNER + 2 * SSM_GROUPS * SSM_STATE
GDN_HEADS = 8
GDN_HEAD_K = 128
GDN_HEAD_V = 128
GDN_K_DIM = GDN_HEADS * GDN_HEAD_K
GDN_V_DIM = GDN_HEADS * GDN_HEAD_V
GDN_QKV_DIM = 2 * GDN_K_DIM + GDN_V_DIM
CONV_K = 4
CHUNK = 64
FFN_HIDDEN = -(-8 * D_MODEL // (3 * 256)) * 256
EPS = 1e-6
IN_SPLIT_SIZES = (SSM_D_INNER, SSM_CONV_DIM, SSM_HEADS,
                  GDN_QKV_DIM, GDN_V_DIM, GDN_HEADS, GDN_HEADS,
                  D_MODEL, D_MODEL)
IN_DIM = sum(IN_SPLIT_SIZES)
IN_SPLIT_IDX = tuple(int(i) for i in np.cumsum(IN_SPLIT_SIZES)[:-1])

kernel_name = "hybrid_ssd_gdn_gated_merge_block"


def rmsnorm(x, w):
    xf = x.astype(jnp.float32)
    y = xf * lax.rsqrt(jnp.mean(xf * xf, axis=-1, keepdims=True) + EPS)
    return (y * w.astype(jnp.float32)).astype(x.dtype)


def causal_dwconv(u, w):
    c = u.shape[-1]
    return lax.conv_general_dilated(
        u, w[:, None, :].astype(u.dtype), window_strides=(1,),
        padding=[(CONV_K - 1, 0)], dimension_numbers=("NWC", "WIO", "NWC"),
        feature_group_count=c)


def to_chunks(t):
    b, s = t.shape[:2]
    return jnp.moveaxis(t.reshape(b, s // CHUNK, CHUNK, *t.shape[2:]), 1, 0)


def from_chunks(t):
    t = jnp.moveaxis(t, 0, 1)
    return t.reshape(t.shape[0], t.shape[1] * t.shape[2], *t.shape[3:])


def ssd_scan(xdt, a, bm, cm):
    bsz, s = xdt.shape[:2]
    r = SSM_HEADS // SSM_GROUPS
    xdt = xdt.reshape(bsz, s, SSM_GROUPS, r, SSM_HEAD_DIM)
    a = a.reshape(bsz, s, SSM_GROUPS, r)
    causal = jnp.tril(jnp.ones((CHUNK, CHUNK), dtype=bool))[None, :, :, None, None]

    def body(state, inp):
        xc, ac, bc, cc = inp
        acum = jnp.cumsum(ac, axis=1)
        seg = acum[:, :, None] - acum[:, None, :]
        lmat = jnp.exp(jnp.where(causal, seg, -jnp.inf))
        cb = jnp.einsum("bign,bjgn->bijg", cc, bc)
        y_diag = jnp.einsum("bijg,bijgr,bjgrp->bigrp", cb, lmat, xc)
        y_off = jnp.einsum("bign,bgrpn->bigrp", cc, state) * jnp.exp(acum)[..., None]
        decay_end = jnp.exp(acum[:, -1:] - acum)
        new_state = state * jnp.exp(acum[:, -1])[..., None, None] + jnp.einsum(
            "bjgn,bjgr,bjgrp->bgrpn", bc, decay_end, xc)
        return new_state, y_diag + y_off

    state0 = jnp.zeros((bsz, SSM_GROUPS, r, SSM_HEAD_DIM, SSM_STATE), jnp.float32)
    _, y = lax.scan(body, state0, (to_chunks(xdt), to_chunks(a), to_chunks(bm), to_chunks(cm)))
    return from_chunks(y).reshape(bsz, s, SSM_HEADS, SSM_HEAD_DIM)


def mamba2_mixer(z, xbc, dt_raw, conv_w, conv_b, dt_bias, a_log, d_skip, norm_w):
    dtype = z.dtype
    bsz, s = z.shape[:2]
    xbc = jax.nn.silu(causal_dwconv(xbc, conv_w) + conv_b.astype(xbc.dtype))
    xs, bm, cm = jnp.split(xbc.astype(jnp.float32),
                           [SSM_D_INNER, SSM_D_INNER + SSM_GROUPS * SSM_STATE], axis=-1)
    xs = xs.reshape(bsz, s, SSM_HEADS, SSM_HEAD_DIM)
    bm = bm.reshape(bsz, s, SSM_GROUPS, SSM_STATE)
    cm = cm.reshape(bsz, s, SSM_GROUPS, SSM_STATE)
    dt = jax.nn.softplus(dt_raw.astype(jnp.float32) + dt_bias.astype(jnp.float32))
    a = -jnp.exp(a_log.astype(jnp.float32))
    y = ssd_scan(xs * dt[..., None], dt * a, bm, cm)
    y = y + d_skip.astype(jnp.float32)[:, None] * xs
    y = y.reshape(bsz, s, SSM_D_INNER) * jax.nn.silu(z.astype(jnp.float32))
    y = y.reshape(bsz, s, SSM_GROUPS, SSM_D_INNER // SSM_GROUPS)
    y = y * lax.rsqrt(jnp.mean(y * y, axis=-1, keepdims=True) + EPS)
    y = y.reshape(bsz, s, SSM_D_INNER) * norm_w.astype(jnp.float32)
    return y.astype(dtype)


def gdn_scan(q, k, v, g, beta):
    bsz = q.shape[0]
    tril = jnp.tril(jnp.ones((CHUNK, CHUNK), dtype=bool))
    strict = jnp.tril(jnp.ones((CHUNK, CHUNK), dtype=bool), -1)
    eye = jnp.eye(CHUNK, dtype=jnp.float32)

    def body(state, inp):
        qc, kc, vc, gcv, bc = inp
        gc = jnp.cumsum(gcv, axis=1)
        gch = jnp.swapaxes(gc, 1, 2)
        decay = jnp.exp(jnp.where(tril, gch[..., :, None] - gch[..., None, :], -jnp.inf))
        kk = jnp.einsum("bihd,bjhd->bhij", kc, kc)
        amat = jnp.where(strict, kk * decay * jnp.swapaxes(bc, 1, 2)[..., :, None], 0.0)
        rhs = jnp.concatenate([vc * bc[..., None], kc * (bc * jnp.exp(gc))[..., None]], axis=-1)
        rhs = jnp.swapaxes(rhs, 1, 2)
        t = lax.linalg.triangular_solve(eye + amat, rhs, left_side=True, lower=True,
                                        unit_diagonal=True)
        u, w = t[..., :GDN_HEAD_V], t[..., GDN_HEAD_V:]
        v_new = u - jnp.einsum("bhqk,bhkv->bhqv", w, state)
        qk = jnp.einsum("bihd,bjhd->bhij", qc, kc) * decay
        o = (jnp.einsum("bihk,bhkv->bhiv", qc * jnp.exp(gc)[..., None], state)
             + jnp.einsum("bhij,bhjv->bhiv", qk, v_new))
        new_state = state * jnp.exp(gch[..., -1])[..., None, None] + jnp.einsum(
            "bjhk,bhjv->bhkv", kc * jnp.exp(gc[:, -1:] - gc)[..., None], v_new)
        return new_state, jnp.swapaxes(o, 1, 2)

    state0 = jnp.zeros((bsz, GDN_HEADS, GDN_HEAD_K, GDN_HEAD_V), jnp.float32)
    _, o = lax.scan(body, state0, (to_chunks(q), to_chunks(k), to_chunks(v),
                                   to_chunks(g), to_chunks(beta)))
    return from_chunks(o)


def gdn_mixer(qkv, z, a_raw, b_raw, conv_w, a_log, dt_bias, norm_w):
    dtype = z.dtype
    bsz, s = z.shape[:2]
    qkv = jax.nn.silu(causal_dwconv(qkv, conv_w)).astype(jnp.float32)
    q, k, v = jnp.split(qkv, [GDN_K_DIM, 2 * GDN_K_DIM], axis=-1)
    q = q.reshape(bsz, s, GDN_HEADS, GDN_HEAD_K)
    k = k.reshape(bsz, s, GDN_HEADS, GDN_HEAD_K)
    v = v.reshape(bsz, s, GDN_HEADS, GDN_HEAD_V)
    q = q * lax.rsqrt(jnp.sum(q * q, axis=-1, keepdims=True) + EPS) * (GDN_HEAD_K ** -0.5)
    k = k * lax.rsqrt(jnp.sum(k * k, axis=-1, keepdims=True) + EPS)
    beta = jax.nn.sigmoid(b_raw.astype(jnp.float32))
    g = -jnp.exp(a_log.astype(jnp.float32)) * jax.nn.softplus(
        a_raw.astype(jnp.float32) + dt_bias.astype(jnp.float32))
    o = gdn_scan(q, k, v, g, beta)
    o = o * lax.rsqrt(jnp.mean(o * o, axis=-1, keepdims=True) + EPS) * norm_w.astype(jnp.float32)
    o = o * jax.nn.silu(z.astype(jnp.float32).reshape(bsz, s, GDN_HEADS, GDN_HEAD_V))
    return o.reshape(bsz, s, GDN_V_DIM).astype(dtype)


def _fwd_setup_inputs(seed: int = 0) -> dict:
    key = jax.random.key(seed)
    ks = jax.random.split(key, 24)
    f32 = jnp.float32

    def nrm(k, shape, scale):
        return jax.random.normal(k, shape, f32) * scale

    def gain(k, shape):
        return 1.0 + 0.02 * jax.random.normal(k, shape, f32)

    def dt_bias_init(k, shape):
        dt = jnp.exp(jax.random.uniform(k, shape, f32, math.log(1e-3), math.log(1e-1)))
        return dt + jnp.log(-jnp.expm1(-dt))

    return {
        "x": jax.random.normal(ks[0], (BATCH, SEQ, D_MODEL), f32),
        "norm_mix_w": gain(ks[1], (DEPTH, D_MODEL)),
        "w_in": nrm(ks[2], (DEPTH, D_MODEL, IN_DIM), D_MODEL ** -0.5),
        "ssm_conv_w": nrm(ks[3], (DEPTH, CONV_K, SSM_CONV_DIM), CONV_K ** -0.5),
        "ssm_conv_b": nrm(ks[4], (DEPTH, SSM_CONV_DIM), 0.02),
        "ssm_dt_bias": dt_bias_init(ks[5], (DEPTH, SSM_HEADS)),
        "ssm_a_log": jnp.log(jax.random.uniform(ks[6], (DEPTH, SSM_HEADS), f32, 1.0, 16.0)),
        "ssm_d": gain(ks[7], (DEPTH, SSM_HEADS)),
        "ssm_norm_w": gain(ks[8], (DEPTH, SSM_D_INNER)),
        "gdn_conv_w": nrm(ks[9], (DEPTH, CONV_K, GDN_QKV_DIM), CONV_K ** -0.5),
        "gdn_a_log": jnp.log(jax.random.uniform(ks[10], (DEPTH, GDN_HEADS), f32, 1.0, 16.0)),
        "gdn_dt_bias": dt_bias_init(ks[11], (DEPTH, GDN_HEADS)),
        "gdn_norm_w": gain(ks[12], (DEPTH, GDN_HEAD_V)),
        "w_proj_ssm": nrm(ks[13], (DEPTH, SSM_D_INNER, D_MODEL), SSM_D_INNER ** -0.5),
        "w_proj_gdn": nrm(ks[14], (DEPTH, GDN_V_DIM, D_MODEL), GDN_V_DIM ** -0.5),
        "w_out": nrm(ks[15], (DEPTH, D_MODEL, D_MODEL), D_MODEL ** -0.5),
        "norm_ffn_w": gain(ks[16], (DEPTH, D_MODEL)),
        "w_ffn_in": nrm(ks[17], (DEPTH, D_MODEL, 2 * FFN_HIDDEN), D_MODEL ** -0.5),
        "w_ffn_down": nrm(ks[18], (DEPTH, FFN_HIDDEN, D_MODEL), FFN_HIDDEN ** -0.5),
        "final_norm_w": gain(ks[19], (D_MODEL,)),
    }


def _fwd_reference(x, norm_mix_w, w_in, ssm_conv_w, ssm_conv_b, ssm_dt_bias, ssm_a_log, ssm_d,
              ssm_norm_w, gdn_conv_w, gdn_a_log, gdn_dt_bias, gdn_norm_w, w_proj_ssm,
              w_proj_gdn, w_out, norm_ffn_w, w_ffn_in, w_ffn_down, final_norm_w):
    for l in range(DEPTH):
        h = rmsnorm(x, norm_mix_w[l])
        proj = h @ w_in[l]
        (ssm_z, ssm_xbc, ssm_dt, gdn_qkv, gdn_z, gdn_a, gdn_b,
         gate_ssm, gate_gdn) = jnp.split(proj, IN_SPLIT_IDX, axis=-1)
        y_ssm = mamba2_mixer(ssm_z, ssm_xbc, ssm_dt, ssm_conv_w[l], ssm_conv_b[l],
                             ssm_dt_bias[l], ssm_a_log[l], ssm_d[l], ssm_norm_w[l])
        y_gdn = gdn_mixer(gdn_qkv, gdn_z, gdn_a, gdn_b, gdn_conv_w[l], gdn_a_log[l],
                          gdn_dt_bias[l], gdn_norm_w[l])
        merged = (jax.nn.sigmoid(gate_ssm) * (y_ssm @ w_proj_ssm[l])
                  + jax.nn.sigmoid(gate_gdn) * (y_gdn @ w_proj_gdn[l]))
        x = x + merged @ w_out[l]
        h = rmsnorm(x, norm_ffn_w[l])
        gate, up = jnp.split(h @ w_ffn_in[l], [FFN_HIDDEN], axis=-1)
        x = x + (jax.nn.silu(gate) * up) @ w_ffn_down[l]
    return rmsnorm(x, final_norm_w)


import jax as _jax
import jax.numpy as _jnp

TWIN_FORMAT = 'train_step'
FWD_PARAMS = ['x', 'norm_mix_w', 'w_in', 'ssm_conv_w', 'ssm_conv_b', 'ssm_dt_bias', 'ssm_a_log', 'ssm_d', 'ssm_norm_w', 'gdn_conv_w', 'gdn_a_log', 'gdn_dt_bias', 'gdn_norm_w', 'w_proj_ssm', 'w_proj_gdn', 'w_out', 'norm_ffn_w', 'w_ffn_in', 'w_ffn_down', 'final_norm_w']
TWIN_WEIGHTS = ['norm_mix_w', 'w_in', 'ssm_conv_w', 'ssm_conv_b', 'ssm_dt_bias', 'ssm_a_log', 'ssm_d', 'ssm_norm_w', 'gdn_conv_w', 'gdn_a_log', 'gdn_dt_bias', 'gdn_norm_w', 'w_proj_ssm', 'w_proj_gdn', 'w_out', 'norm_ffn_w', 'w_ffn_in', 'w_ffn_down', 'final_norm_w']
TWIN_DIFF_INPUT = 'x'
TWIN_INPUTS = ['x', 'norm_mix_w', 'w_in', 'ssm_conv_w', 'ssm_conv_b', 'ssm_dt_bias', 'ssm_a_log', 'ssm_d', 'ssm_norm_w', 'gdn_conv_w', 'gdn_a_log', 'gdn_dt_bias', 'gdn_norm_w', 'w_proj_ssm', 'w_proj_gdn', 'w_out', 'norm_ffn_w', 'w_ffn_in', 'w_ffn_down', 'final_norm_w', 'loss_target', 'm_norm_mix_w', 'm_w_in', 'm_ssm_conv_w', 'm_ssm_conv_b', 'm_ssm_dt_bias', 'm_ssm_a_log', 'm_ssm_d', 'm_ssm_norm_w', 'm_gdn_conv_w', 'm_gdn_a_log', 'm_gdn_dt_bias', 'm_gdn_norm_w', 'm_w_proj_ssm', 'm_w_proj_gdn', 'm_w_out', 'm_norm_ffn_w', 'm_w_ffn_in', 'm_w_ffn_down', 'm_final_norm_w', 'v_norm_mix_w', 'v_w_in', 'v_ssm_conv_w', 'v_ssm_conv_b', 'v_ssm_dt_bias', 'v_ssm_a_log', 'v_ssm_d', 'v_ssm_norm_w', 'v_gdn_conv_w', 'v_gdn_a_log', 'v_gdn_dt_bias', 'v_gdn_norm_w', 'v_w_proj_ssm', 'v_w_proj_gdn', 'v_w_out', 'v_norm_ffn_w', 'v_w_ffn_in', 'v_w_ffn_down', 'v_final_norm_w']
TWIN_OUTPUTS = ['loss', 'grad_x', 'grad_norm_mix_w', 'grad_w_in', 'grad_ssm_conv_w', 'grad_ssm_conv_b', 'grad_ssm_dt_bias', 'grad_ssm_a_log', 'grad_ssm_d', 'grad_ssm_norm_w', 'grad_gdn_conv_w', 'grad_gdn_a_log', 'grad_gdn_dt_bias', 'grad_gdn_norm_w', 'grad_w_proj_ssm', 'grad_w_proj_gdn', 'grad_w_out', 'grad_norm_ffn_w', 'grad_w_ffn_in', 'grad_w_ffn_down', 'grad_final_norm_w', 'delta_norm_mix_w', 'delta_w_in', 'delta_ssm_conv_w', 'delta_ssm_conv_b', 'delta_ssm_dt_bias', 'delta_ssm_a_log', 'delta_ssm_d', 'delta_ssm_norm_w', 'delta_gdn_conv_w', 'delta_gdn_a_log', 'delta_gdn_dt_bias', 'delta_gdn_norm_w', 'delta_w_proj_ssm', 'delta_w_proj_gdn', 'delta_w_out', 'delta_norm_ffn_w', 'delta_w_ffn_in', 'delta_w_ffn_down', 'delta_final_norm_w', 'new_m_norm_mix_w', 'new_m_w_in', 'new_m_ssm_conv_w', 'new_m_ssm_conv_b', 'new_m_ssm_dt_bias', 'new_m_ssm_a_log', 'new_m_ssm_d', 'new_m_ssm_norm_w', 'new_m_gdn_conv_w', 'new_m_gdn_a_log', 'new_m_gdn_dt_bias', 'new_m_gdn_norm_w', 'new_m_w_proj_ssm', 'new_m_w_proj_gdn', 'new_m_w_out', 'new_m_norm_ffn_w', 'new_m_w_ffn_in', 'new_m_w_ffn_down', 'new_m_final_norm_w', 'new_v_norm_mix_w', 'new_v_w_in', 'new_v_ssm_conv_w', 'new_v_ssm_conv_b', 'new_v_ssm_dt_bias', 'new_v_ssm_a_log', 'new_v_ssm_d', 'new_v_ssm_norm_w', 'new_v_gdn_conv_w', 'new_v_gdn_a_log', 'new_v_gdn_dt_bias', 'new_v_gdn_norm_w', 'new_v_w_proj_ssm', 'new_v_w_proj_gdn', 'new_v_w_out', 'new_v_norm_ffn_w', 'new_v_w_ffn_in', 'new_v_w_ffn_down', 'new_v_final_norm_w']
TWIN_LEAF_KINDS = {'loss': 'loss', 'grad_x': 'grad_x', 'grad_norm_mix_w': 'grad_w', 'grad_w_in': 'grad_w', 'grad_ssm_conv_w': 'grad_w', 'grad_ssm_conv_b': 'grad_w', 'grad_ssm_dt_bias': 'grad_w', 'grad_ssm_a_log': 'grad_w', 'grad_ssm_d': 'grad_w', 'grad_ssm_norm_w': 'grad_w', 'grad_gdn_conv_w': 'grad_w', 'grad_gdn_a_log': 'grad_w', 'grad_gdn_dt_bias': 'grad_w', 'grad_gdn_norm_w': 'grad_w', 'grad_w_proj_ssm': 'grad_w', 'grad_w_proj_gdn': 'grad_w', 'grad_w_out': 'grad_w', 'grad_norm_ffn_w': 'grad_w', 'grad_w_ffn_in': 'grad_w', 'grad_w_ffn_down': 'grad_w', 'grad_final_norm_w': 'grad_w', 'delta_norm_mix_w': 'delta_w', 'delta_w_in': 'delta_w', 'delta_ssm_conv_w': 'delta_w', 'delta_ssm_conv_b': 'delta_w', 'delta_ssm_dt_bias': 'delta_w', 'delta_ssm_a_log': 'delta_w', 'delta_ssm_d': 'delta_w', 'delta_ssm_norm_w': 'delta_w', 'delta_gdn_conv_w': 'delta_w', 'delta_gdn_a_log': 'delta_w', 'delta_gdn_dt_bias': 'delta_w', 'delta_gdn_norm_w': 'delta_w', 'delta_w_proj_ssm': 'delta_w', 'delta_w_proj_gdn': 'delta_w', 'delta_w_out': 'delta_w', 'delta_norm_ffn_w': 'delta_w', 'delta_w_ffn_in': 'delta_w', 'delta_w_ffn_down': 'delta_w', 'delta_final_norm_w': 'delta_w', 'new_m_norm_mix_w': 'new_m', 'new_m_w_in': 'new_m', 'new_m_ssm_conv_w': 'new_m', 'new_m_ssm_conv_b': 'new_m', 'new_m_ssm_dt_bias': 'new_m', 'new_m_ssm_a_log': 'new_m', 'new_m_ssm_d': 'new_m', 'new_m_ssm_norm_w': 'new_m', 'new_m_gdn_conv_w': 'new_m', 'new_m_gdn_a_log': 'new_m', 'new_m_gdn_dt_bias': 'new_m', 'new_m_gdn_norm_w': 'new_m', 'new_m_w_proj_ssm': 'new_m', 'new_m_w_proj_gdn': 'new_m', 'new_m_w_out': 'new_m', 'new_m_norm_ffn_w': 'new_m', 'new_m_w_ffn_in': 'new_m', 'new_m_w_ffn_down': 'new_m', 'new_m_final_norm_w': 'new_m', 'new_v_norm_mix_w': 'new_v', 'new_v_w_in': 'new_v', 'new_v_ssm_conv_w': 'new_v', 'new_v_ssm_conv_b': 'new_v', 'new_v_ssm_dt_bias': 'new_v', 'new_v_ssm_a_log': 'new_v', 'new_v_ssm_d': 'new_v', 'new_v_ssm_norm_w': 'new_v', 'new_v_gdn_conv_w': 'new_v', 'new_v_gdn_a_log': 'new_v', 'new_v_gdn_dt_bias': 'new_v', 'new_v_gdn_norm_w': 'new_v', 'new_v_w_proj_ssm': 'new_v', 'new_v_w_proj_gdn': 'new_v', 'new_v_w_out': 'new_v', 'new_v_norm_ffn_w': 'new_v', 'new_v_w_ffn_in': 'new_v', 'new_v_w_ffn_down': 'new_v', 'new_v_final_norm_w': 'new_v'}


def _forward(args):
    return _fwd_reference(*[args[k] for k in FWD_PARAMS])


def _output_shape():
    out = _jax.eval_shape(lambda: _forward(_fwd_setup_inputs(0)))
    return out.shape, out.dtype

N_MICROBATCH = 1
ADAM_LR = 0.001
ADAM_B1 = 0.9
ADAM_B2 = 0.999
ADAM_EPS = 1e-08
ADAM_WD = 0.01
ADAM_STEP = 10
PER_EXAMPLE_BATCH_AXIS = {'x': 0, 'loss_target': 0}
SHARED_INPUTS = []
_WEIGHT_DTYPES = {'norm_mix_w': _jnp.float32, 'w_in': _jnp.float32, 'ssm_conv_w': _jnp.float32, 'ssm_conv_b': _jnp.float32, 'ssm_dt_bias': _jnp.float32, 'ssm_a_log': _jnp.float32, 'ssm_d': _jnp.float32, 'ssm_norm_w': _jnp.float32, 'gdn_conv_w': _jnp.float32, 'gdn_a_log': _jnp.float32, 'gdn_dt_bias': _jnp.float32, 'gdn_norm_w': _jnp.float32, 'w_proj_ssm': _jnp.float32, 'w_proj_gdn': _jnp.float32, 'w_out': _jnp.float32, 'norm_ffn_w': _jnp.float32, 'w_ffn_in': _jnp.float32, 'w_ffn_down': _jnp.float32, 'final_norm_w': _jnp.float32}
MOMENT_SCALE = {'norm_mix_w': 1.811837e-01, 'w_in': 5.854510e-02, 'ssm_conv_w': 8.179253e-02, 'ssm_conv_b': 1.152932e-01, 'ssm_dt_bias': 2.835493e-01, 'ssm_a_log': 2.267538e-01, 'ssm_d': 5.709785e-01, 'ssm_norm_w': 1.022634e-01, 'gdn_conv_w': 4.101596e-02, 'gdn_a_log': 1.994998e-01, 'gdn_dt_bias': 1.888970e-01, 'gdn_norm_w': 1.604546e-01, 'w_proj_ssm': 9.202777e-02, 'w_proj_gdn': 5.343789e-02, 'w_out': 1.064450e-01, 'norm_ffn_w': 1.238527e-01, 'w_ffn_in': 5.123566e-02, 'w_ffn_down': 8.339371e-02, 'final_norm_w': 3.200532e+01}


def _to_microbatches(a, axis):
    t = _jnp.moveaxis(a, axis, 0)
    t = t.reshape((N_MICROBATCH, t.shape[0] // N_MICROBATCH) + t.shape[1:])
    return _jnp.moveaxis(t, 1, axis + 1)


def setup_inputs(seed: int = 0) -> dict:
    inp = _fwd_setup_inputs(seed)
    key = _jax.random.fold_in(_jax.random.key(seed), 7919)
    shape, _ = _output_shape()
    out = dict(inp)
    out["loss_target"] = _jax.random.normal(_jax.random.fold_in(key, 0), shape, _jnp.float32)
    for i, name in enumerate(TWIN_WEIGHTS):
        w = inp[name].astype(_jnp.float32)
        if MOMENT_SCALE is None:
            s = _jnp.sqrt(_jnp.mean(_jnp.square(w)) + 1e-30)
        else:
            s = MOMENT_SCALE[name]
        km, kv = _jax.random.split(_jax.random.fold_in(key, i + 1))
        out[name] = w
        out["m_" + name] = s * _jax.random.normal(km, w.shape, _jnp.float32)
        out["v_" + name] = (s * s) * _jax.random.uniform(kv, w.shape, _jnp.float32, 0.5, 1.5)
    if N_MICROBATCH > 1:
        for name, axis in PER_EXAMPLE_BATCH_AXIS.items():
            out[name] = _to_microbatches(out[name], axis)
    return {'x': out['x'], 'norm_mix_w': out['norm_mix_w'], 'w_in': out['w_in'], 'ssm_conv_w': out['ssm_conv_w'], 'ssm_conv_b': out['ssm_conv_b'], 'ssm_dt_bias': out['ssm_dt_bias'], 'ssm_a_log': out['ssm_a_log'], 'ssm_d': out['ssm_d'], 'ssm_norm_w': out['ssm_norm_w'], 'gdn_conv_w': out['gdn_conv_w'], 'gdn_a_log': out['gdn_a_log'], 'gdn_dt_bias': out['gdn_dt_bias'], 'gdn_norm_w': out['gdn_norm_w'], 'w_proj_ssm': out['w_proj_ssm'], 'w_proj_gdn': out['w_proj_gdn'], 'w_out': out['w_out'], 'norm_ffn_w': out['norm_ffn_w'], 'w_ffn_in': out['w_ffn_in'], 'w_ffn_down': out['w_ffn_down'], 'final_norm_w': out['final_norm_w'], 'loss_target': out['loss_target'], 'm_norm_mix_w': out['m_norm_mix_w'], 'm_w_in': out['m_w_in'], 'm_ssm_conv_w': out['m_ssm_conv_w'], 'm_ssm_conv_b': out['m_ssm_conv_b'], 'm_ssm_dt_bias': out['m_ssm_dt_bias'], 'm_ssm_a_log': out['m_ssm_a_log'], 'm_ssm_d': out['m_ssm_d'], 'm_ssm_norm_w': out['m_ssm_norm_w'], 'm_gdn_conv_w': out['m_gdn_conv_w'], 'm_gdn_a_log': out['m_gdn_a_log'], 'm_gdn_dt_bias': out['m_gdn_dt_bias'], 'm_gdn_norm_w': out['m_gdn_norm_w'], 'm_w_proj_ssm': out['m_w_proj_ssm'], 'm_w_proj_gdn': out['m_w_proj_gdn'], 'm_w_out': out['m_w_out'], 'm_norm_ffn_w': out['m_norm_ffn_w'], 'm_w_ffn_in': out['m_w_ffn_in'], 'm_w_ffn_down': out['m_w_ffn_down'], 'm_final_norm_w': out['m_final_norm_w'], 'v_norm_mix_w': out['v_norm_mix_w'], 'v_w_in': out['v_w_in'], 'v_ssm_conv_w': out['v_ssm_conv_w'], 'v_ssm_conv_b': out['v_ssm_conv_b'], 'v_ssm_dt_bias': out['v_ssm_dt_bias'], 'v_ssm_a_log': out['v_ssm_a_log'], 'v_ssm_d': out['v_ssm_d'], 'v_ssm_norm_w': out['v_ssm_norm_w'], 'v_gdn_conv_w': out['v_gdn_conv_w'], 'v_gdn_a_log': out['v_gdn_a_log'], 'v_gdn_dt_bias': out['v_gdn_dt_bias'], 'v_gdn_norm_w': out['v_gdn_norm_w'], 'v_w_proj_ssm': out['v_w_proj_ssm'], 'v_w_proj_gdn': out['v_w_proj_gdn'], 'v_w_out': out['v_w_out'], 'v_norm_ffn_w': out['v_norm_ffn_w'], 'v_w_ffn_in': out['v_w_ffn_in'], 'v_w_ffn_down': out['v_w_ffn_down'], 'v_final_norm_w': out['v_final_norm_w']}


def _loss(weights, diff, rest, loss_target):
    with _jax.named_scope("forward"):
        args = {**rest, TWIN_DIFF_INPUT: diff, **{k: w.astype(_WEIGHT_DTYPES[k]) for k, w in weights.items()}}
        y = _forward(args)
    with _jax.named_scope("loss_head"):
        err = _jnp.square(y.astype(_jnp.float32) - loss_target)
        return 0.5 * _jnp.sum(_jnp.mean(err, axis=-1)) if err.ndim else 0.5 * err


def _adamw(w, g, m, v):
    m = ADAM_B1 * m + (1.0 - ADAM_B1) * g
    v = ADAM_B2 * v + (1.0 - ADAM_B2) * _jnp.square(g)
    m_hat = m / (1.0 - ADAM_B1 ** ADAM_STEP)
    v_hat = v / (1.0 - ADAM_B2 ** ADAM_STEP)
    delta = -ADAM_LR * (m_hat / (_jnp.sqrt(v_hat) + ADAM_EPS) + ADAM_WD * w)
    return delta, m, v


def reference(x, norm_mix_w, w_in, ssm_conv_w, ssm_conv_b, ssm_dt_bias, ssm_a_log, ssm_d, ssm_norm_w, gdn_conv_w, gdn_a_log, gdn_dt_bias, gdn_norm_w, w_proj_ssm, w_proj_gdn, w_out, norm_ffn_w, w_ffn_in, w_ffn_down, final_norm_w, loss_target, m_norm_mix_w, m_w_in, m_ssm_conv_w, m_ssm_conv_b, m_ssm_dt_bias, m_ssm_a_log, m_ssm_d, m_ssm_norm_w, m_gdn_conv_w, m_gdn_a_log, m_gdn_dt_bias, m_gdn_norm_w, m_w_proj_ssm, m_w_proj_gdn, m_w_out, m_norm_ffn_w, m_w_ffn_in, m_w_ffn_down, m_final_norm_w, v_norm_mix_w, v_w_in, v_ssm_conv_w, v_ssm_conv_b, v_ssm_dt_bias, v_ssm_a_log, v_ssm_d, v_ssm_norm_w, v_gdn_conv_w, v_gdn_a_log, v_gdn_dt_bias, v_gdn_norm_w, v_w_proj_ssm, v_w_proj_gdn, v_w_out, v_norm_ffn_w, v_w_ffn_in, v_w_ffn_down, v_final_norm_w):
    given = dict(x=x, norm_mix_w=norm_mix_w, w_in=w_in, ssm_conv_w=ssm_conv_w, ssm_conv_b=ssm_conv_b, ssm_dt_bias=ssm_dt_bias, ssm_a_log=ssm_a_log, ssm_d=ssm_d, ssm_norm_w=ssm_norm_w, gdn_conv_w=gdn_conv_w, gdn_a_log=gdn_a_log, gdn_dt_bias=gdn_dt_bias, gdn_norm_w=gdn_norm_w, w_proj_ssm=w_proj_ssm, w_proj_gdn=w_proj_gdn, w_out=w_out, norm_ffn_w=norm_ffn_w, w_ffn_in=w_ffn_in, w_ffn_down=w_ffn_down, final_norm_w=final_norm_w, loss_target=loss_target, m_norm_mix_w=m_norm_mix_w, m_w_in=m_w_in, m_ssm_conv_w=m_ssm_conv_w, m_ssm_conv_b=m_ssm_conv_b, m_ssm_dt_bias=m_ssm_dt_bias, m_ssm_a_log=m_ssm_a_log, m_ssm_d=m_ssm_d, m_ssm_norm_w=m_ssm_norm_w, m_gdn_conv_w=m_gdn_conv_w, m_gdn_a_log=m_gdn_a_log, m_gdn_dt_bias=m_gdn_dt_bias, m_gdn_norm_w=m_gdn_norm_w, m_w_proj_ssm=m_w_proj_ssm, m_w_proj_gdn=m_w_proj_gdn, m_w_out=m_w_out, m_norm_ffn_w=m_norm_ffn_w, m_w_ffn_in=m_w_ffn_in, m_w_ffn_down=m_w_ffn_down, m_final_norm_w=m_final_norm_w, v_norm_mix_w=v_norm_mix_w, v_w_in=v_w_in, v_ssm_conv_w=v_ssm_conv_w, v_ssm_conv_b=v_ssm_conv_b, v_ssm_dt_bias=v_ssm_dt_bias, v_ssm_a_log=v_ssm_a_log, v_ssm_d=v_ssm_d, v_ssm_norm_w=v_ssm_norm_w, v_gdn_conv_w=v_gdn_conv_w, v_gdn_a_log=v_gdn_a_log, v_gdn_dt_bias=v_gdn_dt_bias, v_gdn_norm_w=v_gdn_norm_w, v_w_proj_ssm=v_w_proj_ssm, v_w_proj_gdn=v_w_proj_gdn, v_w_out=v_w_out, v_norm_ffn_w=v_norm_ffn_w, v_w_ffn_in=v_w_ffn_in, v_w_ffn_down=v_w_ffn_down, v_final_norm_w=v_final_norm_w)
    weights = {n: given[n] for n in TWIN_WEIGHTS}
    shared = {n: given[n] for n in SHARED_INPUTS}
    per_example = {n: given[n] for n in ['x']}
    grad_fn = _jax.value_and_grad(_loss, argnums=(0, 1))

    def one_microbatch(ex, loss_target):
        ex = dict(ex)
        diff = ex.pop(TWIN_DIFF_INPUT)
        return grad_fn(weights, diff, {**shared, **ex}, loss_target)

    if N_MICROBATCH == 1:
        loss, (grad_w, grad_x) = one_microbatch(per_example, given["loss_target"])
    else:
        def body(carry, xs):
            loss_sum, grad_sum = carry
            l_k, (gw_k, gx_k) = one_microbatch(xs[0], xs[1])
            with _jax.named_scope("update"):
                return (loss_sum + l_k, _jax.tree.map(_jnp.add, grad_sum, gw_k)), gx_k

        init = (_jnp.zeros((), _jnp.float32), _jax.tree.map(_jnp.zeros_like, weights))
        (loss, grad_w), grad_x = _jax.lax.scan(body, init, (per_example, given["loss_target"]))
    with _jax.named_scope("update"):
        delta_w, new_m, new_v = {}, {}, {}
        for n in TWIN_WEIGHTS:
            delta_w[n], new_m[n], new_v[n] = _adamw(weights[n], grad_w[n], given["m_" + n], given["v_" + n])
    return (loss, grad_x, *[grad_w[n] for n in TWIN_WEIGHTS], *[delta_w[n] for n in TWIN_WEIGHTS],
            *[new_m[n] for n in TWIN_WEIGHTS], *[new_v[n] for n in TWIN_WEIGHTS])
```

```python
import functools
import math

import jax
import jax.numpy as jnp
from jax import lax
from jax.experimental import pallas as pl
from jax.experimental.pallas import tpu as pltpu

F32 = jnp.float32
BF16 = jnp.bfloat16

D_MODEL = 1024
DEPTH = 2
SSM_HEADS = 16
SSM_HEAD_DIM = 64
SSM_D_INNER = 1024
SSM_STATE = 128
SSM_CONV_DIM = 1536
GDN_HEADS = 8
GDN_HEAD = 128
GDN_QKV_DIM = 3072
CONV_K = 4
CHUNK = 64
FFN_HIDDEN = 2816
EPS = 1e-6
IN_DIM = 8736

ADAM_LR = 0.001
ADAM_B1 = 0.9
ADAM_B2 = 0.999
ADAM_EPS = 1e-08
ADAM_WD = 0.01
ADAM_STEP = 10

LANE = 128
NEG_BIG = -1e30
VMEM_LIMIT = 56 * 1024 * 1024

AL_Z, AL_XBC, AL_QKV, AL_GZ, AL_GS, AL_GG, AL_SMALL, AL_DIM = 0, 1024, 2560, 5632, 6656, 7680, 8704, 9216
SM_DT, SM_A, SM_B = 0, 16, 24

HI = lax.Precision.HIGHEST
NN = (((1,), (0,)), ((), ()))
NT = (((1,), (1,)), ((), ()))
TN = (((0,), (0,)), ((), ()))


def _cparams(sem):
    return pltpu.CompilerParams(dimension_semantics=sem, vmem_limit_bytes=VMEM_LIMIT)


def _pick(n, prefs):
    for p in prefs:
        if n % p == 0:
            return p
    return n


def _dot(a, b, dims=NN):
    return lax.dot_general(a.astype(BF16), b.astype(BF16), dims, preferred_element_type=F32)


def _dot_hi(a, b, dims=NN):
    return lax.dot_general(a, b, dims, precision=HI, preferred_element_type=F32)


def _sigmoid(x):
    return jax.nn.sigmoid(x)


def _silu(x):
    return x * _sigmoid(x)


def _softplus(x):
    return jnp.maximum(x, 0.0) + jnp.log1p(jnp.exp(-jnp.abs(x)))


def matmul(name, a, b, mode, out_dtype=F32):
    if mode == "nn":
        (m, k), (k2, n) = a.shape, b.shape
    elif mode == "nt":
        (m, k), (n, k2) = a.shape, b.shape
    else:
        (k, m), (k2, n) = a.shape, b.shape
    assert k == k2, (a.shape, b.shape, mode)
    tm = _pick(m, (512, 256, 128))
    tn = _pick(n, (512, 256, 128))
    tk = _pick(k, (1024, 512, 256, 128))
    nk = k // tk
    dims = {"nn": NN, "nt": NT, "tn": TN}[mode]

    def body(a_ref, b_ref, o_ref, acc_ref):
        kk = pl.program_id(2)

        @pl.when(kk == 0)
        def _():
            acc_ref[...] = jnp.zeros_like(acc_ref)

        acc_ref[...] += _dot(a_ref[...], b_ref[...], dims)

        @pl.when(kk == nk - 1)
        def _():
            o_ref[...] = acc_ref[...].astype(o_ref.dtype)

    if mode == "tn":
        a_spec = pl.BlockSpec((tk, tm), lambda i, j, kk: (kk, i))
    else:
        a_spec = pl.BlockSpec((tm, tk), lambda i, j, kk: (i, kk))
    if mode == "nt":
        b_spec = pl.BlockSpec((tn, tk), lambda i, j, kk: (j, kk))
    else:
        b_spec = pl.BlockSpec((tk, tn), lambda i, j, kk: (kk, j))
    return pl.pallas_call(
        body, name=name,
        out_shape=jax.ShapeDtypeStruct((m, n), out_dtype),
        grid=(m // tm, n // tn, nk),
        in_specs=[a_spec, b_spec],
        out_specs=pl.BlockSpec((tm, tn), lambda i, j, kk: (i, j)),
        scratch_shapes=[pltpu.VMEM((tm, tn), F32)],
        compiler_params=_cparams(("parallel", "parallel", "arbitrary")),
    )(a, b)


def _row_map(c0, moves):
    return (lambda j, i: (i, c0 + j)) if moves else (lambda j, i: (i, c0))


def _par_map(c0, moves):
    return (lambda j, i: (0, c0 + j)) if moves else (lambda j, i: (0, c0))


def _in_spec(op, tile):
    _, kind, w, c0, moves = op
    if kind == "row":
        return pl.BlockSpec((tile, w), _row_map(c0, moves))
    return pl.BlockSpec((1, w), _par_map(c0, moves))


def rowwise_fwd(name, fn, t, tile, ncol, ins, outs):
    n_in = len(ins)

    def body(*refs):
        vals = [r[...].astype(F32) for r in refs[:n_in]]
        res = fn(*vals)
        if not isinstance(res, (tuple, list)):
            res = (res,)
        for r, v in zip(refs[n_in:], res):
            r[...] = v.astype(r.dtype)

    res = pl.pallas_call(
        body, name=name,
        out_shape=[jax.ShapeDtypeStruct((t, w * ncol), dt) for w, dt in outs],
        grid=(ncol, t // tile),
        in_specs=[_in_spec(op, tile) for op in ins],
        out_specs=[pl.BlockSpec((tile, w), _row_map(0, True)) for w, _ in outs],
        compiler_params=_cparams(("arbitrary", "arbitrary")),
    )(*[op[0] for op in ins])
    return res


def rowwise_bwd(name, fn, t, tile, ncol, ins, need, cts, addends=None, row_dtypes=None):
    n_in, n_ct = len(ins), len(cts)
    addends = addends or {}
    row_dtypes = row_dtypes or {}
    didx = [i for i in range(n_in) if need[i]]
    add_ops = [addends[i] for i in didx if i in addends]
    n_add = len(add_ops)

    def body(*refs):
        in_refs = refs[:n_in]
        ct_refs = refs[n_in:n_in + n_ct]
        add_refs = refs[n_in + n_ct:n_in + n_ct + n_add]
        out_refs = refs[n_in + n_ct + n_add:]
        vals = [r[...].astype(F32) for r in in_refs]

        def g(*dv):
            full = list(vals)
            for i, v in zip(didx, dv):
                full[i] = v
            res = fn(*full)
            return tuple(res) if isinstance(res, (tuple, list)) else (res,)

        _, vjp = jax.vjp(g, *[vals[i] for i in didx])
        grads = vjp(tuple(c[...].astype(F32) for c in ct_refs))
        j, i = pl.program_id(0), pl.program_id(1)
        a = 0
        for o_ref, gv, idx in zip(out_refs, grads, didx):
            _, kind, _, _, moves = ins[idx]
            if kind == "row":
                if idx in addends:
                    gv = gv + add_refs[a][...].astype(F32)
                    a += 1
                o_ref[...] = gv.astype(o_ref.dtype)
            else:
                first = (i == 0) if moves else jnp.logical_and(i == 0, j == 0)

                @pl.when(first)
                def _(o_ref=o_ref, gv=gv):
                    o_ref[...] = gv

                @pl.when(jnp.logical_not(first))
                def _(o_ref=o_ref, gv=gv):
                    o_ref[...] += gv

    out_shape, out_specs = [], []
    for idx in didx:
        _, kind, w, _, moves = ins[idx]
        cols = w * (ncol if moves else 1)
        if kind == "row":
            out_shape.append(jax.ShapeDtypeStruct((t, cols), row_dtypes.get(idx, F32)))
            out_specs.append(pl.BlockSpec((tile, w), _row_map(0, moves)))
        else:
            out_shape.append(jax.ShapeDtypeStruct((1, cols), F32))
            out_specs.append(pl.BlockSpec((1, w), _par_map(0, moves)))
    ops = list(ins) + list(cts) + add_ops
    res = pl.pallas_call(
        body, name=name,
        out_shape=out_shape,
        grid=(ncol, t // tile),
        in_specs=[_in_spec(op, tile) for op in ops],
        out_specs=out_specs,
        compiler_params=_cparams(("arbitrary", "arbitrary")),
    )(*[op[0] for op in ops])
    return res


def f_rmsnorm(x, w):
    return x * lax.rsqrt(jnp.mean(x * x, axis=-1, keepdims=True) + EPS) * w


def f_ssd_post(y, z, w):
    y = y * _silu(z)
    return y * lax.rsqrt(jnp.mean(y * y, axis=-1, keepdims=True) + EPS) * w


def f_gdn_post(o, z, w):
    o = o * lax.rsqrt(jnp.mean(o * o, axis=-1, keepdims=True) + EPS) * w
    return o * _silu(z)


def f_merge(gs, p1, gg, p2):
    return _sigmoid(gs) * p1 + _sigmoid(gg) * p2


def f_swiglu(g, u):
    return _silu(g) * u


def final_loss(name, x, tgt, w, tile=256):
    t, d = x.shape

    def body(x_ref, t_ref, w_ref, loss_ref, dx_ref, dw_ref):
        i = pl.program_id(0)
        xv, tv, wv = x_ref[...], t_ref[...], w_ref[...]

        def g(xx, ww):
            err = f_rmsnorm(xx, ww) - tv
            return 0.5 * jnp.sum(jnp.mean(err * err, axis=-1, keepdims=True), axis=0, keepdims=True)

        val, vjp = jax.vjp(g, xv, wv)
        dx, dw = vjp(jnp.ones((1, 1), F32))
        dx_ref[...] = dx
        lv = jnp.broadcast_to(val, (1, LANE))

        @pl.when(i == 0)
        def _():
            loss_ref[...] = lv
            dw_ref[...] = dw

        @pl.when(i != 0)
        def _():
            loss_ref[...] += lv
            dw_ref[...] += dw

    return pl.pallas_call(
        body, name=name,
        out_shape=[jax.ShapeDtypeStruct((1, LANE), F32), jax.ShapeDtypeStruct((t, d), F32),
                   jax.ShapeDtypeStruct((1, d), F32)],
        grid=(t // tile,),
        in_specs=[pl.BlockSpec((tile, d), lambda i: (i, 0)), pl.BlockSpec((tile, d), lambda i: (i, 0)),
                  pl.BlockSpec((1, d), lambda i: (0, 0))],
        out_specs=[pl.BlockSpec((1, LANE), lambda i: (0, 0)), pl.BlockSpec((tile, d), lambda i: (i, 0)),
                   pl.BlockSpec((1, d), lambda i: (0, 0))],
        compiler_params=_cparams(("arbitrary",)),
    )(x, tgt, w)


CONV_W = 512
HALO = 8


def conv_fwd(name, src, c0, width, w, b, tile=512):
    t = src.shape[0]
    ncol, nrow = width // CONV_W, t // tile
    cb0 = c0 // CONV_W
    hb = tile // HALO

    def body(prev_ref, cur_ref, w_ref, b_ref, o_ref, ext_ref):
        i = pl.program_id(1)
        ext_ref[0:HALO, :] = jnp.where(i == 0, 0.0, prev_ref[...])
        ext_ref[HALO:HALO + tile, :] = cur_ref[...]
        acc = jnp.broadcast_to(b_ref[...], (tile, CONV_W))
        for k in range(CONV_K):
            acc = acc + w_ref[k:k + 1, :] * ext_ref[pl.ds(HALO - (CONV_K - 1) + k, tile), :]
        o_ref[...] = _silu(acc)

    return pl.pallas_call(
        body, name=name,
        out_shape=jax.ShapeDtypeStruct((t, width), F32),
        grid=(ncol, nrow),
        in_specs=[pl.BlockSpec((HALO, CONV_W), lambda j, i: (jnp.maximum(i * hb - 1, 0), cb0 + j)),
                  pl.BlockSpec((tile, CONV_W), lambda j, i: (i, cb0 + j)),
                  pl.BlockSpec((CONV_K, CONV_W), lambda j, i: (0, j)),
                  pl.BlockSpec((1, CONV_W), lambda j, i: (0, j))],
        out_specs=pl.BlockSpec((tile, CONV_W), lambda j, i: (i, j)),
        scratch_shapes=[pltpu.VMEM((tile + HALO, CONV_W), F32)],
        compiler_params=_cparams(("arbitrary", "arbitrary")),
    )(src, src, w, b)


def conv_bwd_pre(name, src, c0, width, w, b, dy, tile=512):
    t = src.shape[0]
    ncol, nrow = width // CONV_W, t // tile
    cb0 = c0 // CONV_W
    hb = tile // HALO

    def body(prev_ref, cur_ref, w_ref, b_ref, dy_ref, dpre_ref, dw_ref, db_ref, ext_ref):
        i = pl.program_id(1)
        ext_ref[0:HALO, :] = jnp.where(i == 0, 0.0, prev_ref[...])
        ext_ref[HALO:HALO + tile, :] = cur_ref[...]
        pre = jnp.broadcast_to(b_ref[...], (tile, CONV_W))
        for k in range(CONV_K):
            pre = pre + w_ref[k:k + 1, :] * ext_ref[pl.ds(HALO - (CONV_K - 1) + k, tile), :]
        s = _sigmoid(pre)
        dpre = dy_ref[...] * (s * (1.0 + pre * (1.0 - s)))
        dpre_ref[...] = dpre

        @pl.when(i == 0)
        def _():
            dw_ref[...] = jnp.zeros_like(dw_ref)
            db_ref[...] = jnp.zeros_like(db_ref)

        for k in range(CONV_K):
            dw_ref[k:k + 1, :] += jnp.sum(dpre * ext_ref[pl.ds(HALO - (CONV_K - 1) + k, tile), :],
                                          axis=0, keepdims=True)
        db_ref[...] += jnp.sum(dpre, axis=0, keepdims=True)

    return pl.pallas_call(
        body, name=name,
        out_shape=[jax.ShapeDtypeStruct((t, width), F32), jax.ShapeDtypeStruct((CONV_K, width), F32),
                   jax.ShapeDtypeStruct((1, width), F32)],
        grid=(ncol, nrow),
        in_specs=[pl.BlockSpec((HALO, CONV_W), lambda j, i: (jnp.maximum(i * hb - 1, 0), cb0 + j)),
                  pl.BlockSpec((tile, CONV_W), lambda j, i: (i, cb0 + j)),
                  pl.BlockSpec((CONV_K, CONV_W), lambda j, i: (0, j)),
                  pl.BlockSpec((1, CONV_W), lambda j, i: (0, j)),
                  pl.BlockSpec((tile, CONV_W), lambda j, i: (i, j))],
        out_specs=[pl.BlockSpec((tile, CONV_W), lambda j, i: (i, j)),
                   pl.BlockSpec((CONV_K, CONV_W), lambda j, i: (0, j)),
                   pl.BlockSpec((1, CONV_W), lambda j, i: (0, j))],
        scratch_shapes=[pltpu.VMEM((tile + HALO, CONV_W), F32)],
        compiler_params=_cparams(("arbitrary", "arbitrary")),
    )(src, src, w, b, dy)


def conv_bwd_in(name, dpre, w, tile=512):
    t, width = dpre.shape
    ncol, nrow = width // CONV_W, t // tile
    hb = tile // HALO
    last_hb = t // HALO - 1

    def body(cur_ref, nxt_ref, w_ref, o_ref, ext_ref):
        i = pl.program_id(1)
        ext_ref[0:tile, :] = cur_ref[...]
        ext_ref[tile:tile + HALO, :] = jnp.where(i == nrow - 1, 0.0, nxt_ref[...])
        acc = jnp.zeros((tile, CONV_W), F32)
        for k in range(CONV_K):
            acc = acc + w_ref[k:k + 1, :] * ext_ref[pl.ds(CONV_K - 1 - k, tile), :]
        o_ref[...] = acc

    return pl.pallas_call(
        body, name=name,
        out_shape=jax.ShapeDtypeStruct((t, width), F32),
        grid=(ncol, nrow),
        in_specs=[pl.BlockSpec((tile, CONV_W), lambda j, i: (i, j)),
                  pl.BlockSpec((HALO, CONV_W), lambda j, i: (jnp.minimum((i + 1) * hb, last_hb), j)),
                  pl.BlockSpec((CONV_K, CONV_W), lambda j, i: (0, j))],
        out_specs=pl.BlockSpec((tile, CONV_W), lambda j, i: (i, j)),
        scratch_shapes=[pltpu.VMEM((tile + HALO, CONV_W), F32)],
        compiler_params=_cparams(("arbitrary", "arbitrary")),
    )(dpre, dpre, w)


def _iota2(q):
    return (lax.broadcasted_iota(jnp.int32, (q, q), 0), lax.broadcasted_iota(jnp.int32, (q, q), 1))


def _lane_pick(blk, idx):
    lane = lax.broadcasted_iota(jnp.int32, (1, LANE), 1)
    return jnp.sum(jnp.where(lane == idx, blk, 0.0), axis=1, keepdims=True)


def _cum(a_col, r, c):
    a_row = jnp.sum(jnp.where(r == c, a_col, 0.0), axis=0, keepdims=True)
    cum_col = jnp.sum(jnp.where(c <= r, a_row, 0.0), axis=1, keepdims=True)
    cum_row = jnp.sum(jnp.where(r <= c, a_col, 0.0), axis=0, keepdims=True)
    return cum_col, cum_row


def ssd_chunk(xs, bm, cm, small, dtb, alog, dsk, state, p):
    q = xs.shape[0]
    r, c = _iota2(q)
    lane = lax.broadcasted_iota(jnp.int32, (1, LANE), 1)
    m0 = lane < SSM_HEAD_DIM

    def head(h):
        dt = _softplus(_lane_pick(small, SM_DT + h) + _lane_pick(dtb, h))
        a = dt * (-jnp.exp(_lane_pick(alog, h)))
        cum_col, cum_row = _cum(a, r, c)
        lmat = jnp.exp(jnp.where(r >= c, cum_col - cum_row, NEG_BIG))
        tot = jnp.sum(a, axis=0, keepdims=True)
        return dt, _lane_pick(dsk, h), cum_col, lmat, tot

    dt0, d0, cum0, l0, tot0 = head(2 * p)
    dt1, d1, cum1, l1, tot1 = head(2 * p + 1)
    xdt = xs * jnp.where(m0, dt0, dt1)
    cb = _dot(cm, bm, NT)
    y_diag = _dot(cb * l0, jnp.where(m0, xdt, 0.0)) + _dot(cb * l1, jnp.where(m0, 0.0, xdt))
    y_off = _dot(cm, state, NT) * jnp.where(m0, jnp.exp(cum0), jnp.exp(cum1))
    dec = jnp.where(m0, jnp.exp(tot0 - cum0), jnp.exp(tot1 - cum1))
    rowm = lax.broadcasted_iota(jnp.int32, (LANE, 1), 0) < SSM_HEAD_DIM
    new_state = state * jnp.where(rowm, jnp.exp(tot0), jnp.exp(tot1)) + _dot(xdt * dec, bm, TN)
    y = y_diag + y_off + jnp.where(m0, d0, d1) * xs
    return y, new_state


def tri_inverse(a):
    q = a.shape[0]
    r, c = _iota2(q)
    eye = (r == c).astype(F32)
    diag = (r // 16) == (c // 16)
    bd = jnp.where(diag, a, 0.0)
    off = jnp.where(diag, 0.0, a)
    b2 = _dot_hi(bd, bd)
    b4 = _dot_hi(b2, b2)
    b8 = _dot_hi(b4, b4)
    dinv = _dot_hi(_dot_hi(_dot_hi(eye - bd, eye + b2), eye + b4), eye + b8)
    n = _dot_hi(dinv, off)
    n2 = _dot_hi(n, n)
    return _dot_hi(eye - n, _dot_hi(eye + n2, dinv))


@jax.custom_vjp
def _solve_with(xinv, a, rhs):
    del a
    return _dot_hi(xinv, rhs)


def _solve_with_fwd(xinv, a, rhs):
    t = _dot_hi(xinv, rhs)
    return t, (xinv, t)


def _solve_with_bwd(res, dt):
    xinv, t = res
    d_rhs = _dot_hi(xinv, dt, TN)
    d_a = -_dot(d_rhs, t, NT)
    return jnp.zeros_like(xinv), d_a, d_rhs


_solve_with.defvjp(_solve_with_fwd, _solve_with_bwd)


def gdn_chunk(qh, kh, vh, small, alog, dtb, state, h, xinv=None):
    q = qh.shape[0]
    r, c = _iota2(q)
    qn = qh * lax.rsqrt(jnp.sum(qh * qh, axis=-1, keepdims=True) + EPS) * (GDN_HEAD ** -0.5)
    kn = kh * lax.rsqrt(jnp.sum(kh * kh, axis=-1, keepdims=True) + EPS)
    beta = _sigmoid(_lane_pick(small, SM_B + h))
    g = -jnp.exp(_lane_pick(alog, h)) * _softplus(_lane_pick(small, SM_A + h) + _lane_pick(dtb, h))
    gc_col, gc_row = _cum(g, r, c)
    decay = jnp.exp(jnp.where(r >= c, gc_col - gc_row, NEG_BIG))
    kk = _dot(kn, kn, NT)
    amat = jnp.where(r > c, kk * decay * beta, 0.0)
    eg = jnp.exp(gc_col)
    rhs = jnp.concatenate([vh * beta, kn * (beta * eg)], axis=1)
    if xinv is None:
        xinv = tri_inverse(amat)
        t = _dot_hi(xinv, rhs)
    else:
        t = _solve_with(xinv, amat, rhs)
    u, w = t[:, :GDN_HEAD], t[:, GDN_HEAD:]
    v_new = u - _dot(w, state)
    qk = _dot(qn, kn, NT) * decay
    o = _dot(qn * eg, state) + _dot(qk, v_new)
    tot = jnp.sum(g, axis=0, keepdims=True)
    new_state = state * jnp.exp(tot) + _dot(kn * jnp.exp(tot - gc_col), v_new, TN)
    return o, new_state, xinv


def _acc(ref, val, first):
    @pl.when(first)
    def _():
        ref[...] = val

    @pl.when(jnp.logical_not(first))
    def _():
        ref[...] += val


def ssd_scan_fwd(name, xbc, proj, dtb, alog, dsk):
    t = xbc.shape[0]
    nc, npair = t // CHUNK, SSM_HEADS // 2
    small_blk = AL_SMALL // LANE

    def body(xs_ref, b_ref, c_ref, sm_ref, dtb_ref, alog_ref, dsk_ref, y_ref, sin_ref, st_ref):
        ci, p = pl.program_id(0), pl.program_id(1)

        @pl.when(ci == 0)
        def _():
            st_ref[p] = jnp.zeros((LANE, LANE), F32)

        s_in = st_ref[p]
        sin_ref[0, 0] = s_in
        y, s_new = ssd_chunk(xs_ref[...], b_ref[...], c_ref[...], sm_ref[...], dtb_ref[...],
                             alog_ref[...], dsk_ref[...], s_in, p)
        y_ref[...] = y
        st_ref[p] = s_new

    par = pl.BlockSpec((1, LANE), lambda ci, p: (0, 0))
    return pl.pallas_call(
        body, name=name,
        out_shape=[jax.ShapeDtypeStruct((t, SSM_D_INNER), F32),
                   jax.ShapeDtypeStruct((nc, npair, LANE, LANE), F32)],
        grid=(nc, npair),
        in_specs=[pl.BlockSpec((CHUNK, LANE), lambda ci, p: (ci, p)),
                  pl.BlockSpec((CHUNK, LANE), lambda ci, p: (ci, 8 + p // 4)),
                  pl.BlockSpec((CHUNK, LANE), lambda ci, p: (ci, 10 + p // 4)),
                  pl.BlockSpec((CHUNK, LANE), lambda ci, p: (ci, small_blk)),
                  par, par, par],
        out_specs=[pl.BlockSpec((CHUNK, LANE), lambda ci, p: (ci, p)),
                   pl.BlockSpec((1, 1, LANE, LANE), lambda ci, p: (ci, p, 0, 0))],
        scratch_shapes=[pltpu.VMEM((npair, LANE, LANE), F32)],
        compiler_params=_cparams(("arbitrary", "arbitrary")),
    )(xbc, xbc, xbc, proj, dtb, alog, dsk)


def ssd_scan_bwd(name, xbc, proj, dtb, alog, dsk, s_in, dy):
    t = xbc.shape[0]
    nc, npair = t // CHUNK, SSM_HEADS // 2
    small_blk = AL_SMALL // LANE

    def body(xs_ref, b_ref, c_ref, sm_ref, dtb_ref, alog_ref, dsk_ref, sin_ref, dy_ref,
             dxs_ref, db_ref, dc_ref, dsm_ref, ddtb_ref, dalog_ref, ddsk_ref, dst_ref):
        ci, p = pl.program_id(0), pl.program_id(1)

        @pl.when(ci == 0)
        def _():
            dst_ref[p] = jnp.zeros((LANE, LANE), F32)

        def fn(xs, bm, cm, sm, dtb_, alog_, dsk_, st):
            return ssd_chunk(xs, bm, cm, sm, dtb_, alog_, dsk_, st, p)

        _, vjp = jax.vjp(fn, xs_ref[...], b_ref[...], c_ref[...], sm_ref[...], dtb_ref[...],
                         alog_ref[...], dsk_ref[...], sin_ref[0, 0])
        dxs, dbm, dcm, dsm, ddtb, dalog, ddsk, dst = vjp((dy_ref[...], dst_ref[p]))
        dxs_ref[...] = dxs
        dst_ref[p] = dst
        _acc(db_ref, dbm, p % 4 == 0)
        _acc(dc_ref, dcm, p % 4 == 0)
        _acc(dsm_ref, dsm, p == 0)
        first = jnp.logical_and(ci == 0, p == 0)
        _acc(ddtb_ref, ddtb, first)
        _acc(dalog_ref, dalog, first)
        _acc(ddsk_ref, ddsk, first)

    par = pl.BlockSpec((1, LANE), lambda ci, p: (0, 0))
    rev = lambda ci: nc - 1 - ci
    return pl.pallas_call(
        body, name=name,
        out_shape=[jax.ShapeDtypeStruct((t, SSM_D_INNER), F32),
                   jax.ShapeDtypeStruct((t, 2 * SSM_STATE), F32),
                   jax.ShapeDtypeStruct((t, 2 * SSM_STATE), F32),
                   jax.ShapeDtypeStruct((t, LANE), F32),
                   jax.ShapeDtypeStruct((1, LANE), F32), jax.ShapeDtypeStruct((1, LANE), F32),
                   jax.ShapeDtypeStruct((1, LANE), F32)],
        grid=(nc, npair),
        in_specs=[pl.BlockSpec((CHUNK, LANE), lambda ci, p: (rev(ci), p)),
                  pl.BlockSpec((CHUNK, LANE), lambda ci, p: (rev(ci), 8 + p // 4)),
                  pl.BlockSpec((CHUNK, LANE), lambda ci, p: (rev(ci), 10 + p // 4)),
                  pl.BlockSpec((CHUNK, LANE), lambda ci, p: (rev(ci), small_blk)),
                  par, par, par,
                  pl.BlockSpec((1, 1, LANE, LANE), lambda ci, p: (rev(ci), p, 0, 0)),
                  pl.BlockSpec((CHUNK, LANE), lambda ci, p: (rev(ci), p))],
        out_specs=[pl.BlockSpec((CHUNK, LANE), lambda ci, p: (rev(ci), p)),
                   pl.BlockSpec((CHUNK, LANE), lambda ci, p: (rev(ci), p // 4)),
                   pl.BlockSpec((CHUNK, LANE), lambda ci, p: (rev(ci), p // 4)),
                   pl.BlockSpec((CHUNK, LANE), lambda ci, p: (rev(ci), 0)),
                   par, par, par],
        scratch_shapes=[pltpu.VMEM((npair, LANE, LANE), F32)],
        compiler_params=_cparams(("arbitrary", "arbitrary")),
    )(xbc, xbc, xbc, proj, dtb, alog, dsk, s_in, dy)


def gdn_scan_fwd(name, qkv, proj, alog, dtb):
    t = qkv.shape[0]
    nc, nh = t // CHUNK, GDN_HEADS
    small_blk = AL_SMALL // LANE

    def body(q_ref, k_ref, v_ref, sm_ref, alog_ref, dtb_ref, o_ref, sin_ref, x_ref, st_ref):
        ci, h = pl.program_id(0), pl.program_id(1)

        @pl.when(ci == 0)
        def _():
            st_ref[h] = jnp.zeros((LANE, LANE), F32)

        s_in = st_ref[h]
        sin_ref[0, 0] = s_in
        o, s_new, xinv = gdn_chunk(q_ref[...], k_ref[...], v_ref[...], sm_ref[...], alog_ref[...],
                                   dtb_ref[...], s_in, h)
        o_ref[...] = o
        x_ref[0, 0] = xinv
        st_ref[h] = s_new

    par = pl.BlockSpec((1, LANE), lambda ci, h: (0, 0))
    return pl.pallas_call(
        body, name=name,
        out_shape=[jax.ShapeDtypeStruct((t, GDN_HEADS * GDN_HEAD), F32),
                   jax.ShapeDtypeStruct((nc, nh, LANE, LANE), F32),
                   jax.ShapeDtypeStruct((nc, nh, CHUNK, CHUNK), F32)],
        grid=(nc, nh),
        in_specs=[pl.BlockSpec((CHUNK, LANE), lambda ci, h: (ci, h)),
                  pl.BlockSpec((CHUNK, LANE), lambda ci, h: (ci, 8 + h)),
                  pl.BlockSpec((CHUNK, LANE), lambda ci, h: (ci, 16 + h)),
                  pl.BlockSpec((CHUNK, LANE), lambda ci, h: (ci, small_blk)),
                  par, par],
        out_specs=[pl.BlockSpec((CHUNK, LANE), lambda ci, h: (ci, h)),
                   pl.BlockSpec((1, 1, LANE, LANE), lambda ci, h: (ci, h, 0, 0)),
                   pl.BlockSpec((1, 1, CHUNK, CHUNK), lambda ci, h: (ci, h, 0, 0))],
        scratch_shapes=[pltpu.VMEM((nh, LANE, LANE), F32)],
        compiler_params=_cparams(("arbitrary", "arbitrary")),
    )(qkv, qkv, qkv, proj, alog, dtb)


def gdn_scan_bwd(name, qkv, proj, alog, dtb, s_in, xinv, do, dsm_in):
    t = qkv.shape[0]
    nc, nh = t // CHUNK, GDN_HEADS
    small_blk = AL_SMALL // LANE

    def body(q_ref, k_ref, v_ref, sm_ref, alog_ref, dtb_ref, sin_ref, x_ref, do_ref, dsmi_ref,
             dq_ref, dk_ref, dv_ref, dsm_ref, dalog_ref, ddtb_ref, dst_ref):
        ci, h = pl.program_id(0), pl.program_id(1)

        @pl.when(ci == 0)
        def _():
            dst_ref[h] = jnp.zeros((LANE, LANE), F32)

        xi = x_ref[0, 0]

        def fn(qh, kh, vh, sm, alog_, dtb_, st):
            o, s_new, _ = gdn_chunk(qh, kh, vh, sm, alog_, dtb_, st, h, xinv=xi)
            return o, s_new

        _, vjp = jax.vjp(fn, q_ref[...], k_ref[...], v_ref[...], sm_ref[...], alog_ref[...],
                         dtb_ref[...], sin_ref[0, 0])
        dq, dk, dv, dsm, dalog, ddtb, dst = vjp((do_ref[...], dst_ref[h]))
        dq_ref[...] = dq
        dk_ref[...] = dk
        dv_ref[...] = dv
        dst_ref[h] = dst

        @pl.when(h == 0)
        def _():
            dsm_ref[...] = dsmi_ref[...] + dsm

        @pl.when(h != 0)
        def _():
            dsm_ref[...] += dsm

        first = jnp.logical_and(ci == 0, h == 0)
        _acc(dalog_ref, dalog, first)
        _acc(ddtb_ref, ddtb, first)

    par = pl.BlockSpec((1, LANE), lambda ci, h: (0, 0))
    rev = lambda ci: nc - 1 - ci
    hd = jax.ShapeDtypeStruct((t, GDN_HEADS * GDN_HEAD), F32)
    return pl.pallas_call(
        body, name=name,
        out_shape=[hd, hd, hd, jax.ShapeDtypeStruct((t, LANE), F32),
                   jax.ShapeDtypeStruct((1, LANE), F32), jax.ShapeDtypeStruct((1, LANE), F32)],
        grid=(nc, nh),
        in_specs=[pl.BlockSpec((CHUNK, LANE), lambda ci, h: (rev(ci), h)),
                  pl.BlockSpec((CHUNK, LANE), lambda ci, h: (rev(ci), 8 + h)),
                  pl.BlockSpec((CHUNK, LANE), lambda ci, h: (rev(ci), 16 + h)),
                  pl.BlockSpec((CHUNK, LANE), lambda ci, h: (rev(ci), small_blk)),
                  par, par,
                  pl.BlockSpec((1, 1, LANE, LANE), lambda ci, h: (rev(ci), h, 0, 0)),
                  pl.BlockSpec((1, 1, CHUNK, CHUNK), lambda ci, h: (rev(ci), h, 0, 0)),
                  pl.BlockSpec((CHUNK, LANE), lambda ci, h: (rev(ci), h)),
                  pl.BlockSpec((CHUNK, LANE), lambda ci, h: (rev(ci), 0))],
        out_specs=[pl.BlockSpec((CHUNK, LANE), lambda ci, h: (rev(ci), h)),
                   pl.BlockSpec((CHUNK, LANE), lambda ci, h: (rev(ci), h)),
                   pl.BlockSpec((CHUNK, LANE), lambda ci, h: (rev(ci), h)),
                   pl.BlockSpec((CHUNK, LANE), lambda ci, h: (rev(ci), 0)),
                   par, par],
        scratch_shapes=[pltpu.VMEM((nh, LANE, LANE), F32)],
        compiler_params=_cparams(("arbitrary", "arbitrary")),
    )(qkv, qkv, qkv, proj, alog, dtb, s_in, xinv, do, dsm_in)


def _row(arr, w, c0=0, moves=False):
    return (arr, "row", w, c0, moves)


def _par(arr, w, c0=0, moves=False):
    return (arr, "par", w, c0, moves)


def matmul_add(name, a, b, res):
    (m, k), (_, n) = a.shape, b.shape
    tm = _pick(m, (512, 256, 128))
    tn = _pick(n, (512, 256, 128))
    tk = _pick(k, (1024, 512, 256, 128))
    nk = k // tk

    def body(a_ref, b_ref, r_ref, o_ref, acc_ref):
        kk = pl.program_id(2)

        @pl.when(kk == 0)
        def _():
            acc_ref[...] = r_ref[...]

        acc_ref[...] += _dot(a_ref[...], b_ref[...])

        @pl.when(kk == nk - 1)
        def _():
            o_ref[...] = acc_ref[...]

    return pl.pallas_call(
        body, name=name,
        out_shape=jax.ShapeDtypeStruct((m, n), F32),
        grid=(m // tm, n // tn, nk),
        in_specs=[pl.BlockSpec((tm, tk), lambda i, j, kk: (i, kk)),
                  pl.BlockSpec((tk, tn), lambda i, j, kk: (kk, j)),
                  pl.BlockSpec((tm, tn), lambda i, j, kk: (i, j))],
        out_specs=pl.BlockSpec((tm, tn), lambda i, j, kk: (i, j)),
        scratch_shapes=[pltpu.VMEM((tm, tn), F32)],
        compiler_params=_cparams(("parallel", "parallel", "arbitrary")),
    )(a, b, res)


def layer_fwd(l, x, w):
    t = x.shape[0]
    rt = min(256, t)
    s = {"x": x}
    s["h"] = rowwise_fwd(f"norm_mix_l{l}", f_rmsnorm, t, rt, 1,
                         [_row(x, D_MODEL), _par(w["norm_mix_w"], D_MODEL)], [(D_MODEL, BF16)])[0]
    s["proj"] = matmul(f"in_proj_l{l}", s["h"], w["w_in"], "nn")
    s["xbc"] = conv_fwd(f"ssm_conv_l{l}", s["proj"], AL_XBC, SSM_CONV_DIM, w["ssm_conv_w"], w["ssm_conv_b"],
                        tile=min(512, t))
    s["qkv"] = conv_fwd(f"gdn_conv_l{l}", s["proj"], AL_QKV, GDN_QKV_DIM, w["gdn_conv_w"], w["gdn_conv_b"],
                        tile=min(512, t))
    s["y_scan"], s["ssd_sin"] = ssd_scan_fwd(f"ssd_scan_l{l}", s["xbc"], s["proj"], w["ssm_dt_bias"],
                                             w["ssm_a_log"], w["ssm_d"])
    s["o_scan"], s["gdn_sin"], s["gdn_x"] = gdn_scan_fwd(f"gdn_scan_l{l}", s["qkv"], s["proj"],
                                                         w["gdn_a_log"], w["gdn_dt_bias"])
    s["y_ssm"] = rowwise_fwd(f"ssd_post_l{l}", f_ssd_post, t, rt, 2,
                             [_row(s["y_scan"], 512, 0, True), _row(s["proj"], 512, AL_Z // 512, True),
                              _par(w["ssm_norm_w"], 512, 0, True)], [(512, BF16)])[0]
    s["y_gdn"] = rowwise_fwd(f"gdn_post_l{l}", f_gdn_post, t, rt, GDN_HEADS,
                             [_row(s["o_scan"], LANE, 0, True), _row(s["proj"], LANE, AL_GZ // LANE, True),
                              _par(w["gdn_norm_w"], LANE)], [(LANE, BF16)])[0]
    s["p1"] = matmul(f"proj_ssm_l{l}", s["y_ssm"], w["w_proj_ssm"], "nn")
    s["p2"] = matmul(f"proj_gdn_l{l}", s["y_gdn"], w["w_proj_gdn"], "nn")
    s["merged"] = rowwise_fwd(f"merge_l{l}", f_merge, t, rt, 2,
                              [_row(s["proj"], 512, AL_GS // 512, True), _row(s["p1"], 512, 0, True),
                               _row(s["proj"], 512, AL_GG // 512, True), _row(s["p2"], 512, 0, True)],
                              [(512, BF16)])[0]
    s["x1"] = matmul_add(f"out_proj_l{l}", s["merged"], w["w_out"], x)
    s["h2"] = rowwise_fwd(f"norm_ffn_l{l}", f_rmsnorm, t, rt, 1,
                          [_row(s["x1"], D_MODEL), _par(w["norm_ffn_w"], D_MODEL)], [(D_MODEL, BF16)])[0]
    s["gu"] = matmul(f"ffn_in_l{l}", s["h2"], w["w_ffn_in"], "nn")
    s["act"] = rowwise_fwd(f"swiglu_l{l}", f_swiglu, t, rt, FFN_HIDDEN // 256,
                           [_row(s["gu"], 256, 0, True), _row(s["gu"], 256, FFN_HIDDEN // 256, True)],
                           [(256, BF16)])[0]
    x2 = matmul_add(f"ffn_down_l{l}", s["act"], w["w_ffn_down"], s["x1"])
    return x2, s


def layer_bwd(l, dx2, w, s):
    t = dx2.shape[0]
    rt = min(256, t)
    ct = min(512, t)
    g = {}
    dact = matmul(f"ffn_down_dx_l{l}", dx2, w["w_ffn_down"], "nt")
    g["w_ffn_down"] = matmul(f"ffn_down_dw_l{l}", s["act"], dx2, "tn")
    nf = FFN_HIDDEN // 256
    dgate, dup = rowwise_bwd(f"swiglu_bwd_l{l}", f_swiglu, t, rt, nf,
                             [_row(s["gu"], 256, 0, True), _row(s["gu"], 256, nf, True)], [True, True],
                             [_row(dact, 256, 0, True)])
    dgu = jnp.concatenate([dgate, dup], axis=1)
    dh2 = matmul(f"ffn_in_dx_l{l}", dgu, w["w_ffn_in"], "nt")
    g["w_ffn_in"] = matmul(f"ffn_in_dw_l{l}", s["h2"], dgu, "tn")
    dx1, g["norm_ffn_w"] = rowwise_bwd(f"norm_ffn_bwd_l{l}", f_rmsnorm, t, rt, 1,
                                       [_row(s["x1"], D_MODEL), _par(w["norm_ffn_w"], D_MODEL)], [True, True],
                                       [_row(dh2, D_MODEL)], addends={0: _row(dx2, D_MODEL)})
    dmerged = matmul(f"out_proj_dx_l{l}", dx1, w["w_out"], "nt")
    g["w_out"] = matmul(f"out_proj_dw_l{l}", s["merged"], dx1, "tn")
    dgs, dp1, dgg, dp2 = rowwise_bwd(
        f"merge_bwd_l{l}", f_merge, t, rt, 2,
        [_row(s["proj"], 512, AL_GS // 512, True), _row(s["p1"], 512, 0, True),
         _row(s["proj"], 512, AL_GG // 512, True), _row(s["p2"], 512, 0, True)], [True] * 4,
        [_row(dmerged, 512, 0, True)])
    dy_ssm = matmul(f"proj_ssm_dx_l{l}", dp1, w["w_proj_ssm"], "nt")
    g["w_proj_ssm"] = matmul(f"proj_ssm_dw_l{l}", s["y_ssm"], dp1, "tn")
    dy_gdn = matmul(f"proj_gdn_dx_l{l}", dp2, w["w_proj_gdn"], "nt")
    g["w_proj_gdn"] = matmul(f"proj_gdn_dw_l{l}", s["y_gdn"], dp2, "tn")
    dy_scan, dz, g["ssm_norm_w"] = rowwise_bwd(
        f"ssd_post_bwd_l{l}", f_ssd_post, t, rt, 2,
        [_row(s["y_scan"], 512, 0, True), _row(s["proj"], 512, AL_Z // 512, True),
         _par(w["ssm_norm_w"], 512, 0, True)], [True] * 3, [_row(dy_ssm, 512, 0, True)])
    dxs, dbm, dcm, dsm, g["ssm_dt_bias"], g["ssm_a_log"], g["ssm_d"] = ssd_scan_bwd(
        f"ssd_scan_bwd_l{l}", s["xbc"], s["proj"], w["ssm_dt_bias"], w["ssm_a_log"], w["ssm_d"],
        s["ssd_sin"], dy_scan)
    dxbc_act = jnp.concatenate([dxs, dbm, dcm], axis=1)
    dpre, g["ssm_conv_w"], g["ssm_conv_b"] = conv_bwd_pre(
        f"ssm_conv_bwd_l{l}", s["proj"], AL_XBC, SSM_CONV_DIM, w["ssm_conv_w"], w["ssm_conv_b"], dxbc_act, tile=ct)
    dxbc = conv_bwd_in(f"ssm_conv_din_l{l}", dpre, w["ssm_conv_w"], tile=ct)
    do_scan, dgz, g["gdn_norm_w"] = rowwise_bwd(
        f"gdn_post_bwd_l{l}", f_gdn_post, t, rt, GDN_HEADS,
        [_row(s["o_scan"], LANE, 0, True), _row(s["proj"], LANE, AL_GZ // LANE, True),
         _par(w["gdn_norm_w"], LANE)], [True] * 3, [_row(dy_gdn, LANE, 0, True)])
    dq, dk, dv, dsm, g["gdn_a_log"], g["gdn_dt_bias"] = gdn_scan_bwd(
        f"gdn_scan_bwd_l{l}", s["qkv"], s["proj"], w["gdn_a_log"], w["gdn_dt_bias"], s["gdn_sin"],
        s["gdn_x"], do_scan, dsm)
    dqkv_act = jnp.concatenate([dq, dk, dv], axis=1)
    dpre, g["gdn_conv_w"], _ = conv_bwd_pre(
        f"gdn_conv_bwd_l{l}", s["proj"], AL_QKV, GDN_QKV_DIM, w["gdn_conv_w"], w["gdn_conv_b"], dqkv_act, tile=ct)
    dqkv = conv_bwd_in(f"gdn_conv_din_l{l}", dpre, w["gdn_conv_w"], tile=ct)
    dproj = jnp.concatenate([dz, dxbc, dqkv, dgz, dgs, dgg, dsm,
                             jnp.zeros((t, AL_DIM - AL_SMALL - LANE), F32)], axis=1)
    dh = matmul(f"in_proj_dx_l{l}", dproj, w["w_in"], "nt")
    g["w_in"] = matmul(f"in_proj_dw_l{l}", s["h"], dproj, "tn")
    dx0, g["norm_mix_w"] = rowwise_bwd(f"norm_mix_bwd_l{l}", f_rmsnorm, t, rt, 1,
                                       [_row(s["x"], D_MODEL), _par(w["norm_mix_w"], D_MODEL)], [True, True],
                                       [_row(dh, D_MODEL)], addends={0: _row(dx1, D_MODEL)})
    return dx0, g


def _align_w_in(w):
    pad = jnp.zeros((w.shape[0], AL_DIM - AL_SMALL - 32), w.dtype)
    return jnp.concatenate([w[:, 0:2560], w[:, 2576:6672], w[:, 6688:8736],
                            w[:, 2560:2576], w[:, 6672:6688], pad], axis=1)


def _unalign_w_in(g):
    return jnp.concatenate([g[:, 0:2560], g[:, AL_SMALL:AL_SMALL + 16], g[:, 2560:6656],
                            g[:, AL_SMALL + 16:AL_SMALL + 32], g[:, 6656:8704]], axis=1)


def _pad_lane(v):
    return jnp.zeros((1, LANE), F32).at[0, :v.shape[0]].set(v)


def local_step(x, target, full):
    ws = []
    for l in range(DEPTH):
        ws.append({
            "norm_mix_w": full["norm_mix_w"][l][None], "w_in": _align_w_in(full["w_in"][l]),
            "ssm_conv_w": full["ssm_conv_w"][l], "ssm_conv_b": full["ssm_conv_b"][l][None],
            "ssm_dt_bias": _pad_lane(full["ssm_dt_bias"][l]), "ssm_a_log": _pad_lane(full["ssm_a_log"][l]),
            "ssm_d": _pad_lane(full["ssm_d"][l]), "ssm_norm_w": full["ssm_norm_w"][l][None],
            "gdn_conv_w": full["gdn_conv_w"][l], "gdn_conv_b": jnp.zeros((1, GDN_QKV_DIM), F32),
            "gdn_a_log": _pad_lane(full["gdn_a_log"][l]), "gdn_dt_bias": _pad_lane(full["gdn_dt_bias"][l]),
            "gdn_norm_w": full["gdn_norm_w"][l][None],
            "w_proj_ssm": full["w_proj_ssm"][l], "w_proj_gdn": full["w_proj_gdn"][l], "w_out": full["w_out"][l],
            "norm_ffn_w": full["norm_ffn_w"][l][None], "w_ffn_in": full["w_ffn_in"][l],
            "w_ffn_down": full["w_ffn_down"][l],
        })
    saved = []
    h = x
    for l in range(DEPTH):
        h, s = layer_fwd(l, h, ws[l])
        saved.append(s)
    loss, dx, g_final = final_loss("final_loss", h, target, full["final_norm_w"][None], tile=min(256, x.shape[0]))
    per_layer = [None] * DEPTH
    for l in reversed(range(DEPTH)):
        dx, per_layer[l] = layer_bwd(l, dx, ws[l], saved[l])
    grads = {"final_norm_w": g_final[0]}
    for name in per_layer[0]:
        rows = []
        for l in range(DEPTH):
            gl = per_layer[l][name]
            if name == "w_in":
                gl = _unalign_w_in(gl)
            elif name in ("ssm_dt_bias", "ssm_a_log", "ssm_d"):
                gl = gl[0, :SSM_HEADS]
            elif name in ("gdn_a_log", "gdn_dt_bias"):
                gl = gl[0, :GDN_HEADS]
            elif name in ("norm_mix_w", "ssm_conv_b", "ssm_norm_w", "gdn_norm_w", "norm_ffn_w"):
                gl = gl[0]
            rows.append(gl)
        grads[name] = jnp.stack(rows)
    return loss, dx, grads


MESH = pl.DeviceIdType.MESH
HBM = pl.BlockSpec(memory_space=pltpu.HBM)
N_CHIPS = 4
N_DEV = 8
PACK_W = 1024


def _pos():
    return lax.axis_index("x"), lax.axis_index("y"), lax.axis_index("c")


def _other_chips(x, y):
    return [(1 - x, y), (x, 1 - y), (1 - x, 1 - y)]


def _rcopy(src, dst, send_sem, recv_sem, dev):
    return pltpu.make_async_remote_copy(src_ref=src, dst_ref=dst, send_sem=send_sem, recv_sem=recv_sem,
                                        device_id=dev, device_id_type=MESH)


def chip_allgather(name, buf):
    r, cd = buf.shape

    def body(src, out, send_sems, recv_sems, lsem):
        x, y, c = _pos()
        me = 2 * x + y
        local = pltpu.make_async_copy(src, out.at[me], lsem)
        local.start()
        sends = []
        for k, (px, py) in enumerate(_other_chips(x, y)):
            cp = _rcopy(src, out.at[me], send_sems.at[k], recv_sems.at[k], (px, py, c))
            cp.start()
            sends.append(cp)
        for k, (px, py) in enumerate(_other_chips(x, y)):
            _rcopy(src, out.at[2 * px + py], send_sems.at[k], recv_sems.at[k], (px, py, c)).wait_recv()
        for cp in sends:
            cp.wait_send()
        local.wait()

    return pl.pallas_call(
        body, name=name, out_shape=jax.ShapeDtypeStruct((N_CHIPS, r, cd), buf.dtype),
        in_specs=[HBM], out_specs=HBM,
        scratch_shapes=[pltpu.SemaphoreType.DMA((3,)), pltpu.SemaphoreType.DMA((3,)), pltpu.SemaphoreType.DMA(())],
    )(buf)


def pair_swap(name, p):
    _, nj, rh, cd = p.shape

    def body(src, out, send_sem, recv_sem):
        x, y, c = _pos()
        cp = _rcopy(src.at[1 - c], out, send_sem, recv_sem, (x, y, 1 - c))
        cp.start()
        cp.wait()

    return pl.pallas_call(
        body, name=name, out_shape=jax.ShapeDtypeStruct((nj, rh, cd), p.dtype),
        in_specs=[HBM], out_specs=HBM,
        scratch_shapes=[pltpu.SemaphoreType.DMA(()), pltpu.SemaphoreType.DMA(())],
    )(p)


def chip_scatter(name, s):
    nj, rh, cd = s.shape

    def body(src, out, send_sems, recv_sems, lsem):
        x, y, c = _pos()
        me = 2 * x + y
        local = pltpu.make_async_copy(src.at[me], out.at[me], lsem)
        local.start()
        sends = []
        for k, (px, py) in enumerate(_other_chips(x, y)):
            cp = _rcopy(src.at[2 * px + py], out.at[me], send_sems.at[k], recv_sems.at[k], (px, py, c))
            cp.start()
            sends.append(cp)
        for k, (px, py) in enumerate(_other_chips(x, y)):
            pj = 2 * px + py
            _rcopy(src.at[pj], out.at[pj], send_sems.at[k], recv_sems.at[k], (px, py, c)).wait_recv()
        for cp in sends:
            cp.wait_send()
        local.wait()

    return pl.pallas_call(
        body, name=name, out_shape=jax.ShapeDtypeStruct((nj, rh, cd), s.dtype),
        in_specs=[HBM], out_specs=HBM,
        scratch_shapes=[pltpu.SemaphoreType.DMA((3,)), pltpu.SemaphoreType.DMA((3,)), pltpu.SemaphoreType.DMA(())],
    )(s)


def pair_share(name, g):
    rh, cd = g.shape

    def body(src, out, send_sem, recv_sem, lsem):
        x, y, c = _pos()
        local = pltpu.make_async_copy(src, out.at[c], lsem)
        local.start()
        cp = _rcopy(src, out.at[c], send_sem, recv_sem, (x, y, 1 - c))
        cp.start()
        _rcopy(src, out.at[1 - c], send_sem, recv_sem, (x, y, 1 - c)).wait_recv()
        cp.wait_send()
        local.wait()

    return pl.pallas_call(
        body, name=name, out_shape=jax.ShapeDtypeStruct((2, rh, cd), g.dtype),
        in_specs=[HBM], out_specs=HBM,
        scratch_shapes=[pltpu.SemaphoreType.DMA(()), pltpu.SemaphoreType.DMA(()), pltpu.SemaphoreType.DMA(())],
    )(g)


def all_allgather(name, buf):
    r, cd = buf.shape

    def body(src, out, send_sems, recv_sems, lsem):
        x, y, c = _pos()
        me = 4 * x + 2 * y + c
        local = pltpu.make_async_copy(src, out.at[me], lsem)
        local.start()

        def peer(mask):
            px = 1 - x if mask & 4 else x
            py = 1 - y if mask & 2 else y
            pc = 1 - c if mask & 1 else c
            return px, py, pc

        sends = []
        for mask in range(1, N_DEV):
            cp = _rcopy(src, out.at[me], send_sems.at[mask - 1], recv_sems.at[mask - 1], peer(mask))
            cp.start()
            sends.append(cp)
        for mask in range(1, N_DEV):
            px, py, pc = peer(mask)
            _rcopy(src, out.at[4 * px + 2 * py + pc], send_sems.at[mask - 1], recv_sems.at[mask - 1],
                   (px, py, pc)).wait_recv()
        for cp in sends:
            cp.wait_send()
        local.wait()

    return pl.pallas_call(
        body, name=name, out_shape=jax.ShapeDtypeStruct((N_DEV, r, cd), buf.dtype),
        in_specs=[HBM], out_specs=HBM,
        scratch_shapes=[pltpu.SemaphoreType.DMA((N_DEV - 1,)), pltpu.SemaphoreType.DMA((N_DEV - 1,)),
                        pltpu.SemaphoreType.DMA(())],
    )(buf)


def pair_add(name, p, recv, c):
    _, nj, rh, cd = p.shape
    tr = _pick(rh, (512, 256, 128, 64, 32, 16, 8))

    def body(c_ref, p_ref, r_ref, o_ref):
        del c_ref
        o_ref[...] = p_ref[0] + r_ref[...]

    return pl.pallas_call(
        body, name=name, out_shape=jax.ShapeDtypeStruct((nj, rh, cd), F32),
        grid_spec=pltpu.PrefetchScalarGridSpec(
            num_scalar_prefetch=1, grid=(nj, rh // tr),
            in_specs=[pl.BlockSpec((1, 1, tr, cd), lambda j, i, c_ref: (c_ref[0], j, i, 0)),
                      pl.BlockSpec((1, tr, cd), lambda j, i, c_ref: (j, i, 0))],
            out_specs=pl.BlockSpec((1, tr, cd), lambda j, i, c_ref: (j, i, 0))),
        compiler_params=_cparams(("arbitrary", "arbitrary")),
    )(jnp.reshape(c, (1,)).astype(jnp.int32), p, recv)


def slab_sum(name, a):
    n, r, cd = a.shape
    tr = _pick(r, (256, 128, 64, 32, 16, 8))

    def body(a_ref, o_ref):
        acc = a_ref[0]
        for j in range(1, n):
            acc = acc + a_ref[j]
        o_ref[...] = acc

    return pl.pallas_call(
        body, name=name, out_shape=jax.ShapeDtypeStruct((r, cd), F32),
        grid=(r // tr,),
        in_specs=[pl.BlockSpec((n, tr, cd), lambda i: (0, i, 0))],
        out_specs=pl.BlockSpec((tr, cd), lambda i: (i, 0)),
        compiler_params=_cparams(("arbitrary",)),
    )(a)


ADAM_C1 = 1.0 - ADAM_B1 ** ADAM_STEP
ADAM_C2 = 1.0 - ADAM_B2 ** ADAM_STEP


def adamw(name, w, g, m, v):
    r, cd = w.shape
    tr = r
    for cand in (512, 256, 128, 64, 32, 16, 8):
        if r % cand == 0 and cand * cd * 4 <= (1 << 20):
            tr = cand
            break

    def body(w_ref, g_ref, m_ref, v_ref, d_ref, nm_ref, nv_ref):
        gv = g_ref[...]
        nm = ADAM_B1 * m_ref[...] + (1.0 - ADAM_B1) * gv
        nv = ADAM_B2 * v_ref[...] + (1.0 - ADAM_B2) * (gv * gv)
        m_hat = nm / ADAM_C1
        v_hat = nv / ADAM_C2
        d_ref[...] = -ADAM_LR * (m_hat / (jnp.sqrt(v_hat) + ADAM_EPS) + ADAM_WD * w_ref[...])
        nm_ref[...] = nm
        nv_ref[...] = nv

    spec = pl.BlockSpec((tr, cd), lambda i: (i, 0))
    sd = jax.ShapeDtypeStruct((r, cd), F32)
    return pl.pallas_call(
        body, name=name, out_shape=[sd, sd, sd], grid=(r // tr,),
        in_specs=[spec] * 4, out_specs=[spec] * 3,
        compiler_params=_cparams(("arbitrary",)),
    )(w, g, m, v)


WEIGHTS = ("norm_mix_w", "w_in", "ssm_conv_w", "ssm_conv_b", "ssm_dt_bias", "ssm_a_log", "ssm_d", "ssm_norm_w",
           "gdn_conv_w", "gdn_a_log", "gdn_dt_bias", "gdn_norm_w", "w_proj_ssm", "w_proj_gdn", "w_out",
           "norm_ffn_w", "w_ffn_in", "w_ffn_down", "final_norm_w")
BIG = (("w_in", 2), ("w_proj_ssm", 1), ("w_proj_gdn", 1), ("w_out", 1), ("w_ffn_in", 2), ("w_ffn_down", 1))
CONVW = (("ssm_conv_w", 2), ("gdn_conv_w", 2))
SHARDED = BIG + CONVW
SMALL = tuple(n for n in WEIGHTS if n not in dict(SHARDED))
GRAD_ROWS = 10240


def _pack(arrs, rows, dtype):
    flat = jnp.concatenate([a.reshape(-1).astype(dtype) for a in arrs])
    return jnp.pad(flat, (0, rows * PACK_W - flat.shape[0])).reshape(rows, PACK_W)


def _unpack(buf, shapes):
    flat = buf.reshape(-1)
    out, o = [], 0
    for shp in shapes:
        n = math.prod(shp)
        out.append(flat[o:o + n].reshape(shp))
        o += n
    return out


def _pack_small(d, extra=None):
    arrs = [d[n] for n in SMALL] + ([extra] if extra is not None else [])
    n = sum(math.prod(a.shape) for a in arrs)
    rows = -(-n // (8 * LANE)) * 8
    flat = jnp.concatenate([a.reshape(-1) for a in arrs])
    return jnp.pad(flat, (0, rows * LANE - n)).reshape(rows, LANE)


def _shard_slice(a, axis, j):
    n = a.shape[axis] // N_CHIPS
    return lax.slice_in_dim(a, j * n, (j + 1) * n, axis=axis)


def kernel(x, norm_mix_w, w_in, ssm_conv_w, ssm_conv_b, ssm_dt_bias, ssm_a_log, ssm_d, ssm_norm_w, gdn_conv_w, gdn_a_log, gdn_dt_bias, gdn_norm_w, w_proj_ssm, w_proj_gdn, w_out, norm_ffn_w, w_ffn_in, w_ffn_down, final_norm_w, loss_target, m_norm_mix_w, m_w_in, m_ssm_conv_w, m_ssm_conv_b, m_ssm_dt_bias, m_ssm_a_log, m_ssm_d, m_ssm_norm_w, m_gdn_conv_w, m_gdn_a_log, m_gdn_dt_bias, m_gdn_norm_w, m_w_proj_ssm, m_w_proj_gdn, m_w_out, m_norm_ffn_w, m_w_ffn_in, m_w_ffn_down, m_final_norm_w, v_norm_mix_w, v_w_in, v_ssm_conv_w, v_ssm_conv_b, v_ssm_dt_bias, v_ssm_a_log, v_ssm_d, v_ssm_norm_w, v_gdn_conv_w, v_gdn_a_log, v_gdn_dt_bias, v_gdn_norm_w, v_w_proj_ssm, v_w_proj_gdn, v_w_out, v_norm_ffn_w, v_w_ffn_in, v_w_ffn_down, v_final_norm_w):
    wl = (norm_mix_w, w_in, ssm_conv_w, ssm_conv_b, ssm_dt_bias, ssm_a_log, ssm_d, ssm_norm_w, gdn_conv_w,
          gdn_a_log, gdn_dt_bias, gdn_norm_w, w_proj_ssm, w_proj_gdn, w_out, norm_ffn_w, w_ffn_in, w_ffn_down,
          final_norm_w)
    ml = (m_norm_mix_w, m_w_in, m_ssm_conv_w, m_ssm_conv_b, m_ssm_dt_bias, m_ssm_a_log, m_ssm_d, m_ssm_norm_w,
          m_gdn_conv_w, m_gdn_a_log, m_gdn_dt_bias, m_gdn_norm_w, m_w_proj_ssm, m_w_proj_gdn, m_w_out,
          m_norm_ffn_w, m_w_ffn_in, m_w_ffn_down, m_final_norm_w)
    vl = (v_norm_mix_w, v_w_in, v_ssm_conv_w, v_ssm_conv_b, v_ssm_dt_bias, v_ssm_a_log, v_ssm_d, v_ssm_norm_w,
          v_gdn_conv_w, v_gdn_a_log, v_gdn_dt_bias, v_gdn_norm_w, v_w_proj_ssm, v_w_proj_gdn, v_w_out,
          v_norm_ffn_w, v_w_ffn_in, v_w_ffn_down, v_final_norm_w)
    w = dict(zip(WEIGHTS, wl))
    m = dict(zip(WEIGHTS, ml))
    v = dict(zip(WEIGHTS, vl))
    c = lax.axis_index("c")

    big_shapes = [w[n].shape for n, _ in BIG]
    conv_shapes = [w[n].shape for n, _ in CONVW]
    big_rows = sum(math.prod(s) for s in big_shapes) // PACK_W
    gathered_big = chip_allgather("gather_w", _pack([w[n] for n, _ in BIG], big_rows, BF16))
    gathered_conv = chip_allgather("gather_conv_w", _pack([w[n] for n, _ in CONVW], 16, F32))
    full = {n: w[n] for n in SMALL}
    big_parts = [_unpack(gathered_big[j], big_shapes) for j in range(N_CHIPS)]
    conv_parts = [_unpack(gathered_conv[j], conv_shapes) for j in range(N_CHIPS)]
    for i, (n, axis) in enumerate(BIG):
        full[n] = jnp.concatenate([big_parts[j][i] for j in range(N_CHIPS)], axis=axis)
    for i, (n, axis) in enumerate(CONVW):
        full[n] = jnp.concatenate([conv_parts[j][i] for j in range(N_CHIPS)], axis=axis)

    loss_part, grad_x, grads = local_step(x[0], loss_target[0], full)

    rh = GRAD_ROWS // 2
    packed = jnp.stack([_pack([_shard_slice(grads[n], axis, j) for n, axis in SHARDED], GRAD_ROWS, F32)
                        for j in range(N_CHIPS)])
    packed = packed.reshape(N_CHIPS, 2, rh, PACK_W).transpose(1, 0, 2, 3)
    from_pair = pair_swap("grad_pair_swap", packed)
    chip_part = pair_add("grad_pair_add", packed, from_pair, c)
    from_chips = chip_scatter("grad_chip_scatter", chip_part)
    reduced_half = slab_sum("grad_chip_sum", from_chips)
    reduced = pair_share("grad_pair_share", reduced_half).reshape(GRAD_ROWS, PACK_W)
    g_sharded = dict(zip([n for n, _ in SHARDED], _unpack(reduced, [w[n].shape for n, _ in SHARDED])))

    small_all = all_allgather("gather_small", _pack_small(grads, extra=loss_part[0, :1]))
    small_sum = slab_sum("small_sum", small_all)
    small_vals = _unpack(small_sum, [w[n].shape for n in SMALL] + [(1,)])
    g_small = dict(zip(SMALL, small_vals[:-1]))
    loss = small_vals[-1].reshape(())

    out_g, out_d, out_m, out_v = {}, {}, {}, {}
    for n, _ in SHARDED:
        shp = w[n].shape
        two = lambda a: a.reshape(-1, shp[-1])
        d_, m_, v_ = adamw(f"adamw_{n}", two(w[n]), two(g_sharded[n]), two(m[n]), two(v[n]))
        out_g[n], out_d[n], out_m[n], out_v[n] = g_sharded[n], d_.reshape(shp), m_.reshape(shp), v_.reshape(shp)
    d_, m_, v_ = adamw("adamw_small", _pack_small(w), _pack_small(g_small), _pack_small(m), _pack_small(v))
    small_shapes = [w[n].shape for n in SMALL]
    for n, dd, mm, vv in zip(SMALL, _unpack(d_, small_shapes), _unpack(m_, small_shapes), _unpack(v_, small_shapes)):
        out_g[n], out_d[n], out_m[n], out_v[n] = g_small[n], dd, mm, vv

    return (loss, grad_x[None], *[out_g[n] for n in WEIGHTS], *[out_d[n] for n in WEIGHTS],
            *[out_m[n] for n in WEIGHTS], *[out_v[n] for n in WEIGHTS])
```

```python
import functools
import math

import jax
import jax.numpy as jnp
from jax import lax
from jax.experimental import pallas as pl
from jax.experimental.pallas import tpu as pltpu

F32 = jnp.float32
BF16 = jnp.bfloat16

D_MODEL = 1024
DEPTH = 2
SSM_HEADS = 16
SSM_HEAD_DIM = 64
SSM_D_INNER = 1024
SSM_STATE = 128
SSM_CONV_DIM = 1536
GDN_HEADS = 8
GDN_HEAD = 128
GDN_QKV_DIM = 3072
CONV_K = 4
CHUNK = 64
FFN_HIDDEN = 2816
EPS = 1e-6
IN_DIM = 8736

ADAM_LR = 0.001
ADAM_B1 = 0.9
ADAM_B2 = 0.999
ADAM_EPS = 1e-08
ADAM_WD = 0.01
ADAM_STEP = 10

LANE = 128
NEG_BIG = -1e30
VMEM_LIMIT = 56 * 1024 * 1024

AL_Z, AL_XBC, AL_QKV, AL_GZ, AL_GS, AL_GG, AL_SMALL, AL_DIM = 0, 1024, 2560, 5632, 6656, 7680, 8704, 9216
SM_DT, SM_A, SM_B = 0, 16, 24

HI = lax.Precision.HIGHEST
NN = (((1,), (0,)), ((), ()))
NT = (((1,), (1,)), ((), ()))
TN = (((0,), (0,)), ((), ()))


def _cparams(sem):
    return pltpu.CompilerParams(dimension_semantics=sem, vmem_limit_bytes=VMEM_LIMIT)


def _pick(n, prefs):
    for p in prefs:
        if n % p == 0:
            return p
    return n


MAX_K_BLOCK_BYTES = 3 << 20


def _pick_k(k, other, a, b):
    if k * other * max(a.dtype.itemsize, b.dtype.itemsize) <= MAX_K_BLOCK_BYTES:
        return k
    return _pick(k, (1024, 512, 256, 128))


def _dot(a, b, dims=NN):
    return lax.dot_general(a.astype(BF16), b.astype(BF16), dims, preferred_element_type=F32)


def _dot_hi(a, b, dims=NN):
    return lax.dot_general(a, b, dims, precision=HI, preferred_element_type=F32)


def _sigmoid(x):
    return jax.nn.sigmoid(x)


def _silu(x):
    return x * _sigmoid(x)


def _softplus(x):
    return jnp.maximum(x, 0.0) + jnp.log1p(jnp.exp(-jnp.abs(x)))


def matmul(name, a, b, mode, out_dtype=F32):
    if mode == "nn":
        (m, k), (k2, n) = a.shape, b.shape
    elif mode == "nt":
        (m, k), (n, k2) = a.shape, b.shape
    else:
        (k, m), (k2, n) = a.shape, b.shape
    assert k == k2, (a.shape, b.shape, mode)
    tm = _pick(m, (512, 256, 128))
    tn = _pick(n, (512, 256, 128))
    tk = _pick_k(k, max(tm, tn), a, b)
    nk = k // tk
    dims = {"nn": NN, "nt": NT, "tn": TN}[mode]

    def body_acc(a_ref, b_ref, o_ref, acc_ref):
        kk = pl.program_id(2)

        @pl.when(kk == 0)
        def _():
            acc_ref[...] = jnp.zeros_like(acc_ref)

        acc_ref[...] += _dot(a_ref[...], b_ref[...], dims)

        @pl.when(kk == nk - 1)
        def _():
            o_ref[...] = acc_ref[...].astype(o_ref.dtype)

    def body_one(a_ref, b_ref, o_ref):
        o_ref[...] = _dot(a_ref[...], b_ref[...], dims).astype(o_ref.dtype)

    body = body_one if nk == 1 else body_acc
    if mode == "tn":
        a_spec = pl.BlockSpec((tk, tm), lambda i, j, kk: (kk, i))
    else:
        a_spec = pl.BlockSpec((tm, tk), lambda i, j, kk: (i, kk))
    if mode == "nt":
        b_spec = pl.BlockSpec((tn, tk), lambda i, j, kk: (j, kk))
    else:
        b_spec = pl.BlockSpec((tk, tn), lambda i, j, kk: (kk, j))
    return pl.pallas_call(
        body, name=name,
        out_shape=jax.ShapeDtypeStruct((m, n), out_dtype),
        grid=(m // tm, n // tn, nk),
        in_specs=[a_spec, b_spec],
        out_specs=pl.BlockSpec((tm, tn), lambda i, j, kk: (i, j)),
        scratch_shapes=[] if nk == 1 else [pltpu.VMEM((tm, tn), F32)],
        compiler_params=_cparams(("parallel", "parallel", "arbitrary")),
    )(a, b)


def _row_map(c0, moves):
    return (lambda j, i: (i, c0 + j)) if moves else (lambda j, i: (i, c0))


def _par_map(c0, moves):
    return (lambda j, i: (0, c0 + j)) if moves else (lambda j, i: (0, c0))


def _in_spec(op, tile):
    _, kind, w, c0, moves = op
    if kind == "row":
        return pl.BlockSpec((tile, w), _row_map(c0, moves))
    return pl.BlockSpec((1, w), _par_map(c0, moves))


def rowwise_fwd(name, fn, t, tile, ncol, ins, outs):
    n_in = len(ins)

    def body(*refs):
        vals = [r[...].astype(F32) for r in refs[:n_in]]
        res = fn(*vals)
        if not isinstance(res, (tuple, list)):
            res = (res,)
        for r, v in zip(refs[n_in:], res):
            r[...] = v.astype(r.dtype)

    res = pl.pallas_call(
        body, name=name,
        out_shape=[jax.ShapeDtypeStruct((t, w * ncol), dt) for w, dt in outs],
        grid=(ncol, t // tile),
        in_specs=[_in_spec(op, tile) for op in ins],
        out_specs=[pl.BlockSpec((tile, w), _row_map(0, True)) for w, _ in outs],
        compiler_params=_cparams(("arbitrary", "arbitrary")),
    )(*[op[0] for op in ins])
    return res


def rowwise_bwd(name, fn, t, tile, ncol, ins, need, cts, addends=None, row_dtypes=None):
    n_in, n_ct = len(ins), len(cts)
    addends = addends or {}
    row_dtypes = row_dtypes or {}
    didx = [i for i in range(n_in) if need[i]]
    add_ops = [addends[i] for i in didx if i in addends]
    n_add = len(add_ops)

    def body(*refs):
        in_refs = refs[:n_in]
        ct_refs = refs[n_in:n_in + n_ct]
        add_refs = refs[n_in + n_ct:n_in + n_ct + n_add]
        out_refs = refs[n_in + n_ct + n_add:]
        vals = [r[...].astype(F32) for r in in_refs]

        def g(*dv):
            full = list(vals)
            for i, v in zip(didx, dv):
                full[i] = v
            res = fn(*full)
            return tuple(res) if isinstance(res, (tuple, list)) else (res,)

        _, vjp = jax.vjp(g, *[vals[i] for i in didx])
        grads = vjp(tuple(c[...].astype(F32) for c in ct_refs))
        j, i = pl.program_id(0), pl.program_id(1)
        a = 0
        for o_ref, gv, idx in zip(out_refs, grads, didx):
            _, kind, _, _, moves = ins[idx]
            if kind == "row":
                if idx in addends:
                    gv = gv + add_refs[a][...].astype(F32)
                    a += 1
                o_ref[...] = gv.astype(o_ref.dtype)
            else:
                first = (i == 0) if moves else jnp.logical_and(i == 0, j == 0)

                @pl.when(first)
                def _(o_ref=o_ref, gv=gv):
                    o_ref[...] = gv

                @pl.when(jnp.logical_not(first))
                def _(o_ref=o_ref, gv=gv):
                    o_ref[...] += gv

    out_shape, out_specs = [], []
    for idx in didx:
        _, kind, w, _, moves = ins[idx]
        cols = w * (ncol if moves else 1)
        if kind == "row":
            out_shape.append(jax.ShapeDtypeStruct((t, cols), row_dtypes.get(idx, F32)))
            out_specs.append(pl.BlockSpec((tile, w), _row_map(0, moves)))
        else:
            out_shape.append(jax.ShapeDtypeStruct((1, cols), F32))
            out_specs.append(pl.BlockSpec((1, w), _par_map(0, moves)))
    ops = list(ins) + list(cts) + add_ops
    res = pl.pallas_call(
        body, name=name,
        out_shape=out_shape,
        grid=(ncol, t // tile),
        in_specs=[_in_spec(op, tile) for op in ops],
        out_specs=out_specs,
        compiler_params=_cparams(("arbitrary", "arbitrary")),
    )(*[op[0] for op in ops])
    return res


def f_rmsnorm(x, w):
    return x * lax.rsqrt(jnp.mean(x * x, axis=-1, keepdims=True) + EPS) * w


def f_ssd_post(y, z, w):
    y = y * _silu(z)
    return y * lax.rsqrt(jnp.mean(y * y, axis=-1, keepdims=True) + EPS) * w


def f_gdn_post(o, z, w):
    o = o * lax.rsqrt(jnp.mean(o * o, axis=-1, keepdims=True) + EPS) * w
    return o * _silu(z)


def f_merge(gs, p1, gg, p2):
    return _sigmoid(gs) * p1 + _sigmoid(gg) * p2


def f_swiglu(g, u):
    return _silu(g) * u


def final_loss(name, x, tgt, w, tile=256):
    t, d = x.shape

    def body(x_ref, t_ref, w_ref, loss_ref, dx_ref, dw_ref):
        i = pl.program_id(0)
        xv, tv, wv = x_ref[...], t_ref[...], w_ref[...]

        def g(xx, ww):
            err = f_rmsnorm(xx, ww) - tv
            return 0.5 * jnp.sum(jnp.mean(err * err, axis=-1, keepdims=True), axis=0, keepdims=True)

        val, vjp = jax.vjp(g, xv, wv)
        dx, dw = vjp(jnp.ones((1, 1), F32))
        dx_ref[...] = dx
        lv = jnp.broadcast_to(val, (1, LANE))

        @pl.when(i == 0)
        def _():
            loss_ref[...] = lv
            dw_ref[...] = dw

        @pl.when(i != 0)
        def _():
            loss_ref[...] += lv
            dw_ref[...] += dw

    return pl.pallas_call(
        body, name=name,
        out_shape=[jax.ShapeDtypeStruct((1, LANE), F32), jax.ShapeDtypeStruct((t, d), F32),
                   jax.ShapeDtypeStruct((1, d), F32)],
        grid=(t // tile,),
        in_specs=[pl.BlockSpec((tile, d), lambda i: (i, 0)), pl.BlockSpec((tile, d), lambda i: (i, 0)),
                  pl.BlockSpec((1, d), lambda i: (0, 0))],
        out_specs=[pl.BlockSpec((1, LANE), lambda i: (0, 0)), pl.BlockSpec((tile, d), lambda i: (i, 0)),
                   pl.BlockSpec((1, d), lambda i: (0, 0))],
        compiler_params=_cparams(("arbitrary",)),
    )(x, tgt, w)


CONV_W = 512
HALO = 8


def conv_fwd(name, src, c0, width, w, b, tile=512):
    t = src.shape[0]
    ncol, nrow = width // CONV_W, t // tile
    cb0 = c0 // CONV_W
    hb = tile // HALO

    def body(prev_ref, cur_ref, w_ref, b_ref, o_ref, ext_ref):
        i = pl.program_id(1)
        ext_ref[0:HALO, :] = jnp.where(i == 0, 0.0, prev_ref[...])
        ext_ref[HALO:HALO + tile, :] = cur_ref[...]
        acc = jnp.broadcast_to(b_ref[...], (tile, CONV_W))
        for k in range(CONV_K):
            acc = acc + w_ref[k:k + 1, :] * ext_ref[pl.ds(HALO - (CONV_K - 1) + k, tile), :]
        o_ref[...] = _silu(acc)

    return pl.pallas_call(
        body, name=name,
        out_shape=jax.ShapeDtypeStruct((t, width), F32),
        grid=(ncol, nrow),
        in_specs=[pl.BlockSpec((HALO, CONV_W), lambda j, i: (jnp.maximum(i * hb - 1, 0), cb0 + j)),
                  pl.BlockSpec((tile, CONV_W), lambda j, i: (i, cb0 + j)),
                  pl.BlockSpec((CONV_K, CONV_W), lambda j, i: (0, j)),
                  pl.BlockSpec((1, CONV_W), lambda j, i: (0, j))],
        out_specs=pl.BlockSpec((tile, CONV_W), lambda j, i: (i, j)),
        scratch_shapes=[pltpu.VMEM((tile + HALO, CONV_W), F32)],
        compiler_params=_cparams(("arbitrary", "arbitrary")),
    )(src, src, w, b)


def conv_bwd_pre(name, src, c0, width, w, b, dy, tile=512):
    t = src.shape[0]
    ncol, nrow = width // CONV_W, t // tile
    cb0 = c0 // CONV_W
    hb = tile // HALO

    def body(prev_ref, cur_ref, w_ref, b_ref, dy_ref, dpre_ref, dw_ref, db_ref, ext_ref):
        i = pl.program_id(1)
        ext_ref[0:HALO, :] = jnp.where(i == 0, 0.0, prev_ref[...])
        ext_ref[HALO:HALO + tile, :] = cur_ref[...]
        pre = jnp.broadcast_to(b_ref[...], (tile, CONV_W))
        for k in range(CONV_K):
            pre = pre + w_ref[k:k + 1, :] * ext_ref[pl.ds(HALO - (CONV_K - 1) + k, tile), :]
        s = _sigmoid(pre)
        dpre = dy_ref[...] * (s * (1.0 + pre * (1.0 - s)))
        dpre_ref[...] = dpre

        @pl.when(i == 0)
        def _():
            dw_ref[...] = jnp.zeros_like(dw_ref)
            db_ref[...] = jnp.zeros_like(db_ref)

        for k in range(CONV_K):
            dw_ref[k:k + 1, :] += jnp.sum(dpre * ext_ref[pl.ds(HALO - (CONV_K - 1) + k, tile), :],
                                          axis=0, keepdims=True)
        db_ref[...] += jnp.sum(dpre, axis=0, keepdims=True)

    return pl.pallas_call(
        body, name=name,
        out_shape=[jax.ShapeDtypeStruct((t, width), F32), jax.ShapeDtypeStruct((CONV_K, width), F32),
                   jax.ShapeDtypeStruct((1, width), F32)],
        grid=(ncol, nrow),
        in_specs=[pl.BlockSpec((HALO, CONV_W), lambda j, i: (jnp.maximum(i * hb - 1, 0), cb0 + j)),
                  pl.BlockSpec((tile, CONV_W), lambda j, i: (i, cb0 + j)),
                  pl.BlockSpec((CONV_K, CONV_W), lambda j, i: (0, j)),
                  pl.BlockSpec((1, CONV_W), lambda j, i: (0, j)),
                  pl.BlockSpec((tile, CONV_W), lambda j, i: (i, j))],
        out_specs=[pl.BlockSpec((tile, CONV_W), lambda j, i: (i, j)),
                   pl.BlockSpec((CONV_K, CONV_W), lambda j, i: (0, j)),
                   pl.BlockSpec((1, CONV_W), lambda j, i: (0, j))],
        scratch_shapes=[pltpu.VMEM((tile + HALO, CONV_W), F32)],
        compiler_params=_cparams(("arbitrary", "arbitrary")),
    )(src, src, w, b, dy)


def conv_bwd_in(name, dpre, w, tile=512):
    t, width = dpre.shape
    ncol, nrow = width // CONV_W, t // tile
    hb = tile // HALO
    last_hb = t // HALO - 1

    def body(cur_ref, nxt_ref, w_ref, o_ref, ext_ref):
        i = pl.program_id(1)
        ext_ref[0:tile, :] = cur_ref[...]
        ext_ref[tile:tile + HALO, :] = jnp.where(i == nrow - 1, 0.0, nxt_ref[...])
        acc = jnp.zeros((tile, CONV_W), F32)
        for k in range(CONV_K):
            acc = acc + w_ref[k:k + 1, :] * ext_ref[pl.ds(CONV_K - 1 - k, tile), :]
        o_ref[...] = acc

    return pl.pallas_call(
        body, name=name,
        out_shape=jax.ShapeDtypeStruct((t, width), F32),
        grid=(ncol, nrow),
        in_specs=[pl.BlockSpec((tile, CONV_W), lambda j, i: (i, j)),
                  pl.BlockSpec((HALO, CONV_W), lambda j, i: (jnp.minimum((i + 1) * hb, last_hb), j)),
                  pl.BlockSpec((CONV_K, CONV_W), lambda j, i: (0, j))],
        out_specs=pl.BlockSpec((tile, CONV_W), lambda j, i: (i, j)),
        scratch_shapes=[pltpu.VMEM((tile + HALO, CONV_W), F32)],
        compiler_params=_cparams(("arbitrary", "arbitrary")),
    )(dpre, dpre, w)


def _iota2(q):
    return (lax.broadcasted_iota(jnp.int32, (q, q), 0), lax.broadcasted_iota(jnp.int32, (q, q), 1))


def _lane_pick(blk, idx):
    lane = lax.broadcasted_iota(jnp.int32, (1, LANE), 1)
    return jnp.sum(jnp.where(lane == idx, blk, 0.0), axis=1, keepdims=True)


def _cum(a_col, r, c):
    a_row = jnp.sum(jnp.where(r == c, a_col, 0.0), axis=0, keepdims=True)
    cum_col = jnp.sum(jnp.where(c <= r, a_row, 0.0), axis=1, keepdims=True)
    cum_row = jnp.sum(jnp.where(r <= c, a_col, 0.0), axis=0, keepdims=True)
    return cum_col, cum_row


_SSD_B = SSM_D_INNER
_SSD_C = SSM_D_INNER + 2 * SSM_STATE


def _interleave(gens):
    results = [None] * len(gens)
    live = list(range(len(gens)))
    while live:
        for i in list(live):
            try:
                next(gens[i])
            except StopIteration as stop:
                results[i] = stop.value
                live.remove(i)
    return results


def ssd_chunk(xs, bm, cm, small, dtb, alog, dsk, state, p, cb):
    q = xs.shape[0]
    r, c = _iota2(q)
    lane = lax.broadcasted_iota(jnp.int32, (1, LANE), 1)
    m0 = lane < SSM_HEAD_DIM

    def head(h):
        dt = _softplus(_lane_pick(small, SM_DT + h) + _lane_pick(dtb, h))
        a = dt * (-jnp.exp(_lane_pick(alog, h)))
        cum_col, cum_row = _cum(a, r, c)
        lmat = jnp.exp(jnp.where(r >= c, cum_col - cum_row, NEG_BIG))
        tot = jnp.sum(a, axis=0, keepdims=True)
        return dt, _lane_pick(dsk, h), cum_col, lmat, tot

    dt0, d0, cum0, l0, tot0 = head(2 * p)
    dt1, d1, cum1, l1, tot1 = head(2 * p + 1)
    xdt = xs * jnp.where(m0, dt0, dt1)
    yield
    y_diag = _dot(cb * l0, jnp.where(m0, xdt, 0.0)) + _dot(cb * l1, jnp.where(m0, 0.0, xdt))
    y_off = _dot(cm, state, NT) * jnp.where(m0, jnp.exp(cum0), jnp.exp(cum1))
    yield
    dec = jnp.where(m0, jnp.exp(tot0 - cum0), jnp.exp(tot1 - cum1))
    rowm = lax.broadcasted_iota(jnp.int32, (LANE, 1), 0) < SSM_HEAD_DIM
    new_state = state * jnp.where(rowm, jnp.exp(tot0), jnp.exp(tot1)) + _dot(xdt * dec, bm, TN)
    y = y_diag + y_off + jnp.where(m0, d0, d1) * xs
    return y, new_state


def ssd_group(xs4, bm, cm, small, dtb, alog, dsk, states, pairs):
    cb = _dot(cm, bm, NT)
    res = _interleave([ssd_chunk(xs, bm, cm, small, dtb, alog, dsk, st, p, cb)
                       for xs, st, p in zip(xs4, states, pairs)])
    return tuple(y for y, _ in res), tuple(s for _, s in res)


def tri_inverse(a):
    q = a.shape[0]
    r, c = _iota2(q)
    eye = (r == c).astype(F32)
    diag = (r // 16) == (c // 16)
    bd = jnp.where(diag, a, 0.0)
    off = jnp.where(diag, 0.0, a)
    b2 = _dot_hi(bd, bd)
    d1 = _dot_hi(eye - bd, eye + b2)
    yield
    b4 = _dot_hi(b2, b2)
    yield
    b8 = _dot_hi(b4, b4)
    d2 = _dot_hi(d1, eye + b4)
    yield
    dinv = _dot_hi(d2, eye + b8)
    yield
    n = _dot_hi(dinv, off)
    yield
    n2 = _dot_hi(n, n)
    yield
    m = _dot_hi(eye + n2, dinv)
    yield
    return _dot_hi(eye - n, m)


@jax.custom_vjp
def _solve_with(xinv, a, rhs):
    del a
    return _dot_hi(xinv, rhs)


def _solve_with_fwd(xinv, a, rhs):
    t = _dot_hi(xinv, rhs)
    return t, (xinv, t)


def _solve_with_bwd(res, dt):
    xinv, t = res
    d_rhs = _dot_hi(xinv, dt, TN)
    d_a = -_dot(d_rhs, t, NT)
    return jnp.zeros_like(xinv), d_a, d_rhs


_solve_with.defvjp(_solve_with_fwd, _solve_with_bwd)


_GDN_K = GDN_HEADS * GDN_HEAD
_GDN_V = 2 * GDN_HEADS * GDN_HEAD


def gdn_chunk(qh, kh, vh, small, alog, dtb, state, h, xinv=None):
    q = qh.shape[0]
    r, c = _iota2(q)
    qn = qh * lax.rsqrt(jnp.sum(qh * qh, axis=-1, keepdims=True) + EPS) * (GDN_HEAD ** -0.5)
    kn = kh * lax.rsqrt(jnp.sum(kh * kh, axis=-1, keepdims=True) + EPS)
    beta = _sigmoid(_lane_pick(small, SM_B + h))
    g = -jnp.exp(_lane_pick(alog, h)) * _softplus(_lane_pick(small, SM_A + h) + _lane_pick(dtb, h))
    gc_col, gc_row = _cum(g, r, c)
    decay = jnp.exp(jnp.where(r >= c, gc_col - gc_row, NEG_BIG))
    yield
    kk = _dot(kn, kn, NT)
    qk = _dot(qn, kn, NT) * decay
    amat = jnp.where(r > c, kk * decay * beta, 0.0)
    eg = jnp.exp(gc_col)
    rhs = jnp.concatenate([vh * beta, kn * (beta * eg)], axis=1)
    qs = _dot(qn * eg, state)
    yield
    if xinv is None:
        xinv = yield from tri_inverse(amat)
        t = _dot_hi(xinv, rhs)
    else:
        t = _solve_with(xinv, amat, rhs)
    yield
    u, w = t[:, :GDN_HEAD], t[:, GDN_HEAD:]
    v_new = u - _dot(w, state)
    yield
    o = qs + _dot(qk, v_new)
    tot = jnp.sum(g, axis=0, keepdims=True)
    new_state = state * jnp.exp(tot) + _dot(kn * jnp.exp(tot - gc_col), v_new, TN)
    return o, new_state, xinv


def gdn_heads(qs, ks, vs, small, alog, dtb, states, xinvs=None):
    nh = len(qs)
    res = _interleave([gdn_chunk(qs[h], ks[h], vs[h], small, alog, dtb, states[h], h,
                                 None if xinvs is None else xinvs[h]) for h in range(nh)])
    return tuple(o for o, _, _ in res), tuple(s for _, s, _ in res), tuple(x for _, _, x in res)


def _acc(ref, val, first):
    @pl.when(first)
    def _():
        ref[...] = val

    @pl.when(jnp.logical_not(first))
    def _():
        ref[...] += val


def ssd_scan_fwd(name, xbc, proj, dtb, alog, dsk):
    t = xbc.shape[0]
    nc, npair = t // CHUNK, SSM_HEADS // 2
    small_blk = AL_SMALL // LANE

    def body(xbc_ref, sm_ref, dtb_ref, alog_ref, dsk_ref, y_ref, sin_ref, st_ref):
        ci = pl.program_id(0)

        @pl.when(ci == 0)
        def _():
            st_ref[...] = jnp.zeros_like(st_ref)

        sm, dtb_, alog_, dsk_ = sm_ref[...], dtb_ref[...], alog_ref[...], dsk_ref[...]
        for grp in range(2):
            pairs = list(range(4 * grp, 4 * grp + 4))
            s_in = tuple(st_ref[p] for p in pairs)
            ys, s_new = ssd_group(tuple(xbc_ref[:, p * LANE:(p + 1) * LANE] for p in pairs),
                                  xbc_ref[:, _SSD_B + grp * LANE:_SSD_B + (grp + 1) * LANE],
                                  xbc_ref[:, _SSD_C + grp * LANE:_SSD_C + (grp + 1) * LANE],
                                  sm, dtb_, alog_, dsk_, s_in, pairs)
            for i, p in enumerate(pairs):
                sin_ref[0, p] = s_in[i]
                y_ref[:, p * LANE:(p + 1) * LANE] = ys[i]
                st_ref[p] = s_new[i]

    par = pl.BlockSpec((1, LANE), lambda ci: (0, 0))
    return pl.pallas_call(
        body, name=name,
        out_shape=[jax.ShapeDtypeStruct((t, SSM_D_INNER), F32),
                   jax.ShapeDtypeStruct((nc, npair, LANE, LANE), F32)],
        grid=(nc,),
        in_specs=[pl.BlockSpec((CHUNK, SSM_CONV_DIM), lambda ci: (ci, 0)),
                  pl.BlockSpec((CHUNK, LANE), lambda ci: (ci, small_blk)),
                  par, par, par],
        out_specs=[pl.BlockSpec((CHUNK, SSM_D_INNER), lambda ci: (ci, 0)),
                   pl.BlockSpec((1, npair, LANE, LANE), lambda ci: (ci, 0, 0, 0))],
        scratch_shapes=[pltpu.VMEM((npair, LANE, LANE), F32)],
        compiler_params=_cparams(("arbitrary",)),
    )(xbc, proj, dtb, alog, dsk)


def ssd_scan_bwd(name, xbc, proj, dtb, alog, dsk, s_in, dy):
    t = xbc.shape[0]
    nc, npair = t // CHUNK, SSM_HEADS // 2
    small_blk = AL_SMALL // LANE

    def body(xbc_ref, sm_ref, dtb_ref, alog_ref, dsk_ref, sin_ref, dy_ref,
             dxbc_ref, dsm_ref, ddtb_ref, dalog_ref, ddsk_ref, dst_ref):
        ci = pl.program_id(0)

        @pl.when(ci == 0)
        def _():
            dst_ref[...] = jnp.zeros_like(dst_ref)

        dsm_t = jnp.zeros((CHUNK, LANE), F32)
        dpar = [jnp.zeros((1, LANE), F32)] * 3
        for grp in range(2):
            pairs = list(range(4 * grp, 4 * grp + 4))

            def fn(xs4, bm, cm, sm, dtb_, alog_, dsk_, sts, pairs=pairs):
                return ssd_group(xs4, bm, cm, sm, dtb_, alog_, dsk_, sts, pairs)

            xs4 = tuple(xbc_ref[:, p * LANE:(p + 1) * LANE] for p in pairs)
            _, vjp = jax.vjp(fn, xs4, xbc_ref[:, _SSD_B + grp * LANE:_SSD_B + (grp + 1) * LANE],
                             xbc_ref[:, _SSD_C + grp * LANE:_SSD_C + (grp + 1) * LANE],
                             sm_ref[...], dtb_ref[...], alog_ref[...], dsk_ref[...],
                             tuple(sin_ref[0, p] for p in pairs))
            dxs4, dbm, dcm, dsm, ddtb, dalog, ddsk, dsts = vjp(
                (tuple(dy_ref[:, p * LANE:(p + 1) * LANE] for p in pairs), tuple(dst_ref[p] for p in pairs)))
            for p, dxs, dst in zip(pairs, dxs4, dsts):
                dxbc_ref[:, p * LANE:(p + 1) * LANE] = dxs
                dst_ref[p] = dst
            dxbc_ref[:, _SSD_B + grp * LANE:_SSD_B + (grp + 1) * LANE] = dbm
            dxbc_ref[:, _SSD_C + grp * LANE:_SSD_C + (grp + 1) * LANE] = dcm
            dsm_t = dsm_t + dsm
            dpar = [a + b for a, b in zip(dpar, (ddtb, dalog, ddsk))]
        dsm_ref[...] = dsm_t
        _acc(ddtb_ref, dpar[0], ci == 0)
        _acc(dalog_ref, dpar[1], ci == 0)
        _acc(ddsk_ref, dpar[2], ci == 0)

    par = pl.BlockSpec((1, LANE), lambda ci: (0, 0))
    rev = lambda ci: nc - 1 - ci
    return pl.pallas_call(
        body, name=name,
        out_shape=[jax.ShapeDtypeStruct((t, SSM_CONV_DIM), F32),
                   jax.ShapeDtypeStruct((t, LANE), F32),
                   jax.ShapeDtypeStruct((1, LANE), F32), jax.ShapeDtypeStruct((1, LANE), F32),
                   jax.ShapeDtypeStruct((1, LANE), F32)],
        grid=(nc,),
        in_specs=[pl.BlockSpec((CHUNK, SSM_CONV_DIM), lambda ci: (rev(ci), 0)),
                  pl.BlockSpec((CHUNK, LANE), lambda ci: (rev(ci), small_blk)),
                  par, par, par,
                  pl.BlockSpec((1, npair, LANE, LANE), lambda ci: (rev(ci), 0, 0, 0)),
                  pl.BlockSpec((CHUNK, SSM_D_INNER), lambda ci: (rev(ci), 0))],
        out_specs=[pl.BlockSpec((CHUNK, SSM_CONV_DIM), lambda ci: (rev(ci), 0)),
                   pl.BlockSpec((CHUNK, LANE), lambda ci: (rev(ci), 0)),
                   par, par, par],
        scratch_shapes=[pltpu.VMEM((npair, LANE, LANE), F32)],
        compiler_params=_cparams(("arbitrary",)),
    )(xbc, proj, dtb, alog, dsk, s_in, dy)


def gdn_scan_fwd(name, qkv, proj, alog, dtb):
    t = qkv.shape[0]
    nc, nh = t // CHUNK, GDN_HEADS
    small_blk = AL_SMALL // LANE

    def body(qkv_ref, sm_ref, alog_ref, dtb_ref, o_ref, sin_ref, x_ref, st_ref):
        ci = pl.program_id(0)

        @pl.when(ci == 0)
        def _():
            st_ref[...] = jnp.zeros_like(st_ref)

        s_in = tuple(st_ref[h] for h in range(nh))
        os, s_new, xinvs = gdn_heads(
            tuple(qkv_ref[:, h * LANE:(h + 1) * LANE] for h in range(nh)),
            tuple(qkv_ref[:, _GDN_K + h * LANE:_GDN_K + (h + 1) * LANE] for h in range(nh)),
            tuple(qkv_ref[:, _GDN_V + h * LANE:_GDN_V + (h + 1) * LANE] for h in range(nh)),
            sm_ref[...], alog_ref[...], dtb_ref[...], s_in)
        for h in range(nh):
            sin_ref[0, h] = s_in[h]
            o_ref[:, h * LANE:(h + 1) * LANE] = os[h]
            x_ref[0, h] = xinvs[h]
            st_ref[h] = s_new[h]

    par = pl.BlockSpec((1, LANE), lambda ci: (0, 0))
    return pl.pallas_call(
        body, name=name,
        out_shape=[jax.ShapeDtypeStruct((t, GDN_HEADS * GDN_HEAD), F32),
                   jax.ShapeDtypeStruct((nc, nh, LANE, LANE), F32),
                   jax.ShapeDtypeStruct((nc, nh, CHUNK, CHUNK), F32)],
        grid=(nc,),
        in_specs=[pl.BlockSpec((CHUNK, GDN_QKV_DIM), lambda ci: (ci, 0)),
                  pl.BlockSpec((CHUNK, LANE), lambda ci: (ci, small_blk)),
                  par, par],
        out_specs=[pl.BlockSpec((CHUNK, GDN_HEADS * GDN_HEAD), lambda ci: (ci, 0)),
                   pl.BlockSpec((1, nh, LANE, LANE), lambda ci: (ci, 0, 0, 0)),
                   pl.BlockSpec((1, nh, CHUNK, CHUNK), lambda ci: (ci, 0, 0, 0))],
        scratch_shapes=[pltpu.VMEM((nh, LANE, LANE), F32)],
        compiler_params=_cparams(("arbitrary",)),
    )(qkv, proj, alog, dtb)


def gdn_scan_bwd(name, qkv, proj, alog, dtb, s_in, xinv, do, dsm_in):
    t = qkv.shape[0]
    nc, nh = t // CHUNK, GDN_HEADS
    small_blk = AL_SMALL // LANE

    def body(qkv_ref, sm_ref, alog_ref, dtb_ref, sin_ref, x_ref, do_ref, dsmi_ref,
             dqkv_ref, dsm_ref, dalog_ref, ddtb_ref, dst_ref):
        ci = pl.program_id(0)

        @pl.when(ci == 0)
        def _():
            dst_ref[...] = jnp.zeros_like(dst_ref)

        xis = tuple(x_ref[0, h] for h in range(nh))

        def fn(qs, ks, vs, sm, alog_, dtb_, sts):
            os, s_new, _ = gdn_heads(qs, ks, vs, sm, alog_, dtb_, sts, xinvs=xis)
            return os, s_new

        _, vjp = jax.vjp(fn, tuple(qkv_ref[:, h * LANE:(h + 1) * LANE] for h in range(nh)),
                         tuple(qkv_ref[:, _GDN_K + h * LANE:_GDN_K + (h + 1) * LANE] for h in range(nh)),
                         tuple(qkv_ref[:, _GDN_V + h * LANE:_GDN_V + (h + 1) * LANE] for h in range(nh)),
                         sm_ref[...], alog_ref[...], dtb_ref[...], tuple(sin_ref[0, h] for h in range(nh)))
        dqs, dks, dvs, dsm, dalog, ddtb, dsts = vjp(
            (tuple(do_ref[:, h * LANE:(h + 1) * LANE] for h in range(nh)), tuple(dst_ref[h] for h in range(nh))))
        for h in range(nh):
            dqkv_ref[:, h * LANE:(h + 1) * LANE] = dqs[h]
            dqkv_ref[:, _GDN_K + h * LANE:_GDN_K + (h + 1) * LANE] = dks[h]
            dqkv_ref[:, _GDN_V + h * LANE:_GDN_V + (h + 1) * LANE] = dvs[h]
            dst_ref[h] = dsts[h]
        dsm_ref[...] = dsmi_ref[...] + dsm
        _acc(dalog_ref, dalog, ci == 0)
        _acc(ddtb_ref, ddtb, ci == 0)

    par = pl.BlockSpec((1, LANE), lambda ci: (0, 0))
    rev = lambda ci: nc - 1 - ci
    return pl.pallas_call(
        body, name=name,
        out_shape=[jax.ShapeDtypeStruct((t, GDN_QKV_DIM), F32), jax.ShapeDtypeStruct((t, LANE), F32),
                   jax.ShapeDtypeStruct((1, LANE), F32), jax.ShapeDtypeStruct((1, LANE), F32)],
        grid=(nc,),
        in_specs=[pl.BlockSpec((CHUNK, GDN_QKV_DIM), lambda ci: (rev(ci), 0)),
                  pl.BlockSpec((CHUNK, LANE), lambda ci: (rev(ci), small_blk)),
                  par, par,
                  pl.BlockSpec((1, nh, LANE, LANE), lambda ci: (rev(ci), 0, 0, 0)),
                  pl.BlockSpec((1, nh, CHUNK, CHUNK), lambda ci: (rev(ci), 0, 0, 0)),
                  pl.BlockSpec((CHUNK, GDN_HEADS * GDN_HEAD), lambda ci: (rev(ci), 0)),
                  pl.BlockSpec((CHUNK, LANE), lambda ci: (rev(ci), 0))],
        out_specs=[pl.BlockSpec((CHUNK, GDN_QKV_DIM), lambda ci: (rev(ci), 0)),
                   pl.BlockSpec((CHUNK, LANE), lambda ci: (rev(ci), 0)),
                   par, par],
        scratch_shapes=[pltpu.VMEM((nh, LANE, LANE), F32)],
        compiler_params=_cparams(("arbitrary",)),
    )(qkv, proj, alog, dtb, s_in, xinv, do, dsm_in)


def _row(arr, w, c0=0, moves=False):
    return (arr, "row", w, c0, moves)


def _par(arr, w, c0=0, moves=False):
    return (arr, "par", w, c0, moves)


def matmul_add(name, a, b, res):
    (m, k), (_, n) = a.shape, b.shape
    tm = _pick(m, (512, 256, 128))
    tn = _pick(n, (512, 256, 128))
    tk = _pick_k(k, max(tm, tn), a, b)
    nk = k // tk

    def body_acc(a_ref, b_ref, r_ref, o_ref, acc_ref):
        kk = pl.program_id(2)

        @pl.when(kk == 0)
        def _():
            acc_ref[...] = r_ref[...]

        acc_ref[...] += _dot(a_ref[...], b_ref[...])

        @pl.when(kk == nk - 1)
        def _():
            o_ref[...] = acc_ref[...]

    def body_one(a_ref, b_ref, r_ref, o_ref):
        o_ref[...] = r_ref[...] + _dot(a_ref[...], b_ref[...])

    body = body_one if nk == 1 else body_acc
    return pl.pallas_call(
        body, name=name,
        out_shape=jax.ShapeDtypeStruct((m, n), F32),
        grid=(m // tm, n // tn, nk),
        in_specs=[pl.BlockSpec((tm, tk), lambda i, j, kk: (i, kk)),
                  pl.BlockSpec((tk, tn), lambda i, j, kk: (kk, j)),
                  pl.BlockSpec((tm, tn), lambda i, j, kk: (i, j))],
        out_specs=pl.BlockSpec((tm, tn), lambda i, j, kk: (i, j)),
        scratch_shapes=[] if nk == 1 else [pltpu.VMEM((tm, tn), F32)],
        compiler_params=_cparams(("parallel", "parallel", "arbitrary")),
    )(a, b, res)


def layer_fwd(l, x, w):
    t = x.shape[0]
    rt = min(256, t)
    s = {"x": x}
    s["h"] = rowwise_fwd(f"norm_mix_l{l}", f_rmsnorm, t, rt, 1,
                         [_row(x, D_MODEL), _par(w["norm_mix_w"], D_MODEL)], [(D_MODEL, BF16)])[0]
    s["proj"] = matmul(f"in_proj_l{l}", s["h"], w["w_in"], "nn")
    s["xbc"] = conv_fwd(f"ssm_conv_l{l}", s["proj"], AL_XBC, SSM_CONV_DIM, w["ssm_conv_w"], w["ssm_conv_b"],
                        tile=min(512, t))
    s["qkv"] = conv_fwd(f"gdn_conv_l{l}", s["proj"], AL_QKV, GDN_QKV_DIM, w["gdn_conv_w"], w["gdn_conv_b"],
                        tile=min(512, t))
    s["y_scan"], s["ssd_sin"] = ssd_scan_fwd(f"ssd_scan_l{l}", s["xbc"], s["proj"], w["ssm_dt_bias"],
                                             w["ssm_a_log"], w["ssm_d"])
    s["o_scan"], s["gdn_sin"], s["gdn_x"] = gdn_scan_fwd(f"gdn_scan_l{l}", s["qkv"], s["proj"],
                                                         w["gdn_a_log"], w["gdn_dt_bias"])
    s["y_ssm"] = rowwise_fwd(f"ssd_post_l{l}", f_ssd_post, t, rt, 2,
                             [_row(s["y_scan"], 512, 0, True), _row(s["proj"], 512, AL_Z // 512, True),
                              _par(w["ssm_norm_w"], 512, 0, True)], [(512, BF16)])[0]
    s["y_gdn"] = rowwise_fwd(f"gdn_post_l{l}", f_gdn_post, t, rt, GDN_HEADS,
                             [_row(s["o_scan"], LANE, 0, True), _row(s["proj"], LANE, AL_GZ // LANE, True),
                              _par(w["gdn_norm_w"], LANE)], [(LANE, BF16)])[0]
    s["p1"] = matmul(f"proj_ssm_l{l}", s["y_ssm"], w["w_proj_ssm"], "nn")
    s["p2"] = matmul(f"proj_gdn_l{l}", s["y_gdn"], w["w_proj_gdn"], "nn")
    s["merged"] = rowwise_fwd(f"merge_l{l}", f_merge, t, rt, 2,
                              [_row(s["proj"], 512, AL_GS // 512, True), _row(s["p1"], 512, 0, True),
                               _row(s["proj"], 512, AL_GG // 512, True), _row(s["p2"], 512, 0, True)],
                              [(512, BF16)])[0]
    s["x1"] = matmul_add(f"out_proj_l{l}", s["merged"], w["w_out"], x)
    s["h2"] = rowwise_fwd(f"norm_ffn_l{l}", f_rmsnorm, t, rt, 1,
                          [_row(s["x1"], D_MODEL), _par(w["norm_ffn_w"], D_MODEL)], [(D_MODEL, BF16)])[0]
    s["gu"] = matmul(f"ffn_in_l{l}", s["h2"], w["w_ffn_in"], "nn")
    s["act"] = rowwise_fwd(f"swiglu_l{l}", f_swiglu, t, rt, FFN_HIDDEN // 256,
                           [_row(s["gu"], 256, 0, True), _row(s["gu"], 256, FFN_HIDDEN // 256, True)],
                           [(256, BF16)])[0]
    x2 = matmul_add(f"ffn_down_l{l}", s["act"], w["w_ffn_down"], s["x1"])
    return x2, s


def layer_bwd(l, dx2, w, s):
    t = dx2.shape[0]
    rt = min(256, t)
    ct = min(512, t)
    g = {}
    dact = matmul(f"ffn_down_dx_l{l}", dx2, w["w_ffn_down"], "nt")
    g["w_ffn_down"] = matmul(f"ffn_down_dw_l{l}", s["act"], dx2, "tn")
    nf = FFN_HIDDEN // 256
    dgate, dup = rowwise_bwd(f"swiglu_bwd_l{l}", f_swiglu, t, rt, nf,
                             [_row(s["gu"], 256, 0, True), _row(s["gu"], 256, nf, True)], [True, True],
                             [_row(dact, 256, 0, True)])
    dgu = jnp.concatenate([dgate, dup], axis=1)
    dh2 = matmul(f"ffn_in_dx_l{l}", dgu, w["w_ffn_in"], "nt")
    g["w_ffn_in"] = matmul(f"ffn_in_dw_l{l}", s["h2"], dgu, "tn")
    dx1, g["norm_ffn_w"] = rowwise_bwd(f"norm_ffn_bwd_l{l}", f_rmsnorm, t, rt, 1,
                                       [_row(s["x1"], D_MODEL), _par(w["norm_ffn_w"], D_MODEL)], [True, True],
                                       [_row(dh2, D_MODEL)], addends={0: _row(dx2, D_MODEL)})
    dmerged = matmul(f"out_proj_dx_l{l}", dx1, w["w_out"], "nt")
    g["w_out"] = matmul(f"out_proj_dw_l{l}", s["merged"], dx1, "tn")
    dgs, dp1, dgg, dp2 = rowwise_bwd(
        f"merge_bwd_l{l}", f_merge, t, rt, 2,
        [_row(s["proj"], 512, AL_GS // 512, True), _row(s["p1"], 512, 0, True),
         _row(s["proj"], 512, AL_GG // 512, True), _row(s["p2"], 512, 0, True)], [True] * 4,
        [_row(dmerged, 512, 0, True)])
    dy_ssm = matmul(f"proj_ssm_dx_l{l}", dp1, w["w_proj_ssm"], "nt")
    g["w_proj_ssm"] = matmul(f"proj_ssm_dw_l{l}", s["y_ssm"], dp1, "tn")
    dy_gdn = matmul(f"proj_gdn_dx_l{l}", dp2, w["w_proj_gdn"], "nt")
    g["w_proj_gdn"] = matmul(f"proj_gdn_dw_l{l}", s["y_gdn"], dp2, "tn")
    dy_scan, dz, g["ssm_norm_w"] = rowwise_bwd(
        f"ssd_post_bwd_l{l}", f_ssd_post, t, rt, 2,
        [_row(s["y_scan"], 512, 0, True), _row(s["proj"], 512, AL_Z // 512, True),
         _par(w["ssm_norm_w"], 512, 0, True)], [True] * 3, [_row(dy_ssm, 512, 0, True)])
    dxbc_act, dsm, g["ssm_dt_bias"], g["ssm_a_log"], g["ssm_d"] = ssd_scan_bwd(
        f"ssd_scan_bwd_l{l}", s["xbc"], s["proj"], w["ssm_dt_bias"], w["ssm_a_log"], w["ssm_d"],
        s["ssd_sin"], dy_scan)
    dpre, g["ssm_conv_w"], g["ssm_conv_b"] = conv_bwd_pre(
        f"ssm_conv_bwd_l{l}", s["proj"], AL_XBC, SSM_CONV_DIM, w["ssm_conv_w"], w["ssm_conv_b"], dxbc_act, tile=ct)
    dxbc = conv_bwd_in(f"ssm_conv_din_l{l}", dpre, w["ssm_conv_w"], tile=ct)
    do_scan, dgz, g["gdn_norm_w"] = rowwise_bwd(
        f"gdn_post_bwd_l{l}", f_gdn_post, t, rt, GDN_HEADS,
        [_row(s["o_scan"], LANE, 0, True), _row(s["proj"], LANE, AL_GZ // LANE, True),
         _par(w["gdn_norm_w"], LANE)], [True] * 3, [_row(dy_gdn, LANE, 0, True)])
    dqkv_act, dsm, g["gdn_a_log"], g["gdn_dt_bias"] = gdn_scan_bwd(
        f"gdn_scan_bwd_l{l}", s["qkv"], s["proj"], w["gdn_a_log"], w["gdn_dt_bias"], s["gdn_sin"],
        s["gdn_x"], do_scan, dsm)
    dpre, g["gdn_conv_w"], _ = conv_bwd_pre(
        f"gdn_conv_bwd_l{l}", s["proj"], AL_QKV, GDN_QKV_DIM, w["gdn_conv_w"], w["gdn_conv_b"], dqkv_act, tile=ct)
    dqkv = conv_bwd_in(f"gdn_conv_din_l{l}", dpre, w["gdn_conv_w"], tile=ct)
    dproj = jnp.concatenate([dz, dxbc, dqkv, dgz, dgs, dgg, dsm,
                             jnp.zeros((t, AL_DIM - AL_SMALL - LANE), F32)], axis=1)
    dh = matmul(f"in_proj_dx_l{l}", dproj, w["w_in"], "nt")
    g["w_in"] = matmul(f"in_proj_dw_l{l}", s["h"], dproj, "tn")
    dx0, g["norm_mix_w"] = rowwise_bwd(f"norm_mix_bwd_l{l}", f_rmsnorm, t, rt, 1,
                                       [_row(s["x"], D_MODEL), _par(w["norm_mix_w"], D_MODEL)], [True, True],
                                       [_row(dh, D_MODEL)], addends={0: _row(dx1, D_MODEL)})
    return dx0, g


def _align_w_in(w):
    pad = jnp.zeros((w.shape[0], AL_DIM - AL_SMALL - 32), w.dtype)
    return jnp.concatenate([w[:, 0:2560], w[:, 2576:6672], w[:, 6688:8736],
                            w[:, 2560:2576], w[:, 6672:6688], pad], axis=1)


def _unalign_w_in(g):
    return jnp.concatenate([g[:, 0:2560], g[:, AL_SMALL:AL_SMALL + 16], g[:, 2560:6656],
                            g[:, AL_SMALL + 16:AL_SMALL + 32], g[:, 6656:8704]], axis=1)


def _pad_lane(v):
    return jnp.zeros((1, LANE), F32).at[0, :v.shape[0]].set(v)


def local_step(x, target, full):
    ws = []
    for l in range(DEPTH):
        ws.append({
            "norm_mix_w": full["norm_mix_w"][l][None], "w_in": _align_w_in(full["w_in"][l]),
            "ssm_conv_w": full["ssm_conv_w"][l], "ssm_conv_b": full["ssm_conv_b"][l][None],
            "ssm_dt_bias": _pad_lane(full["ssm_dt_bias"][l]), "ssm_a_log": _pad_lane(full["ssm_a_log"][l]),
            "ssm_d": _pad_lane(full["ssm_d"][l]), "ssm_norm_w": full["ssm_norm_w"][l][None],
            "gdn_conv_w": full["gdn_conv_w"][l], "gdn_conv_b": jnp.zeros((1, GDN_QKV_DIM), F32),
            "gdn_a_log": _pad_lane(full["gdn_a_log"][l]), "gdn_dt_bias": _pad_lane(full["gdn_dt_bias"][l]),
            "gdn_norm_w": full["gdn_norm_w"][l][None],
            "w_proj_ssm": full["w_proj_ssm"][l], "w_proj_gdn": full["w_proj_gdn"][l], "w_out": full["w_out"][l],
            "norm_ffn_w": full["norm_ffn_w"][l][None], "w_ffn_in": full["w_ffn_in"][l],
            "w_ffn_down": full["w_ffn_down"][l],
        })
    saved = []
    h = x
    for l in range(DEPTH):
        h, s = layer_fwd(l, h, ws[l])
        saved.append(s)
    loss, dx, g_final = final_loss("final_loss", h, target, full["final_norm_w"][None], tile=min(256, x.shape[0]))
    per_layer = [None] * DEPTH
    for l in reversed(range(DEPTH)):
        dx, per_layer[l] = layer_bwd(l, dx, ws[l], saved[l])
    grads = {"final_norm_w": g_final[0]}
    for name in per_layer[0]:
        rows = []
        for l in range(DEPTH):
            gl = per_layer[l][name]
            if name == "w_in":
                gl = _unalign_w_in(gl)
            elif name in ("ssm_dt_bias", "ssm_a_log", "ssm_d"):
                gl = gl[0, :SSM_HEADS]
            elif name in ("gdn_a_log", "gdn_dt_bias"):
                gl = gl[0, :GDN_HEADS]
            elif name in ("norm_mix_w", "ssm_conv_b", "ssm_norm_w", "gdn_norm_w", "norm_ffn_w"):
                gl = gl[0]
            rows.append(gl)
        grads[name] = jnp.stack(rows)
    return loss, dx, grads


MESH = pl.DeviceIdType.MESH
HBM = pl.BlockSpec(memory_space=pltpu.HBM)
N_CHIPS = 4
N_DEV = 8
PACK_W = 1024


def _pos():
    return lax.axis_index("x"), lax.axis_index("y"), lax.axis_index("c")


def _other_chips(x, y):
    return [(1 - x, y), (x, 1 - y), (1 - x, 1 - y)]


def _rcopy(src, dst, send_sem, recv_sem, dev):
    return pltpu.make_async_remote_copy(src_ref=src, dst_ref=dst, send_sem=send_sem, recv_sem=recv_sem,
                                        device_id=dev, device_id_type=MESH)


def chip_allgather(name, buf):
    _, rh, cd = buf.shape

    def body(src, out, send_sems, recv_sems, lsem):
        x, y, c = _pos()
        me = 2 * x + y
        sib = (x, y, 1 - c)
        chips = [(px, py, 2 * px + py) for px, py in _other_chips(x, y)]
        local = pltpu.make_async_copy(src, out.at[me], lsem)
        local.start()
        sends = []
        for k, (px, py, _) in enumerate(chips):
            cp = _rcopy(src.at[c], out.at[me, c], send_sems.at[k], recv_sems.at[k], (px, py, c))
            cp.start()
            sends.append(cp)
        for k, (px, py, pj) in enumerate(chips):
            _rcopy(src.at[c], out.at[pj, c], send_sems.at[k], recv_sems.at[k], (px, py, c)).wait_recv()
            fw = _rcopy(out.at[pj, c], out.at[pj, c], send_sems.at[3 + k], recv_sems.at[3 + k], sib)
            fw.start()
            sends.append(fw)
        for k, (_, _, pj) in enumerate(chips):
            _rcopy(out.at[pj, 1 - c], out.at[pj, 1 - c], send_sems.at[3 + k], recv_sems.at[3 + k], sib).wait_recv()
        for cp in sends:
            cp.wait_send()
        local.wait()

    return pl.pallas_call(
        body, name=name, out_shape=jax.ShapeDtypeStruct((N_CHIPS, 2, rh, cd), buf.dtype),
        in_specs=[HBM], out_specs=HBM,
        scratch_shapes=[pltpu.SemaphoreType.DMA((6,)), pltpu.SemaphoreType.DMA((6,)), pltpu.SemaphoreType.DMA(())],
    )(buf)


def pair_swap(name, p):
    _, nj, rh, cd = p.shape

    def body(src, out, send_sem, recv_sem):
        x, y, c = _pos()
        cp = _rcopy(src.at[1 - c], out, send_sem, recv_sem, (x, y, 1 - c))
        cp.start()
        cp.wait()

    return pl.pallas_call(
        body, name=name, out_shape=jax.ShapeDtypeStruct((nj, rh, cd), p.dtype),
        in_specs=[HBM], out_specs=HBM,
        scratch_shapes=[pltpu.SemaphoreType.DMA(()), pltpu.SemaphoreType.DMA(())],
    )(p)


def chip_scatter(name, s):
    nj, rh, cd = s.shape

    def body(src, out, send_sems, recv_sems, lsem):
        x, y, c = _pos()
        me = 2 * x + y
        local = pltpu.make_async_copy(src.at[me], out.at[me], lsem)
        local.start()
        sends = []
        for k, (px, py) in enumerate(_other_chips(x, y)):
            cp = _rcopy(src.at[2 * px + py], out.at[me], send_sems.at[k], recv_sems.at[k], (px, py, c))
            cp.start()
            sends.append(cp)
        for k, (px, py) in enumerate(_other_chips(x, y)):
            pj = 2 * px + py
            _rcopy(src.at[pj], out.at[pj], send_sems.at[k], recv_sems.at[k], (px, py, c)).wait_recv()
        for cp in sends:
            cp.wait_send()
        local.wait()

    return pl.pallas_call(
        body, name=name, out_shape=jax.ShapeDtypeStruct((nj, rh, cd), s.dtype),
        in_specs=[HBM], out_specs=HBM,
        scratch_shapes=[pltpu.SemaphoreType.DMA((3,)), pltpu.SemaphoreType.DMA((3,)), pltpu.SemaphoreType.DMA(())],
    )(s)


def pair_share(name, g):
    rh, cd = g.shape

    def body(src, out, send_sem, recv_sem, lsem):
        x, y, c = _pos()
        local = pltpu.make_async_copy(src, out.at[c], lsem)
        local.start()
        cp = _rcopy(src, out.at[c], send_sem, recv_sem, (x, y, 1 - c))
        cp.start()
        _rcopy(src, out.at[1 - c], send_sem, recv_sem, (x, y, 1 - c)).wait_recv()
        cp.wait_send()
        local.wait()

    return pl.pallas_call(
        body, name=name, out_shape=jax.ShapeDtypeStruct((2, rh, cd), g.dtype),
        in_specs=[HBM], out_specs=HBM,
        scratch_shapes=[pltpu.SemaphoreType.DMA(()), pltpu.SemaphoreType.DMA(()), pltpu.SemaphoreType.DMA(())],
    )(g)


def all_allgather(name, buf):
    r, cd = buf.shape

    def body(src, out, send_sems, recv_sems, lsem):
        x, y, c = _pos()
        me = 4 * x + 2 * y + c
        local = pltpu.make_async_copy(src, out.at[me], lsem)
        local.start()

        def peer(mask):
            px = 1 - x if mask & 4 else x
            py = 1 - y if mask & 2 else y
            pc = 1 - c if mask & 1 else c
            return px, py, pc

        sends = []
        for mask in range(1, N_DEV):
            cp = _rcopy(src, out.at[me], send_sems.at[mask - 1], recv_sems.at[mask - 1], peer(mask))
            cp.start()
            sends.append(cp)
        for mask in range(1, N_DEV):
            px, py, pc = peer(mask)
            _rcopy(src, out.at[4 * px + 2 * py + pc], send_sems.at[mask - 1], recv_sems.at[mask - 1],
                   (px, py, pc)).wait_recv()
        for cp in sends:
            cp.wait_send()
        local.wait()

    return pl.pallas_call(
        body, name=name, out_shape=jax.ShapeDtypeStruct((N_DEV, r, cd), buf.dtype),
        in_specs=[HBM], out_specs=HBM,
        scratch_shapes=[pltpu.SemaphoreType.DMA((N_DEV - 1,)), pltpu.SemaphoreType.DMA((N_DEV - 1,)),
                        pltpu.SemaphoreType.DMA(())],
    )(buf)


def pair_add(name, p, recv, c):
    _, nj, rh, cd = p.shape
    tr = _pick(rh, (512, 256, 128, 64, 32, 16))

    def body(c_ref, p_ref, r_ref, o_ref):
        del c_ref
        o_ref[...] = (p_ref[0] + r_ref[...]).astype(o_ref.dtype)

    return pl.pallas_call(
        body, name=name, out_shape=jax.ShapeDtypeStruct((nj, rh, cd), BF16),
        grid_spec=pltpu.PrefetchScalarGridSpec(
            num_scalar_prefetch=1, grid=(nj, rh // tr),
            in_specs=[pl.BlockSpec((1, 1, tr, cd), lambda j, i, c_ref: (c_ref[0], j, i, 0)),
                      pl.BlockSpec((1, tr, cd), lambda j, i, c_ref: (j, i, 0))],
            out_specs=pl.BlockSpec((1, tr, cd), lambda j, i, c_ref: (j, i, 0))),
        compiler_params=_cparams(("arbitrary", "arbitrary")),
    )(jnp.reshape(c, (1,)).astype(jnp.int32), p, recv)


def slab_sum(name, a):
    n, r, cd = a.shape
    tr = _pick(r, (256, 128, 64, 32, 16, 8))

    def body(a_ref, o_ref):
        acc = a_ref[0].astype(F32)
        for j in range(1, n):
            acc = acc + a_ref[j].astype(F32)
        o_ref[...] = acc

    return pl.pallas_call(
        body, name=name, out_shape=jax.ShapeDtypeStruct((r, cd), F32),
        grid=(r // tr,),
        in_specs=[pl.BlockSpec((n, tr, cd), lambda i: (0, i, 0))],
        out_specs=pl.BlockSpec((tr, cd), lambda i: (i, 0)),
        compiler_params=_cparams(("arbitrary",)),
    )(a)


ADAM_C1 = 1.0 - ADAM_B1 ** ADAM_STEP
ADAM_C2 = 1.0 - ADAM_B2 ** ADAM_STEP


def adamw(name, w, g, m, v):
    r, cd = w.shape
    tr = r
    for cand in (512, 256, 128, 64, 32, 16, 8):
        if r % cand == 0 and cand * cd * 4 <= (1 << 20):
            tr = cand
            break

    def body(w_ref, g_ref, m_ref, v_ref, d_ref, nm_ref, nv_ref):
        gv = g_ref[...]
        nm = ADAM_B1 * m_ref[...] + (1.0 - ADAM_B1) * gv
        nv = ADAM_B2 * v_ref[...] + (1.0 - ADAM_B2) * (gv * gv)
        m_hat = nm / ADAM_C1
        v_hat = nv / ADAM_C2
        d_ref[...] = -ADAM_LR * (m_hat / (jnp.sqrt(v_hat) + ADAM_EPS) + ADAM_WD * w_ref[...])
        nm_ref[...] = nm
        nv_ref[...] = nv

    spec = pl.BlockSpec((tr, cd), lambda i: (i, 0))
    sd = jax.ShapeDtypeStruct((r, cd), F32)
    return pl.pallas_call(
        body, name=name, out_shape=[sd, sd, sd], grid=(r // tr,),
        in_specs=[spec] * 4, out_specs=[spec] * 3,
        compiler_params=_cparams(("arbitrary",)),
    )(w, g, m, v)


WEIGHTS = ("norm_mix_w", "w_in", "ssm_conv_w", "ssm_conv_b", "ssm_dt_bias", "ssm_a_log", "ssm_d", "ssm_norm_w",
           "gdn_conv_w", "gdn_a_log", "gdn_dt_bias", "gdn_norm_w", "w_proj_ssm", "w_proj_gdn", "w_out",
           "norm_ffn_w", "w_ffn_in", "w_ffn_down", "final_norm_w")
BIG = (("w_in", 2), ("w_proj_ssm", 1), ("w_proj_gdn", 1), ("w_out", 1), ("w_ffn_in", 2), ("w_ffn_down", 1))
CONVW = (("ssm_conv_w", 2), ("gdn_conv_w", 2))
SHARDED = BIG + CONVW
SMALL = tuple(n for n in WEIGHTS if n not in dict(SHARDED))
GRAD_ROWS = 10240


def _pack(arrs, rows, dtype):
    flat = jnp.concatenate([a.reshape(-1).astype(dtype) for a in arrs])
    return jnp.pad(flat, (0, rows * PACK_W - flat.shape[0])).reshape(rows, PACK_W)


def _unpack(buf, shapes):
    flat = buf.reshape(-1)
    out, o = [], 0
    for shp in shapes:
        n = math.prod(shp)
        out.append(flat[o:o + n].reshape(shp))
        o += n
    return out


def _pack_small(d, extra=None):
    arrs = [d[n] for n in SMALL] + ([extra] if extra is not None else [])
    n = sum(math.prod(a.shape) for a in arrs)
    rows = -(-n // (8 * LANE)) * 8
    flat = jnp.concatenate([a.reshape(-1) for a in arrs])
    return jnp.pad(flat, (0, rows * LANE - n)).reshape(rows, LANE)


def _shard_slice(a, axis, j):
    n = a.shape[axis] // N_CHIPS
    return lax.slice_in_dim(a, j * n, (j + 1) * n, axis=axis)


def kernel(x, norm_mix_w, w_in, ssm_conv_w, ssm_conv_b, ssm_dt_bias, ssm_a_log, ssm_d, ssm_norm_w, gdn_conv_w, gdn_a_log, gdn_dt_bias, gdn_norm_w, w_proj_ssm, w_proj_gdn, w_out, norm_ffn_w, w_ffn_in, w_ffn_down, final_norm_w, loss_target, m_norm_mix_w, m_w_in, m_ssm_conv_w, m_ssm_conv_b, m_ssm_dt_bias, m_ssm_a_log, m_ssm_d, m_ssm_norm_w, m_gdn_conv_w, m_gdn_a_log, m_gdn_dt_bias, m_gdn_norm_w, m_w_proj_ssm, m_w_proj_gdn, m_w_out, m_norm_ffn_w, m_w_ffn_in, m_w_ffn_down, m_final_norm_w, v_norm_mix_w, v_w_in, v_ssm_conv_w, v_ssm_conv_b, v_ssm_dt_bias, v_ssm_a_log, v_ssm_d, v_ssm_norm_w, v_gdn_conv_w, v_gdn_a_log, v_gdn_dt_bias, v_gdn_norm_w, v_w_proj_ssm, v_w_proj_gdn, v_w_out, v_norm_ffn_w, v_w_ffn_in, v_w_ffn_down, v_final_norm_w):
    wl = (norm_mix_w, w_in, ssm_conv_w, ssm_conv_b, ssm_dt_bias, ssm_a_log, ssm_d, ssm_norm_w, gdn_conv_w,
          gdn_a_log, gdn_dt_bias, gdn_norm_w, w_proj_ssm, w_proj_gdn, w_out, norm_ffn_w, w_ffn_in, w_ffn_down,
          final_norm_w)
    ml = (m_norm_mix_w, m_w_in, m_ssm_conv_w, m_ssm_conv_b, m_ssm_dt_bias, m_ssm_a_log, m_ssm_d, m_ssm_norm_w,
          m_gdn_conv_w, m_gdn_a_log, m_gdn_dt_bias, m_gdn_norm_w, m_w_proj_ssm, m_w_proj_gdn, m_w_out,
          m_norm_ffn_w, m_w_ffn_in, m_w_ffn_down, m_final_norm_w)
    vl = (v_norm_mix_w, v_w_in, v_ssm_conv_w, v_ssm_conv_b, v_ssm_dt_bias, v_ssm_a_log, v_ssm_d, v_ssm_norm_w,
          v_gdn_conv_w, v_gdn_a_log, v_gdn_dt_bias, v_gdn_norm_w, v_w_proj_ssm, v_w_proj_gdn, v_w_out,
          v_norm_ffn_w, v_w_ffn_in, v_w_ffn_down, v_final_norm_w)
    w = dict(zip(WEIGHTS, wl))
    m = dict(zip(WEIGHTS, ml))
    v = dict(zip(WEIGHTS, vl))
    c = lax.axis_index("c")

    big_shapes = [w[n].shape for n, _ in BIG]
    conv_shapes = [w[n].shape for n, _ in CONVW]
    big_rows = -(-sum(math.prod(s) for s in big_shapes) // (32 * PACK_W)) * 32
    gathered_big = chip_allgather(
        "gather_w", _pack([w[n] for n, _ in BIG], big_rows, BF16).reshape(2, big_rows // 2, PACK_W))
    gathered_conv = chip_allgather(
        "gather_conv_w", _pack([w[n] for n, _ in CONVW], 16, F32).reshape(2, 8, PACK_W))
    full = {n: w[n] for n in SMALL}
    big_parts = [_unpack(gathered_big[j], big_shapes) for j in range(N_CHIPS)]
    conv_parts = [_unpack(gathered_conv[j], conv_shapes) for j in range(N_CHIPS)]
    for i, (n, axis) in enumerate(BIG):
        full[n] = jnp.concatenate([big_parts[j][i] for j in range(N_CHIPS)], axis=axis)
    for i, (n, axis) in enumerate(CONVW):
        full[n] = jnp.concatenate([conv_parts[j][i] for j in range(N_CHIPS)], axis=axis)

    loss_part, grad_x, grads = local_step(x[0], loss_target[0], full)

    rh = GRAD_ROWS // 2
    packed = jnp.stack([_pack([_shard_slice(grads[n], axis, j) for n, axis in SHARDED], GRAD_ROWS, F32)
                        for j in range(N_CHIPS)])
    packed = packed.reshape(N_CHIPS, 2, rh, PACK_W).transpose(1, 0, 2, 3)
    from_pair = pair_swap("grad_pair_swap", packed)
    chip_part = pair_add("grad_pair_add", packed, from_pair, c)
    from_chips = chip_scatter("grad_chip_scatter", chip_part)
    reduced_half = slab_sum("grad_chip_sum", from_chips)
    reduced = pair_share("grad_pair_share", reduced_half).reshape(GRAD_ROWS, PACK_W)
    g_sharded = dict(zip([n for n, _ in SHARDED], _unpack(reduced, [w[n].shape for n, _ in SHARDED])))

    small_all = all_allgather("gather_small", _pack_small(grads, extra=loss_part[0, :1]))
    small_sum = slab_sum("small_sum", small_all)
    small_vals = _unpack(small_sum, [w[n].shape for n in SMALL] + [(1,)])
    g_small = dict(zip(SMALL, small_vals[:-1]))
    loss = small_vals[-1].reshape(())

    out_g, out_d, out_m, out_v = {}, {}, {}, {}
    for n, _ in SHARDED:
        shp = w[n].shape
        two = lambda a: a.reshape(-1, shp[-1])
        d_, m_, v_ = adamw(f"adamw_{n}", two(w[n]), two(g_sharded[n]), two(m[n]), two(v[n]))
        out_g[n], out_d[n], out_m[n], out_v[n] = g_sharded[n], d_.reshape(shp), m_.reshape(shp), v_.reshape(shp)
    d_, m_, v_ = adamw("adamw_small", _pack_small(w), _pack_small(g_small), _pack_small(m), _pack_small(v))
    small_shapes = [w[n].shape for n in SMALL]
    for n, dd, mm, vv in zip(SMALL, _unpack(d_, small_shapes), _unpack(m_, small_shapes), _unpack(v_, small_shapes)):
        out_g[n], out_d[n], out_m[n], out_v[n] = g_small[n], dd, mm, vv

    return (loss, grad_x[None], *[out_g[n] for n in WEIGHTS], *[out_d[n] for n in WEIGHTS],
            *[out_m[n] for n in WEIGHTS], *[out_v[n] for n in WEIGHTS])
```

```python
import math

import jax
import jax.numpy as jnp
from jax import lax
from jax.experimental import pallas as pl
from jax.experimental.pallas import tpu as pltpu

F32 = jnp.float32
BF16 = jnp.bfloat16

D_MODEL = 1024
DEPTH = 2
SSM_HEADS = 16
SSM_HEAD_DIM = 64
SSM_D_INNER = 1024
SSM_STATE = 128
SSM_CONV_DIM = 1536
GDN_HEADS = 8
GDN_HEAD = 128
GDN_QKV_DIM = 3072
CONV_K = 4
CHUNK = 64
FFN_HIDDEN = 2816
EPS = 1e-6
IN_DIM = 8736

ADAM_LR = 0.001
ADAM_B1 = 0.9
ADAM_B2 = 0.999
ADAM_EPS = 1e-08
ADAM_WD = 0.01
ADAM_STEP = 10

LANE = 128
NEG_BIG = -1e30
VMEM_LIMIT = 56 * 1024 * 1024

AL_Z, AL_XBC, AL_QKV, AL_GZ, AL_GS, AL_GG, AL_SMALL, AL_DIM = 0, 1024, 2560, 5632, 6656, 7680, 8704, 9216
SM_DT, SM_A, SM_B = 0, 16, 24

HI = lax.Precision.HIGHEST
NN = (((1,), (0,)), ((), ()))
NT = (((1,), (1,)), ((), ()))
TN = (((0,), (0,)), ((), ()))


def _cparams(sem):
    return pltpu.CompilerParams(dimension_semantics=sem, vmem_limit_bytes=VMEM_LIMIT)


def _pick(n, prefs):
    for p in prefs:
        if n % p == 0:
            return p
    return n


MAX_K_BLOCK_BYTES = 3 << 20


def _pick_k(k, other, a, b):
    if k * other * max(a.dtype.itemsize, b.dtype.itemsize) <= MAX_K_BLOCK_BYTES:
        return k
    return _pick(k, (1024, 512, 256, 128))


def _dot(a, b, dims=NN):
    return lax.dot_general(a.astype(BF16), b.astype(BF16), dims, preferred_element_type=F32)


def _dot_hi(a, b, dims=NN):
    return lax.dot_general(a, b, dims, precision=HI, preferred_element_type=F32)


def _sigmoid(x):
    return jax.nn.sigmoid(x)


def _silu(x):
    return x * _sigmoid(x)


def _softplus(x):
    return jnp.maximum(x, 0.0) + jnp.log1p(jnp.exp(-jnp.abs(x)))


def matmul(name, a, b, mode, out_dtype=F32, chip_major=False, stack=None):
    if mode == "nn":
        (m, k), (k2, n) = a.shape, b.shape
    elif mode == "nt":
        (m, k), (n, k2) = a.shape, b.shape
    else:
        (k, m), (k2, n) = a.shape, b.shape
    assert k == k2, (a.shape, b.shape, mode)
    tm = _pick(m, (512, 256, 128))
    if chip_major:
        shard = n // N_CHIPS
        tn = _pick(shard, (1024, 768, 512))
        per = shard // tn
        base_shape, base_blk = (N_CHIPS, m, shard), (None, tm, tn)
        base_idx = lambda i, j: (j // per, i, j % per)
    else:
        tn = _pick(n, (512, 256, 128))
        base_shape, base_blk = (m, n), (tm, tn)
        base_idx = lambda i, j: (i, j)
    tk = _pick_k(k, max(tm, tn), a, b)
    nk = k // tk
    dims = {"nn": NN, "nt": NT, "tn": TN}[mode]

    def body_acc(a_ref, b_ref, o_ref, acc_ref):
        kk = pl.program_id(2)

        @pl.when(kk == 0)
        def _():
            acc_ref[...] = jnp.zeros_like(acc_ref)

        acc_ref[...] += _dot(a_ref[...], b_ref[...], dims)

        @pl.when(kk == nk - 1)
        def _():
            o_ref[...] = acc_ref[...].astype(o_ref.dtype)

    def body_one(a_ref, b_ref, o_ref):
        o_ref[...] = _dot(a_ref[...], b_ref[...], dims).astype(o_ref.dtype)

    compute = body_one if nk == 1 else body_acc
    if mode == "tn":
        a_spec = pl.BlockSpec((tk, tm), lambda i, j, kk: (kk, i))
    else:
        a_spec = pl.BlockSpec((tm, tk), lambda i, j, kk: (i, kk))
    if mode == "nt":
        b_spec = pl.BlockSpec((tn, tk), lambda i, j, kk: (j, kk))
    else:
        b_spec = pl.BlockSpec((tk, tn), lambda i, j, kk: (kk, j))
    in_specs, operands, aliases, body = [a_spec, b_spec], [a, b], {}, compute
    if stack is None:
        out_shape, out_blk, out_idx = base_shape, base_blk, (lambda i, j, kk: base_idx(i, j))
    else:
        layer, buf = stack
        out_shape, out_blk = (DEPTH,) + base_shape, (None,) + base_blk
        out_idx = lambda i, j, kk: (layer,) + base_idx(i, j)
        if buf is not None:
            assert buf.shape == out_shape and buf.dtype == out_dtype
            in_specs.append(pl.BlockSpec(memory_space=pl.ANY))
            operands.append(buf)
            aliases = {2: 0}

            def body(a_ref, b_ref, buf_ref, *rest):
                del buf_ref
                compute(a_ref, b_ref, *rest)

    return pl.pallas_call(
        body, name=name,
        out_shape=jax.ShapeDtypeStruct(out_shape, out_dtype),
        grid=(m // tm, n // tn, nk),
        in_specs=in_specs,
        out_specs=pl.BlockSpec(out_blk, out_idx),
        scratch_shapes=[] if nk == 1 else [pltpu.VMEM((tm, tn), F32)],
        input_output_aliases=aliases,
        compiler_params=_cparams(("parallel", "parallel", "arbitrary")),
    )(*operands)


def _row_map(c0, moves):
    return (lambda j, i: (i, c0 + j)) if moves else (lambda j, i: (i, c0))


def _par_map(c0, moves):
    return (lambda j, i: (0, c0 + j)) if moves else (lambda j, i: (0, c0))


def _in_spec(op, tile):
    _, kind, w, c0, moves = op
    if kind == "row":
        return pl.BlockSpec((tile, w), _row_map(c0, moves))
    return pl.BlockSpec((1, w), _par_map(c0, moves))


ROW_BLOCK_ELEMS = 1 << 18


def _row_tile(t, tile, ops):
    width = max(op[2] for op in ops if op[1] == "row")
    return min(t, max(tile, ROW_BLOCK_ELEMS // width))


def rowwise_fwd(name, fn, t, tile, ncol, ins, outs):
    n_in = len(ins)
    tile = _row_tile(t, tile, ins)

    def body(*refs):
        vals = [r[...].astype(F32) for r in refs[:n_in]]
        res = fn(*vals)
        if not isinstance(res, (tuple, list)):
            res = (res,)
        for r, v in zip(refs[n_in:], res):
            r[...] = v.astype(r.dtype)

    res = pl.pallas_call(
        body, name=name,
        out_shape=[jax.ShapeDtypeStruct((t, w * ncol), dt) for w, dt in outs],
        grid=(ncol, t // tile),
        in_specs=[_in_spec(op, tile) for op in ins],
        out_specs=[pl.BlockSpec((tile, w), _row_map(0, True)) for w, _ in outs],
        compiler_params=_cparams(("arbitrary", "arbitrary")),
    )(*[op[0] for op in ins])
    return res


def rowwise_bwd(name, fn, t, tile, ncol, ins, need, cts, addends=None, row_dtypes=None):
    n_in, n_ct = len(ins), len(cts)
    tile = _row_tile(t, tile, ins)
    addends = addends or {}
    row_dtypes = row_dtypes or {}
    didx = [i for i in range(n_in) if need[i]]
    add_ops = [addends[i] for i in didx if i in addends]
    n_add = len(add_ops)

    def body(*refs):
        in_refs = refs[:n_in]
        ct_refs = refs[n_in:n_in + n_ct]
        add_refs = refs[n_in + n_ct:n_in + n_ct + n_add]
        out_refs = refs[n_in + n_ct + n_add:]
        vals = [r[...].astype(F32) for r in in_refs]

        def g(*dv):
            full = list(vals)
            for i, v in zip(didx, dv):
                full[i] = v
            res = fn(*full)
            return tuple(res) if isinstance(res, (tuple, list)) else (res,)

        _, vjp = jax.vjp(g, *[vals[i] for i in didx])
        grads = vjp(tuple(c[...].astype(F32) for c in ct_refs))
        j, i = pl.program_id(0), pl.program_id(1)
        a = 0
        for o_ref, gv, idx in zip(out_refs, grads, didx):
            _, kind, _, _, moves = ins[idx]
            if kind == "row":
                if idx in addends:
                    gv = gv + add_refs[a][...].astype(F32)
                    a += 1
                o_ref[...] = gv.astype(o_ref.dtype)
            else:
                first = (i == 0) if moves else jnp.logical_and(i == 0, j == 0)

                @pl.when(first)
                def _(o_ref=o_ref, gv=gv):
                    o_ref[...] = gv

                @pl.when(jnp.logical_not(first))
                def _(o_ref=o_ref, gv=gv):
                    o_ref[...] += gv

    out_shape, out_specs = [], []
    for idx in didx:
        _, kind, w, _, moves = ins[idx]
        cols = w * (ncol if moves else 1)
        if kind == "row":
            out_shape.append(jax.ShapeDtypeStruct((t, cols), row_dtypes.get(idx, F32)))
            out_specs.append(pl.BlockSpec((tile, w), _row_map(0, moves)))
        else:
            out_shape.append(jax.ShapeDtypeStruct((1, cols), F32))
            out_specs.append(pl.BlockSpec((1, w), _par_map(0, moves)))
    ops = list(ins) + list(cts) + add_ops
    res = pl.pallas_call(
        body, name=name,
        out_shape=out_shape,
        grid=(ncol, t // tile),
        in_specs=[_in_spec(op, tile) for op in ops],
        out_specs=out_specs,
        compiler_params=_cparams(("arbitrary", "arbitrary")),
    )(*[op[0] for op in ops])
    return res


def f_rmsnorm(x, w):
    return x * lax.rsqrt(jnp.mean(x * x, axis=-1, keepdims=True) + EPS) * w


def f_ssd_post(y, z, w):
    y = y * _silu(z)
    return y * lax.rsqrt(jnp.mean(y * y, axis=-1, keepdims=True) + EPS) * w


def f_gdn_post(o, z, w):
    o = o * lax.rsqrt(jnp.mean(o * o, axis=-1, keepdims=True) + EPS) * w
    return o * _silu(z)


def f_merge(gs, p1, gg, p2):
    return _sigmoid(gs) * p1 + _sigmoid(gg) * p2


def f_swiglu(g, u):
    return _silu(g) * u


def final_loss(name, x, tgt, w, tile=256):
    t, d = x.shape

    def body(x_ref, t_ref, w_ref, loss_ref, dx_ref, dw_ref):
        i = pl.program_id(0)
        xv, tv, wv = x_ref[...], t_ref[...], w_ref[...]

        def g(xx, ww):
            err = f_rmsnorm(xx, ww) - tv
            return 0.5 * jnp.sum(jnp.mean(err * err, axis=-1, keepdims=True), axis=0, keepdims=True)

        val, vjp = jax.vjp(g, xv, wv)
        dx, dw = vjp(jnp.ones((1, 1), F32))
        dx_ref[...] = dx
        lv = jnp.broadcast_to(val, (1, LANE))

        @pl.when(i == 0)
        def _():
            loss_ref[...] = lv
            dw_ref[...] = dw

        @pl.when(i != 0)
        def _():
            loss_ref[...] += lv
            dw_ref[...] += dw

    return pl.pallas_call(
        body, name=name,
        out_shape=[jax.ShapeDtypeStruct((1, LANE), F32), jax.ShapeDtypeStruct((t, d), F32),
                   jax.ShapeDtypeStruct((1, d), F32)],
        grid=(t // tile,),
        in_specs=[pl.BlockSpec((tile, d), lambda i: (i, 0)), pl.BlockSpec((tile, d), lambda i: (i, 0)),
                  pl.BlockSpec((1, d), lambda i: (0, 0))],
        out_specs=[pl.BlockSpec((1, LANE), lambda i: (0, 0)), pl.BlockSpec((tile, d), lambda i: (i, 0)),
                   pl.BlockSpec((1, d), lambda i: (0, 0))],
        compiler_params=_cparams(("arbitrary",)),
    )(x, tgt, w)


CONV_W = 512
HALO = 8


def conv_fwd(name, src, c0, width, w, b, tile=512):
    t = src.shape[0]
    ncol, nrow = width // CONV_W, t // tile
    cb0 = c0 // CONV_W
    hb = tile // HALO

    def body(prev_ref, cur_ref, w_ref, b_ref, o_ref, ext_ref):
        i = pl.program_id(1)
        ext_ref[0:HALO, :] = jnp.where(i == 0, 0.0, prev_ref[...])
        ext_ref[HALO:HALO + tile, :] = cur_ref[...]
        acc = jnp.broadcast_to(b_ref[...], (tile, CONV_W))
        for k in range(CONV_K):
            acc = acc + w_ref[k:k + 1, :] * ext_ref[pl.ds(HALO - (CONV_K - 1) + k, tile), :]
        o_ref[...] = _silu(acc)

    return pl.pallas_call(
        body, name=name,
        out_shape=jax.ShapeDtypeStruct((t, width), F32),
        grid=(ncol, nrow),
        in_specs=[pl.BlockSpec((HALO, CONV_W), lambda j, i: (jnp.maximum(i * hb - 1, 0), cb0 + j)),
                  pl.BlockSpec((tile, CONV_W), lambda j, i: (i, cb0 + j)),
                  pl.BlockSpec((CONV_K, CONV_W), lambda j, i: (0, j)),
                  pl.BlockSpec((1, CONV_W), lambda j, i: (0, j))],
        out_specs=pl.BlockSpec((tile, CONV_W), lambda j, i: (i, j)),
        scratch_shapes=[pltpu.VMEM((tile + HALO, CONV_W), F32)],
        compiler_params=_cparams(("arbitrary", "arbitrary")),
    )(src, src, w, b)


def conv_bwd_pre(name, src, c0, width, w, b, dy, tile=512):
    t = src.shape[0]
    ncol, nrow = width // CONV_W, t // tile
    cb0 = c0 // CONV_W
    hb = tile // HALO

    def body(prev_ref, cur_ref, w_ref, b_ref, dy_ref, dpre_ref, dw_ref, db_ref, ext_ref):
        i = pl.program_id(1)
        ext_ref[0:HALO, :] = jnp.where(i == 0, 0.0, prev_ref[...])
        ext_ref[HALO:HALO + tile, :] = cur_ref[...]
        pre = jnp.broadcast_to(b_ref[...], (tile, CONV_W))
        for k in range(CONV_K):
            pre = pre + w_ref[k:k + 1, :] * ext_ref[pl.ds(HALO - (CONV_K - 1) + k, tile), :]
        s = _sigmoid(pre)
        dpre = dy_ref[...] * (s * (1.0 + pre * (1.0 - s)))
        dpre_ref[...] = dpre

        @pl.when(i == 0)
        def _():
            dw_ref[...] = jnp.zeros_like(dw_ref)
            db_ref[...] = jnp.zeros_like(db_ref)

        for k in range(CONV_K):
            dw_ref[k:k + 1, :] += jnp.sum(dpre * ext_ref[pl.ds(HALO - (CONV_K - 1) + k, tile), :],
                                          axis=0, keepdims=True)
        db_ref[...] += jnp.sum(dpre, axis=0, keepdims=True)

    return pl.pallas_call(
        body, name=name,
        out_shape=[jax.ShapeDtypeStruct((t, width), F32), jax.ShapeDtypeStruct((CONV_K, width), F32),
                   jax.ShapeDtypeStruct((1, width), F32)],
        grid=(ncol, nrow),
        in_specs=[pl.BlockSpec((HALO, CONV_W), lambda j, i: (jnp.maximum(i * hb - 1, 0), cb0 + j)),
                  pl.BlockSpec((tile, CONV_W), lambda j, i: (i, cb0 + j)),
                  pl.BlockSpec((CONV_K, CONV_W), lambda j, i: (0, j)),
                  pl.BlockSpec((1, CONV_W), lambda j, i: (0, j)),
                  pl.BlockSpec((tile, CONV_W), lambda j, i: (i, j))],
        out_specs=[pl.BlockSpec((tile, CONV_W), lambda j, i: (i, j)),
                   pl.BlockSpec((CONV_K, CONV_W), lambda j, i: (0, j)),
                   pl.BlockSpec((1, CONV_W), lambda j, i: (0, j))],
        scratch_shapes=[pltpu.VMEM((tile + HALO, CONV_W), F32)],
        compiler_params=_cparams(("arbitrary", "arbitrary")),
    )(src, src, w, b, dy)


def conv_bwd_in(name, dpre, w, tile=512):
    t, width = dpre.shape
    ncol, nrow = width // CONV_W, t // tile
    hb = tile // HALO
    last_hb = t // HALO - 1

    def body(cur_ref, nxt_ref, w_ref, o_ref, ext_ref):
        i = pl.program_id(1)
        ext_ref[0:tile, :] = cur_ref[...]
        ext_ref[tile:tile + HALO, :] = jnp.where(i == nrow - 1, 0.0, nxt_ref[...])
        acc = jnp.zeros((tile, CONV_W), F32)
        for k in range(CONV_K):
            acc = acc + w_ref[k:k + 1, :] * ext_ref[pl.ds(CONV_K - 1 - k, tile), :]
        o_ref[...] = acc

    return pl.pallas_call(
        body, name=name,
        out_shape=jax.ShapeDtypeStruct((t, width), F32),
        grid=(ncol, nrow),
        in_specs=[pl.BlockSpec((tile, CONV_W), lambda j, i: (i, j)),
                  pl.BlockSpec((HALO, CONV_W), lambda j, i: (jnp.minimum((i + 1) * hb, last_hb), j)),
                  pl.BlockSpec((CONV_K, CONV_W), lambda j, i: (0, j))],
        out_specs=pl.BlockSpec((tile, CONV_W), lambda j, i: (i, j)),
        scratch_shapes=[pltpu.VMEM((tile + HALO, CONV_W), F32)],
        compiler_params=_cparams(("arbitrary", "arbitrary")),
    )(dpre, dpre, w)


def _iota2(q):
    return (lax.broadcasted_iota(jnp.int32, (q, q), 0), lax.broadcasted_iota(jnp.int32, (q, q), 1))


def _lane_pick(blk, idx):
    lane = lax.broadcasted_iota(jnp.int32, (1, LANE), 1)
    return jnp.sum(jnp.where(lane == idx, blk, 0.0), axis=1, keepdims=True)


def _cum(a_col, r, c):
    a_row = jnp.sum(jnp.where(r == c, a_col, 0.0), axis=0, keepdims=True)
    cum_col = jnp.sum(jnp.where(c <= r, a_row, 0.0), axis=1, keepdims=True)
    cum_row = jnp.sum(jnp.where(r <= c, a_col, 0.0), axis=0, keepdims=True)
    return cum_col, cum_row


_SSD_B = SSM_D_INNER
_SSD_C = SSM_D_INNER + 2 * SSM_STATE


def _interleave(gens):
    results = [None] * len(gens)
    live = list(range(len(gens)))
    while live:
        for i in list(live):
            try:
                next(gens[i])
            except StopIteration as stop:
                results[i] = stop.value
                live.remove(i)
    return results


def ssd_chunk(xs, bm, cm, small, dtb, alog, dsk, state, p, cb):
    q = xs.shape[0]
    r, c = _iota2(q)
    lane = lax.broadcasted_iota(jnp.int32, (1, LANE), 1)
    m0 = lane < SSM_HEAD_DIM

    def head(h):
        dt = _softplus(_lane_pick(small, SM_DT + h) + _lane_pick(dtb, h))
        a = dt * (-jnp.exp(_lane_pick(alog, h)))
        cum_col, cum_row = _cum(a, r, c)
        lmat = jnp.exp(jnp.where(r >= c, cum_col - cum_row, NEG_BIG))
        tot = jnp.sum(a, axis=0, keepdims=True)
        return dt, _lane_pick(dsk, h), cum_col, lmat, tot

    dt0, d0, cum0, l0, tot0 = head(2 * p)
    dt1, d1, cum1, l1, tot1 = head(2 * p + 1)
    xdt = xs * jnp.where(m0, dt0, dt1)
    yield
    y_diag = _dot(cb * l0, jnp.where(m0, xdt, 0.0)) + _dot(cb * l1, jnp.where(m0, 0.0, xdt))
    y_off = _dot(cm, state, NT) * jnp.where(m0, jnp.exp(cum0), jnp.exp(cum1))
    yield
    dec = jnp.where(m0, jnp.exp(tot0 - cum0), jnp.exp(tot1 - cum1))
    rowm = lax.broadcasted_iota(jnp.int32, (LANE, 1), 0) < SSM_HEAD_DIM
    new_state = state * jnp.where(rowm, jnp.exp(tot0), jnp.exp(tot1)) + _dot(xdt * dec, bm, TN)
    y = y_diag + y_off + jnp.where(m0, d0, d1) * xs
    return y, new_state


def ssd_group(xs4, bm, cm, small, dtb, alog, dsk, states, pairs):
    cb = _dot(cm, bm, NT)
    res = _interleave([ssd_chunk(xs, bm, cm, small, dtb, alog, dsk, st, p, cb)
                       for xs, st, p in zip(xs4, states, pairs)])
    return tuple(y for y, _ in res), tuple(s for _, s in res)


def tri_inverse(a):
    q = a.shape[0]
    r, c = _iota2(q)
    eye = (r == c).astype(F32)
    diag = (r // 16) == (c // 16)
    bd = jnp.where(diag, a, 0.0)
    off = jnp.where(diag, 0.0, a)
    b2 = _dot_hi(bd, bd)
    d1 = _dot_hi(eye - bd, eye + b2)
    yield
    b4 = _dot_hi(b2, b2)
    yield
    b8 = _dot_hi(b4, b4)
    d2 = _dot_hi(d1, eye + b4)
    yield
    dinv = _dot_hi(d2, eye + b8)
    yield
    n = _dot_hi(dinv, off)
    yield
    n2 = _dot_hi(n, n)
    yield
    m = _dot_hi(eye + n2, dinv)
    yield
    return _dot_hi(eye - n, m)


@jax.custom_vjp
def _solve_with(xinv, a, rhs):
    del a
    return _dot_hi(xinv, rhs)


def _solve_with_fwd(xinv, a, rhs):
    t = _dot_hi(xinv, rhs)
    return t, (xinv, t)


def _solve_with_bwd(res, dt):
    xinv, t = res
    d_rhs = _dot_hi(xinv, dt, TN)
    d_a = -_dot(d_rhs, t, NT)
    return jnp.zeros_like(xinv), d_a, d_rhs


_solve_with.defvjp(_solve_with_fwd, _solve_with_bwd)


_GDN_K = GDN_HEADS * GDN_HEAD
_GDN_V = 2 * GDN_HEADS * GDN_HEAD


def gdn_chunk(qh, kh, vh, small, alog, dtb, state, h, xinv=None):
    q = qh.shape[0]
    r, c = _iota2(q)
    qn = qh * lax.rsqrt(jnp.sum(qh * qh, axis=-1, keepdims=True) + EPS) * (GDN_HEAD ** -0.5)
    kn = kh * lax.rsqrt(jnp.sum(kh * kh, axis=-1, keepdims=True) + EPS)
    beta = _sigmoid(_lane_pick(small, SM_B + h))
    g = -jnp.exp(_lane_pick(alog, h)) * _softplus(_lane_pick(small, SM_A + h) + _lane_pick(dtb, h))
    gc_col, gc_row = _cum(g, r, c)
    decay = jnp.exp(jnp.where(r >= c, gc_col - gc_row, NEG_BIG))
    yield
    kk = _dot(kn, kn, NT)
    qk = _dot(qn, kn, NT) * decay
    amat = jnp.where(r > c, kk * decay * beta, 0.0)
    eg = jnp.exp(gc_col)
    rhs = jnp.concatenate([vh * beta, kn * (beta * eg)], axis=1)
    qs = _dot(qn * eg, state)
    yield
    if xinv is None:
        xinv = yield from tri_inverse(amat)
        t = _dot_hi(xinv, rhs)
    else:
        t = _solve_with(xinv, amat, rhs)
    yield
    u, w = t[:, :GDN_HEAD], t[:, GDN_HEAD:]
    v_new = u - _dot(w, state)
    yield
    o = qs + _dot(qk, v_new)
    tot = jnp.sum(g, axis=0, keepdims=True)
    new_state = state * jnp.exp(tot) + _dot(kn * jnp.exp(tot - gc_col), v_new, TN)
    return o, new_state, xinv


def gdn_heads(qs, ks, vs, small, alog, dtb, states, xinvs=None):
    nh = len(qs)
    res = _interleave([gdn_chunk(qs[h], ks[h], vs[h], small, alog, dtb, states[h], h,
                                 None if xinvs is None else xinvs[h]) for h in range(nh)])
    return tuple(o for o, _, _ in res), tuple(s for _, s, _ in res), tuple(x for _, _, x in res)


def _acc(ref, val, first):
    @pl.when(first)
    def _():
        ref[...] = val

    @pl.when(jnp.logical_not(first))
    def _():
        ref[...] += val


def ssd_scan_fwd(name, xbc, proj, dtb, alog, dsk):
    t = xbc.shape[0]
    nc, npair = t // CHUNK, SSM_HEADS // 2
    small_blk = AL_SMALL // LANE

    def body(xbc_ref, sm_ref, dtb_ref, alog_ref, dsk_ref, y_ref, sin_ref, st_ref):
        ci = pl.program_id(0)

        @pl.when(ci == 0)
        def _():
            st_ref[...] = jnp.zeros_like(st_ref)

        sm, dtb_, alog_, dsk_ = sm_ref[...], dtb_ref[...], alog_ref[...], dsk_ref[...]
        for grp in range(2):
            pairs = list(range(4 * grp, 4 * grp + 4))
            s_in = tuple(st_ref[p] for p in pairs)
            ys, s_new = ssd_group(tuple(xbc_ref[:, p * LANE:(p + 1) * LANE] for p in pairs),
                                  xbc_ref[:, _SSD_B + grp * LANE:_SSD_B + (grp + 1) * LANE],
                                  xbc_ref[:, _SSD_C + grp * LANE:_SSD_C + (grp + 1) * LANE],
                                  sm, dtb_, alog_, dsk_, s_in, pairs)
            for i, p in enumerate(pairs):
                sin_ref[0, p] = s_in[i]
                y_ref[:, p * LANE:(p + 1) * LANE] = ys[i]
                st_ref[p] = s_new[i]

    par = pl.BlockSpec((1, LANE), lambda ci: (0, 0))
    return pl.pallas_call(
        body, name=name,
        out_shape=[jax.ShapeDtypeStruct((t, SSM_D_INNER), F32),
                   jax.ShapeDtypeStruct((nc, npair, LANE, LANE), F32)],
        grid=(nc,),
        in_specs=[pl.BlockSpec((CHUNK, SSM_CONV_DIM), lambda ci: (ci, 0)),
                  pl.BlockSpec((CHUNK, LANE), lambda ci: (ci, small_blk)),
                  par, par, par],
        out_specs=[pl.BlockSpec((CHUNK, SSM_D_INNER), lambda ci: (ci, 0)),
                   pl.BlockSpec((1, npair, LANE, LANE), lambda ci: (ci, 0, 0, 0))],
        scratch_shapes=[pltpu.VMEM((npair, LANE, LANE), F32)],
        compiler_params=_cparams(("arbitrary",)),
    )(xbc, proj, dtb, alog, dsk)


def ssd_scan_bwd(name, xbc, proj, dtb, alog, dsk, s_in, dy):
    t = xbc.shape[0]
    nc, npair = t // CHUNK, SSM_HEADS // 2
    small_blk = AL_SMALL // LANE

    def body(xbc_ref, sm_ref, dtb_ref, alog_ref, dsk_ref, sin_ref, dy_ref,
             dxbc_ref, dsm_ref, ddtb_ref, dalog_ref, ddsk_ref, dst_ref):
        ci = pl.program_id(0)

        @pl.when(ci == 0)
        def _():
            dst_ref[...] = jnp.zeros_like(dst_ref)

        dsm_t = jnp.zeros((CHUNK, LANE), F32)
        dpar = [jnp.zeros((1, LANE), F32)] * 3
        for grp in range(2):
            pairs = list(range(4 * grp, 4 * grp + 4))

            def fn(xs4, bm, cm, sm, dtb_, alog_, dsk_, sts, pairs=pairs):
                return ssd_group(xs4, bm, cm, sm, dtb_, alog_, dsk_, sts, pairs)

            xs4 = tuple(xbc_ref[:, p * LANE:(p + 1) * LANE] for p in pairs)
            _, vjp = jax.vjp(fn, xs4, xbc_ref[:, _SSD_B + grp * LANE:_SSD_B + (grp + 1) * LANE],
                             xbc_ref[:, _SSD_C + grp * LANE:_SSD_C + (grp + 1) * LANE],
                             sm_ref[...], dtb_ref[...], alog_ref[...], dsk_ref[...],
                             tuple(sin_ref[0, p] for p in pairs))
            dxs4, dbm, dcm, dsm, ddtb, dalog, ddsk, dsts = vjp(
                (tuple(dy_ref[:, p * LANE:(p + 1) * LANE] for p in pairs), tuple(dst_ref[p] for p in pairs)))
            for p, dxs, dst in zip(pairs, dxs4, dsts):
                dxbc_ref[:, p * LANE:(p + 1) * LANE] = dxs
                dst_ref[p] = dst
            dxbc_ref[:, _SSD_B + grp * LANE:_SSD_B + (grp + 1) * LANE] = dbm
            dxbc_ref[:, _SSD_C + grp * LANE:_SSD_C + (grp + 1) * LANE] = dcm
            dsm_t = dsm_t + dsm
            dpar = [a + b for a, b in zip(dpar, (ddtb, dalog, ddsk))]
        dsm_ref[...] = dsm_t
        _acc(ddtb_ref, dpar[0], ci == 0)
        _acc(dalog_ref, dpar[1], ci == 0)
        _acc(ddsk_ref, dpar[2], ci == 0)

    par = pl.BlockSpec((1, LANE), lambda ci: (0, 0))
    rev = lambda ci: nc - 1 - ci
    return pl.pallas_call(
        body, name=name,
        out_shape=[jax.ShapeDtypeStruct((t, SSM_CONV_DIM), F32),
                   jax.ShapeDtypeStruct((t, LANE), F32),
                   jax.ShapeDtypeStruct((1, LANE), F32), jax.ShapeDtypeStruct((1, LANE), F32),
                   jax.ShapeDtypeStruct((1, LANE), F32)],
        grid=(nc,),
        in_specs=[pl.BlockSpec((CHUNK, SSM_CONV_DIM), lambda ci: (rev(ci), 0)),
                  pl.BlockSpec((CHUNK, LANE), lambda ci: (rev(ci), small_blk)),
                  par, par, par,
                  pl.BlockSpec((1, npair, LANE, LANE), lambda ci: (rev(ci), 0, 0, 0)),
                  pl.BlockSpec((CHUNK, SSM_D_INNER), lambda ci: (rev(ci), 0))],
        out_specs=[pl.BlockSpec((CHUNK, SSM_CONV_DIM), lambda ci: (rev(ci), 0)),
                   pl.BlockSpec((CHUNK, LANE), lambda ci: (rev(ci), 0)),
                   par, par, par],
        scratch_shapes=[pltpu.VMEM((npair, LANE, LANE), F32)],
        compiler_params=_cparams(("arbitrary",)),
    )(xbc, proj, dtb, alog, dsk, s_in, dy)


def gdn_scan_fwd(name, qkv, proj, alog, dtb):
    t = qkv.shape[0]
    nc, nh = t // CHUNK, GDN_HEADS
    small_blk = AL_SMALL // LANE

    def body(qkv_ref, sm_ref, alog_ref, dtb_ref, o_ref, sin_ref, x_ref, st_ref):
        ci = pl.program_id(0)

        @pl.when(ci == 0)
        def _():
            st_ref[...] = jnp.zeros_like(st_ref)

        s_in = tuple(st_ref[h] for h in range(nh))
        os, s_new, xinvs = gdn_heads(
            tuple(qkv_ref[:, h * LANE:(h + 1) * LANE] for h in range(nh)),
            tuple(qkv_ref[:, _GDN_K + h * LANE:_GDN_K + (h + 1) * LANE] for h in range(nh)),
            tuple(qkv_ref[:, _GDN_V + h * LANE:_GDN_V + (h + 1) * LANE] for h in range(nh)),
            sm_ref[...], alog_ref[...], dtb_ref[...], s_in)
        for h in range(nh):
            sin_ref[0, h] = s_in[h]
            o_ref[:, h * LANE:(h + 1) * LANE] = os[h]
            x_ref[0, h] = xinvs[h]
            st_ref[h] = s_new[h]

    par = pl.BlockSpec((1, LANE), lambda ci: (0, 0))
    return pl.pallas_call(
        body, name=name,
        out_shape=[jax.ShapeDtypeStruct((t, GDN_HEADS * GDN_HEAD), F32),
                   jax.ShapeDtypeStruct((nc, nh, LANE, LANE), F32),
                   jax.ShapeDtypeStruct((nc, nh, CHUNK, CHUNK), F32)],
        grid=(nc,),
        in_specs=[pl.BlockSpec((CHUNK, GDN_QKV_DIM), lambda ci: (ci, 0)),
                  pl.BlockSpec((CHUNK, LANE), lambda ci: (ci, small_blk)),
                  par, par],
        out_specs=[pl.BlockSpec((CHUNK, GDN_HEADS * GDN_HEAD), lambda ci: (ci, 0)),
                   pl.BlockSpec((1, nh, LANE, LANE), lambda ci: (ci, 0, 0, 0)),
                   pl.BlockSpec((1, nh, CHUNK, CHUNK), lambda ci: (ci, 0, 0, 0))],
        scratch_shapes=[pltpu.VMEM((nh, LANE, LANE), F32)],
        compiler_params=_cparams(("arbitrary",)),
    )(qkv, proj, alog, dtb)


def gdn_scan_bwd(name, qkv, proj, alog, dtb, s_in, xinv, do, dsm_in):
    t = qkv.shape[0]
    nc, nh = t // CHUNK, GDN_HEADS
    small_blk = AL_SMALL // LANE

    def body(qkv_ref, sm_ref, alog_ref, dtb_ref, sin_ref, x_ref, do_ref, dsmi_ref,
             dqkv_ref, dsm_ref, dalog_ref, ddtb_ref, dst_ref):
        ci = pl.program_id(0)

        @pl.when(ci == 0)
        def _():
            dst_ref[...] = jnp.zeros_like(dst_ref)

        xis = tuple(x_ref[0, h] for h in range(nh))

        def fn(qs, ks, vs, sm, alog_, dtb_, sts):
            os, s_new, _ = gdn_heads(qs, ks, vs, sm, alog_, dtb_, sts, xinvs=xis)
            return os, s_new

        _, vjp = jax.vjp(fn, tuple(qkv_ref[:, h * LANE:(h + 1) * LANE] for h in range(nh)),
                         tuple(qkv_ref[:, _GDN_K + h * LANE:_GDN_K + (h + 1) * LANE] for h in range(nh)),
                         tuple(qkv_ref[:, _GDN_V + h * LANE:_GDN_V + (h + 1) * LANE] for h in range(nh)),
                         sm_ref[...], alog_ref[...], dtb_ref[...], tuple(sin_ref[0, h] for h in range(nh)))
        dqs, dks, dvs, dsm, dalog, ddtb, dsts = vjp(
            (tuple(do_ref[:, h * LANE:(h + 1) * LANE] for h in range(nh)), tuple(dst_ref[h] for h in range(nh))))
        for h in range(nh):
            dqkv_ref[:, h * LANE:(h + 1) * LANE] = dqs[h]
            dqkv_ref[:, _GDN_K + h * LANE:_GDN_K + (h + 1) * LANE] = dks[h]
            dqkv_ref[:, _GDN_V + h * LANE:_GDN_V + (h + 1) * LANE] = dvs[h]
            dst_ref[h] = dsts[h]
        dsm_ref[...] = dsmi_ref[...] + dsm
        _acc(dalog_ref, dalog, ci == 0)
        _acc(ddtb_ref, ddtb, ci == 0)

    par = pl.BlockSpec((1, LANE), lambda ci: (0, 0))
    rev = lambda ci: nc - 1 - ci
    return pl.pallas_call(
        body, name=name,
        out_shape=[jax.ShapeDtypeStruct((t, GDN_QKV_DIM), F32), jax.ShapeDtypeStruct((t, LANE), F32),
                   jax.ShapeDtypeStruct((1, LANE), F32), jax.ShapeDtypeStruct((1, LANE), F32)],
        grid=(nc,),
        in_specs=[pl.BlockSpec((CHUNK, GDN_QKV_DIM), lambda ci: (rev(ci), 0)),
                  pl.BlockSpec((CHUNK, LANE), lambda ci: (rev(ci), small_blk)),
                  par, par,
                  pl.BlockSpec((1, nh, LANE, LANE), lambda ci: (rev(ci), 0, 0, 0)),
                  pl.BlockSpec((1, nh, CHUNK, CHUNK), lambda ci: (rev(ci), 0, 0, 0)),
                  pl.BlockSpec((CHUNK, GDN_HEADS * GDN_HEAD), lambda ci: (rev(ci), 0)),
                  pl.BlockSpec((CHUNK, LANE), lambda ci: (rev(ci), 0))],
        out_specs=[pl.BlockSpec((CHUNK, GDN_QKV_DIM), lambda ci: (rev(ci), 0)),
                   pl.BlockSpec((CHUNK, LANE), lambda ci: (rev(ci), 0)),
                   par, par],
        scratch_shapes=[pltpu.VMEM((nh, LANE, LANE), F32)],
        compiler_params=_cparams(("arbitrary",)),
    )(qkv, proj, alog, dtb, s_in, xinv, do, dsm_in)


def _row(arr, w, c0=0, moves=False):
    return (arr, "row", w, c0, moves)


def _par(arr, w, c0=0, moves=False):
    return (arr, "par", w, c0, moves)


def matmul_add(name, a, b, res):
    (m, k), (_, n) = a.shape, b.shape
    tm = _pick(m, (512, 256, 128))
    tn = _pick(n, (512, 256, 128))
    tk = _pick_k(k, max(tm, tn), a, b)
    nk = k // tk

    def body_acc(a_ref, b_ref, r_ref, o_ref, acc_ref):
        kk = pl.program_id(2)

        @pl.when(kk == 0)
        def _():
            acc_ref[...] = r_ref[...]

        acc_ref[...] += _dot(a_ref[...], b_ref[...])

        @pl.when(kk == nk - 1)
        def _():
            o_ref[...] = acc_ref[...]

    def body_one(a_ref, b_ref, r_ref, o_ref):
        o_ref[...] = r_ref[...] + _dot(a_ref[...], b_ref[...])

    body = body_one if nk == 1 else body_acc
    return pl.pallas_call(
        body, name=name,
        out_shape=jax.ShapeDtypeStruct((m, n), F32),
        grid=(m // tm, n // tn, nk),
        in_specs=[pl.BlockSpec((tm, tk), lambda i, j, kk: (i, kk)),
                  pl.BlockSpec((tk, tn), lambda i, j, kk: (kk, j)),
                  pl.BlockSpec((tm, tn), lambda i, j, kk: (i, j))],
        out_specs=pl.BlockSpec((tm, tn), lambda i, j, kk: (i, j)),
        scratch_shapes=[] if nk == 1 else [pltpu.VMEM((tm, tn), F32)],
        compiler_params=_cparams(("parallel", "parallel", "arbitrary")),
    )(a, b, res)


def layer_fwd(l, x, w):
    t = x.shape[0]
    rt = min(256, t)
    s = {"x": x}
    s["h"] = rowwise_fwd(f"norm_mix_l{l}", f_rmsnorm, t, rt, 1,
                         [_row(x, D_MODEL), _par(w["norm_mix_w"], D_MODEL)], [(D_MODEL, BF16)])[0]
    s["proj"] = matmul(f"in_proj_l{l}", s["h"], w["w_in"], "nn")
    s["xbc"] = conv_fwd(f"ssm_conv_l{l}", s["proj"], AL_XBC, SSM_CONV_DIM, w["ssm_conv_w"], w["ssm_conv_b"],
                        tile=min(512, t))
    s["qkv"] = conv_fwd(f"gdn_conv_l{l}", s["proj"], AL_QKV, GDN_QKV_DIM, w["gdn_conv_w"], w["gdn_conv_b"],
                        tile=min(512, t))
    s["y_scan"], s["ssd_sin"] = ssd_scan_fwd(f"ssd_scan_l{l}", s["xbc"], s["proj"], w["ssm_dt_bias"],
                                             w["ssm_a_log"], w["ssm_d"])
    s["o_scan"], s["gdn_sin"], s["gdn_x"] = gdn_scan_fwd(f"gdn_scan_l{l}", s["qkv"], s["proj"],
                                                         w["gdn_a_log"], w["gdn_dt_bias"])
    s["y_ssm"] = rowwise_fwd(f"ssd_post_l{l}", f_ssd_post, t, rt, 2,
                             [_row(s["y_scan"], 512, 0, True), _row(s["proj"], 512, AL_Z // 512, True),
                              _par(w["ssm_norm_w"], 512, 0, True)], [(512, BF16)])[0]
    s["y_gdn"] = rowwise_fwd(f"gdn_post_l{l}", f_gdn_post, t, rt, GDN_HEADS,
                             [_row(s["o_scan"], LANE, 0, True), _row(s["proj"], LANE, AL_GZ // LANE, True),
                              _par(w["gdn_norm_w"], LANE)], [(LANE, BF16)])[0]
    s["p1"] = matmul(f"proj_ssm_l{l}", s["y_ssm"], w["w_proj_ssm"], "nn")
    s["p2"] = matmul(f"proj_gdn_l{l}", s["y_gdn"], w["w_proj_gdn"], "nn")
    s["merged"] = rowwise_fwd(f"merge_l{l}", f_merge, t, rt, 2,
                              [_row(s["proj"], 512, AL_GS // 512, True), _row(s["p1"], 512, 0, True),
                               _row(s["proj"], 512, AL_GG // 512, True), _row(s["p2"], 512, 0, True)],
                              [(512, BF16)])[0]
    s["x1"] = matmul_add(f"out_proj_l{l}", s["merged"], w["w_out"], x)
    s["h2"] = rowwise_fwd(f"norm_ffn_l{l}", f_rmsnorm, t, rt, 1,
                          [_row(s["x1"], D_MODEL), _par(w["norm_ffn_w"], D_MODEL)], [(D_MODEL, BF16)])[0]
    s["gu"] = matmul(f"ffn_in_l{l}", s["h2"], w["w_ffn_in"], "nn")
    s["act"] = rowwise_fwd(f"swiglu_l{l}", f_swiglu, t, rt, FFN_HIDDEN // 256,
                           [_row(s["gu"], 256, 0, True), _row(s["gu"], 256, FFN_HIDDEN // 256, True)],
                           [(256, BF16)])[0]
    x2 = matmul_add(f"ffn_down_l{l}", s["act"], w["w_ffn_down"], s["x1"])
    return x2, s


IN_SHARD = IN_DIM // 4
IN_SHARD_PAD = 2304


def _shard_padded(pieces):
    t = pieces[0].shape[0]
    zeros = jnp.zeros((t, IN_SHARD_PAD - IN_SHARD), pieces[0].dtype)
    out, start = [], 0
    bounds = [(j * IN_SHARD, (j + 1) * IN_SHARD) for j in range(4)]
    offs = []
    for p in pieces:
        offs.append((start, start + p.shape[1], p))
        start += p.shape[1]
    assert start == IN_DIM
    for lo, hi in bounds:
        for a, b, p in offs:
            s0, s1 = max(a, lo), min(b, hi)
            if s0 < s1:
                out.append(p if (s0, s1) == (a, b) else p[:, s0 - a:s1 - a])
        out.append(zeros)
    return jnp.concatenate(out, axis=1)


def layer_bwd(l, dx2, w, s, gbuf):
    t = dx2.shape[0]
    rt = min(256, t)
    ct = min(512, t)
    g = {}
    dact = matmul(f"ffn_down_dx_l{l}", dx2, w["w_ffn_down"], "nt")
    g["w_ffn_down"] = matmul(f"ffn_down_dw_l{l}", s["act"], dx2, "tn", stack=(l, gbuf.get("w_ffn_down")))
    nf = FFN_HIDDEN // 256
    dgate, dup = rowwise_bwd(f"swiglu_bwd_l{l}", f_swiglu, t, rt, nf,
                             [_row(s["gu"], 256, 0, True), _row(s["gu"], 256, nf, True)], [True, True],
                             [_row(dact, 256, 0, True)])
    dgu = jnp.concatenate([dgate, dup], axis=1)
    dh2 = matmul(f"ffn_in_dx_l{l}", dgu, w["w_ffn_in"], "nt")
    g["w_ffn_in"] = matmul(f"ffn_in_dw_l{l}", s["h2"], dgu, "tn", chip_major=True,
                           stack=(l, gbuf.get("w_ffn_in")))
    dx1, g["norm_ffn_w"] = rowwise_bwd(f"norm_ffn_bwd_l{l}", f_rmsnorm, t, rt, 1,
                                       [_row(s["x1"], D_MODEL), _par(w["norm_ffn_w"], D_MODEL)], [True, True],
                                       [_row(dh2, D_MODEL)], addends={0: _row(dx2, D_MODEL)})
    dmerged = matmul(f"out_proj_dx_l{l}", dx1, w["w_out"], "nt")
    g["w_out"] = matmul(f"out_proj_dw_l{l}", s["merged"], dx1, "tn", stack=(l, gbuf.get("w_out")))
    dgs, dp1, dgg, dp2 = rowwise_bwd(
        f"merge_bwd_l{l}", f_merge, t, rt, 2,
        [_row(s["proj"], 512, AL_GS // 512, True), _row(s["p1"], 512, 0, True),
         _row(s["proj"], 512, AL_GG // 512, True), _row(s["p2"], 512, 0, True)], [True] * 4,
        [_row(dmerged, 512, 0, True)])
    dy_ssm = matmul(f"proj_ssm_dx_l{l}", dp1, w["w_proj_ssm"], "nt")
    g["w_proj_ssm"] = matmul(f"proj_ssm_dw_l{l}", s["y_ssm"], dp1, "tn", stack=(l, gbuf.get("w_proj_ssm")))
    dy_gdn = matmul(f"proj_gdn_dx_l{l}", dp2, w["w_proj_gdn"], "nt")
    g["w_proj_gdn"] = matmul(f"proj_gdn_dw_l{l}", s["y_gdn"], dp2, "tn", stack=(l, gbuf.get("w_proj_gdn")))
    dy_scan, dz, g["ssm_norm_w"] = rowwise_bwd(
        f"ssd_post_bwd_l{l}", f_ssd_post, t, rt, 2,
        [_row(s["y_scan"], 512, 0, True), _row(s["proj"], 512, AL_Z // 512, True),
         _par(w["ssm_norm_w"], 512, 0, True)], [True] * 3, [_row(dy_ssm, 512, 0, True)])
    dxbc_act, dsm, g["ssm_dt_bias"], g["ssm_a_log"], g["ssm_d"] = ssd_scan_bwd(
        f"ssd_scan_bwd_l{l}", s["xbc"], s["proj"], w["ssm_dt_bias"], w["ssm_a_log"], w["ssm_d"],
        s["ssd_sin"], dy_scan)
    dpre, g["ssm_conv_w"], g["ssm_conv_b"] = conv_bwd_pre(
        f"ssm_conv_bwd_l{l}", s["proj"], AL_XBC, SSM_CONV_DIM, w["ssm_conv_w"], w["ssm_conv_b"], dxbc_act, tile=ct)
    dxbc = conv_bwd_in(f"ssm_conv_din_l{l}", dpre, w["ssm_conv_w"], tile=ct)
    do_scan, dgz, g["gdn_norm_w"] = rowwise_bwd(
        f"gdn_post_bwd_l{l}", f_gdn_post, t, rt, GDN_HEADS,
        [_row(s["o_scan"], LANE, 0, True), _row(s["proj"], LANE, AL_GZ // LANE, True),
         _par(w["gdn_norm_w"], LANE)], [True] * 3, [_row(dy_gdn, LANE, 0, True)])
    dqkv_act, dsm, g["gdn_a_log"], g["gdn_dt_bias"] = gdn_scan_bwd(
        f"gdn_scan_bwd_l{l}", s["qkv"], s["proj"], w["gdn_a_log"], w["gdn_dt_bias"], s["gdn_sin"],
        s["gdn_x"], do_scan, dsm)
    dpre, g["gdn_conv_w"], _ = conv_bwd_pre(
        f"gdn_conv_bwd_l{l}", s["proj"], AL_QKV, GDN_QKV_DIM, w["gdn_conv_w"], w["gdn_conv_b"], dqkv_act, tile=ct)
    dqkv = conv_bwd_in(f"gdn_conv_din_l{l}", dpre, w["gdn_conv_w"], tile=ct)
    dproj = _shard_padded([dz, dxbc, dsm[:, SM_DT:SM_DT + SSM_HEADS], dqkv, dgz,
                           dsm[:, SM_A:SM_A + 2 * GDN_HEADS], dgs, dgg])
    dh = matmul(f"in_proj_dx_l{l}", dproj, w["w_in_sp"], "nt")
    g["w_in"] = matmul(f"in_proj_dw_l{l}", s["h"], dproj, "tn", chip_major=True, stack=(l, gbuf.get("w_in")))
    dx0, g["norm_mix_w"] = rowwise_bwd(f"norm_mix_bwd_l{l}", f_rmsnorm, t, rt, 1,
                                       [_row(s["x"], D_MODEL), _par(w["norm_mix_w"], D_MODEL)], [True, True],
                                       [_row(dh, D_MODEL)], addends={0: _row(dx1, D_MODEL)})
    return dx0, g


def _align_w_in(w):
    pad = jnp.zeros((w.shape[0], AL_DIM - AL_SMALL - 32), w.dtype)
    return jnp.concatenate([w[:, 0:2560], w[:, 2576:6672], w[:, 6688:8736],
                            w[:, 2560:2576], w[:, 6672:6688], pad], axis=1)


def _pad_lane(v):
    return jnp.zeros((1, LANE), F32).at[0, :v.shape[0]].set(v)


def local_step(x, target, full):
    ws = []
    for l in range(DEPTH):
        ws.append({
            "norm_mix_w": full["norm_mix_w"][l][None], "w_in": _align_w_in(full["w_in"][l]),
            "w_in_sp": full["w_in_sp"][l],
            "ssm_conv_w": full["ssm_conv_w"][l], "ssm_conv_b": full["ssm_conv_b"][l][None],
            "ssm_dt_bias": _pad_lane(full["ssm_dt_bias"][l]), "ssm_a_log": _pad_lane(full["ssm_a_log"][l]),
            "ssm_d": _pad_lane(full["ssm_d"][l]), "ssm_norm_w": full["ssm_norm_w"][l][None],
            "gdn_conv_w": full["gdn_conv_w"][l], "gdn_conv_b": jnp.zeros((1, GDN_QKV_DIM), F32),
            "gdn_a_log": _pad_lane(full["gdn_a_log"][l]), "gdn_dt_bias": _pad_lane(full["gdn_dt_bias"][l]),
            "gdn_norm_w": full["gdn_norm_w"][l][None],
            "w_proj_ssm": full["w_proj_ssm"][l], "w_proj_gdn": full["w_proj_gdn"][l], "w_out": full["w_out"][l],
            "norm_ffn_w": full["norm_ffn_w"][l][None], "w_ffn_in": full["w_ffn_in"][l],
            "w_ffn_down": full["w_ffn_down"][l],
        })
    saved = []
    h = x
    for l in range(DEPTH):
        h, s = layer_fwd(l, h, ws[l])
        saved.append(s)
    loss, dx, g_final = final_loss("final_loss", h, target, full["final_norm_w"][None], tile=min(256, x.shape[0]))
    per_layer = [None] * DEPTH
    gbuf = {}
    for l in reversed(range(DEPTH)):
        dx, per_layer[l] = layer_bwd(l, dx, ws[l], saved[l], gbuf)
        gbuf = {n: per_layer[l][n] for n, _ in BIG}
    grads = {"final_norm_w": g_final[0], **gbuf}
    for name in per_layer[0]:
        if name in gbuf:
            continue
        rows = []
        for l in range(DEPTH):
            gl = per_layer[l][name]
            if name in ("ssm_dt_bias", "ssm_a_log", "ssm_d"):
                gl = gl[0, :SSM_HEADS]
            elif name in ("gdn_a_log", "gdn_dt_bias"):
                gl = gl[0, :GDN_HEADS]
            elif name in ("norm_mix_w", "ssm_conv_b", "ssm_norm_w", "gdn_norm_w", "norm_ffn_w"):
                gl = gl[0]
            rows.append(gl)
        grads[name] = jnp.stack(rows)
    return loss, dx, grads


MESH = pl.DeviceIdType.MESH
HBM = pl.BlockSpec(memory_space=pltpu.HBM)
N_CHIPS = 4
N_DEV = 8


def _pos():
    return lax.axis_index("x"), lax.axis_index("y"), lax.axis_index("c")


def _rcopy(src, dst, send_sem, recv_sem, dev):
    return pltpu.make_async_remote_copy(src_ref=src, dst_ref=dst, send_sem=send_sem, recv_sem=recv_sem,
                                        device_id=dev, device_id_type=MESH)


RELATIONS = (2, 1, 3)


def _related_chip(x, y, mask):
    return (1 - x if mask & 2 else x, 1 - y if mask & 1 else y)


def weights_gather(name, bufs):
    n = len(bufs)

    def body(*refs):
        outs, send_sems, recv_sems = refs[n:2 * n], refs[2 * n], refs[2 * n + 1]
        x, y, c = _pos()
        sib = (x, y, 1 - c)
        sends = []
        for i, a in enumerate(outs):
            for k, m in enumerate(RELATIONS):
                px, py = _related_chip(x, y, m)
                cp = _rcopy(a.at[0, c], a.at[m, c], send_sems.at[6 * i + k], recv_sems.at[6 * i + k], (px, py, c))
                cp.start()
                sends.append(cp)
        for i, a in enumerate(outs):
            for k, m in enumerate(RELATIONS):
                px, py = _related_chip(x, y, m)
                _rcopy(a.at[0, c], a.at[m, c], send_sems.at[6 * i + k], recv_sems.at[6 * i + k],
                       (px, py, c)).wait_recv()
                fw = _rcopy(a.at[m, c], a.at[m, c], send_sems.at[6 * i + 3 + k], recv_sems.at[6 * i + 3 + k], sib)
                fw.start()
                sends.append(fw)
        for i, a in enumerate(outs):
            for k, m in enumerate(RELATIONS):
                _rcopy(a.at[m, 1 - c], a.at[m, 1 - c], send_sems.at[6 * i + 3 + k], recv_sems.at[6 * i + 3 + k],
                       sib).wait_recv()
        for cp in sends:
            cp.wait_send()

    return pl.pallas_call(
        body, name=name, out_shape=[jax.ShapeDtypeStruct(b.shape, b.dtype) for b in bufs],
        in_specs=[HBM] * n, out_specs=[HBM] * n,
        input_output_aliases={i: i for i in range(n)},
        scratch_shapes=[pltpu.SemaphoreType.DMA((6 * n,)), pltpu.SemaphoreType.DMA((6 * n,))],
    )(*bufs)


def pair_swap(name, gs):
    n = len(gs)

    def body(*refs):
        srcs, outs, send_sems, recv_sems = refs[:n], refs[n:2 * n], refs[2 * n], refs[2 * n + 1]
        x, y, c = _pos()
        cps = [_rcopy(s.at[1 - c], o, send_sems.at[i], recv_sems.at[i], (x, y, 1 - c))
               for i, (s, o) in enumerate(zip(srcs, outs))]
        for cp in cps:
            cp.start()
        for cp in cps:
            cp.wait()

    return pl.pallas_call(
        body, name=name, out_shape=[jax.ShapeDtypeStruct(g.shape[1:], g.dtype) for g in gs],
        in_specs=[HBM] * n, out_specs=[HBM] * n,
        scratch_shapes=[pltpu.SemaphoreType.DMA((n,)), pltpu.SemaphoreType.DMA((n,))],
    )(*gs)


def chip_scatter(name, ss):
    n = len(ss)

    def body(*refs):
        srcs, outs, send_sems, recv_sems = refs[:n], refs[n:2 * n], refs[2 * n], refs[2 * n + 1]
        x, y, c = _pos()
        sends = []
        for i, (s, o) in enumerate(zip(srcs, outs)):
            for k, m in enumerate(RELATIONS):
                px, py = _related_chip(x, y, m)
                cp = _rcopy(s.at[2 * px + py], o.at[k], send_sems.at[3 * i + k], recv_sems.at[3 * i + k],
                            (px, py, c))
                cp.start()
                sends.append(cp)
        for cp in sends:
            cp.wait()

    return pl.pallas_call(
        body, name=name, out_shape=[jax.ShapeDtypeStruct((3,) + s.shape[1:], s.dtype) for s in ss],
        in_specs=[HBM] * n, out_specs=[HBM] * n,
        scratch_shapes=[pltpu.SemaphoreType.DMA((3 * n,)), pltpu.SemaphoreType.DMA((3 * n,))],
    )(*ss)


def pair_share(name, bufs):
    n = len(bufs)

    def body(*refs):
        outs, send_sems, recv_sems = refs[n:2 * n], refs[2 * n], refs[2 * n + 1]
        x, y, c = _pos()
        sends = []
        for i, o in enumerate(outs):
            cp = _rcopy(o.at[c], o.at[c], send_sems.at[i], recv_sems.at[i], (x, y, 1 - c))
            cp.start()
            sends.append(cp)
        for i, o in enumerate(outs):
            _rcopy(o.at[1 - c], o.at[1 - c], send_sems.at[i], recv_sems.at[i], (x, y, 1 - c)).wait_recv()
        for cp in sends:
            cp.wait_send()

    return pl.pallas_call(
        body, name=name, out_shape=[jax.ShapeDtypeStruct(b.shape, b.dtype) for b in bufs],
        in_specs=[HBM] * n, out_specs=[HBM] * n,
        input_output_aliases={i: i for i in range(n)},
        scratch_shapes=[pltpu.SemaphoreType.DMA((n,)), pltpu.SemaphoreType.DMA((n,))],
    )(*bufs)


def all_allgather(name, buf):
    r, cd = buf.shape

    def body(src, out, send_sems, recv_sems, lsem):
        x, y, c = _pos()
        me = 4 * x + 2 * y + c
        local = pltpu.make_async_copy(src, out.at[me], lsem)
        local.start()

        def peer(mask):
            px = 1 - x if mask & 4 else x
            py = 1 - y if mask & 2 else y
            pc = 1 - c if mask & 1 else c
            return px, py, pc

        sends = []
        for mask in range(1, N_DEV):
            cp = _rcopy(src, out.at[me], send_sems.at[mask - 1], recv_sems.at[mask - 1], peer(mask))
            cp.start()
            sends.append(cp)
        for mask in range(1, N_DEV):
            px, py, pc = peer(mask)
            _rcopy(src, out.at[4 * px + 2 * py + pc], send_sems.at[mask - 1], recv_sems.at[mask - 1],
                   (px, py, pc)).wait_recv()
        for cp in sends:
            cp.wait_send()
        local.wait()

    return pl.pallas_call(
        body, name=name, out_shape=jax.ShapeDtypeStruct((N_DEV, r, cd), buf.dtype),
        in_specs=[HBM], out_specs=HBM,
        scratch_shapes=[pltpu.SemaphoreType.DMA((N_DEV - 1,)), pltpu.SemaphoreType.DMA((N_DEV - 1,)),
                        pltpu.SemaphoreType.DMA(())],
    )(buf)


ELEMENTWISE_BLOCK_BYTES = 2 << 20


def _row_block(rows, cols):
    for cand in (1024, 512, 256, 128, 64, 32, 16):
        if rows % cand == 0 and cand * cols * 4 <= ELEMENTWISE_BLOCK_BYTES:
            return cand
    return rows


def chip_sum(name, s, r, me, c):
    _, a, b = s.shape
    tr = _row_block(a, b)

    def body(idx_ref, s_ref, r_ref, o_ref):
        del idx_ref
        acc = s_ref[...].astype(F32)
        for k in range(3):
            acc = acc + r_ref[k].astype(F32)
        o_ref[...] = acc

    return pl.pallas_call(
        body, name=name, out_shape=jax.ShapeDtypeStruct((2, a, b), F32),
        grid_spec=pltpu.PrefetchScalarGridSpec(
            num_scalar_prefetch=1, grid=(a // tr,),
            in_specs=[pl.BlockSpec((None, tr, b), lambda i, idx: (idx[0], i, 0)),
                      pl.BlockSpec((3, tr, b), lambda i, idx: (0, i, 0))],
            out_specs=pl.BlockSpec((None, tr, b), lambda i, idx: (idx[1], i, 0))),
        compiler_params=_cparams(("arbitrary",)),
    )(jnp.stack([me, c]).astype(jnp.int32), s, r)


def pair_add(name, p, recv, c):
    _, nj, rh, cd = p.shape
    tr = _row_block(rh, cd)

    def body(c_ref, p_ref, r_ref, o_ref):
        del c_ref
        o_ref[...] = (p_ref[0] + r_ref[...]).astype(o_ref.dtype)

    return pl.pallas_call(
        body, name=name, out_shape=jax.ShapeDtypeStruct((nj, rh, cd), BF16),
        grid_spec=pltpu.PrefetchScalarGridSpec(
            num_scalar_prefetch=1, grid=(nj, rh // tr),
            in_specs=[pl.BlockSpec((1, 1, tr, cd), lambda j, i, c_ref: (c_ref[0], j, i, 0)),
                      pl.BlockSpec((1, tr, cd), lambda j, i, c_ref: (j, i, 0))],
            out_specs=pl.BlockSpec((1, tr, cd), lambda j, i, c_ref: (j, i, 0))),
        compiler_params=_cparams(("arbitrary", "arbitrary")),
    )(jnp.reshape(c, (1,)).astype(jnp.int32), p, recv)


def slab_sum(name, a):
    n, r, cd = a.shape
    tr = _pick(r, (256, 128, 64, 32, 16, 8))

    def body(a_ref, o_ref):
        acc = a_ref[0].astype(F32)
        for j in range(1, n):
            acc = acc + a_ref[j].astype(F32)
        o_ref[...] = acc

    return pl.pallas_call(
        body, name=name, out_shape=jax.ShapeDtypeStruct((r, cd), F32),
        grid=(r // tr,),
        in_specs=[pl.BlockSpec((n, tr, cd), lambda i: (0, i, 0))],
        out_specs=pl.BlockSpec((tr, cd), lambda i: (i, 0)),
        compiler_params=_cparams(("arbitrary",)),
    )(a)


ADAM_C1 = 1.0 - ADAM_B1 ** ADAM_STEP
ADAM_C2 = 1.0 - ADAM_B2 ** ADAM_STEP


def adamw(name, w, g, m, v):
    r, cd = w.shape
    tr = r
    for cand in (512, 256, 128, 64, 32, 16, 8):
        if r % cand == 0 and cand * cd * 4 <= (1 << 20):
            tr = cand
            break

    def body(w_ref, g_ref, m_ref, v_ref, d_ref, nm_ref, nv_ref):
        gv = g_ref[...]
        nm = ADAM_B1 * m_ref[...] + (1.0 - ADAM_B1) * gv
        nv = ADAM_B2 * v_ref[...] + (1.0 - ADAM_B2) * (gv * gv)
        m_hat = nm / ADAM_C1
        v_hat = nv / ADAM_C2
        d_ref[...] = -ADAM_LR * (m_hat / (jnp.sqrt(v_hat) + ADAM_EPS) + ADAM_WD * w_ref[...])
        nm_ref[...] = nm
        nv_ref[...] = nv

    spec = pl.BlockSpec((tr, cd), lambda i: (i, 0))
    sd = jax.ShapeDtypeStruct((r, cd), F32)
    return pl.pallas_call(
        body, name=name, out_shape=[sd, sd, sd], grid=(r // tr,),
        in_specs=[spec] * 4, out_specs=[spec] * 3,
        compiler_params=_cparams(("arbitrary",)),
    )(w, g, m, v)


WEIGHTS = ("norm_mix_w", "w_in", "ssm_conv_w", "ssm_conv_b", "ssm_dt_bias", "ssm_a_log", "ssm_d", "ssm_norm_w",
           "gdn_conv_w", "gdn_a_log", "gdn_dt_bias", "gdn_norm_w", "w_proj_ssm", "w_proj_gdn", "w_out",
           "norm_ffn_w", "w_ffn_in", "w_ffn_down", "final_norm_w")
BIG = (("w_in", 2), ("w_proj_ssm", 1), ("w_proj_gdn", 1), ("w_out", 1), ("w_ffn_in", 2), ("w_ffn_down", 1))
CONVW = (("ssm_conv_w", 2), ("gdn_conv_w", 2))
SHARDED = BIG + CONVW
SMALL = tuple(n for n in WEIGHTS if n not in dict(SHARDED))


def _unpack(buf, shapes, lead=()):
    flat = buf.reshape(lead + (-1,))
    out, o = [], 0
    for shp in shapes:
        n = math.prod(shp)
        out.append(flat[..., o:o + n].reshape(lead + tuple(shp)))
        o += n
    return out


def _pack_rows(arrs, lead=()):
    nl = len(lead)
    flat = jnp.concatenate([a.reshape(lead + (-1,)) for a in arrs], axis=nl)
    n = flat.shape[nl]
    rows = -(-n // (8 * LANE)) * 8
    flat = jnp.pad(flat, [(0, 0)] * nl + [(0, rows * LANE - n)])
    return flat.reshape(lead + (rows, LANE))


def _slot_buffer(shard):
    return jnp.pad(shard[None], [(0, N_CHIPS - 1)] + [(0, 0)] * shard.ndim)


def kernel(x, norm_mix_w, w_in, ssm_conv_w, ssm_conv_b, ssm_dt_bias, ssm_a_log, ssm_d, ssm_norm_w, gdn_conv_w, gdn_a_log, gdn_dt_bias, gdn_norm_w, w_proj_ssm, w_proj_gdn, w_out, norm_ffn_w, w_ffn_in, w_ffn_down, final_norm_w, loss_target, m_norm_mix_w, m_w_in, m_ssm_conv_w, m_ssm_conv_b, m_ssm_dt_bias, m_ssm_a_log, m_ssm_d, m_ssm_norm_w, m_gdn_conv_w, m_gdn_a_log, m_gdn_dt_bias, m_gdn_norm_w, m_w_proj_ssm, m_w_proj_gdn, m_w_out, m_norm_ffn_w, m_w_ffn_in, m_w_ffn_down, m_final_norm_w, v_norm_mix_w, v_w_in, v_ssm_conv_w, v_ssm_conv_b, v_ssm_dt_bias, v_ssm_a_log, v_ssm_d, v_ssm_norm_w, v_gdn_conv_w, v_gdn_a_log, v_gdn_dt_bias, v_gdn_norm_w, v_w_proj_ssm, v_w_proj_gdn, v_w_out, v_norm_ffn_w, v_w_ffn_in, v_w_ffn_down, v_final_norm_w):
    wl = (norm_mix_w, w_in, ssm_conv_w, ssm_conv_b, ssm_dt_bias, ssm_a_log, ssm_d, ssm_norm_w, gdn_conv_w,
          gdn_a_log, gdn_dt_bias, gdn_norm_w, w_proj_ssm, w_proj_gdn, w_out, norm_ffn_w, w_ffn_in, w_ffn_down,
          final_norm_w)
    ml = (m_norm_mix_w, m_w_in, m_ssm_conv_w, m_ssm_conv_b, m_ssm_dt_bias, m_ssm_a_log, m_ssm_d, m_ssm_norm_w,
          m_gdn_conv_w, m_gdn_a_log, m_gdn_dt_bias, m_gdn_norm_w, m_w_proj_ssm, m_w_proj_gdn, m_w_out,
          m_norm_ffn_w, m_w_ffn_in, m_w_ffn_down, m_final_norm_w)
    vl = (v_norm_mix_w, v_w_in, v_ssm_conv_w, v_ssm_conv_b, v_ssm_dt_bias, v_ssm_a_log, v_ssm_d, v_ssm_norm_w,
          v_gdn_conv_w, v_gdn_a_log, v_gdn_dt_bias, v_gdn_norm_w, v_w_proj_ssm, v_w_proj_gdn, v_w_out,
          v_norm_ffn_w, v_w_ffn_in, v_w_ffn_down, v_final_norm_w)
    w = dict(zip(WEIGHTS, wl))
    m = dict(zip(WEIGHTS, ml))
    v = dict(zip(WEIGHTS, vl))
    x_pos, y_pos, c = _pos()
    me = 2 * x_pos + y_pos
    big = [n for n, _ in BIG]

    shards = [w[n].astype(BF16) for n in big]
    shards[0] = jnp.pad(shards[0], ((0, 0), (0, 0), (0, IN_SHARD_PAD - IN_SHARD)))
    conv_shapes = [w[n].shape[1:] for n, _ in CONVW]
    conv_pack = _pack_rows([w[n] for n, _ in CONVW], lead=(DEPTH,))
    gathered = weights_gather("gather_w", [_slot_buffer(s) for s in shards + [conv_pack]])
    by_chip = [[lax.dynamic_index_in_dim(g_, jnp.bitwise_xor(me, j), 0, keepdims=False) for j in range(N_CHIPS)]
               for g_ in gathered]
    full = {n: w[n] for n in SMALL}
    for i, (n, axis) in enumerate(BIG):
        full[n] = jnp.concatenate(by_chip[i], axis=axis)
    full["w_in_sp"] = full["w_in"]
    full["w_in"] = jnp.concatenate([p[:, :, :IN_SHARD] for p in by_chip[0]], axis=2)
    conv_parts = [_unpack(by_chip[-1][j], conv_shapes, lead=(DEPTH,)) for j in range(N_CHIPS)]
    for i, (n, axis) in enumerate(CONVW):
        full[n] = jnp.concatenate([conv_parts[j][i] for j in range(N_CHIPS)], axis=axis)

    loss_part, grad_x, grads = local_step(x[0], loss_target[0], full)

    by4 = [grads[n].reshape((DEPTH, N_CHIPS, -1) + grads[n].shape[-1:]) for n in big]
    from_pair = pair_swap("grad_pair_swap", by4)
    chip_part = [pair_add(f"grad_pair_add_{n}", g_, r_, c) for n, g_, r_ in zip(big, by4, from_pair)]
    from_chips = chip_scatter("grad_chip_scatter", chip_part)
    halves = [chip_sum(f"grad_chip_sum_{n}", s_, r_, me, c) for n, s_, r_ in zip(big, chip_part, from_chips)]
    reduced = pair_share("grad_pair_share", halves)
    g_sharded = dict(zip(big, reduced))
    g_sharded["w_in"] = g_sharded["w_in"][:, :, :IN_SHARD]

    small_names = list(SMALL) + [n for n, _ in CONVW]
    small_all = all_allgather("gather_small", _pack_rows([grads[n] for n in small_names] + [loss_part[0, :1]]))
    small_sum = slab_sum("small_sum", small_all)
    small_vals = _unpack(small_sum, [grads[n].shape for n in small_names] + [(1,)])
    g_small = dict(zip(small_names, small_vals[:-1]))
    loss = small_vals[-1].reshape(())
    for n, axis in CONVW:
        size = w[n].shape[axis]
        g_sharded[n] = lax.dynamic_slice_in_dim(g_small.pop(n), me * size, size, axis=axis)

    out_g, out_d, out_m, out_v = {}, {}, {}, {}
    for n, _ in SHARDED:
        shp = w[n].shape
        two = lambda a: a.reshape(-1, shp[-1])
        d_, m_, v_ = adamw(f"adamw_{n}", two(w[n]), two(g_sharded[n]), two(m[n]), two(v[n]))
        out_g[n], out_d[n], out_m[n], out_v[n] = g_sharded[n], d_.reshape(shp), m_.reshape(shp), v_.reshape(shp)
    d_, m_, v_ = adamw("adamw_small", *[_pack_rows([d[n] for n in SMALL]) for d in (w, g_small, m, v)])
    small_shapes = [w[n].shape for n in SMALL]
    for n, dd, mm, vv in zip(SMALL, _unpack(d_, small_shapes), _unpack(m_, small_shapes), _unpack(v_, small_shapes)):
        out_g[n], out_d[n], out_m[n], out_v[n] = g_small[n], dd, mm, vv

    return (loss, grad_x[None], *[out_g[n] for n in WEIGHTS], *[out_d[n] for n in WEIGHTS],
            *[out_m[n] for n in WEIGHTS], *[out_v[n] for n in WEIGHTS])
```

```python
import math

import jax
import jax.numpy as jnp
from jax import lax
from jax.experimental import pallas as pl
from jax.experimental.pallas import tpu as pltpu

F32 = jnp.float32
BF16 = jnp.bfloat16

D_MODEL = 1024
DEPTH = 2
SSM_HEADS = 16
SSM_HEAD_DIM = 64
SSM_D_INNER = 1024
SSM_STATE = 128
SSM_CONV_DIM = 1536
GDN_HEADS = 8
GDN_HEAD = 128
GDN_QKV_DIM = 3072
CONV_K = 4
CHUNK = 64
FFN_HIDDEN = 2816
EPS = 1e-6
IN_DIM = 8736

ADAM_LR = 0.001
ADAM_B1 = 0.9
ADAM_B2 = 0.999
ADAM_EPS = 1e-08
ADAM_WD = 0.01
ADAM_STEP = 10

LANE = 128
NEG_BIG = -1e30
VMEM_LIMIT = 56 * 1024 * 1024

AL_Z, AL_XBC, AL_QKV, AL_GZ, AL_GS, AL_GG, AL_SMALL, AL_DIM = 0, 1024, 2560, 5632, 6656, 7680, 8704, 9216
SM_DT, SM_A, SM_B = 0, 16, 24

HI = lax.Precision.HIGHEST
NN = (((1,), (0,)), ((), ()))
NT = (((1,), (1,)), ((), ()))
TN = (((0,), (0,)), ((), ()))


def _cparams(sem):
    return pltpu.CompilerParams(dimension_semantics=sem, vmem_limit_bytes=VMEM_LIMIT)


def _pick(n, prefs):
    for p in prefs:
        if n % p == 0:
            return p
    return n


MATMUL_VMEM_BUDGET = 40 << 20
N_CHIPS = 4


def _matmul_tiles(m, n, n_dom, k, a_item, b_item, o_item):
    best = None
    tms = [c for c in (2048, 1024, 512, 256, 128) if m % c == 0] or [m]
    tns = [c for c in (1024, 768, 512, 256, 128) if n_dom % c == 0] or [n_dom]
    tks = [k] + [c for c in (2048, 1024, 512, 256, 128) if c < k and k % c == 0]
    for tm in tms:
        for tn in tns:
            for tk in tks:
                nk = k // tk
                vmem = (2 * (tm * tk * a_item + tk * tn * b_item + tm * tn * o_item) + tm * tn * 4 * (2 if nk > 1 else 1)
                        + (tm * tk * 2 if a_item > 2 else 0) + (tk * tn * 2 if b_item > 2 else 0))
                if vmem > MATMUL_VMEM_BUDGET:
                    continue
                traffic = m * k * a_item * (1 if nk == 1 else n // tn) + k * n * b_item * (m // tm)
                steps = (m // tm) * (n // tn) * nk
                key = (traffic, steps)
                if best is None or key < best[0]:
                    best = (key, (tm, tn, tk))
    return best[1]


def _dot(a, b, dims=NN):
    return lax.dot_general(a.astype(BF16), b.astype(BF16), dims, preferred_element_type=F32)


def _dot_hi(a, b, dims=NN):
    return lax.dot_general(a, b, dims, precision=HI, preferred_element_type=F32)


def _sigmoid(x):
    return jax.nn.sigmoid(x)


def _silu(x):
    return x * _sigmoid(x)


def _softplus(x):
    return jnp.maximum(x, 0.0) + jnp.log1p(jnp.exp(-jnp.abs(x)))


def matmul(name, a, b, mode, out_dtype=F32, chip_major=False, stack=None):
    if mode == "nn":
        (m, k), (k2, n) = a.shape, b.shape
    elif mode == "nt":
        (m, k), (n, k2) = a.shape, b.shape
    else:
        (k, m), (k2, n) = a.shape, b.shape
    assert k == k2, (a.shape, b.shape, mode)
    shard = n // N_CHIPS if chip_major else n
    tm, tn, tk = _matmul_tiles(m, n, shard, k, a.dtype.itemsize, b.dtype.itemsize, jnp.dtype(out_dtype).itemsize)
    if chip_major:
        per = shard // tn
        base_shape, base_blk = (N_CHIPS, m, shard), (None, tm, tn)
        base_idx = lambda i, j: (j // per, i, j % per)
    else:
        base_shape, base_blk = (m, n), (tm, tn)
        base_idx = lambda i, j: (i, j)
    nk = k // tk
    dims = {"nn": NN, "nt": NT, "tn": TN}[mode]

    def body_acc(a_ref, b_ref, o_ref, acc_ref):
        kk = pl.program_id(2)

        @pl.when(kk == 0)
        def _():
            acc_ref[...] = jnp.zeros_like(acc_ref)

        acc_ref[...] += _dot(a_ref[...], b_ref[...], dims)

        @pl.when(kk == nk - 1)
        def _():
            o_ref[...] = acc_ref[...].astype(o_ref.dtype)

    def body_one(a_ref, b_ref, o_ref):
        o_ref[...] = _dot(a_ref[...], b_ref[...], dims).astype(o_ref.dtype)

    compute = body_one if nk == 1 else body_acc
    if mode == "tn":
        a_spec = pl.BlockSpec((tk, tm), lambda i, j, kk: (kk, i))
    else:
        a_spec = pl.BlockSpec((tm, tk), lambda i, j, kk: (i, kk))
    if mode == "nt":
        b_spec = pl.BlockSpec((tn, tk), lambda i, j, kk: (j, kk))
    else:
        b_spec = pl.BlockSpec((tk, tn), lambda i, j, kk: (kk, j))
    in_specs, operands, aliases, body = [a_spec, b_spec], [a, b], {}, compute
    if stack is None:
        out_shape, out_blk, out_idx = base_shape, base_blk, (lambda i, j, kk: base_idx(i, j))
    else:
        layer, buf = stack
        out_shape, out_blk = (DEPTH,) + base_shape, (None,) + base_blk
        out_idx = lambda i, j, kk: (layer,) + base_idx(i, j)
        if buf is not None:
            assert buf.shape == out_shape and buf.dtype == out_dtype
            in_specs.append(pl.BlockSpec(memory_space=pl.ANY))
            operands.append(buf)
            aliases = {2: 0}

            def body(a_ref, b_ref, buf_ref, *rest):
                del buf_ref
                compute(a_ref, b_ref, *rest)

    return pl.pallas_call(
        body, name=name,
        out_shape=jax.ShapeDtypeStruct(out_shape, out_dtype),
        grid=(m // tm, n // tn, nk),
        in_specs=in_specs,
        out_specs=pl.BlockSpec(out_blk, out_idx),
        scratch_shapes=[] if nk == 1 else [pltpu.VMEM((tm, tn), F32)],
        input_output_aliases=aliases,
        compiler_params=_cparams(("parallel", "parallel", "arbitrary")),
    )(*operands)


def _row_map(c0, moves):
    return (lambda j, i: (i, c0 + j)) if moves else (lambda j, i: (i, c0))


def _par_map(c0, moves):
    return (lambda j, i: (0, c0 + j)) if moves else (lambda j, i: (0, c0))


def _in_spec(op, tile):
    _, kind, w, c0, moves = op
    if kind == "row":
        return pl.BlockSpec((tile, w), _row_map(c0, moves))
    return pl.BlockSpec((1, w), _par_map(c0, moves))


ROW_BLOCK_ELEMS = 1 << 18


def _row_tile(t, tile, ops):
    width = max(op[2] for op in ops if op[1] == "row")
    return min(t, max(tile, ROW_BLOCK_ELEMS // width))


def rowwise_fwd(name, fn, t, tile, ncol, ins, outs):
    n_in = len(ins)
    tile = _row_tile(t, tile, ins)

    def body(*refs):
        vals = [r[...].astype(F32) for r in refs[:n_in]]
        res = fn(*vals)
        if not isinstance(res, (tuple, list)):
            res = (res,)
        for r, v in zip(refs[n_in:], res):
            r[...] = v.astype(r.dtype)

    res = pl.pallas_call(
        body, name=name,
        out_shape=[jax.ShapeDtypeStruct((t, w * ncol), dt) for w, dt in outs],
        grid=(ncol, t // tile),
        in_specs=[_in_spec(op, tile) for op in ins],
        out_specs=[pl.BlockSpec((tile, w), _row_map(0, True)) for w, _ in outs],
        compiler_params=_cparams(("arbitrary", "arbitrary")),
    )(*[op[0] for op in ins])
    return res


def rowwise_bwd(name, fn, t, tile, ncol, ins, need, cts, addends=None, row_dtypes=None):
    n_in, n_ct = len(ins), len(cts)
    tile = _row_tile(t, tile, ins)
    addends = addends or {}
    row_dtypes = row_dtypes or {}
    didx = [i for i in range(n_in) if need[i]]
    add_ops = [addends[i] for i in didx if i in addends]
    n_add = len(add_ops)

    def body(*refs):
        in_refs = refs[:n_in]
        ct_refs = refs[n_in:n_in + n_ct]
        add_refs = refs[n_in + n_ct:n_in + n_ct + n_add]
        out_refs = refs[n_in + n_ct + n_add:]
        vals = [r[...].astype(F32) for r in in_refs]

        def g(*dv):
            full = list(vals)
            for i, v in zip(didx, dv):
                full[i] = v
            res = fn(*full)
            return tuple(res) if isinstance(res, (tuple, list)) else (res,)

        _, vjp = jax.vjp(g, *[vals[i] for i in didx])
        grads = vjp(tuple(c[...].astype(F32) for c in ct_refs))
        j, i = pl.program_id(0), pl.program_id(1)
        a = 0
        for o_ref, gv, idx in zip(out_refs, grads, didx):
            _, kind, _, _, moves = ins[idx]
            if kind == "row":
                if idx in addends:
                    gv = gv + add_refs[a][...].astype(F32)
                    a += 1
                o_ref[...] = gv.astype(o_ref.dtype)
            else:
                first = (i == 0) if moves else jnp.logical_and(i == 0, j == 0)

                @pl.when(first)
                def _(o_ref=o_ref, gv=gv):
                    o_ref[...] = gv

                @pl.when(jnp.logical_not(first))
                def _(o_ref=o_ref, gv=gv):
                    o_ref[...] += gv

    out_shape, out_specs = [], []
    for idx in didx:
        _, kind, w, _, moves = ins[idx]
        cols = w * (ncol if moves else 1)
        if kind == "row":
            out_shape.append(jax.ShapeDtypeStruct((t, cols), row_dtypes.get(idx, F32)))
            out_specs.append(pl.BlockSpec((tile, w), _row_map(0, moves)))
        else:
            out_shape.append(jax.ShapeDtypeStruct((1, cols), F32))
            out_specs.append(pl.BlockSpec((1, w), _par_map(0, moves)))
    ops = list(ins) + list(cts) + add_ops
    res = pl.pallas_call(
        body, name=name,
        out_shape=out_shape,
        grid=(ncol, t // tile),
        in_specs=[_in_spec(op, tile) for op in ops],
        out_specs=out_specs,
        compiler_params=_cparams(("arbitrary", "arbitrary")),
    )(*[op[0] for op in ops])
    return res


def f_rmsnorm(x, w):
    return x * lax.rsqrt(jnp.mean(x * x, axis=-1, keepdims=True) + EPS) * w


def f_ssd_post(y, z, w):
    y = y * _silu(z)
    return y * lax.rsqrt(jnp.mean(y * y, axis=-1, keepdims=True) + EPS) * w


def f_gdn_post(o, z, w):
    o = o * lax.rsqrt(jnp.mean(o * o, axis=-1, keepdims=True) + EPS) * w
    return o * _silu(z)


def f_merge(gs, p1, gg, p2):
    return _sigmoid(gs) * p1 + _sigmoid(gg) * p2


def f_swiglu(g, u):
    return _silu(g) * u


def final_loss(name, x, tgt, w, tile=256):
    t, d = x.shape

    def body(x_ref, t_ref, w_ref, loss_ref, dx_ref, dw_ref):
        i = pl.program_id(0)
        xv, tv, wv = x_ref[...], t_ref[...], w_ref[...]

        def g(xx, ww):
            err = f_rmsnorm(xx, ww) - tv
            return 0.5 * jnp.sum(jnp.mean(err * err, axis=-1, keepdims=True), axis=0, keepdims=True)

        val, vjp = jax.vjp(g, xv, wv)
        dx, dw = vjp(jnp.ones((1, 1), F32))
        dx_ref[...] = dx
        lv = jnp.broadcast_to(val, (1, LANE))

        @pl.when(i == 0)
        def _():
            loss_ref[...] = lv
            dw_ref[...] = dw

        @pl.when(i != 0)
        def _():
            loss_ref[...] += lv
            dw_ref[...] += dw

    return pl.pallas_call(
        body, name=name,
        out_shape=[jax.ShapeDtypeStruct((1, LANE), F32), jax.ShapeDtypeStruct((t, d), F32),
                   jax.ShapeDtypeStruct((1, d), F32)],
        grid=(t // tile,),
        in_specs=[pl.BlockSpec((tile, d), lambda i: (i, 0)), pl.BlockSpec((tile, d), lambda i: (i, 0)),
                  pl.BlockSpec((1, d), lambda i: (0, 0))],
        out_specs=[pl.BlockSpec((1, LANE), lambda i: (0, 0)), pl.BlockSpec((tile, d), lambda i: (i, 0)),
                   pl.BlockSpec((1, d), lambda i: (0, 0))],
        compiler_params=_cparams(("arbitrary",)),
    )(x, tgt, w)


CONV_W = 512
HALO = 8


def conv_fwd(name, src, c0, width, w, b, tile=512):
    t = src.shape[0]
    ncol, nrow = width // CONV_W, t // tile
    cb0 = c0 // CONV_W
    hb = tile // HALO

    def body(prev_ref, cur_ref, w_ref, b_ref, o_ref, ext_ref):
        i = pl.program_id(1)
        ext_ref[0:HALO, :] = jnp.where(i == 0, 0.0, prev_ref[...])
        ext_ref[HALO:HALO + tile, :] = cur_ref[...]
        acc = jnp.broadcast_to(b_ref[...], (tile, CONV_W))
        for k in range(CONV_K):
            acc = acc + w_ref[k:k + 1, :] * ext_ref[pl.ds(HALO - (CONV_K - 1) + k, tile), :]
        o_ref[...] = _silu(acc)

    return pl.pallas_call(
        body, name=name,
        out_shape=jax.ShapeDtypeStruct((t, width), F32),
        grid=(ncol, nrow),
        in_specs=[pl.BlockSpec((HALO, CONV_W), lambda j, i: (jnp.maximum(i * hb - 1, 0), cb0 + j)),
                  pl.BlockSpec((tile, CONV_W), lambda j, i: (i, cb0 + j)),
                  pl.BlockSpec((CONV_K, CONV_W), lambda j, i: (0, j)),
                  pl.BlockSpec((1, CONV_W), lambda j, i: (0, j))],
        out_specs=pl.BlockSpec((tile, CONV_W), lambda j, i: (i, j)),
        scratch_shapes=[pltpu.VMEM((tile + HALO, CONV_W), F32)],
        compiler_params=_cparams(("arbitrary", "arbitrary")),
    )(src, src, w, b)


def conv_bwd_pre(name, src, c0, width, w, b, dy, tile=512):
    t = src.shape[0]
    ncol, nrow = width // CONV_W, t // tile
    cb0 = c0 // CONV_W
    hb = tile // HALO

    def body(prev_ref, cur_ref, w_ref, b_ref, dy_ref, dpre_ref, dw_ref, db_ref, ext_ref):
        i = pl.program_id(1)
        ext_ref[0:HALO, :] = jnp.where(i == 0, 0.0, prev_ref[...])
        ext_ref[HALO:HALO + tile, :] = cur_ref[...]
        pre = jnp.broadcast_to(b_ref[...], (tile, CONV_W))
        for k in range(CONV_K):
            pre = pre + w_ref[k:k + 1, :] * ext_ref[pl.ds(HALO - (CONV_K - 1) + k, tile), :]
        s = _sigmoid(pre)
        dpre = dy_ref[...] * (s * (1.0 + pre * (1.0 - s)))
        dpre_ref[...] = dpre

        @pl.when(i == 0)
        def _():
            dw_ref[...] = jnp.zeros_like(dw_ref)
            db_ref[...] = jnp.zeros_like(db_ref)

        for k in range(CONV_K):
            dw_ref[k:k + 1, :] += jnp.sum(dpre * ext_ref[pl.ds(HALO - (CONV_K - 1) + k, tile), :],
                                          axis=0, keepdims=True)
        db_ref[...] += jnp.sum(dpre, axis=0, keepdims=True)

    return pl.pallas_call(
        body, name=name,
        out_shape=[jax.ShapeDtypeStruct((t, width), F32), jax.ShapeDtypeStruct((CONV_K, width), F32),
                   jax.ShapeDtypeStruct((1, width), F32)],
        grid=(ncol, nrow),
        in_specs=[pl.BlockSpec((HALO, CONV_W), lambda j, i: (jnp.maximum(i * hb - 1, 0), cb0 + j)),
                  pl.BlockSpec((tile, CONV_W), lambda j, i: (i, cb0 + j)),
                  pl.BlockSpec((CONV_K, CONV_W), lambda j, i: (0, j)),
                  pl.BlockSpec((1, CONV_W), lambda j, i: (0, j)),
                  pl.BlockSpec((tile, CONV_W), lambda j, i: (i, j))],
        out_specs=[pl.BlockSpec((tile, CONV_W), lambda j, i: (i, j)),
                   pl.BlockSpec((CONV_K, CONV_W), lambda j, i: (0, j)),
                   pl.BlockSpec((1, CONV_W), lambda j, i: (0, j))],
        scratch_shapes=[pltpu.VMEM((tile + HALO, CONV_W), F32)],
        compiler_params=_cparams(("arbitrary", "arbitrary")),
    )(src, src, w, b, dy)


def conv_bwd_in(name, dpre, w, tile=512):
    t, width = dpre.shape
    ncol, nrow = width // CONV_W, t // tile
    hb = tile // HALO
    last_hb = t // HALO - 1

    def body(cur_ref, nxt_ref, w_ref, o_ref, ext_ref):
        i = pl.program_id(1)
        ext_ref[0:tile, :] = cur_ref[...]
        ext_ref[tile:tile + HALO, :] = jnp.where(i == nrow - 1, 0.0, nxt_ref[...])
        acc = jnp.zeros((tile, CONV_W), F32)
        for k in range(CONV_K):
            acc = acc + w_ref[k:k + 1, :] * ext_ref[pl.ds(CONV_K - 1 - k, tile), :]
        o_ref[...] = acc.astype(o_ref.dtype)

    return pl.pallas_call(
        body, name=name,
        out_shape=jax.ShapeDtypeStruct((t, width), BF16),
        grid=(ncol, nrow),
        in_specs=[pl.BlockSpec((tile, CONV_W), lambda j, i: (i, j)),
                  pl.BlockSpec((HALO, CONV_W), lambda j, i: (jnp.minimum((i + 1) * hb, last_hb), j)),
                  pl.BlockSpec((CONV_K, CONV_W), lambda j, i: (0, j))],
        out_specs=pl.BlockSpec((tile, CONV_W), lambda j, i: (i, j)),
        scratch_shapes=[pltpu.VMEM((tile + HALO, CONV_W), F32)],
        compiler_params=_cparams(("arbitrary", "arbitrary")),
    )(dpre, dpre, w)


def _iota2(q):
    return (lax.broadcasted_iota(jnp.int32, (q, q), 0), lax.broadcasted_iota(jnp.int32, (q, q), 1))


def _lane_pick(blk, idx):
    lane = lax.broadcasted_iota(jnp.int32, (1, LANE), 1)
    return jnp.sum(jnp.where(lane == idx, blk, 0.0), axis=1, keepdims=True)


def _cum(a_col, r, c):
    a_row = jnp.sum(jnp.where(r == c, a_col, 0.0), axis=0, keepdims=True)
    cum_col = jnp.sum(jnp.where(c <= r, a_row, 0.0), axis=1, keepdims=True)
    cum_row = jnp.sum(jnp.where(r <= c, a_col, 0.0), axis=0, keepdims=True)
    return cum_col, cum_row


_SSD_B = SSM_D_INNER
_SSD_C = SSM_D_INNER + 2 * SSM_STATE


def _interleave(gens):
    results = [None] * len(gens)
    live = list(range(len(gens)))
    while live:
        for i in list(live):
            try:
                next(gens[i])
            except StopIteration as stop:
                results[i] = stop.value
                live.remove(i)
    return results


def ssd_chunk(xs, bm, cm, small, dtb, alog, dsk, state, p, cb):
    q = xs.shape[0]
    r, c = _iota2(q)
    lane = lax.broadcasted_iota(jnp.int32, (1, LANE), 1)
    m0 = lane < SSM_HEAD_DIM

    def head(h):
        dt = _softplus(_lane_pick(small, SM_DT + h) + _lane_pick(dtb, h))
        a = dt * (-jnp.exp(_lane_pick(alog, h)))
        cum_col, cum_row = _cum(a, r, c)
        lmat = jnp.exp(jnp.where(r >= c, cum_col - cum_row, NEG_BIG))
        tot = jnp.sum(a, axis=0, keepdims=True)
        return dt, _lane_pick(dsk, h), cum_col, lmat, tot

    dt0, d0, cum0, l0, tot0 = head(2 * p)
    dt1, d1, cum1, l1, tot1 = head(2 * p + 1)
    xdt = xs * jnp.where(m0, dt0, dt1)
    yield
    y_diag = _dot(cb * l0, jnp.where(m0, xdt, 0.0)) + _dot(cb * l1, jnp.where(m0, 0.0, xdt))
    y_off = _dot(cm, state, NT) * jnp.where(m0, jnp.exp(cum0), jnp.exp(cum1))
    yield
    dec = jnp.where(m0, jnp.exp(tot0 - cum0), jnp.exp(tot1 - cum1))
    rowm = lax.broadcasted_iota(jnp.int32, (LANE, 1), 0) < SSM_HEAD_DIM
    new_state = state * jnp.where(rowm, jnp.exp(tot0), jnp.exp(tot1)) + _dot(xdt * dec, bm, TN)
    y = y_diag + y_off + jnp.where(m0, d0, d1) * xs
    return y, new_state


def ssd_pairs(xs, bms, cms, small, dtb, alog, dsk, states):
    cbs = [_dot(cm, bm, NT) for cm, bm in zip(cms, bms)]
    res = _interleave([ssd_chunk(x, bms[p // 4], cms[p // 4], small, dtb, alog, dsk, st, p, cbs[p // 4])
                       for p, (x, st) in enumerate(zip(xs, states))])
    return tuple(y for y, _ in res), tuple(s for _, s in res)


def tri_inverse(a):
    q = a.shape[0]
    r, c = _iota2(q)
    eye = (r == c).astype(F32)
    diag = (r // 16) == (c // 16)
    bd = jnp.where(diag, a, 0.0)
    off = jnp.where(diag, 0.0, a)
    b2 = _dot_hi(bd, bd)
    d1 = _dot_hi(eye - bd, eye + b2)
    yield
    b4 = _dot_hi(b2, b2)
    yield
    b8 = _dot_hi(b4, b4)
    d2 = _dot_hi(d1, eye + b4)
    yield
    dinv = _dot_hi(d2, eye + b8)
    yield
    n = _dot_hi(dinv, off)
    yield
    n2 = _dot_hi(n, n)
    yield
    m = _dot_hi(eye + n2, dinv)
    yield
    return _dot_hi(eye - n, m)


@jax.custom_vjp
def _solve_with(xinv, a, rhs):
    del a
    return _dot_hi(xinv, rhs)


def _solve_with_fwd(xinv, a, rhs):
    t = _dot_hi(xinv, rhs)
    return t, (xinv, t)


def _solve_with_bwd(res, dt):
    xinv, t = res
    d_rhs = _dot_hi(xinv, dt, TN)
    d_a = -_dot(d_rhs, t, NT)
    return jnp.zeros_like(xinv), d_a, d_rhs


_solve_with.defvjp(_solve_with_fwd, _solve_with_bwd)


_GDN_K = GDN_HEADS * GDN_HEAD
_GDN_V = 2 * GDN_HEADS * GDN_HEAD


def gdn_chunk(qh, kh, vh, small, alog, dtb, state, h, xinv=None):
    q = qh.shape[0]
    r, c = _iota2(q)
    qn = qh * lax.rsqrt(jnp.sum(qh * qh, axis=-1, keepdims=True) + EPS) * (GDN_HEAD ** -0.5)
    kn = kh * lax.rsqrt(jnp.sum(kh * kh, axis=-1, keepdims=True) + EPS)
    beta = _sigmoid(_lane_pick(small, SM_B + h))
    g = -jnp.exp(_lane_pick(alog, h)) * _softplus(_lane_pick(small, SM_A + h) + _lane_pick(dtb, h))
    gc_col, gc_row = _cum(g, r, c)
    decay = jnp.exp(jnp.where(r >= c, gc_col - gc_row, NEG_BIG))
    yield
    kk = _dot(kn, kn, NT)
    qk = _dot(qn, kn, NT) * decay
    amat = jnp.where(r > c, kk * decay * beta, 0.0)
    eg = jnp.exp(gc_col)
    rhs = jnp.concatenate([vh * beta, kn * (beta * eg)], axis=1)
    qs = _dot(qn * eg, state)
    yield
    if xinv is None:
        xinv = yield from tri_inverse(amat)
        t = _dot_hi(xinv, rhs)
    else:
        t = _solve_with(xinv, amat, rhs)
    yield
    u, w = t[:, :GDN_HEAD], t[:, GDN_HEAD:]
    v_new = u - _dot(w, state)
    yield
    o = qs + _dot(qk, v_new)
    tot = jnp.sum(g, axis=0, keepdims=True)
    new_state = state * jnp.exp(tot) + _dot(kn * jnp.exp(tot - gc_col), v_new, TN)
    return o, new_state, xinv


def gdn_heads(qs, ks, vs, small, alog, dtb, states, xinvs=None):
    nh = len(qs)
    res = _interleave([gdn_chunk(qs[h], ks[h], vs[h], small, alog, dtb, states[h], h,
                                 None if xinvs is None else xinvs[h]) for h in range(nh)])
    return tuple(o for o, _, _ in res), tuple(s for _, s, _ in res), tuple(x for _, _, x in res)


def _acc(ref, val, first):
    @pl.when(first)
    def _():
        ref[...] = val

    @pl.when(jnp.logical_not(first))
    def _():
        ref[...] += val


def ssd_scan_fwd(name, xbc, proj, dtb, alog, dsk):
    t = xbc.shape[0]
    nc, npair = t // CHUNK, SSM_HEADS // 2
    small_blk = AL_SMALL // LANE

    def body(xbc_ref, sm_ref, dtb_ref, alog_ref, dsk_ref, y_ref, sin_ref, st_ref):
        ci = pl.program_id(0)

        @pl.when(ci == 0)
        def _():
            st_ref[...] = jnp.zeros_like(st_ref)

        s_in = tuple(st_ref[p] for p in range(npair))
        ys, s_new = ssd_pairs(tuple(xbc_ref[:, p * LANE:(p + 1) * LANE] for p in range(npair)),
                              tuple(xbc_ref[:, _SSD_B + g * LANE:_SSD_B + (g + 1) * LANE] for g in range(2)),
                              tuple(xbc_ref[:, _SSD_C + g * LANE:_SSD_C + (g + 1) * LANE] for g in range(2)),
                              sm_ref[...], dtb_ref[...], alog_ref[...], dsk_ref[...], s_in)
        for p in range(npair):
            sin_ref[0, p] = s_in[p]
            y_ref[:, p * LANE:(p + 1) * LANE] = ys[p]
            st_ref[p] = s_new[p]

    par = pl.BlockSpec((1, LANE), lambda ci: (0, 0))
    return pl.pallas_call(
        body, name=name,
        out_shape=[jax.ShapeDtypeStruct((t, SSM_D_INNER), F32),
                   jax.ShapeDtypeStruct((nc, npair, LANE, LANE), F32)],
        grid=(nc,),
        in_specs=[pl.BlockSpec((CHUNK, SSM_CONV_DIM), lambda ci: (ci, 0)),
                  pl.BlockSpec((CHUNK, LANE), lambda ci: (ci, small_blk)),
                  par, par, par],
        out_specs=[pl.BlockSpec((CHUNK, SSM_D_INNER), lambda ci: (ci, 0)),
                   pl.BlockSpec((1, npair, LANE, LANE), lambda ci: (ci, 0, 0, 0))],
        scratch_shapes=[pltpu.VMEM((npair, LANE, LANE), F32)],
        compiler_params=_cparams(("arbitrary",)),
    )(xbc, proj, dtb, alog, dsk)


def ssd_scan_bwd(name, xbc, proj, dtb, alog, dsk, s_in, dy):
    t = xbc.shape[0]
    nc, npair = t // CHUNK, SSM_HEADS // 2
    small_blk = AL_SMALL // LANE

    def body(xbc_ref, sm_ref, dtb_ref, alog_ref, dsk_ref, sin_ref, dy_ref,
             dxbc_ref, dsm_ref, ddtb_ref, dalog_ref, ddsk_ref, dst_ref):
        ci = pl.program_id(0)

        @pl.when(ci == 0)
        def _():
            dst_ref[...] = jnp.zeros_like(dst_ref)

        _, vjp = jax.vjp(ssd_pairs, tuple(xbc_ref[:, p * LANE:(p + 1) * LANE] for p in range(npair)),
                         tuple(xbc_ref[:, _SSD_B + g * LANE:_SSD_B + (g + 1) * LANE] for g in range(2)),
                         tuple(xbc_ref[:, _SSD_C + g * LANE:_SSD_C + (g + 1) * LANE] for g in range(2)),
                         sm_ref[...], dtb_ref[...], alog_ref[...], dsk_ref[...],
                         tuple(sin_ref[0, p] for p in range(npair)))
        dxs, dbms, dcms, dsm, ddtb, dalog, ddsk, dsts = vjp(
            (tuple(dy_ref[:, p * LANE:(p + 1) * LANE] for p in range(npair)),
             tuple(dst_ref[p] for p in range(npair))))
        for p in range(npair):
            dxbc_ref[:, p * LANE:(p + 1) * LANE] = dxs[p]
            dst_ref[p] = dsts[p]
        for g in range(2):
            dxbc_ref[:, _SSD_B + g * LANE:_SSD_B + (g + 1) * LANE] = dbms[g]
            dxbc_ref[:, _SSD_C + g * LANE:_SSD_C + (g + 1) * LANE] = dcms[g]
        dsm_ref[...] = dsm
        _acc(ddtb_ref, ddtb, ci == 0)
        _acc(dalog_ref, dalog, ci == 0)
        _acc(ddsk_ref, ddsk, ci == 0)

    par = pl.BlockSpec((1, LANE), lambda ci: (0, 0))
    rev = lambda ci: nc - 1 - ci
    return pl.pallas_call(
        body, name=name,
        out_shape=[jax.ShapeDtypeStruct((t, SSM_CONV_DIM), F32),
                   jax.ShapeDtypeStruct((t, LANE), F32),
                   jax.ShapeDtypeStruct((1, LANE), F32), jax.ShapeDtypeStruct((1, LANE), F32),
                   jax.ShapeDtypeStruct((1, LANE), F32)],
        grid=(nc,),
        in_specs=[pl.BlockSpec((CHUNK, SSM_CONV_DIM), lambda ci: (rev(ci), 0)),
                  pl.BlockSpec((CHUNK, LANE), lambda ci: (rev(ci), small_blk)),
                  par, par, par,
                  pl.BlockSpec((1, npair, LANE, LANE), lambda ci: (rev(ci), 0, 0, 0)),
                  pl.BlockSpec((CHUNK, SSM_D_INNER), lambda ci: (rev(ci), 0))],
        out_specs=[pl.BlockSpec((CHUNK, SSM_CONV_DIM), lambda ci: (rev(ci), 0)),
                   pl.BlockSpec((CHUNK, LANE), lambda ci: (rev(ci), 0)),
                   par, par, par],
        scratch_shapes=[pltpu.VMEM((npair, LANE, LANE), F32)],
        compiler_params=_cparams(("arbitrary",)),
    )(xbc, proj, dtb, alog, dsk, s_in, dy)


def gdn_scan_fwd(name, qkv, proj, alog, dtb):
    t = qkv.shape[0]
    nc, nh = t // CHUNK, GDN_HEADS
    small_blk = AL_SMALL // LANE

    def body(qkv_ref, sm_ref, alog_ref, dtb_ref, o_ref, sin_ref, x_ref, st_ref):
        ci = pl.program_id(0)

        @pl.when(ci == 0)
        def _():
            st_ref[...] = jnp.zeros_like(st_ref)

        s_in = tuple(st_ref[h] for h in range(nh))
        os, s_new, xinvs = gdn_heads(
            tuple(qkv_ref[:, h * LANE:(h + 1) * LANE] for h in range(nh)),
            tuple(qkv_ref[:, _GDN_K + h * LANE:_GDN_K + (h + 1) * LANE] for h in range(nh)),
            tuple(qkv_ref[:, _GDN_V + h * LANE:_GDN_V + (h + 1) * LANE] for h in range(nh)),
            sm_ref[...], alog_ref[...], dtb_ref[...], s_in)
        for h in range(nh):
            sin_ref[0, h] = s_in[h]
            o_ref[:, h * LANE:(h + 1) * LANE] = os[h]
            x_ref[0, h] = xinvs[h]
            st_ref[h] = s_new[h]

    par = pl.BlockSpec((1, LANE), lambda ci: (0, 0))
    return pl.pallas_call(
        body, name=name,
        out_shape=[jax.ShapeDtypeStruct((t, GDN_HEADS * GDN_HEAD), F32),
                   jax.ShapeDtypeStruct((nc, nh, LANE, LANE), F32),
                   jax.ShapeDtypeStruct((nc, nh, CHUNK, CHUNK), F32)],
        grid=(nc,),
        in_specs=[pl.BlockSpec((CHUNK, GDN_QKV_DIM), lambda ci: (ci, 0)),
                  pl.BlockSpec((CHUNK, LANE), lambda ci: (ci, small_blk)),
                  par, par],
        out_specs=[pl.BlockSpec((CHUNK, GDN_HEADS * GDN_HEAD), lambda ci: (ci, 0)),
                   pl.BlockSpec((1, nh, LANE, LANE), lambda ci: (ci, 0, 0, 0)),
                   pl.BlockSpec((1, nh, CHUNK, CHUNK), lambda ci: (ci, 0, 0, 0))],
        scratch_shapes=[pltpu.VMEM((nh, LANE, LANE), F32)],
        compiler_params=_cparams(("arbitrary",)),
    )(qkv, proj, alog, dtb)


def gdn_scan_bwd(name, qkv, proj, alog, dtb, s_in, xinv, do, dsm_in):
    t = qkv.shape[0]
    nc, nh = t // CHUNK, GDN_HEADS
    small_blk = AL_SMALL // LANE

    def body(qkv_ref, sm_ref, alog_ref, dtb_ref, sin_ref, x_ref, do_ref, dsmi_ref,
             dqkv_ref, dsm_ref, dalog_ref, ddtb_ref, dst_ref):
        ci = pl.program_id(0)

        @pl.when(ci == 0)
        def _():
            dst_ref[...] = jnp.zeros_like(dst_ref)

        xis = tuple(x_ref[0, h] for h in range(nh))

        def fn(qs, ks, vs, sm, alog_, dtb_, sts):
            os, s_new, _ = gdn_heads(qs, ks, vs, sm, alog_, dtb_, sts, xinvs=xis)
            return os, s_new

        _, vjp = jax.vjp(fn, tuple(qkv_ref[:, h * LANE:(h + 1) * LANE] for h in range(nh)),
                         tuple(qkv_ref[:, _GDN_K + h * LANE:_GDN_K + (h + 1) * LANE] for h in range(nh)),
                         tuple(qkv_ref[:, _GDN_V + h * LANE:_GDN_V + (h + 1) * LANE] for h in range(nh)),
                         sm_ref[...], alog_ref[...], dtb_ref[...], tuple(sin_ref[0, h] for h in range(nh)))
        dqs, dks, dvs, dsm, dalog, ddtb, dsts = vjp(
            (tuple(do_ref[:, h * LANE:(h + 1) * LANE] for h in range(nh)), tuple(dst_ref[h] for h in range(nh))))
        for h in range(nh):
            dqkv_ref[:, h * LANE:(h + 1) * LANE] = dqs[h]
            dqkv_ref[:, _GDN_K + h * LANE:_GDN_K + (h + 1) * LANE] = dks[h]
            dqkv_ref[:, _GDN_V + h * LANE:_GDN_V + (h + 1) * LANE] = dvs[h]
            dst_ref[h] = dsts[h]
        dsm_ref[...] = dsmi_ref[...] + dsm
        _acc(dalog_ref, dalog, ci == 0)
        _acc(ddtb_ref, ddtb, ci == 0)

    par = pl.BlockSpec((1, LANE), lambda ci: (0, 0))
    rev = lambda ci: nc - 1 - ci
    return pl.pallas_call(
        body, name=name,
        out_shape=[jax.ShapeDtypeStruct((t, GDN_QKV_DIM), F32), jax.ShapeDtypeStruct((t, LANE), F32),
                   jax.ShapeDtypeStruct((1, LANE), F32), jax.ShapeDtypeStruct((1, LANE), F32)],
        grid=(nc,),
        in_specs=[pl.BlockSpec((CHUNK, GDN_QKV_DIM), lambda ci: (rev(ci), 0)),
                  pl.BlockSpec((CHUNK, LANE), lambda ci: (rev(ci), small_blk)),
                  par, par,
                  pl.BlockSpec((1, nh, LANE, LANE), lambda ci: (rev(ci), 0, 0, 0)),
                  pl.BlockSpec((1, nh, CHUNK, CHUNK), lambda ci: (rev(ci), 0, 0, 0)),
                  pl.BlockSpec((CHUNK, GDN_HEADS * GDN_HEAD), lambda ci: (rev(ci), 0)),
                  pl.BlockSpec((CHUNK, LANE), lambda ci: (rev(ci), 0))],
        out_specs=[pl.BlockSpec((CHUNK, GDN_QKV_DIM), lambda ci: (rev(ci), 0)),
                   pl.BlockSpec((CHUNK, LANE), lambda ci: (rev(ci), 0)),
                   par, par],
        scratch_shapes=[pltpu.VMEM((nh, LANE, LANE), F32)],
        compiler_params=_cparams(("arbitrary",)),
    )(qkv, proj, alog, dtb, s_in, xinv, do, dsm_in)


def _row(arr, w, c0=0, moves=False):
    return (arr, "row", w, c0, moves)


def _par(arr, w, c0=0, moves=False):
    return (arr, "par", w, c0, moves)


def matmul_add(name, a, b, res):
    (m, k), (_, n) = a.shape, b.shape
    tm, tn, tk = _matmul_tiles(m, n, n, k, a.dtype.itemsize, b.dtype.itemsize, 4 + res.dtype.itemsize)
    nk = k // tk

    def body_acc(a_ref, b_ref, r_ref, o_ref, acc_ref):
        kk = pl.program_id(2)

        @pl.when(kk == 0)
        def _():
            acc_ref[...] = r_ref[...]

        acc_ref[...] += _dot(a_ref[...], b_ref[...])

        @pl.when(kk == nk - 1)
        def _():
            o_ref[...] = acc_ref[...]

    def body_one(a_ref, b_ref, r_ref, o_ref):
        o_ref[...] = r_ref[...] + _dot(a_ref[...], b_ref[...])

    body = body_one if nk == 1 else body_acc
    return pl.pallas_call(
        body, name=name,
        out_shape=jax.ShapeDtypeStruct((m, n), F32),
        grid=(m // tm, n // tn, nk),
        in_specs=[pl.BlockSpec((tm, tk), lambda i, j, kk: (i, kk)),
                  pl.BlockSpec((tk, tn), lambda i, j, kk: (kk, j)),
                  pl.BlockSpec((tm, tn), lambda i, j, kk: (i, j))],
        out_specs=pl.BlockSpec((tm, tn), lambda i, j, kk: (i, j)),
        scratch_shapes=[] if nk == 1 else [pltpu.VMEM((tm, tn), F32)],
        compiler_params=_cparams(("parallel", "parallel", "arbitrary")),
    )(a, b, res)


def layer_fwd(l, x, w):
    t = x.shape[0]
    rt = min(256, t)
    s = {"x": x}
    s["h"] = rowwise_fwd(f"norm_mix_l{l}", f_rmsnorm, t, rt, 1,
                         [_row(x, D_MODEL), _par(w["norm_mix_w"], D_MODEL)], [(D_MODEL, BF16)])[0]
    s["proj"] = matmul(f"in_proj_l{l}", s["h"], w["w_in"], "nn")
    s["xbc"] = conv_fwd(f"ssm_conv_l{l}", s["proj"], AL_XBC, SSM_CONV_DIM, w["ssm_conv_w"], w["ssm_conv_b"],
                        tile=min(512, t))
    s["qkv"] = conv_fwd(f"gdn_conv_l{l}", s["proj"], AL_QKV, GDN_QKV_DIM, w["gdn_conv_w"], w["gdn_conv_b"],
                        tile=min(512, t))
    s["y_scan"], s["ssd_sin"] = ssd_scan_fwd(f"ssd_scan_l{l}", s["xbc"], s["proj"], w["ssm_dt_bias"],
                                             w["ssm_a_log"], w["ssm_d"])
    s["o_scan"], s["gdn_sin"], s["gdn_x"] = gdn_scan_fwd(f"gdn_scan_l{l}", s["qkv"], s["proj"],
                                                         w["gdn_a_log"], w["gdn_dt_bias"])
    s["y_ssm"] = rowwise_fwd(f"ssd_post_l{l}", f_ssd_post, t, rt, 2,
                             [_row(s["y_scan"], 512, 0, True), _row(s["proj"], 512, AL_Z // 512, True),
                              _par(w["ssm_norm_w"], 512, 0, True)], [(512, BF16)])[0]
    s["y_gdn"] = rowwise_fwd(f"gdn_post_l{l}", f_gdn_post, t, rt, GDN_HEADS,
                             [_row(s["o_scan"], LANE, 0, True), _row(s["proj"], LANE, AL_GZ // LANE, True),
                              _par(w["gdn_norm_w"], LANE)], [(LANE, BF16)])[0]
    s["p1"] = matmul(f"proj_ssm_l{l}", s["y_ssm"], w["w_proj_ssm"], "nn")
    s["p2"] = matmul(f"proj_gdn_l{l}", s["y_gdn"], w["w_proj_gdn"], "nn")
    s["merged"] = rowwise_fwd(f"merge_l{l}", f_merge, t, rt, 2,
                              [_row(s["proj"], 512, AL_GS // 512, True), _row(s["p1"], 512, 0, True),
                               _row(s["proj"], 512, AL_GG // 512, True), _row(s["p2"], 512, 0, True)],
                              [(512, BF16)])[0]
    s["x1"] = matmul_add(f"out_proj_l{l}", s["merged"], w["w_out"], x)
    s["h2"] = rowwise_fwd(f"norm_ffn_l{l}", f_rmsnorm, t, rt, 1,
                          [_row(s["x1"], D_MODEL), _par(w["norm_ffn_w"], D_MODEL)], [(D_MODEL, BF16)])[0]
    s["gu"] = matmul(f"ffn_in_l{l}", s["h2"], w["w_ffn_in"], "nn")
    s["act"] = rowwise_fwd(f"swiglu_l{l}", f_swiglu, t, rt, FFN_HIDDEN // 256,
                           [_row(s["gu"], 256, 0, True), _row(s["gu"], 256, FFN_HIDDEN // 256, True)],
                           [(256, BF16)])[0]
    x2 = matmul_add(f"ffn_down_l{l}", s["act"], w["w_ffn_down"], s["x1"])
    return x2, s


IN_SHARD = IN_DIM // 4
IN_SHARD_PAD = 2304


def _shard_padded(pieces):
    t = pieces[0].shape[0]
    pieces = [p.astype(BF16) for p in pieces]
    zeros = jnp.zeros((t, IN_SHARD_PAD - IN_SHARD), BF16)
    out, start = [], 0
    bounds = [(j * IN_SHARD, (j + 1) * IN_SHARD) for j in range(4)]
    offs = []
    for p in pieces:
        offs.append((start, start + p.shape[1], p))
        start += p.shape[1]
    assert start == IN_DIM
    for lo, hi in bounds:
        for a, b, p in offs:
            s0, s1 = max(a, lo), min(b, hi)
            if s0 < s1:
                out.append(p if (s0, s1) == (a, b) else p[:, s0 - a:s1 - a])
        out.append(zeros)
    return jnp.concatenate(out, axis=1)


def layer_bwd(l, dx2, w, s, gbuf):
    t = dx2.shape[0]
    rt = min(256, t)
    ct = min(512, t)
    g = {}
    dact = matmul(f"ffn_down_dx_l{l}", dx2, w["w_ffn_down"], "nt")
    g["w_ffn_down"] = matmul(f"ffn_down_dw_l{l}", s["act"], dx2, "tn", stack=(l, gbuf.get("w_ffn_down")))
    nf = FFN_HIDDEN // 256
    dgate, dup = rowwise_bwd(f"swiglu_bwd_l{l}", f_swiglu, t, rt, nf,
                             [_row(s["gu"], 256, 0, True), _row(s["gu"], 256, nf, True)], [True, True],
                             [_row(dact, 256, 0, True)], row_dtypes={0: BF16, 1: BF16})
    dgu = jnp.concatenate([dgate, dup], axis=1)
    dh2 = matmul(f"ffn_in_dx_l{l}", dgu, w["w_ffn_in"], "nt")
    g["w_ffn_in"] = matmul(f"ffn_in_dw_l{l}", s["h2"], dgu, "tn", chip_major=True,
                           stack=(l, gbuf.get("w_ffn_in")))
    dx1, g["norm_ffn_w"] = rowwise_bwd(f"norm_ffn_bwd_l{l}", f_rmsnorm, t, rt, 1,
                                       [_row(s["x1"], D_MODEL), _par(w["norm_ffn_w"], D_MODEL)], [True, True],
                                       [_row(dh2, D_MODEL)], addends={0: _row(dx2, D_MODEL)})
    dmerged = matmul(f"out_proj_dx_l{l}", dx1, w["w_out"], "nt")
    g["w_out"] = matmul(f"out_proj_dw_l{l}", s["merged"], dx1, "tn", stack=(l, gbuf.get("w_out")))
    dgs, dp1, dgg, dp2 = rowwise_bwd(
        f"merge_bwd_l{l}", f_merge, t, rt, 2,
        [_row(s["proj"], 512, AL_GS // 512, True), _row(s["p1"], 512, 0, True),
         _row(s["proj"], 512, AL_GG // 512, True), _row(s["p2"], 512, 0, True)], [True] * 4,
        [_row(dmerged, 512, 0, True)], row_dtypes={0: BF16, 1: BF16, 2: BF16, 3: BF16})
    dy_ssm = matmul(f"proj_ssm_dx_l{l}", dp1, w["w_proj_ssm"], "nt")
    g["w_proj_ssm"] = matmul(f"proj_ssm_dw_l{l}", s["y_ssm"], dp1, "tn", stack=(l, gbuf.get("w_proj_ssm")))
    dy_gdn = matmul(f"proj_gdn_dx_l{l}", dp2, w["w_proj_gdn"], "nt")
    g["w_proj_gdn"] = matmul(f"proj_gdn_dw_l{l}", s["y_gdn"], dp2, "tn", stack=(l, gbuf.get("w_proj_gdn")))
    dy_scan, dz, g["ssm_norm_w"] = rowwise_bwd(
        f"ssd_post_bwd_l{l}", f_ssd_post, t, rt, 2,
        [_row(s["y_scan"], 512, 0, True), _row(s["proj"], 512, AL_Z // 512, True),
         _par(w["ssm_norm_w"], 512, 0, True)], [True] * 3, [_row(dy_ssm, 512, 0, True)], row_dtypes={1: BF16})
    dxbc_act, dsm, g["ssm_dt_bias"], g["ssm_a_log"], g["ssm_d"] = ssd_scan_bwd(
        f"ssd_scan_bwd_l{l}", s["xbc"], s["proj"], w["ssm_dt_bias"], w["ssm_a_log"], w["ssm_d"],
        s["ssd_sin"], dy_scan)
    dpre, g["ssm_conv_w"], g["ssm_conv_b"] = conv_bwd_pre(
        f"ssm_conv_bwd_l{l}", s["proj"], AL_XBC, SSM_CONV_DIM, w["ssm_conv_w"], w["ssm_conv_b"], dxbc_act, tile=ct)
    dxbc = conv_bwd_in(f"ssm_conv_din_l{l}", dpre, w["ssm_conv_w"], tile=ct)
    do_scan, dgz, g["gdn_norm_w"] = rowwise_bwd(
        f"gdn_post_bwd_l{l}", f_gdn_post, t, rt, GDN_HEADS,
        [_row(s["o_scan"], LANE, 0, True), _row(s["proj"], LANE, AL_GZ // LANE, True),
         _par(w["gdn_norm_w"], LANE)], [True] * 3, [_row(dy_gdn, LANE, 0, True)], row_dtypes={1: BF16})
    dqkv_act, dsm, g["gdn_a_log"], g["gdn_dt_bias"] = gdn_scan_bwd(
        f"gdn_scan_bwd_l{l}", s["qkv"], s["proj"], w["gdn_a_log"], w["gdn_dt_bias"], s["gdn_sin"],
        s["gdn_x"], do_scan, dsm)
    dpre, g["gdn_conv_w"], _ = conv_bwd_pre(
        f"gdn_conv_bwd_l{l}", s["proj"], AL_QKV, GDN_QKV_DIM, w["gdn_conv_w"], w["gdn_conv_b"], dqkv_act, tile=ct)
    dqkv = conv_bwd_in(f"gdn_conv_din_l{l}", dpre, w["gdn_conv_w"], tile=ct)
    dproj = _shard_padded([dz, dxbc, dsm[:, SM_DT:SM_DT + SSM_HEADS], dqkv, dgz,
                           dsm[:, SM_A:SM_A + 2 * GDN_HEADS], dgs, dgg])
    dh = matmul(f"in_proj_dx_l{l}", dproj, w["w_in_sp"], "nt")
    g["w_in"] = matmul(f"in_proj_dw_l{l}", s["h"], dproj, "tn", chip_major=True, stack=(l, gbuf.get("w_in")))
    dx0, g["norm_mix_w"] = rowwise_bwd(f"norm_mix_bwd_l{l}", f_rmsnorm, t, rt, 1,
                                       [_row(s["x"], D_MODEL), _par(w["norm_mix_w"], D_MODEL)], [True, True],
                                       [_row(dh, D_MODEL)], addends={0: _row(dx1, D_MODEL)})
    return dx0, g


def _align_w_in(w):
    pad = jnp.zeros((w.shape[0], AL_DIM - AL_SMALL - 32), w.dtype)
    return jnp.concatenate([w[:, 0:2560], w[:, 2576:6672], w[:, 6688:8736],
                            w[:, 2560:2576], w[:, 6672:6688], pad], axis=1)


def _pad_lane(v):
    return jnp.zeros((1, LANE), F32).at[0, :v.shape[0]].set(v)


def local_step(x, target, full):
    ws = []
    for l in range(DEPTH):
        ws.append({
            "norm_mix_w": full["norm_mix_w"][l][None], "w_in": _align_w_in(full["w_in"][l]),
            "w_in_sp": full["w_in_sp"][l],
            "ssm_conv_w": full["ssm_conv_w"][l], "ssm_conv_b": full["ssm_conv_b"][l][None],
            "ssm_dt_bias": _pad_lane(full["ssm_dt_bias"][l]), "ssm_a_log": _pad_lane(full["ssm_a_log"][l]),
            "ssm_d": _pad_lane(full["ssm_d"][l]), "ssm_norm_w": full["ssm_norm_w"][l][None],
            "gdn_conv_w": full["gdn_conv_w"][l], "gdn_conv_b": jnp.zeros((1, GDN_QKV_DIM), F32),
            "gdn_a_log": _pad_lane(full["gdn_a_log"][l]), "gdn_dt_bias": _pad_lane(full["gdn_dt_bias"][l]),
            "gdn_norm_w": full["gdn_norm_w"][l][None],
            "w_proj_ssm": full["w_proj_ssm"][l], "w_proj_gdn": full["w_proj_gdn"][l], "w_out": full["w_out"][l],
            "norm_ffn_w": full["norm_ffn_w"][l][None], "w_ffn_in": full["w_ffn_in"][l],
            "w_ffn_down": full["w_ffn_down"][l],
        })
    saved = []
    h = x
    for l in range(DEPTH):
        h, s = layer_fwd(l, h, ws[l])
        saved.append(s)
    loss, dx, g_final = final_loss("final_loss", h, target, full["final_norm_w"][None], tile=min(256, x.shape[0]))
    per_layer = [None] * DEPTH
    gbuf = {}
    for l in reversed(range(DEPTH)):
        dx, per_layer[l] = layer_bwd(l, dx, ws[l], saved[l], gbuf)
        gbuf = {n: per_layer[l][n] for n, _ in BIG}
    grads = {"final_norm_w": g_final[0], **gbuf}
    for name in per_layer[0]:
        if name in gbuf:
            continue
        rows = []
        for l in range(DEPTH):
            gl = per_layer[l][name]
            if name in ("ssm_dt_bias", "ssm_a_log", "ssm_d"):
                gl = gl[0, :SSM_HEADS]
            elif name in ("gdn_a_log", "gdn_dt_bias"):
                gl = gl[0, :GDN_HEADS]
            elif name in ("norm_mix_w", "ssm_conv_b", "ssm_norm_w", "gdn_norm_w", "norm_ffn_w"):
                gl = gl[0]
            rows.append(gl)
        grads[name] = jnp.stack(rows)
    return loss, dx, grads


MESH = pl.DeviceIdType.MESH
HBM = pl.BlockSpec(memory_space=pltpu.HBM)
N_DEV = 8


def _pos():
    return lax.axis_index("x"), lax.axis_index("y"), lax.axis_index("c")


def _rcopy(src, dst, send_sem, recv_sem, dev):
    return pltpu.make_async_remote_copy(src_ref=src, dst_ref=dst, send_sem=send_sem, recv_sem=recv_sem,
                                        device_id=dev, device_id_type=MESH)


RELATIONS = (2, 1, 3)


def _related_chip(x, y, mask):
    return (1 - x if mask & 2 else x, 1 - y if mask & 1 else y)


def weights_gather(name, bufs):
    n = len(bufs)

    def body(*refs):
        outs, send_sems, recv_sems = refs[n:2 * n], refs[2 * n], refs[2 * n + 1]
        x, y, c = _pos()
        sib = (x, y, 1 - c)
        sends = []
        for i, a in enumerate(outs):
            for k, m in enumerate(RELATIONS):
                px, py = _related_chip(x, y, m)
                cp = _rcopy(a.at[0, c], a.at[m, c], send_sems.at[6 * i + k], recv_sems.at[6 * i + k], (px, py, c))
                cp.start()
                sends.append(cp)
        for i, a in enumerate(outs):
            for k, m in enumerate(RELATIONS):
                px, py = _related_chip(x, y, m)
                _rcopy(a.at[0, c], a.at[m, c], send_sems.at[6 * i + k], recv_sems.at[6 * i + k],
                       (px, py, c)).wait_recv()
                fw = _rcopy(a.at[m, c], a.at[m, c], send_sems.at[6 * i + 3 + k], recv_sems.at[6 * i + 3 + k], sib)
                fw.start()
                sends.append(fw)
        for i, a in enumerate(outs):
            for k, m in enumerate(RELATIONS):
                _rcopy(a.at[m, 1 - c], a.at[m, 1 - c], send_sems.at[6 * i + 3 + k], recv_sems.at[6 * i + 3 + k],
                       sib).wait_recv()
        for cp in sends:
            cp.wait_send()

    return pl.pallas_call(
        body, name=name, out_shape=[jax.ShapeDtypeStruct(b.shape, b.dtype) for b in bufs],
        in_specs=[HBM] * n, out_specs=[HBM] * n,
        input_output_aliases={i: i for i in range(n)},
        scratch_shapes=[pltpu.SemaphoreType.DMA((6 * n,)), pltpu.SemaphoreType.DMA((6 * n,))],
    )(*bufs)


def pair_swap(name, gs):
    n = len(gs)

    def body(*refs):
        srcs, outs, send_sems, recv_sems = refs[:n], refs[n:2 * n], refs[2 * n], refs[2 * n + 1]
        x, y, c = _pos()
        cps = [_rcopy(s.at[1 - c], o, send_sems.at[i], recv_sems.at[i], (x, y, 1 - c))
               for i, (s, o) in enumerate(zip(srcs, outs))]
        for cp in cps:
            cp.start()
        for cp in cps:
            cp.wait()

    return pl.pallas_call(
        body, name=name, out_shape=[jax.ShapeDtypeStruct(g.shape[1:], g.dtype) for g in gs],
        in_specs=[HBM] * n, out_specs=[HBM] * n,
        scratch_shapes=[pltpu.SemaphoreType.DMA((n,)), pltpu.SemaphoreType.DMA((n,))],
    )(*gs)


def chip_scatter(name, ss):
    n = len(ss)

    def body(*refs):
        srcs, outs, send_sems, recv_sems = refs[:n], refs[n:2 * n], refs[2 * n], refs[2 * n + 1]
        x, y, c = _pos()
        sends = []
        for i, (s, o) in enumerate(zip(srcs, outs)):
            for k, m in enumerate(RELATIONS):
                px, py = _related_chip(x, y, m)
                cp = _rcopy(s.at[2 * px + py], o.at[k], send_sems.at[3 * i + k], recv_sems.at[3 * i + k],
                            (px, py, c))
                cp.start()
                sends.append(cp)
        for cp in sends:
            cp.wait()

    return pl.pallas_call(
        body, name=name, out_shape=[jax.ShapeDtypeStruct((3,) + s.shape[1:], s.dtype) for s in ss],
        in_specs=[HBM] * n, out_specs=[HBM] * n,
        scratch_shapes=[pltpu.SemaphoreType.DMA((3 * n,)), pltpu.SemaphoreType.DMA((3 * n,))],
    )(*ss)


def pair_share(name, bufs):
    n = len(bufs)

    def body(*refs):
        outs, send_sems, recv_sems = refs[n:2 * n], refs[2 * n], refs[2 * n + 1]
        x, y, c = _pos()
        sends = []
        for i, o in enumerate(outs):
            cp = _rcopy(o.at[c], o.at[c], send_sems.at[i], recv_sems.at[i], (x, y, 1 - c))
            cp.start()
            sends.append(cp)
        for i, o in enumerate(outs):
            _rcopy(o.at[1 - c], o.at[1 - c], send_sems.at[i], recv_sems.at[i], (x, y, 1 - c)).wait_recv()
        for cp in sends:
            cp.wait_send()

    return pl.pallas_call(
        body, name=name, out_shape=[jax.ShapeDtypeStruct(b.shape, b.dtype) for b in bufs],
        in_specs=[HBM] * n, out_specs=[HBM] * n,
        input_output_aliases={i: i for i in range(n)},
        scratch_shapes=[pltpu.SemaphoreType.DMA((n,)), pltpu.SemaphoreType.DMA((n,))],
    )(*bufs)


def all_allgather(name, buf):
    r, cd = buf.shape

    def body(src, out, send_sems, recv_sems, lsem):
        x, y, c = _pos()
        me = 4 * x + 2 * y + c
        local = pltpu.make_async_copy(src, out.at[me], lsem)
        local.start()

        def peer(mask):
            px = 1 - x if mask & 4 else x
            py = 1 - y if mask & 2 else y
            pc = 1 - c if mask & 1 else c
            return px, py, pc

        sends = []
        for mask in range(1, N_DEV):
            cp = _rcopy(src, out.at[me], send_sems.at[mask - 1], recv_sems.at[mask - 1], peer(mask))
            cp.start()
            sends.append(cp)
        for mask in range(1, N_DEV):
            px, py, pc = peer(mask)
            _rcopy(src, out.at[4 * px + 2 * py + pc], send_sems.at[mask - 1], recv_sems.at[mask - 1],
                   (px, py, pc)).wait_recv()
        for cp in sends:
            cp.wait_send()
        local.wait()

    return pl.pallas_call(
        body, name=name, out_shape=jax.ShapeDtypeStruct((N_DEV, r, cd), buf.dtype),
        in_specs=[HBM], out_specs=HBM,
        scratch_shapes=[pltpu.SemaphoreType.DMA((N_DEV - 1,)), pltpu.SemaphoreType.DMA((N_DEV - 1,)),
                        pltpu.SemaphoreType.DMA(())],
    )(buf)


ELEMENTWISE_BLOCK_BYTES = 2 << 20


def _row_block(rows, cols):
    for cand in (1024, 512, 256, 128, 64, 32, 16):
        if rows % cand == 0 and cand * cols * 4 <= ELEMENTWISE_BLOCK_BYTES:
            return cand
    return rows


def chip_sum(name, s, r, me, c):
    _, a, b = s.shape
    tr = _row_block(a, b)

    def body(idx_ref, s_ref, r_ref, o_ref):
        del idx_ref
        acc = s_ref[...].astype(F32)
        for k in range(3):
            acc = acc + r_ref[k].astype(F32)
        o_ref[...] = acc

    return pl.pallas_call(
        body, name=name, out_shape=jax.ShapeDtypeStruct((2, a, b), F32),
        grid_spec=pltpu.PrefetchScalarGridSpec(
            num_scalar_prefetch=1, grid=(a // tr,),
            in_specs=[pl.BlockSpec((None, tr, b), lambda i, idx: (idx[0], i, 0)),
                      pl.BlockSpec((3, tr, b), lambda i, idx: (0, i, 0))],
            out_specs=pl.BlockSpec((None, tr, b), lambda i, idx: (idx[1], i, 0))),
        compiler_params=_cparams(("arbitrary",)),
    )(jnp.stack([me, c]).astype(jnp.int32), s, r)


def pair_add(name, p, recv, c):
    _, nj, rh, cd = p.shape
    tr = _row_block(rh, cd)

    def body(c_ref, p_ref, r_ref, o_ref):
        del c_ref
        o_ref[...] = (p_ref[0] + r_ref[...]).astype(o_ref.dtype)

    return pl.pallas_call(
        body, name=name, out_shape=jax.ShapeDtypeStruct((nj, rh, cd), BF16),
        grid_spec=pltpu.PrefetchScalarGridSpec(
            num_scalar_prefetch=1, grid=(nj, rh // tr),
            in_specs=[pl.BlockSpec((1, 1, tr, cd), lambda j, i, c_ref: (c_ref[0], j, i, 0)),
                      pl.BlockSpec((1, tr, cd), lambda j, i, c_ref: (j, i, 0))],
            out_specs=pl.BlockSpec((1, tr, cd), lambda j, i, c_ref: (j, i, 0))),
        compiler_params=_cparams(("arbitrary", "arbitrary")),
    )(jnp.reshape(c, (1,)).astype(jnp.int32), p, recv)


def slab_sum(name, a):
    n, r, cd = a.shape
    tr = _pick(r, (256, 128, 64, 32, 16, 8))

    def body(a_ref, o_ref):
        acc = a_ref[0].astype(F32)
        for j in range(1, n):
            acc = acc + a_ref[j].astype(F32)
        o_ref[...] = acc

    return pl.pallas_call(
        body, name=name, out_shape=jax.ShapeDtypeStruct((r, cd), F32),
        grid=(r // tr,),
        in_specs=[pl.BlockSpec((n, tr, cd), lambda i: (0, i, 0))],
        out_specs=pl.BlockSpec((tr, cd), lambda i: (i, 0)),
        compiler_params=_cparams(("arbitrary",)),
    )(a)


ADAM_C1 = 1.0 - ADAM_B1 ** ADAM_STEP
ADAM_C2 = 1.0 - ADAM_B2 ** ADAM_STEP


def adamw(name, w, g, m, v):
    r, cd = w.shape
    tr = r
    for cand in (512, 256, 128, 64, 32, 16, 8):
        if r % cand == 0 and cand * cd * 4 <= (1 << 20):
            tr = cand
            break

    def body(w_ref, g_ref, m_ref, v_ref, d_ref, nm_ref, nv_ref):
        gv = g_ref[...]
        nm = ADAM_B1 * m_ref[...] + (1.0 - ADAM_B1) * gv
        nv = ADAM_B2 * v_ref[...] + (1.0 - ADAM_B2) * (gv * gv)
        m_hat = nm / ADAM_C1
        v_hat = nv / ADAM_C2
        d_ref[...] = -ADAM_LR * (m_hat / (jnp.sqrt(v_hat) + ADAM_EPS) + ADAM_WD * w_ref[...])
        nm_ref[...] = nm
        nv_ref[...] = nv

    spec = pl.BlockSpec((tr, cd), lambda i: (i, 0))
    sd = jax.ShapeDtypeStruct((r, cd), F32)
    return pl.pallas_call(
        body, name=name, out_shape=[sd, sd, sd], grid=(r // tr,),
        in_specs=[spec] * 4, out_specs=[spec] * 3,
        compiler_params=_cparams(("arbitrary",)),
    )(w, g, m, v)


WEIGHTS = ("norm_mix_w", "w_in", "ssm_conv_w", "ssm_conv_b", "ssm_dt_bias", "ssm_a_log", "ssm_d", "ssm_norm_w",
           "gdn_conv_w", "gdn_a_log", "gdn_dt_bias", "gdn_norm_w", "w_proj_ssm", "w_proj_gdn", "w_out",
           "norm_ffn_w", "w_ffn_in", "w_ffn_down", "final_norm_w")
BIG = (("w_in", 2), ("w_proj_ssm", 1), ("w_proj_gdn", 1), ("w_out", 1), ("w_ffn_in", 2), ("w_ffn_down", 1))
CONVW = (("ssm_conv_w", 2), ("gdn_conv_w", 2))
SHARDED = BIG + CONVW
SMALL = tuple(n for n in WEIGHTS if n not in dict(SHARDED))


def _unpack(buf, shapes, lead=()):
    flat = buf.reshape(lead + (-1,))
    out, o = [], 0
    for shp in shapes:
        n = math.prod(shp)
        out.append(flat[..., o:o + n].reshape(lead + tuple(shp)))
        o += n
    return out


def _pack_rows(arrs, lead=()):
    nl = len(lead)
    flat = jnp.concatenate([a.reshape(lead + (-1,)) for a in arrs], axis=nl)
    n = flat.shape[nl]
    rows = -(-n // (8 * LANE)) * 8
    flat = jnp.pad(flat, [(0, 0)] * nl + [(0, rows * LANE - n)])
    return flat.reshape(lead + (rows, LANE))


def _slot_buffer(shard):
    return jnp.pad(shard[None], [(0, N_CHIPS - 1)] + [(0, 0)] * shard.ndim)


def kernel(x, norm_mix_w, w_in, ssm_conv_w, ssm_conv_b, ssm_dt_bias, ssm_a_log, ssm_d, ssm_norm_w, gdn_conv_w, gdn_a_log, gdn_dt_bias, gdn_norm_w, w_proj_ssm, w_proj_gdn, w_out, norm_ffn_w, w_ffn_in, w_ffn_down, final_norm_w, loss_target, m_norm_mix_w, m_w_in, m_ssm_conv_w, m_ssm_conv_b, m_ssm_dt_bias, m_ssm_a_log, m_ssm_d, m_ssm_norm_w, m_gdn_conv_w, m_gdn_a_log, m_gdn_dt_bias, m_gdn_norm_w, m_w_proj_ssm, m_w_proj_gdn, m_w_out, m_norm_ffn_w, m_w_ffn_in, m_w_ffn_down, m_final_norm_w, v_norm_mix_w, v_w_in, v_ssm_conv_w, v_ssm_conv_b, v_ssm_dt_bias, v_ssm_a_log, v_ssm_d, v_ssm_norm_w, v_gdn_conv_w, v_gdn_a_log, v_gdn_dt_bias, v_gdn_norm_w, v_w_proj_ssm, v_w_proj_gdn, v_w_out, v_norm_ffn_w, v_w_ffn_in, v_w_ffn_down, v_final_norm_w):
    wl = (norm_mix_w, w_in, ssm_conv_w, ssm_conv_b, ssm_dt_bias, ssm_a_log, ssm_d, ssm_norm_w, gdn_conv_w,
          gdn_a_log, gdn_dt_bias, gdn_norm_w, w_proj_ssm, w_proj_gdn, w_out, norm_ffn_w, w_ffn_in, w_ffn_down,
          final_norm_w)
    ml = (m_norm_mix_w, m_w_in, m_ssm_conv_w, m_ssm_conv_b, m_ssm_dt_bias, m_ssm_a_log, m_ssm_d, m_ssm_norm_w,
          m_gdn_conv_w, m_gdn_a_log, m_gdn_dt_bias, m_gdn_norm_w, m_w_proj_ssm, m_w_proj_gdn, m_w_out,
          m_norm_ffn_w, m_w_ffn_in, m_w_ffn_down, m_final_norm_w)
    vl = (v_norm_mix_w, v_w_in, v_ssm_conv_w, v_ssm_conv_b, v_ssm_dt_bias, v_ssm_a_log, v_ssm_d, v_ssm_norm_w,
          v_gdn_conv_w, v_gdn_a_log, v_gdn_dt_bias, v_gdn_norm_w, v_w_proj_ssm, v_w_proj_gdn, v_w_out,
          v_norm_ffn_w, v_w_ffn_in, v_w_ffn_down, v_final_norm_w)
    w = dict(zip(WEIGHTS, wl))
    m = dict(zip(WEIGHTS, ml))
    v = dict(zip(WEIGHTS, vl))
    x_pos, y_pos, c = _pos()
    me = 2 * x_pos + y_pos
    big = [n for n, _ in BIG]

    shards = [w[n].astype(BF16) for n in big]
    shards[0] = jnp.pad(shards[0], ((0, 0), (0, 0), (0, IN_SHARD_PAD - IN_SHARD)))
    conv_shapes = [w[n].shape[1:] for n, _ in CONVW]
    conv_pack = _pack_rows([w[n] for n, _ in CONVW], lead=(DEPTH,))
    gathered = weights_gather("gather_w", [_slot_buffer(s) for s in shards + [conv_pack]])
    by_chip = [[lax.dynamic_index_in_dim(g_, jnp.bitwise_xor(me, j), 0, keepdims=False) for j in range(N_CHIPS)]
               for g_ in gathered]
    full = {n: w[n] for n in SMALL}
    for i, (n, axis) in enumerate(BIG):
        full[n] = jnp.concatenate(by_chip[i], axis=axis)
    full["w_in_sp"] = full["w_in"]
    full["w_in"] = jnp.concatenate([p[:, :, :IN_SHARD] for p in by_chip[0]], axis=2)
    conv_parts = [_unpack(by_chip[-1][j], conv_shapes, lead=(DEPTH,)) for j in range(N_CHIPS)]
    for i, (n, axis) in enumerate(CONVW):
        full[n] = jnp.concatenate([conv_parts[j][i] for j in range(N_CHIPS)], axis=axis)

    loss_part, grad_x, grads = local_step(x[0], loss_target[0], full)

    by4 = [grads[n].reshape((DEPTH, N_CHIPS, -1) + grads[n].shape[-1:]) for n in big]
    from_pair = pair_swap("grad_pair_swap", by4)
    chip_part = [pair_add(f"grad_pair_add_{n}", g_, r_, c) for n, g_, r_ in zip(big, by4, from_pair)]
    from_chips = chip_scatter("grad_chip_scatter", chip_part)
    halves = [chip_sum(f"grad_chip_sum_{n}", s_, r_, me, c) for n, s_, r_ in zip(big, chip_part, from_chips)]
    reduced = pair_share("grad_pair_share", halves)
    g_sharded = dict(zip(big, reduced))
    g_sharded["w_in"] = g_sharded["w_in"][:, :, :IN_SHARD]

    small_names = list(SMALL) + [n for n, _ in CONVW]
    small_all = all_allgather("gather_small", _pack_rows([grads[n] for n in small_names] + [loss_part[0, :1]]))
    small_sum = slab_sum("small_sum", small_all)
    small_vals = _unpack(small_sum, [grads[n].shape for n in small_names] + [(1,)])
    g_small = dict(zip(small_names, small_vals[:-1]))
    loss = small_vals[-1].reshape(())
    for n, axis in CONVW:
        size = w[n].shape[axis]
        g_sharded[n] = lax.dynamic_slice_in_dim(g_small.pop(n), me * size, size, axis=axis)

    out_g, out_d, out_m, out_v = {}, {}, {}, {}
    for n, _ in SHARDED:
        shp = w[n].shape
        two = lambda a: a.reshape(-1, shp[-1])
        d_, m_, v_ = adamw(f"adamw_{n}", two(w[n]), two(g_sharded[n]), two(m[n]), two(v[n]))
        out_g[n], out_d[n], out_m[n], out_v[n] = g_sharded[n], d_.reshape(shp), m_.reshape(shp), v_.reshape(shp)
    d_, m_, v_ = adamw("adamw_small", *[_pack_rows([d[n] for n in SMALL]) for d in (w, g_small, m, v)])
    small_shapes = [w[n].shape for n in SMALL]
    for n, dd, mm, vv in zip(SMALL, _unpack(d_, small_shapes), _unpack(m_, small_shapes), _unpack(v_, small_shapes)):
        out_g[n], out_d[n], out_m[n], out_v[n] = g_small[n], dd, mm, vv

    return (loss, grad_x[None], *[out_g[n] for n in WEIGHTS], *[out_d[n] for n in WEIGHTS],
            *[out_m[n] for n in WEIGHTS], *[out_v[n] for n in WEIGHTS])
```

```python
import math

import jax
import jax.numpy as jnp
from jax import lax
from jax.experimental import pallas as pl
from jax.experimental.pallas import tpu as pltpu

F32 = jnp.float32
BF16 = jnp.bfloat16

D_MODEL = 1024
DEPTH = 2
SSM_HEADS = 16
SSM_HEAD_DIM = 64
SSM_D_INNER = 1024
SSM_STATE = 128
SSM_CONV_DIM = 1536
GDN_HEADS = 8
GDN_HEAD = 128
GDN_QKV_DIM = 3072
CONV_K = 4
CHUNK = 64
FFN_HIDDEN = 2816
EPS = 1e-6
IN_DIM = 8736

ADAM_LR = 0.001
ADAM_B1 = 0.9
ADAM_B2 = 0.999
ADAM_EPS = 1e-08
ADAM_WD = 0.01
ADAM_STEP = 10

LANE = 128
NEG_BIG = -1e30
VMEM_LIMIT = 56 * 1024 * 1024

AL_Z, AL_XBC, AL_QKV, AL_GZ, AL_GS, AL_GG, AL_SMALL, AL_DIM = 0, 1024, 2560, 5632, 6656, 7680, 8704, 9216
SM_DT, SM_A, SM_B = 0, 16, 24

HI = lax.Precision.HIGHEST
NN = (((1,), (0,)), ((), ()))
NT = (((1,), (1,)), ((), ()))
TN = (((0,), (0,)), ((), ()))


def _cparams(sem):
    return pltpu.CompilerParams(dimension_semantics=sem, vmem_limit_bytes=VMEM_LIMIT)


def _pick(n, prefs):
    for p in prefs:
        if n % p == 0:
            return p
    return n


MATMUL_VMEM_BUDGET = 40 << 20
N_CHIPS = 4


def _matmul_tiles(m, n, n_dom, k, a_item, b_item, o_item):
    best = None
    def cands(dim, cap):
        return [c for c in range(LANE, min(dim, cap) + 1, LANE) if dim % c == 0] or [dim]

    tms, tns, tks = cands(m, 2048), cands(n_dom, 2304), cands(k, 1 << 30)
    for tm in tms:
        for tn in tns:
            for tk in tks:
                nk = k // tk
                vmem = (2 * (tm * tk * a_item + tk * tn * b_item + tm * tn * o_item) + tm * tn * 4 * (2 if nk > 1 else 1)
                        + (tm * tk * 2 if a_item > 2 else 0) + (tk * tn * 2 if b_item > 2 else 0))
                if vmem > MATMUL_VMEM_BUDGET:
                    continue
                traffic = m * k * a_item * (1 if nk == 1 else n // tn) + k * n * b_item * (m // tm)
                steps = (m // tm) * (n // tn) * nk
                key = (traffic, steps)
                if best is None or key < best[0]:
                    best = (key, (tm, tn, tk))
    return best[1]


def _dot(a, b, dims=NN):
    return lax.dot_general(a.astype(BF16), b.astype(BF16), dims, preferred_element_type=F32)


def _dot_hi(a, b, dims=NN):
    return lax.dot_general(a, b, dims, precision=HI, preferred_element_type=F32)


def _sigmoid(x):
    return jax.nn.sigmoid(x)


def _silu(x):
    return x * _sigmoid(x)


def _softplus(x):
    return jnp.maximum(x, 0.0) + jnp.log1p(jnp.exp(-jnp.abs(x)))


def matmul(name, a, b, mode, out_dtype=F32, chip_major=False, stack=None):
    if mode == "nn":
        (m, k), (k2, n) = a.shape, b.shape
    elif mode == "nt":
        (m, k), (n, k2) = a.shape, b.shape
    else:
        (k, m), (k2, n) = a.shape, b.shape
    assert k == k2, (a.shape, b.shape, mode)
    shard = n // N_CHIPS if chip_major else n
    tm, tn, tk = _matmul_tiles(m, n, shard, k, a.dtype.itemsize, b.dtype.itemsize, jnp.dtype(out_dtype).itemsize)
    if chip_major:
        per = shard // tn
        base_shape, base_blk = (N_CHIPS, m, shard), (None, tm, tn)
        base_idx = lambda i, j: (j // per, i, j % per)
    else:
        base_shape, base_blk = (m, n), (tm, tn)
        base_idx = lambda i, j: (i, j)
    nk = k // tk
    dims = {"nn": NN, "nt": NT, "tn": TN}[mode]

    def body_acc(a_ref, b_ref, o_ref, acc_ref):
        kk = pl.program_id(2)

        @pl.when(kk == 0)
        def _():
            acc_ref[...] = jnp.zeros_like(acc_ref)

        acc_ref[...] += _dot(a_ref[...], b_ref[...], dims)

        @pl.when(kk == nk - 1)
        def _():
            o_ref[...] = acc_ref[...].astype(o_ref.dtype)

    def body_one(a_ref, b_ref, o_ref):
        o_ref[...] = _dot(a_ref[...], b_ref[...], dims).astype(o_ref.dtype)

    compute = body_one if nk == 1 else body_acc
    if mode == "tn":
        a_spec = pl.BlockSpec((tk, tm), lambda i, j, kk: (kk, i))
    else:
        a_spec = pl.BlockSpec((tm, tk), lambda i, j, kk: (i, kk))
    if mode == "nt":
        b_spec = pl.BlockSpec((tn, tk), lambda i, j, kk: (j, kk))
    else:
        b_spec = pl.BlockSpec((tk, tn), lambda i, j, kk: (kk, j))
    in_specs, operands, aliases, body = [a_spec, b_spec], [a, b], {}, compute
    if stack is None:
        out_shape, out_blk, out_idx = base_shape, base_blk, (lambda i, j, kk: base_idx(i, j))
    else:
        layer, buf = stack
        out_shape, out_blk = (DEPTH,) + base_shape, (None,) + base_blk
        out_idx = lambda i, j, kk: (layer,) + base_idx(i, j)
        if buf is not None:
            assert buf.shape == out_shape and buf.dtype == out_dtype
            in_specs.append(pl.BlockSpec(memory_space=pl.ANY))
            operands.append(buf)
            aliases = {2: 0}

            def body(a_ref, b_ref, buf_ref, *rest):
                del buf_ref
                compute(a_ref, b_ref, *rest)

    return pl.pallas_call(
        body, name=name,
        out_shape=jax.ShapeDtypeStruct(out_shape, out_dtype),
        grid=(m // tm, n // tn, nk),
        in_specs=in_specs,
        out_specs=pl.BlockSpec(out_blk, out_idx),
        scratch_shapes=[] if nk == 1 else [pltpu.VMEM((tm, tn), F32)],
        input_output_aliases=aliases,
        compiler_params=_cparams(("parallel", "parallel", "arbitrary")),
    )(*operands)


def _row_map(c0, moves):
    return (lambda j, i: (i, c0 + j)) if moves else (lambda j, i: (i, c0))


def _par_map(c0, moves):
    return (lambda j, i: (0, c0 + j)) if moves else (lambda j, i: (0, c0))


def _in_spec(op, tile):
    _, kind, w, c0, moves = op
    if kind == "row":
        return pl.BlockSpec((tile, w), _row_map(c0, moves))
    return pl.BlockSpec((1, w), _par_map(c0, moves))


ROW_BLOCK_ELEMS = 1 << 18


def _row_tile(t, tile, ops):
    width = max(op[2] for op in ops if op[1] == "row")
    return min(t, max(tile, ROW_BLOCK_ELEMS // width))


def rowwise_fwd(name, fn, t, tile, ncol, ins, outs):
    n_in = len(ins)
    tile = _row_tile(t, tile, ins)

    def body(*refs):
        vals = [r[...].astype(F32) for r in refs[:n_in]]
        res = fn(*vals)
        if not isinstance(res, (tuple, list)):
            res = (res,)
        for r, v in zip(refs[n_in:], res):
            r[...] = v.astype(r.dtype)

    res = pl.pallas_call(
        body, name=name,
        out_shape=[jax.ShapeDtypeStruct((t, w * ncol), dt) for w, dt in outs],
        grid=(ncol, t // tile),
        in_specs=[_in_spec(op, tile) for op in ins],
        out_specs=[pl.BlockSpec((tile, w), _row_map(0, True)) for w, _ in outs],
        compiler_params=_cparams(("arbitrary", "arbitrary")),
    )(*[op[0] for op in ins])
    return res


def rowwise_bwd(name, fn, t, tile, ncol, ins, need, cts, addends=None, row_dtypes=None):
    n_in, n_ct = len(ins), len(cts)
    tile = _row_tile(t, tile, ins)
    addends = addends or {}
    row_dtypes = row_dtypes or {}
    didx = [i for i in range(n_in) if need[i]]
    add_ops = [addends[i] for i in didx if i in addends]
    n_add = len(add_ops)

    def body(*refs):
        in_refs = refs[:n_in]
        ct_refs = refs[n_in:n_in + n_ct]
        add_refs = refs[n_in + n_ct:n_in + n_ct + n_add]
        out_refs = refs[n_in + n_ct + n_add:]
        vals = [r[...].astype(F32) for r in in_refs]

        def g(*dv):
            full = list(vals)
            for i, v in zip(didx, dv):
                full[i] = v
            res = fn(*full)
            return tuple(res) if isinstance(res, (tuple, list)) else (res,)

        _, vjp = jax.vjp(g, *[vals[i] for i in didx])
        grads = vjp(tuple(c[...].astype(F32) for c in ct_refs))
        j, i = pl.program_id(0), pl.program_id(1)
        a = 0
        for o_ref, gv, idx in zip(out_refs, grads, didx):
            _, kind, _, _, moves = ins[idx]
            if kind == "row":
                if idx in addends:
                    gv = gv + add_refs[a][...].astype(F32)
                    a += 1
                o_ref[...] = gv.astype(o_ref.dtype)
            else:
                first = (i == 0) if moves else jnp.logical_and(i == 0, j == 0)

                @pl.when(first)
                def _(o_ref=o_ref, gv=gv):
                    o_ref[...] = gv

                @pl.when(jnp.logical_not(first))
                def _(o_ref=o_ref, gv=gv):
                    o_ref[...] += gv

    out_shape, out_specs = [], []
    for idx in didx:
        _, kind, w, _, moves = ins[idx]
        cols = w * (ncol if moves else 1)
        if kind == "row":
            out_shape.append(jax.ShapeDtypeStruct((t, cols), row_dtypes.get(idx, F32)))
            out_specs.append(pl.BlockSpec((tile, w), _row_map(0, moves)))
        else:
            out_shape.append(jax.ShapeDtypeStruct((1, cols), F32))
            out_specs.append(pl.BlockSpec((1, w), _par_map(0, moves)))
    ops = list(ins) + list(cts) + add_ops
    res = pl.pallas_call(
        body, name=name,
        out_shape=out_shape,
        grid=(ncol, t // tile),
        in_specs=[_in_spec(op, tile) for op in ops],
        out_specs=out_specs,
        compiler_params=_cparams(("arbitrary", "arbitrary")),
    )(*[op[0] for op in ops])
    return res


def f_rmsnorm(x, w):
    return x * lax.rsqrt(jnp.mean(x * x, axis=-1, keepdims=True) + EPS) * w


def f_ssd_post(y, z, w):
    y = y * _silu(z)
    return y * lax.rsqrt(jnp.mean(y * y, axis=-1, keepdims=True) + EPS) * w


def f_gdn_post(o, z, w):
    o = o * lax.rsqrt(jnp.mean(o * o, axis=-1, keepdims=True) + EPS) * w
    return o * _silu(z)


def f_merge(gs, p1, gg, p2):
    return _sigmoid(gs) * p1 + _sigmoid(gg) * p2


def f_swiglu(g, u):
    return _silu(g) * u


def final_loss(name, x, tgt, w, tile=256):
    t, d = x.shape

    def body(x_ref, t_ref, w_ref, loss_ref, dx_ref, dw_ref):
        i = pl.program_id(0)
        xv, tv, wv = x_ref[...], t_ref[...], w_ref[...]

        def g(xx, ww):
            err = f_rmsnorm(xx, ww) - tv
            return 0.5 * jnp.sum(jnp.mean(err * err, axis=-1, keepdims=True), axis=0, keepdims=True)

        val, vjp = jax.vjp(g, xv, wv)
        dx, dw = vjp(jnp.ones((1, 1), F32))
        dx_ref[...] = dx
        lv = jnp.broadcast_to(val, (1, LANE))

        @pl.when(i == 0)
        def _():
            loss_ref[...] = lv
            dw_ref[...] = dw

        @pl.when(i != 0)
        def _():
            loss_ref[...] += lv
            dw_ref[...] += dw

    return pl.pallas_call(
        body, name=name,
        out_shape=[jax.ShapeDtypeStruct((1, LANE), F32), jax.ShapeDtypeStruct((t, d), F32),
                   jax.ShapeDtypeStruct((1, d), F32)],
        grid=(t // tile,),
        in_specs=[pl.BlockSpec((tile, d), lambda i: (i, 0)), pl.BlockSpec((tile, d), lambda i: (i, 0)),
                  pl.BlockSpec((1, d), lambda i: (0, 0))],
        out_specs=[pl.BlockSpec((1, LANE), lambda i: (0, 0)), pl.BlockSpec((tile, d), lambda i: (i, 0)),
                   pl.BlockSpec((1, d), lambda i: (0, 0))],
        compiler_params=_cparams(("arbitrary",)),
    )(x, tgt, w)


CONV_W = 512
HALO = 8


def conv_fwd(name, src, c0, width, w, b, tile=512):
    t = src.shape[0]
    ncol, nrow = width // CONV_W, t // tile
    cb0 = c0 // CONV_W
    hb = tile // HALO

    def body(prev_ref, cur_ref, w_ref, b_ref, o_ref, ext_ref):
        i = pl.program_id(1)
        ext_ref[0:HALO, :] = jnp.where(i == 0, 0.0, prev_ref[...])
        ext_ref[HALO:HALO + tile, :] = cur_ref[...]
        acc = jnp.broadcast_to(b_ref[...], (tile, CONV_W))
        for k in range(CONV_K):
            acc = acc + w_ref[k:k + 1, :] * ext_ref[pl.ds(HALO - (CONV_K - 1) + k, tile), :]
        o_ref[...] = _silu(acc)

    return pl.pallas_call(
        body, name=name,
        out_shape=jax.ShapeDtypeStruct((t, width), F32),
        grid=(ncol, nrow),
        in_specs=[pl.BlockSpec((HALO, CONV_W), lambda j, i: (jnp.maximum(i * hb - 1, 0), cb0 + j)),
                  pl.BlockSpec((tile, CONV_W), lambda j, i: (i, cb0 + j)),
                  pl.BlockSpec((CONV_K, CONV_W), lambda j, i: (0, j)),
                  pl.BlockSpec((1, CONV_W), lambda j, i: (0, j))],
        out_specs=pl.BlockSpec((tile, CONV_W), lambda j, i: (i, j)),
        scratch_shapes=[pltpu.VMEM((tile + HALO, CONV_W), F32)],
        compiler_params=_cparams(("arbitrary", "arbitrary")),
    )(src, src, w, b)


def conv_bwd_pre(name, src, c0, width, w, b, dy, tile=512):
    t = src.shape[0]
    ncol, nrow = width // CONV_W, t // tile
    cb0 = c0 // CONV_W
    hb = tile // HALO

    def body(prev_ref, cur_ref, w_ref, b_ref, dy_ref, dpre_ref, dw_ref, db_ref, ext_ref):
        i = pl.program_id(1)
        ext_ref[0:HALO, :] = jnp.where(i == 0, 0.0, prev_ref[...])
        ext_ref[HALO:HALO + tile, :] = cur_ref[...]
        pre = jnp.broadcast_to(b_ref[...], (tile, CONV_W))
        for k in range(CONV_K):
            pre = pre + w_ref[k:k + 1, :] * ext_ref[pl.ds(HALO - (CONV_K - 1) + k, tile), :]
        s = _sigmoid(pre)
        dpre = dy_ref[...] * (s * (1.0 + pre * (1.0 - s)))
        dpre_ref[...] = dpre

        @pl.when(i == 0)
        def _():
            dw_ref[...] = jnp.zeros_like(dw_ref)
            db_ref[...] = jnp.zeros_like(db_ref)

        for k in range(CONV_K):
            dw_ref[k:k + 1, :] += jnp.sum(dpre * ext_ref[pl.ds(HALO - (CONV_K - 1) + k, tile), :],
                                          axis=0, keepdims=True)
        db_ref[...] += jnp.sum(dpre, axis=0, keepdims=True)

    return pl.pallas_call(
        body, name=name,
        out_shape=[jax.ShapeDtypeStruct((t, width), F32), jax.ShapeDtypeStruct((CONV_K, width), F32),
                   jax.ShapeDtypeStruct((1, width), F32)],
        grid=(ncol, nrow),
        in_specs=[pl.BlockSpec((HALO, CONV_W), lambda j, i: (jnp.maximum(i * hb - 1, 0), cb0 + j)),
                  pl.BlockSpec((tile, CONV_W), lambda j, i: (i, cb0 + j)),
                  pl.BlockSpec((CONV_K, CONV_W), lambda j, i: (0, j)),
                  pl.BlockSpec((1, CONV_W), lambda j, i: (0, j)),
                  pl.BlockSpec((tile, CONV_W), lambda j, i: (i, j))],
        out_specs=[pl.BlockSpec((tile, CONV_W), lambda j, i: (i, j)),
                   pl.BlockSpec((CONV_K, CONV_W), lambda j, i: (0, j)),
                   pl.BlockSpec((1, CONV_W), lambda j, i: (0, j))],
        scratch_shapes=[pltpu.VMEM((tile + HALO, CONV_W), F32)],
        compiler_params=_cparams(("arbitrary", "arbitrary")),
    )(src, src, w, b, dy)


def conv_bwd_in(name, dpre, w, tile=512):
    t, width = dpre.shape
    ncol, nrow = width // CONV_W, t // tile
    hb = tile // HALO
    last_hb = t // HALO - 1

    def body(cur_ref, nxt_ref, w_ref, o_ref, ext_ref):
        i = pl.program_id(1)
        ext_ref[0:tile, :] = cur_ref[...]
        ext_ref[tile:tile + HALO, :] = jnp.where(i == nrow - 1, 0.0, nxt_ref[...])
        acc = jnp.zeros((tile, CONV_W), F32)
        for k in range(CONV_K):
            acc = acc + w_ref[k:k + 1, :] * ext_ref[pl.ds(CONV_K - 1 - k, tile), :]
        o_ref[...] = acc.astype(o_ref.dtype)

    return pl.pallas_call(
        body, name=name,
        out_shape=jax.ShapeDtypeStruct((t, width), BF16),
        grid=(ncol, nrow),
        in_specs=[pl.BlockSpec((tile, CONV_W), lambda j, i: (i, j)),
                  pl.BlockSpec((HALO, CONV_W), lambda j, i: (jnp.minimum((i + 1) * hb, last_hb), j)),
                  pl.BlockSpec((CONV_K, CONV_W), lambda j, i: (0, j))],
        out_specs=pl.BlockSpec((tile, CONV_W), lambda j, i: (i, j)),
        scratch_shapes=[pltpu.VMEM((tile + HALO, CONV_W), F32)],
        compiler_params=_cparams(("arbitrary", "arbitrary")),
    )(dpre, dpre, w)


def _iota2(q):
    return (lax.broadcasted_iota(jnp.int32, (q, q), 0), lax.broadcasted_iota(jnp.int32, (q, q), 1))


def _lane_pick(blk, idx):
    lane = lax.broadcasted_iota(jnp.int32, (1, LANE), 1)
    return jnp.sum(jnp.where(lane == idx, blk, 0.0), axis=1, keepdims=True)


class _Decay:
    def __init__(self, a):
        q = a.shape[0]
        r, c = _iota2(q)
        self.r, self.c = r, c
        self.cum = _dot_hi((c <= r).astype(F32), a)
        self.cum_t = _dot_hi(a, (r <= c).astype(F32), TN)
        self.tot = self.cum[q - 1:q, :]
        self.e_cum = jnp.exp(self.cum)
        self.e_rest = jnp.exp(self.tot - self.cum)
        self.e_tot = jnp.exp(self.tot)

    def mask(self, lane):
        rows = lax.broadcasted_iota(jnp.int32, (LANE, 1), 0)
        cum_row = jnp.sum(jnp.where(rows == lane, self.cum_t, 0.0), axis=0, keepdims=True)
        return jnp.exp(jnp.where(self.r >= self.c, _lane_pick(self.cum, lane) - cum_row, NEG_BIG))


_SSD_B = SSM_D_INNER
_SSD_C = SSM_D_INNER + 2 * SSM_STATE


def _interleave(gens):
    results = [None] * len(gens)
    live = list(range(len(gens)))
    while live:
        for i in list(live):
            try:
                next(gens[i])
            except StopIteration as stop:
                results[i] = stop.value
                live.remove(i)
    return results


def ssd_chunk(xs, bm, cm, dt_all, dsk, dec, state, p, cb):
    lane = lax.broadcasted_iota(jnp.int32, (1, LANE), 1)
    m0 = lane < SSM_HEAD_DIM
    h0, h1 = 2 * p, 2 * p + 1

    def both(blk):
        return jnp.where(m0, _lane_pick(blk, h0), _lane_pick(blk, h1))

    xdt = xs * both(dt_all)
    l0, l1 = dec.mask(h0), dec.mask(h1)
    yield
    y_diag = _dot(cb * l0, jnp.where(m0, xdt, 0.0)) + _dot(cb * l1, jnp.where(m0, 0.0, xdt))
    y_off = _dot(cm, state, NT) * both(dec.e_cum)
    yield
    rowm = lax.broadcasted_iota(jnp.int32, (LANE, 1), 0) < SSM_HEAD_DIM
    new_state = (state * jnp.where(rowm, _lane_pick(dec.e_tot, h0), _lane_pick(dec.e_tot, h1))
                 + _dot(xdt * both(dec.e_rest), bm, TN))
    y = y_diag + y_off + both(dsk) * xs
    return y, new_state


def ssd_pairs(xs, bms, cms, small, dtb, alog, dsk, states):
    dt_all = _softplus(small + dtb)
    dec = _Decay(dt_all * (-jnp.exp(alog)))
    cbs = [_dot(cm, bm, NT) for cm, bm in zip(cms, bms)]
    res = _interleave([ssd_chunk(x, bms[p // 4], cms[p // 4], dt_all, dsk, dec, st, p, cbs[p // 4])
                       for p, (x, st) in enumerate(zip(xs, states))])
    return tuple(y for y, _ in res), tuple(s for _, s in res)


def tri_inverse(a):
    q = a.shape[0]
    r, c = _iota2(q)
    eye = (r == c).astype(F32)
    diag = (r // 16) == (c // 16)
    bd = jnp.where(diag, a, 0.0)
    off = jnp.where(diag, 0.0, a)
    b2 = _dot_hi(bd, bd)
    d1 = _dot_hi(eye - bd, eye + b2)
    yield
    b4 = _dot_hi(b2, b2)
    yield
    b8 = _dot_hi(b4, b4)
    d2 = _dot_hi(d1, eye + b4)
    yield
    dinv = _dot_hi(d2, eye + b8)
    yield
    n = _dot_hi(dinv, off)
    yield
    n2 = _dot_hi(n, n)
    yield
    m = _dot_hi(eye + n2, dinv)
    yield
    return _dot_hi(eye - n, m)


@jax.custom_vjp
def _solve_with(xinv, a, rhs):
    del a
    return _dot_hi(xinv, rhs)


def _solve_with_fwd(xinv, a, rhs):
    t = _dot_hi(xinv, rhs)
    return t, (xinv, t)


def _solve_with_bwd(res, dt):
    xinv, t = res
    d_rhs = _dot_hi(xinv, dt, TN)
    d_a = -_dot(d_rhs, t, NT)
    return jnp.zeros_like(xinv), d_a, d_rhs


_solve_with.defvjp(_solve_with_fwd, _solve_with_bwd)


_GDN_K = GDN_HEADS * GDN_HEAD
_GDN_V = 2 * GDN_HEADS * GDN_HEAD


def gdn_chunk(qh, kh, vh, beta_all, dec, state, h, xinv=None):
    r, c = dec.r, dec.c
    qn = qh * lax.rsqrt(jnp.sum(qh * qh, axis=-1, keepdims=True) + EPS) * (GDN_HEAD ** -0.5)
    kn = kh * lax.rsqrt(jnp.sum(kh * kh, axis=-1, keepdims=True) + EPS)
    beta = _lane_pick(beta_all, SM_B + h)
    decay = dec.mask(SM_A + h)
    yield
    kk = _dot(kn, kn, NT)
    qk = _dot(qn, kn, NT) * decay
    amat = jnp.where(r > c, kk * decay * beta, 0.0)
    eg = _lane_pick(dec.e_cum, SM_A + h)
    rhs = jnp.concatenate([vh * beta, kn * (beta * eg)], axis=1)
    qs = _dot(qn * eg, state)
    yield
    if xinv is None:
        xinv = yield from tri_inverse(amat)
        t = _dot_hi(xinv, rhs)
    else:
        t = _solve_with(xinv, amat, rhs)
    yield
    u, w = t[:, :GDN_HEAD], t[:, GDN_HEAD:]
    v_new = u - _dot(w, state)
    yield
    o = qs + _dot(qk, v_new)
    new_state = (state * _lane_pick(dec.e_tot, SM_A + h)
                 + _dot(kn * _lane_pick(dec.e_rest, SM_A + h), v_new, TN))
    return o, new_state, xinv


def gdn_heads(qs, ks, vs, small, alog, dtb, states, xinvs=None):
    nh = len(qs)
    beta_all = _sigmoid(small)
    dec = _Decay(-jnp.exp(alog) * _softplus(small + dtb))
    res = _interleave([gdn_chunk(qs[h], ks[h], vs[h], beta_all, dec, states[h], h,
                                 None if xinvs is None else xinvs[h]) for h in range(nh)])
    return tuple(o for o, _, _ in res), tuple(s for _, s, _ in res), tuple(x for _, _, x in res)


def _acc(ref, val, first):
    @pl.when(first)
    def _():
        ref[...] = val

    @pl.when(jnp.logical_not(first))
    def _():
        ref[...] += val


def ssd_scan_fwd(name, xbc, proj, dtb, alog, dsk):
    t = xbc.shape[0]
    nc, npair = t // CHUNK, SSM_HEADS // 2
    small_blk = AL_SMALL // LANE

    def body(xbc_ref, sm_ref, dtb_ref, alog_ref, dsk_ref, y_ref, sin_ref, st_ref):
        ci = pl.program_id(0)

        @pl.when(ci == 0)
        def _():
            st_ref[...] = jnp.zeros_like(st_ref)

        s_in = tuple(st_ref[p] for p in range(npair))
        ys, s_new = ssd_pairs(tuple(xbc_ref[:, p * LANE:(p + 1) * LANE] for p in range(npair)),
                              tuple(xbc_ref[:, _SSD_B + g * LANE:_SSD_B + (g + 1) * LANE] for g in range(2)),
                              tuple(xbc_ref[:, _SSD_C + g * LANE:_SSD_C + (g + 1) * LANE] for g in range(2)),
                              sm_ref[...], dtb_ref[...], alog_ref[...], dsk_ref[...], s_in)
        for p in range(npair):
            sin_ref[0, p] = s_in[p]
            y_ref[:, p * LANE:(p + 1) * LANE] = ys[p]
            st_ref[p] = s_new[p]

    par = pl.BlockSpec((1, LANE), lambda ci: (0, 0))
    return pl.pallas_call(
        body, name=name,
        out_shape=[jax.ShapeDtypeStruct((t, SSM_D_INNER), F32),
                   jax.ShapeDtypeStruct((nc, npair, LANE, LANE), F32)],
        grid=(nc,),
        in_specs=[pl.BlockSpec((CHUNK, SSM_CONV_DIM), lambda ci: (ci, 0)),
                  pl.BlockSpec((CHUNK, LANE), lambda ci: (ci, small_blk)),
                  par, par, par],
        out_specs=[pl.BlockSpec((CHUNK, SSM_D_INNER), lambda ci: (ci, 0)),
                   pl.BlockSpec((1, npair, LANE, LANE), lambda ci: (ci, 0, 0, 0))],
        scratch_shapes=[pltpu.VMEM((npair, LANE, LANE), F32)],
        compiler_params=_cparams(("arbitrary",)),
    )(xbc, proj, dtb, alog, dsk)


def ssd_scan_bwd(name, xbc, proj, dtb, alog, dsk, s_in, dy):
    t = xbc.shape[0]
    nc, npair = t // CHUNK, SSM_HEADS // 2
    small_blk = AL_SMALL // LANE

    def body(xbc_ref, sm_ref, dtb_ref, alog_ref, dsk_ref, sin_ref, dy_ref,
             dxbc_ref, dsm_ref, ddtb_ref, dalog_ref, ddsk_ref, dst_ref):
        ci = pl.program_id(0)

        @pl.when(ci == 0)
        def _():
            dst_ref[...] = jnp.zeros_like(dst_ref)

        _, vjp = jax.vjp(ssd_pairs, tuple(xbc_ref[:, p * LANE:(p + 1) * LANE] for p in range(npair)),
                         tuple(xbc_ref[:, _SSD_B + g * LANE:_SSD_B + (g + 1) * LANE] for g in range(2)),
                         tuple(xbc_ref[:, _SSD_C + g * LANE:_SSD_C + (g + 1) * LANE] for g in range(2)),
                         sm_ref[...], dtb_ref[...], alog_ref[...], dsk_ref[...],
                         tuple(sin_ref[0, p] for p in range(npair)))
        dxs, dbms, dcms, dsm, ddtb, dalog, ddsk, dsts = vjp(
            (tuple(dy_ref[:, p * LANE:(p + 1) * LANE] for p in range(npair)),
             tuple(dst_ref[p] for p in range(npair))))
        for p in range(npair):
            dxbc_ref[:, p * LANE:(p + 1) * LANE] = dxs[p]
            dst_ref[p] = dsts[p]
        for g in range(2):
            dxbc_ref[:, _SSD_B + g * LANE:_SSD_B + (g + 1) * LANE] = dbms[g]
            dxbc_ref[:, _SSD_C + g * LANE:_SSD_C + (g + 1) * LANE] = dcms[g]
        dsm_ref[...] = dsm
        _acc(ddtb_ref, ddtb, ci == 0)
        _acc(dalog_ref, dalog, ci == 0)
        _acc(ddsk_ref, ddsk, ci == 0)

    par = pl.BlockSpec((1, LANE), lambda ci: (0, 0))
    rev = lambda ci: nc - 1 - ci
    return pl.pallas_call(
        body, name=name,
        out_shape=[jax.ShapeDtypeStruct((t, SSM_CONV_DIM), F32),
                   jax.ShapeDtypeStruct((t, LANE), F32),
                   jax.ShapeDtypeStruct((1, LANE), F32), jax.ShapeDtypeStruct((1, LANE), F32),
                   jax.ShapeDtypeStruct((1, LANE), F32)],
        grid=(nc,),
        in_specs=[pl.BlockSpec((CHUNK, SSM_CONV_DIM), lambda ci: (rev(ci), 0)),
                  pl.BlockSpec((CHUNK, LANE), lambda ci: (rev(ci), small_blk)),
                  par, par, par,
                  pl.BlockSpec((1, npair, LANE, LANE), lambda ci: (rev(ci), 0, 0, 0)),
                  pl.BlockSpec((CHUNK, SSM_D_INNER), lambda ci: (rev(ci), 0))],
        out_specs=[pl.BlockSpec((CHUNK, SSM_CONV_DIM), lambda ci: (rev(ci), 0)),
                   pl.BlockSpec((CHUNK, LANE), lambda ci: (rev(ci), 0)),
                   par, par, par],
        scratch_shapes=[pltpu.VMEM((npair, LANE, LANE), F32)],
        compiler_params=_cparams(("arbitrary",)),
    )(xbc, proj, dtb, alog, dsk, s_in, dy)


def gdn_scan_fwd(name, qkv, proj, alog, dtb):
    t = qkv.shape[0]
    nc, nh = t // CHUNK, GDN_HEADS
    small_blk = AL_SMALL // LANE

    def body(qkv_ref, sm_ref, alog_ref, dtb_ref, o_ref, sin_ref, x_ref, st_ref):
        ci = pl.program_id(0)

        @pl.when(ci == 0)
        def _():
            st_ref[...] = jnp.zeros_like(st_ref)

        s_in = tuple(st_ref[h] for h in range(nh))
        os, s_new, xinvs = gdn_heads(
            tuple(qkv_ref[:, h * LANE:(h + 1) * LANE] for h in range(nh)),
            tuple(qkv_ref[:, _GDN_K + h * LANE:_GDN_K + (h + 1) * LANE] for h in range(nh)),
            tuple(qkv_ref[:, _GDN_V + h * LANE:_GDN_V + (h + 1) * LANE] for h in range(nh)),
            sm_ref[...], alog_ref[...], dtb_ref[...], s_in)
        for h in range(nh):
            sin_ref[0, h] = s_in[h]
            o_ref[:, h * LANE:(h + 1) * LANE] = os[h]
            x_ref[0, h] = xinvs[h]
            st_ref[h] = s_new[h]

    par = pl.BlockSpec((1, LANE), lambda ci: (0, 0))
    return pl.pallas_call(
        body, name=name,
        out_shape=[jax.ShapeDtypeStruct((t, GDN_HEADS * GDN_HEAD), F32),
                   jax.ShapeDtypeStruct((nc, nh, LANE, LANE), F32),
                   jax.ShapeDtypeStruct((nc, nh, CHUNK, CHUNK), F32)],
        grid=(nc,),
        in_specs=[pl.BlockSpec((CHUNK, GDN_QKV_DIM), lambda ci: (ci, 0)),
                  pl.BlockSpec((CHUNK, LANE), lambda ci: (ci, small_blk)),
                  par, par],
        out_specs=[pl.BlockSpec((CHUNK, GDN_HEADS * GDN_HEAD), lambda ci: (ci, 0)),
                   pl.BlockSpec((1, nh, LANE, LANE), lambda ci: (ci, 0, 0, 0)),
                   pl.BlockSpec((1, nh, CHUNK, CHUNK), lambda ci: (ci, 0, 0, 0))],
        scratch_shapes=[pltpu.VMEM((nh, LANE, LANE), F32)],
        compiler_params=_cparams(("arbitrary",)),
    )(qkv, proj, alog, dtb)


def gdn_scan_bwd(name, qkv, proj, alog, dtb, s_in, xinv, do, dsm_in):
    t = qkv.shape[0]
    nc, nh = t // CHUNK, GDN_HEADS
    small_blk = AL_SMALL // LANE

    def body(qkv_ref, sm_ref, alog_ref, dtb_ref, sin_ref, x_ref, do_ref, dsmi_ref,
             dqkv_ref, dsm_ref, dalog_ref, ddtb_ref, dst_ref):
        ci = pl.program_id(0)

        @pl.when(ci == 0)
        def _():
            dst_ref[...] = jnp.zeros_like(dst_ref)

        xis = tuple(x_ref[0, h] for h in range(nh))

        def fn(qs, ks, vs, sm, alog_, dtb_, sts):
            os, s_new, _ = gdn_heads(qs, ks, vs, sm, alog_, dtb_, sts, xinvs=xis)
            return os, s_new

        _, vjp = jax.vjp(fn, tuple(qkv_ref[:, h * LANE:(h + 1) * LANE] for h in range(nh)),
                         tuple(qkv_ref[:, _GDN_K + h * LANE:_GDN_K + (h + 1) * LANE] for h in range(nh)),
                         tuple(qkv_ref[:, _GDN_V + h * LANE:_GDN_V + (h + 1) * LANE] for h in range(nh)),
                         sm_ref[...], alog_ref[...], dtb_ref[...], tuple(sin_ref[0, h] for h in range(nh)))
        dqs, dks, dvs, dsm, dalog, ddtb, dsts = vjp(
            (tuple(do_ref[:, h * LANE:(h + 1) * LANE] for h in range(nh)), tuple(dst_ref[h] for h in range(nh))))
        for h in range(nh):
            dqkv_ref[:, h * LANE:(h + 1) * LANE] = dqs[h]
            dqkv_ref[:, _GDN_K + h * LANE:_GDN_K + (h + 1) * LANE] = dks[h]
            dqkv_ref[:, _GDN_V + h * LANE:_GDN_V + (h + 1) * LANE] = dvs[h]
            dst_ref[h] = dsts[h]
        dsm_ref[...] = dsmi_ref[...] + dsm
        _acc(dalog_ref, dalog, ci == 0)
        _acc(ddtb_ref, ddtb, ci == 0)

    par = pl.BlockSpec((1, LANE), lambda ci: (0, 0))
    rev = lambda ci: nc - 1 - ci
    return pl.pallas_call(
        body, name=name,
        out_shape=[jax.ShapeDtypeStruct((t, GDN_QKV_DIM), F32), jax.ShapeDtypeStruct((t, LANE), F32),
                   jax.ShapeDtypeStruct((1, LANE), F32), jax.ShapeDtypeStruct((1, LANE), F32)],
        grid=(nc,),
        in_specs=[pl.BlockSpec((CHUNK, GDN_QKV_DIM), lambda ci: (rev(ci), 0)),
                  pl.BlockSpec((CHUNK, LANE), lambda ci: (rev(ci), small_blk)),
                  par, par,
                  pl.BlockSpec((1, nh, LANE, LANE), lambda ci: (rev(ci), 0, 0, 0)),
                  pl.BlockSpec((1, nh, CHUNK, CHUNK), lambda ci: (rev(ci), 0, 0, 0)),
                  pl.BlockSpec((CHUNK, GDN_HEADS * GDN_HEAD), lambda ci: (rev(ci), 0)),
                  pl.BlockSpec((CHUNK, LANE), lambda ci: (rev(ci), 0))],
        out_specs=[pl.BlockSpec((CHUNK, GDN_QKV_DIM), lambda ci: (rev(ci), 0)),
                   pl.BlockSpec((CHUNK, LANE), lambda ci: (rev(ci), 0)),
                   par, par],
        scratch_shapes=[pltpu.VMEM((nh, LANE, LANE), F32)],
        compiler_params=_cparams(("arbitrary",)),
    )(qkv, proj, alog, dtb, s_in, xinv, do, dsm_in)


def _row(arr, w, c0=0, moves=False):
    return (arr, "row", w, c0, moves)


def _par(arr, w, c0=0, moves=False):
    return (arr, "par", w, c0, moves)


def matmul_add(name, a, b, res):
    (m, k), (_, n) = a.shape, b.shape
    tm, tn, tk = _matmul_tiles(m, n, n, k, a.dtype.itemsize, b.dtype.itemsize, 4 + res.dtype.itemsize)
    nk = k // tk

    def body_acc(a_ref, b_ref, r_ref, o_ref, acc_ref):
        kk = pl.program_id(2)

        @pl.when(kk == 0)
        def _():
            acc_ref[...] = r_ref[...]

        acc_ref[...] += _dot(a_ref[...], b_ref[...])

        @pl.when(kk == nk - 1)
        def _():
            o_ref[...] = acc_ref[...]

    def body_one(a_ref, b_ref, r_ref, o_ref):
        o_ref[...] = r_ref[...] + _dot(a_ref[...], b_ref[...])

    body = body_one if nk == 1 else body_acc
    return pl.pallas_call(
        body, name=name,
        out_shape=jax.ShapeDtypeStruct((m, n), F32),
        grid=(m // tm, n // tn, nk),
        in_specs=[pl.BlockSpec((tm, tk), lambda i, j, kk: (i, kk)),
                  pl.BlockSpec((tk, tn), lambda i, j, kk: (kk, j)),
                  pl.BlockSpec((tm, tn), lambda i, j, kk: (i, j))],
        out_specs=pl.BlockSpec((tm, tn), lambda i, j, kk: (i, j)),
        scratch_shapes=[] if nk == 1 else [pltpu.VMEM((tm, tn), F32)],
        compiler_params=_cparams(("parallel", "parallel", "arbitrary")),
    )(a, b, res)


def layer_fwd(l, x, w):
    t = x.shape[0]
    rt = min(256, t)
    s = {"x": x}
    s["h"] = rowwise_fwd(f"norm_mix_l{l}", f_rmsnorm, t, rt, 1,
                         [_row(x, D_MODEL), _par(w["norm_mix_w"], D_MODEL)], [(D_MODEL, BF16)])[0]
    s["proj"] = matmul(f"in_proj_l{l}", s["h"], w["w_in"], "nn")
    s["xbc"] = conv_fwd(f"ssm_conv_l{l}", s["proj"], AL_XBC, SSM_CONV_DIM, w["ssm_conv_w"], w["ssm_conv_b"],
                        tile=min(512, t))
    s["qkv"] = conv_fwd(f"gdn_conv_l{l}", s["proj"], AL_QKV, GDN_QKV_DIM, w["gdn_conv_w"], w["gdn_conv_b"],
                        tile=min(512, t))
    s["y_scan"], s["ssd_sin"] = ssd_scan_fwd(f"ssd_scan_l{l}", s["xbc"], s["proj"], w["ssm_dt_bias"],
                                             w["ssm_a_log"], w["ssm_d"])
    s["o_scan"], s["gdn_sin"], s["gdn_x"] = gdn_scan_fwd(f"gdn_scan_l{l}", s["qkv"], s["proj"],
                                                         w["gdn_a_log"], w["gdn_dt_bias"])
    s["y_ssm"] = rowwise_fwd(f"ssd_post_l{l}", f_ssd_post, t, rt, 2,
                             [_row(s["y_scan"], 512, 0, True), _row(s["proj"], 512, AL_Z // 512, True),
                              _par(w["ssm_norm_w"], 512, 0, True)], [(512, BF16)])[0]
    s["y_gdn"] = rowwise_fwd(f"gdn_post_l{l}", f_gdn_post, t, rt, GDN_HEADS,
                             [_row(s["o_scan"], LANE, 0, True), _row(s["proj"], LANE, AL_GZ // LANE, True),
                              _par(w["gdn_norm_w"], LANE)], [(LANE, BF16)])[0]
    s["p1"] = matmul(f"proj_ssm_l{l}", s["y_ssm"], w["w_proj_ssm"], "nn")
    s["p2"] = matmul(f"proj_gdn_l{l}", s["y_gdn"], w["w_proj_gdn"], "nn")
    s["merged"] = rowwise_fwd(f"merge_l{l}", f_merge, t, rt, 2,
                              [_row(s["proj"], 512, AL_GS // 512, True), _row(s["p1"], 512, 0, True),
                               _row(s["proj"], 512, AL_GG // 512, True), _row(s["p2"], 512, 0, True)],
                              [(512, BF16)])[0]
    s["x1"] = matmul_add(f"out_proj_l{l}", s["merged"], w["w_out"], x)
    s["h2"] = rowwise_fwd(f"norm_ffn_l{l}", f_rmsnorm, t, rt, 1,
                          [_row(s["x1"], D_MODEL), _par(w["norm_ffn_w"], D_MODEL)], [(D_MODEL, BF16)])[0]
    s["gu"] = matmul(f"ffn_in_l{l}", s["h2"], w["w_ffn_in"], "nn")
    s["act"] = rowwise_fwd(f"swiglu_l{l}", f_swiglu, t, rt, FFN_HIDDEN // 256,
                           [_row(s["gu"], 256, 0, True), _row(s["gu"], 256, FFN_HIDDEN // 256, True)],
                           [(256, BF16)])[0]
    x2 = matmul_add(f"ffn_down_l{l}", s["act"], w["w_ffn_down"], s["x1"])
    return x2, s


IN_SHARD = IN_DIM // 4
IN_SHARD_PAD = 2304


def _aligned_to_shards(g):
    orig = jnp.concatenate([g[:, 0:2560], g[:, AL_SMALL:AL_SMALL + 16], g[:, 2560:6656],
                            g[:, AL_SMALL + 16:AL_SMALL + 32], g[:, 6656:8704]], axis=1)
    return jnp.stack([jnp.pad(orig[:, j * IN_SHARD:(j + 1) * IN_SHARD], ((0, 0), (0, IN_SHARD_PAD - IN_SHARD)))
                      for j in range(N_CHIPS)])


def layer_bwd(l, dx2, w, s, gbuf):
    t = dx2.shape[0]
    rt = min(256, t)
    ct = min(512, t)
    g = {}
    dact = matmul(f"ffn_down_dx_l{l}", dx2, w["w_ffn_down"], "nt")
    g["w_ffn_down"] = matmul(f"ffn_down_dw_l{l}", s["act"], dx2, "tn", stack=(l, gbuf.get("w_ffn_down")))
    nf = FFN_HIDDEN // 256
    dgate, dup = rowwise_bwd(f"swiglu_bwd_l{l}", f_swiglu, t, rt, nf,
                             [_row(s["gu"], 256, 0, True), _row(s["gu"], 256, nf, True)], [True, True],
                             [_row(dact, 256, 0, True)], row_dtypes={0: BF16, 1: BF16})
    dgu = jnp.concatenate([dgate, dup], axis=1)
    dh2 = matmul(f"ffn_in_dx_l{l}", dgu, w["w_ffn_in"], "nt")
    g["w_ffn_in"] = matmul(f"ffn_in_dw_l{l}", s["h2"], dgu, "tn", chip_major=True,
                           stack=(l, gbuf.get("w_ffn_in")))
    dx1, g["norm_ffn_w"] = rowwise_bwd(f"norm_ffn_bwd_l{l}", f_rmsnorm, t, rt, 1,
                                       [_row(s["x1"], D_MODEL), _par(w["norm_ffn_w"], D_MODEL)], [True, True],
                                       [_row(dh2, D_MODEL)], addends={0: _row(dx2, D_MODEL)})
    dmerged = matmul(f"out_proj_dx_l{l}", dx1, w["w_out"], "nt")
    g["w_out"] = matmul(f"out_proj_dw_l{l}", s["merged"], dx1, "tn", stack=(l, gbuf.get("w_out")))
    dgs, dp1, dgg, dp2 = rowwise_bwd(
        f"merge_bwd_l{l}", f_merge, t, rt, 2,
        [_row(s["proj"], 512, AL_GS // 512, True), _row(s["p1"], 512, 0, True),
         _row(s["proj"], 512, AL_GG // 512, True), _row(s["p2"], 512, 0, True)], [True] * 4,
        [_row(dmerged, 512, 0, True)], row_dtypes={0: BF16, 1: BF16, 2: BF16, 3: BF16})
    dy_ssm = matmul(f"proj_ssm_dx_l{l}", dp1, w["w_proj_ssm"], "nt")
    g["w_proj_ssm"] = matmul(f"proj_ssm_dw_l{l}", s["y_ssm"], dp1, "tn", stack=(l, gbuf.get("w_proj_ssm")))
    dy_gdn = matmul(f"proj_gdn_dx_l{l}", dp2, w["w_proj_gdn"], "nt")
    g["w_proj_gdn"] = matmul(f"proj_gdn_dw_l{l}", s["y_gdn"], dp2, "tn", stack=(l, gbuf.get("w_proj_gdn")))
    dy_scan, dz, g["ssm_norm_w"] = rowwise_bwd(
        f"ssd_post_bwd_l{l}", f_ssd_post, t, rt, 2,
        [_row(s["y_scan"], 512, 0, True), _row(s["proj"], 512, AL_Z // 512, True),
         _par(w["ssm_norm_w"], 512, 0, True)], [True] * 3, [_row(dy_ssm, 512, 0, True)], row_dtypes={1: BF16})
    dxbc_act, dsm, g["ssm_dt_bias"], g["ssm_a_log"], g["ssm_d"] = ssd_scan_bwd(
        f"ssd_scan_bwd_l{l}", s["xbc"], s["proj"], w["ssm_dt_bias"], w["ssm_a_log"], w["ssm_d"],
        s["ssd_sin"], dy_scan)
    dpre, g["ssm_conv_w"], g["ssm_conv_b"] = conv_bwd_pre(
        f"ssm_conv_bwd_l{l}", s["proj"], AL_XBC, SSM_CONV_DIM, w["ssm_conv_w"], w["ssm_conv_b"], dxbc_act, tile=ct)
    dxbc = conv_bwd_in(f"ssm_conv_din_l{l}", dpre, w["ssm_conv_w"], tile=ct)
    do_scan, dgz, g["gdn_norm_w"] = rowwise_bwd(
        f"gdn_post_bwd_l{l}", f_gdn_post, t, rt, GDN_HEADS,
        [_row(s["o_scan"], LANE, 0, True), _row(s["proj"], LANE, AL_GZ // LANE, True),
         _par(w["gdn_norm_w"], LANE)], [True] * 3, [_row(dy_gdn, LANE, 0, True)], row_dtypes={1: BF16})
    dqkv_act, dsm, g["gdn_a_log"], g["gdn_dt_bias"] = gdn_scan_bwd(
        f"gdn_scan_bwd_l{l}", s["qkv"], s["proj"], w["gdn_a_log"], w["gdn_dt_bias"], s["gdn_sin"],
        s["gdn_x"], do_scan, dsm)
    dpre, g["gdn_conv_w"], _ = conv_bwd_pre(
        f"gdn_conv_bwd_l{l}", s["proj"], AL_QKV, GDN_QKV_DIM, w["gdn_conv_w"], w["gdn_conv_b"], dqkv_act, tile=ct)
    dqkv = conv_bwd_in(f"gdn_conv_din_l{l}", dpre, w["gdn_conv_w"], tile=ct)
    dproj = jnp.concatenate([dz, dxbc, dqkv, dgz, dgs, dgg, dsm.astype(BF16),
                             jnp.zeros((t, AL_DIM - AL_SMALL - LANE), BF16)], axis=1)
    dh = matmul(f"in_proj_dx_l{l}", dproj, w["w_in"], "nt")
    g["w_in"] = matmul(f"in_proj_dw_l{l}", s["h"], dproj, "tn")
    dx0, g["norm_mix_w"] = rowwise_bwd(f"norm_mix_bwd_l{l}", f_rmsnorm, t, rt, 1,
                                       [_row(s["x"], D_MODEL), _par(w["norm_mix_w"], D_MODEL)], [True, True],
                                       [_row(dh, D_MODEL)], addends={0: _row(dx1, D_MODEL)})
    return dx0, g


def _align_w_in(w):
    pad = jnp.zeros((w.shape[0], AL_DIM - AL_SMALL - 32), w.dtype)
    return jnp.concatenate([w[:, 0:2560], w[:, 2576:6672], w[:, 6688:8736],
                            w[:, 2560:2576], w[:, 6672:6688], pad], axis=1)


def _pad_lane(v, at=0):
    return jnp.pad(v[None], ((0, 0), (at, LANE - at - v.shape[0])))


def local_step(x, target, full):
    ws = []
    for l in range(DEPTH):
        ws.append({
            "norm_mix_w": full["norm_mix_w"][l][None], "w_in": _align_w_in(full["w_in"][l]),
            "ssm_conv_w": full["ssm_conv_w"][l], "ssm_conv_b": full["ssm_conv_b"][l][None],
            "ssm_dt_bias": _pad_lane(full["ssm_dt_bias"][l]), "ssm_a_log": _pad_lane(full["ssm_a_log"][l]),
            "ssm_d": _pad_lane(full["ssm_d"][l]), "ssm_norm_w": full["ssm_norm_w"][l][None],
            "gdn_conv_w": full["gdn_conv_w"][l], "gdn_conv_b": jnp.zeros((1, GDN_QKV_DIM), F32),
            "gdn_a_log": _pad_lane(full["gdn_a_log"][l], SM_A),
            "gdn_dt_bias": _pad_lane(full["gdn_dt_bias"][l], SM_A),
            "gdn_norm_w": full["gdn_norm_w"][l][None],
            "w_proj_ssm": full["w_proj_ssm"][l], "w_proj_gdn": full["w_proj_gdn"][l], "w_out": full["w_out"][l],
            "norm_ffn_w": full["norm_ffn_w"][l][None], "w_ffn_in": full["w_ffn_in"][l],
            "w_ffn_down": full["w_ffn_down"][l],
        })
    saved = []
    h = x
    for l in range(DEPTH):
        h, s = layer_fwd(l, h, ws[l])
        saved.append(s)
    loss, dx, g_final = final_loss("final_loss", h, target, full["final_norm_w"][None], tile=min(256, x.shape[0]))
    per_layer = [None] * DEPTH
    gbuf = {}
    for l in reversed(range(DEPTH)):
        dx, per_layer[l] = layer_bwd(l, dx, ws[l], saved[l], gbuf)
        gbuf = {n: per_layer[l][n] for n, _ in BIG if n != "w_in"}
    gbuf["w_in"] = jnp.stack([_aligned_to_shards(per_layer[l]["w_in"]) for l in range(DEPTH)])
    grads = {"final_norm_w": g_final[0], **gbuf}
    for name in per_layer[0]:
        if name in gbuf:
            continue
        rows = []
        for l in range(DEPTH):
            gl = per_layer[l][name]
            if name in ("ssm_dt_bias", "ssm_a_log", "ssm_d"):
                gl = gl[0, :SSM_HEADS]
            elif name in ("gdn_a_log", "gdn_dt_bias"):
                gl = gl[0, SM_A:SM_A + GDN_HEADS]
            elif name in ("norm_mix_w", "ssm_conv_b", "ssm_norm_w", "gdn_norm_w", "norm_ffn_w"):
                gl = gl[0]
            rows.append(gl)
        grads[name] = jnp.stack(rows)
    return loss, dx, grads


MESH = pl.DeviceIdType.MESH
HBM = pl.BlockSpec(memory_space=pltpu.HBM)
N_DEV = 8


def _pos():
    return lax.axis_index("x"), lax.axis_index("y"), lax.axis_index("c")


def _rcopy(src, dst, send_sem, recv_sem, dev):
    return pltpu.make_async_remote_copy(src_ref=src, dst_ref=dst, send_sem=send_sem, recv_sem=recv_sem,
                                        device_id=dev, device_id_type=MESH)


RELATIONS = (2, 1, 3)


def _related_chip(x, y, mask):
    return (1 - x if mask & 2 else x, 1 - y if mask & 1 else y)


def weights_gather(name, bufs):
    n = len(bufs)

    def body(*refs):
        outs, send_sems, recv_sems = refs[n:2 * n], refs[2 * n], refs[2 * n + 1]
        x, y, c = _pos()
        sib = (x, y, 1 - c)
        sends = []
        for i, a in enumerate(outs):
            for k, m in enumerate(RELATIONS):
                px, py = _related_chip(x, y, m)
                cp = _rcopy(a.at[0, c], a.at[m, c], send_sems.at[6 * i + k], recv_sems.at[6 * i + k], (px, py, c))
                cp.start()
                sends.append(cp)
        for i, a in enumerate(outs):
            for k, m in enumerate(RELATIONS):
                px, py = _related_chip(x, y, m)
                _rcopy(a.at[0, c], a.at[m, c], send_sems.at[6 * i + k], recv_sems.at[6 * i + k],
                       (px, py, c)).wait_recv()
                fw = _rcopy(a.at[m, c], a.at[m, c], send_sems.at[6 * i + 3 + k], recv_sems.at[6 * i + 3 + k], sib)
                fw.start()
                sends.append(fw)
        for i, a in enumerate(outs):
            for k, m in enumerate(RELATIONS):
                _rcopy(a.at[m, 1 - c], a.at[m, 1 - c], send_sems.at[6 * i + 3 + k], recv_sems.at[6 * i + 3 + k],
                       sib).wait_recv()
        for cp in sends:
            cp.wait_send()

    return pl.pallas_call(
        body, name=name, out_shape=[jax.ShapeDtypeStruct(b.shape, b.dtype) for b in bufs],
        in_specs=[HBM] * n, out_specs=[HBM] * n,
        input_output_aliases={i: i for i in range(n)},
        scratch_shapes=[pltpu.SemaphoreType.DMA((6 * n,)), pltpu.SemaphoreType.DMA((6 * n,))],
    )(*bufs)


def pair_swap(name, gs):
    n = len(gs)

    def body(*refs):
        srcs, outs, send_sems, recv_sems = refs[:n], refs[n:2 * n], refs[2 * n], refs[2 * n + 1]
        x, y, c = _pos()
        cps = [_rcopy(s.at[1 - c], o, send_sems.at[i], recv_sems.at[i], (x, y, 1 - c))
               for i, (s, o) in enumerate(zip(srcs, outs))]
        for cp in cps:
            cp.start()
        for cp in cps:
            cp.wait()

    return pl.pallas_call(
        body, name=name, out_shape=[jax.ShapeDtypeStruct(g.shape[1:], g.dtype) for g in gs],
        in_specs=[HBM] * n, out_specs=[HBM] * n,
        scratch_shapes=[pltpu.SemaphoreType.DMA((n,)), pltpu.SemaphoreType.DMA((n,))],
    )(*gs)


def chip_scatter(name, ss):
    n = len(ss)

    def body(*refs):
        srcs, outs, send_sems, recv_sems = refs[:n], refs[n:2 * n], refs[2 * n], refs[2 * n + 1]
        x, y, c = _pos()
        sends = []
        for i, (s, o) in enumerate(zip(srcs, outs)):
            for k, m in enumerate(RELATIONS):
                px, py = _related_chip(x, y, m)
                cp = _rcopy(s.at[2 * px + py], o.at[k], send_sems.at[3 * i + k], recv_sems.at[3 * i + k],
                            (px, py, c))
                cp.start()
                sends.append(cp)
        for cp in sends:
            cp.wait()

    return pl.pallas_call(
        body, name=name, out_shape=[jax.ShapeDtypeStruct((3,) + s.shape[1:], s.dtype) for s in ss],
        in_specs=[HBM] * n, out_specs=[HBM] * n,
        scratch_shapes=[pltpu.SemaphoreType.DMA((3 * n,)), pltpu.SemaphoreType.DMA((3 * n,))],
    )(*ss)


def pair_share(name, bufs):
    n = len(bufs)

    def body(*refs):
        outs, send_sems, recv_sems = refs[n:2 * n], refs[2 * n], refs[2 * n + 1]
        x, y, c = _pos()
        sends = []
        for i, o in enumerate(outs):
            cp = _rcopy(o.at[c], o.at[c], send_sems.at[i], recv_sems.at[i], (x, y, 1 - c))
            cp.start()
            sends.append(cp)
        for i, o in enumerate(outs):
            _rcopy(o.at[1 - c], o.at[1 - c], send_sems.at[i], recv_sems.at[i], (x, y, 1 - c)).wait_recv()
        for cp in sends:
            cp.wait_send()

    return pl.pallas_call(
        body, name=name, out_shape=[jax.ShapeDtypeStruct(b.shape, b.dtype) for b in bufs],
        in_specs=[HBM] * n, out_specs=[HBM] * n,
        input_output_aliases={i: i for i in range(n)},
        scratch_shapes=[pltpu.SemaphoreType.DMA((n,)), pltpu.SemaphoreType.DMA((n,))],
    )(*bufs)


def all_allgather(name, buf):
    r, cd = buf.shape

    def body(src, out, send_sems, recv_sems, lsem):
        x, y, c = _pos()
        me = 4 * x + 2 * y + c
        local = pltpu.make_async_copy(src, out.at[me], lsem)
        local.start()

        def peer(mask):
            px = 1 - x if mask & 4 else x
            py = 1 - y if mask & 2 else y
            pc = 1 - c if mask & 1 else c
            return px, py, pc

        sends = []
        for mask in range(1, N_DEV):
            cp = _rcopy(src, out.at[me], send_sems.at[mask - 1], recv_sems.at[mask - 1], peer(mask))
            cp.start()
            sends.append(cp)
        for mask in range(1, N_DEV):
            px, py, pc = peer(mask)
            _rcopy(src, out.at[4 * px + 2 * py + pc], send_sems.at[mask - 1], recv_sems.at[mask - 1],
                   (px, py, pc)).wait_recv()
        for cp in sends:
            cp.wait_send()
        local.wait()

    return pl.pallas_call(
        body, name=name, out_shape=jax.ShapeDtypeStruct((N_DEV, r, cd), buf.dtype),
        in_specs=[HBM], out_specs=HBM,
        scratch_shapes=[pltpu.SemaphoreType.DMA((N_DEV - 1,)), pltpu.SemaphoreType.DMA((N_DEV - 1,)),
                        pltpu.SemaphoreType.DMA(())],
    )(buf)


ELEMENTWISE_BLOCK_BYTES = 2 << 20


def _row_block(rows, cols):
    for cand in (1024, 512, 256, 128, 64, 32, 16):
        if rows % cand == 0 and cand * cols * 4 <= ELEMENTWISE_BLOCK_BYTES:
            return cand
    return rows


def chip_sum(name, s, r, me, c):
    _, a, b = s.shape
    tr = _row_block(a, b)

    def body(idx_ref, s_ref, r_ref, o_ref):
        del idx_ref
        acc = s_ref[...].astype(F32)
        for k in range(3):
            acc = acc + r_ref[k].astype(F32)
        o_ref[...] = acc

    return pl.pallas_call(
        body, name=name, out_shape=jax.ShapeDtypeStruct((2, a, b), F32),
        grid_spec=pltpu.PrefetchScalarGridSpec(
            num_scalar_prefetch=1, grid=(a // tr,),
            in_specs=[pl.BlockSpec((None, tr, b), lambda i, idx: (idx[0], i, 0)),
                      pl.BlockSpec((3, tr, b), lambda i, idx: (0, i, 0))],
            out_specs=pl.BlockSpec((None, tr, b), lambda i, idx: (idx[1], i, 0))),
        compiler_params=_cparams(("arbitrary",)),
    )(jnp.stack([me, c]).astype(jnp.int32), s, r)


def pair_add(name, p, recv, c):
    _, nj, rh, cd = p.shape
    tr = _row_block(rh, cd)

    def body(c_ref, p_ref, r_ref, o_ref):
        del c_ref
        o_ref[...] = (p_ref[0] + r_ref[...]).astype(o_ref.dtype)

    return pl.pallas_call(
        body, name=name, out_shape=jax.ShapeDtypeStruct((nj, rh, cd), BF16),
        grid_spec=pltpu.PrefetchScalarGridSpec(
            num_scalar_prefetch=1, grid=(nj, rh // tr),
            in_specs=[pl.BlockSpec((1, 1, tr, cd), lambda j, i, c_ref: (c_ref[0], j, i, 0)),
                      pl.BlockSpec((1, tr, cd), lambda j, i, c_ref: (j, i, 0))],
            out_specs=pl.BlockSpec((1, tr, cd), lambda j, i, c_ref: (j, i, 0))),
        compiler_params=_cparams(("arbitrary", "arbitrary")),
    )(jnp.reshape(c, (1,)).astype(jnp.int32), p, recv)


def slab_sum(name, a):
    n, r, cd = a.shape
    tr = _pick(r, (256, 128, 64, 32, 16, 8))

    def body(a_ref, o_ref):
        acc = a_ref[0].astype(F32)
        for j in range(1, n):
            acc = acc + a_ref[j].astype(F32)
        o_ref[...] = acc

    return pl.pallas_call(
        body, name=name, out_shape=jax.ShapeDtypeStruct((r, cd), F32),
        grid=(r // tr,),
        in_specs=[pl.BlockSpec((n, tr, cd), lambda i: (0, i, 0))],
        out_specs=pl.BlockSpec((tr, cd), lambda i: (i, 0)),
        compiler_params=_cparams(("arbitrary",)),
    )(a)


ADAM_C1 = 1.0 - ADAM_B1 ** ADAM_STEP
ADAM_C2 = 1.0 - ADAM_B2 ** ADAM_STEP


def adamw(name, w, g, m, v):
    r, cd = w.shape
    tr = r
    for cand in (512, 256, 128, 64, 32, 16, 8):
        if r % cand == 0 and cand * cd * 4 <= (1 << 20):
            tr = cand
            break

    def body(w_ref, g_ref, m_ref, v_ref, d_ref, nm_ref, nv_ref):
        gv = g_ref[...]
        nm = ADAM_B1 * m_ref[...] + (1.0 - ADAM_B1) * gv
        nv = ADAM_B2 * v_ref[...] + (1.0 - ADAM_B2) * (gv * gv)
        m_hat = nm / ADAM_C1
        v_hat = nv / ADAM_C2
        d_ref[...] = -ADAM_LR * (m_hat / (jnp.sqrt(v_hat) + ADAM_EPS) + ADAM_WD * w_ref[...])
        nm_ref[...] = nm
        nv_ref[...] = nv

    spec = pl.BlockSpec((tr, cd), lambda i: (i, 0))
    sd = jax.ShapeDtypeStruct((r, cd), F32)
    return pl.pallas_call(
        body, name=name, out_shape=[sd, sd, sd], grid=(r // tr,),
        in_specs=[spec] * 4, out_specs=[spec] * 3,
        compiler_params=_cparams(("arbitrary",)),
    )(w, g, m, v)


WEIGHTS = ("norm_mix_w", "w_in", "ssm_conv_w", "ssm_conv_b", "ssm_dt_bias", "ssm_a_log", "ssm_d", "ssm_norm_w",
           "gdn_conv_w", "gdn_a_log", "gdn_dt_bias", "gdn_norm_w", "w_proj_ssm", "w_proj_gdn", "w_out",
           "norm_ffn_w", "w_ffn_in", "w_ffn_down", "final_norm_w")
BIG = (("w_in", 2), ("w_proj_ssm", 1), ("w_proj_gdn", 1), ("w_out", 1), ("w_ffn_in", 2), ("w_ffn_down", 1))
CONVW = (("ssm_conv_w", 2), ("gdn_conv_w", 2))
SHARDED = BIG + CONVW
SMALL = tuple(n for n in WEIGHTS if n not in dict(SHARDED))


def _unpack(buf, shapes, lead=()):
    flat = buf.reshape(lead + (-1,))
    out, o = [], 0
    for shp in shapes:
        n = math.prod(shp)
        out.append(flat[..., o:o + n].reshape(lead + tuple(shp)))
        o += n
    return out


def _pack_rows(arrs, lead=()):
    nl = len(lead)
    flat = jnp.concatenate([a.reshape(lead + (-1,)) for a in arrs], axis=nl)
    n = flat.shape[nl]
    rows = -(-n // (8 * LANE)) * 8
    flat = jnp.pad(flat, [(0, 0)] * nl + [(0, rows * LANE - n)])
    return flat.reshape(lead + (rows, LANE))


def _slot_buffer(shard):
    return jnp.pad(shard[None], [(0, N_CHIPS - 1)] + [(0, 0)] * shard.ndim)


def kernel(x, norm_mix_w, w_in, ssm_conv_w, ssm_conv_b, ssm_dt_bias, ssm_a_log, ssm_d, ssm_norm_w, gdn_conv_w, gdn_a_log, gdn_dt_bias, gdn_norm_w, w_proj_ssm, w_proj_gdn, w_out, norm_ffn_w, w_ffn_in, w_ffn_down, final_norm_w, loss_target, m_norm_mix_w, m_w_in, m_ssm_conv_w, m_ssm_conv_b, m_ssm_dt_bias, m_ssm_a_log, m_ssm_d, m_ssm_norm_w, m_gdn_conv_w, m_gdn_a_log, m_gdn_dt_bias, m_gdn_norm_w, m_w_proj_ssm, m_w_proj_gdn, m_w_out, m_norm_ffn_w, m_w_ffn_in, m_w_ffn_down, m_final_norm_w, v_norm_mix_w, v_w_in, v_ssm_conv_w, v_ssm_conv_b, v_ssm_dt_bias, v_ssm_a_log, v_ssm_d, v_ssm_norm_w, v_gdn_conv_w, v_gdn_a_log, v_gdn_dt_bias, v_gdn_norm_w, v_w_proj_ssm, v_w_proj_gdn, v_w_out, v_norm_ffn_w, v_w_ffn_in, v_w_ffn_down, v_final_norm_w):
    wl = (norm_mix_w, w_in, ssm_conv_w, ssm_conv_b, ssm_dt_bias, ssm_a_log, ssm_d, ssm_norm_w, gdn_conv_w,
          gdn_a_log, gdn_dt_bias, gdn_norm_w, w_proj_ssm, w_proj_gdn, w_out, norm_ffn_w, w_ffn_in, w_ffn_down,
          final_norm_w)
    ml = (m_norm_mix_w, m_w_in, m_ssm_conv_w, m_ssm_conv_b, m_ssm_dt_bias, m_ssm_a_log, m_ssm_d, m_ssm_norm_w,
          m_gdn_conv_w, m_gdn_a_log, m_gdn_dt_bias, m_gdn_norm_w, m_w_proj_ssm, m_w_proj_gdn, m_w_out,
          m_norm_ffn_w, m_w_ffn_in, m_w_ffn_down, m_final_norm_w)
    vl = (v_norm_mix_w, v_w_in, v_ssm_conv_w, v_ssm_conv_b, v_ssm_dt_bias, v_ssm_a_log, v_ssm_d, v_ssm_norm_w,
          v_gdn_conv_w, v_gdn_a_log, v_gdn_dt_bias, v_gdn_norm_w, v_w_proj_ssm, v_w_proj_gdn, v_w_out,
          v_norm_ffn_w, v_w_ffn_in, v_w_ffn_down, v_final_norm_w)
    w = dict(zip(WEIGHTS, wl))
    m = dict(zip(WEIGHTS, ml))
    v = dict(zip(WEIGHTS, vl))
    x_pos, y_pos, c = _pos()
    me = 2 * x_pos + y_pos
    big = [n for n, _ in BIG]

    shards = [w[n].astype(BF16) for n in big]
    shards[0] = jnp.pad(shards[0], ((0, 0), (0, 0), (0, IN_SHARD_PAD - IN_SHARD)))
    conv_shapes = [w[n].shape[1:] for n, _ in CONVW]
    conv_pack = _pack_rows([w[n] for n, _ in CONVW], lead=(DEPTH,))
    gathered = weights_gather("gather_w", [_slot_buffer(s) for s in shards + [conv_pack]])
    by_chip = [[lax.dynamic_index_in_dim(g_, jnp.bitwise_xor(me, j), 0, keepdims=False) for j in range(N_CHIPS)]
               for g_ in gathered]
    full = {n: w[n] for n in SMALL}
    for i, (n, axis) in enumerate(BIG):
        full[n] = jnp.concatenate(by_chip[i], axis=axis)
    full["w_in"] = jnp.concatenate([p[:, :, :IN_SHARD] for p in by_chip[0]], axis=2)
    conv_parts = [_unpack(by_chip[-1][j], conv_shapes, lead=(DEPTH,)) for j in range(N_CHIPS)]
    for i, (n, axis) in enumerate(CONVW):
        full[n] = jnp.concatenate([conv_parts[j][i] for j in range(N_CHIPS)], axis=axis)

    loss_part, grad_x, grads = local_step(x[0], loss_target[0], full)

    by4 = [grads[n].reshape((DEPTH, N_CHIPS, -1) + grads[n].shape[-1:]) for n in big]
    from_pair = pair_swap("grad_pair_swap", by4)
    chip_part = [pair_add(f"grad_pair_add_{n}", g_, r_, c) for n, g_, r_ in zip(big, by4, from_pair)]
    from_chips = chip_scatter("grad_chip_scatter", chip_part)
    halves = [chip_sum(f"grad_chip_sum_{n}", s_, r_, me, c) for n, s_, r_ in zip(big, chip_part, from_chips)]
    reduced = pair_share("grad_pair_share", halves)
    g_sharded = dict(zip(big, reduced))
    g_sharded["w_in"] = g_sharded["w_in"][:, :, :IN_SHARD]

    small_names = list(SMALL) + [n for n, _ in CONVW]
    small_all = all_allgather("gather_small", _pack_rows([grads[n] for n in small_names] + [loss_part[0, :1]]))
    small_sum = slab_sum("small_sum", small_all)
    small_vals = _unpack(small_sum, [grads[n].shape for n in small_names] + [(1,)])
    g_small = dict(zip(small_names, small_vals[:-1]))
    loss = small_vals[-1].reshape(())
    for n, axis in CONVW:
        size = w[n].shape[axis]
        g_sharded[n] = lax.dynamic_slice_in_dim(g_small.pop(n), me * size, size, axis=axis)

    out_g, out_d, out_m, out_v = {}, {}, {}, {}
    for n, _ in SHARDED:
        shp = w[n].shape
        two = lambda a: a.reshape(-1, shp[-1])
        d_, m_, v_ = adamw(f"adamw_{n}", two(w[n]), two(g_sharded[n]), two(m[n]), two(v[n]))
        out_g[n], out_d[n], out_m[n], out_v[n] = g_sharded[n], d_.reshape(shp), m_.reshape(shp), v_.reshape(shp)
    d_, m_, v_ = adamw("adamw_small", *[_pack_rows([d[n] for n in SMALL]) for d in (w, g_small, m, v)])
    small_shapes = [w[n].shape for n in SMALL]
    for n, dd, mm, vv in zip(SMALL, _unpack(d_, small_shapes), _unpack(m_, small_shapes), _unpack(v_, small_shapes)):
        out_g[n], out_d[n], out_m[n], out_v[n] = g_small[n], dd, mm, vv

    return (loss, grad_x[None], *[out_g[n] for n in WEIGHTS], *[out_d[n] for n in WEIGHTS],
            *[out_m[n] for n in WEIGHTS], *[out_v[n] for n in WEIGHTS])
```

```python
import math

import jax
import jax.numpy as jnp
from jax import lax
from jax.experimental import pallas as pl
from jax.experimental.pallas import tpu as pltpu

F32 = jnp.float32
BF16 = jnp.bfloat16

D_MODEL = 1024
DEPTH = 2
SSM_HEADS = 16
SSM_HEAD_DIM = 64
SSM_D_INNER = 1024
SSM_STATE = 128
SSM_CONV_DIM = 1536
GDN_HEADS = 8
GDN_HEAD = 128
GDN_QKV_DIM = 3072
CONV_K = 4
CHUNK = 64
FFN_HIDDEN = 2816
EPS = 1e-6
IN_DIM = 8736

ADAM_LR = 0.001
ADAM_B1 = 0.9
ADAM_B2 = 0.999
ADAM_EPS = 1e-08
ADAM_WD = 0.01
ADAM_STEP = 10

LANE = 128
NEG_BIG = -1e30
VMEM_LIMIT = 56 * 1024 * 1024

AL_Z, AL_XBC, AL_QKV, AL_GZ, AL_GS, AL_GG, AL_SMALL, AL_DIM = 0, 1024, 2560, 5632, 6656, 7680, 8704, 9216
SM_DT, SM_A, SM_B = 0, 16, 24

HI = lax.Precision.HIGHEST
NN = (((1,), (0,)), ((), ()))
NT = (((1,), (1,)), ((), ()))
TN = (((0,), (0,)), ((), ()))


def _cparams(sem):
    return pltpu.CompilerParams(dimension_semantics=sem, vmem_limit_bytes=VMEM_LIMIT)


def _pick(n, prefs):
    for p in prefs:
        if n % p == 0:
            return p
    return n


MATMUL_VMEM_BUDGET = 40 << 20
MXU_WIDTH = 256
HBM_BYTES_PER_S = 3.3e12
MXU_FLOPS_PER_S = 9.0e14
GRID_STEP_S = 0.35e-6
N_CHIPS = 4


def _matmul_tiles(m, n, n_dom, k, a_item, b_item, o_item):
    best = None
    def cands(dim, cap):
        return [c for c in range(LANE, min(dim, cap) + 1, LANE) if dim % c == 0] or [dim]

    tms, tns, tks = cands(m, 2048), cands(n_dom, 2304), cands(k, 1 << 30)
    for tm in tms:
        for tn in tns:
            for tk in tks:
                nk = k // tk
                vmem = (2 * (tm * tk * a_item + tk * tn * b_item + tm * tn * o_item) + tm * tn * 4 * (2 if nk > 1 else 1)
                        + (tm * tk * 2 if a_item > 2 else 0) + (tk * tn * 2 if b_item > 2 else 0))
                if vmem > MATMUL_VMEM_BUDGET:
                    continue
                traffic = m * k * a_item * (1 if nk == 1 else n // tn) + k * n * b_item * (m // tm)
                mxu_fill = tn / (-(-tn // MXU_WIDTH) * MXU_WIDTH)
                cost = (max(traffic / HBM_BYTES_PER_S, 2.0 * m * n * k / (MXU_FLOPS_PER_S * mxu_fill))
                        + (m // tm) * (n // tn) * nk * GRID_STEP_S)
                if best is None or cost < best[0]:
                    best = (cost, (tm, tn, tk))
    return best[1]


def _dot(a, b, dims=NN):
    return lax.dot_general(a.astype(BF16), b.astype(BF16), dims, preferred_element_type=F32)


def _dot3(a, b, dims=NN):
    a_hi, b_hi = a.astype(BF16), b.astype(BF16)
    a_lo = (a - a_hi.astype(F32)).astype(BF16)
    b_lo = (b - b_hi.astype(F32)).astype(BF16)

    def dg(u, v):
        return lax.dot_general(u, v, dims, preferred_element_type=F32)

    return dg(a_hi, b_hi) + (dg(a_hi, b_lo) + dg(a_lo, b_hi))


def _dot_hi(a, b, dims=NN):
    return lax.dot_general(a, b, dims, precision=HI, preferred_element_type=F32)


def _sigmoid(x):
    return jax.nn.sigmoid(x)


def _silu(x):
    return x * _sigmoid(x)


def _softplus(x):
    return jnp.maximum(x, 0.0) + jnp.log1p(jnp.exp(-jnp.abs(x)))


def matmul(name, a, b, mode, out_dtype=F32, chip_major=False, stack=None):
    if mode == "nn":
        (m, k), (k2, n) = a.shape, b.shape
    elif mode == "nt":
        (m, k), (n, k2) = a.shape, b.shape
    else:
        (k, m), (k2, n) = a.shape, b.shape
    assert k == k2, (a.shape, b.shape, mode)
    shard = n // N_CHIPS if chip_major else n
    tm, tn, tk = _matmul_tiles(m, n, shard, k, a.dtype.itemsize, b.dtype.itemsize, jnp.dtype(out_dtype).itemsize)
    if chip_major:
        per = shard // tn
        base_shape, base_blk = (N_CHIPS, m, shard), (None, tm, tn)
        base_idx = lambda i, j: (j // per, i, j % per)
    else:
        base_shape, base_blk = (m, n), (tm, tn)
        base_idx = lambda i, j: (i, j)
    nk = k // tk
    dims = {"nn": NN, "nt": NT, "tn": TN}[mode]

    def body_acc(a_ref, b_ref, o_ref, acc_ref):
        kk = pl.program_id(2)

        @pl.when(kk == 0)
        def _():
            acc_ref[...] = jnp.zeros_like(acc_ref)

        acc_ref[...] += _dot(a_ref[...], b_ref[...], dims)

        @pl.when(kk == nk - 1)
        def _():
            o_ref[...] = acc_ref[...].astype(o_ref.dtype)

    def body_one(a_ref, b_ref, o_ref):
        o_ref[...] = _dot(a_ref[...], b_ref[...], dims).astype(o_ref.dtype)

    compute = body_one if nk == 1 else body_acc
    if mode == "tn":
        a_spec = pl.BlockSpec((tk, tm), lambda i, j, kk: (kk, i))
    else:
        a_spec = pl.BlockSpec((tm, tk), lambda i, j, kk: (i, kk))
    if mode == "nt":
        b_spec = pl.BlockSpec((tn, tk), lambda i, j, kk: (j, kk))
    else:
        b_spec = pl.BlockSpec((tk, tn), lambda i, j, kk: (kk, j))
    in_specs, operands, aliases, body = [a_spec, b_spec], [a, b], {}, compute
    if stack is None:
        out_shape, out_blk, out_idx = base_shape, base_blk, (lambda i, j, kk: base_idx(i, j))
    else:
        layer, buf = stack
        out_shape, out_blk = (DEPTH,) + base_shape, (None,) + base_blk
        out_idx = lambda i, j, kk: (layer,) + base_idx(i, j)
        if buf is not None:
            assert buf.shape == out_shape and buf.dtype == out_dtype
            in_specs.append(pl.BlockSpec(memory_space=pl.ANY))
            operands.append(buf)
            aliases = {2: 0}

            def body(a_ref, b_ref, buf_ref, *rest):
                del buf_ref
                compute(a_ref, b_ref, *rest)

    return pl.pallas_call(
        body, name=name,
        out_shape=jax.ShapeDtypeStruct(out_shape, out_dtype),
        grid=(m // tm, n // tn, nk),
        in_specs=in_specs,
        out_specs=pl.BlockSpec(out_blk, out_idx),
        scratch_shapes=[] if nk == 1 else [pltpu.VMEM((tm, tn), F32)],
        input_output_aliases=aliases,
        compiler_params=_cparams(("parallel", "parallel", "arbitrary")),
    )(*operands)


def _row_map(c0, moves):
    return (lambda j, i: (i, c0 + j)) if moves else (lambda j, i: (i, c0))


def _par_map(c0, moves):
    return (lambda j, i: (0, c0 + j)) if moves else (lambda j, i: (0, c0))


def _in_spec(op, tile):
    _, kind, w, c0, moves = op
    if kind == "row":
        return pl.BlockSpec((tile, w), _row_map(c0, moves))
    return pl.BlockSpec((1, w), _par_map(c0, moves))


ROW_BLOCK_ELEMS = 1 << 18


def _row_tile(t, tile, ops):
    width = max(op[2] for op in ops if op[1] == "row")
    return min(t, max(tile, ROW_BLOCK_ELEMS // width))


def rowwise_fwd(name, fn, t, tile, ncol, ins, outs):
    n_in = len(ins)
    tile = _row_tile(t, tile, ins)

    def body(*refs):
        vals = [r[...].astype(F32) for r in refs[:n_in]]
        res = fn(*vals)
        if not isinstance(res, (tuple, list)):
            res = (res,)
        for r, v in zip(refs[n_in:], res):
            r[...] = v.astype(r.dtype)

    res = pl.pallas_call(
        body, name=name,
        out_shape=[jax.ShapeDtypeStruct((t, w * ncol), dt) for w, dt in outs],
        grid=(ncol, t // tile),
        in_specs=[_in_spec(op, tile) for op in ins],
        out_specs=[pl.BlockSpec((tile, w), _row_map(0, True)) for w, _ in outs],
        compiler_params=_cparams(("arbitrary", "arbitrary")),
    )(*[op[0] for op in ins])
    return res


def rowwise_bwd(name, fn, t, tile, ncol, ins, need, cts, addends=None, row_dtypes=None):
    n_in, n_ct = len(ins), len(cts)
    tile = _row_tile(t, tile, ins)
    addends = addends or {}
    row_dtypes = row_dtypes or {}
    didx = [i for i in range(n_in) if need[i]]
    add_ops = [addends[i] for i in didx if i in addends]
    n_add = len(add_ops)

    def body(*refs):
        in_refs = refs[:n_in]
        ct_refs = refs[n_in:n_in + n_ct]
        add_refs = refs[n_in + n_ct:n_in + n_ct + n_add]
        out_refs = refs[n_in + n_ct + n_add:]
        vals = [r[...].astype(F32) for r in in_refs]

        def g(*dv):
            full = list(vals)
            for i, v in zip(didx, dv):
                full[i] = v
            res = fn(*full)
            return tuple(res) if isinstance(res, (tuple, list)) else (res,)

        _, vjp = jax.vjp(g, *[vals[i] for i in didx])
        grads = vjp(tuple(c[...].astype(F32) for c in ct_refs))
        j, i = pl.program_id(0), pl.program_id(1)
        a = 0
        for o_ref, gv, idx in zip(out_refs, grads, didx):
            _, kind, _, _, moves = ins[idx]
            if kind == "row":
                if idx in addends:
                    gv = gv + add_refs[a][...].astype(F32)
                    a += 1
                o_ref[...] = gv.astype(o_ref.dtype)
            else:
                first = (i == 0) if moves else jnp.logical_and(i == 0, j == 0)

                @pl.when(first)
                def _(o_ref=o_ref, gv=gv):
                    o_ref[...] = gv

                @pl.when(jnp.logical_not(first))
                def _(o_ref=o_ref, gv=gv):
                    o_ref[...] += gv

    out_shape, out_specs = [], []
    for idx in didx:
        _, kind, w, _, moves = ins[idx]
        cols = w * (ncol if moves else 1)
        if kind == "row":
            out_shape.append(jax.ShapeDtypeStruct((t, cols), row_dtypes.get(idx, F32)))
            out_specs.append(pl.BlockSpec((tile, w), _row_map(0, moves)))
        else:
            out_shape.append(jax.ShapeDtypeStruct((1, cols), F32))
            out_specs.append(pl.BlockSpec((1, w), _par_map(0, moves)))
    ops = list(ins) + list(cts) + add_ops
    res = pl.pallas_call(
        body, name=name,
        out_shape=out_shape,
        grid=(ncol, t // tile),
        in_specs=[_in_spec(op, tile) for op in ops],
        out_specs=out_specs,
        compiler_params=_cparams(("arbitrary", "arbitrary")),
    )(*[op[0] for op in ops])
    return res


def f_rmsnorm(x, w):
    return x * lax.rsqrt(jnp.mean(x * x, axis=-1, keepdims=True) + EPS) * w


def f_ssd_post(y, z, w):
    y = y * _silu(z)
    return y * lax.rsqrt(jnp.mean(y * y, axis=-1, keepdims=True) + EPS) * w


def f_gdn_post(o, z, w):
    o = o * lax.rsqrt(jnp.mean(o * o, axis=-1, keepdims=True) + EPS) * w
    return o * _silu(z)


def f_merge(gs, p1, gg, p2):
    return _sigmoid(gs) * p1 + _sigmoid(gg) * p2


def f_swiglu(g, u):
    return _silu(g) * u


def final_loss(name, x, tgt, w, tile=256):
    t, d = x.shape

    def body(x_ref, t_ref, w_ref, loss_ref, dx_ref, dw_ref):
        i = pl.program_id(0)
        xv, tv, wv = x_ref[...], t_ref[...], w_ref[...]

        def g(xx, ww):
            err = f_rmsnorm(xx, ww) - tv
            return 0.5 * jnp.sum(jnp.mean(err * err, axis=-1, keepdims=True), axis=0, keepdims=True)

        val, vjp = jax.vjp(g, xv, wv)
        dx, dw = vjp(jnp.ones((1, 1), F32))
        dx_ref[...] = dx
        lv = jnp.broadcast_to(val, (1, LANE))

        @pl.when(i == 0)
        def _():
            loss_ref[...] = lv
            dw_ref[...] = dw

        @pl.when(i != 0)
        def _():
            loss_ref[...] += lv
            dw_ref[...] += dw

    return pl.pallas_call(
        body, name=name,
        out_shape=[jax.ShapeDtypeStruct((1, LANE), F32), jax.ShapeDtypeStruct((t, d), F32),
                   jax.ShapeDtypeStruct((1, d), F32)],
        grid=(t // tile,),
        in_specs=[pl.BlockSpec((tile, d), lambda i: (i, 0)), pl.BlockSpec((tile, d), lambda i: (i, 0)),
                  pl.BlockSpec((1, d), lambda i: (0, 0))],
        out_specs=[pl.BlockSpec((1, LANE), lambda i: (0, 0)), pl.BlockSpec((tile, d), lambda i: (i, 0)),
                   pl.BlockSpec((1, d), lambda i: (0, 0))],
        compiler_params=_cparams(("arbitrary",)),
    )(x, tgt, w)


CONV_W = 512
HALO = 8


def conv_fwd(name, src, c0, width, w, b, tile=512):
    t = src.shape[0]
    ncol, nrow = width // CONV_W, t // tile
    cb0 = c0 // CONV_W
    hb = tile // HALO

    def body(prev_ref, cur_ref, w_ref, b_ref, o_ref, ext_ref):
        i = pl.program_id(1)
        ext_ref[0:HALO, :] = jnp.where(i == 0, 0.0, prev_ref[...])
        ext_ref[HALO:HALO + tile, :] = cur_ref[...]
        acc = jnp.broadcast_to(b_ref[...], (tile, CONV_W))
        for k in range(CONV_K):
            acc = acc + w_ref[k:k + 1, :] * ext_ref[pl.ds(HALO - (CONV_K - 1) + k, tile), :]
        o_ref[...] = _silu(acc)

    return pl.pallas_call(
        body, name=name,
        out_shape=jax.ShapeDtypeStruct((t, width), F32),
        grid=(ncol, nrow),
        in_specs=[pl.BlockSpec((HALO, CONV_W), lambda j, i: (jnp.maximum(i * hb - 1, 0), cb0 + j)),
                  pl.BlockSpec((tile, CONV_W), lambda j, i: (i, cb0 + j)),
                  pl.BlockSpec((CONV_K, CONV_W), lambda j, i: (0, j)),
                  pl.BlockSpec((1, CONV_W), lambda j, i: (0, j))],
        out_specs=pl.BlockSpec((tile, CONV_W), lambda j, i: (i, j)),
        scratch_shapes=[pltpu.VMEM((tile + HALO, CONV_W), F32)],
        compiler_params=_cparams(("arbitrary", "arbitrary")),
    )(src, src, w, b)


def conv_bwd_pre(name, src, c0, width, w, b, dy, tile=512):
    t = src.shape[0]
    ncol, nrow = width // CONV_W, t // tile
    cb0 = c0 // CONV_W
    hb = tile // HALO

    def body(prev_ref, cur_ref, w_ref, b_ref, dy_ref, dpre_ref, dw_ref, db_ref, ext_ref):
        i = pl.program_id(1)
        ext_ref[0:HALO, :] = jnp.where(i == 0, 0.0, prev_ref[...])
        ext_ref[HALO:HALO + tile, :] = cur_ref[...]
        pre = jnp.broadcast_to(b_ref[...], (tile, CONV_W))
        for k in range(CONV_K):
            pre = pre + w_ref[k:k + 1, :] * ext_ref[pl.ds(HALO - (CONV_K - 1) + k, tile), :]
        s = _sigmoid(pre)
        dpre = dy_ref[...] * (s * (1.0 + pre * (1.0 - s)))
        dpre_ref[...] = dpre

        @pl.when(i == 0)
        def _():
            dw_ref[...] = jnp.zeros_like(dw_ref)
            db_ref[...] = jnp.zeros_like(db_ref)

        for k in range(CONV_K):
            dw_ref[k:k + 1, :] += jnp.sum(dpre * ext_ref[pl.ds(HALO - (CONV_K - 1) + k, tile), :],
                                          axis=0, keepdims=True)
        db_ref[...] += jnp.sum(dpre, axis=0, keepdims=True)

    return pl.pallas_call(
        body, name=name,
        out_shape=[jax.ShapeDtypeStruct((t, width), F32), jax.ShapeDtypeStruct((CONV_K, width), F32),
                   jax.ShapeDtypeStruct((1, width), F32)],
        grid=(ncol, nrow),
        in_specs=[pl.BlockSpec((HALO, CONV_W), lambda j, i: (jnp.maximum(i * hb - 1, 0), cb0 + j)),
                  pl.BlockSpec((tile, CONV_W), lambda j, i: (i, cb0 + j)),
                  pl.BlockSpec((CONV_K, CONV_W), lambda j, i: (0, j)),
                  pl.BlockSpec((1, CONV_W), lambda j, i: (0, j)),
                  pl.BlockSpec((tile, CONV_W), lambda j, i: (i, j))],
        out_specs=[pl.BlockSpec((tile, CONV_W), lambda j, i: (i, j)),
                   pl.BlockSpec((CONV_K, CONV_W), lambda j, i: (0, j)),
                   pl.BlockSpec((1, CONV_W), lambda j, i: (0, j))],
        scratch_shapes=[pltpu.VMEM((tile + HALO, CONV_W), F32)],
        compiler_params=_cparams(("arbitrary", "arbitrary")),
    )(src, src, w, b, dy)


def conv_bwd_in(name, dpre, w, tile=512):
    t, width = dpre.shape
    ncol, nrow = width // CONV_W, t // tile
    hb = tile // HALO
    last_hb = t // HALO - 1

    def body(cur_ref, nxt_ref, w_ref, o_ref, ext_ref):
        i = pl.program_id(1)
        ext_ref[0:tile, :] = cur_ref[...]
        ext_ref[tile:tile + HALO, :] = jnp.where(i == nrow - 1, 0.0, nxt_ref[...])
        acc = jnp.zeros((tile, CONV_W), F32)
        for k in range(CONV_K):
            acc = acc + w_ref[k:k + 1, :] * ext_ref[pl.ds(CONV_K - 1 - k, tile), :]
        o_ref[...] = acc.astype(o_ref.dtype)

    return pl.pallas_call(
        body, name=name,
        out_shape=jax.ShapeDtypeStruct((t, width), BF16),
        grid=(ncol, nrow),
        in_specs=[pl.BlockSpec((tile, CONV_W), lambda j, i: (i, j)),
                  pl.BlockSpec((HALO, CONV_W), lambda j, i: (jnp.minimum((i + 1) * hb, last_hb), j)),
                  pl.BlockSpec((CONV_K, CONV_W), lambda j, i: (0, j))],
        out_specs=pl.BlockSpec((tile, CONV_W), lambda j, i: (i, j)),
        scratch_shapes=[pltpu.VMEM((tile + HALO, CONV_W), F32)],
        compiler_params=_cparams(("arbitrary", "arbitrary")),
    )(dpre, dpre, w)


def _iota2(q):
    return (lax.broadcasted_iota(jnp.int32, (q, q), 0), lax.broadcasted_iota(jnp.int32, (q, q), 1))


def _lane_pick(blk, idx):
    lane = lax.broadcasted_iota(jnp.int32, (1, LANE), 1)
    return jnp.sum(jnp.where(lane == idx, blk, 0.0), axis=1, keepdims=True)


class _Decay:
    def __init__(self, a):
        q = a.shape[0]
        r, c = _iota2(q)
        self.r, self.c = r, c
        self.cum = _dot_hi((c <= r).astype(F32), a)
        self.cum_t = _dot_hi(a, (r <= c).astype(F32), TN)
        self.tot = self.cum[q - 1:q, :]
        self.e_cum = jnp.exp(self.cum)
        self.e_rest = jnp.exp(self.tot - self.cum)
        self.e_tot = jnp.exp(self.tot)

    def mask(self, lane):
        rows = lax.broadcasted_iota(jnp.int32, (LANE, 1), 0)
        cum_row = jnp.sum(jnp.where(rows == lane, self.cum_t, 0.0), axis=0, keepdims=True)
        return jnp.exp(jnp.where(self.r >= self.c, _lane_pick(self.cum, lane) - cum_row, NEG_BIG))


_SSD_B = SSM_D_INNER
_SSD_C = SSM_D_INNER + 2 * SSM_STATE


def _interleave(gens):
    results = [None] * len(gens)
    live = list(range(len(gens)))
    while live:
        for i in list(live):
            try:
                next(gens[i])
            except StopIteration as stop:
                results[i] = stop.value
                live.remove(i)
    return results


def ssd_chunk(xs, bm, cm, dt_all, dsk, dec, state, p, cb):
    lane = lax.broadcasted_iota(jnp.int32, (1, LANE), 1)
    m0 = lane < SSM_HEAD_DIM
    h0, h1 = 2 * p, 2 * p + 1

    def both(blk):
        return jnp.where(m0, _lane_pick(blk, h0), _lane_pick(blk, h1))

    xdt = xs * both(dt_all)
    l0, l1 = dec.mask(h0), dec.mask(h1)
    yield
    y_diag = _dot(cb * l0, jnp.where(m0, xdt, 0.0)) + _dot(cb * l1, jnp.where(m0, 0.0, xdt))
    y_off = _dot(cm, state, NT) * both(dec.e_cum)
    yield
    rowm = lax.broadcasted_iota(jnp.int32, (LANE, 1), 0) < SSM_HEAD_DIM
    new_state = (state * jnp.where(rowm, _lane_pick(dec.e_tot, h0), _lane_pick(dec.e_tot, h1))
                 + _dot(xdt * both(dec.e_rest), bm, TN))
    y = y_diag + y_off + both(dsk) * xs
    return y, new_state


def ssd_pairs(xs, bms, cms, small, dtb, alog, dsk, states):
    dt_all = _softplus(small + dtb)
    dec = _Decay(dt_all * (-jnp.exp(alog)))
    cbs = [_dot(cm, bm, NT) for cm, bm in zip(cms, bms)]
    res = _interleave([ssd_chunk(x, bms[p // 4], cms[p // 4], dt_all, dsk, dec, st, p, cbs[p // 4])
                       for p, (x, st) in enumerate(zip(xs, states))])
    return tuple(y for y, _ in res), tuple(s for _, s in res)


def tri_inverse(a):
    q = a.shape[0]
    r, c = _iota2(q)
    eye = (r == c).astype(F32)
    diag = (r // 16) == (c // 16)
    bd = jnp.where(diag, a, 0.0)
    off = jnp.where(diag, 0.0, a)
    b2 = _dot3(bd, bd)
    d1 = _dot3(eye - bd, eye + b2)
    yield
    b4 = _dot3(b2, b2)
    yield
    b8 = _dot3(b4, b4)
    d2 = _dot3(d1, eye + b4)
    yield
    dinv = _dot3(d2, eye + b8)
    yield
    n = _dot3(dinv, off)
    yield
    n2 = _dot3(n, n)
    yield
    m = _dot3(eye + n2, dinv)
    yield
    return _dot3(eye - n, m)


@jax.custom_vjp
def _solve_with(xinv, a, rhs):
    del a
    return _dot3(xinv, rhs)


def _solve_with_fwd(xinv, a, rhs):
    t = _dot3(xinv, rhs)
    return t, (xinv, t)


def _solve_with_bwd(res, dt):
    xinv, t = res
    d_rhs = _dot3(xinv, dt, TN)
    d_a = -_dot(d_rhs, t, NT)
    return jnp.zeros_like(xinv), d_a, d_rhs


_solve_with.defvjp(_solve_with_fwd, _solve_with_bwd)


_GDN_K = GDN_HEADS * GDN_HEAD
_GDN_V = 2 * GDN_HEADS * GDN_HEAD


def gdn_chunk(qh, kh, vh, beta_all, dec, state, h, xinv=None):
    r, c = dec.r, dec.c
    qn = qh * lax.rsqrt(jnp.sum(qh * qh, axis=-1, keepdims=True) + EPS) * (GDN_HEAD ** -0.5)
    kn = kh * lax.rsqrt(jnp.sum(kh * kh, axis=-1, keepdims=True) + EPS)
    beta = _lane_pick(beta_all, SM_B + h)
    decay = dec.mask(SM_A + h)
    yield
    kk = _dot(kn, kn, NT)
    qk = _dot(qn, kn, NT) * decay
    amat = jnp.where(r > c, kk * decay * beta, 0.0)
    eg = _lane_pick(dec.e_cum, SM_A + h)
    rhs = jnp.concatenate([vh * beta, kn * (beta * eg)], axis=1)
    qs = _dot(qn * eg, state)
    yield
    if xinv is None:
        xinv = yield from tri_inverse(amat)
        t = _dot3(xinv, rhs)
    else:
        t = _solve_with(xinv, amat, rhs)
    yield
    u, w = t[:, :GDN_HEAD], t[:, GDN_HEAD:]
    v_new = u - _dot(w, state)
    yield
    o = qs + _dot(qk, v_new)
    new_state = (state * _lane_pick(dec.e_tot, SM_A + h)
                 + _dot(kn * _lane_pick(dec.e_rest, SM_A + h), v_new, TN))
    return o, new_state, xinv


def gdn_heads(qs, ks, vs, small, alog, dtb, states, xinvs=None):
    nh = len(qs)
    beta_all = _sigmoid(small)
    dec = _Decay(-jnp.exp(alog) * _softplus(small + dtb))
    res = _interleave([gdn_chunk(qs[h], ks[h], vs[h], beta_all, dec, states[h], h,
                                 None if xinvs is None else xinvs[h]) for h in range(nh)])
    return tuple(o for o, _, _ in res), tuple(s for _, s, _ in res), tuple(x for _, _, x in res)


def _acc(ref, val, first):
    @pl.when(first)
    def _():
        ref[...] = val

    @pl.when(jnp.logical_not(first))
    def _():
        ref[...] += val


def ssd_scan_fwd(name, xbc, proj, dtb, alog, dsk):
    t = xbc.shape[0]
    nc, npair = t // CHUNK, SSM_HEADS // 2
    small_blk = AL_SMALL // LANE

    def body(xbc_ref, sm_ref, dtb_ref, alog_ref, dsk_ref, y_ref, sin_ref, st_ref):
        ci = pl.program_id(0)

        @pl.when(ci == 0)
        def _():
            st_ref[...] = jnp.zeros_like(st_ref)

        s_in = tuple(st_ref[p] for p in range(npair))
        ys, s_new = ssd_pairs(tuple(xbc_ref[:, p * LANE:(p + 1) * LANE] for p in range(npair)),
                              tuple(xbc_ref[:, _SSD_B + g * LANE:_SSD_B + (g + 1) * LANE] for g in range(2)),
                              tuple(xbc_ref[:, _SSD_C + g * LANE:_SSD_C + (g + 1) * LANE] for g in range(2)),
                              sm_ref[...], dtb_ref[...], alog_ref[...], dsk_ref[...], s_in)
        for p in range(npair):
            sin_ref[0, p] = s_in[p]
            y_ref[:, p * LANE:(p + 1) * LANE] = ys[p]
            st_ref[p] = s_new[p]

    par = pl.BlockSpec((1, LANE), lambda ci: (0, 0))
    return pl.pallas_call(
        body, name=name,
        out_shape=[jax.ShapeDtypeStruct((t, SSM_D_INNER), F32),
                   jax.ShapeDtypeStruct((nc, npair, LANE, LANE), F32)],
        grid=(nc,),
        in_specs=[pl.BlockSpec((CHUNK, SSM_CONV_DIM), lambda ci: (ci, 0)),
                  pl.BlockSpec((CHUNK, LANE), lambda ci: (ci, small_blk)),
                  par, par, par],
        out_specs=[pl.BlockSpec((CHUNK, SSM_D_INNER), lambda ci: (ci, 0)),
                   pl.BlockSpec((1, npair, LANE, LANE), lambda ci: (ci, 0, 0, 0))],
        scratch_shapes=[pltpu.VMEM((npair, LANE, LANE), F32)],
        compiler_params=_cparams(("arbitrary",)),
    )(xbc, proj, dtb, alog, dsk)


def ssd_scan_bwd(name, xbc, proj, dtb, alog, dsk, s_in, dy):
    t = xbc.shape[0]
    nc, npair = t // CHUNK, SSM_HEADS // 2
    small_blk = AL_SMALL // LANE

    def body(xbc_ref, sm_ref, dtb_ref, alog_ref, dsk_ref, sin_ref, dy_ref,
             dxbc_ref, dsm_ref, ddtb_ref, dalog_ref, ddsk_ref, dst_ref):
        ci = pl.program_id(0)

        @pl.when(ci == 0)
        def _():
            dst_ref[...] = jnp.zeros_like(dst_ref)

        _, vjp = jax.vjp(ssd_pairs, tuple(xbc_ref[:, p * LANE:(p + 1) * LANE] for p in range(npair)),
                         tuple(xbc_ref[:, _SSD_B + g * LANE:_SSD_B + (g + 1) * LANE] for g in range(2)),
                         tuple(xbc_ref[:, _SSD_C + g * LANE:_SSD_C + (g + 1) * LANE] for g in range(2)),
                         sm_ref[...], dtb_ref[...], alog_ref[...], dsk_ref[...],
                         tuple(sin_ref[0, p] for p in range(npair)))
        dxs, dbms, dcms, dsm, ddtb, dalog, ddsk, dsts = vjp(
            (tuple(dy_ref[:, p * LANE:(p + 1) * LANE] for p in range(npair)),
             tuple(dst_ref[p] for p in range(npair))))
        for p in range(npair):
            dxbc_ref[:, p * LANE:(p + 1) * LANE] = dxs[p]
            dst_ref[p] = dsts[p]
        for g in range(2):
            dxbc_ref[:, _SSD_B + g * LANE:_SSD_B + (g + 1) * LANE] = dbms[g]
            dxbc_ref[:, _SSD_C + g * LANE:_SSD_C + (g + 1) * LANE] = dcms[g]
        dsm_ref[...] = dsm
        _acc(ddtb_ref, ddtb, ci == 0)
        _acc(dalog_ref, dalog, ci == 0)
        _acc(ddsk_ref, ddsk, ci == 0)

    par = pl.BlockSpec((1, LANE), lambda ci: (0, 0))
    rev = lambda ci: nc - 1 - ci
    return pl.pallas_call(
        body, name=name,
        out_shape=[jax.ShapeDtypeStruct((t, SSM_CONV_DIM), F32),
                   jax.ShapeDtypeStruct((t, LANE), F32),
                   jax.ShapeDtypeStruct((1, LANE), F32), jax.ShapeDtypeStruct((1, LANE), F32),
                   jax.ShapeDtypeStruct((1, LANE), F32)],
        grid=(nc,),
        in_specs=[pl.BlockSpec((CHUNK, SSM_CONV_DIM), lambda ci: (rev(ci), 0)),
                  pl.BlockSpec((CHUNK, LANE), lambda ci: (rev(ci), small_blk)),
                  par, par, par,
                  pl.BlockSpec((1, npair, LANE, LANE), lambda ci: (rev(ci), 0, 0, 0)),
                  pl.BlockSpec((CHUNK, SSM_D_INNER), lambda ci: (rev(ci), 0))],
        out_specs=[pl.BlockSpec((CHUNK, SSM_CONV_DIM), lambda ci: (rev(ci), 0)),
                   pl.BlockSpec((CHUNK, LANE), lambda ci: (rev(ci), 0)),
                   par, par, par],
        scratch_shapes=[pltpu.VMEM((npair, LANE, LANE), F32)],
        compiler_params=_cparams(("arbitrary",)),
    )(xbc, proj, dtb, alog, dsk, s_in, dy)


def gdn_scan_fwd(name, qkv, proj, alog, dtb):
    t = qkv.shape[0]
    nc, nh = t // CHUNK, GDN_HEADS
    small_blk = AL_SMALL // LANE

    def body(qkv_ref, sm_ref, alog_ref, dtb_ref, o_ref, sin_ref, x_ref, st_ref):
        ci = pl.program_id(0)

        @pl.when(ci == 0)
        def _():
            st_ref[...] = jnp.zeros_like(st_ref)

        s_in = tuple(st_ref[h] for h in range(nh))
        os, s_new, xinvs = gdn_heads(
            tuple(qkv_ref[:, h * LANE:(h + 1) * LANE] for h in range(nh)),
            tuple(qkv_ref[:, _GDN_K + h * LANE:_GDN_K + (h + 1) * LANE] for h in range(nh)),
            tuple(qkv_ref[:, _GDN_V + h * LANE:_GDN_V + (h + 1) * LANE] for h in range(nh)),
            sm_ref[...], alog_ref[...], dtb_ref[...], s_in)
        for h in range(nh):
            sin_ref[0, h] = s_in[h]
            o_ref[:, h * LANE:(h + 1) * LANE] = os[h]
            x_ref[0, h] = xinvs[h]
            st_ref[h] = s_new[h]

    par = pl.BlockSpec((1, LANE), lambda ci: (0, 0))
    return pl.pallas_call(
        body, name=name,
        out_shape=[jax.ShapeDtypeStruct((t, GDN_HEADS * GDN_HEAD), F32),
                   jax.ShapeDtypeStruct((nc, nh, LANE, LANE), F32),
                   jax.ShapeDtypeStruct((nc, nh, CHUNK, CHUNK), F32)],
        grid=(nc,),
        in_specs=[pl.BlockSpec((CHUNK, GDN_QKV_DIM), lambda ci: (ci, 0)),
                  pl.BlockSpec((CHUNK, LANE), lambda ci: (ci, small_blk)),
                  par, par],
        out_specs=[pl.BlockSpec((CHUNK, GDN_HEADS * GDN_HEAD), lambda ci: (ci, 0)),
                   pl.BlockSpec((1, nh, LANE, LANE), lambda ci: (ci, 0, 0, 0)),
                   pl.BlockSpec((1, nh, CHUNK, CHUNK), lambda ci: (ci, 0, 0, 0))],
        scratch_shapes=[pltpu.VMEM((nh, LANE, LANE), F32)],
        compiler_params=_cparams(("arbitrary",)),
    )(qkv, proj, alog, dtb)


def gdn_scan_bwd(name, qkv, proj, alog, dtb, s_in, xinv, do, dsm_in):
    t = qkv.shape[0]
    nc, nh = t // CHUNK, GDN_HEADS
    small_blk = AL_SMALL // LANE

    def body(qkv_ref, sm_ref, alog_ref, dtb_ref, sin_ref, x_ref, do_ref, dsmi_ref,
             dqkv_ref, dsm_ref, dalog_ref, ddtb_ref, dst_ref):
        ci = pl.program_id(0)

        @pl.when(ci == 0)
        def _():
            dst_ref[...] = jnp.zeros_like(dst_ref)

        xis = tuple(x_ref[0, h] for h in range(nh))

        def fn(qs, ks, vs, sm, alog_, dtb_, sts):
            os, s_new, _ = gdn_heads(qs, ks, vs, sm, alog_, dtb_, sts, xinvs=xis)
            return os, s_new

        _, vjp = jax.vjp(fn, tuple(qkv_ref[:, h * LANE:(h + 1) * LANE] for h in range(nh)),
                         tuple(qkv_ref[:, _GDN_K + h * LANE:_GDN_K + (h + 1) * LANE] for h in range(nh)),
                         tuple(qkv_ref[:, _GDN_V + h * LANE:_GDN_V + (h + 1) * LANE] for h in range(nh)),
                         sm_ref[...], alog_ref[...], dtb_ref[...], tuple(sin_ref[0, h] for h in range(nh)))
        dqs, dks, dvs, dsm, dalog, ddtb, dsts = vjp(
            (tuple(do_ref[:, h * LANE:(h + 1) * LANE] for h in range(nh)), tuple(dst_ref[h] for h in range(nh))))
        for h in range(nh):
            dqkv_ref[:, h * LANE:(h + 1) * LANE] = dqs[h]
            dqkv_ref[:, _GDN_K + h * LANE:_GDN_K + (h + 1) * LANE] = dks[h]
            dqkv_ref[:, _GDN_V + h * LANE:_GDN_V + (h + 1) * LANE] = dvs[h]
            dst_ref[h] = dsts[h]
        dsm_ref[...] = dsmi_ref[...] + dsm
        _acc(dalog_ref, dalog, ci == 0)
        _acc(ddtb_ref, ddtb, ci == 0)

    par = pl.BlockSpec((1, LANE), lambda ci: (0, 0))
    rev = lambda ci: nc - 1 - ci
    return pl.pallas_call(
        body, name=name,
        out_shape=[jax.ShapeDtypeStruct((t, GDN_QKV_DIM), F32), jax.ShapeDtypeStruct((t, LANE), F32),
                   jax.ShapeDtypeStruct((1, LANE), F32), jax.ShapeDtypeStruct((1, LANE), F32)],
        grid=(nc,),
        in_specs=[pl.BlockSpec((CHUNK, GDN_QKV_DIM), lambda ci: (rev(ci), 0)),
                  pl.BlockSpec((CHUNK, LANE), lambda ci: (rev(ci), small_blk)),
                  par, par,
                  pl.BlockSpec((1, nh, LANE, LANE), lambda ci: (rev(ci), 0, 0, 0)),
                  pl.BlockSpec((1, nh, CHUNK, CHUNK), lambda ci: (rev(ci), 0, 0, 0)),
                  pl.BlockSpec((CHUNK, GDN_HEADS * GDN_HEAD), lambda ci: (rev(ci), 0)),
                  pl.BlockSpec((CHUNK, LANE), lambda ci: (rev(ci), 0))],
        out_specs=[pl.BlockSpec((CHUNK, GDN_QKV_DIM), lambda ci: (rev(ci), 0)),
                   pl.BlockSpec((CHUNK, LANE), lambda ci: (rev(ci), 0)),
                   par, par],
        scratch_shapes=[pltpu.VMEM((nh, LANE, LANE), F32)],
        compiler_params=_cparams(("arbitrary",)),
    )(qkv, proj, alog, dtb, s_in, xinv, do, dsm_in)


def _row(arr, w, c0=0, moves=False):
    return (arr, "row", w, c0, moves)


def _par(arr, w, c0=0, moves=False):
    return (arr, "par", w, c0, moves)


def matmul_add(name, a, b, res):
    (m, k), (_, n) = a.shape, b.shape
    tm, tn, tk = _matmul_tiles(m, n, n, k, a.dtype.itemsize, b.dtype.itemsize, 4 + res.dtype.itemsize)
    nk = k // tk

    def body_acc(a_ref, b_ref, r_ref, o_ref, acc_ref):
        kk = pl.program_id(2)

        @pl.when(kk == 0)
        def _():
            acc_ref[...] = r_ref[...]

        acc_ref[...] += _dot(a_ref[...], b_ref[...])

        @pl.when(kk == nk - 1)
        def _():
            o_ref[...] = acc_ref[...]

    def body_one(a_ref, b_ref, r_ref, o_ref):
        o_ref[...] = r_ref[...] + _dot(a_ref[...], b_ref[...])

    body = body_one if nk == 1 else body_acc
    return pl.pallas_call(
        body, name=name,
        out_shape=jax.ShapeDtypeStruct((m, n), F32),
        grid=(m // tm, n // tn, nk),
        in_specs=[pl.BlockSpec((tm, tk), lambda i, j, kk: (i, kk)),
                  pl.BlockSpec((tk, tn), lambda i, j, kk: (kk, j)),
                  pl.BlockSpec((tm, tn), lambda i, j, kk: (i, j))],
        out_specs=pl.BlockSpec((tm, tn), lambda i, j, kk: (i, j)),
        scratch_shapes=[] if nk == 1 else [pltpu.VMEM((tm, tn), F32)],
        compiler_params=_cparams(("parallel", "parallel", "arbitrary")),
    )(a, b, res)


def layer_fwd(l, x, w):
    t = x.shape[0]
    rt = min(256, t)
    s = {"x": x}
    s["h"] = rowwise_fwd(f"norm_mix_l{l}", f_rmsnorm, t, rt, 1,
                         [_row(x, D_MODEL), _par(w["norm_mix_w"], D_MODEL)], [(D_MODEL, BF16)])[0]
    s["proj"] = matmul(f"in_proj_l{l}", s["h"], w["w_in"], "nn")
    s["xbc"] = conv_fwd(f"ssm_conv_l{l}", s["proj"], AL_XBC, SSM_CONV_DIM, w["ssm_conv_w"], w["ssm_conv_b"],
                        tile=min(512, t))
    s["qkv"] = conv_fwd(f"gdn_conv_l{l}", s["proj"], AL_QKV, GDN_QKV_DIM, w["gdn_conv_w"], w["gdn_conv_b"],
                        tile=min(512, t))
    s["y_scan"], s["ssd_sin"] = ssd_scan_fwd(f"ssd_scan_l{l}", s["xbc"], s["proj"], w["ssm_dt_bias"],
                                             w["ssm_a_log"], w["ssm_d"])
    s["o_scan"], s["gdn_sin"], s["gdn_x"] = gdn_scan_fwd(f"gdn_scan_l{l}", s["qkv"], s["proj"],
                                                         w["gdn_a_log"], w["gdn_dt_bias"])
    s["y_ssm"] = rowwise_fwd(f"ssd_post_l{l}", f_ssd_post, t, rt, 2,
                             [_row(s["y_scan"], 512, 0, True), _row(s["proj"], 512, AL_Z // 512, True),
                              _par(w["ssm_norm_w"], 512, 0, True)], [(512, BF16)])[0]
    s["y_gdn"] = rowwise_fwd(f"gdn_post_l{l}", f_gdn_post, t, rt, GDN_HEADS,
                             [_row(s["o_scan"], LANE, 0, True), _row(s["proj"], LANE, AL_GZ // LANE, True),
                              _par(w["gdn_norm_w"], LANE)], [(LANE, BF16)])[0]
    s["p1"] = matmul(f"proj_ssm_l{l}", s["y_ssm"], w["w_proj_ssm"], "nn")
    s["p2"] = matmul(f"proj_gdn_l{l}", s["y_gdn"], w["w_proj_gdn"], "nn")
    s["merged"] = rowwise_fwd(f"merge_l{l}", f_merge, t, rt, 2,
                              [_row(s["proj"], 512, AL_GS // 512, True), _row(s["p1"], 512, 0, True),
                               _row(s["proj"], 512, AL_GG // 512, True), _row(s["p2"], 512, 0, True)],
                              [(512, BF16)])[0]
    s["x1"] = matmul_add(f"out_proj_l{l}", s["merged"], w["w_out"], x)
    s["h2"] = rowwise_fwd(f"norm_ffn_l{l}", f_rmsnorm, t, rt, 1,
                          [_row(s["x1"], D_MODEL), _par(w["norm_ffn_w"], D_MODEL)], [(D_MODEL, BF16)])[0]
    s["gu"] = matmul(f"ffn_in_l{l}", s["h2"], w["w_ffn_in"], "nn")
    s["act"] = rowwise_fwd(f"swiglu_l{l}", f_swiglu, t, rt, FFN_HIDDEN // 256,
                           [_row(s["gu"], 256, 0, True), _row(s["gu"], 256, FFN_HIDDEN // 256, True)],
                           [(256, BF16)])[0]
    x2 = matmul_add(f"ffn_down_l{l}", s["act"], w["w_ffn_down"], s["x1"])
    return x2, s


IN_SHARD = IN_DIM // 4
IN_SHARD_PAD = 2304


def _aligned_to_shards(g):
    orig = jnp.concatenate([g[:, 0:2560], g[:, AL_SMALL:AL_SMALL + 16], g[:, 2560:6656],
                            g[:, AL_SMALL + 16:AL_SMALL + 32], g[:, 6656:8704]], axis=1)
    return jnp.stack([jnp.pad(orig[:, j * IN_SHARD:(j + 1) * IN_SHARD], ((0, 0), (0, IN_SHARD_PAD - IN_SHARD)))
                      for j in range(N_CHIPS)])


def layer_bwd(l, dx2, w, s, gbuf):
    t = dx2.shape[0]
    rt = min(256, t)
    ct = min(512, t)
    g = {}
    dact = matmul(f"ffn_down_dx_l{l}", dx2, w["w_ffn_down"], "nt")
    g["w_ffn_down"] = matmul(f"ffn_down_dw_l{l}", s["act"], dx2, "tn", stack=(l, gbuf.get("w_ffn_down")))
    nf = FFN_HIDDEN // 256
    dgate, dup = rowwise_bwd(f"swiglu_bwd_l{l}", f_swiglu, t, rt, nf,
                             [_row(s["gu"], 256, 0, True), _row(s["gu"], 256, nf, True)], [True, True],
                             [_row(dact, 256, 0, True)], row_dtypes={0: BF16, 1: BF16})
    dgu = jnp.concatenate([dgate, dup], axis=1)
    dh2 = matmul(f"ffn_in_dx_l{l}", dgu, w["w_ffn_in"], "nt")
    g["w_ffn_in"] = matmul(f"ffn_in_dw_l{l}", s["h2"], dgu, "tn", chip_major=True,
                           stack=(l, gbuf.get("w_ffn_in")))
    dx1, g["norm_ffn_w"] = rowwise_bwd(f"norm_ffn_bwd_l{l}", f_rmsnorm, t, rt, 1,
                                       [_row(s["x1"], D_MODEL), _par(w["norm_ffn_w"], D_MODEL)], [True, True],
                                       [_row(dh2, D_MODEL)], addends={0: _row(dx2, D_MODEL)})
    dmerged = matmul(f"out_proj_dx_l{l}", dx1, w["w_out"], "nt")
    g["w_out"] = matmul(f"out_proj_dw_l{l}", s["merged"], dx1, "tn", stack=(l, gbuf.get("w_out")))
    dgs, dp1, dgg, dp2 = rowwise_bwd(
        f"merge_bwd_l{l}", f_merge, t, rt, 2,
        [_row(s["proj"], 512, AL_GS // 512, True), _row(s["p1"], 512, 0, True),
         _row(s["proj"], 512, AL_GG // 512, True), _row(s["p2"], 512, 0, True)], [True] * 4,
        [_row(dmerged, 512, 0, True)], row_dtypes={0: BF16, 1: BF16, 2: BF16, 3: BF16})
    dy_ssm = matmul(f"proj_ssm_dx_l{l}", dp1, w["w_proj_ssm"], "nt")
    g["w_proj_ssm"] = matmul(f"proj_ssm_dw_l{l}", s["y_ssm"], dp1, "tn", stack=(l, gbuf.get("w_proj_ssm")))
    dy_gdn = matmul(f"proj_gdn_dx_l{l}", dp2, w["w_proj_gdn"], "nt")
    g["w_proj_gdn"] = matmul(f"proj_gdn_dw_l{l}", s["y_gdn"], dp2, "tn", stack=(l, gbuf.get("w_proj_gdn")))
    dy_scan, dz, g["ssm_norm_w"] = rowwise_bwd(
        f"ssd_post_bwd_l{l}", f_ssd_post, t, rt, 2,
        [_row(s["y_scan"], 512, 0, True), _row(s["proj"], 512, AL_Z // 512, True),
         _par(w["ssm_norm_w"], 512, 0, True)], [True] * 3, [_row(dy_ssm, 512, 0, True)], row_dtypes={1: BF16})
    dxbc_act, dsm, g["ssm_dt_bias"], g["ssm_a_log"], g["ssm_d"] = ssd_scan_bwd(
        f"ssd_scan_bwd_l{l}", s["xbc"], s["proj"], w["ssm_dt_bias"], w["ssm_a_log"], w["ssm_d"],
        s["ssd_sin"], dy_scan)
    dpre, g["ssm_conv_w"], g["ssm_conv_b"] = conv_bwd_pre(
        f"ssm_conv_bwd_l{l}", s["proj"], AL_XBC, SSM_CONV_DIM, w["ssm_conv_w"], w["ssm_conv_b"], dxbc_act, tile=ct)
    dxbc = conv_bwd_in(f"ssm_conv_din_l{l}", dpre, w["ssm_conv_w"], tile=ct)
    do_scan, dgz, g["gdn_norm_w"] = rowwise_bwd(
        f"gdn_post_bwd_l{l}", f_gdn_post, t, rt, GDN_HEADS,
        [_row(s["o_scan"], LANE, 0, True), _row(s["proj"], LANE, AL_GZ // LANE, True),
         _par(w["gdn_norm_w"], LANE)], [True] * 3, [_row(dy_gdn, LANE, 0, True)], row_dtypes={1: BF16})
    dqkv_act, dsm, g["gdn_a_log"], g["gdn_dt_bias"] = gdn_scan_bwd(
        f"gdn_scan_bwd_l{l}", s["qkv"], s["proj"], w["gdn_a_log"], w["gdn_dt_bias"], s["gdn_sin"],
        s["gdn_x"], do_scan, dsm)
    dpre, g["gdn_conv_w"], _ = conv_bwd_pre(
        f"gdn_conv_bwd_l{l}", s["proj"], AL_QKV, GDN_QKV_DIM, w["gdn_conv_w"], w["gdn_conv_b"], dqkv_act, tile=ct)
    dqkv = conv_bwd_in(f"gdn_conv_din_l{l}", dpre, w["gdn_conv_w"], tile=ct)
    dproj = jnp.concatenate([dz, dxbc, dqkv, dgz, dgs, dgg, dsm.astype(BF16),
                             jnp.zeros((t, AL_DIM - AL_SMALL - LANE), BF16)], axis=1)
    dh = matmul(f"in_proj_dx_l{l}", dproj, w["w_in"], "nt")
    g["w_in"] = matmul(f"in_proj_dw_l{l}", s["h"], dproj, "tn")
    dx0, g["norm_mix_w"] = rowwise_bwd(f"norm_mix_bwd_l{l}", f_rmsnorm, t, rt, 1,
                                       [_row(s["x"], D_MODEL), _par(w["norm_mix_w"], D_MODEL)], [True, True],
                                       [_row(dh, D_MODEL)], addends={0: _row(dx1, D_MODEL)})
    return dx0, g


def _align_w_in(w):
    pad = jnp.zeros((w.shape[0], AL_DIM - AL_SMALL - 32), w.dtype)
    return jnp.concatenate([w[:, 0:2560], w[:, 2576:6672], w[:, 6688:8736],
                            w[:, 2560:2576], w[:, 6672:6688], pad], axis=1)


def _pad_lane(v, at=0):
    return jnp.pad(v[None], ((0, 0), (at, LANE - at - v.shape[0])))


def local_step(x, target, full):
    ws = []
    for l in range(DEPTH):
        ws.append({
            "norm_mix_w": full["norm_mix_w"][l][None], "w_in": _align_w_in(full["w_in"][l]),
            "ssm_conv_w": full["ssm_conv_w"][l], "ssm_conv_b": full["ssm_conv_b"][l][None],
            "ssm_dt_bias": _pad_lane(full["ssm_dt_bias"][l]), "ssm_a_log": _pad_lane(full["ssm_a_log"][l]),
            "ssm_d": _pad_lane(full["ssm_d"][l]), "ssm_norm_w": full["ssm_norm_w"][l][None],
            "gdn_conv_w": full["gdn_conv_w"][l], "gdn_conv_b": jnp.zeros((1, GDN_QKV_DIM), F32),
            "gdn_a_log": _pad_lane(full["gdn_a_log"][l], SM_A),
            "gdn_dt_bias": _pad_lane(full["gdn_dt_bias"][l], SM_A),
            "gdn_norm_w": full["gdn_norm_w"][l][None],
            "w_proj_ssm": full["w_proj_ssm"][l], "w_proj_gdn": full["w_proj_gdn"][l], "w_out": full["w_out"][l],
            "norm_ffn_w": full["norm_ffn_w"][l][None], "w_ffn_in": full["w_ffn_in"][l],
            "w_ffn_down": full["w_ffn_down"][l],
        })
    saved = []
    h = x
    for l in range(DEPTH):
        h, s = layer_fwd(l, h, ws[l])
        saved.append(s)
    loss, dx, g_final = final_loss("final_loss", h, target, full["final_norm_w"][None], tile=min(256, x.shape[0]))
    per_layer = [None] * DEPTH
    gbuf = {}
    for l in reversed(range(DEPTH)):
        dx, per_layer[l] = layer_bwd(l, dx, ws[l], saved[l], gbuf)
        gbuf = {n: per_layer[l][n] for n, _ in BIG if n != "w_in"}
    gbuf["w_in"] = jnp.stack([_aligned_to_shards(per_layer[l]["w_in"]) for l in range(DEPTH)])
    grads = {"final_norm_w": g_final[0], **gbuf}
    for name in per_layer[0]:
        if name in gbuf:
            continue
        rows = []
        for l in range(DEPTH):
            gl = per_layer[l][name]
            if name in ("ssm_dt_bias", "ssm_a_log", "ssm_d"):
                gl = gl[0, :SSM_HEADS]
            elif name in ("gdn_a_log", "gdn_dt_bias"):
                gl = gl[0, SM_A:SM_A + GDN_HEADS]
            elif name in ("norm_mix_w", "ssm_conv_b", "ssm_norm_w", "gdn_norm_w", "norm_ffn_w"):
                gl = gl[0]
            rows.append(gl)
        grads[name] = jnp.stack(rows)
    return loss, dx, grads


MESH = pl.DeviceIdType.MESH
HBM = pl.BlockSpec(memory_space=pltpu.HBM)
N_DEV = 8


def _pos():
    return lax.axis_index("x"), lax.axis_index("y"), lax.axis_index("c")


def _rcopy(src, dst, send_sem, recv_sem, dev):
    return pltpu.make_async_remote_copy(src_ref=src, dst_ref=dst, send_sem=send_sem, recv_sem=recv_sem,
                                        device_id=dev, device_id_type=MESH)


RELATIONS = (2, 1, 3)


def _related_chip(x, y, mask):
    return (1 - x if mask & 2 else x, 1 - y if mask & 1 else y)


def weights_gather(name, bufs):
    n = len(bufs)

    def body(*refs):
        outs, send_sems, recv_sems = refs[n:2 * n], refs[2 * n], refs[2 * n + 1]
        x, y, c = _pos()
        sib = (x, y, 1 - c)
        sends = []
        for i, a in enumerate(outs):
            for k, m in enumerate(RELATIONS):
                px, py = _related_chip(x, y, m)
                cp = _rcopy(a.at[0, c], a.at[m, c], send_sems.at[6 * i + k], recv_sems.at[6 * i + k], (px, py, c))
                cp.start()
                sends.append(cp)
        for i, a in enumerate(outs):
            for k, m in enumerate(RELATIONS):
                px, py = _related_chip(x, y, m)
                _rcopy(a.at[0, c], a.at[m, c], send_sems.at[6 * i + k], recv_sems.at[6 * i + k],
                       (px, py, c)).wait_recv()
                fw = _rcopy(a.at[m, c], a.at[m, c], send_sems.at[6 * i + 3 + k], recv_sems.at[6 * i + 3 + k], sib)
                fw.start()
                sends.append(fw)
        for i, a in enumerate(outs):
            for k, m in enumerate(RELATIONS):
                _rcopy(a.at[m, 1 - c], a.at[m, 1 - c], send_sems.at[6 * i + 3 + k], recv_sems.at[6 * i + 3 + k],
                       sib).wait_recv()
        for cp in sends:
            cp.wait_send()

    return pl.pallas_call(
        body, name=name, out_shape=[jax.ShapeDtypeStruct(b.shape, b.dtype) for b in bufs],
        in_specs=[HBM] * n, out_specs=[HBM] * n,
        input_output_aliases={i: i for i in range(n)},
        scratch_shapes=[pltpu.SemaphoreType.DMA((6 * n,)), pltpu.SemaphoreType.DMA((6 * n,))],
    )(*bufs)


def pair_swap(name, gs):
    n = len(gs)

    def body(*refs):
        srcs, outs, send_sems, recv_sems = refs[:n], refs[n:2 * n], refs[2 * n], refs[2 * n + 1]
        x, y, c = _pos()
        cps = [_rcopy(s.at[1 - c], o, send_sems.at[i], recv_sems.at[i], (x, y, 1 - c))
               for i, (s, o) in enumerate(zip(srcs, outs))]
        for cp in cps:
            cp.start()
        for cp in cps:
            cp.wait()

    return pl.pallas_call(
        body, name=name, out_shape=[jax.ShapeDtypeStruct(g.shape[1:], g.dtype) for g in gs],
        in_specs=[HBM] * n, out_specs=[HBM] * n,
        scratch_shapes=[pltpu.SemaphoreType.DMA((n,)), pltpu.SemaphoreType.DMA((n,))],
    )(*gs)


def chip_scatter(name, ss):
    n = len(ss)

    def body(*refs):
        srcs, outs, send_sems, recv_sems = refs[:n], refs[n:2 * n], refs[2 * n], refs[2 * n + 1]
        x, y, c = _pos()
        sends = []
        for i, (s, o) in enumerate(zip(srcs, outs)):
            for k, m in enumerate(RELATIONS):
                px, py = _related_chip(x, y, m)
                cp = _rcopy(s.at[2 * px + py], o.at[k], send_sems.at[3 * i + k], recv_sems.at[3 * i + k],
                            (px, py, c))
                cp.start()
                sends.append(cp)
        for cp in sends:
            cp.wait()

    return pl.pallas_call(
        body, name=name, out_shape=[jax.ShapeDtypeStruct((3,) + s.shape[1:], s.dtype) for s in ss],
        in_specs=[HBM] * n, out_specs=[HBM] * n,
        scratch_shapes=[pltpu.SemaphoreType.DMA((3 * n,)), pltpu.SemaphoreType.DMA((3 * n,))],
    )(*ss)


def pair_share(name, bufs):
    n = len(bufs)

    def body(*refs):
        outs, send_sems, recv_sems = refs[n:2 * n], refs[2 * n], refs[2 * n + 1]
        x, y, c = _pos()
        sends = []
        for i, o in enumerate(outs):
            cp = _rcopy(o.at[c], o.at[c], send_sems.at[i], recv_sems.at[i], (x, y, 1 - c))
            cp.start()
            sends.append(cp)
        for i, o in enumerate(outs):
            _rcopy(o.at[1 - c], o.at[1 - c], send_sems.at[i], recv_sems.at[i], (x, y, 1 - c)).wait_recv()
        for cp in sends:
            cp.wait_send()

    return pl.pallas_call(
        body, name=name, out_shape=[jax.ShapeDtypeStruct(b.shape, b.dtype) for b in bufs],
        in_specs=[HBM] * n, out_specs=[HBM] * n,
        input_output_aliases={i: i for i in range(n)},
        scratch_shapes=[pltpu.SemaphoreType.DMA((n,)), pltpu.SemaphoreType.DMA((n,))],
    )(*bufs)


def all_allgather(name, buf):
    r, cd = buf.shape

    def body(src, out, send_sems, recv_sems, lsem):
        x, y, c = _pos()
        me = 4 * x + 2 * y + c
        local = pltpu.make_async_copy(src, out.at[me], lsem)
        local.start()

        def peer(mask):
            px = 1 - x if mask & 4 else x
            py = 1 - y if mask & 2 else y
            pc = 1 - c if mask & 1 else c
            return px, py, pc

        sends = []
        for mask in range(1, N_DEV):
            cp = _rcopy(src, out.at[me], send_sems.at[mask - 1], recv_sems.at[mask - 1], peer(mask))
            cp.start()
            sends.append(cp)
        for mask in range(1, N_DEV):
            px, py, pc = peer(mask)
            _rcopy(src, out.at[4 * px + 2 * py + pc], send_sems.at[mask - 1], recv_sems.at[mask - 1],
                   (px, py, pc)).wait_recv()
        for cp in sends:
            cp.wait_send()
        local.wait()

    return pl.pallas_call(
        body, name=name, out_shape=jax.ShapeDtypeStruct((N_DEV, r, cd), buf.dtype),
        in_specs=[HBM], out_specs=HBM,
        scratch_shapes=[pltpu.SemaphoreType.DMA((N_DEV - 1,)), pltpu.SemaphoreType.DMA((N_DEV - 1,)),
                        pltpu.SemaphoreType.DMA(())],
    )(buf)


ELEMENTWISE_BLOCK_BYTES = 2 << 20


def _row_block(rows, cols):
    for cand in (1024, 512, 256, 128, 64, 32, 16):
        if rows % cand == 0 and cand * cols * 4 <= ELEMENTWISE_BLOCK_BYTES:
            return cand
    return rows


def chip_sum(name, s, r, me, c):
    _, a, b = s.shape
    tr = _row_block(a, b)

    def body(idx_ref, s_ref, r_ref, o_ref):
        del idx_ref
        acc = s_ref[...].astype(F32)
        for k in range(3):
            acc = acc + r_ref[k].astype(F32)
        o_ref[...] = acc

    return pl.pallas_call(
        body, name=name, out_shape=jax.ShapeDtypeStruct((2, a, b), F32),
        grid_spec=pltpu.PrefetchScalarGridSpec(
            num_scalar_prefetch=1, grid=(a // tr,),
            in_specs=[pl.BlockSpec((None, tr, b), lambda i, idx: (idx[0], i, 0)),
                      pl.BlockSpec((3, tr, b), lambda i, idx: (0, i, 0))],
            out_specs=pl.BlockSpec((None, tr, b), lambda i, idx: (idx[1], i, 0))),
        compiler_params=_cparams(("arbitrary",)),
    )(jnp.stack([me, c]).astype(jnp.int32), s, r)


def pair_add(name, p, recv, c):
    _, nj, rh, cd = p.shape
    tr = _row_block(rh, cd)

    def body(c_ref, p_ref, r_ref, o_ref):
        del c_ref
        o_ref[...] = (p_ref[0] + r_ref[...]).astype(o_ref.dtype)

    return pl.pallas_call(
        body, name=name, out_shape=jax.ShapeDtypeStruct((nj, rh, cd), BF16),
        grid_spec=pltpu.PrefetchScalarGridSpec(
            num_scalar_prefetch=1, grid=(nj, rh // tr),
            in_specs=[pl.BlockSpec((1, 1, tr, cd), lambda j, i, c_ref: (c_ref[0], j, i, 0)),
                      pl.BlockSpec((1, tr, cd), lambda j, i, c_ref: (j, i, 0))],
            out_specs=pl.BlockSpec((1, tr, cd), lambda j, i, c_ref: (j, i, 0))),
        compiler_params=_cparams(("arbitrary", "arbitrary")),
    )(jnp.reshape(c, (1,)).astype(jnp.int32), p, recv)


def slab_sum(name, a):
    n, r, cd = a.shape
    tr = _pick(r, (256, 128, 64, 32, 16, 8))

    def body(a_ref, o_ref):
        acc = a_ref[0].astype(F32)
        for j in range(1, n):
            acc = acc + a_ref[j].astype(F32)
        o_ref[...] = acc

    return pl.pallas_call(
        body, name=name, out_shape=jax.ShapeDtypeStruct((r, cd), F32),
        grid=(r // tr,),
        in_specs=[pl.BlockSpec((n, tr, cd), lambda i: (0, i, 0))],
        out_specs=pl.BlockSpec((tr, cd), lambda i: (i, 0)),
        compiler_params=_cparams(("arbitrary",)),
    )(a)


ADAM_C1 = 1.0 - ADAM_B1 ** ADAM_STEP
ADAM_C2 = 1.0 - ADAM_B2 ** ADAM_STEP


def adamw(name, w, g, m, v):
    r, cd = w.shape
    tr = r
    for cand in (512, 256, 128, 64, 32, 16, 8):
        if r % cand == 0 and cand * cd * 4 <= (1 << 20):
            tr = cand
            break

    def body(w_ref, g_ref, m_ref, v_ref, d_ref, nm_ref, nv_ref):
        gv = g_ref[...]
        nm = ADAM_B1 * m_ref[...] + (1.0 - ADAM_B1) * gv
        nv = ADAM_B2 * v_ref[...] + (1.0 - ADAM_B2) * (gv * gv)
        m_hat = nm / ADAM_C1
        v_hat = nv / ADAM_C2
        d_ref[...] = -ADAM_LR * (m_hat / (jnp.sqrt(v_hat) + ADAM_EPS) + ADAM_WD * w_ref[...])
        nm_ref[...] = nm
        nv_ref[...] = nv

    spec = pl.BlockSpec((tr, cd), lambda i: (i, 0))
    sd = jax.ShapeDtypeStruct((r, cd), F32)
    return pl.pallas_call(
        body, name=name, out_shape=[sd, sd, sd], grid=(r // tr,),
        in_specs=[spec] * 4, out_specs=[spec] * 3,
        compiler_params=_cparams(("arbitrary",)),
    )(w, g, m, v)


WEIGHTS = ("norm_mix_w", "w_in", "ssm_conv_w", "ssm_conv_b", "ssm_dt_bias", "ssm_a_log", "ssm_d", "ssm_norm_w",
           "gdn_conv_w", "gdn_a_log", "gdn_dt_bias", "gdn_norm_w", "w_proj_ssm", "w_proj_gdn", "w_out",
           "norm_ffn_w", "w_ffn_in", "w_ffn_down", "final_norm_w")
BIG = (("w_in", 2), ("w_proj_ssm", 1), ("w_proj_gdn", 1), ("w_out", 1), ("w_ffn_in", 2), ("w_ffn_down", 1))
CONVW = (("ssm_conv_w", 2), ("gdn_conv_w", 2))
SHARDED = BIG + CONVW
SMALL = tuple(n for n in WEIGHTS if n not in dict(SHARDED))


def _unpack(buf, shapes, lead=()):
    flat = buf.reshape(lead + (-1,))
    out, o = [], 0
    for shp in shapes:
        n = math.prod(shp)
        out.append(flat[..., o:o + n].reshape(lead + tuple(shp)))
        o += n
    return out


def _pack_rows(arrs, lead=()):
    nl = len(lead)
    flat = jnp.concatenate([a.reshape(lead + (-1,)) for a in arrs], axis=nl)
    n = flat.shape[nl]
    rows = -(-n // (8 * LANE)) * 8
    flat = jnp.pad(flat, [(0, 0)] * nl + [(0, rows * LANE - n)])
    return flat.reshape(lead + (rows, LANE))


def _slot_buffer(shard):
    return jnp.pad(shard[None], [(0, N_CHIPS - 1)] + [(0, 0)] * shard.ndim)


def kernel(x, norm_mix_w, w_in, ssm_conv_w, ssm_conv_b, ssm_dt_bias, ssm_a_log, ssm_d, ssm_norm_w, gdn_conv_w, gdn_a_log, gdn_dt_bias, gdn_norm_w, w_proj_ssm, w_proj_gdn, w_out, norm_ffn_w, w_ffn_in, w_ffn_down, final_norm_w, loss_target, m_norm_mix_w, m_w_in, m_ssm_conv_w, m_ssm_conv_b, m_ssm_dt_bias, m_ssm_a_log, m_ssm_d, m_ssm_norm_w, m_gdn_conv_w, m_gdn_a_log, m_gdn_dt_bias, m_gdn_norm_w, m_w_proj_ssm, m_w_proj_gdn, m_w_out, m_norm_ffn_w, m_w_ffn_in, m_w_ffn_down, m_final_norm_w, v_norm_mix_w, v_w_in, v_ssm_conv_w, v_ssm_conv_b, v_ssm_dt_bias, v_ssm_a_log, v_ssm_d, v_ssm_norm_w, v_gdn_conv_w, v_gdn_a_log, v_gdn_dt_bias, v_gdn_norm_w, v_w_proj_ssm, v_w_proj_gdn, v_w_out, v_norm_ffn_w, v_w_ffn_in, v_w_ffn_down, v_final_norm_w):
    wl = (norm_mix_w, w_in, ssm_conv_w, ssm_conv_b, ssm_dt_bias, ssm_a_log, ssm_d, ssm_norm_w, gdn_conv_w,
          gdn_a_log, gdn_dt_bias, gdn_norm_w, w_proj_ssm, w_proj_gdn, w_out, norm_ffn_w, w_ffn_in, w_ffn_down,
          final_norm_w)
    ml = (m_norm_mix_w, m_w_in, m_ssm_conv_w, m_ssm_conv_b, m_ssm_dt_bias, m_ssm_a_log, m_ssm_d, m_ssm_norm_w,
          m_gdn_conv_w, m_gdn_a_log, m_gdn_dt_bias, m_gdn_norm_w, m_w_proj_ssm, m_w_proj_gdn, m_w_out,
          m_norm_ffn_w, m_w_ffn_in, m_w_ffn_down, m_final_norm_w)
    vl = (v_norm_mix_w, v_w_in, v_ssm_conv_w, v_ssm_conv_b, v_ssm_dt_bias, v_ssm_a_log, v_ssm_d, v_ssm_norm_w,
          v_gdn_conv_w, v_gdn_a_log, v_gdn_dt_bias, v_gdn_norm_w, v_w_proj_ssm, v_w_proj_gdn, v_w_out,
          v_norm_ffn_w, v_w_ffn_in, v_w_ffn_down, v_final_norm_w)
    w = dict(zip(WEIGHTS, wl))
    m = dict(zip(WEIGHTS, ml))
    v = dict(zip(WEIGHTS, vl))
    x_pos, y_pos, c = _pos()
    me = 2 * x_pos + y_pos
    big = [n for n, _ in BIG]

    shards = [w[n].astype(BF16) for n in big]
    shards[0] = jnp.pad(shards[0], ((0, 0), (0, 0), (0, IN_SHARD_PAD - IN_SHARD)))
    conv_shapes = [w[n].shape[1:] for n, _ in CONVW]
    conv_pack = _pack_rows([w[n] for n, _ in CONVW], lead=(DEPTH,))
    gathered = weights_gather("gather_w", [_slot_buffer(s) for s in shards + [conv_pack]])
    by_chip = [[lax.dynamic_index_in_dim(g_, jnp.bitwise_xor(me, j), 0, keepdims=False) for j in range(N_CHIPS)]
               for g_ in gathered]
    full = {n: w[n] for n in SMALL}
    for i, (n, axis) in enumerate(BIG):
        full[n] = jnp.concatenate(by_chip[i], axis=axis)
    full["w_in"] = jnp.concatenate([p[:, :, :IN_SHARD] for p in by_chip[0]], axis=2)
    conv_parts = [_unpack(by_chip[-1][j], conv_shapes, lead=(DEPTH,)) for j in range(N_CHIPS)]
    for i, (n, axis) in enumerate(CONVW):
        full[n] = jnp.concatenate([conv_parts[j][i] for j in range(N_CHIPS)], axis=axis)

    loss_part, grad_x, grads = local_step(x[0], loss_target[0], full)

    by4 = [grads[n].reshape((DEPTH, N_CHIPS, -1) + grads[n].shape[-1:]) for n in big]
    from_pair = pair_swap("grad_pair_swap", by4)
    chip_part = [pair_add(f"grad_pair_add_{n}", g_, r_, c) for n, g_, r_ in zip(big, by4, from_pair)]
    from_chips = chip_scatter("grad_chip_scatter", chip_part)
    halves = [chip_sum(f"grad_chip_sum_{n}", s_, r_, me, c) for n, s_, r_ in zip(big, chip_part, from_chips)]
    reduced = pair_share("grad_pair_share", halves)
    g_sharded = dict(zip(big, reduced))
    g_sharded["w_in"] = g_sharded["w_in"][:, :, :IN_SHARD]

    small_names = list(SMALL) + [n for n, _ in CONVW]
    small_all = all_allgather("gather_small", _pack_rows([grads[n] for n in small_names] + [loss_part[0, :1]]))
    small_sum = slab_sum("small_sum", small_all)
    small_vals = _unpack(small_sum, [grads[n].shape for n in small_names] + [(1,)])
    g_small = dict(zip(small_names, small_vals[:-1]))
    loss = small_vals[-1].reshape(())
    for n, axis in CONVW:
        size = w[n].shape[axis]
        g_sharded[n] = lax.dynamic_slice_in_dim(g_small.pop(n), me * size, size, axis=axis)

    out_g, out_d, out_m, out_v = {}, {}, {}, {}
    for n, _ in SHARDED:
        shp = w[n].shape
        two = lambda a: a.reshape(-1, shp[-1])
        d_, m_, v_ = adamw(f"adamw_{n}", two(w[n]), two(g_sharded[n]), two(m[n]), two(v[n]))
        out_g[n], out_d[n], out_m[n], out_v[n] = g_sharded[n], d_.reshape(shp), m_.reshape(shp), v_.reshape(shp)
    d_, m_, v_ = adamw("adamw_small", *[_pack_rows([d[n] for n in SMALL]) for d in (w, g_small, m, v)])
    small_shapes = [w[n].shape for n in SMALL]
    for n, dd, mm, vv in zip(SMALL, _unpack(d_, small_shapes), _unpack(m_, small_shapes), _unpack(v_, small_shapes)):
        out_g[n], out_d[n], out_m[n], out_v[n] = g_small[n], dd, mm, vv

    return (loss, grad_x[None], *[out_g[n] for n in WEIGHTS], *[out_d[n] for n in WEIGHTS],
            *[out_m[n] for n in WEIGHTS], *[out_v[n] for n in WEIGHTS])
```

```python
import math

import jax
import jax.numpy as jnp
from jax import lax
from jax.experimental import pallas as pl
from jax.experimental.pallas import tpu as pltpu

F32 = jnp.float32
BF16 = jnp.bfloat16

D_MODEL = 1024
DEPTH = 2
SSM_HEADS = 16
SSM_HEAD_DIM = 64
SSM_D_INNER = 1024
SSM_STATE = 128
SSM_CONV_DIM = 1536
GDN_HEADS = 8
GDN_HEAD = 128
GDN_QKV_DIM = 3072
CONV_K = 4
CHUNK = 64
FFN_HIDDEN = 2816
EPS = 1e-6
IN_DIM = 8736

ADAM_LR = 0.001
ADAM_B1 = 0.9
ADAM_B2 = 0.999
ADAM_EPS = 1e-08
ADAM_WD = 0.01
ADAM_STEP = 10

LANE = 128
NEG_BIG = -1e30
VMEM_LIMIT = 56 * 1024 * 1024

AL_Z, AL_XBC, AL_QKV, AL_GZ, AL_GS, AL_GG, AL_SMALL, AL_DIM = 0, 1024, 2560, 5632, 6656, 7680, 8704, 9216
SM_DT, SM_A, SM_B = 0, 16, 24

HI = lax.Precision.HIGHEST
NN = (((1,), (0,)), ((), ()))
NT = (((1,), (1,)), ((), ()))
TN = (((0,), (0,)), ((), ()))


def _cparams(sem):
    return pltpu.CompilerParams(dimension_semantics=sem, vmem_limit_bytes=VMEM_LIMIT)


def _pick(n, prefs):
    for p in prefs:
        if n % p == 0:
            return p
    return n


MATMUL_VMEM_BUDGET = 40 << 20
MXU_WIDTH = 256
HBM_BYTES_PER_S = 3.3e12
MXU_FLOPS_PER_S = 9.0e14
GRID_STEP_S = 0.35e-6
N_CHIPS = 4


def _matmul_tiles(m, n, n_dom, k, a_item, b_item, o_item):
    best = None
    def cands(dim, cap):
        return [c for c in range(LANE, min(dim, cap) + 1, LANE) if dim % c == 0] or [dim]

    tms, tns, tks = cands(m, 2048), cands(n_dom, 2304), cands(k, 1 << 30)
    for tm in tms:
        for tn in tns:
            for tk in tks:
                nk = k // tk
                vmem = (2 * (tm * tk * a_item + tk * tn * b_item + tm * tn * o_item) + tm * tn * 4 * (2 if nk > 1 else 1)
                        + (tm * tk * 2 if a_item > 2 else 0) + (tk * tn * 2 if b_item > 2 else 0))
                if vmem > MATMUL_VMEM_BUDGET:
                    continue
                traffic = m * k * a_item * (1 if nk == 1 else n // tn) + k * n * b_item * (m // tm)
                mxu_fill = tn / (-(-tn // MXU_WIDTH) * MXU_WIDTH)
                cost = (max(traffic / HBM_BYTES_PER_S, 2.0 * m * n * k / (MXU_FLOPS_PER_S * mxu_fill))
                        + (m // tm) * (n // tn) * nk * GRID_STEP_S)
                if best is None or cost < best[0]:
                    best = (cost, (tm, tn, tk))
    return best[1]


def _dot(a, b, dims=NN):
    return lax.dot_general(a.astype(BF16), b.astype(BF16), dims, preferred_element_type=F32)


def _dot3(a, b, dims=NN):
    a_hi, b_hi = a.astype(BF16), b.astype(BF16)
    a_lo = (a - a_hi.astype(F32)).astype(BF16)
    b_lo = (b - b_hi.astype(F32)).astype(BF16)

    def dg(u, v):
        return lax.dot_general(u, v, dims, preferred_element_type=F32)

    return dg(a_hi, b_hi) + (dg(a_hi, b_lo) + dg(a_lo, b_hi))


def _dot_hi(a, b, dims=NN):
    return lax.dot_general(a, b, dims, precision=HI, preferred_element_type=F32)


def _sigmoid(x):
    return jax.nn.sigmoid(x)


def _silu(x):
    return x * _sigmoid(x)


def _softplus(x):
    return jnp.maximum(x, 0.0) + jnp.log1p(jnp.exp(-jnp.abs(x)))


def matmul(name, a, b, mode, out_dtype=F32, chip_major=False, stack=None):
    if mode == "nn":
        (m, k), (k2, n) = a.shape, b.shape
    elif mode == "nt":
        (m, k), (n, k2) = a.shape, b.shape
    else:
        (k, m), (k2, n) = a.shape, b.shape
    assert k == k2, (a.shape, b.shape, mode)
    shard = n // N_CHIPS if chip_major else n
    tm, tn, tk = _matmul_tiles(m, n, shard, k, a.dtype.itemsize, b.dtype.itemsize, jnp.dtype(out_dtype).itemsize)
    if chip_major:
        per = shard // tn
        base_shape, base_blk = (N_CHIPS, m, shard), (None, tm, tn)
        base_idx = lambda i, j: (j // per, i, j % per)
    else:
        base_shape, base_blk = (m, n), (tm, tn)
        base_idx = lambda i, j: (i, j)
    nk = k // tk
    dims = {"nn": NN, "nt": NT, "tn": TN}[mode]

    def body_acc(a_ref, b_ref, o_ref, acc_ref):
        kk = pl.program_id(2)

        @pl.when(kk == 0)
        def _():
            acc_ref[...] = jnp.zeros_like(acc_ref)

        acc_ref[...] += _dot(a_ref[...], b_ref[...], dims)

        @pl.when(kk == nk - 1)
        def _():
            o_ref[...] = acc_ref[...].astype(o_ref.dtype)

    def body_one(a_ref, b_ref, o_ref):
        o_ref[...] = _dot(a_ref[...], b_ref[...], dims).astype(o_ref.dtype)

    compute = body_one if nk == 1 else body_acc
    if mode == "tn":
        a_spec = pl.BlockSpec((tk, tm), lambda i, j, kk: (kk, i))
    else:
        a_spec = pl.BlockSpec((tm, tk), lambda i, j, kk: (i, kk))
    if mode == "nt":
        b_spec = pl.BlockSpec((tn, tk), lambda i, j, kk: (j, kk))
    else:
        b_spec = pl.BlockSpec((tk, tn), lambda i, j, kk: (kk, j))
    in_specs, operands, aliases, body = [a_spec, b_spec], [a, b], {}, compute
    if stack is None:
        out_shape, out_blk, out_idx = base_shape, base_blk, (lambda i, j, kk: base_idx(i, j))
    else:
        layer, buf = stack
        out_shape, out_blk = (DEPTH,) + base_shape, (None,) + base_blk
        out_idx = lambda i, j, kk: (layer,) + base_idx(i, j)
        if buf is not None:
            assert buf.shape == out_shape and buf.dtype == out_dtype
            in_specs.append(pl.BlockSpec(memory_space=pl.ANY))
            operands.append(buf)
            aliases = {2: 0}

            def body(a_ref, b_ref, buf_ref, *rest):
                del buf_ref
                compute(a_ref, b_ref, *rest)

    return pl.pallas_call(
        body, name=name,
        out_shape=jax.ShapeDtypeStruct(out_shape, out_dtype),
        grid=(m // tm, n // tn, nk),
        in_specs=in_specs,
        out_specs=pl.BlockSpec(out_blk, out_idx),
        scratch_shapes=[] if nk == 1 else [pltpu.VMEM((tm, tn), F32)],
        input_output_aliases=aliases,
        compiler_params=_cparams(("parallel", "parallel", "arbitrary")),
    )(*operands)


def _row_map(c0, moves):
    return (lambda j, i: (i, c0 + j)) if moves else (lambda j, i: (i, c0))


def _par_map(c0, moves):
    return (lambda j, i: (0, c0 + j)) if moves else (lambda j, i: (0, c0))


def _in_spec(op, tile):
    _, kind, w, c0, moves = op
    if kind == "row":
        return pl.BlockSpec((tile, w), _row_map(c0, moves))
    return pl.BlockSpec((1, w), _par_map(c0, moves))


ROW_BLOCK_ELEMS = 1 << 18


def _row_tile(t, tile, ops):
    width = max(op[2] for op in ops if op[1] == "row")
    return min(t, max(tile, ROW_BLOCK_ELEMS // width))


def rowwise_fwd(name, fn, t, tile, ncol, ins, outs):
    n_in = len(ins)
    tile = _row_tile(t, tile, ins)

    def body(*refs):
        vals = [r[...].astype(F32) for r in refs[:n_in]]
        res = fn(*vals)
        if not isinstance(res, (tuple, list)):
            res = (res,)
        for r, v in zip(refs[n_in:], res):
            r[...] = v.astype(r.dtype)

    res = pl.pallas_call(
        body, name=name,
        out_shape=[jax.ShapeDtypeStruct((t, w * ncol), dt) for w, dt in outs],
        grid=(ncol, t // tile),
        in_specs=[_in_spec(op, tile) for op in ins],
        out_specs=[pl.BlockSpec((tile, w), _row_map(0, True)) for w, _ in outs],
        compiler_params=_cparams(("arbitrary", "arbitrary")),
    )(*[op[0] for op in ins])
    return res


def rowwise_bwd(name, fn, t, tile, ncol, ins, need, cts, addends=None, row_dtypes=None):
    n_in, n_ct = len(ins), len(cts)
    tile = _row_tile(t, tile, ins)
    addends = addends or {}
    row_dtypes = row_dtypes or {}
    didx = [i for i in range(n_in) if need[i]]
    add_ops = [addends[i] for i in didx if i in addends]
    n_add = len(add_ops)

    def body(*refs):
        in_refs = refs[:n_in]
        ct_refs = refs[n_in:n_in + n_ct]
        add_refs = refs[n_in + n_ct:n_in + n_ct + n_add]
        out_refs = refs[n_in + n_ct + n_add:]
        vals = [r[...].astype(F32) for r in in_refs]

        def g(*dv):
            full = list(vals)
            for i, v in zip(didx, dv):
                full[i] = v
            res = fn(*full)
            return tuple(res) if isinstance(res, (tuple, list)) else (res,)

        _, vjp = jax.vjp(g, *[vals[i] for i in didx])
        grads = vjp(tuple(c[...].astype(F32) for c in ct_refs))
        j, i = pl.program_id(0), pl.program_id(1)
        a = 0
        for o_ref, gv, idx in zip(out_refs, grads, didx):
            _, kind, _, _, moves = ins[idx]
            if kind == "row":
                if idx in addends:
                    gv = gv + add_refs[a][...].astype(F32)
                    a += 1
                o_ref[...] = gv.astype(o_ref.dtype)
            else:
                first = (i == 0) if moves else jnp.logical_and(i == 0, j == 0)

                @pl.when(first)
                def _(o_ref=o_ref, gv=gv):
                    o_ref[...] = gv

                @pl.when(jnp.logical_not(first))
                def _(o_ref=o_ref, gv=gv):
                    o_ref[...] += gv

    out_shape, out_specs = [], []
    for idx in didx:
        _, kind, w, _, moves = ins[idx]
        cols = w * (ncol if moves else 1)
        if kind == "row":
            out_shape.append(jax.ShapeDtypeStruct((t, cols), row_dtypes.get(idx, F32)))
            out_specs.append(pl.BlockSpec((tile, w), _row_map(0, moves)))
        else:
            out_shape.append(jax.ShapeDtypeStruct((1, cols), F32))
            out_specs.append(pl.BlockSpec((1, w), _par_map(0, moves)))
    ops = list(ins) + list(cts) + add_ops
    res = pl.pallas_call(
        body, name=name,
        out_shape=out_shape,
        grid=(ncol, t // tile),
        in_specs=[_in_spec(op, tile) for op in ops],
        out_specs=out_specs,
        compiler_params=_cparams(("arbitrary", "arbitrary")),
    )(*[op[0] for op in ops])
    return res


def f_rmsnorm(x, w):
    return x * lax.rsqrt(jnp.mean(x * x, axis=-1, keepdims=True) + EPS) * w


def f_ssd_post(y, z, w):
    y = y * _silu(z)
    return y * lax.rsqrt(jnp.mean(y * y, axis=-1, keepdims=True) + EPS) * w


def f_gdn_post(o, z, w):
    o = o * lax.rsqrt(jnp.mean(o * o, axis=-1, keepdims=True) + EPS) * w
    return o * _silu(z)


def f_merge(gs, p1, gg, p2):
    return _sigmoid(gs) * p1 + _sigmoid(gg) * p2


def f_swiglu(g, u):
    return _silu(g) * u


def final_loss(name, x, tgt, w, tile=256):
    t, d = x.shape

    def body(x_ref, t_ref, w_ref, loss_ref, dx_ref, dw_ref):
        i = pl.program_id(0)
        xv, tv, wv = x_ref[...], t_ref[...], w_ref[...]

        def g(xx, ww):
            err = f_rmsnorm(xx, ww) - tv
            return 0.5 * jnp.sum(jnp.mean(err * err, axis=-1, keepdims=True), axis=0, keepdims=True)

        val, vjp = jax.vjp(g, xv, wv)
        dx, dw = vjp(jnp.ones((1, 1), F32))
        dx_ref[...] = dx
        lv = jnp.broadcast_to(val, (1, LANE))

        @pl.when(i == 0)
        def _():
            loss_ref[...] = lv
            dw_ref[...] = dw

        @pl.when(i != 0)
        def _():
            loss_ref[...] += lv
            dw_ref[...] += dw

    return pl.pallas_call(
        body, name=name,
        out_shape=[jax.ShapeDtypeStruct((1, LANE), F32), jax.ShapeDtypeStruct((t, d), F32),
                   jax.ShapeDtypeStruct((1, d), F32)],
        grid=(t // tile,),
        in_specs=[pl.BlockSpec((tile, d), lambda i: (i, 0)), pl.BlockSpec((tile, d), lambda i: (i, 0)),
                  pl.BlockSpec((1, d), lambda i: (0, 0))],
        out_specs=[pl.BlockSpec((1, LANE), lambda i: (0, 0)), pl.BlockSpec((tile, d), lambda i: (i, 0)),
                   pl.BlockSpec((1, d), lambda i: (0, 0))],
        compiler_params=_cparams(("arbitrary",)),
    )(x, tgt, w)


CONV_W = 512
HALO = 8
STRIPS = 4


def _rows_back(before, cur, d):
    rows = lax.broadcasted_iota(jnp.int32, cur.shape, 0)
    return jnp.where(rows < d, pltpu.roll(before, d, 0), pltpu.roll(cur, d, 0))


def _rows_ahead(cur, after, d):
    rows = lax.broadcasted_iota(jnp.int32, cur.shape, 0)
    return jnp.where(rows < HALO - d, pltpu.roll(cur, HALO - d, 0), pltpu.roll(after, HALO - d, 0))


def _conv_taps(taps, bias, before, cur):
    shifted = [_rows_back(before, cur, CONV_K - 1 - k) for k in range(CONV_K - 1)] + [cur]
    pre = bias + taps[CONV_K - 1] * cur
    for k in range(CONV_K - 1):
        pre = pre + taps[k] * shifted[k]
    return pre, shifted


def conv_fwd(name, src, c0, width, w, b, tile=512):
    t = src.shape[0]
    ncol, nrow = width // CONV_W, t // tile
    cb0 = c0 // CONV_W
    hb = tile // HALO

    def body(prev_ref, cur_ref, w_ref, b_ref, o_ref):
        i = pl.program_id(1)
        taps = [w_ref[k:k + 1, :] for k in range(CONV_K)]
        bias = b_ref[...]

        def strips(g, before):
            for u in range(STRIPS):
                r0 = pl.multiple_of((g * STRIPS + u) * HALO, HALO)
                cur = cur_ref[pl.ds(r0, HALO), :]
                pre, _ = _conv_taps(taps, bias, before, cur)
                o_ref[pl.ds(r0, HALO), :] = _silu(pre)
                before = cur
            return before

        lax.fori_loop(0, tile // (HALO * STRIPS), strips, jnp.where(i == 0, 0.0, prev_ref[...]))

    return pl.pallas_call(
        body, name=name,
        out_shape=jax.ShapeDtypeStruct((t, width), F32),
        grid=(ncol, nrow),
        in_specs=[pl.BlockSpec((HALO, CONV_W), lambda j, i: (jnp.maximum(i * hb - 1, 0), cb0 + j)),
                  pl.BlockSpec((tile, CONV_W), lambda j, i: (i, cb0 + j)),
                  pl.BlockSpec((CONV_K, CONV_W), lambda j, i: (0, j)),
                  pl.BlockSpec((1, CONV_W), lambda j, i: (0, j))],
        out_specs=pl.BlockSpec((tile, CONV_W), lambda j, i: (i, j)),
        compiler_params=_cparams(("arbitrary", "arbitrary")),
    )(src, src, w, b)


def conv_bwd(name, src, c0, width, w, b, dy, tile=512):
    t = src.shape[0]
    ncol, nrow = width // CONV_W, t // tile
    cb0 = c0 // CONV_W
    hb = tile // HALO
    last_hb = t // HALO - 1
    nstrip = tile // HALO

    def body(sprev_ref, scur_ref, snext_ref, w_ref, b_ref, dycur_ref, dynext_ref,
             du_ref, dw_ref, db_ref, dpre_ref):
        i = pl.program_id(1)
        taps = [w_ref[k:k + 1, :] for k in range(CONV_K)]
        bias = b_ref[...]

        def dpre_of(before, cur, dy_strip):
            pre, shifted = _conv_taps(taps, bias, before, cur)
            s = _sigmoid(pre)
            return dy_strip * (s * (1.0 + pre * (1.0 - s))), shifted

        def strips1(g, carry):
            before, dws, dbs = carry
            for u in range(STRIPS):
                r0 = pl.multiple_of((g * STRIPS + u) * HALO, HALO)
                cur = scur_ref[pl.ds(r0, HALO), :]
                dpre, shifted = dpre_of(before, cur, dycur_ref[pl.ds(r0, HALO), :])
                dpre_ref[pl.ds(r0, HALO), :] = dpre
                before, dws, dbs = cur, tuple(a + dpre * v for a, v in zip(dws, shifted)), dbs + dpre
            return before, dws, dbs

        zero = jnp.zeros((HALO, CONV_W), F32)
        before, dws, dbs = lax.fori_loop(0, nstrip // STRIPS, strips1,
                                         (jnp.where(i == 0, 0.0, sprev_ref[...]), (zero,) * CONV_K, zero))
        dpre_next, _ = dpre_of(before, snext_ref[...], dynext_ref[...])
        dpre_ref[pl.ds(tile, HALO), :] = jnp.where(i == nrow - 1, 0.0, dpre_next)

        def strips2(g, _):
            parts = []
            for u in range(STRIPS):
                r0 = pl.multiple_of((g * STRIPS + u) * HALO, HALO)
                cur = dpre_ref[pl.ds(r0, HALO), :]
                after = dpre_ref[pl.ds(r0 + HALO, HALO), :]
                acc = taps[CONV_K - 1] * cur
                for d in range(1, CONV_K):
                    acc = acc + taps[CONV_K - 1 - d] * _rows_ahead(cur, after, d)
                parts.append(acc)
            r0 = pl.multiple_of(g * STRIPS * HALO, STRIPS * HALO)
            du_ref[pl.ds(r0, STRIPS * HALO), :] = jnp.concatenate(parts, axis=0).astype(du_ref.dtype)
            return 0

        lax.fori_loop(0, nstrip // STRIPS, strips2, 0)
        dw_tile = jnp.concatenate([jnp.sum(a, axis=0, keepdims=True) for a in dws], axis=0)
        db_tile = jnp.sum(dbs, axis=0, keepdims=True)
        _acc(dw_ref, dw_tile, i == 0)
        _acc(db_ref, db_tile, i == 0)

    return pl.pallas_call(
        body, name=name,
        out_shape=[jax.ShapeDtypeStruct((t, width), BF16), jax.ShapeDtypeStruct((CONV_K, width), F32),
                   jax.ShapeDtypeStruct((1, width), F32)],
        grid=(ncol, nrow),
        in_specs=[pl.BlockSpec((HALO, CONV_W), lambda j, i: (jnp.maximum(i * hb - 1, 0), cb0 + j)),
                  pl.BlockSpec((tile, CONV_W), lambda j, i: (i, cb0 + j)),
                  pl.BlockSpec((HALO, CONV_W), lambda j, i: (jnp.minimum((i + 1) * hb, last_hb), cb0 + j)),
                  pl.BlockSpec((CONV_K, CONV_W), lambda j, i: (0, j)),
                  pl.BlockSpec((1, CONV_W), lambda j, i: (0, j)),
                  pl.BlockSpec((tile, CONV_W), lambda j, i: (i, j)),
                  pl.BlockSpec((HALO, CONV_W), lambda j, i: (jnp.minimum((i + 1) * hb, last_hb), j))],
        out_specs=[pl.BlockSpec((tile, CONV_W), lambda j, i: (i, j)),
                   pl.BlockSpec((CONV_K, CONV_W), lambda j, i: (0, j)),
                   pl.BlockSpec((1, CONV_W), lambda j, i: (0, j))],
        scratch_shapes=[pltpu.VMEM((tile + HALO, CONV_W), F32)],
        compiler_params=_cparams(("arbitrary", "arbitrary")),
    )(src, src, src, w, b, dy, dy)


def _iota2(q):
    return (lax.broadcasted_iota(jnp.int32, (q, q), 0), lax.broadcasted_iota(jnp.int32, (q, q), 1))


def _lane_pick(blk, idx):
    lane = lax.broadcasted_iota(jnp.int32, (1, LANE), 1)
    return jnp.sum(jnp.where(lane == idx, blk, 0.0), axis=1, keepdims=True)


class _Decay:
    def __init__(self, a):
        q = a.shape[0]
        r, c = _iota2(q)
        self.r, self.c = r, c
        self.cum = _dot_hi((c <= r).astype(F32), a)
        self.cum_t = _dot_hi(a, (r <= c).astype(F32), TN)
        self.tot = self.cum[q - 1:q, :]
        self.e_cum = jnp.exp(self.cum)
        self.e_rest = jnp.exp(self.tot - self.cum)
        self.e_tot = jnp.exp(self.tot)

    def mask(self, lane):
        rows = lax.broadcasted_iota(jnp.int32, (LANE, 1), 0)
        cum_row = jnp.sum(jnp.where(rows == lane, self.cum_t, 0.0), axis=0, keepdims=True)
        return jnp.exp(jnp.where(self.r >= self.c, _lane_pick(self.cum, lane) - cum_row, NEG_BIG))


_SSD_B = SSM_D_INNER
_SSD_C = SSM_D_INNER + 2 * SSM_STATE


def _interleave(gens):
    results = [None] * len(gens)
    live = list(range(len(gens)))
    while live:
        for i in list(live):
            try:
                next(gens[i])
            except StopIteration as stop:
                results[i] = stop.value
                live.remove(i)
    return results


def ssd_chunk(xs, bm, cm, dt_all, dsk, dec, state, p, cb):
    lane = lax.broadcasted_iota(jnp.int32, (1, LANE), 1)
    m0 = lane < SSM_HEAD_DIM
    h0, h1 = 2 * p, 2 * p + 1

    def both(blk):
        return jnp.where(m0, _lane_pick(blk, h0), _lane_pick(blk, h1))

    xdt = xs * both(dt_all)
    l0, l1 = dec.mask(h0), dec.mask(h1)
    yield
    y_diag = _dot(cb * l0, jnp.where(m0, xdt, 0.0)) + _dot(cb * l1, jnp.where(m0, 0.0, xdt))
    y_off = _dot(cm, state, NT) * both(dec.e_cum)
    yield
    rowm = lax.broadcasted_iota(jnp.int32, (LANE, 1), 0) < SSM_HEAD_DIM
    new_state = (state * jnp.where(rowm, _lane_pick(dec.e_tot, h0), _lane_pick(dec.e_tot, h1))
                 + _dot(xdt * both(dec.e_rest), bm, TN))
    y = y_diag + y_off + both(dsk) * xs
    return y, new_state


def ssd_pairs(xs, bms, cms, small, dtb, alog, dsk, states):
    dt_all = _softplus(small + dtb)
    dec = _Decay(dt_all * (-jnp.exp(alog)))
    cbs = [_dot(cm, bm, NT) for cm, bm in zip(cms, bms)]
    res = _interleave([ssd_chunk(x, bms[p // 4], cms[p // 4], dt_all, dsk, dec, st, p, cbs[p // 4])
                       for p, (x, st) in enumerate(zip(xs, states))])
    return tuple(y for y, _ in res), tuple(s for _, s in res)


def tri_inverse(a):
    q = a.shape[0]
    r, c = _iota2(q)
    eye = (r == c).astype(F32)
    diag = (r // 16) == (c // 16)
    bd = jnp.where(diag, a, 0.0)
    off = jnp.where(diag, 0.0, a)
    b2 = _dot3(bd, bd)
    d1 = _dot3(eye - bd, eye + b2)
    yield
    b4 = _dot3(b2, b2)
    yield
    b8 = _dot3(b4, b4)
    d2 = _dot3(d1, eye + b4)
    yield
    dinv = _dot3(d2, eye + b8)
    yield
    n = _dot3(dinv, off)
    yield
    n2 = _dot3(n, n)
    yield
    m = _dot3(eye + n2, dinv)
    yield
    return _dot3(eye - n, m)


@jax.custom_vjp
def _solve_with(xinv, a, rhs):
    del a
    return _dot3(xinv, rhs)


def _solve_with_fwd(xinv, a, rhs):
    t = _dot3(xinv, rhs)
    return t, (xinv, t)


def _solve_with_bwd(res, dt):
    xinv, t = res
    d_rhs = _dot3(xinv, dt, TN)
    d_a = -_dot(d_rhs, t, NT)
    return jnp.zeros_like(xinv), d_a, d_rhs


_solve_with.defvjp(_solve_with_fwd, _solve_with_bwd)


_GDN_K = GDN_HEADS * GDN_HEAD
_GDN_V = 2 * GDN_HEADS * GDN_HEAD


def gdn_chunk(qh, kh, vh, beta_all, dec, state, h, xinv=None):
    r, c = dec.r, dec.c
    qn = qh * lax.rsqrt(jnp.sum(qh * qh, axis=-1, keepdims=True) + EPS) * (GDN_HEAD ** -0.5)
    kn = kh * lax.rsqrt(jnp.sum(kh * kh, axis=-1, keepdims=True) + EPS)
    beta = _lane_pick(beta_all, SM_B + h)
    decay = dec.mask(SM_A + h)
    yield
    kk = _dot(kn, kn, NT)
    qk = _dot(qn, kn, NT) * decay
    amat = jnp.where(r > c, kk * decay * beta, 0.0)
    eg = _lane_pick(dec.e_cum, SM_A + h)
    rhs = jnp.concatenate([vh * beta, kn * (beta * eg)], axis=1)
    qs = _dot(qn * eg, state)
    yield
    if xinv is None:
        xinv = yield from tri_inverse(amat)
        t = _dot3(xinv, rhs)
    else:
        t = _solve_with(xinv, amat, rhs)
    yield
    u, w = t[:, :GDN_HEAD], t[:, GDN_HEAD:]
    v_new = u - _dot(w, state)
    yield
    o = qs + _dot(qk, v_new)
    new_state = (state * _lane_pick(dec.e_tot, SM_A + h)
                 + _dot(kn * _lane_pick(dec.e_rest, SM_A + h), v_new, TN))
    return o, new_state, xinv


def gdn_heads(qs, ks, vs, small, alog, dtb, states, xinvs=None):
    nh = len(qs)
    beta_all = _sigmoid(small)
    dec = _Decay(-jnp.exp(alog) * _softplus(small + dtb))
    res = _interleave([gdn_chunk(qs[h], ks[h], vs[h], beta_all, dec, states[h], h,
                                 None if xinvs is None else xinvs[h]) for h in range(nh)])
    return tuple(o for o, _, _ in res), tuple(s for _, s, _ in res), tuple(x for _, _, x in res)


def _acc(ref, val, first):
    @pl.when(first)
    def _():
        ref[...] = val

    @pl.when(jnp.logical_not(first))
    def _():
        ref[...] += val


def ssd_scan_fwd(name, xbc, proj, dtb, alog, dsk):
    t = xbc.shape[0]
    nc, npair = t // CHUNK, SSM_HEADS // 2
    small_blk = AL_SMALL // LANE

    def body(xbc_ref, sm_ref, dtb_ref, alog_ref, dsk_ref, y_ref, sin_ref, st_ref):
        ci = pl.program_id(0)

        @pl.when(ci == 0)
        def _():
            st_ref[...] = jnp.zeros_like(st_ref)

        s_in = tuple(st_ref[p] for p in range(npair))
        ys, s_new = ssd_pairs(tuple(xbc_ref[:, p * LANE:(p + 1) * LANE] for p in range(npair)),
                              tuple(xbc_ref[:, _SSD_B + g * LANE:_SSD_B + (g + 1) * LANE] for g in range(2)),
                              tuple(xbc_ref[:, _SSD_C + g * LANE:_SSD_C + (g + 1) * LANE] for g in range(2)),
                              sm_ref[...], dtb_ref[...], alog_ref[...], dsk_ref[...], s_in)
        for p in range(npair):
            sin_ref[0, p] = s_in[p]
            y_ref[:, p * LANE:(p + 1) * LANE] = ys[p]
            st_ref[p] = s_new[p]

    par = pl.BlockSpec((1, LANE), lambda ci: (0, 0))
    return pl.pallas_call(
        body, name=name,
        out_shape=[jax.ShapeDtypeStruct((t, SSM_D_INNER), F32),
                   jax.ShapeDtypeStruct((nc, npair, LANE, LANE), F32)],
        grid=(nc,),
        in_specs=[pl.BlockSpec((CHUNK, SSM_CONV_DIM), lambda ci: (ci, 0)),
                  pl.BlockSpec((CHUNK, LANE), lambda ci: (ci, small_blk)),
                  par, par, par],
        out_specs=[pl.BlockSpec((CHUNK, SSM_D_INNER), lambda ci: (ci, 0)),
                   pl.BlockSpec((1, npair, LANE, LANE), lambda ci: (ci, 0, 0, 0))],
        scratch_shapes=[pltpu.VMEM((npair, LANE, LANE), F32)],
        compiler_params=_cparams(("arbitrary",)),
    )(xbc, proj, dtb, alog, dsk)


def ssd_scan_bwd(name, xbc, proj, dtb, alog, dsk, s_in, dy):
    t = xbc.shape[0]
    nc, npair = t // CHUNK, SSM_HEADS // 2
    small_blk = AL_SMALL // LANE

    def body(xbc_ref, sm_ref, dtb_ref, alog_ref, dsk_ref, sin_ref, dy_ref,
             dxbc_ref, dsm_ref, ddtb_ref, dalog_ref, ddsk_ref, dst_ref):
        ci = pl.program_id(0)

        @pl.when(ci == 0)
        def _():
            dst_ref[...] = jnp.zeros_like(dst_ref)

        _, vjp = jax.vjp(ssd_pairs, tuple(xbc_ref[:, p * LANE:(p + 1) * LANE] for p in range(npair)),
                         tuple(xbc_ref[:, _SSD_B + g * LANE:_SSD_B + (g + 1) * LANE] for g in range(2)),
                         tuple(xbc_ref[:, _SSD_C + g * LANE:_SSD_C + (g + 1) * LANE] for g in range(2)),
                         sm_ref[...], dtb_ref[...], alog_ref[...], dsk_ref[...],
                         tuple(sin_ref[0, p] for p in range(npair)))
        dxs, dbms, dcms, dsm, ddtb, dalog, ddsk, dsts = vjp(
            (tuple(dy_ref[:, p * LANE:(p + 1) * LANE] for p in range(npair)),
             tuple(dst_ref[p] for p in range(npair))))
        for p in range(npair):
            dxbc_ref[:, p * LANE:(p + 1) * LANE] = dxs[p]
            dst_ref[p] = dsts[p]
        for g in range(2):
            dxbc_ref[:, _SSD_B + g * LANE:_SSD_B + (g + 1) * LANE] = dbms[g]
            dxbc_ref[:, _SSD_C + g * LANE:_SSD_C + (g + 1) * LANE] = dcms[g]
        dsm_ref[...] = dsm
        _acc(ddtb_ref, ddtb, ci == 0)
        _acc(dalog_ref, dalog, ci == 0)
        _acc(ddsk_ref, ddsk, ci == 0)

    par = pl.BlockSpec((1, LANE), lambda ci: (0, 0))
    rev = lambda ci: nc - 1 - ci
    return pl.pallas_call(
        body, name=name,
        out_shape=[jax.ShapeDtypeStruct((t, SSM_CONV_DIM), F32),
                   jax.ShapeDtypeStruct((t, LANE), F32),
                   jax.ShapeDtypeStruct((1, LANE), F32), jax.ShapeDtypeStruct((1, LANE), F32),
                   jax.ShapeDtypeStruct((1, LANE), F32)],
        grid=(nc,),
        in_specs=[pl.BlockSpec((CHUNK, SSM_CONV_DIM), lambda ci: (rev(ci), 0)),
                  pl.BlockSpec((CHUNK, LANE), lambda ci: (rev(ci), small_blk)),
                  par, par, par,
                  pl.BlockSpec((1, npair, LANE, LANE), lambda ci: (rev(ci), 0, 0, 0)),
                  pl.BlockSpec((CHUNK, SSM_D_INNER), lambda ci: (rev(ci), 0))],
        out_specs=[pl.BlockSpec((CHUNK, SSM_CONV_DIM), lambda ci: (rev(ci), 0)),
                   pl.BlockSpec((CHUNK, LANE), lambda ci: (rev(ci), 0)),
                   par, par, par],
        scratch_shapes=[pltpu.VMEM((npair, LANE, LANE), F32)],
        compiler_params=_cparams(("arbitrary",)),
    )(xbc, proj, dtb, alog, dsk, s_in, dy)


def gdn_scan_fwd(name, qkv, proj, alog, dtb):
    t = qkv.shape[0]
    nc, nh = t // CHUNK, GDN_HEADS
    small_blk = AL_SMALL // LANE

    def body(qkv_ref, sm_ref, alog_ref, dtb_ref, o_ref, sin_ref, x_ref, st_ref):
        ci = pl.program_id(0)

        @pl.when(ci == 0)
        def _():
            st_ref[...] = jnp.zeros_like(st_ref)

        s_in = tuple(st_ref[h] for h in range(nh))
        os, s_new, xinvs = gdn_heads(
            tuple(qkv_ref[:, h * LANE:(h + 1) * LANE] for h in range(nh)),
            tuple(qkv_ref[:, _GDN_K + h * LANE:_GDN_K + (h + 1) * LANE] for h in range(nh)),
            tuple(qkv_ref[:, _GDN_V + h * LANE:_GDN_V + (h + 1) * LANE] for h in range(nh)),
            sm_ref[...], alog_ref[...], dtb_ref[...], s_in)
        for h in range(nh):
            sin_ref[0, h] = s_in[h]
            o_ref[:, h * LANE:(h + 1) * LANE] = os[h]
            x_ref[0, h] = xinvs[h]
            st_ref[h] = s_new[h]

    par = pl.BlockSpec((1, LANE), lambda ci: (0, 0))
    return pl.pallas_call(
        body, name=name,
        out_shape=[jax.ShapeDtypeStruct((t, GDN_HEADS * GDN_HEAD), F32),
                   jax.ShapeDtypeStruct((nc, nh, LANE, LANE), F32),
                   jax.ShapeDtypeStruct((nc, nh, CHUNK, CHUNK), F32)],
        grid=(nc,),
        in_specs=[pl.BlockSpec((CHUNK, GDN_QKV_DIM), lambda ci: (ci, 0)),
                  pl.BlockSpec((CHUNK, LANE), lambda ci: (ci, small_blk)),
                  par, par],
        out_specs=[pl.BlockSpec((CHUNK, GDN_HEADS * GDN_HEAD), lambda ci: (ci, 0)),
                   pl.BlockSpec((1, nh, LANE, LANE), lambda ci: (ci, 0, 0, 0)),
                   pl.BlockSpec((1, nh, CHUNK, CHUNK), lambda ci: (ci, 0, 0, 0))],
        scratch_shapes=[pltpu.VMEM((nh, LANE, LANE), F32)],
        compiler_params=_cparams(("arbitrary",)),
    )(qkv, proj, alog, dtb)


def gdn_scan_bwd(name, qkv, proj, alog, dtb, s_in, xinv, do, dsm_in):
    t = qkv.shape[0]
    nc, nh = t // CHUNK, GDN_HEADS
    small_blk = AL_SMALL // LANE

    def body(qkv_ref, sm_ref, alog_ref, dtb_ref, sin_ref, x_ref, do_ref, dsmi_ref,
             dqkv_ref, dsm_ref, dalog_ref, ddtb_ref, dst_ref):
        ci = pl.program_id(0)

        @pl.when(ci == 0)
        def _():
            dst_ref[...] = jnp.zeros_like(dst_ref)

        xis = tuple(x_ref[0, h] for h in range(nh))

        def fn(qs, ks, vs, sm, alog_, dtb_, sts):
            os, s_new, _ = gdn_heads(qs, ks, vs, sm, alog_, dtb_, sts, xinvs=xis)
            return os, s_new

        _, vjp = jax.vjp(fn, tuple(qkv_ref[:, h * LANE:(h + 1) * LANE] for h in range(nh)),
                         tuple(qkv_ref[:, _GDN_K + h * LANE:_GDN_K + (h + 1) * LANE] for h in range(nh)),
                         tuple(qkv_ref[:, _GDN_V + h * LANE:_GDN_V + (h + 1) * LANE] for h in range(nh)),
                         sm_ref[...], alog_ref[...], dtb_ref[...], tuple(sin_ref[0, h] for h in range(nh)))
        dqs, dks, dvs, dsm, dalog, ddtb, dsts = vjp(
            (tuple(do_ref[:, h * LANE:(h + 1) * LANE] for h in range(nh)), tuple(dst_ref[h] for h in range(nh))))
        for h in range(nh):
            dqkv_ref[:, h * LANE:(h + 1) * LANE] = dqs[h]
            dqkv_ref[:, _GDN_K + h * LANE:_GDN_K + (h + 1) * LANE] = dks[h]
            dqkv_ref[:, _GDN_V + h * LANE:_GDN_V + (h + 1) * LANE] = dvs[h]
            dst_ref[h] = dsts[h]
        dsm_ref[...] = dsmi_ref[...] + dsm
        _acc(dalog_ref, dalog, ci == 0)
        _acc(ddtb_ref, ddtb, ci == 0)

    par = pl.BlockSpec((1, LANE), lambda ci: (0, 0))
    rev = lambda ci: nc - 1 - ci
    return pl.pallas_call(
        body, name=name,
        out_shape=[jax.ShapeDtypeStruct((t, GDN_QKV_DIM), F32), jax.ShapeDtypeStruct((t, LANE), F32),
                   jax.ShapeDtypeStruct((1, LANE), F32), jax.ShapeDtypeStruct((1, LANE), F32)],
        grid=(nc,),
        in_specs=[pl.BlockSpec((CHUNK, GDN_QKV_DIM), lambda ci: (rev(ci), 0)),
                  pl.BlockSpec((CHUNK, LANE), lambda ci: (rev(ci), small_blk)),
                  par, par,
                  pl.BlockSpec((1, nh, LANE, LANE), lambda ci: (rev(ci), 0, 0, 0)),
                  pl.BlockSpec((1, nh, CHUNK, CHUNK), lambda ci: (rev(ci), 0, 0, 0)),
                  pl.BlockSpec((CHUNK, GDN_HEADS * GDN_HEAD), lambda ci: (rev(ci), 0)),
                  pl.BlockSpec((CHUNK, LANE), lambda ci: (rev(ci), 0))],
        out_specs=[pl.BlockSpec((CHUNK, GDN_QKV_DIM), lambda ci: (rev(ci), 0)),
                   pl.BlockSpec((CHUNK, LANE), lambda ci: (rev(ci), 0)),
                   par, par],
        scratch_shapes=[pltpu.VMEM((nh, LANE, LANE), F32)],
        compiler_params=_cparams(("arbitrary",)),
    )(qkv, proj, alog, dtb, s_in, xinv, do, dsm_in)


def _row(arr, w, c0=0, moves=False):
    return (arr, "row", w, c0, moves)


def _par(arr, w, c0=0, moves=False):
    return (arr, "par", w, c0, moves)


def matmul_add(name, a, b, res):
    (m, k), (_, n) = a.shape, b.shape
    tm, tn, tk = _matmul_tiles(m, n, n, k, a.dtype.itemsize, b.dtype.itemsize, 4 + res.dtype.itemsize)
    nk = k // tk

    def body_acc(a_ref, b_ref, r_ref, o_ref, acc_ref):
        kk = pl.program_id(2)

        @pl.when(kk == 0)
        def _():
            acc_ref[...] = r_ref[...]

        acc_ref[...] += _dot(a_ref[...], b_ref[...])

        @pl.when(kk == nk - 1)
        def _():
            o_ref[...] = acc_ref[...]

    def body_one(a_ref, b_ref, r_ref, o_ref):
        o_ref[...] = r_ref[...] + _dot(a_ref[...], b_ref[...])

    body = body_one if nk == 1 else body_acc
    return pl.pallas_call(
        body, name=name,
        out_shape=jax.ShapeDtypeStruct((m, n), F32),
        grid=(m // tm, n // tn, nk),
        in_specs=[pl.BlockSpec((tm, tk), lambda i, j, kk: (i, kk)),
                  pl.BlockSpec((tk, tn), lambda i, j, kk: (kk, j)),
                  pl.BlockSpec((tm, tn), lambda i, j, kk: (i, j))],
        out_specs=pl.BlockSpec((tm, tn), lambda i, j, kk: (i, j)),
        scratch_shapes=[] if nk == 1 else [pltpu.VMEM((tm, tn), F32)],
        compiler_params=_cparams(("parallel", "parallel", "arbitrary")),
    )(a, b, res)


def layer_fwd(l, x, w):
    t = x.shape[0]
    rt = min(256, t)
    s = {"x": x}
    s["h"] = rowwise_fwd(f"norm_mix_l{l}", f_rmsnorm, t, rt, 1,
                         [_row(x, D_MODEL), _par(w["norm_mix_w"], D_MODEL)], [(D_MODEL, BF16)])[0]
    s["proj"] = matmul(f"in_proj_l{l}", s["h"], w["w_in"], "nn")
    s["xbc"] = conv_fwd(f"ssm_conv_l{l}", s["proj"], AL_XBC, SSM_CONV_DIM, w["ssm_conv_w"], w["ssm_conv_b"],
                        tile=min(512, t))
    s["qkv"] = conv_fwd(f"gdn_conv_l{l}", s["proj"], AL_QKV, GDN_QKV_DIM, w["gdn_conv_w"], w["gdn_conv_b"],
                        tile=min(512, t))
    s["y_scan"], s["ssd_sin"] = ssd_scan_fwd(f"ssd_scan_l{l}", s["xbc"], s["proj"], w["ssm_dt_bias"],
                                             w["ssm_a_log"], w["ssm_d"])
    s["o_scan"], s["gdn_sin"], s["gdn_x"] = gdn_scan_fwd(f"gdn_scan_l{l}", s["qkv"], s["proj"],
                                                         w["gdn_a_log"], w["gdn_dt_bias"])
    s["y_ssm"] = rowwise_fwd(f"ssd_post_l{l}", f_ssd_post, t, rt, 2,
                             [_row(s["y_scan"], 512, 0, True), _row(s["proj"], 512, AL_Z // 512, True),
                              _par(w["ssm_norm_w"], 512, 0, True)], [(512, BF16)])[0]
    s["y_gdn"] = rowwise_fwd(f"gdn_post_l{l}", f_gdn_post, t, rt, GDN_HEADS,
                             [_row(s["o_scan"], LANE, 0, True), _row(s["proj"], LANE, AL_GZ // LANE, True),
                              _par(w["gdn_norm_w"], LANE)], [(LANE, BF16)])[0]
    s["p1"] = matmul(f"proj_ssm_l{l}", s["y_ssm"], w["w_proj_ssm"], "nn")
    s["p2"] = matmul(f"proj_gdn_l{l}", s["y_gdn"], w["w_proj_gdn"], "nn")
    s["merged"] = rowwise_fwd(f"merge_l{l}", f_merge, t, rt, 2,
                              [_row(s["proj"], 512, AL_GS // 512, True), _row(s["p1"], 512, 0, True),
                               _row(s["proj"], 512, AL_GG // 512, True), _row(s["p2"], 512, 0, True)],
                              [(512, BF16)])[0]
    s["x1"] = matmul_add(f"out_proj_l{l}", s["merged"], w["w_out"], x)
    s["h2"] = rowwise_fwd(f"norm_ffn_l{l}", f_rmsnorm, t, rt, 1,
                          [_row(s["x1"], D_MODEL), _par(w["norm_ffn_w"], D_MODEL)], [(D_MODEL, BF16)])[0]
    s["gu"] = matmul(f"ffn_in_l{l}", s["h2"], w["w_ffn_in"], "nn")
    s["act"] = rowwise_fwd(f"swiglu_l{l}", f_swiglu, t, rt, FFN_HIDDEN // 256,
                           [_row(s["gu"], 256, 0, True), _row(s["gu"], 256, FFN_HIDDEN // 256, True)],
                           [(256, BF16)])[0]
    x2 = matmul_add(f"ffn_down_l{l}", s["act"], w["w_ffn_down"], s["x1"])
    return x2, s


IN_SHARD = IN_DIM // 4
IN_SHARD_PAD = 2304


def _aligned_to_shards(g):
    orig = jnp.concatenate([g[:, 0:2560], g[:, AL_SMALL:AL_SMALL + 16], g[:, 2560:6656],
                            g[:, AL_SMALL + 16:AL_SMALL + 32], g[:, 6656:8704]], axis=1)
    return jnp.stack([jnp.pad(orig[:, j * IN_SHARD:(j + 1) * IN_SHARD], ((0, 0), (0, IN_SHARD_PAD - IN_SHARD)))
                      for j in range(N_CHIPS)])


def layer_bwd(l, dx2, w, s, gbuf):
    t = dx2.shape[0]
    rt = min(256, t)
    ct = min(512, t)
    g = {}
    dact = matmul(f"ffn_down_dx_l{l}", dx2, w["w_ffn_down"], "nt")
    g["w_ffn_down"] = matmul(f"ffn_down_dw_l{l}", s["act"], dx2, "tn", stack=(l, gbuf.get("w_ffn_down")))
    nf = FFN_HIDDEN // 256
    dgate, dup = rowwise_bwd(f"swiglu_bwd_l{l}", f_swiglu, t, rt, nf,
                             [_row(s["gu"], 256, 0, True), _row(s["gu"], 256, nf, True)], [True, True],
                             [_row(dact, 256, 0, True)], row_dtypes={0: BF16, 1: BF16})
    dgu = jnp.concatenate([dgate, dup], axis=1)
    dh2 = matmul(f"ffn_in_dx_l{l}", dgu, w["w_ffn_in"], "nt")
    g["w_ffn_in"] = matmul(f"ffn_in_dw_l{l}", s["h2"], dgu, "tn", chip_major=True,
                           stack=(l, gbuf.get("w_ffn_in")))
    dx1, g["norm_ffn_w"] = rowwise_bwd(f"norm_ffn_bwd_l{l}", f_rmsnorm, t, rt, 1,
                                       [_row(s["x1"], D_MODEL), _par(w["norm_ffn_w"], D_MODEL)], [True, True],
                                       [_row(dh2, D_MODEL)], addends={0: _row(dx2, D_MODEL)})
    dmerged = matmul(f"out_proj_dx_l{l}", dx1, w["w_out"], "nt")
    g["w_out"] = matmul(f"out_proj_dw_l{l}", s["merged"], dx1, "tn", stack=(l, gbuf.get("w_out")))
    dgs, dp1, dgg, dp2 = rowwise_bwd(
        f"merge_bwd_l{l}", f_merge, t, rt, 2,
        [_row(s["proj"], 512, AL_GS // 512, True), _row(s["p1"], 512, 0, True),
         _row(s["proj"], 512, AL_GG // 512, True), _row(s["p2"], 512, 0, True)], [True] * 4,
        [_row(dmerged, 512, 0, True)], row_dtypes={0: BF16, 1: BF16, 2: BF16, 3: BF16})
    dy_ssm = matmul(f"proj_ssm_dx_l{l}", dp1, w["w_proj_ssm"], "nt")
    g["w_proj_ssm"] = matmul(f"proj_ssm_dw_l{l}", s["y_ssm"], dp1, "tn", stack=(l, gbuf.get("w_proj_ssm")))
    dy_gdn = matmul(f"proj_gdn_dx_l{l}", dp2, w["w_proj_gdn"], "nt")
    g["w_proj_gdn"] = matmul(f"proj_gdn_dw_l{l}", s["y_gdn"], dp2, "tn", stack=(l, gbuf.get("w_proj_gdn")))
    dy_scan, dz, g["ssm_norm_w"] = rowwise_bwd(
        f"ssd_post_bwd_l{l}", f_ssd_post, t, rt, 2,
        [_row(s["y_scan"], 512, 0, True), _row(s["proj"], 512, AL_Z // 512, True),
         _par(w["ssm_norm_w"], 512, 0, True)], [True] * 3, [_row(dy_ssm, 512, 0, True)], row_dtypes={1: BF16})
    dxbc_act, dsm, g["ssm_dt_bias"], g["ssm_a_log"], g["ssm_d"] = ssd_scan_bwd(
        f"ssd_scan_bwd_l{l}", s["xbc"], s["proj"], w["ssm_dt_bias"], w["ssm_a_log"], w["ssm_d"],
        s["ssd_sin"], dy_scan)
    dxbc, g["ssm_conv_w"], g["ssm_conv_b"] = conv_bwd(
        f"ssm_conv_bwd_l{l}", s["proj"], AL_XBC, SSM_CONV_DIM, w["ssm_conv_w"], w["ssm_conv_b"], dxbc_act, tile=ct)
    do_scan, dgz, g["gdn_norm_w"] = rowwise_bwd(
        f"gdn_post_bwd_l{l}", f_gdn_post, t, rt, GDN_HEADS,
        [_row(s["o_scan"], LANE, 0, True), _row(s["proj"], LANE, AL_GZ // LANE, True),
         _par(w["gdn_norm_w"], LANE)], [True] * 3, [_row(dy_gdn, LANE, 0, True)], row_dtypes={1: BF16})
    dqkv_act, dsm, g["gdn_a_log"], g["gdn_dt_bias"] = gdn_scan_bwd(
        f"gdn_scan_bwd_l{l}", s["qkv"], s["proj"], w["gdn_a_log"], w["gdn_dt_bias"], s["gdn_sin"],
        s["gdn_x"], do_scan, dsm)
    dqkv, g["gdn_conv_w"], _ = conv_bwd(
        f"gdn_conv_bwd_l{l}", s["proj"], AL_QKV, GDN_QKV_DIM, w["gdn_conv_w"], w["gdn_conv_b"], dqkv_act, tile=ct)
    dproj = jnp.concatenate([dz, dxbc, dqkv, dgz, dgs, dgg, dsm.astype(BF16),
                             jnp.zeros((t, AL_DIM - AL_SMALL - LANE), BF16)], axis=1)
    dh = matmul(f"in_proj_dx_l{l}", dproj, w["w_in"], "nt")
    g["w_in"] = matmul(f"in_proj_dw_l{l}", s["h"], dproj, "tn")
    dx0, g["norm_mix_w"] = rowwise_bwd(f"norm_mix_bwd_l{l}", f_rmsnorm, t, rt, 1,
                                       [_row(s["x"], D_MODEL), _par(w["norm_mix_w"], D_MODEL)], [True, True],
                                       [_row(dh, D_MODEL)], addends={0: _row(dx1, D_MODEL)})
    return dx0, g


def _align_w_in(w):
    pad = jnp.zeros((w.shape[0], AL_DIM - AL_SMALL - 32), w.dtype)
    return jnp.concatenate([w[:, 0:2560], w[:, 2576:6672], w[:, 6688:8736],
                            w[:, 2560:2576], w[:, 6672:6688], pad], axis=1)


def _pad_lane(v, at=0):
    return jnp.pad(v[None], ((0, 0), (at, LANE - at - v.shape[0])))


def local_step(x, target, full):
    ws = []
    for l in range(DEPTH):
        ws.append({
            "norm_mix_w": full["norm_mix_w"][l][None], "w_in": _align_w_in(full["w_in"][l]),
            "ssm_conv_w": full["ssm_conv_w"][l], "ssm_conv_b": full["ssm_conv_b"][l][None],
            "ssm_dt_bias": _pad_lane(full["ssm_dt_bias"][l]), "ssm_a_log": _pad_lane(full["ssm_a_log"][l]),
            "ssm_d": _pad_lane(full["ssm_d"][l]), "ssm_norm_w": full["ssm_norm_w"][l][None],
            "gdn_conv_w": full["gdn_conv_w"][l], "gdn_conv_b": jnp.zeros((1, GDN_QKV_DIM), F32),
            "gdn_a_log": _pad_lane(full["gdn_a_log"][l], SM_A),
            "gdn_dt_bias": _pad_lane(full["gdn_dt_bias"][l], SM_A),
            "gdn_norm_w": full["gdn_norm_w"][l][None],
            "w_proj_ssm": full["w_proj_ssm"][l], "w_proj_gdn": full["w_proj_gdn"][l], "w_out": full["w_out"][l],
            "norm_ffn_w": full["norm_ffn_w"][l][None], "w_ffn_in": full["w_ffn_in"][l],
            "w_ffn_down": full["w_ffn_down"][l],
        })
    saved = []
    h = x
    for l in range(DEPTH):
        h, s = layer_fwd(l, h, ws[l])
        saved.append(s)
    loss, dx, g_final = final_loss("final_loss", h, target, full["final_norm_w"][None], tile=min(256, x.shape[0]))
    per_layer = [None] * DEPTH
    gbuf = {}
    for l in reversed(range(DEPTH)):
        dx, per_layer[l] = layer_bwd(l, dx, ws[l], saved[l], gbuf)
        gbuf = {n: per_layer[l][n] for n, _ in BIG if n != "w_in"}
    gbuf["w_in"] = jnp.stack([_aligned_to_shards(per_layer[l]["w_in"]) for l in range(DEPTH)])
    grads = {"final_norm_w": g_final[0], **gbuf}
    for name in per_layer[0]:
        if name in gbuf:
            continue
        rows = []
        for l in range(DEPTH):
            gl = per_layer[l][name]
            if name in ("ssm_dt_bias", "ssm_a_log", "ssm_d"):
                gl = gl[0, :SSM_HEADS]
            elif name in ("gdn_a_log", "gdn_dt_bias"):
                gl = gl[0, SM_A:SM_A + GDN_HEADS]
            elif name in ("norm_mix_w", "ssm_conv_b", "ssm_norm_w", "gdn_norm_w", "norm_ffn_w"):
                gl = gl[0]
            rows.append(gl)
        grads[name] = jnp.stack(rows)
    return loss, dx, grads


MESH = pl.DeviceIdType.MESH
HBM = pl.BlockSpec(memory_space=pltpu.HBM)
N_DEV = 8


def _pos():
    return lax.axis_index("x"), lax.axis_index("y"), lax.axis_index("c")


def _rcopy(src, dst, send_sem, recv_sem, dev):
    return pltpu.make_async_remote_copy(src_ref=src, dst_ref=dst, send_sem=send_sem, recv_sem=recv_sem,
                                        device_id=dev, device_id_type=MESH)


RELATIONS = (2, 1, 3)


def _related_chip(x, y, mask):
    return (1 - x if mask & 2 else x, 1 - y if mask & 1 else y)


def weights_gather(name, bufs):
    n = len(bufs)

    def body(*refs):
        outs, send_sems, recv_sems = refs[n:2 * n], refs[2 * n], refs[2 * n + 1]
        x, y, c = _pos()
        sib = (x, y, 1 - c)
        sends = []
        for i, a in enumerate(outs):
            for k, m in enumerate(RELATIONS):
                px, py = _related_chip(x, y, m)
                cp = _rcopy(a.at[0, c], a.at[m, c], send_sems.at[6 * i + k], recv_sems.at[6 * i + k], (px, py, c))
                cp.start()
                sends.append(cp)
        for i, a in enumerate(outs):
            for k, m in enumerate(RELATIONS):
                px, py = _related_chip(x, y, m)
                _rcopy(a.at[0, c], a.at[m, c], send_sems.at[6 * i + k], recv_sems.at[6 * i + k],
                       (px, py, c)).wait_recv()
                fw = _rcopy(a.at[m, c], a.at[m, c], send_sems.at[6 * i + 3 + k], recv_sems.at[6 * i + 3 + k], sib)
                fw.start()
                sends.append(fw)
        for i, a in enumerate(outs):
            for k, m in enumerate(RELATIONS):
                _rcopy(a.at[m, 1 - c], a.at[m, 1 - c], send_sems.at[6 * i + 3 + k], recv_sems.at[6 * i + 3 + k],
                       sib).wait_recv()
        for cp in sends:
            cp.wait_send()

    return pl.pallas_call(
        body, name=name, out_shape=[jax.ShapeDtypeStruct(b.shape, b.dtype) for b in bufs],
        in_specs=[HBM] * n, out_specs=[HBM] * n,
        input_output_aliases={i: i for i in range(n)},
        scratch_shapes=[pltpu.SemaphoreType.DMA((6 * n,)), pltpu.SemaphoreType.DMA((6 * n,))],
    )(*bufs)


def pair_swap(name, gs):
    n = len(gs)

    def body(*refs):
        srcs, outs, send_sems, recv_sems = refs[:n], refs[n:2 * n], refs[2 * n], refs[2 * n + 1]
        x, y, c = _pos()
        cps = [_rcopy(s.at[1 - c], o, send_sems.at[i], recv_sems.at[i], (x, y, 1 - c))
               for i, (s, o) in enumerate(zip(srcs, outs))]
        for cp in cps:
            cp.start()
        for cp in cps:
            cp.wait()

    return pl.pallas_call(
        body, name=name, out_shape=[jax.ShapeDtypeStruct(g.shape[1:], g.dtype) for g in gs],
        in_specs=[HBM] * n, out_specs=[HBM] * n,
        scratch_shapes=[pltpu.SemaphoreType.DMA((n,)), pltpu.SemaphoreType.DMA((n,))],
    )(*gs)


def chip_scatter(name, ss):
    n = len(ss)

    def body(*refs):
        srcs, outs, send_sems, recv_sems = refs[:n], refs[n:2 * n], refs[2 * n], refs[2 * n + 1]
        x, y, c = _pos()
        sends = []
        for i, (s, o) in enumerate(zip(srcs, outs)):
            for k, m in enumerate(RELATIONS):
                px, py = _related_chip(x, y, m)
                cp = _rcopy(s.at[2 * px + py], o.at[k], send_sems.at[3 * i + k], recv_sems.at[3 * i + k],
                            (px, py, c))
                cp.start()
                sends.append(cp)
        for cp in sends:
            cp.wait()

    return pl.pallas_call(
        body, name=name, out_shape=[jax.ShapeDtypeStruct((3,) + s.shape[1:], s.dtype) for s in ss],
        in_specs=[HBM] * n, out_specs=[HBM] * n,
        scratch_shapes=[pltpu.SemaphoreType.DMA((3 * n,)), pltpu.SemaphoreType.DMA((3 * n,))],
    )(*ss)


def pair_share(name, bufs):
    n = len(bufs)

    def body(*refs):
        outs, send_sems, recv_sems = refs[n:2 * n], refs[2 * n], refs[2 * n + 1]
        x, y, c = _pos()
        sends = []
        for i, o in enumerate(outs):
            cp = _rcopy(o.at[c], o.at[c], send_sems.at[i], recv_sems.at[i], (x, y, 1 - c))
            cp.start()
            sends.append(cp)
        for i, o in enumerate(outs):
            _rcopy(o.at[1 - c], o.at[1 - c], send_sems.at[i], recv_sems.at[i], (x, y, 1 - c)).wait_recv()
        for cp in sends:
            cp.wait_send()

    return pl.pallas_call(
        body, name=name, out_shape=[jax.ShapeDtypeStruct(b.shape, b.dtype) for b in bufs],
        in_specs=[HBM] * n, out_specs=[HBM] * n,
        input_output_aliases={i: i for i in range(n)},
        scratch_shapes=[pltpu.SemaphoreType.DMA((n,)), pltpu.SemaphoreType.DMA((n,))],
    )(*bufs)


def all_allgather(name, buf):
    r, cd = buf.shape

    def body(src, out, send_sems, recv_sems, lsem):
        x, y, c = _pos()
        me = 4 * x + 2 * y + c
        local = pltpu.make_async_copy(src, out.at[me], lsem)
        local.start()

        def peer(mask):
            px = 1 - x if mask & 4 else x
            py = 1 - y if mask & 2 else y
            pc = 1 - c if mask & 1 else c
            return px, py, pc

        sends = []
        for mask in range(1, N_DEV):
            cp = _rcopy(src, out.at[me], send_sems.at[mask - 1], recv_sems.at[mask - 1], peer(mask))
            cp.start()
            sends.append(cp)
        for mask in range(1, N_DEV):
            px, py, pc = peer(mask)
            _rcopy(src, out.at[4 * px + 2 * py + pc], send_sems.at[mask - 1], recv_sems.at[mask - 1],
                   (px, py, pc)).wait_recv()
        for cp in sends:
            cp.wait_send()
        local.wait()

    return pl.pallas_call(
        body, name=name, out_shape=jax.ShapeDtypeStruct((N_DEV, r, cd), buf.dtype),
        in_specs=[HBM], out_specs=HBM,
        scratch_shapes=[pltpu.SemaphoreType.DMA((N_DEV - 1,)), pltpu.SemaphoreType.DMA((N_DEV - 1,)),
                        pltpu.SemaphoreType.DMA(())],
    )(buf)


ELEMENTWISE_BLOCK_BYTES = 2 << 20


def _row_block(rows, cols):
    for cand in (1024, 512, 256, 128, 64, 32, 16):
        if rows % cand == 0 and cand * cols * 4 <= ELEMENTWISE_BLOCK_BYTES:
            return cand
    return rows


def chip_sum(name, s, r, me, c):
    _, a, b = s.shape
    tr = _row_block(a, b)

    def body(idx_ref, s_ref, r_ref, o_ref):
        del idx_ref
        acc = s_ref[...].astype(F32)
        for k in range(3):
            acc = acc + r_ref[k].astype(F32)
        o_ref[...] = acc

    return pl.pallas_call(
        body, name=name, out_shape=jax.ShapeDtypeStruct((2, a, b), F32),
        grid_spec=pltpu.PrefetchScalarGridSpec(
            num_scalar_prefetch=1, grid=(a // tr,),
            in_specs=[pl.BlockSpec((None, tr, b), lambda i, idx: (idx[0], i, 0)),
                      pl.BlockSpec((3, tr, b), lambda i, idx: (0, i, 0))],
            out_specs=pl.BlockSpec((None, tr, b), lambda i, idx: (idx[1], i, 0))),
        compiler_params=_cparams(("arbitrary",)),
    )(jnp.stack([me, c]).astype(jnp.int32), s, r)


def pair_add(name, p, recv, c):
    _, nj, rh, cd = p.shape
    tr = _row_block(rh, cd)

    def body(c_ref, p_ref, r_ref, o_ref):
        del c_ref
        o_ref[...] = (p_ref[0] + r_ref[...]).astype(o_ref.dtype)

    return pl.pallas_call(
        body, name=name, out_shape=jax.ShapeDtypeStruct((nj, rh, cd), BF16),
        grid_spec=pltpu.PrefetchScalarGridSpec(
            num_scalar_prefetch=1, grid=(nj, rh // tr),
            in_specs=[pl.BlockSpec((1, 1, tr, cd), lambda j, i, c_ref: (c_ref[0], j, i, 0)),
                      pl.BlockSpec((1, tr, cd), lambda j, i, c_ref: (j, i, 0))],
            out_specs=pl.BlockSpec((1, tr, cd), lambda j, i, c_ref: (j, i, 0))),
        compiler_params=_cparams(("arbitrary", "arbitrary")),
    )(jnp.reshape(c, (1,)).astype(jnp.int32), p, recv)


def slab_sum(name, a):
    n, r, cd = a.shape
    tr = _pick(r, (256, 128, 64, 32, 16, 8))

    def body(a_ref, o_ref):
        acc = a_ref[0].astype(F32)
        for j in range(1, n):
            acc = acc + a_ref[j].astype(F32)
        o_ref[...] = acc

    return pl.pallas_call(
        body, name=name, out_shape=jax.ShapeDtypeStruct((r, cd), F32),
        grid=(r // tr,),
        in_specs=[pl.BlockSpec((n, tr, cd), lambda i: (0, i, 0))],
        out_specs=pl.BlockSpec((tr, cd), lambda i: (i, 0)),
        compiler_params=_cparams(("arbitrary",)),
    )(a)


ADAM_C1 = 1.0 - ADAM_B1 ** ADAM_STEP
ADAM_C2 = 1.0 - ADAM_B2 ** ADAM_STEP


def adamw(name, w, g, m, v):
    r, cd = w.shape
    tr = r
    for cand in (512, 256, 128, 64, 32, 16, 8):
        if r % cand == 0 and cand * cd * 4 <= (1 << 20):
            tr = cand
            break

    def body(w_ref, g_ref, m_ref, v_ref, d_ref, nm_ref, nv_ref):
        gv = g_ref[...]
        nm = ADAM_B1 * m_ref[...] + (1.0 - ADAM_B1) * gv
        nv = ADAM_B2 * v_ref[...] + (1.0 - ADAM_B2) * (gv * gv)
        m_hat = nm / ADAM_C1
        v_hat = nv / ADAM_C2
        d_ref[...] = -ADAM_LR * (m_hat / (jnp.sqrt(v_hat) + ADAM_EPS) + ADAM_WD * w_ref[...])
        nm_ref[...] = nm
        nv_ref[...] = nv

    spec = pl.BlockSpec((tr, cd), lambda i: (i, 0))
    sd = jax.ShapeDtypeStruct((r, cd), F32)
    return pl.pallas_call(
        body, name=name, out_shape=[sd, sd, sd], grid=(r // tr,),
        in_specs=[spec] * 4, out_specs=[spec] * 3,
        compiler_params=_cparams(("arbitrary",)),
    )(w, g, m, v)


WEIGHTS = ("norm_mix_w", "w_in", "ssm_conv_w", "ssm_conv_b", "ssm_dt_bias", "ssm_a_log", "ssm_d", "ssm_norm_w",
           "gdn_conv_w", "gdn_a_log", "gdn_dt_bias", "gdn_norm_w", "w_proj_ssm", "w_proj_gdn", "w_out",
           "norm_ffn_w", "w_ffn_in", "w_ffn_down", "final_norm_w")
BIG = (("w_in", 2), ("w_proj_ssm", 1), ("w_proj_gdn", 1), ("w_out", 1), ("w_ffn_in", 2), ("w_ffn_down", 1))
CONVW = (("ssm_conv_w", 2), ("gdn_conv_w", 2))
SHARDED = BIG + CONVW
SMALL = tuple(n for n in WEIGHTS if n not in dict(SHARDED))


def _unpack(buf, shapes, lead=()):
    flat = buf.reshape(lead + (-1,))
    out, o = [], 0
    for shp in shapes:
        n = math.prod(shp)
        out.append(flat[..., o:o + n].reshape(lead + tuple(shp)))
        o += n
    return out


def _pack_rows(arrs, lead=()):
    nl = len(lead)
    flat = jnp.concatenate([a.reshape(lead + (-1,)) for a in arrs], axis=nl)
    n = flat.shape[nl]
    rows = -(-n // (8 * LANE)) * 8
    flat = jnp.pad(flat, [(0, 0)] * nl + [(0, rows * LANE - n)])
    return flat.reshape(lead + (rows, LANE))


def _slot_buffer(shard):
    return jnp.pad(shard[None], [(0, N_CHIPS - 1)] + [(0, 0)] * shard.ndim)


def kernel(x, norm_mix_w, w_in, ssm_conv_w, ssm_conv_b, ssm_dt_bias, ssm_a_log, ssm_d, ssm_norm_w, gdn_conv_w, gdn_a_log, gdn_dt_bias, gdn_norm_w, w_proj_ssm, w_proj_gdn, w_out, norm_ffn_w, w_ffn_in, w_ffn_down, final_norm_w, loss_target, m_norm_mix_w, m_w_in, m_ssm_conv_w, m_ssm_conv_b, m_ssm_dt_bias, m_ssm_a_log, m_ssm_d, m_ssm_norm_w, m_gdn_conv_w, m_gdn_a_log, m_gdn_dt_bias, m_gdn_norm_w, m_w_proj_ssm, m_w_proj_gdn, m_w_out, m_norm_ffn_w, m_w_ffn_in, m_w_ffn_down, m_final_norm_w, v_norm_mix_w, v_w_in, v_ssm_conv_w, v_ssm_conv_b, v_ssm_dt_bias, v_ssm_a_log, v_ssm_d, v_ssm_norm_w, v_gdn_conv_w, v_gdn_a_log, v_gdn_dt_bias, v_gdn_norm_w, v_w_proj_ssm, v_w_proj_gdn, v_w_out, v_norm_ffn_w, v_w_ffn_in, v_w_ffn_down, v_final_norm_w):
    wl = (norm_mix_w, w_in, ssm_conv_w, ssm_conv_b, ssm_dt_bias, ssm_a_log, ssm_d, ssm_norm_w, gdn_conv_w,
          gdn_a_log, gdn_dt_bias, gdn_norm_w, w_proj_ssm, w_proj_gdn, w_out, norm_ffn_w, w_ffn_in, w_ffn_down,
          final_norm_w)
    ml = (m_norm_mix_w, m_w_in, m_ssm_conv_w, m_ssm_conv_b, m_ssm_dt_bias, m_ssm_a_log, m_ssm_d, m_ssm_norm_w,
          m_gdn_conv_w, m_gdn_a_log, m_gdn_dt_bias, m_gdn_norm_w, m_w_proj_ssm, m_w_proj_gdn, m_w_out,
          m_norm_ffn_w, m_w_ffn_in, m_w_ffn_down, m_final_norm_w)
    vl = (v_norm_mix_w, v_w_in, v_ssm_conv_w, v_ssm_conv_b, v_ssm_dt_bias, v_ssm_a_log, v_ssm_d, v_ssm_norm_w,
          v_gdn_conv_w, v_gdn_a_log, v_gdn_dt_bias, v_gdn_norm_w, v_w_proj_ssm, v_w_proj_gdn, v_w_out,
          v_norm_ffn_w, v_w_ffn_in, v_w_ffn_down, v_final_norm_w)
    w = dict(zip(WEIGHTS, wl))
    m = dict(zip(WEIGHTS, ml))
    v = dict(zip(WEIGHTS, vl))
    x_pos, y_pos, c = _pos()
    me = 2 * x_pos + y_pos
    big = [n for n, _ in BIG]

    shards = [w[n].astype(BF16) for n in big]
    shards[0] = jnp.pad(shards[0], ((0, 0), (0, 0), (0, IN_SHARD_PAD - IN_SHARD)))
    conv_shapes = [w[n].shape[1:] for n, _ in CONVW]
    conv_pack = _pack_rows([w[n] for n, _ in CONVW], lead=(DEPTH,))
    gathered = weights_gather("gather_w", [_slot_buffer(s) for s in shards + [conv_pack]])
    by_chip = [[lax.dynamic_index_in_dim(g_, jnp.bitwise_xor(me, j), 0, keepdims=False) for j in range(N_CHIPS)]
               for g_ in gathered]
    full = {n: w[n] for n in SMALL}
    for i, (n, axis) in enumerate(BIG):
        full[n] = jnp.concatenate(by_chip[i], axis=axis)
    full["w_in"] = jnp.concatenate([p[:, :, :IN_SHARD] for p in by_chip[0]], axis=2)
    conv_parts = [_unpack(by_chip[-1][j], conv_shapes, lead=(DEPTH,)) for j in range(N_CHIPS)]
    for i, (n, axis) in enumerate(CONVW):
        full[n] = jnp.concatenate([conv_parts[j][i] for j in range(N_CHIPS)], axis=axis)

    loss_part, grad_x, grads = local_step(x[0], loss_target[0], full)

    by4 = [grads[n].reshape((DEPTH, N_CHIPS, -1) + grads[n].shape[-1:]) for n in big]
    from_pair = pair_swap("grad_pair_swap", by4)
    chip_part = [pair_add(f"grad_pair_add_{n}", g_, r_, c) for n, g_, r_ in zip(big, by4, from_pair)]
    from_chips = chip_scatter("grad_chip_scatter", chip_part)
    halves = [chip_sum(f"grad_chip_sum_{n}", s_, r_, me, c) for n, s_, r_ in zip(big, chip_part, from_chips)]
    reduced = pair_share("grad_pair_share", halves)
    g_sharded = dict(zip(big, reduced))
    g_sharded["w_in"] = g_sharded["w_in"][:, :, :IN_SHARD]

    small_names = list(SMALL) + [n for n, _ in CONVW]
    small_all = all_allgather("gather_small", _pack_rows([grads[n] for n in small_names] + [loss_part[0, :1]]))
    small_sum = slab_sum("small_sum", small_all)
    small_vals = _unpack(small_sum, [grads[n].shape for n in small_names] + [(1,)])
    g_small = dict(zip(small_names, small_vals[:-1]))
    loss = small_vals[-1].reshape(())
    for n, axis in CONVW:
        size = w[n].shape[axis]
        g_sharded[n] = lax.dynamic_slice_in_dim(g_small.pop(n), me * size, size, axis=axis)

    out_g, out_d, out_m, out_v = {}, {}, {}, {}
    for n, _ in SHARDED:
        shp = w[n].shape
        two = lambda a: a.reshape(-1, shp[-1])
        d_, m_, v_ = adamw(f"adamw_{n}", two(w[n]), two(g_sharded[n]), two(m[n]), two(v[n]))
        out_g[n], out_d[n], out_m[n], out_v[n] = g_sharded[n], d_.reshape(shp), m_.reshape(shp), v_.reshape(shp)
    d_, m_, v_ = adamw("adamw_small", *[_pack_rows([d[n] for n in SMALL]) for d in (w, g_small, m, v)])
    small_shapes = [w[n].shape for n in SMALL]
    for n, dd, mm, vv in zip(SMALL, _unpack(d_, small_shapes), _unpack(m_, small_shapes), _unpack(v_, small_shapes)):
        out_g[n], out_d[n], out_m[n], out_v[n] = g_small[n], dd, mm, vv

    return (loss, grad_x[None], *[out_g[n] for n in WEIGHTS], *[out_d[n] for n in WEIGHTS],
            *[out_m[n] for n in WEIGHTS], *[out_v[n] for n in WEIGHTS])
```

```python
import math

import jax
import jax.numpy as jnp
from jax import lax
from jax.experimental import pallas as pl
from jax.experimental.pallas import tpu as pltpu

F32 = jnp.float32
BF16 = jnp.bfloat16

D_MODEL = 1024
DEPTH = 2
SSM_HEADS = 16
SSM_HEAD_DIM = 64
SSM_D_INNER = 1024
SSM_STATE = 128
SSM_CONV_DIM = 1536
GDN_HEADS = 8
GDN_HEAD = 128
GDN_QKV_DIM = 3072
CONV_K = 4
CHUNK = 64
FFN_HIDDEN = 2816
EPS = 1e-6
IN_DIM = 8736

ADAM_LR = 0.001
ADAM_B1 = 0.9
ADAM_B2 = 0.999
ADAM_EPS = 1e-08
ADAM_WD = 0.01
ADAM_STEP = 10

LANE = 128
NEG_BIG = -1e30
VMEM_LIMIT = 56 * 1024 * 1024

AL_Z, AL_XBC, AL_QKV, AL_GZ, AL_GS, AL_GG, AL_SMALL, AL_DIM = 0, 1024, 2560, 5632, 6656, 7680, 8704, 9216
SM_DT, SM_A, SM_B = 0, 16, 24

HI = lax.Precision.HIGHEST
NN = (((1,), (0,)), ((), ()))
NT = (((1,), (1,)), ((), ()))
TN = (((0,), (0,)), ((), ()))


def _cparams(sem):
    return pltpu.CompilerParams(dimension_semantics=sem, vmem_limit_bytes=VMEM_LIMIT)


def _pick(n, prefs):
    for p in prefs:
        if n % p == 0:
            return p
    return n


MATMUL_VMEM_BUDGET = 40 << 20
MXU_WIDTH = 256
HBM_BYTES_PER_S = 3.3e12
MXU_FLOPS_PER_S = 9.0e14
GRID_STEP_S = 0.35e-6
N_CHIPS = 4


def _matmul_tiles(m, n, n_dom, k, a_item, b_item, o_item):
    best = None
    def cands(dim, cap):
        return [c for c in range(LANE, min(dim, cap) + 1, LANE) if dim % c == 0] or [dim]

    tms, tns, tks = cands(m, 2048), cands(n_dom, 2304), cands(k, 1 << 30)
    for tm in tms:
        for tn in tns:
            for tk in tks:
                nk = k // tk
                vmem = (2 * (tm * tk * a_item + tk * tn * b_item + tm * tn * o_item) + tm * tn * 4 * (2 if nk > 1 else 1)
                        + (tm * tk * 2 if a_item > 2 else 0) + (tk * tn * 2 if b_item > 2 else 0))
                if vmem > MATMUL_VMEM_BUDGET:
                    continue
                traffic = m * k * a_item * (1 if nk == 1 else n // tn) + k * n * b_item * (m // tm)
                mxu_fill = tn / (-(-tn // MXU_WIDTH) * MXU_WIDTH)
                cost = (max(traffic / HBM_BYTES_PER_S, 2.0 * m * n * k / (MXU_FLOPS_PER_S * mxu_fill))
                        + (m // tm) * (n // tn) * nk * GRID_STEP_S)
                if best is None or cost < best[0]:
                    best = (cost, (tm, tn, tk))
    return best[1]


def _dot(a, b, dims=NN):
    return lax.dot_general(a.astype(BF16), b.astype(BF16), dims, preferred_element_type=F32)


def _dot3(a, b, dims=NN):
    a_hi, b_hi = a.astype(BF16), b.astype(BF16)
    a_lo = (a - a_hi.astype(F32)).astype(BF16)
    b_lo = (b - b_hi.astype(F32)).astype(BF16)

    def dg(u, v):
        return lax.dot_general(u, v, dims, preferred_element_type=F32)

    return dg(a_hi, b_hi) + (dg(a_hi, b_lo) + dg(a_lo, b_hi))


def _dot_hi(a, b, dims=NN):
    return lax.dot_general(a, b, dims, precision=HI, preferred_element_type=F32)


def _sigmoid(x):
    return jax.nn.sigmoid(x)


def _silu(x):
    return x * _sigmoid(x)


def _softplus(x):
    return jnp.maximum(x, 0.0) + jnp.log1p(jnp.exp(-jnp.abs(x)))


def matmul(name, a, b, mode, out_dtype=F32, chip_major=False, stack=None):
    if mode == "nn":
        (m, k), (k2, n) = a.shape, b.shape
    elif mode == "nt":
        (m, k), (n, k2) = a.shape, b.shape
    else:
        (k, m), (k2, n) = a.shape, b.shape
    assert k == k2, (a.shape, b.shape, mode)
    shard = n // N_CHIPS if chip_major else n
    tm, tn, tk = _matmul_tiles(m, n, shard, k, a.dtype.itemsize, b.dtype.itemsize, jnp.dtype(out_dtype).itemsize)
    if chip_major:
        per = shard // tn
        base_shape, base_blk = (N_CHIPS, m, shard), (None, tm, tn)
        base_idx = lambda i, j: (j // per, i, j % per)
    else:
        base_shape, base_blk = (m, n), (tm, tn)
        base_idx = lambda i, j: (i, j)
    nk = k // tk
    dims = {"nn": NN, "nt": NT, "tn": TN}[mode]

    def body_acc(a_ref, b_ref, o_ref, acc_ref):
        kk = pl.program_id(2)

        @pl.when(kk == 0)
        def _():
            acc_ref[...] = jnp.zeros_like(acc_ref)

        acc_ref[...] += _dot(a_ref[...], b_ref[...], dims)

        @pl.when(kk == nk - 1)
        def _():
            o_ref[...] = acc_ref[...].astype(o_ref.dtype)

    def body_one(a_ref, b_ref, o_ref):
        o_ref[...] = _dot(a_ref[...], b_ref[...], dims).astype(o_ref.dtype)

    compute = body_one if nk == 1 else body_acc
    if mode == "tn":
        a_spec = pl.BlockSpec((tk, tm), lambda i, j, kk: (kk, i))
    else:
        a_spec = pl.BlockSpec((tm, tk), lambda i, j, kk: (i, kk))
    if mode == "nt":
        b_spec = pl.BlockSpec((tn, tk), lambda i, j, kk: (j, kk))
    else:
        b_spec = pl.BlockSpec((tk, tn), lambda i, j, kk: (kk, j))
    in_specs, operands, aliases, body = [a_spec, b_spec], [a, b], {}, compute
    if stack is None:
        out_shape, out_blk, out_idx = base_shape, base_blk, (lambda i, j, kk: base_idx(i, j))
    else:
        layer, buf = stack
        out_shape, out_blk = (DEPTH,) + base_shape, (None,) + base_blk
        out_idx = lambda i, j, kk: (layer,) + base_idx(i, j)
        if buf is not None:
            assert buf.shape == out_shape and buf.dtype == out_dtype
            in_specs.append(pl.BlockSpec(memory_space=pl.ANY))
            operands.append(buf)
            aliases = {2: 0}

            def body(a_ref, b_ref, buf_ref, *rest):
                del buf_ref
                compute(a_ref, b_ref, *rest)

    return pl.pallas_call(
        body, name=name,
        out_shape=jax.ShapeDtypeStruct(out_shape, out_dtype),
        grid=(m // tm, n // tn, nk),
        in_specs=in_specs,
        out_specs=pl.BlockSpec(out_blk, out_idx),
        scratch_shapes=[] if nk == 1 else [pltpu.VMEM((tm, tn), F32)],
        input_output_aliases=aliases,
        compiler_params=_cparams(("parallel", "parallel", "arbitrary")),
    )(*operands)


def _row_map(c0, moves):
    return (lambda j, i: (i, c0 + j)) if moves else (lambda j, i: (i, c0))


def _par_map(c0, moves):
    return (lambda j, i: (0, c0 + j)) if moves else (lambda j, i: (0, c0))


def _in_spec(op, tile):
    _, kind, w, c0, moves = op
    if kind == "row":
        return pl.BlockSpec((tile, w), _row_map(c0, moves))
    return pl.BlockSpec((1, w), _par_map(c0, moves))


ROW_BLOCK_ELEMS = 1 << 18


def _row_tile(t, tile, ops):
    width = max(op[2] for op in ops if op[1] == "row")
    return min(t, max(tile, ROW_BLOCK_ELEMS // width))


def rowwise_fwd(name, fn, t, tile, ncol, ins, outs):
    n_in = len(ins)
    tile = _row_tile(t, tile, ins)

    def body(*refs):
        vals = [r[...].astype(F32) for r in refs[:n_in]]
        res = fn(*vals)
        if not isinstance(res, (tuple, list)):
            res = (res,)
        for r, v in zip(refs[n_in:], res):
            r[...] = v.astype(r.dtype)

    res = pl.pallas_call(
        body, name=name,
        out_shape=[jax.ShapeDtypeStruct((t, w * ncol), dt) for w, dt in outs],
        grid=(ncol, t // tile),
        in_specs=[_in_spec(op, tile) for op in ins],
        out_specs=[pl.BlockSpec((tile, w), _row_map(0, True)) for w, _ in outs],
        compiler_params=_cparams(("arbitrary", "arbitrary")),
    )(*[op[0] for op in ins])
    return res


def rowwise_bwd(name, fn, t, tile, ncol, ins, need, cts, addends=None, row_dtypes=None):
    n_in, n_ct = len(ins), len(cts)
    tile = _row_tile(t, tile, ins)
    addends = addends or {}
    row_dtypes = row_dtypes or {}
    didx = [i for i in range(n_in) if need[i]]
    add_ops = [addends[i] for i in didx if i in addends]
    n_add = len(add_ops)

    def body(*refs):
        in_refs = refs[:n_in]
        ct_refs = refs[n_in:n_in + n_ct]
        add_refs = refs[n_in + n_ct:n_in + n_ct + n_add]
        out_refs = refs[n_in + n_ct + n_add:]
        vals = [r[...].astype(F32) for r in in_refs]

        def g(*dv):
            full = list(vals)
            for i, v in zip(didx, dv):
                full[i] = v
            res = fn(*full)
            return tuple(res) if isinstance(res, (tuple, list)) else (res,)

        _, vjp = jax.vjp(g, *[vals[i] for i in didx])
        grads = vjp(tuple(c[...].astype(F32) for c in ct_refs))
        j, i = pl.program_id(0), pl.program_id(1)
        a = 0
        for o_ref, gv, idx in zip(out_refs, grads, didx):
            _, kind, _, _, moves = ins[idx]
            if kind == "row":
                if idx in addends:
                    gv = gv + add_refs[a][...].astype(F32)
                    a += 1
                o_ref[...] = gv.astype(o_ref.dtype)
            else:
                first = (i == 0) if moves else jnp.logical_and(i == 0, j == 0)

                @pl.when(first)
                def _(o_ref=o_ref, gv=gv):
                    o_ref[...] = gv

                @pl.when(jnp.logical_not(first))
                def _(o_ref=o_ref, gv=gv):
                    o_ref[...] += gv

    out_shape, out_specs = [], []
    for idx in didx:
        _, kind, w, _, moves = ins[idx]
        cols = w * (ncol if moves else 1)
        if kind == "row":
            out_shape.append(jax.ShapeDtypeStruct((t, cols), row_dtypes.get(idx, F32)))
            out_specs.append(pl.BlockSpec((tile, w), _row_map(0, moves)))
        else:
            out_shape.append(jax.ShapeDtypeStruct((1, cols), F32))
            out_specs.append(pl.BlockSpec((1, w), _par_map(0, moves)))
    ops = list(ins) + list(cts) + add_ops
    res = pl.pallas_call(
        body, name=name,
        out_shape=out_shape,
        grid=(ncol, t // tile),
        in_specs=[_in_spec(op, tile) for op in ops],
        out_specs=out_specs,
        compiler_params=_cparams(("arbitrary", "arbitrary")),
    )(*[op[0] for op in ops])
    return res


def f_rmsnorm(x, w):
    return x * lax.rsqrt(jnp.mean(x * x, axis=-1, keepdims=True) + EPS) * w


def f_ssd_post(y, z, w):
    y = y * _silu(z)
    return y * lax.rsqrt(jnp.mean(y * y, axis=-1, keepdims=True) + EPS) * w


def f_gdn_post(o, z, w):
    o = o * lax.rsqrt(jnp.mean(o * o, axis=-1, keepdims=True) + EPS) * w
    return o * _silu(z)


def f_merge(gs, p1, gg, p2):
    return _sigmoid(gs) * p1 + _sigmoid(gg) * p2


def f_swiglu(g, u):
    return _silu(g) * u


def final_loss(name, x, tgt, w, tile=256):
    t, d = x.shape

    def body(x_ref, t_ref, w_ref, loss_ref, dx_ref, dw_ref):
        i = pl.program_id(0)
        xv, tv, wv = x_ref[...], t_ref[...], w_ref[...]

        def g(xx, ww):
            err = f_rmsnorm(xx, ww) - tv
            return 0.5 * jnp.sum(jnp.mean(err * err, axis=-1, keepdims=True), axis=0, keepdims=True)

        val, vjp = jax.vjp(g, xv, wv)
        dx, dw = vjp(jnp.ones((1, 1), F32))
        dx_ref[...] = dx
        lv = jnp.broadcast_to(val, (1, LANE))

        @pl.when(i == 0)
        def _():
            loss_ref[...] = lv
            dw_ref[...] = dw

        @pl.when(i != 0)
        def _():
            loss_ref[...] += lv
            dw_ref[...] += dw

    return pl.pallas_call(
        body, name=name,
        out_shape=[jax.ShapeDtypeStruct((1, LANE), F32), jax.ShapeDtypeStruct((t, d), F32),
                   jax.ShapeDtypeStruct((1, d), F32)],
        grid=(t // tile,),
        in_specs=[pl.BlockSpec((tile, d), lambda i: (i, 0)), pl.BlockSpec((tile, d), lambda i: (i, 0)),
                  pl.BlockSpec((1, d), lambda i: (0, 0))],
        out_specs=[pl.BlockSpec((1, LANE), lambda i: (0, 0)), pl.BlockSpec((tile, d), lambda i: (i, 0)),
                   pl.BlockSpec((1, d), lambda i: (0, 0))],
        compiler_params=_cparams(("arbitrary",)),
    )(x, tgt, w)


CONV_W = 512
HALO = 8
STRIPS = 4


def _rows_back(before, cur, d):
    rows = lax.broadcasted_iota(jnp.int32, cur.shape, 0)
    return jnp.where(rows < d, pltpu.roll(before, d, 0), pltpu.roll(cur, d, 0))


def _rows_ahead(cur, after, d):
    rows = lax.broadcasted_iota(jnp.int32, cur.shape, 0)
    return jnp.where(rows < HALO - d, pltpu.roll(cur, HALO - d, 0), pltpu.roll(after, HALO - d, 0))


def _conv_taps(taps, bias, before, cur):
    shifted = [_rows_back(before, cur, CONV_K - 1 - k) for k in range(CONV_K - 1)] + [cur]
    pre = bias + taps[CONV_K - 1] * cur
    for k in range(CONV_K - 1):
        pre = pre + taps[k] * shifted[k]
    return pre, shifted


def conv_fwd(name, src, c0, width, w, b, tile=512):
    t = src.shape[0]
    ncol, nrow = width // CONV_W, t // tile
    cb0 = c0 // CONV_W
    hb = tile // HALO

    def body(prev_ref, cur_ref, w_ref, b_ref, o_ref):
        i = pl.program_id(1)
        taps = [w_ref[k:k + 1, :] for k in range(CONV_K)]
        bias = b_ref[...]

        def strips(g, before):
            for u in range(STRIPS):
                r0 = pl.multiple_of((g * STRIPS + u) * HALO, HALO)
                cur = cur_ref[pl.ds(r0, HALO), :]
                pre, _ = _conv_taps(taps, bias, before, cur)
                o_ref[pl.ds(r0, HALO), :] = _silu(pre)
                before = cur
            return before

        lax.fori_loop(0, tile // (HALO * STRIPS), strips, jnp.where(i == 0, 0.0, prev_ref[...]))

    return pl.pallas_call(
        body, name=name,
        out_shape=jax.ShapeDtypeStruct((t, width), F32),
        grid=(ncol, nrow),
        in_specs=[pl.BlockSpec((HALO, CONV_W), lambda j, i: (jnp.maximum(i * hb - 1, 0), cb0 + j)),
                  pl.BlockSpec((tile, CONV_W), lambda j, i: (i, cb0 + j)),
                  pl.BlockSpec((CONV_K, CONV_W), lambda j, i: (0, j)),
                  pl.BlockSpec((1, CONV_W), lambda j, i: (0, j))],
        out_specs=pl.BlockSpec((tile, CONV_W), lambda j, i: (i, j)),
        compiler_params=_cparams(("arbitrary", "arbitrary")),
    )(src, src, w, b)


def conv_bwd(name, src, c0, width, w, b, dy, tile=512):
    t = src.shape[0]
    ncol, nrow = width // CONV_W, t // tile
    cb0 = c0 // CONV_W
    hb = tile // HALO
    last_hb = t // HALO - 1
    nstrip = tile // HALO

    def body(sprev_ref, scur_ref, snext_ref, w_ref, b_ref, dycur_ref, dynext_ref,
             du_ref, dw_ref, db_ref, dpre_ref):
        i = pl.program_id(1)
        taps = [w_ref[k:k + 1, :] for k in range(CONV_K)]
        bias = b_ref[...]

        def dpre_of(before, cur, dy_strip):
            pre, shifted = _conv_taps(taps, bias, before, cur)
            s = _sigmoid(pre)
            return dy_strip * (s * (1.0 + pre * (1.0 - s))), shifted

        def strips1(g, carry):
            before, dws, dbs = carry
            for u in range(STRIPS):
                r0 = pl.multiple_of((g * STRIPS + u) * HALO, HALO)
                cur = scur_ref[pl.ds(r0, HALO), :]
                dpre, shifted = dpre_of(before, cur, dycur_ref[pl.ds(r0, HALO), :])
                dpre_ref[pl.ds(r0, HALO), :] = dpre
                before, dws, dbs = cur, tuple(a + dpre * v for a, v in zip(dws, shifted)), dbs + dpre
            return before, dws, dbs

        zero = jnp.zeros((HALO, CONV_W), F32)
        before, dws, dbs = lax.fori_loop(0, nstrip // STRIPS, strips1,
                                         (jnp.where(i == 0, 0.0, sprev_ref[...]), (zero,) * CONV_K, zero))
        dpre_next, _ = dpre_of(before, snext_ref[...], dynext_ref[...])
        dpre_ref[pl.ds(tile, HALO), :] = jnp.where(i == nrow - 1, 0.0, dpre_next)

        def strips2(g, _):
            parts = []
            for u in range(STRIPS):
                r0 = pl.multiple_of((g * STRIPS + u) * HALO, HALO)
                cur = dpre_ref[pl.ds(r0, HALO), :]
                after = dpre_ref[pl.ds(r0 + HALO, HALO), :]
                acc = taps[CONV_K - 1] * cur
                for d in range(1, CONV_K):
                    acc = acc + taps[CONV_K - 1 - d] * _rows_ahead(cur, after, d)
                parts.append(acc)
            r0 = pl.multiple_of(g * STRIPS * HALO, STRIPS * HALO)
            du_ref[pl.ds(r0, STRIPS * HALO), :] = jnp.concatenate(parts, axis=0).astype(du_ref.dtype)
            return 0

        lax.fori_loop(0, nstrip // STRIPS, strips2, 0)
        dw_tile = jnp.concatenate([jnp.sum(a, axis=0, keepdims=True) for a in dws], axis=0)
        db_tile = jnp.sum(dbs, axis=0, keepdims=True)
        _acc(dw_ref, dw_tile, i == 0)
        _acc(db_ref, db_tile, i == 0)

    return pl.pallas_call(
        body, name=name,
        out_shape=[jax.ShapeDtypeStruct((t, width), BF16), jax.ShapeDtypeStruct((CONV_K, width), F32),
                   jax.ShapeDtypeStruct((1, width), F32)],
        grid=(ncol, nrow),
        in_specs=[pl.BlockSpec((HALO, CONV_W), lambda j, i: (jnp.maximum(i * hb - 1, 0), cb0 + j)),
                  pl.BlockSpec((tile, CONV_W), lambda j, i: (i, cb0 + j)),
                  pl.BlockSpec((HALO, CONV_W), lambda j, i: (jnp.minimum((i + 1) * hb, last_hb), cb0 + j)),
                  pl.BlockSpec((CONV_K, CONV_W), lambda j, i: (0, j)),
                  pl.BlockSpec((1, CONV_W), lambda j, i: (0, j)),
                  pl.BlockSpec((tile, CONV_W), lambda j, i: (i, j)),
                  pl.BlockSpec((HALO, CONV_W), lambda j, i: (jnp.minimum((i + 1) * hb, last_hb), j))],
        out_specs=[pl.BlockSpec((tile, CONV_W), lambda j, i: (i, j)),
                   pl.BlockSpec((CONV_K, CONV_W), lambda j, i: (0, j)),
                   pl.BlockSpec((1, CONV_W), lambda j, i: (0, j))],
        scratch_shapes=[pltpu.VMEM((tile + HALO, CONV_W), F32)],
        compiler_params=_cparams(("arbitrary", "arbitrary")),
    )(src, src, src, w, b, dy, dy)


def _iota2(q):
    return (lax.broadcasted_iota(jnp.int32, (q, q), 0), lax.broadcasted_iota(jnp.int32, (q, q), 1))


def _lane_pick(blk, idx):
    lane = lax.broadcasted_iota(jnp.int32, (1, LANE), 1)
    return jnp.sum(jnp.where(lane == idx, blk, 0.0), axis=1, keepdims=True)


class _Decay:
    def __init__(self, a):
        q = a.shape[0]
        r, c = _iota2(q)
        self.r, self.c = r, c
        self.cum = _dot_hi((c <= r).astype(F32), a)
        self.cum_t = _dot_hi(a, (r <= c).astype(F32), TN)
        self.tot = self.cum[q - 1:q, :]
        self.e_cum = jnp.exp(self.cum)
        self.e_rest = jnp.exp(self.tot - self.cum)
        self.e_tot = jnp.exp(self.tot)

    def mask(self, lane):
        rows = lax.broadcasted_iota(jnp.int32, (LANE, 1), 0)
        cum_row = jnp.sum(jnp.where(rows == lane, self.cum_t, 0.0), axis=0, keepdims=True)
        return jnp.exp(jnp.where(self.r >= self.c, _lane_pick(self.cum, lane) - cum_row, NEG_BIG))


_SSD_B = SSM_D_INNER
_SSD_C = SSM_D_INNER + 2 * SSM_STATE


def _interleave(gens):
    results = [None] * len(gens)
    live = list(range(len(gens)))
    while live:
        for i in list(live):
            try:
                next(gens[i])
            except StopIteration as stop:
                results[i] = stop.value
                live.remove(i)
    return results


def ssd_chunk(xs, bm, cm, dt_all, dsk, dec, state, p, cb):
    lane = lax.broadcasted_iota(jnp.int32, (1, LANE), 1)
    m0 = lane < SSM_HEAD_DIM
    h0, h1 = 2 * p, 2 * p + 1

    def both(blk):
        return jnp.where(m0, _lane_pick(blk, h0), _lane_pick(blk, h1))

    xdt = xs * both(dt_all)
    l0, l1 = dec.mask(h0), dec.mask(h1)
    yield
    y_diag = _dot(cb * l0, jnp.where(m0, xdt, 0.0)) + _dot(cb * l1, jnp.where(m0, 0.0, xdt))
    y_off = _dot(cm, state, NT) * both(dec.e_cum)
    yield
    rowm = lax.broadcasted_iota(jnp.int32, (LANE, 1), 0) < SSM_HEAD_DIM
    new_state = (state * jnp.where(rowm, _lane_pick(dec.e_tot, h0), _lane_pick(dec.e_tot, h1))
                 + _dot(xdt * both(dec.e_rest), bm, TN))
    y = y_diag + y_off + both(dsk) * xs
    return y, new_state


def ssd_pairs(xs, bms, cms, small, dtb, alog, dsk, states):
    dt_all = _softplus(small + dtb)
    dec = _Decay(dt_all * (-jnp.exp(alog)))
    cbs = [_dot(cm, bm, NT) for cm, bm in zip(cms, bms)]
    res = _interleave([ssd_chunk(x, bms[p // 4], cms[p // 4], dt_all, dsk, dec, st, p, cbs[p // 4])
                       for p, (x, st) in enumerate(zip(xs, states))])
    return tuple(y for y, _ in res), tuple(s for _, s in res)


def tri_inverse(a):
    q = a.shape[0]
    r, c = _iota2(q)
    eye = (r == c).astype(F32)
    diag = (r // 16) == (c // 16)
    bd = jnp.where(diag, a, 0.0)
    off = jnp.where(diag, 0.0, a)
    b2 = _dot3(bd, bd)
    d1 = _dot3(eye - bd, eye + b2)
    yield
    b4 = _dot3(b2, b2)
    yield
    b8 = _dot3(b4, b4)
    d2 = _dot3(d1, eye + b4)
    yield
    dinv = _dot3(d2, eye + b8)
    yield
    n = _dot3(dinv, off)
    yield
    n2 = _dot3(n, n)
    yield
    m = _dot3(eye + n2, dinv)
    yield
    return _dot3(eye - n, m)


@jax.custom_vjp
def _solve_with(xinv, a, rhs):
    del a
    return _dot3(xinv, rhs)


def _solve_with_fwd(xinv, a, rhs):
    t = _dot3(xinv, rhs)
    return t, (xinv, t)


def _solve_with_bwd(res, dt):
    xinv, t = res
    d_rhs = _dot3(xinv, dt, TN)
    d_a = -_dot(d_rhs, t, NT)
    return jnp.zeros_like(xinv), d_a, d_rhs


_solve_with.defvjp(_solve_with_fwd, _solve_with_bwd)


_GDN_K = GDN_HEADS * GDN_HEAD
_GDN_V = 2 * GDN_HEADS * GDN_HEAD


def gdn_chunk(qh, kh, vh, beta_all, dec, state, h, xinv=None):
    r, c = dec.r, dec.c
    qn = qh * lax.rsqrt(jnp.sum(qh * qh, axis=-1, keepdims=True) + EPS) * (GDN_HEAD ** -0.5)
    kn = kh * lax.rsqrt(jnp.sum(kh * kh, axis=-1, keepdims=True) + EPS)
    beta = _lane_pick(beta_all, SM_B + h)
    decay = dec.mask(SM_A + h)
    yield
    kk = _dot(kn, kn, NT)
    qk = _dot(qn, kn, NT) * decay
    amat = jnp.where(r > c, kk * decay * beta, 0.0)
    eg = _lane_pick(dec.e_cum, SM_A + h)
    rhs = jnp.concatenate([vh * beta, kn * (beta * eg)], axis=1)
    qs = _dot(qn * eg, state)
    yield
    if xinv is None:
        xinv = yield from tri_inverse(amat)
        t = _dot3(xinv, rhs)
    else:
        t = _solve_with(xinv, amat, rhs)
    yield
    u, w = t[:, :GDN_HEAD], t[:, GDN_HEAD:]
    v_new = u - _dot(w, state)
    yield
    o = qs + _dot(qk, v_new)
    new_state = (state * _lane_pick(dec.e_tot, SM_A + h)
                 + _dot(kn * _lane_pick(dec.e_rest, SM_A + h), v_new, TN))
    return o, new_state, xinv


def gdn_heads(qs, ks, vs, small, alog, dtb, states, xinvs=None):
    nh = len(qs)
    beta_all = _sigmoid(small)
    dec = _Decay(-jnp.exp(alog) * _softplus(small + dtb))
    res = _interleave([gdn_chunk(qs[h], ks[h], vs[h], beta_all, dec, states[h], h,
                                 None if xinvs is None else xinvs[h]) for h in range(nh)])
    return tuple(o for o, _, _ in res), tuple(s for _, s, _ in res), tuple(x for _, _, x in res)


def _acc(ref, val, first):
    @pl.when(first)
    def _():
        ref[...] = val

    @pl.when(jnp.logical_not(first))
    def _():
        ref[...] += val


def ssd_scan_fwd(name, xbc, proj, dtb, alog, dsk):
    t = xbc.shape[0]
    nc, npair = t // CHUNK, SSM_HEADS // 2
    small_blk = AL_SMALL // LANE

    def body(xbc_ref, sm_ref, dtb_ref, alog_ref, dsk_ref, y_ref, sin_ref, st_ref):
        ci = pl.program_id(0)

        @pl.when(ci == 0)
        def _():
            st_ref[...] = jnp.zeros_like(st_ref)

        s_in = tuple(st_ref[p] for p in range(npair))
        ys, s_new = ssd_pairs(tuple(xbc_ref[:, p * LANE:(p + 1) * LANE] for p in range(npair)),
                              tuple(xbc_ref[:, _SSD_B + g * LANE:_SSD_B + (g + 1) * LANE] for g in range(2)),
                              tuple(xbc_ref[:, _SSD_C + g * LANE:_SSD_C + (g + 1) * LANE] for g in range(2)),
                              sm_ref[...], dtb_ref[...], alog_ref[...], dsk_ref[...], s_in)
        for p in range(npair):
            sin_ref[0, p] = s_in[p]
            y_ref[:, p * LANE:(p + 1) * LANE] = ys[p]
            st_ref[p] = s_new[p]

    par = pl.BlockSpec((1, LANE), lambda ci: (0, 0))
    return pl.pallas_call(
        body, name=name,
        out_shape=[jax.ShapeDtypeStruct((t, SSM_D_INNER), F32),
                   jax.ShapeDtypeStruct((nc, npair, LANE, LANE), F32)],
        grid=(nc,),
        in_specs=[pl.BlockSpec((CHUNK, SSM_CONV_DIM), lambda ci: (ci, 0)),
                  pl.BlockSpec((CHUNK, LANE), lambda ci: (ci, small_blk)),
                  par, par, par],
        out_specs=[pl.BlockSpec((CHUNK, SSM_D_INNER), lambda ci: (ci, 0)),
                   pl.BlockSpec((1, npair, LANE, LANE), lambda ci: (ci, 0, 0, 0))],
        scratch_shapes=[pltpu.VMEM((npair, LANE, LANE), F32)],
        compiler_params=_cparams(("arbitrary",)),
    )(xbc, proj, dtb, alog, dsk)


def ssd_scan_bwd(name, xbc, proj, dtb, alog, dsk, s_in, dy):
    t = xbc.shape[0]
    nc, npair = t // CHUNK, SSM_HEADS // 2
    small_blk = AL_SMALL // LANE

    def body(xbc_ref, sm_ref, dtb_ref, alog_ref, dsk_ref, sin_ref, dy_ref,
             dxbc_ref, dsm_ref, ddtb_ref, dalog_ref, ddsk_ref, dst_ref):
        ci = pl.program_id(0)

        @pl.when(ci == 0)
        def _():
            dst_ref[...] = jnp.zeros_like(dst_ref)

        _, vjp = jax.vjp(ssd_pairs, tuple(xbc_ref[:, p * LANE:(p + 1) * LANE] for p in range(npair)),
                         tuple(xbc_ref[:, _SSD_B + g * LANE:_SSD_B + (g + 1) * LANE] for g in range(2)),
                         tuple(xbc_ref[:, _SSD_C + g * LANE:_SSD_C + (g + 1) * LANE] for g in range(2)),
                         sm_ref[...], dtb_ref[...], alog_ref[...], dsk_ref[...],
                         tuple(sin_ref[0, p] for p in range(npair)))
        dxs, dbms, dcms, dsm, ddtb, dalog, ddsk, dsts = vjp(
            (tuple(dy_ref[:, p * LANE:(p + 1) * LANE] for p in range(npair)),
             tuple(dst_ref[p] for p in range(npair))))
        for p in range(npair):
            dxbc_ref[:, p * LANE:(p + 1) * LANE] = dxs[p]
            dst_ref[p] = dsts[p]
        for g in range(2):
            dxbc_ref[:, _SSD_B + g * LANE:_SSD_B + (g + 1) * LANE] = dbms[g]
            dxbc_ref[:, _SSD_C + g * LANE:_SSD_C + (g + 1) * LANE] = dcms[g]
        dsm_ref[...] = dsm
        _acc(ddtb_ref, ddtb, ci == 0)
        _acc(dalog_ref, dalog, ci == 0)
        _acc(ddsk_ref, ddsk, ci == 0)

    par = pl.BlockSpec((1, LANE), lambda ci: (0, 0))
    rev = lambda ci: nc - 1 - ci
    return pl.pallas_call(
        body, name=name,
        out_shape=[jax.ShapeDtypeStruct((t, SSM_CONV_DIM), F32),
                   jax.ShapeDtypeStruct((t, LANE), F32),
                   jax.ShapeDtypeStruct((1, LANE), F32), jax.ShapeDtypeStruct((1, LANE), F32),
                   jax.ShapeDtypeStruct((1, LANE), F32)],
        grid=(nc,),
        in_specs=[pl.BlockSpec((CHUNK, SSM_CONV_DIM), lambda ci: (rev(ci), 0)),
                  pl.BlockSpec((CHUNK, LANE), lambda ci: (rev(ci), small_blk)),
                  par, par, par,
                  pl.BlockSpec((1, npair, LANE, LANE), lambda ci: (rev(ci), 0, 0, 0)),
                  pl.BlockSpec((CHUNK, SSM_D_INNER), lambda ci: (rev(ci), 0))],
        out_specs=[pl.BlockSpec((CHUNK, SSM_CONV_DIM), lambda ci: (rev(ci), 0)),
                   pl.BlockSpec((CHUNK, LANE), lambda ci: (rev(ci), 0)),
                   par, par, par],
        scratch_shapes=[pltpu.VMEM((npair, LANE, LANE), F32)],
        compiler_params=_cparams(("arbitrary",)),
    )(xbc, proj, dtb, alog, dsk, s_in, dy)


def gdn_scan_fwd(name, qkv, proj, alog, dtb):
    t = qkv.shape[0]
    nc, nh = t // CHUNK, GDN_HEADS
    small_blk = AL_SMALL // LANE

    def body(qkv_ref, sm_ref, alog_ref, dtb_ref, o_ref, sin_ref, x_ref, st_ref):
        ci = pl.program_id(0)

        @pl.when(ci == 0)
        def _():
            st_ref[...] = jnp.zeros_like(st_ref)

        s_in = tuple(st_ref[h] for h in range(nh))
        os, s_new, xinvs = gdn_heads(
            tuple(qkv_ref[:, h * LANE:(h + 1) * LANE] for h in range(nh)),
            tuple(qkv_ref[:, _GDN_K + h * LANE:_GDN_K + (h + 1) * LANE] for h in range(nh)),
            tuple(qkv_ref[:, _GDN_V + h * LANE:_GDN_V + (h + 1) * LANE] for h in range(nh)),
            sm_ref[...], alog_ref[...], dtb_ref[...], s_in)
        for h in range(nh):
            sin_ref[0, h] = s_in[h]
            o_ref[:, h * LANE:(h + 1) * LANE] = os[h]
            x_ref[0, h] = xinvs[h]
            st_ref[h] = s_new[h]

    par = pl.BlockSpec((1, LANE), lambda ci: (0, 0))
    return pl.pallas_call(
        body, name=name,
        out_shape=[jax.ShapeDtypeStruct((t, GDN_HEADS * GDN_HEAD), F32),
                   jax.ShapeDtypeStruct((nc, nh, LANE, LANE), F32),
                   jax.ShapeDtypeStruct((nc, nh, CHUNK, CHUNK), F32)],
        grid=(nc,),
        in_specs=[pl.BlockSpec((CHUNK, GDN_QKV_DIM), lambda ci: (ci, 0)),
                  pl.BlockSpec((CHUNK, LANE), lambda ci: (ci, small_blk)),
                  par, par],
        out_specs=[pl.BlockSpec((CHUNK, GDN_HEADS * GDN_HEAD), lambda ci: (ci, 0)),
                   pl.BlockSpec((1, nh, LANE, LANE), lambda ci: (ci, 0, 0, 0)),
                   pl.BlockSpec((1, nh, CHUNK, CHUNK), lambda ci: (ci, 0, 0, 0))],
        scratch_shapes=[pltpu.VMEM((nh, LANE, LANE), F32)],
        compiler_params=_cparams(("arbitrary",)),
    )(qkv, proj, alog, dtb)


def gdn_scan_bwd(name, qkv, proj, alog, dtb, s_in, xinv, do, dsm_in):
    t = qkv.shape[0]
    nc, nh = t // CHUNK, GDN_HEADS
    small_blk = AL_SMALL // LANE

    def body(qkv_ref, sm_ref, alog_ref, dtb_ref, sin_ref, x_ref, do_ref, dsmi_ref,
             dqkv_ref, dsm_ref, dalog_ref, ddtb_ref, dst_ref):
        ci = pl.program_id(0)

        @pl.when(ci == 0)
        def _():
            dst_ref[...] = jnp.zeros_like(dst_ref)

        xis = tuple(x_ref[0, h] for h in range(nh))

        def fn(qs, ks, vs, sm, alog_, dtb_, sts):
            os, s_new, _ = gdn_heads(qs, ks, vs, sm, alog_, dtb_, sts, xinvs=xis)
            return os, s_new

        _, vjp = jax.vjp(fn, tuple(qkv_ref[:, h * LANE:(h + 1) * LANE] for h in range(nh)),
                         tuple(qkv_ref[:, _GDN_K + h * LANE:_GDN_K + (h + 1) * LANE] for h in range(nh)),
                         tuple(qkv_ref[:, _GDN_V + h * LANE:_GDN_V + (h + 1) * LANE] for h in range(nh)),
                         sm_ref[...], alog_ref[...], dtb_ref[...], tuple(sin_ref[0, h] for h in range(nh)))
        dqs, dks, dvs, dsm, dalog, ddtb, dsts = vjp(
            (tuple(do_ref[:, h * LANE:(h + 1) * LANE] for h in range(nh)), tuple(dst_ref[h] for h in range(nh))))
        for h in range(nh):
            dqkv_ref[:, h * LANE:(h + 1) * LANE] = dqs[h]
            dqkv_ref[:, _GDN_K + h * LANE:_GDN_K + (h + 1) * LANE] = dks[h]
            dqkv_ref[:, _GDN_V + h * LANE:_GDN_V + (h + 1) * LANE] = dvs[h]
            dst_ref[h] = dsts[h]
        dsm_ref[...] = dsmi_ref[...] + dsm
        _acc(dalog_ref, dalog, ci == 0)
        _acc(ddtb_ref, ddtb, ci == 0)

    par = pl.BlockSpec((1, LANE), lambda ci: (0, 0))
    rev = lambda ci: nc - 1 - ci
    return pl.pallas_call(
        body, name=name,
        out_shape=[jax.ShapeDtypeStruct((t, GDN_QKV_DIM), F32), jax.ShapeDtypeStruct((t, LANE), F32),
                   jax.ShapeDtypeStruct((1, LANE), F32), jax.ShapeDtypeStruct((1, LANE), F32)],
        grid=(nc,),
        in_specs=[pl.BlockSpec((CHUNK, GDN_QKV_DIM), lambda ci: (rev(ci), 0)),
                  pl.BlockSpec((CHUNK, LANE), lambda ci: (rev(ci), small_blk)),
                  par, par,
                  pl.BlockSpec((1, nh, LANE, LANE), lambda ci: (rev(ci), 0, 0, 0)),
                  pl.BlockSpec((1, nh, CHUNK, CHUNK), lambda ci: (rev(ci), 0, 0, 0)),
                  pl.BlockSpec((CHUNK, GDN_HEADS * GDN_HEAD), lambda ci: (rev(ci), 0)),
                  pl.BlockSpec((CHUNK, LANE), lambda ci: (rev(ci), 0))],
        out_specs=[pl.BlockSpec((CHUNK, GDN_QKV_DIM), lambda ci: (rev(ci), 0)),
                   pl.BlockSpec((CHUNK, LANE), lambda ci: (rev(ci), 0)),
                   par, par],
        scratch_shapes=[pltpu.VMEM((nh, LANE, LANE), F32)],
        compiler_params=_cparams(("arbitrary",)),
    )(qkv, proj, alog, dtb, s_in, xinv, do, dsm_in)


def _row(arr, w, c0=0, moves=False):
    return (arr, "row", w, c0, moves)


def _par(arr, w, c0=0, moves=False):
    return (arr, "par", w, c0, moves)


def matmul_add(name, a, b, res):
    (m, k), (_, n) = a.shape, b.shape
    tm, tn, tk = _matmul_tiles(m, n, n, k, a.dtype.itemsize, b.dtype.itemsize, 4 + res.dtype.itemsize)
    nk = k // tk

    def body_acc(a_ref, b_ref, r_ref, o_ref, acc_ref):
        kk = pl.program_id(2)

        @pl.when(kk == 0)
        def _():
            acc_ref[...] = r_ref[...]

        acc_ref[...] += _dot(a_ref[...], b_ref[...])

        @pl.when(kk == nk - 1)
        def _():
            o_ref[...] = acc_ref[...]

    def body_one(a_ref, b_ref, r_ref, o_ref):
        o_ref[...] = r_ref[...] + _dot(a_ref[...], b_ref[...])

    body = body_one if nk == 1 else body_acc
    return pl.pallas_call(
        body, name=name,
        out_shape=jax.ShapeDtypeStruct((m, n), F32),
        grid=(m // tm, n // tn, nk),
        in_specs=[pl.BlockSpec((tm, tk), lambda i, j, kk: (i, kk)),
                  pl.BlockSpec((tk, tn), lambda i, j, kk: (kk, j)),
                  pl.BlockSpec((tm, tn), lambda i, j, kk: (i, j))],
        out_specs=pl.BlockSpec((tm, tn), lambda i, j, kk: (i, j)),
        scratch_shapes=[] if nk == 1 else [pltpu.VMEM((tm, tn), F32)],
        compiler_params=_cparams(("parallel", "parallel", "arbitrary")),
    )(a, b, res)


def layer_fwd(l, x, w):
    t = x.shape[0]
    rt = min(256, t)
    s = {"x": x}
    s["h"] = rowwise_fwd(f"norm_mix_l{l}", f_rmsnorm, t, rt, 1,
                         [_row(x, D_MODEL), _par(w["norm_mix_w"], D_MODEL)], [(D_MODEL, BF16)])[0]
    s["proj"] = matmul(f"in_proj_l{l}", s["h"], w["w_in"], "nn")
    s["xbc"] = conv_fwd(f"ssm_conv_l{l}", s["proj"], AL_XBC, SSM_CONV_DIM, w["ssm_conv_w"], w["ssm_conv_b"],
                        tile=min(512, t))
    s["qkv"] = conv_fwd(f"gdn_conv_l{l}", s["proj"], AL_QKV, GDN_QKV_DIM, w["gdn_conv_w"], w["gdn_conv_b"],
                        tile=min(512, t))
    s["y_scan"], s["ssd_sin"] = ssd_scan_fwd(f"ssd_scan_l{l}", s["xbc"], s["proj"], w["ssm_dt_bias"],
                                             w["ssm_a_log"], w["ssm_d"])
    s["o_scan"], s["gdn_sin"], s["gdn_x"] = gdn_scan_fwd(f"gdn_scan_l{l}", s["qkv"], s["proj"],
                                                         w["gdn_a_log"], w["gdn_dt_bias"])
    s["y_ssm"] = rowwise_fwd(f"ssd_post_l{l}", f_ssd_post, t, rt, 2,
                             [_row(s["y_scan"], 512, 0, True), _row(s["proj"], 512, AL_Z // 512, True),
                              _par(w["ssm_norm_w"], 512, 0, True)], [(512, BF16)])[0]
    s["y_gdn"] = rowwise_fwd(f"gdn_post_l{l}", f_gdn_post, t, rt, GDN_HEADS,
                             [_row(s["o_scan"], LANE, 0, True), _row(s["proj"], LANE, AL_GZ // LANE, True),
                              _par(w["gdn_norm_w"], LANE)], [(LANE, BF16)])[0]
    s["p1"] = matmul(f"proj_ssm_l{l}", s["y_ssm"], w["w_proj_ssm"], "nn")
    s["p2"] = matmul(f"proj_gdn_l{l}", s["y_gdn"], w["w_proj_gdn"], "nn")
    s["merged"] = rowwise_fwd(f"merge_l{l}", f_merge, t, rt, 2,
                              [_row(s["proj"], 512, AL_GS // 512, True), _row(s["p1"], 512, 0, True),
                               _row(s["proj"], 512, AL_GG // 512, True), _row(s["p2"], 512, 0, True)],
                              [(512, BF16)])[0]
    s["x1"] = matmul_add(f"out_proj_l{l}", s["merged"], w["w_out"], x)
    s["h2"] = rowwise_fwd(f"norm_ffn_l{l}", f_rmsnorm, t, rt, 1,
                          [_row(s["x1"], D_MODEL), _par(w["norm_ffn_w"], D_MODEL)], [(D_MODEL, BF16)])[0]
    s["gu"] = matmul(f"ffn_in_l{l}", s["h2"], w["w_ffn_in"], "nn")
    s["act"] = rowwise_fwd(f"swiglu_l{l}", f_swiglu, t, rt, FFN_HIDDEN // 256,
                           [_row(s["gu"], 256, 0, True), _row(s["gu"], 256, FFN_HIDDEN // 256, True)],
                           [(256, BF16)])[0]
    x2 = matmul_add(f"ffn_down_l{l}", s["act"], w["w_ffn_down"], s["x1"])
    return x2, s


IN_SHARD = IN_DIM // 4
IN_SHARD_PAD = 2304


def _aligned_to_shards(g):
    orig = jnp.concatenate([g[:, 0:2560], g[:, AL_SMALL:AL_SMALL + 16], g[:, 2560:6656],
                            g[:, AL_SMALL + 16:AL_SMALL + 32], g[:, 6656:8704]], axis=1)
    return jnp.stack([jnp.pad(orig[:, j * IN_SHARD:(j + 1) * IN_SHARD], ((0, 0), (0, IN_SHARD_PAD - IN_SHARD)))
                      for j in range(N_CHIPS)])


def layer_bwd(l, dx2, w, s, gbuf):
    t = dx2.shape[0]
    rt = min(256, t)
    ct = min(512, t)
    g = {}
    dact = matmul(f"ffn_down_dx_l{l}", dx2, w["w_ffn_down"], "nt")
    g["w_ffn_down"] = matmul(f"ffn_down_dw_l{l}", s["act"], dx2, "tn", stack=(l, gbuf.get("w_ffn_down")))
    nf = FFN_HIDDEN // 256
    dgate, dup = rowwise_bwd(f"swiglu_bwd_l{l}", f_swiglu, t, rt, nf,
                             [_row(s["gu"], 256, 0, True), _row(s["gu"], 256, nf, True)], [True, True],
                             [_row(dact, 256, 0, True)], row_dtypes={0: BF16, 1: BF16})
    dgu = jnp.concatenate([dgate, dup], axis=1)
    dh2 = matmul(f"ffn_in_dx_l{l}", dgu, w["w_ffn_in"], "nt")
    g["w_ffn_in"] = matmul(f"ffn_in_dw_l{l}", s["h2"], dgu, "tn", chip_major=True,
                           stack=(l, gbuf.get("w_ffn_in")))
    dx1, g["norm_ffn_w"] = rowwise_bwd(f"norm_ffn_bwd_l{l}", f_rmsnorm, t, rt, 1,
                                       [_row(s["x1"], D_MODEL), _par(w["norm_ffn_w"], D_MODEL)], [True, True],
                                       [_row(dh2, D_MODEL)], addends={0: _row(dx2, D_MODEL)})
    dmerged = matmul(f"out_proj_dx_l{l}", dx1, w["w_out"], "nt")
    g["w_out"] = matmul(f"out_proj_dw_l{l}", s["merged"], dx1, "tn", stack=(l, gbuf.get("w_out")))
    dgs, dp1, dgg, dp2 = rowwise_bwd(
        f"merge_bwd_l{l}", f_merge, t, rt, 2,
        [_row(s["proj"], 512, AL_GS // 512, True), _row(s["p1"], 512, 0, True),
         _row(s["proj"], 512, AL_GG // 512, True), _row(s["p2"], 512, 0, True)], [True] * 4,
        [_row(dmerged, 512, 0, True)], row_dtypes={0: BF16, 1: BF16, 2: BF16, 3: BF16})
    dy_ssm = matmul(f"proj_ssm_dx_l{l}", dp1, w["w_proj_ssm"], "nt")
    g["w_proj_ssm"] = matmul(f"proj_ssm_dw_l{l}", s["y_ssm"], dp1, "tn", stack=(l, gbuf.get("w_proj_ssm")))
    dy_gdn = matmul(f"proj_gdn_dx_l{l}", dp2, w["w_proj_gdn"], "nt")
    g["w_proj_gdn"] = matmul(f"proj_gdn_dw_l{l}", s["y_gdn"], dp2, "tn", stack=(l, gbuf.get("w_proj_gdn")))
    dy_scan, dz, g["ssm_norm_w"] = rowwise_bwd(
        f"ssd_post_bwd_l{l}", f_ssd_post, t, rt, 2,
        [_row(s["y_scan"], 512, 0, True), _row(s["proj"], 512, AL_Z // 512, True),
         _par(w["ssm_norm_w"], 512, 0, True)], [True] * 3, [_row(dy_ssm, 512, 0, True)], row_dtypes={1: BF16})
    dxbc_act, dsm, g["ssm_dt_bias"], g["ssm_a_log"], g["ssm_d"] = ssd_scan_bwd(
        f"ssd_scan_bwd_l{l}", s["xbc"], s["proj"], w["ssm_dt_bias"], w["ssm_a_log"], w["ssm_d"],
        s["ssd_sin"], dy_scan)
    dxbc, g["ssm_conv_w"], g["ssm_conv_b"] = conv_bwd(
        f"ssm_conv_bwd_l{l}", s["proj"], AL_XBC, SSM_CONV_DIM, w["ssm_conv_w"], w["ssm_conv_b"], dxbc_act, tile=ct)
    do_scan, dgz, g["gdn_norm_w"] = rowwise_bwd(
        f"gdn_post_bwd_l{l}", f_gdn_post, t, rt, GDN_HEADS,
        [_row(s["o_scan"], LANE, 0, True), _row(s["proj"], LANE, AL_GZ // LANE, True),
         _par(w["gdn_norm_w"], LANE)], [True] * 3, [_row(dy_gdn, LANE, 0, True)], row_dtypes={1: BF16})
    dqkv_act, dsm, g["gdn_a_log"], g["gdn_dt_bias"] = gdn_scan_bwd(
        f"gdn_scan_bwd_l{l}", s["qkv"], s["proj"], w["gdn_a_log"], w["gdn_dt_bias"], s["gdn_sin"],
        s["gdn_x"], do_scan, dsm)
    dqkv, g["gdn_conv_w"], _ = conv_bwd(
        f"gdn_conv_bwd_l{l}", s["proj"], AL_QKV, GDN_QKV_DIM, w["gdn_conv_w"], w["gdn_conv_b"], dqkv_act, tile=ct)
    dproj = jnp.concatenate([dz, dxbc, dqkv, dgz, dgs, dgg, dsm.astype(BF16),
                             jnp.zeros((t, AL_DIM - AL_SMALL - LANE), BF16)], axis=1)
    dh = matmul(f"in_proj_dx_l{l}", dproj, w["w_in"], "nt")
    g["w_in"] = matmul(f"in_proj_dw_l{l}", s["h"], dproj, "tn", stack=(l, gbuf.get("w_in")))
    dx0, g["norm_mix_w"] = rowwise_bwd(f"norm_mix_bwd_l{l}", f_rmsnorm, t, rt, 1,
                                       [_row(s["x"], D_MODEL), _par(w["norm_mix_w"], D_MODEL)], [True, True],
                                       [_row(dh, D_MODEL)], addends={0: _row(dx1, D_MODEL)})
    return dx0, g


def _align_w_in(w):
    pad = jnp.zeros((w.shape[0], AL_DIM - AL_SMALL - 32), w.dtype)
    return jnp.concatenate([w[:, 0:2560], w[:, 2576:6672], w[:, 6688:8736],
                            w[:, 2560:2576], w[:, 6672:6688], pad], axis=1)


def _pad_lane(v, at=0):
    return jnp.pad(v[None], ((0, 0), (at, LANE - at - v.shape[0])))


def local_step(x, target, full):
    ws = []
    for l in range(DEPTH):
        ws.append({
            "norm_mix_w": full["norm_mix_w"][l][None], "w_in": _align_w_in(full["w_in"][l]),
            "ssm_conv_w": full["ssm_conv_w"][l], "ssm_conv_b": full["ssm_conv_b"][l][None],
            "ssm_dt_bias": _pad_lane(full["ssm_dt_bias"][l]), "ssm_a_log": _pad_lane(full["ssm_a_log"][l]),
            "ssm_d": _pad_lane(full["ssm_d"][l]), "ssm_norm_w": full["ssm_norm_w"][l][None],
            "gdn_conv_w": full["gdn_conv_w"][l], "gdn_conv_b": jnp.zeros((1, GDN_QKV_DIM), F32),
            "gdn_a_log": _pad_lane(full["gdn_a_log"][l], SM_A),
            "gdn_dt_bias": _pad_lane(full["gdn_dt_bias"][l], SM_A),
            "gdn_norm_w": full["gdn_norm_w"][l][None],
            "w_proj_ssm": full["w_proj_ssm"][l], "w_proj_gdn": full["w_proj_gdn"][l], "w_out": full["w_out"][l],
            "norm_ffn_w": full["norm_ffn_w"][l][None], "w_ffn_in": full["w_ffn_in"][l],
            "w_ffn_down": full["w_ffn_down"][l],
        })
    saved = []
    h = x
    for l in range(DEPTH):
        h, s = layer_fwd(l, h, ws[l])
        saved.append(s)
    loss, dx, g_final = final_loss("final_loss", h, target, full["final_norm_w"][None], tile=min(256, x.shape[0]))
    per_layer = [None] * DEPTH
    gbuf = {}
    for l in reversed(range(DEPTH)):
        dx, per_layer[l] = layer_bwd(l, dx, ws[l], saved[l], gbuf)
        gbuf = {n: per_layer[l][n] for n, _ in BIG}
    grads = {"final_norm_w": g_final[0], **gbuf}
    for name in per_layer[0]:
        if name in gbuf:
            continue
        rows = []
        for l in range(DEPTH):
            gl = per_layer[l][name]
            if name in ("ssm_dt_bias", "ssm_a_log", "ssm_d"):
                gl = gl[0, :SSM_HEADS]
            elif name in ("gdn_a_log", "gdn_dt_bias"):
                gl = gl[0, SM_A:SM_A + GDN_HEADS]
            elif name in ("norm_mix_w", "ssm_conv_b", "ssm_norm_w", "gdn_norm_w", "norm_ffn_w"):
                gl = gl[0]
            rows.append(gl)
        grads[name] = jnp.stack(rows)
    return loss, dx, grads


MESH = pl.DeviceIdType.MESH
HBM = pl.BlockSpec(memory_space=pltpu.HBM)
N_DEV = 8


def _pos():
    return lax.axis_index("x"), lax.axis_index("y"), lax.axis_index("c")


def _rcopy(src, dst, send_sem, recv_sem, dev):
    return pltpu.make_async_remote_copy(src_ref=src, dst_ref=dst, send_sem=send_sem, recv_sem=recv_sem,
                                        device_id=dev, device_id_type=MESH)


RELATIONS = (2, 1, 3)


def _related_chip(x, y, mask):
    return (1 - x if mask & 2 else x, 1 - y if mask & 1 else y)


def weights_gather(name, bufs):
    n = len(bufs)

    def body(*refs):
        outs, send_sems, recv_sems = refs[n:2 * n], refs[2 * n], refs[2 * n + 1]
        x, y, c = _pos()
        sib = (x, y, 1 - c)
        sends = []
        for i, a in enumerate(outs):
            for k, m in enumerate(RELATIONS):
                px, py = _related_chip(x, y, m)
                cp = _rcopy(a.at[0, c], a.at[m, c], send_sems.at[6 * i + k], recv_sems.at[6 * i + k], (px, py, c))
                cp.start()
                sends.append(cp)
        for i, a in enumerate(outs):
            for k, m in enumerate(RELATIONS):
                px, py = _related_chip(x, y, m)
                _rcopy(a.at[0, c], a.at[m, c], send_sems.at[6 * i + k], recv_sems.at[6 * i + k],
                       (px, py, c)).wait_recv()
                fw = _rcopy(a.at[m, c], a.at[m, c], send_sems.at[6 * i + 3 + k], recv_sems.at[6 * i + 3 + k], sib)
                fw.start()
                sends.append(fw)
        for i, a in enumerate(outs):
            for k, m in enumerate(RELATIONS):
                _rcopy(a.at[m, 1 - c], a.at[m, 1 - c], send_sems.at[6 * i + 3 + k], recv_sems.at[6 * i + 3 + k],
                       sib).wait_recv()
        for cp in sends:
            cp.wait_send()

    return pl.pallas_call(
        body, name=name, out_shape=[jax.ShapeDtypeStruct(b.shape, b.dtype) for b in bufs],
        in_specs=[HBM] * n, out_specs=[HBM] * n,
        input_output_aliases={i: i for i in range(n)},
        scratch_shapes=[pltpu.SemaphoreType.DMA((6 * n,)), pltpu.SemaphoreType.DMA((6 * n,))],
    )(*bufs)


def pair_swap(name, gs):
    n = len(gs)

    def body(*refs):
        srcs, outs, send_sems, recv_sems = refs[:n], refs[n:2 * n], refs[2 * n], refs[2 * n + 1]
        x, y, c = _pos()
        cps = [_rcopy(s.at[1 - c], o, send_sems.at[i], recv_sems.at[i], (x, y, 1 - c))
               for i, (s, o) in enumerate(zip(srcs, outs))]
        for cp in cps:
            cp.start()
        for cp in cps:
            cp.wait()

    return pl.pallas_call(
        body, name=name, out_shape=[jax.ShapeDtypeStruct(g.shape[1:], g.dtype) for g in gs],
        in_specs=[HBM] * n, out_specs=[HBM] * n,
        scratch_shapes=[pltpu.SemaphoreType.DMA((n,)), pltpu.SemaphoreType.DMA((n,))],
    )(*gs)


def chip_scatter(name, ss):
    n = len(ss)

    def body(*refs):
        srcs, outs, send_sems, recv_sems = refs[:n], refs[n:2 * n], refs[2 * n], refs[2 * n + 1]
        x, y, c = _pos()
        sends = []
        for i, (s, o) in enumerate(zip(srcs, outs)):
            for k, m in enumerate(RELATIONS):
                px, py = _related_chip(x, y, m)
                cp = _rcopy(s.at[2 * px + py], o.at[k], send_sems.at[3 * i + k], recv_sems.at[3 * i + k],
                            (px, py, c))
                cp.start()
                sends.append(cp)
        for cp in sends:
            cp.wait()

    return pl.pallas_call(
        body, name=name, out_shape=[jax.ShapeDtypeStruct((3,) + s.shape[1:], s.dtype) for s in ss],
        in_specs=[HBM] * n, out_specs=[HBM] * n,
        scratch_shapes=[pltpu.SemaphoreType.DMA((3 * n,)), pltpu.SemaphoreType.DMA((3 * n,))],
    )(*ss)


def pair_share(name, bufs):
    n = len(bufs)

    def body(*refs):
        outs, send_sems, recv_sems = refs[n:2 * n], refs[2 * n], refs[2 * n + 1]
        x, y, c = _pos()
        sends = []
        for i, o in enumerate(outs):
            cp = _rcopy(o.at[c], o.at[c], send_sems.at[i], recv_sems.at[i], (x, y, 1 - c))
            cp.start()
            sends.append(cp)
        for i, o in enumerate(outs):
            _rcopy(o.at[1 - c], o.at[1 - c], send_sems.at[i], recv_sems.at[i], (x, y, 1 - c)).wait_recv()
        for cp in sends:
            cp.wait_send()

    return pl.pallas_call(
        body, name=name, out_shape=[jax.ShapeDtypeStruct(b.shape, b.dtype) for b in bufs],
        in_specs=[HBM] * n, out_specs=[HBM] * n,
        input_output_aliases={i: i for i in range(n)},
        scratch_shapes=[pltpu.SemaphoreType.DMA((n,)), pltpu.SemaphoreType.DMA((n,))],
    )(*bufs)


def all_allgather(name, buf):
    r, cd = buf.shape

    def body(src, out, send_sems, recv_sems, lsem):
        x, y, c = _pos()
        me = 4 * x + 2 * y + c
        local = pltpu.make_async_copy(src, out.at[me], lsem)
        local.start()

        def peer(mask):
            px = 1 - x if mask & 4 else x
            py = 1 - y if mask & 2 else y
            pc = 1 - c if mask & 1 else c
            return px, py, pc

        sends = []
        for mask in range(1, N_DEV):
            cp = _rcopy(src, out.at[me], send_sems.at[mask - 1], recv_sems.at[mask - 1], peer(mask))
            cp.start()
            sends.append(cp)
        for mask in range(1, N_DEV):
            px, py, pc = peer(mask)
            _rcopy(src, out.at[4 * px + 2 * py + pc], send_sems.at[mask - 1], recv_sems.at[mask - 1],
                   (px, py, pc)).wait_recv()
        for cp in sends:
            cp.wait_send()
        local.wait()

    return pl.pallas_call(
        body, name=name, out_shape=jax.ShapeDtypeStruct((N_DEV, r, cd), buf.dtype),
        in_specs=[HBM], out_specs=HBM,
        scratch_shapes=[pltpu.SemaphoreType.DMA((N_DEV - 1,)), pltpu.SemaphoreType.DMA((N_DEV - 1,)),
                        pltpu.SemaphoreType.DMA(())],
    )(buf)


ELEMENTWISE_BLOCK_BYTES = 2 << 20


def _row_block(rows, cols):
    for cand in (1024, 512, 256, 128, 64, 32, 16):
        if rows % cand == 0 and cand * cols * 4 <= ELEMENTWISE_BLOCK_BYTES:
            return cand
    return rows


def chip_sum(name, s, r, me, c):
    _, a, b = s.shape
    tr = _row_block(a, b)

    def body(idx_ref, s_ref, r_ref, o_ref):
        del idx_ref
        acc = s_ref[...].astype(F32)
        for k in range(3):
            acc = acc + r_ref[k].astype(F32)
        o_ref[...] = acc

    return pl.pallas_call(
        body, name=name, out_shape=jax.ShapeDtypeStruct((2, a, b), F32),
        grid_spec=pltpu.PrefetchScalarGridSpec(
            num_scalar_prefetch=1, grid=(a // tr,),
            in_specs=[pl.BlockSpec((None, tr, b), lambda i, idx: (idx[0], i, 0)),
                      pl.BlockSpec((3, tr, b), lambda i, idx: (0, i, 0))],
            out_specs=pl.BlockSpec((None, tr, b), lambda i, idx: (idx[1], i, 0))),
        compiler_params=_cparams(("arbitrary",)),
    )(jnp.stack([me, c]).astype(jnp.int32), s, r)


def pair_add(name, p, recv, c):
    _, nj, rh, cd = p.shape
    tr = _row_block(rh, cd)

    def body(c_ref, p_ref, r_ref, o_ref):
        del c_ref
        o_ref[...] = (p_ref[0] + r_ref[...]).astype(o_ref.dtype)

    return pl.pallas_call(
        body, name=name, out_shape=jax.ShapeDtypeStruct((nj, rh, cd), BF16),
        grid_spec=pltpu.PrefetchScalarGridSpec(
            num_scalar_prefetch=1, grid=(nj, rh // tr),
            in_specs=[pl.BlockSpec((1, 1, tr, cd), lambda j, i, c_ref: (c_ref[0], j, i, 0)),
                      pl.BlockSpec((1, tr, cd), lambda j, i, c_ref: (j, i, 0))],
            out_specs=pl.BlockSpec((1, tr, cd), lambda j, i, c_ref: (j, i, 0))),
        compiler_params=_cparams(("arbitrary", "arbitrary")),
    )(jnp.reshape(c, (1,)).astype(jnp.int32), p, recv)


def slab_sum(name, a):
    n, r, cd = a.shape
    tr = _pick(r, (256, 128, 64, 32, 16, 8))

    def body(a_ref, o_ref):
        acc = a_ref[0].astype(F32)
        for j in range(1, n):
            acc = acc + a_ref[j].astype(F32)
        o_ref[...] = acc

    return pl.pallas_call(
        body, name=name, out_shape=jax.ShapeDtypeStruct((r, cd), F32),
        grid=(r // tr,),
        in_specs=[pl.BlockSpec((n, tr, cd), lambda i: (0, i, 0))],
        out_specs=pl.BlockSpec((tr, cd), lambda i: (i, 0)),
        compiler_params=_cparams(("arbitrary",)),
    )(a)


ADAM_C1 = 1.0 - ADAM_B1 ** ADAM_STEP
ADAM_C2 = 1.0 - ADAM_B2 ** ADAM_STEP


def adamw(name, w, g, m, v):
    r, cd = w.shape
    tr = r
    for cand in (512, 256, 128, 64, 32, 16, 8):
        if r % cand == 0 and cand * cd * 4 <= (1 << 20):
            tr = cand
            break

    def body(w_ref, g_ref, m_ref, v_ref, d_ref, nm_ref, nv_ref):
        gv = g_ref[...]
        nm = ADAM_B1 * m_ref[...] + (1.0 - ADAM_B1) * gv
        nv = ADAM_B2 * v_ref[...] + (1.0 - ADAM_B2) * (gv * gv)
        m_hat = nm / ADAM_C1
        v_hat = nv / ADAM_C2
        d_ref[...] = -ADAM_LR * (m_hat / (jnp.sqrt(v_hat) + ADAM_EPS) + ADAM_WD * w_ref[...])
        nm_ref[...] = nm
        nv_ref[...] = nv

    spec = pl.BlockSpec((tr, cd), lambda i: (i, 0))
    sd = jax.ShapeDtypeStruct((r, cd), F32)
    return pl.pallas_call(
        body, name=name, out_shape=[sd, sd, sd], grid=(r // tr,),
        in_specs=[spec] * 4, out_specs=[spec] * 3,
        compiler_params=_cparams(("arbitrary",)),
    )(w, g, m, v)


WEIGHTS = ("norm_mix_w", "w_in", "ssm_conv_w", "ssm_conv_b", "ssm_dt_bias", "ssm_a_log", "ssm_d", "ssm_norm_w",
           "gdn_conv_w", "gdn_a_log", "gdn_dt_bias", "gdn_norm_w", "w_proj_ssm", "w_proj_gdn", "w_out",
           "norm_ffn_w", "w_ffn_in", "w_ffn_down", "final_norm_w")
BIG = (("w_in", 2), ("w_proj_ssm", 1), ("w_proj_gdn", 1), ("w_out", 1), ("w_ffn_in", 2), ("w_ffn_down", 1))
CONVW = (("ssm_conv_w", 2), ("gdn_conv_w", 2))
SHARDED = BIG + CONVW
SMALL = tuple(n for n in WEIGHTS if n not in dict(SHARDED))


def _unpack(buf, shapes, lead=()):
    flat = buf.reshape(lead + (-1,))
    out, o = [], 0
    for shp in shapes:
        n = math.prod(shp)
        out.append(flat[..., o:o + n].reshape(lead + tuple(shp)))
        o += n
    return out


def _pack_rows(arrs, lead=()):
    nl = len(lead)
    flat = jnp.concatenate([a.reshape(lead + (-1,)) for a in arrs], axis=nl)
    n = flat.shape[nl]
    rows = -(-n // (8 * LANE)) * 8
    flat = jnp.pad(flat, [(0, 0)] * nl + [(0, rows * LANE - n)])
    return flat.reshape(lead + (rows, LANE))


def _slot_buffer(shard):
    return jnp.pad(shard[None], [(0, N_CHIPS - 1)] + [(0, 0)] * shard.ndim)


def kernel(x, norm_mix_w, w_in, ssm_conv_w, ssm_conv_b, ssm_dt_bias, ssm_a_log, ssm_d, ssm_norm_w, gdn_conv_w, gdn_a_log, gdn_dt_bias, gdn_norm_w, w_proj_ssm, w_proj_gdn, w_out, norm_ffn_w, w_ffn_in, w_ffn_down, final_norm_w, loss_target, m_norm_mix_w, m_w_in, m_ssm_conv_w, m_ssm_conv_b, m_ssm_dt_bias, m_ssm_a_log, m_ssm_d, m_ssm_norm_w, m_gdn_conv_w, m_gdn_a_log, m_gdn_dt_bias, m_gdn_norm_w, m_w_proj_ssm, m_w_proj_gdn, m_w_out, m_norm_ffn_w, m_w_ffn_in, m_w_ffn_down, m_final_norm_w, v_norm_mix_w, v_w_in, v_ssm_conv_w, v_ssm_conv_b, v_ssm_dt_bias, v_ssm_a_log, v_ssm_d, v_ssm_norm_w, v_gdn_conv_w, v_gdn_a_log, v_gdn_dt_bias, v_gdn_norm_w, v_w_proj_ssm, v_w_proj_gdn, v_w_out, v_norm_ffn_w, v_w_ffn_in, v_w_ffn_down, v_final_norm_w):
    wl = (norm_mix_w, w_in, ssm_conv_w, ssm_conv_b, ssm_dt_bias, ssm_a_log, ssm_d, ssm_norm_w, gdn_conv_w,
          gdn_a_log, gdn_dt_bias, gdn_norm_w, w_proj_ssm, w_proj_gdn, w_out, norm_ffn_w, w_ffn_in, w_ffn_down,
          final_norm_w)
    ml = (m_norm_mix_w, m_w_in, m_ssm_conv_w, m_ssm_conv_b, m_ssm_dt_bias, m_ssm_a_log, m_ssm_d, m_ssm_norm_w,
          m_gdn_conv_w, m_gdn_a_log, m_gdn_dt_bias, m_gdn_norm_w, m_w_proj_ssm, m_w_proj_gdn, m_w_out,
          m_norm_ffn_w, m_w_ffn_in, m_w_ffn_down, m_final_norm_w)
    vl = (v_norm_mix_w, v_w_in, v_ssm_conv_w, v_ssm_conv_b, v_ssm_dt_bias, v_ssm_a_log, v_ssm_d, v_ssm_norm_w,
          v_gdn_conv_w, v_gdn_a_log, v_gdn_dt_bias, v_gdn_norm_w, v_w_proj_ssm, v_w_proj_gdn, v_w_out,
          v_norm_ffn_w, v_w_ffn_in, v_w_ffn_down, v_final_norm_w)
    w = dict(zip(WEIGHTS, wl))
    m = dict(zip(WEIGHTS, ml))
    v = dict(zip(WEIGHTS, vl))
    x_pos, y_pos, c = _pos()
    me = 2 * x_pos + y_pos
    big = [n for n, _ in BIG]

    shards = [w[n].astype(BF16) for n in big]
    shards[0] = jnp.pad(shards[0], ((0, 0), (0, 0), (0, IN_SHARD_PAD - IN_SHARD)))
    conv_shapes = [w[n].shape[1:] for n, _ in CONVW]
    conv_pack = _pack_rows([w[n] for n, _ in CONVW], lead=(DEPTH,))
    gathered = weights_gather("gather_w", [_slot_buffer(s) for s in shards + [conv_pack]])
    by_chip = [[lax.dynamic_index_in_dim(g_, jnp.bitwise_xor(me, j), 0, keepdims=False) for j in range(N_CHIPS)]
               for g_ in gathered]
    full = {n: w[n] for n in SMALL}
    for i, (n, axis) in enumerate(BIG):
        cols = IN_SHARD if n == "w_in" else by_chip[i][0].shape[-1]
        full[n] = [jnp.concatenate([p[l][:, :cols] for p in by_chip[i]], axis=axis - 1) for l in range(DEPTH)]
    conv_parts = [_unpack(by_chip[-1][j], conv_shapes, lead=(DEPTH,)) for j in range(N_CHIPS)]
    for i, (n, axis) in enumerate(CONVW):
        full[n] = jnp.concatenate([conv_parts[j][i] for j in range(N_CHIPS)], axis=axis)

    loss_part, grad_x, grads = local_step(x[0], loss_target[0], full)

    by4 = [grads[n][:, None] if n == "w_in" else grads[n].reshape((DEPTH, N_CHIPS, -1) + grads[n].shape[-1:])
           for n in big]
    from_pair = pair_swap("grad_pair_swap", by4)
    chip_part = [pair_add(f"grad_pair_add_{n}", g_, r_, c) for n, g_, r_ in zip(big, by4, from_pair)]
    chip_part[0] = _aligned_to_shards(chip_part[0][0])
    from_chips = chip_scatter("grad_chip_scatter", chip_part)
    halves = [chip_sum(f"grad_chip_sum_{n}", s_, r_, me, c) for n, s_, r_ in zip(big, chip_part, from_chips)]
    reduced = pair_share("grad_pair_share", halves)
    g_sharded = dict(zip(big, reduced))
    g_sharded["w_in"] = g_sharded["w_in"][:, :, :IN_SHARD]

    small_names = list(SMALL) + [n for n, _ in CONVW]
    small_all = all_allgather("gather_small", _pack_rows([grads[n] for n in small_names] + [loss_part[0, :1]]))
    small_sum = slab_sum("small_sum", small_all)
    small_vals = _unpack(small_sum, [grads[n].shape for n in small_names] + [(1,)])
    g_small = dict(zip(small_names, small_vals[:-1]))
    loss = small_vals[-1].reshape(())
    for n, axis in CONVW:
        size = w[n].shape[axis]
        g_sharded[n] = lax.dynamic_slice_in_dim(g_small.pop(n), me * size, size, axis=axis)

    out_g, out_d, out_m, out_v = {}, {}, {}, {}
    for n, _ in SHARDED:
        shp = w[n].shape
        two = lambda a: a.reshape(-1, shp[-1])
        d_, m_, v_ = adamw(f"adamw_{n}", two(w[n]), two(g_sharded[n]), two(m[n]), two(v[n]))
        out_g[n], out_d[n], out_m[n], out_v[n] = g_sharded[n], d_.reshape(shp), m_.reshape(shp), v_.reshape(shp)
    d_, m_, v_ = adamw("adamw_small", *[_pack_rows([d[n] for n in SMALL]) for d in (w, g_small, m, v)])
    small_shapes = [w[n].shape for n in SMALL]
    for n, dd, mm, vv in zip(SMALL, _unpack(d_, small_shapes), _unpack(m_, small_shapes), _unpack(v_, small_shapes)):
        out_g[n], out_d[n], out_m[n], out_v[n] = g_small[n], dd, mm, vv

    return (loss, grad_x[None], *[out_g[n] for n in WEIGHTS], *[out_d[n] for n in WEIGHTS],
            *[out_m[n] for n in WEIGHTS], *[out_v[n] for n in WEIGHTS])
```

```python
import math

import jax
import jax.numpy as jnp
from jax import lax
from jax.experimental import pallas as pl
from jax.experimental.pallas import tpu as pltpu

F32 = jnp.float32
BF16 = jnp.bfloat16

D_MODEL = 1024
DEPTH = 2
SSM_HEADS = 16
SSM_HEAD_DIM = 64
SSM_D_INNER = 1024
SSM_STATE = 128
SSM_CONV_DIM = 1536
GDN_HEADS = 8
GDN_HEAD = 128
GDN_QKV_DIM = 3072
CONV_K = 4
CHUNK = 64
SSD_CHUNK = 256
FFN_HIDDEN = 2816
EPS = 1e-6
IN_DIM = 8736

ADAM_LR = 0.001
ADAM_B1 = 0.9
ADAM_B2 = 0.999
ADAM_EPS = 1e-08
ADAM_WD = 0.01
ADAM_STEP = 10

LANE = 128
NEG_BIG = -1e30
VMEM_LIMIT = 56 * 1024 * 1024

AL_Z, AL_XBC, AL_QKV, AL_GZ, AL_GS, AL_GG, AL_SMALL, AL_DIM = 0, 1024, 2560, 5632, 6656, 7680, 8704, 9216
SM_DT, SM_A, SM_B = 0, 16, 24

HI = lax.Precision.HIGHEST
NN = (((1,), (0,)), ((), ()))
NT = (((1,), (1,)), ((), ()))
TN = (((0,), (0,)), ((), ()))


def _cparams(sem):
    return pltpu.CompilerParams(dimension_semantics=sem, vmem_limit_bytes=VMEM_LIMIT)


def _pick(n, prefs):
    for p in prefs:
        if n % p == 0:
            return p
    return n


MATMUL_VMEM_BUDGET = 40 << 20
MXU_WIDTH = 256
HBM_BYTES_PER_S = 3.3e12
MXU_FLOPS_PER_S = 9.0e14
GRID_STEP_S = 0.35e-6
N_CHIPS = 4


def _matmul_tiles(m, n, n_dom, k, a_item, b_item, o_item):
    best = None
    def cands(dim, cap):
        return [c for c in range(LANE, min(dim, cap) + 1, LANE) if dim % c == 0] or [dim]

    tms, tns, tks = cands(m, 2048), cands(n_dom, 2304), cands(k, 1 << 30)
    for tm in tms:
        for tn in tns:
            for tk in tks:
                nk = k // tk
                vmem = (2 * (tm * tk * a_item + tk * tn * b_item + tm * tn * o_item) + tm * tn * 4 * (2 if nk > 1 else 1)
                        + (tm * tk * 2 if a_item > 2 else 0) + (tk * tn * 2 if b_item > 2 else 0))
                if vmem > MATMUL_VMEM_BUDGET:
                    continue
                traffic = m * k * a_item * (1 if nk == 1 else n // tn) + k * n * b_item * (m // tm)
                mxu_fill = tn / (-(-tn // MXU_WIDTH) * MXU_WIDTH)
                cost = (max(traffic / HBM_BYTES_PER_S, 2.0 * m * n * k / (MXU_FLOPS_PER_S * mxu_fill))
                        + (m // tm) * (n // tn) * nk * GRID_STEP_S)
                if best is None or cost < best[0]:
                    best = (cost, (tm, tn, tk))
    return best[1]


def _dot(a, b, dims=NN):
    return lax.dot_general(a.astype(BF16), b.astype(BF16), dims, preferred_element_type=F32)


def _dot3(a, b, dims=NN):
    a_hi, b_hi = a.astype(BF16), b.astype(BF16)
    a_lo = (a - a_hi.astype(F32)).astype(BF16)
    b_lo = (b - b_hi.astype(F32)).astype(BF16)

    def dg(u, v):
        return lax.dot_general(u, v, dims, preferred_element_type=F32)

    return dg(a_hi, b_hi) + (dg(a_hi, b_lo) + dg(a_lo, b_hi))


def _dot_hi(a, b, dims=NN):
    return lax.dot_general(a, b, dims, precision=HI, preferred_element_type=F32)


def _sigmoid(x):
    return jax.nn.sigmoid(x)


def _silu(x):
    return x * _sigmoid(x)


def _softplus(x):
    return jnp.maximum(x, 0.0) + jnp.log1p(jnp.exp(-jnp.abs(x)))


def matmul(name, a, b, mode, out_dtype=F32, chip_major=False, stack=None):
    if mode == "nn":
        (m, k), (k2, n) = a.shape, b.shape
    elif mode == "nt":
        (m, k), (n, k2) = a.shape, b.shape
    else:
        (k, m), (k2, n) = a.shape, b.shape
    assert k == k2, (a.shape, b.shape, mode)
    shard = n // N_CHIPS if chip_major else n
    tm, tn, tk = _matmul_tiles(m, n, shard, k, a.dtype.itemsize, b.dtype.itemsize, jnp.dtype(out_dtype).itemsize)
    if chip_major:
        per = shard // tn
        base_shape, base_blk = (N_CHIPS, m, shard), (None, tm, tn)
        base_idx = lambda i, j: (j // per, i, j % per)
    else:
        base_shape, base_blk = (m, n), (tm, tn)
        base_idx = lambda i, j: (i, j)
    nk = k // tk
    dims = {"nn": NN, "nt": NT, "tn": TN}[mode]

    def body_acc(a_ref, b_ref, o_ref, acc_ref):
        kk = pl.program_id(2)

        @pl.when(kk == 0)
        def _():
            acc_ref[...] = jnp.zeros_like(acc_ref)

        acc_ref[...] += _dot(a_ref[...], b_ref[...], dims)

        @pl.when(kk == nk - 1)
        def _():
            o_ref[...] = acc_ref[...].astype(o_ref.dtype)

    def body_one(a_ref, b_ref, o_ref):
        o_ref[...] = _dot(a_ref[...], b_ref[...], dims).astype(o_ref.dtype)

    compute = body_one if nk == 1 else body_acc
    if mode == "tn":
        a_spec = pl.BlockSpec((tk, tm), lambda i, j, kk: (kk, i))
    else:
        a_spec = pl.BlockSpec((tm, tk), lambda i, j, kk: (i, kk))
    if mode == "nt":
        b_spec = pl.BlockSpec((tn, tk), lambda i, j, kk: (j, kk))
    else:
        b_spec = pl.BlockSpec((tk, tn), lambda i, j, kk: (kk, j))
    in_specs, operands, aliases, body = [a_spec, b_spec], [a, b], {}, compute
    if stack is None:
        out_shape, out_blk, out_idx = base_shape, base_blk, (lambda i, j, kk: base_idx(i, j))
    else:
        layer, buf = stack
        out_shape, out_blk = (DEPTH,) + base_shape, (None,) + base_blk
        out_idx = lambda i, j, kk: (layer,) + base_idx(i, j)
        if buf is not None:
            assert buf.shape == out_shape and buf.dtype == out_dtype
            in_specs.append(pl.BlockSpec(memory_space=pl.ANY))
            operands.append(buf)
            aliases = {2: 0}

            def body(a_ref, b_ref, buf_ref, *rest):
                del buf_ref
                compute(a_ref, b_ref, *rest)

    return pl.pallas_call(
        body, name=name,
        out_shape=jax.ShapeDtypeStruct(out_shape, out_dtype),
        grid=(m // tm, n // tn, nk),
        in_specs=in_specs,
        out_specs=pl.BlockSpec(out_blk, out_idx),
        scratch_shapes=[] if nk == 1 else [pltpu.VMEM((tm, tn), F32)],
        input_output_aliases=aliases,
        compiler_params=_cparams(("parallel", "parallel", "arbitrary")),
    )(*operands)


def _row_map(c0, moves):
    return (lambda j, i: (i, c0 + j)) if moves else (lambda j, i: (i, c0))


def _par_map(c0, moves):
    return (lambda j, i: (0, c0 + j)) if moves else (lambda j, i: (0, c0))


def _in_spec(op, tile):
    _, kind, w, c0, moves = op
    if kind == "row":
        return pl.BlockSpec((tile, w), _row_map(c0, moves))
    return pl.BlockSpec((1, w), _par_map(c0, moves))


ROW_BLOCK_ELEMS = 1 << 18


def _row_tile(t, tile, ops):
    width = max(op[2] for op in ops if op[1] == "row")
    return min(t, max(tile, ROW_BLOCK_ELEMS // width))


def rowwise_fwd(name, fn, t, tile, ncol, ins, outs):
    n_in = len(ins)
    tile = _row_tile(t, tile, ins)

    def body(*refs):
        vals = [r[...].astype(F32) for r in refs[:n_in]]
        res = fn(*vals)
        if not isinstance(res, (tuple, list)):
            res = (res,)
        for r, v in zip(refs[n_in:], res):
            r[...] = v.astype(r.dtype)

    res = pl.pallas_call(
        body, name=name,
        out_shape=[jax.ShapeDtypeStruct((t, w * ncol), dt) for w, dt in outs],
        grid=(ncol, t // tile),
        in_specs=[_in_spec(op, tile) for op in ins],
        out_specs=[pl.BlockSpec((tile, w), _row_map(0, True)) for w, _ in outs],
        compiler_params=_cparams(("arbitrary", "arbitrary")),
    )(*[op[0] for op in ins])
    return res


def rowwise_bwd(name, fn, t, tile, ncol, ins, need, cts, addends=None, row_dtypes=None):
    n_in, n_ct = len(ins), len(cts)
    tile = _row_tile(t, tile, ins)
    addends = addends or {}
    row_dtypes = row_dtypes or {}
    didx = [i for i in range(n_in) if need[i]]
    add_ops = [addends[i] for i in didx if i in addends]
    n_add = len(add_ops)

    def body(*refs):
        in_refs = refs[:n_in]
        ct_refs = refs[n_in:n_in + n_ct]
        add_refs = refs[n_in + n_ct:n_in + n_ct + n_add]
        out_refs = refs[n_in + n_ct + n_add:]
        vals = [r[...].astype(F32) for r in in_refs]

        def g(*dv):
            full = list(vals)
            for i, v in zip(didx, dv):
                full[i] = v
            res = fn(*full)
            return tuple(res) if isinstance(res, (tuple, list)) else (res,)

        _, vjp = jax.vjp(g, *[vals[i] for i in didx])
        grads = vjp(tuple(c[...].astype(F32) for c in ct_refs))
        j, i = pl.program_id(0), pl.program_id(1)
        a = 0
        for o_ref, gv, idx in zip(out_refs, grads, didx):
            _, kind, _, _, moves = ins[idx]
            if kind == "row":
                if idx in addends:
                    gv = gv + add_refs[a][...].astype(F32)
                    a += 1
                o_ref[...] = gv.astype(o_ref.dtype)
            else:
                first = (i == 0) if moves else jnp.logical_and(i == 0, j == 0)

                @pl.when(first)
                def _(o_ref=o_ref, gv=gv):
                    o_ref[...] = gv

                @pl.when(jnp.logical_not(first))
                def _(o_ref=o_ref, gv=gv):
                    o_ref[...] += gv

    out_shape, out_specs = [], []
    for idx in didx:
        _, kind, w, _, moves = ins[idx]
        cols = w * (ncol if moves else 1)
        if kind == "row":
            out_shape.append(jax.ShapeDtypeStruct((t, cols), row_dtypes.get(idx, F32)))
            out_specs.append(pl.BlockSpec((tile, w), _row_map(0, moves)))
        else:
            out_shape.append(jax.ShapeDtypeStruct((1, cols), F32))
            out_specs.append(pl.BlockSpec((1, w), _par_map(0, moves)))
    ops = list(ins) + list(cts) + add_ops
    res = pl.pallas_call(
        body, name=name,
        out_shape=out_shape,
        grid=(ncol, t // tile),
        in_specs=[_in_spec(op, tile) for op in ops],
        out_specs=out_specs,
        compiler_params=_cparams(("arbitrary", "arbitrary")),
    )(*[op[0] for op in ops])
    return res


def f_rmsnorm(x, w):
    return x * lax.rsqrt(jnp.mean(x * x, axis=-1, keepdims=True) + EPS) * w


def f_ssd_post(y, z, w):
    y = y * _silu(z)
    return y * lax.rsqrt(jnp.mean(y * y, axis=-1, keepdims=True) + EPS) * w


def f_gdn_post(o, z, w):
    o = o * lax.rsqrt(jnp.mean(o * o, axis=-1, keepdims=True) + EPS) * w
    return o * _silu(z)


def f_merge(gs, p1, gg, p2):
    return _sigmoid(gs) * p1 + _sigmoid(gg) * p2


def f_swiglu(g, u):
    return _silu(g) * u


def final_loss(name, x, tgt, w, tile=256):
    t, d = x.shape

    def body(x_ref, t_ref, w_ref, loss_ref, dx_ref, dw_ref):
        i = pl.program_id(0)
        xv, tv, wv = x_ref[...], t_ref[...], w_ref[...]

        def g(xx, ww):
            err = f_rmsnorm(xx, ww) - tv
            return 0.5 * jnp.sum(jnp.mean(err * err, axis=-1, keepdims=True), axis=0, keepdims=True)

        val, vjp = jax.vjp(g, xv, wv)
        dx, dw = vjp(jnp.ones((1, 1), F32))
        dx_ref[...] = dx
        lv = jnp.broadcast_to(val, (1, LANE))

        @pl.when(i == 0)
        def _():
            loss_ref[...] = lv
            dw_ref[...] = dw

        @pl.when(i != 0)
        def _():
            loss_ref[...] += lv
            dw_ref[...] += dw

    return pl.pallas_call(
        body, name=name,
        out_shape=[jax.ShapeDtypeStruct((1, LANE), F32), jax.ShapeDtypeStruct((t, d), F32),
                   jax.ShapeDtypeStruct((1, d), F32)],
        grid=(t // tile,),
        in_specs=[pl.BlockSpec((tile, d), lambda i: (i, 0)), pl.BlockSpec((tile, d), lambda i: (i, 0)),
                  pl.BlockSpec((1, d), lambda i: (0, 0))],
        out_specs=[pl.BlockSpec((1, LANE), lambda i: (0, 0)), pl.BlockSpec((tile, d), lambda i: (i, 0)),
                   pl.BlockSpec((1, d), lambda i: (0, 0))],
        compiler_params=_cparams(("arbitrary",)),
    )(x, tgt, w)


CONV_W = 512
HALO = 8
STRIPS = 4


def _rows_back(before, cur, d):
    rows = lax.broadcasted_iota(jnp.int32, cur.shape, 0)
    return jnp.where(rows < d, pltpu.roll(before, d, 0), pltpu.roll(cur, d, 0))


def _rows_ahead(cur, after, d):
    rows = lax.broadcasted_iota(jnp.int32, cur.shape, 0)
    return jnp.where(rows < HALO - d, pltpu.roll(cur, HALO - d, 0), pltpu.roll(after, HALO - d, 0))


def _conv_taps(taps, bias, before, cur):
    shifted = [_rows_back(before, cur, CONV_K - 1 - k) for k in range(CONV_K - 1)] + [cur]
    pre = bias + taps[CONV_K - 1] * cur
    for k in range(CONV_K - 1):
        pre = pre + taps[k] * shifted[k]
    return pre, shifted


def conv_fwd(name, src, c0, width, w, b, tile=512):
    t = src.shape[0]
    ncol, nrow = width // CONV_W, t // tile
    cb0 = c0 // CONV_W
    hb = tile // HALO

    def body(prev_ref, cur_ref, w_ref, b_ref, o_ref):
        i = pl.program_id(1)
        taps = [w_ref[k:k + 1, :] for k in range(CONV_K)]
        bias = b_ref[...]

        def strips(g, before):
            for u in range(STRIPS):
                r0 = pl.multiple_of((g * STRIPS + u) * HALO, HALO)
                cur = cur_ref[pl.ds(r0, HALO), :]
                pre, _ = _conv_taps(taps, bias, before, cur)
                o_ref[pl.ds(r0, HALO), :] = _silu(pre)
                before = cur
            return before

        lax.fori_loop(0, tile // (HALO * STRIPS), strips, jnp.where(i == 0, 0.0, prev_ref[...]))

    return pl.pallas_call(
        body, name=name,
        out_shape=jax.ShapeDtypeStruct((t, width), F32),
        grid=(ncol, nrow),
        in_specs=[pl.BlockSpec((HALO, CONV_W), lambda j, i: (jnp.maximum(i * hb - 1, 0), cb0 + j)),
                  pl.BlockSpec((tile, CONV_W), lambda j, i: (i, cb0 + j)),
                  pl.BlockSpec((CONV_K, CONV_W), lambda j, i: (0, j)),
                  pl.BlockSpec((1, CONV_W), lambda j, i: (0, j))],
        out_specs=pl.BlockSpec((tile, CONV_W), lambda j, i: (i, j)),
        compiler_params=_cparams(("arbitrary", "arbitrary")),
    )(src, src, w, b)


def conv_bwd(name, src, c0, width, w, b, dy, tile=512):
    t = src.shape[0]
    ncol, nrow = width // CONV_W, t // tile
    cb0 = c0 // CONV_W
    hb = tile // HALO
    last_hb = t // HALO - 1
    nstrip = tile // HALO

    def body(sprev_ref, scur_ref, snext_ref, w_ref, b_ref, dycur_ref, dynext_ref,
             du_ref, dw_ref, db_ref, dpre_ref):
        i = pl.program_id(1)
        taps = [w_ref[k:k + 1, :] for k in range(CONV_K)]
        bias = b_ref[...]

        def dpre_of(before, cur, dy_strip):
            pre, shifted = _conv_taps(taps, bias, before, cur)
            s = _sigmoid(pre)
            return dy_strip * (s * (1.0 + pre * (1.0 - s))), shifted

        def strips1(g, carry):
            before, dws, dbs = carry
            for u in range(STRIPS):
                r0 = pl.multiple_of((g * STRIPS + u) * HALO, HALO)
                cur = scur_ref[pl.ds(r0, HALO), :]
                dpre, shifted = dpre_of(before, cur, dycur_ref[pl.ds(r0, HALO), :])
                dpre_ref[pl.ds(r0, HALO), :] = dpre
                before, dws, dbs = cur, tuple(a + dpre * v for a, v in zip(dws, shifted)), dbs + dpre
            return before, dws, dbs

        zero = jnp.zeros((HALO, CONV_W), F32)
        before, dws, dbs = lax.fori_loop(0, nstrip // STRIPS, strips1,
                                         (jnp.where(i == 0, 0.0, sprev_ref[...]), (zero,) * CONV_K, zero))
        dpre_next, _ = dpre_of(before, snext_ref[...], dynext_ref[...])
        dpre_ref[pl.ds(tile, HALO), :] = jnp.where(i == nrow - 1, 0.0, dpre_next)

        def strips2(g, _):
            parts = []
            for u in range(STRIPS):
                r0 = pl.multiple_of((g * STRIPS + u) * HALO, HALO)
                cur = dpre_ref[pl.ds(r0, HALO), :]
                after = dpre_ref[pl.ds(r0 + HALO, HALO), :]
                acc = taps[CONV_K - 1] * cur
                for d in range(1, CONV_K):
                    acc = acc + taps[CONV_K - 1 - d] * _rows_ahead(cur, after, d)
                parts.append(acc)
            r0 = pl.multiple_of(g * STRIPS * HALO, STRIPS * HALO)
            du_ref[pl.ds(r0, STRIPS * HALO), :] = jnp.concatenate(parts, axis=0).astype(du_ref.dtype)
            return 0

        lax.fori_loop(0, nstrip // STRIPS, strips2, 0)
        dw_tile = jnp.concatenate([jnp.sum(a, axis=0, keepdims=True) for a in dws], axis=0)
        db_tile = jnp.sum(dbs, axis=0, keepdims=True)
        _acc(dw_ref, dw_tile, i == 0)
        _acc(db_ref, db_tile, i == 0)

    return pl.pallas_call(
        body, name=name,
        out_shape=[jax.ShapeDtypeStruct((t, width), BF16), jax.ShapeDtypeStruct((CONV_K, width), F32),
                   jax.ShapeDtypeStruct((1, width), F32)],
        grid=(ncol, nrow),
        in_specs=[pl.BlockSpec((HALO, CONV_W), lambda j, i: (jnp.maximum(i * hb - 1, 0), cb0 + j)),
                  pl.BlockSpec((tile, CONV_W), lambda j, i: (i, cb0 + j)),
                  pl.BlockSpec((HALO, CONV_W), lambda j, i: (jnp.minimum((i + 1) * hb, last_hb), cb0 + j)),
                  pl.BlockSpec((CONV_K, CONV_W), lambda j, i: (0, j)),
                  pl.BlockSpec((1, CONV_W), lambda j, i: (0, j)),
                  pl.BlockSpec((tile, CONV_W), lambda j, i: (i, j)),
                  pl.BlockSpec((HALO, CONV_W), lambda j, i: (jnp.minimum((i + 1) * hb, last_hb), j))],
        out_specs=[pl.BlockSpec((tile, CONV_W), lambda j, i: (i, j)),
                   pl.BlockSpec((CONV_K, CONV_W), lambda j, i: (0, j)),
                   pl.BlockSpec((1, CONV_W), lambda j, i: (0, j))],
        scratch_shapes=[pltpu.VMEM((tile + HALO, CONV_W), F32)],
        compiler_params=_cparams(("arbitrary", "arbitrary")),
    )(src, src, src, w, b, dy, dy)


def _iota2(q):
    return (lax.broadcasted_iota(jnp.int32, (q, q), 0), lax.broadcasted_iota(jnp.int32, (q, q), 1))


def _lane_pick(blk, idx):
    lane = lax.broadcasted_iota(jnp.int32, (1, LANE), 1)
    return jnp.sum(jnp.where(lane == idx, blk, 0.0), axis=1, keepdims=True)


class _Decay:
    def __init__(self, a):
        q = a.shape[0]
        r, c = _iota2(q)
        self.r, self.c = r, c
        self.cum = _dot_hi((c <= r).astype(F32), a)
        self.cum_t = _dot_hi(a, (r <= c).astype(F32), TN)
        self.tot = self.cum[q - 1:q, :]
        self.e_cum = jnp.exp(self.cum)
        self.e_rest = jnp.exp(self.tot - self.cum)
        self.e_tot = jnp.exp(self.tot)

    def mask(self, lane):
        rows = lax.broadcasted_iota(jnp.int32, (LANE, 1), 0)
        cum_row = jnp.sum(jnp.where(rows == lane, self.cum_t, 0.0), axis=0, keepdims=True)
        return jnp.exp(jnp.where(self.r >= self.c, _lane_pick(self.cum, lane) - cum_row, NEG_BIG))


_SSD_B = SSM_D_INNER
_SSD_C = SSM_D_INNER + 2 * SSM_STATE


def _interleave(gens):
    results = [None] * len(gens)
    live = list(range(len(gens)))
    while live:
        for i in list(live):
            try:
                next(gens[i])
            except StopIteration as stop:
                results[i] = stop.value
                live.remove(i)
    return results


def ssd_chunk(xs, bm, cm, dt_all, dsk, dec, state, p, cb):
    lane = lax.broadcasted_iota(jnp.int32, (1, LANE), 1)
    m0 = lane < SSM_HEAD_DIM
    h0, h1 = 2 * p, 2 * p + 1

    def both(blk):
        return jnp.where(m0, _lane_pick(blk, h0), _lane_pick(blk, h1))

    xdt = xs * both(dt_all)
    l0, l1 = dec.mask(h0), dec.mask(h1)
    yield
    y_diag = _dot(cb * l0, jnp.where(m0, xdt, 0.0)) + _dot(cb * l1, jnp.where(m0, 0.0, xdt))
    y_off = _dot(cm, state, NT) * both(dec.e_cum)
    yield
    rowm = lax.broadcasted_iota(jnp.int32, (LANE, 1), 0) < SSM_HEAD_DIM
    new_state = (state * jnp.where(rowm, _lane_pick(dec.e_tot, h0), _lane_pick(dec.e_tot, h1))
                 + _dot(xdt * both(dec.e_rest), bm, TN))
    y = y_diag + y_off + both(dsk) * xs
    return y, new_state


def ssd_pairs(xs, bms, cms, small, dtb, alog, dsk, states):
    dt_all = _softplus(small + dtb)
    dec = _Decay(dt_all * (-jnp.exp(alog)))
    cbs = [_dot(cm, bm, NT) for cm, bm in zip(cms, bms)]
    res = _interleave([ssd_chunk(x, bms[p // 4], cms[p // 4], dt_all, dsk, dec, st, p, cbs[p // 4])
                       for p, (x, st) in enumerate(zip(xs, states))])
    return tuple(y for y, _ in res), tuple(s for _, s in res)


def tri_inverse(a):
    q = a.shape[0]
    r, c = _iota2(q)
    eye = (r == c).astype(F32)
    diag = (r // 16) == (c // 16)
    bd = jnp.where(diag, a, 0.0)
    off = jnp.where(diag, 0.0, a)
    b2 = _dot3(bd, bd)
    d1 = _dot3(eye - bd, eye + b2)
    yield
    b4 = _dot3(b2, b2)
    yield
    b8 = _dot3(b4, b4)
    d2 = _dot3(d1, eye + b4)
    yield
    dinv = _dot3(d2, eye + b8)
    yield
    n = _dot3(dinv, off)
    yield
    n2 = _dot3(n, n)
    yield
    m = _dot3(eye + n2, dinv)
    yield
    return _dot3(eye - n, m)


@jax.custom_vjp
def _solve_with(xinv, a, rhs):
    del a
    return _dot3(xinv, rhs)


def _solve_with_fwd(xinv, a, rhs):
    t = _dot3(xinv, rhs)
    return t, (xinv, t)


def _solve_with_bwd(res, dt):
    xinv, t = res
    d_rhs = _dot3(xinv, dt, TN)
    d_a = -_dot(d_rhs, t, NT)
    return jnp.zeros_like(xinv), d_a, d_rhs


_solve_with.defvjp(_solve_with_fwd, _solve_with_bwd)


_GDN_K = GDN_HEADS * GDN_HEAD
_GDN_V = 2 * GDN_HEADS * GDN_HEAD


def gdn_chunk(qh, kh, vh, beta_all, dec, state, h, xinv=None):
    r, c = dec.r, dec.c
    qn = qh * lax.rsqrt(jnp.sum(qh * qh, axis=-1, keepdims=True) + EPS) * (GDN_HEAD ** -0.5)
    kn = kh * lax.rsqrt(jnp.sum(kh * kh, axis=-1, keepdims=True) + EPS)
    beta = _lane_pick(beta_all, SM_B + h)
    decay = dec.mask(SM_A + h)
    yield
    kk = _dot(kn, kn, NT)
    qk = _dot(qn, kn, NT) * decay
    amat = jnp.where(r > c, kk * decay * beta, 0.0)
    eg = _lane_pick(dec.e_cum, SM_A + h)
    rhs = jnp.concatenate([vh * beta, kn * (beta * eg)], axis=1)
    qs = _dot(qn * eg, state)
    yield
    if xinv is None:
        xinv = yield from tri_inverse(amat)
        t = _dot3(xinv, rhs)
    else:
        t = _solve_with(xinv, amat, rhs)
    yield
    u, w = t[:, :GDN_HEAD], t[:, GDN_HEAD:]
    v_new = u - _dot(w, state)
    yield
    o = qs + _dot(qk, v_new)
    new_state = (state * _lane_pick(dec.e_tot, SM_A + h)
                 + _dot(kn * _lane_pick(dec.e_rest, SM_A + h), v_new, TN))
    return o, new_state, xinv


def gdn_heads(qs, ks, vs, small, alog, dtb, states, xinvs=None):
    nh = len(qs)
    beta_all = _sigmoid(small)
    dec = _Decay(-jnp.exp(alog) * _softplus(small + dtb))
    res = _interleave([gdn_chunk(qs[h], ks[h], vs[h], beta_all, dec, states[h], h,
                                 None if xinvs is None else xinvs[h]) for h in range(nh)])
    return tuple(o for o, _, _ in res), tuple(s for _, s, _ in res), tuple(x for _, _, x in res)


def _acc(ref, val, first):
    @pl.when(first)
    def _():
        ref[...] = val

    @pl.when(jnp.logical_not(first))
    def _():
        ref[...] += val


def ssd_scan_fwd(name, xbc, proj, dtb, alog, dsk):
    t = xbc.shape[0]
    nc, npair = t // SSD_CHUNK, SSM_HEADS // 2
    small_blk = AL_SMALL // LANE

    def body(xbc_ref, sm_ref, dtb_ref, alog_ref, dsk_ref, y_ref, sin_ref, st_ref):
        ci = pl.program_id(0)

        @pl.when(ci == 0)
        def _():
            st_ref[...] = jnp.zeros_like(st_ref)

        s_in = tuple(st_ref[p] for p in range(npair))
        ys, s_new = ssd_pairs(tuple(xbc_ref[:, p * LANE:(p + 1) * LANE] for p in range(npair)),
                              tuple(xbc_ref[:, _SSD_B + g * LANE:_SSD_B + (g + 1) * LANE] for g in range(2)),
                              tuple(xbc_ref[:, _SSD_C + g * LANE:_SSD_C + (g + 1) * LANE] for g in range(2)),
                              sm_ref[...], dtb_ref[...], alog_ref[...], dsk_ref[...], s_in)
        for p in range(npair):
            sin_ref[0, p] = s_in[p]
            y_ref[:, p * LANE:(p + 1) * LANE] = ys[p]
            st_ref[p] = s_new[p]

    par = pl.BlockSpec((1, LANE), lambda ci: (0, 0))
    return pl.pallas_call(
        body, name=name,
        out_shape=[jax.ShapeDtypeStruct((t, SSM_D_INNER), F32),
                   jax.ShapeDtypeStruct((nc, npair, LANE, LANE), F32)],
        grid=(nc,),
        in_specs=[pl.BlockSpec((SSD_CHUNK, SSM_CONV_DIM), lambda ci: (ci, 0)),
                  pl.BlockSpec((SSD_CHUNK, LANE), lambda ci: (ci, small_blk)),
                  par, par, par],
        out_specs=[pl.BlockSpec((SSD_CHUNK, SSM_D_INNER), lambda ci: (ci, 0)),
                   pl.BlockSpec((1, npair, LANE, LANE), lambda ci: (ci, 0, 0, 0))],
        scratch_shapes=[pltpu.VMEM((npair, LANE, LANE), F32)],
        compiler_params=_cparams(("arbitrary",)),
    )(xbc, proj, dtb, alog, dsk)


def ssd_scan_bwd(name, xbc, proj, dtb, alog, dsk, s_in, dy):
    t = xbc.shape[0]
    nc, npair = t // SSD_CHUNK, SSM_HEADS // 2
    small_blk = AL_SMALL // LANE

    def body(xbc_ref, sm_ref, dtb_ref, alog_ref, dsk_ref, sin_ref, dy_ref,
             dxbc_ref, dsm_ref, ddtb_ref, dalog_ref, ddsk_ref, dst_ref):
        ci = pl.program_id(0)

        @pl.when(ci == 0)
        def _():
            dst_ref[...] = jnp.zeros_like(dst_ref)

        _, vjp = jax.vjp(ssd_pairs, tuple(xbc_ref[:, p * LANE:(p + 1) * LANE] for p in range(npair)),
                         tuple(xbc_ref[:, _SSD_B + g * LANE:_SSD_B + (g + 1) * LANE] for g in range(2)),
                         tuple(xbc_ref[:, _SSD_C + g * LANE:_SSD_C + (g + 1) * LANE] for g in range(2)),
                         sm_ref[...], dtb_ref[...], alog_ref[...], dsk_ref[...],
                         tuple(sin_ref[0, p] for p in range(npair)))
        dxs, dbms, dcms, dsm, ddtb, dalog, ddsk, dsts = vjp(
            (tuple(dy_ref[:, p * LANE:(p + 1) * LANE] for p in range(npair)),
             tuple(dst_ref[p] for p in range(npair))))
        for p in range(npair):
            dxbc_ref[:, p * LANE:(p + 1) * LANE] = dxs[p]
            dst_ref[p] = dsts[p]
        for g in range(2):
            dxbc_ref[:, _SSD_B + g * LANE:_SSD_B + (g + 1) * LANE] = dbms[g]
            dxbc_ref[:, _SSD_C + g * LANE:_SSD_C + (g + 1) * LANE] = dcms[g]
        dsm_ref[...] = dsm
        _acc(ddtb_ref, ddtb, ci == 0)
        _acc(dalog_ref, dalog, ci == 0)
        _acc(ddsk_ref, ddsk, ci == 0)

    par = pl.BlockSpec((1, LANE), lambda ci: (0, 0))
    rev = lambda ci: nc - 1 - ci
    return pl.pallas_call(
        body, name=name,
        out_shape=[jax.ShapeDtypeStruct((t, SSM_CONV_DIM), F32),
                   jax.ShapeDtypeStruct((t, LANE), F32),
                   jax.ShapeDtypeStruct((1, LANE), F32), jax.ShapeDtypeStruct((1, LANE), F32),
                   jax.ShapeDtypeStruct((1, LANE), F32)],
        grid=(nc,),
        in_specs=[pl.BlockSpec((SSD_CHUNK, SSM_CONV_DIM), lambda ci: (rev(ci), 0)),
                  pl.BlockSpec((SSD_CHUNK, LANE), lambda ci: (rev(ci), small_blk)),
                  par, par, par,
                  pl.BlockSpec((1, npair, LANE, LANE), lambda ci: (rev(ci), 0, 0, 0)),
                  pl.BlockSpec((SSD_CHUNK, SSM_D_INNER), lambda ci: (rev(ci), 0))],
        out_specs=[pl.BlockSpec((SSD_CHUNK, SSM_CONV_DIM), lambda ci: (rev(ci), 0)),
                   pl.BlockSpec((SSD_CHUNK, LANE), lambda ci: (rev(ci), 0)),
                   par, par, par],
        scratch_shapes=[pltpu.VMEM((npair, LANE, LANE), F32)],
        compiler_params=_cparams(("arbitrary",)),
    )(xbc, proj, dtb, alog, dsk, s_in, dy)


def gdn_scan_fwd(name, qkv, proj, alog, dtb):
    t = qkv.shape[0]
    nc, nh = t // CHUNK, GDN_HEADS
    small_blk = AL_SMALL // LANE

    def body(qkv_ref, sm_ref, alog_ref, dtb_ref, o_ref, sin_ref, x_ref, st_ref):
        ci = pl.program_id(0)

        @pl.when(ci == 0)
        def _():
            st_ref[...] = jnp.zeros_like(st_ref)

        s_in = tuple(st_ref[h] for h in range(nh))
        os, s_new, xinvs = gdn_heads(
            tuple(qkv_ref[:, h * LANE:(h + 1) * LANE] for h in range(nh)),
            tuple(qkv_ref[:, _GDN_K + h * LANE:_GDN_K + (h + 1) * LANE] for h in range(nh)),
            tuple(qkv_ref[:, _GDN_V + h * LANE:_GDN_V + (h + 1) * LANE] for h in range(nh)),
            sm_ref[...], alog_ref[...], dtb_ref[...], s_in)
        for h in range(nh):
            sin_ref[0, h] = s_in[h]
            o_ref[:, h * LANE:(h + 1) * LANE] = os[h]
            x_ref[0, h] = xinvs[h]
            st_ref[h] = s_new[h]

    par = pl.BlockSpec((1, LANE), lambda ci: (0, 0))
    return pl.pallas_call(
        body, name=name,
        out_shape=[jax.ShapeDtypeStruct((t, GDN_HEADS * GDN_HEAD), F32),
                   jax.ShapeDtypeStruct((nc, nh, LANE, LANE), F32),
                   jax.ShapeDtypeStruct((nc, nh, CHUNK, CHUNK), F32)],
        grid=(nc,),
        in_specs=[pl.BlockSpec((CHUNK, GDN_QKV_DIM), lambda ci: (ci, 0)),
                  pl.BlockSpec((CHUNK, LANE), lambda ci: (ci, small_blk)),
                  par, par],
        out_specs=[pl.BlockSpec((CHUNK, GDN_HEADS * GDN_HEAD), lambda ci: (ci, 0)),
                   pl.BlockSpec((1, nh, LANE, LANE), lambda ci: (ci, 0, 0, 0)),
                   pl.BlockSpec((1, nh, CHUNK, CHUNK), lambda ci: (ci, 0, 0, 0))],
        scratch_shapes=[pltpu.VMEM((nh, LANE, LANE), F32)],
        compiler_params=_cparams(("arbitrary",)),
    )(qkv, proj, alog, dtb)


def gdn_scan_bwd(name, qkv, proj, alog, dtb, s_in, xinv, do, dsm_in):
    t = qkv.shape[0]
    nc, nh = t // CHUNK, GDN_HEADS
    small_blk = AL_SMALL // LANE

    def body(qkv_ref, sm_ref, alog_ref, dtb_ref, sin_ref, x_ref, do_ref, dsmi_ref,
             dqkv_ref, dsm_ref, dalog_ref, ddtb_ref, dst_ref):
        ci = pl.program_id(0)

        @pl.when(ci == 0)
        def _():
            dst_ref[...] = jnp.zeros_like(dst_ref)

        xis = tuple(x_ref[0, h] for h in range(nh))

        def fn(qs, ks, vs, sm, alog_, dtb_, sts):
            os, s_new, _ = gdn_heads(qs, ks, vs, sm, alog_, dtb_, sts, xinvs=xis)
            return os, s_new

        _, vjp = jax.vjp(fn, tuple(qkv_ref[:, h * LANE:(h + 1) * LANE] for h in range(nh)),
                         tuple(qkv_ref[:, _GDN_K + h * LANE:_GDN_K + (h + 1) * LANE] for h in range(nh)),
                         tuple(qkv_ref[:, _GDN_V + h * LANE:_GDN_V + (h + 1) * LANE] for h in range(nh)),
                         sm_ref[...], alog_ref[...], dtb_ref[...], tuple(sin_ref[0, h] for h in range(nh)))
        dqs, dks, dvs, dsm, dalog, ddtb, dsts = vjp(
            (tuple(do_ref[:, h * LANE:(h + 1) * LANE] for h in range(nh)), tuple(dst_ref[h] for h in range(nh))))
        for h in range(nh):
            dqkv_ref[:, h * LANE:(h + 1) * LANE] = dqs[h]
            dqkv_ref[:, _GDN_K + h * LANE:_GDN_K + (h + 1) * LANE] = dks[h]
            dqkv_ref[:, _GDN_V + h * LANE:_GDN_V + (h + 1) * LANE] = dvs[h]
            dst_ref[h] = dsts[h]
        dsm_ref[...] = dsmi_ref[...] + dsm
        _acc(dalog_ref, dalog, ci == 0)
        _acc(ddtb_ref, ddtb, ci == 0)

    par = pl.BlockSpec((1, LANE), lambda ci: (0, 0))
    rev = lambda ci: nc - 1 - ci
    return pl.pallas_call(
        body, name=name,
        out_shape=[jax.ShapeDtypeStruct((t, GDN_QKV_DIM), F32), jax.ShapeDtypeStruct((t, LANE), F32),
                   jax.ShapeDtypeStruct((1, LANE), F32), jax.ShapeDtypeStruct((1, LANE), F32)],
        grid=(nc,),
        in_specs=[pl.BlockSpec((CHUNK, GDN_QKV_DIM), lambda ci: (rev(ci), 0)),
                  pl.BlockSpec((CHUNK, LANE), lambda ci: (rev(ci), small_blk)),
                  par, par,
                  pl.BlockSpec((1, nh, LANE, LANE), lambda ci: (rev(ci), 0, 0, 0)),
                  pl.BlockSpec((1, nh, CHUNK, CHUNK), lambda ci: (rev(ci), 0, 0, 0)),
                  pl.BlockSpec((CHUNK, GDN_HEADS * GDN_HEAD), lambda ci: (rev(ci), 0)),
                  pl.BlockSpec((CHUNK, LANE), lambda ci: (rev(ci), 0))],
        out_specs=[pl.BlockSpec((CHUNK, GDN_QKV_DIM), lambda ci: (rev(ci), 0)),
                   pl.BlockSpec((CHUNK, LANE), lambda ci: (rev(ci), 0)),
                   par, par],
        scratch_shapes=[pltpu.VMEM((nh, LANE, LANE), F32)],
        compiler_params=_cparams(("arbitrary",)),
    )(qkv, proj, alog, dtb, s_in, xinv, do, dsm_in)


def _row(arr, w, c0=0, moves=False):
    return (arr, "row", w, c0, moves)


def _par(arr, w, c0=0, moves=False):
    return (arr, "par", w, c0, moves)


def matmul_add(name, a, b, res):
    (m, k), (_, n) = a.shape, b.shape
    tm, tn, tk = _matmul_tiles(m, n, n, k, a.dtype.itemsize, b.dtype.itemsize, 4 + res.dtype.itemsize)
    nk = k // tk

    def body_acc(a_ref, b_ref, r_ref, o_ref, acc_ref):
        kk = pl.program_id(2)

        @pl.when(kk == 0)
        def _():
            acc_ref[...] = r_ref[...]

        acc_ref[...] += _dot(a_ref[...], b_ref[...])

        @pl.when(kk == nk - 1)
        def _():
            o_ref[...] = acc_ref[...]

    def body_one(a_ref, b_ref, r_ref, o_ref):
        o_ref[...] = r_ref[...] + _dot(a_ref[...], b_ref[...])

    body = body_one if nk == 1 else body_acc
    return pl.pallas_call(
        body, name=name,
        out_shape=jax.ShapeDtypeStruct((m, n), F32),
        grid=(m // tm, n // tn, nk),
        in_specs=[pl.BlockSpec((tm, tk), lambda i, j, kk: (i, kk)),
                  pl.BlockSpec((tk, tn), lambda i, j, kk: (kk, j)),
                  pl.BlockSpec((tm, tn), lambda i, j, kk: (i, j))],
        out_specs=pl.BlockSpec((tm, tn), lambda i, j, kk: (i, j)),
        scratch_shapes=[] if nk == 1 else [pltpu.VMEM((tm, tn), F32)],
        compiler_params=_cparams(("parallel", "parallel", "arbitrary")),
    )(a, b, res)


def layer_fwd(l, x, w):
    t = x.shape[0]
    rt = min(256, t)
    s = {"x": x}
    s["h"] = rowwise_fwd(f"norm_mix_l{l}", f_rmsnorm, t, rt, 1,
                         [_row(x, D_MODEL), _par(w["norm_mix_w"], D_MODEL)], [(D_MODEL, BF16)])[0]
    s["proj"] = matmul(f"in_proj_l{l}", s["h"], w["w_in"], "nn")
    s["xbc"] = conv_fwd(f"ssm_conv_l{l}", s["proj"], AL_XBC, SSM_CONV_DIM, w["ssm_conv_w"], w["ssm_conv_b"],
                        tile=min(512, t))
    s["qkv"] = conv_fwd(f"gdn_conv_l{l}", s["proj"], AL_QKV, GDN_QKV_DIM, w["gdn_conv_w"], w["gdn_conv_b"],
                        tile=min(512, t))
    s["y_scan"], s["ssd_sin"] = ssd_scan_fwd(f"ssd_scan_l{l}", s["xbc"], s["proj"], w["ssm_dt_bias"],
                                             w["ssm_a_log"], w["ssm_d"])
    s["o_scan"], s["gdn_sin"], s["gdn_x"] = gdn_scan_fwd(f"gdn_scan_l{l}", s["qkv"], s["proj"],
                                                         w["gdn_a_log"], w["gdn_dt_bias"])
    s["y_ssm"] = rowwise_fwd(f"ssd_post_l{l}", f_ssd_post, t, rt, 2,
                             [_row(s["y_scan"], 512, 0, True), _row(s["proj"], 512, AL_Z // 512, True),
                              _par(w["ssm_norm_w"], 512, 0, True)], [(512, BF16)])[0]
    s["y_gdn"] = rowwise_fwd(f"gdn_post_l{l}", f_gdn_post, t, rt, GDN_HEADS,
                             [_row(s["o_scan"], LANE, 0, True), _row(s["proj"], LANE, AL_GZ // LANE, True),
                              _par(w["gdn_norm_w"], LANE)], [(LANE, BF16)])[0]
    s["p1"] = matmul(f"proj_ssm_l{l}", s["y_ssm"], w["w_proj_ssm"], "nn")
    s["p2"] = matmul(f"proj_gdn_l{l}", s["y_gdn"], w["w_proj_gdn"], "nn")
    s["merged"] = rowwise_fwd(f"merge_l{l}", f_merge, t, rt, 2,
                              [_row(s["proj"], 512, AL_GS // 512, True), _row(s["p1"], 512, 0, True),
                               _row(s["proj"], 512, AL_GG // 512, True), _row(s["p2"], 512, 0, True)],
                              [(512, BF16)])[0]
    s["x1"] = matmul_add(f"out_proj_l{l}", s["merged"], w["w_out"], x)
    s["h2"] = rowwise_fwd(f"norm_ffn_l{l}", f_rmsnorm, t, rt, 1,
                          [_row(s["x1"], D_MODEL), _par(w["norm_ffn_w"], D_MODEL)], [(D_MODEL, BF16)])[0]
    s["gu"] = matmul(f"ffn_in_l{l}", s["h2"], w["w_ffn_in"], "nn")
    s["act"] = rowwise_fwd(f"swiglu_l{l}", f_swiglu, t, rt, FFN_HIDDEN // 256,
                           [_row(s["gu"], 256, 0, True), _row(s["gu"], 256, FFN_HIDDEN // 256, True)],
                           [(256, BF16)])[0]
    x2 = matmul_add(f"ffn_down_l{l}", s["act"], w["w_ffn_down"], s["x1"])
    return x2, s


IN_SHARD = IN_DIM // 4
IN_SHARD_PAD = 2304


def _aligned_to_shards(g):
    orig = jnp.concatenate([g[:, 0:2560], g[:, AL_SMALL:AL_SMALL + 16], g[:, 2560:6656],
                            g[:, AL_SMALL + 16:AL_SMALL + 32], g[:, 6656:8704]], axis=1)
    return jnp.stack([jnp.pad(orig[:, j * IN_SHARD:(j + 1) * IN_SHARD], ((0, 0), (0, IN_SHARD_PAD - IN_SHARD)))
                      for j in range(N_CHIPS)])


def layer_bwd(l, dx2, w, s, gbuf):
    t = dx2.shape[0]
    rt = min(256, t)
    ct = min(512, t)
    g = {}
    dact = matmul(f"ffn_down_dx_l{l}", dx2, w["w_ffn_down"], "nt")
    g["w_ffn_down"] = matmul(f"ffn_down_dw_l{l}", s["act"], dx2, "tn", stack=(l, gbuf.get("w_ffn_down")))
    nf = FFN_HIDDEN // 256
    dgate, dup = rowwise_bwd(f"swiglu_bwd_l{l}", f_swiglu, t, rt, nf,
                             [_row(s["gu"], 256, 0, True), _row(s["gu"], 256, nf, True)], [True, True],
                             [_row(dact, 256, 0, True)], row_dtypes={0: BF16, 1: BF16})
    dgu = jnp.concatenate([dgate, dup], axis=1)
    dh2 = matmul(f"ffn_in_dx_l{l}", dgu, w["w_ffn_in"], "nt")
    g["w_ffn_in"] = matmul(f"ffn_in_dw_l{l}", s["h2"], dgu, "tn", chip_major=True,
                           stack=(l, gbuf.get("w_ffn_in")))
    dx1, g["norm_ffn_w"] = rowwise_bwd(f"norm_ffn_bwd_l{l}", f_rmsnorm, t, rt, 1,
                                       [_row(s["x1"], D_MODEL), _par(w["norm_ffn_w"], D_MODEL)], [True, True],
                                       [_row(dh2, D_MODEL)], addends={0: _row(dx2, D_MODEL)})
    dmerged = matmul(f"out_proj_dx_l{l}", dx1, w["w_out"], "nt")
    g["w_out"] = matmul(f"out_proj_dw_l{l}", s["merged"], dx1, "tn", stack=(l, gbuf.get("w_out")))
    dgs, dp1, dgg, dp2 = rowwise_bwd(
        f"merge_bwd_l{l}", f_merge, t, rt, 2,
        [_row(s["proj"], 512, AL_GS // 512, True), _row(s["p1"], 512, 0, True),
         _row(s["proj"], 512, AL_GG // 512, True), _row(s["p2"], 512, 0, True)], [True] * 4,
        [_row(dmerged, 512, 0, True)], row_dtypes={0: BF16, 1: BF16, 2: BF16, 3: BF16})
    dy_ssm = matmul(f"proj_ssm_dx_l{l}", dp1, w["w_proj_ssm"], "nt")
    g["w_proj_ssm"] = matmul(f"proj_ssm_dw_l{l}", s["y_ssm"], dp1, "tn", stack=(l, gbuf.get("w_proj_ssm")))
    dy_gdn = matmul(f"proj_gdn_dx_l{l}", dp2, w["w_proj_gdn"], "nt")
    g["w_proj_gdn"] = matmul(f"proj_gdn_dw_l{l}", s["y_gdn"], dp2, "tn", stack=(l, gbuf.get("w_proj_gdn")))
    dy_scan, dz, g["ssm_norm_w"] = rowwise_bwd(
        f"ssd_post_bwd_l{l}", f_ssd_post, t, rt, 2,
        [_row(s["y_scan"], 512, 0, True), _row(s["proj"], 512, AL_Z // 512, True),
         _par(w["ssm_norm_w"], 512, 0, True)], [True] * 3, [_row(dy_ssm, 512, 0, True)], row_dtypes={1: BF16})
    dxbc_act, dsm, g["ssm_dt_bias"], g["ssm_a_log"], g["ssm_d"] = ssd_scan_bwd(
        f"ssd_scan_bwd_l{l}", s["xbc"], s["proj"], w["ssm_dt_bias"], w["ssm_a_log"], w["ssm_d"],
        s["ssd_sin"], dy_scan)
    dxbc, g["ssm_conv_w"], g["ssm_conv_b"] = conv_bwd(
        f"ssm_conv_bwd_l{l}", s["proj"], AL_XBC, SSM_CONV_DIM, w["ssm_conv_w"], w["ssm_conv_b"], dxbc_act, tile=ct)
    do_scan, dgz, g["gdn_norm_w"] = rowwise_bwd(
        f"gdn_post_bwd_l{l}", f_gdn_post, t, rt, GDN_HEADS,
        [_row(s["o_scan"], LANE, 0, True), _row(s["proj"], LANE, AL_GZ // LANE, True),
         _par(w["gdn_norm_w"], LANE)], [True] * 3, [_row(dy_gdn, LANE, 0, True)], row_dtypes={1: BF16})
    dqkv_act, dsm, g["gdn_a_log"], g["gdn_dt_bias"] = gdn_scan_bwd(
        f"gdn_scan_bwd_l{l}", s["qkv"], s["proj"], w["gdn_a_log"], w["gdn_dt_bias"], s["gdn_sin"],
        s["gdn_x"], do_scan, dsm)
    dqkv, g["gdn_conv_w"], _ = conv_bwd(
        f"gdn_conv_bwd_l{l}", s["proj"], AL_QKV, GDN_QKV_DIM, w["gdn_conv_w"], w["gdn_conv_b"], dqkv_act, tile=ct)
    dproj = jnp.concatenate([dz, dxbc, dqkv, dgz, dgs, dgg, dsm.astype(BF16),
                             jnp.zeros((t, AL_DIM - AL_SMALL - LANE), BF16)], axis=1)
    dh = matmul(f"in_proj_dx_l{l}", dproj, w["w_in"], "nt")
    g["w_in"] = matmul(f"in_proj_dw_l{l}", s["h"], dproj, "tn", stack=(l, gbuf.get("w_in")))
    dx0, g["norm_mix_w"] = rowwise_bwd(f"norm_mix_bwd_l{l}", f_rmsnorm, t, rt, 1,
                                       [_row(s["x"], D_MODEL), _par(w["norm_mix_w"], D_MODEL)], [True, True],
                                       [_row(dh, D_MODEL)], addends={0: _row(dx1, D_MODEL)})
    return dx0, g


def _align_w_in(w):
    pad = jnp.zeros((w.shape[0], AL_DIM - AL_SMALL - 32), w.dtype)
    return jnp.concatenate([w[:, 0:2560], w[:, 2576:6672], w[:, 6688:8736],
                            w[:, 2560:2576], w[:, 6672:6688], pad], axis=1)


def _pad_lane(v, at=0):
    return jnp.pad(v[None], ((0, 0), (at, LANE - at - v.shape[0])))


def local_step(x, target, full):
    ws = []
    for l in range(DEPTH):
        ws.append({
            "norm_mix_w": full["norm_mix_w"][l][None], "w_in": _align_w_in(full["w_in"][l]),
            "ssm_conv_w": full["ssm_conv_w"][l], "ssm_conv_b": full["ssm_conv_b"][l][None],
            "ssm_dt_bias": _pad_lane(full["ssm_dt_bias"][l]), "ssm_a_log": _pad_lane(full["ssm_a_log"][l]),
            "ssm_d": _pad_lane(full["ssm_d"][l]), "ssm_norm_w": full["ssm_norm_w"][l][None],
            "gdn_conv_w": full["gdn_conv_w"][l], "gdn_conv_b": jnp.zeros((1, GDN_QKV_DIM), F32),
            "gdn_a_log": _pad_lane(full["gdn_a_log"][l], SM_A),
            "gdn_dt_bias": _pad_lane(full["gdn_dt_bias"][l], SM_A),
            "gdn_norm_w": full["gdn_norm_w"][l][None],
            "w_proj_ssm": full["w_proj_ssm"][l], "w_proj_gdn": full["w_proj_gdn"][l], "w_out": full["w_out"][l],
            "norm_ffn_w": full["norm_ffn_w"][l][None], "w_ffn_in": full["w_ffn_in"][l],
            "w_ffn_down": full["w_ffn_down"][l],
        })
    saved = []
    h = x
    for l in range(DEPTH):
        h, s = layer_fwd(l, h, ws[l])
        saved.append(s)
    loss, dx, g_final = final_loss("final_loss", h, target, full["final_norm_w"][None], tile=min(256, x.shape[0]))
    per_layer = [None] * DEPTH
    gbuf = {}
    for l in reversed(range(DEPTH)):
        dx, per_layer[l] = layer_bwd(l, dx, ws[l], saved[l], gbuf)
        gbuf = {n: per_layer[l][n] for n, _ in BIG}
    grads = {"final_norm_w": g_final[0], **gbuf}
    for name in per_layer[0]:
        if name in gbuf:
            continue
        rows = []
        for l in range(DEPTH):
            gl = per_layer[l][name]
            if name in ("ssm_dt_bias", "ssm_a_log", "ssm_d"):
                gl = gl[0, :SSM_HEADS]
            elif name in ("gdn_a_log", "gdn_dt_bias"):
                gl = gl[0, SM_A:SM_A + GDN_HEADS]
            elif name in ("norm_mix_w", "ssm_conv_b", "ssm_norm_w", "gdn_norm_w", "norm_ffn_w"):
                gl = gl[0]
            rows.append(gl)
        grads[name] = jnp.stack(rows)
    return loss, dx, grads


MESH = pl.DeviceIdType.MESH
HBM = pl.BlockSpec(memory_space=pltpu.HBM)
N_DEV = 8


def _pos():
    return lax.axis_index("x"), lax.axis_index("y"), lax.axis_index("c")


def _rcopy(src, dst, send_sem, recv_sem, dev):
    return pltpu.make_async_remote_copy(src_ref=src, dst_ref=dst, send_sem=send_sem, recv_sem=recv_sem,
                                        device_id=dev, device_id_type=MESH)


RELATIONS = (2, 1, 3)


def _related_chip(x, y, mask):
    return (1 - x if mask & 2 else x, 1 - y if mask & 1 else y)


def weights_gather(name, bufs):
    n = len(bufs)

    def body(*refs):
        outs, send_sems, recv_sems = refs[n:2 * n], refs[2 * n], refs[2 * n + 1]
        x, y, c = _pos()
        sib = (x, y, 1 - c)
        sends = []
        for i, a in enumerate(outs):
            for k, m in enumerate(RELATIONS):
                px, py = _related_chip(x, y, m)
                cp = _rcopy(a.at[0, c], a.at[m, c], send_sems.at[6 * i + k], recv_sems.at[6 * i + k], (px, py, c))
                cp.start()
                sends.append(cp)
        for i, a in enumerate(outs):
            for k, m in enumerate(RELATIONS):
                px, py = _related_chip(x, y, m)
                _rcopy(a.at[0, c], a.at[m, c], send_sems.at[6 * i + k], recv_sems.at[6 * i + k],
                       (px, py, c)).wait_recv()
                fw = _rcopy(a.at[m, c], a.at[m, c], send_sems.at[6 * i + 3 + k], recv_sems.at[6 * i + 3 + k], sib)
                fw.start()
                sends.append(fw)
        for i, a in enumerate(outs):
            for k, m in enumerate(RELATIONS):
                _rcopy(a.at[m, 1 - c], a.at[m, 1 - c], send_sems.at[6 * i + 3 + k], recv_sems.at[6 * i + 3 + k],
                       sib).wait_recv()
        for cp in sends:
            cp.wait_send()

    return pl.pallas_call(
        body, name=name, out_shape=[jax.ShapeDtypeStruct(b.shape, b.dtype) for b in bufs],
        in_specs=[HBM] * n, out_specs=[HBM] * n,
        input_output_aliases={i: i for i in range(n)},
        scratch_shapes=[pltpu.SemaphoreType.DMA((6 * n,)), pltpu.SemaphoreType.DMA((6 * n,))],
    )(*bufs)


def pair_swap(name, gs):
    n = len(gs)

    def body(*refs):
        srcs, outs, send_sems, recv_sems = refs[:n], refs[n:2 * n], refs[2 * n], refs[2 * n + 1]
        x, y, c = _pos()
        cps = [_rcopy(s.at[1 - c], o, send_sems.at[i], recv_sems.at[i], (x, y, 1 - c))
               for i, (s, o) in enumerate(zip(srcs, outs))]
        for cp in cps:
            cp.start()
        for cp in cps:
            cp.wait()

    return pl.pallas_call(
        body, name=name, out_shape=[jax.ShapeDtypeStruct(g.shape[1:], g.dtype) for g in gs],
        in_specs=[HBM] * n, out_specs=[HBM] * n,
        scratch_shapes=[pltpu.SemaphoreType.DMA((n,)), pltpu.SemaphoreType.DMA((n,))],
    )(*gs)


def chip_scatter(name, ss):
    n = len(ss)

    def body(*refs):
        srcs, outs, send_sems, recv_sems = refs[:n], refs[n:2 * n], refs[2 * n], refs[2 * n + 1]
        x, y, c = _pos()
        sends = []
        for i, (s, o) in enumerate(zip(srcs, outs)):
            for k, m in enumerate(RELATIONS):
                px, py = _related_chip(x, y, m)
                cp = _rcopy(s.at[2 * px + py], o.at[k], send_sems.at[3 * i + k], recv_sems.at[3 * i + k],
                            (px, py, c))
                cp.start()
                sends.append(cp)
        for cp in sends:
            cp.wait()

    return pl.pallas_call(
        body, name=name, out_shape=[jax.ShapeDtypeStruct((3,) + s.shape[1:], s.dtype) for s in ss],
        in_specs=[HBM] * n, out_specs=[HBM] * n,
        scratch_shapes=[pltpu.SemaphoreType.DMA((3 * n,)), pltpu.SemaphoreType.DMA((3 * n,))],
    )(*ss)


def pair_share(name, bufs):
    n = len(bufs)

    def body(*refs):
        outs, send_sems, recv_sems = refs[n:2 * n], refs[2 * n], refs[2 * n + 1]
        x, y, c = _pos()
        sends = []
        for i, o in enumerate(outs):
            cp = _rcopy(o.at[c], o.at[c], send_sems.at[i], recv_sems.at[i], (x, y, 1 - c))
            cp.start()
            sends.append(cp)
        for i, o in enumerate(outs):
            _rcopy(o.at[1 - c], o.at[1 - c], send_sems.at[i], recv_sems.at[i], (x, y, 1 - c)).wait_recv()
        for cp in sends:
            cp.wait_send()

    return pl.pallas_call(
        body, name=name, out_shape=[jax.ShapeDtypeStruct(b.shape, b.dtype) for b in bufs],
        in_specs=[HBM] * n, out_specs=[HBM] * n,
        input_output_aliases={i: i for i in range(n)},
        scratch_shapes=[pltpu.SemaphoreType.DMA((n,)), pltpu.SemaphoreType.DMA((n,))],
    )(*bufs)


def all_allgather(name, buf):
    r, cd = buf.shape

    def body(src, out, send_sems, recv_sems, lsem):
        x, y, c = _pos()
        me = 4 * x + 2 * y + c
        local = pltpu.make_async_copy(src, out.at[me], lsem)
        local.start()

        def peer(mask):
            px = 1 - x if mask & 4 else x
            py = 1 - y if mask & 2 else y
            pc = 1 - c if mask & 1 else c
            return px, py, pc

        sends = []
        for mask in range(1, N_DEV):
            cp = _rcopy(src, out.at[me], send_sems.at[mask - 1], recv_sems.at[mask - 1], peer(mask))
            cp.start()
            sends.append(cp)
        for mask in range(1, N_DEV):
            px, py, pc = peer(mask)
            _rcopy(src, out.at[4 * px + 2 * py + pc], send_sems.at[mask - 1], recv_sems.at[mask - 1],
                   (px, py, pc)).wait_recv()
        for cp in sends:
            cp.wait_send()
        local.wait()

    return pl.pallas_call(
        body, name=name, out_shape=jax.ShapeDtypeStruct((N_DEV, r, cd), buf.dtype),
        in_specs=[HBM], out_specs=HBM,
        scratch_shapes=[pltpu.SemaphoreType.DMA((N_DEV - 1,)), pltpu.SemaphoreType.DMA((N_DEV - 1,)),
                        pltpu.SemaphoreType.DMA(())],
    )(buf)


ELEMENTWISE_BLOCK_BYTES = 2 << 20


def _row_block(rows, cols):
    for cand in (1024, 512, 256, 128, 64, 32, 16):
        if rows % cand == 0 and cand * cols * 4 <= ELEMENTWISE_BLOCK_BYTES:
            return cand
    return rows


def chip_sum(name, s, r, me, c):
    _, a, b = s.shape
    tr = _row_block(a, b)

    def body(idx_ref, s_ref, r_ref, o_ref):
        del idx_ref
        acc = s_ref[...].astype(F32)
        for k in range(3):
            acc = acc + r_ref[k].astype(F32)
        o_ref[...] = acc

    return pl.pallas_call(
        body, name=name, out_shape=jax.ShapeDtypeStruct((2, a, b), F32),
        grid_spec=pltpu.PrefetchScalarGridSpec(
            num_scalar_prefetch=1, grid=(a // tr,),
            in_specs=[pl.BlockSpec((None, tr, b), lambda i, idx: (idx[0], i, 0)),
                      pl.BlockSpec((3, tr, b), lambda i, idx: (0, i, 0))],
            out_specs=pl.BlockSpec((None, tr, b), lambda i, idx: (idx[1], i, 0))),
        compiler_params=_cparams(("arbitrary",)),
    )(jnp.stack([me, c]).astype(jnp.int32), s, r)


def pair_add(name, p, recv, c):
    _, nj, rh, cd = p.shape
    tr = _row_block(rh, cd)

    def body(c_ref, p_ref, r_ref, o_ref):
        del c_ref
        o_ref[...] = (p_ref[0] + r_ref[...]).astype(o_ref.dtype)

    return pl.pallas_call(
        body, name=name, out_shape=jax.ShapeDtypeStruct((nj, rh, cd), BF16),
        grid_spec=pltpu.PrefetchScalarGridSpec(
            num_scalar_prefetch=1, grid=(nj, rh // tr),
            in_specs=[pl.BlockSpec((1, 1, tr, cd), lambda j, i, c_ref: (c_ref[0], j, i, 0)),
                      pl.BlockSpec((1, tr, cd), lambda j, i, c_ref: (j, i, 0))],
            out_specs=pl.BlockSpec((1, tr, cd), lambda j, i, c_ref: (j, i, 0))),
        compiler_params=_cparams(("arbitrary", "arbitrary")),
    )(jnp.reshape(c, (1,)).astype(jnp.int32), p, recv)


def slab_sum(name, a):
    n, r, cd = a.shape
    tr = _pick(r, (256, 128, 64, 32, 16, 8))

    def body(a_ref, o_ref):
        acc = a_ref[0].astype(F32)
        for j in range(1, n):
            acc = acc + a_ref[j].astype(F32)
        o_ref[...] = acc

    return pl.pallas_call(
        body, name=name, out_shape=jax.ShapeDtypeStruct((r, cd), F32),
        grid=(r // tr,),
        in_specs=[pl.BlockSpec((n, tr, cd), lambda i: (0, i, 0))],
        out_specs=pl.BlockSpec((tr, cd), lambda i: (i, 0)),
        compiler_params=_cparams(("arbitrary",)),
    )(a)


ADAM_C1 = 1.0 - ADAM_B1 ** ADAM_STEP
ADAM_C2 = 1.0 - ADAM_B2 ** ADAM_STEP


def adamw(name, w, g, m, v):
    r, cd = w.shape
    tr = r
    for cand in (512, 256, 128, 64, 32, 16, 8):
        if r % cand == 0 and cand * cd * 4 <= (1 << 20):
            tr = cand
            break

    def body(w_ref, g_ref, m_ref, v_ref, d_ref, nm_ref, nv_ref):
        gv = g_ref[...]
        nm = ADAM_B1 * m_ref[...] + (1.0 - ADAM_B1) * gv
        nv = ADAM_B2 * v_ref[...] + (1.0 - ADAM_B2) * (gv * gv)
        m_hat = nm / ADAM_C1
        v_hat = nv / ADAM_C2
        d_ref[...] = -ADAM_LR * (m_hat / (jnp.sqrt(v_hat) + ADAM_EPS) + ADAM_WD * w_ref[...])
        nm_ref[...] = nm
        nv_ref[...] = nv

    spec = pl.BlockSpec((tr, cd), lambda i: (i, 0))
    sd = jax.ShapeDtypeStruct((r, cd), F32)
    return pl.pallas_call(
        body, name=name, out_shape=[sd, sd, sd], grid=(r // tr,),
        in_specs=[spec] * 4, out_specs=[spec] * 3,
        compiler_params=_cparams(("arbitrary",)),
    )(w, g, m, v)


WEIGHTS = ("norm_mix_w", "w_in", "ssm_conv_w", "ssm_conv_b", "ssm_dt_bias", "ssm_a_log", "ssm_d", "ssm_norm_w",
           "gdn_conv_w", "gdn_a_log", "gdn_dt_bias", "gdn_norm_w", "w_proj_ssm", "w_proj_gdn", "w_out",
           "norm_ffn_w", "w_ffn_in", "w_ffn_down", "final_norm_w")
BIG = (("w_in", 2), ("w_proj_ssm", 1), ("w_proj_gdn", 1), ("w_out", 1), ("w_ffn_in", 2), ("w_ffn_down", 1))
CONVW = (("ssm_conv_w", 2), ("gdn_conv_w", 2))
SHARDED = BIG + CONVW
SMALL = tuple(n for n in WEIGHTS if n not in dict(SHARDED))


def _unpack(buf, shapes, lead=()):
    flat = buf.reshape(lead + (-1,))
    out, o = [], 0
    for shp in shapes:
        n = math.prod(shp)
        out.append(flat[..., o:o + n].reshape(lead + tuple(shp)))
        o += n
    return out


def _pack_rows(arrs, lead=()):
    nl = len(lead)
    flat = jnp.concatenate([a.reshape(lead + (-1,)) for a in arrs], axis=nl)
    n = flat.shape[nl]
    rows = -(-n // (8 * LANE)) * 8
    flat = jnp.pad(flat, [(0, 0)] * nl + [(0, rows * LANE - n)])
    return flat.reshape(lead + (rows, LANE))


def _slot_buffer(shard):
    return jnp.pad(shard[None], [(0, N_CHIPS - 1)] + [(0, 0)] * shard.ndim)


def kernel(x, norm_mix_w, w_in, ssm_conv_w, ssm_conv_b, ssm_dt_bias, ssm_a_log, ssm_d, ssm_norm_w, gdn_conv_w, gdn_a_log, gdn_dt_bias, gdn_norm_w, w_proj_ssm, w_proj_gdn, w_out, norm_ffn_w, w_ffn_in, w_ffn_down, final_norm_w, loss_target, m_norm_mix_w, m_w_in, m_ssm_conv_w, m_ssm_conv_b, m_ssm_dt_bias, m_ssm_a_log, m_ssm_d, m_ssm_norm_w, m_gdn_conv_w, m_gdn_a_log, m_gdn_dt_bias, m_gdn_norm_w, m_w_proj_ssm, m_w_proj_gdn, m_w_out, m_norm_ffn_w, m_w_ffn_in, m_w_ffn_down, m_final_norm_w, v_norm_mix_w, v_w_in, v_ssm_conv_w, v_ssm_conv_b, v_ssm_dt_bias, v_ssm_a_log, v_ssm_d, v_ssm_norm_w, v_gdn_conv_w, v_gdn_a_log, v_gdn_dt_bias, v_gdn_norm_w, v_w_proj_ssm, v_w_proj_gdn, v_w_out, v_norm_ffn_w, v_w_ffn_in, v_w_ffn_down, v_final_norm_w):
    wl = (norm_mix_w, w_in, ssm_conv_w, ssm_conv_b, ssm_dt_bias, ssm_a_log, ssm_d, ssm_norm_w, gdn_conv_w,
          gdn_a_log, gdn_dt_bias, gdn_norm_w, w_proj_ssm, w_proj_gdn, w_out, norm_ffn_w, w_ffn_in, w_ffn_down,
          final_norm_w)
    ml = (m_norm_mix_w, m_w_in, m_ssm_conv_w, m_ssm_conv_b, m_ssm_dt_bias, m_ssm_a_log, m_ssm_d, m_ssm_norm_w,
          m_gdn_conv_w, m_gdn_a_log, m_gdn_dt_bias, m_gdn_norm_w, m_w_proj_ssm, m_w_proj_gdn, m_w_out,
          m_norm_ffn_w, m_w_ffn_in, m_w_ffn_down, m_final_norm_w)
    vl = (v_norm_mix_w, v_w_in, v_ssm_conv_w, v_ssm_conv_b, v_ssm_dt_bias, v_ssm_a_log, v_ssm_d, v_ssm_norm_w,
          v_gdn_conv_w, v_gdn_a_log, v_gdn_dt_bias, v_gdn_norm_w, v_w_proj_ssm, v_w_proj_gdn, v_w_out,
          v_norm_ffn_w, v_w_ffn_in, v_w_ffn_down, v_final_norm_w)
    w = dict(zip(WEIGHTS, wl))
    m = dict(zip(WEIGHTS, ml))
    v = dict(zip(WEIGHTS, vl))
    x_pos, y_pos, c = _pos()
    me = 2 * x_pos + y_pos
    big = [n for n, _ in BIG]

    shards = [w[n].astype(BF16) for n in big]
    shards[0] = jnp.pad(shards[0], ((0, 0), (0, 0), (0, IN_SHARD_PAD - IN_SHARD)))
    conv_shapes = [w[n].shape[1:] for n, _ in CONVW]
    conv_pack = _pack_rows([w[n] for n, _ in CONVW], lead=(DEPTH,))
    gathered = weights_gather("gather_w", [_slot_buffer(s) for s in shards + [conv_pack]])
    by_chip = [[lax.dynamic_index_in_dim(g_, jnp.bitwise_xor(me, j), 0, keepdims=False) for j in range(N_CHIPS)]
               for g_ in gathered]
    full = {n: w[n] for n in SMALL}
    for i, (n, axis) in enumerate(BIG):
        cols = IN_SHARD if n == "w_in" else by_chip[i][0].shape[-1]
        full[n] = [jnp.concatenate([p[l][:, :cols] for p in by_chip[i]], axis=axis - 1) for l in range(DEPTH)]
    conv_parts = [_unpack(by_chip[-1][j], conv_shapes, lead=(DEPTH,)) for j in range(N_CHIPS)]
    for i, (n, axis) in enumerate(CONVW):
        full[n] = jnp.concatenate([conv_parts[j][i] for j in range(N_CHIPS)], axis=axis)

    loss_part, grad_x, grads = local_step(x[0], loss_target[0], full)

    by4 = [grads[n][:, None] if n == "w_in" else grads[n].reshape((DEPTH, N_CHIPS, -1) + grads[n].shape[-1:])
           for n in big]
    from_pair = pair_swap("grad_pair_swap", by4)
    chip_part = [pair_add(f"grad_pair_add_{n}", g_, r_, c) for n, g_, r_ in zip(big, by4, from_pair)]
    chip_part[0] = _aligned_to_shards(chip_part[0][0])
    from_chips = chip_scatter("grad_chip_scatter", chip_part)
    halves = [chip_sum(f"grad_chip_sum_{n}", s_, r_, me, c) for n, s_, r_ in zip(big, chip_part, from_chips)]
    reduced = pair_share("grad_pair_share", halves)
    g_sharded = dict(zip(big, reduced))
    g_sharded["w_in"] = g_sharded["w_in"][:, :, :IN_SHARD]

    small_names = list(SMALL) + [n for n, _ in CONVW]
    small_all = all_allgather("gather_small", _pack_rows([grads[n] for n in small_names] + [loss_part[0, :1]]))
    small_sum = slab_sum("small_sum", small_all)
    small_vals = _unpack(small_sum, [grads[n].shape for n in small_names] + [(1,)])
    g_small = dict(zip(small_names, small_vals[:-1]))
    loss = small_vals[-1].reshape(())
    for n, axis in CONVW:
        size = w[n].shape[axis]
        g_sharded[n] = lax.dynamic_slice_in_dim(g_small.pop(n), me * size, size, axis=axis)

    out_g, out_d, out_m, out_v = {}, {}, {}, {}
    for n, _ in SHARDED:
        shp = w[n].shape
        two = lambda a: a.reshape(-1, shp[-1])
        d_, m_, v_ = adamw(f"adamw_{n}", two(w[n]), two(g_sharded[n]), two(m[n]), two(v[n]))
        out_g[n], out_d[n], out_m[n], out_v[n] = g_sharded[n], d_.reshape(shp), m_.reshape(shp), v_.reshape(shp)
    d_, m_, v_ = adamw("adamw_small", *[_pack_rows([d[n] for n in SMALL]) for d in (w, g_small, m, v)])
    small_shapes = [w[n].shape for n in SMALL]
    for n, dd, mm, vv in zip(SMALL, _unpack(d_, small_shapes), _unpack(m_, small_shapes), _unpack(v_, small_shapes)):
        out_g[n], out_d[n], out_m[n], out_v[n] = g_small[n], dd, mm, vv

    return (loss, grad_x[None], *[out_g[n] for n in WEIGHTS], *[out_d[n] for n in WEIGHTS],
            *[out_m[n] for n in WEIGHTS], *[out_v[n] for n in WEIGHTS])
```

```python
import math

import jax
import jax.numpy as jnp
from jax import lax
from jax.experimental import pallas as pl
from jax.experimental.pallas import tpu as pltpu

F32 = jnp.float32
BF16 = jnp.bfloat16

D_MODEL = 1024
DEPTH = 2
SSM_HEADS = 16
SSM_HEAD_DIM = 64
SSM_D_INNER = 1024
SSM_STATE = 128
SSM_CONV_DIM = 1536
GDN_HEADS = 8
GDN_HEAD = 128
GDN_QKV_DIM = 3072
CONV_K = 4
CHUNK = 128
SSD_CHUNK = 256
FFN_HIDDEN = 2816
EPS = 1e-6
IN_DIM = 8736

ADAM_LR = 0.001
ADAM_B1 = 0.9
ADAM_B2 = 0.999
ADAM_EPS = 1e-08
ADAM_WD = 0.01
ADAM_STEP = 10

LANE = 128
NEG_BIG = -1e30
VMEM_LIMIT = 56 * 1024 * 1024

AL_Z, AL_XBC, AL_QKV, AL_GZ, AL_GS, AL_GG, AL_SMALL, AL_DIM = 0, 1024, 2560, 5632, 6656, 7680, 8704, 9216
SM_DT, SM_A, SM_B = 0, 16, 24

HI = lax.Precision.HIGHEST
NN = (((1,), (0,)), ((), ()))
NT = (((1,), (1,)), ((), ()))
TN = (((0,), (0,)), ((), ()))


def _cparams(sem):
    return pltpu.CompilerParams(dimension_semantics=sem, vmem_limit_bytes=VMEM_LIMIT)


def _pick(n, prefs):
    for p in prefs:
        if n % p == 0:
            return p
    return n


MATMUL_VMEM_BUDGET = 40 << 20
MXU_WIDTH = 256
HBM_BYTES_PER_S = 3.3e12
MXU_FLOPS_PER_S = 9.0e14
GRID_STEP_S = 0.35e-6
N_CHIPS = 4


def _matmul_tiles(m, n, n_dom, k, a_item, b_item, o_item):
    best = None
    def cands(dim, cap):
        return [c for c in range(LANE, min(dim, cap) + 1, LANE) if dim % c == 0] or [dim]

    tms, tns, tks = cands(m, 2048), cands(n_dom, 2304), cands(k, 1 << 30)
    for tm in tms:
        for tn in tns:
            for tk in tks:
                nk = k // tk
                vmem = (2 * (tm * tk * a_item + tk * tn * b_item + tm * tn * o_item) + tm * tn * 4 * (2 if nk > 1 else 1)
                        + (tm * tk * 2 if a_item > 2 else 0) + (tk * tn * 2 if b_item > 2 else 0))
                if vmem > MATMUL_VMEM_BUDGET:
                    continue
                traffic = m * k * a_item * (1 if nk == 1 else n // tn) + k * n * b_item * (m // tm)
                mxu_fill = tn / (-(-tn // MXU_WIDTH) * MXU_WIDTH)
                cost = (max(traffic / HBM_BYTES_PER_S, 2.0 * m * n * k / (MXU_FLOPS_PER_S * mxu_fill))
                        + (m // tm) * (n // tn) * nk * GRID_STEP_S)
                if best is None or cost < best[0]:
                    best = (cost, (tm, tn, tk))
    return best[1]


def _dot(a, b, dims=NN):
    return lax.dot_general(a.astype(BF16), b.astype(BF16), dims, preferred_element_type=F32)


def _dot3(a, b, dims=NN):
    a_hi, b_hi = a.astype(BF16), b.astype(BF16)
    a_lo = (a - a_hi.astype(F32)).astype(BF16)
    b_lo = (b - b_hi.astype(F32)).astype(BF16)

    def dg(u, v):
        return lax.dot_general(u, v, dims, preferred_element_type=F32)

    return dg(a_hi, b_hi) + (dg(a_hi, b_lo) + dg(a_lo, b_hi))


def _dot_hi(a, b, dims=NN):
    return lax.dot_general(a, b, dims, precision=HI, preferred_element_type=F32)


def _sigmoid(x):
    return jax.nn.sigmoid(x)


def _silu(x):
    return x * _sigmoid(x)


def _softplus(x):
    return jnp.maximum(x, 0.0) + jnp.log1p(jnp.exp(-jnp.abs(x)))


def matmul(name, a, b, mode, out_dtype=F32, chip_major=False, stack=None):
    if mode == "nn":
        (m, k), (k2, n) = a.shape, b.shape
    elif mode == "nt":
        (m, k), (n, k2) = a.shape, b.shape
    else:
        (k, m), (k2, n) = a.shape, b.shape
    assert k == k2, (a.shape, b.shape, mode)
    shard = n // N_CHIPS if chip_major else n
    tm, tn, tk = _matmul_tiles(m, n, shard, k, a.dtype.itemsize, b.dtype.itemsize, jnp.dtype(out_dtype).itemsize)
    if chip_major:
        per = shard // tn
        base_shape, base_blk = (N_CHIPS, m, shard), (None, tm, tn)
        base_idx = lambda i, j: (j // per, i, j % per)
    else:
        base_shape, base_blk = (m, n), (tm, tn)
        base_idx = lambda i, j: (i, j)
    nk = k // tk
    dims = {"nn": NN, "nt": NT, "tn": TN}[mode]

    def body_acc(a_ref, b_ref, o_ref, acc_ref):
        kk = pl.program_id(2)

        @pl.when(kk == 0)
        def _():
            acc_ref[...] = jnp.zeros_like(acc_ref)

        acc_ref[...] += _dot(a_ref[...], b_ref[...], dims)

        @pl.when(kk == nk - 1)
        def _():
            o_ref[...] = acc_ref[...].astype(o_ref.dtype)

    def body_one(a_ref, b_ref, o_ref):
        o_ref[...] = _dot(a_ref[...], b_ref[...], dims).astype(o_ref.dtype)

    compute = body_one if nk == 1 else body_acc
    if mode == "tn":
        a_spec = pl.BlockSpec((tk, tm), lambda i, j, kk: (kk, i))
    else:
        a_spec = pl.BlockSpec((tm, tk), lambda i, j, kk: (i, kk))
    if mode == "nt":
        b_spec = pl.BlockSpec((tn, tk), lambda i, j, kk: (j, kk))
    else:
        b_spec = pl.BlockSpec((tk, tn), lambda i, j, kk: (kk, j))
    in_specs, operands, aliases, body = [a_spec, b_spec], [a, b], {}, compute
    if stack is None:
        out_shape, out_blk, out_idx = base_shape, base_blk, (lambda i, j, kk: base_idx(i, j))
    else:
        layer, buf = stack
        out_shape, out_blk = (DEPTH,) + base_shape, (None,) + base_blk
        out_idx = lambda i, j, kk: (layer,) + base_idx(i, j)
        if buf is not None:
            assert buf.shape == out_shape and buf.dtype == out_dtype
            in_specs.append(pl.BlockSpec(memory_space=pl.ANY))
            operands.append(buf)
            aliases = {2: 0}

            def body(a_ref, b_ref, buf_ref, *rest):
                del buf_ref
                compute(a_ref, b_ref, *rest)

    return pl.pallas_call(
        body, name=name,
        out_shape=jax.ShapeDtypeStruct(out_shape, out_dtype),
        grid=(m // tm, n // tn, nk),
        in_specs=in_specs,
        out_specs=pl.BlockSpec(out_blk, out_idx),
        scratch_shapes=[] if nk == 1 else [pltpu.VMEM((tm, tn), F32)],
        input_output_aliases=aliases,
        compiler_params=_cparams(("parallel", "parallel", "arbitrary")),
    )(*operands)


def _row_map(c0, moves):
    return (lambda j, i: (i, c0 + j)) if moves else (lambda j, i: (i, c0))


def _par_map(c0, moves):
    return (lambda j, i: (0, c0 + j)) if moves else (lambda j, i: (0, c0))


def _in_spec(op, tile):
    _, kind, w, c0, moves = op
    if kind == "row":
        return pl.BlockSpec((tile, w), _row_map(c0, moves))
    return pl.BlockSpec((1, w), _par_map(c0, moves))


ROW_BLOCK_ELEMS = 1 << 18


def _row_tile(t, tile, ops):
    width = max(op[2] for op in ops if op[1] == "row")
    return min(t, max(tile, ROW_BLOCK_ELEMS // width))


def rowwise_fwd(name, fn, t, tile, ncol, ins, outs):
    n_in = len(ins)
    tile = _row_tile(t, tile, ins)

    def body(*refs):
        vals = [r[...].astype(F32) for r in refs[:n_in]]
        res = fn(*vals)
        if not isinstance(res, (tuple, list)):
            res = (res,)
        for r, v in zip(refs[n_in:], res):
            r[...] = v.astype(r.dtype)

    res = pl.pallas_call(
        body, name=name,
        out_shape=[jax.ShapeDtypeStruct((t, w * ncol), dt) for w, dt in outs],
        grid=(ncol, t // tile),
        in_specs=[_in_spec(op, tile) for op in ins],
        out_specs=[pl.BlockSpec((tile, w), _row_map(0, True)) for w, _ in outs],
        compiler_params=_cparams(("arbitrary", "arbitrary")),
    )(*[op[0] for op in ins])
    return res


def rowwise_bwd(name, fn, t, tile, ncol, ins, need, cts, addends=None, row_dtypes=None):
    n_in, n_ct = len(ins), len(cts)
    tile = _row_tile(t, tile, ins)
    addends = addends or {}
    row_dtypes = row_dtypes or {}
    didx = [i for i in range(n_in) if need[i]]
    add_ops = [addends[i] for i in didx if i in addends]
    n_add = len(add_ops)

    def body(*refs):
        in_refs = refs[:n_in]
        ct_refs = refs[n_in:n_in + n_ct]
        add_refs = refs[n_in + n_ct:n_in + n_ct + n_add]
        out_refs = refs[n_in + n_ct + n_add:]
        vals = [r[...].astype(F32) for r in in_refs]

        def g(*dv):
            full = list(vals)
            for i, v in zip(didx, dv):
                full[i] = v
            res = fn(*full)
            return tuple(res) if isinstance(res, (tuple, list)) else (res,)

        _, vjp = jax.vjp(g, *[vals[i] for i in didx])
        grads = vjp(tuple(c[...].astype(F32) for c in ct_refs))
        j, i = pl.program_id(0), pl.program_id(1)
        a = 0
        for o_ref, gv, idx in zip(out_refs, grads, didx):
            _, kind, _, _, moves = ins[idx]
            if kind == "row":
                if idx in addends:
                    gv = gv + add_refs[a][...].astype(F32)
                    a += 1
                o_ref[...] = gv.astype(o_ref.dtype)
            else:
                first = (i == 0) if moves else jnp.logical_and(i == 0, j == 0)

                @pl.when(first)
                def _(o_ref=o_ref, gv=gv):
                    o_ref[...] = gv

                @pl.when(jnp.logical_not(first))
                def _(o_ref=o_ref, gv=gv):
                    o_ref[...] += gv

    out_shape, out_specs = [], []
    for idx in didx:
        _, kind, w, _, moves = ins[idx]
        cols = w * (ncol if moves else 1)
        if kind == "row":
            out_shape.append(jax.ShapeDtypeStruct((t, cols), row_dtypes.get(idx, F32)))
            out_specs.append(pl.BlockSpec((tile, w), _row_map(0, moves)))
        else:
            out_shape.append(jax.ShapeDtypeStruct((1, cols), F32))
            out_specs.append(pl.BlockSpec((1, w), _par_map(0, moves)))
    ops = list(ins) + list(cts) + add_ops
    res = pl.pallas_call(
        body, name=name,
        out_shape=out_shape,
        grid=(ncol, t // tile),
        in_specs=[_in_spec(op, tile) for op in ops],
        out_specs=out_specs,
        compiler_params=_cparams(("arbitrary", "arbitrary")),
    )(*[op[0] for op in ops])
    return res


def f_rmsnorm(x, w):
    return x * lax.rsqrt(jnp.mean(x * x, axis=-1, keepdims=True) + EPS) * w


def f_ssd_post(y, z, w):
    y = y * _silu(z)
    return y * lax.rsqrt(jnp.mean(y * y, axis=-1, keepdims=True) + EPS) * w


def f_gdn_post(o, z, w):
    o = o * lax.rsqrt(jnp.mean(o * o, axis=-1, keepdims=True) + EPS) * w
    return o * _silu(z)


def f_merge(gs, p1, gg, p2):
    return _sigmoid(gs) * p1 + _sigmoid(gg) * p2


def f_swiglu(g, u):
    return _silu(g) * u


def final_loss(name, x, tgt, w, tile=256):
    t, d = x.shape

    def body(x_ref, t_ref, w_ref, loss_ref, dx_ref, dw_ref):
        i = pl.program_id(0)
        xv, tv, wv = x_ref[...], t_ref[...], w_ref[...]

        def g(xx, ww):
            err = f_rmsnorm(xx, ww) - tv
            return 0.5 * jnp.sum(jnp.mean(err * err, axis=-1, keepdims=True), axis=0, keepdims=True)

        val, vjp = jax.vjp(g, xv, wv)
        dx, dw = vjp(jnp.ones((1, 1), F32))
        dx_ref[...] = dx
        lv = jnp.broadcast_to(val, (1, LANE))

        @pl.when(i == 0)
        def _():
            loss_ref[...] = lv
            dw_ref[...] = dw

        @pl.when(i != 0)
        def _():
            loss_ref[...] += lv
            dw_ref[...] += dw

    return pl.pallas_call(
        body, name=name,
        out_shape=[jax.ShapeDtypeStruct((1, LANE), F32), jax.ShapeDtypeStruct((t, d), F32),
                   jax.ShapeDtypeStruct((1, d), F32)],
        grid=(t // tile,),
        in_specs=[pl.BlockSpec((tile, d), lambda i: (i, 0)), pl.BlockSpec((tile, d), lambda i: (i, 0)),
                  pl.BlockSpec((1, d), lambda i: (0, 0))],
        out_specs=[pl.BlockSpec((1, LANE), lambda i: (0, 0)), pl.BlockSpec((tile, d), lambda i: (i, 0)),
                   pl.BlockSpec((1, d), lambda i: (0, 0))],
        compiler_params=_cparams(("arbitrary",)),
    )(x, tgt, w)


CONV_W = 512
HALO = 8
STRIPS = 4


def _rows_back(before, cur, d):
    rows = lax.broadcasted_iota(jnp.int32, cur.shape, 0)
    return jnp.where(rows < d, pltpu.roll(before, d, 0), pltpu.roll(cur, d, 0))


def _rows_ahead(cur, after, d):
    rows = lax.broadcasted_iota(jnp.int32, cur.shape, 0)
    return jnp.where(rows < HALO - d, pltpu.roll(cur, HALO - d, 0), pltpu.roll(after, HALO - d, 0))


def _conv_taps(taps, bias, before, cur):
    shifted = [_rows_back(before, cur, CONV_K - 1 - k) for k in range(CONV_K - 1)] + [cur]
    pre = bias + taps[CONV_K - 1] * cur
    for k in range(CONV_K - 1):
        pre = pre + taps[k] * shifted[k]
    return pre, shifted


def conv_fwd(name, src, c0, width, w, b, tile=512):
    t = src.shape[0]
    ncol, nrow = width // CONV_W, t // tile
    cb0 = c0 // CONV_W
    hb = tile // HALO

    def body(prev_ref, cur_ref, w_ref, b_ref, o_ref):
        i = pl.program_id(1)
        taps = [w_ref[k:k + 1, :] for k in range(CONV_K)]
        bias = b_ref[...]

        def strips(g, before):
            for u in range(STRIPS):
                r0 = pl.multiple_of((g * STRIPS + u) * HALO, HALO)
                cur = cur_ref[pl.ds(r0, HALO), :]
                pre, _ = _conv_taps(taps, bias, before, cur)
                o_ref[pl.ds(r0, HALO), :] = _silu(pre)
                before = cur
            return before

        lax.fori_loop(0, tile // (HALO * STRIPS), strips, jnp.where(i == 0, 0.0, prev_ref[...]))

    return pl.pallas_call(
        body, name=name,
        out_shape=jax.ShapeDtypeStruct((t, width), F32),
        grid=(ncol, nrow),
        in_specs=[pl.BlockSpec((HALO, CONV_W), lambda j, i: (jnp.maximum(i * hb - 1, 0), cb0 + j)),
                  pl.BlockSpec((tile, CONV_W), lambda j, i: (i, cb0 + j)),
                  pl.BlockSpec((CONV_K, CONV_W), lambda j, i: (0, j)),
                  pl.BlockSpec((1, CONV_W), lambda j, i: (0, j))],
        out_specs=pl.BlockSpec((tile, CONV_W), lambda j, i: (i, j)),
        compiler_params=_cparams(("arbitrary", "arbitrary")),
    )(src, src, w, b)


def conv_bwd(name, src, c0, width, w, b, dy, tile=512):
    t = src.shape[0]
    ncol, nrow = width // CONV_W, t // tile
    cb0 = c0 // CONV_W
    hb = tile // HALO
    last_hb = t // HALO - 1
    nstrip = tile // HALO

    def body(sprev_ref, scur_ref, snext_ref, w_ref, b_ref, dycur_ref, dynext_ref,
             du_ref, dw_ref, db_ref, dpre_ref):
        i = pl.program_id(1)
        taps = [w_ref[k:k + 1, :] for k in range(CONV_K)]
        bias = b_ref[...]

        def dpre_of(before, cur, dy_strip):
            pre, shifted = _conv_taps(taps, bias, before, cur)
            s = _sigmoid(pre)
            return dy_strip * (s * (1.0 + pre * (1.0 - s))), shifted

        def strips1(g, carry):
            before, dws, dbs = carry
            for u in range(STRIPS):
                r0 = pl.multiple_of((g * STRIPS + u) * HALO, HALO)
                cur = scur_ref[pl.ds(r0, HALO), :]
                dpre, shifted = dpre_of(before, cur, dycur_ref[pl.ds(r0, HALO), :])
                dpre_ref[pl.ds(r0, HALO), :] = dpre
                before, dws, dbs = cur, tuple(a + dpre * v for a, v in zip(dws, shifted)), dbs + dpre
            return before, dws, dbs

        zero = jnp.zeros((HALO, CONV_W), F32)
        before, dws, dbs = lax.fori_loop(0, nstrip // STRIPS, strips1,
                                         (jnp.where(i == 0, 0.0, sprev_ref[...]), (zero,) * CONV_K, zero))
        dpre_next, _ = dpre_of(before, snext_ref[...], dynext_ref[...])
        dpre_ref[pl.ds(tile, HALO), :] = jnp.where(i == nrow - 1, 0.0, dpre_next)

        def strips2(g, _):
            parts = []
            for u in range(STRIPS):
                r0 = pl.multiple_of((g * STRIPS + u) * HALO, HALO)
                cur = dpre_ref[pl.ds(r0, HALO), :]
                after = dpre_ref[pl.ds(r0 + HALO, HALO), :]
                acc = taps[CONV_K - 1] * cur
                for d in range(1, CONV_K):
                    acc = acc + taps[CONV_K - 1 - d] * _rows_ahead(cur, after, d)
                parts.append(acc)
            r0 = pl.multiple_of(g * STRIPS * HALO, STRIPS * HALO)
            du_ref[pl.ds(r0, STRIPS * HALO), :] = jnp.concatenate(parts, axis=0).astype(du_ref.dtype)
            return 0

        lax.fori_loop(0, nstrip // STRIPS, strips2, 0)
        dw_tile = jnp.concatenate([jnp.sum(a, axis=0, keepdims=True) for a in dws], axis=0)
        db_tile = jnp.sum(dbs, axis=0, keepdims=True)
        _acc(dw_ref, dw_tile, i == 0)
        _acc(db_ref, db_tile, i == 0)

    return pl.pallas_call(
        body, name=name,
        out_shape=[jax.ShapeDtypeStruct((t, width), BF16), jax.ShapeDtypeStruct((CONV_K, width), F32),
                   jax.ShapeDtypeStruct((1, width), F32)],
        grid=(ncol, nrow),
        in_specs=[pl.BlockSpec((HALO, CONV_W), lambda j, i: (jnp.maximum(i * hb - 1, 0), cb0 + j)),
                  pl.BlockSpec((tile, CONV_W), lambda j, i: (i, cb0 + j)),
                  pl.BlockSpec((HALO, CONV_W), lambda j, i: (jnp.minimum((i + 1) * hb, last_hb), cb0 + j)),
                  pl.BlockSpec((CONV_K, CONV_W), lambda j, i: (0, j)),
                  pl.BlockSpec((1, CONV_W), lambda j, i: (0, j)),
                  pl.BlockSpec((tile, CONV_W), lambda j, i: (i, j)),
                  pl.BlockSpec((HALO, CONV_W), lambda j, i: (jnp.minimum((i + 1) * hb, last_hb), j))],
        out_specs=[pl.BlockSpec((tile, CONV_W), lambda j, i: (i, j)),
                   pl.BlockSpec((CONV_K, CONV_W), lambda j, i: (0, j)),
                   pl.BlockSpec((1, CONV_W), lambda j, i: (0, j))],
        scratch_shapes=[pltpu.VMEM((tile + HALO, CONV_W), F32)],
        compiler_params=_cparams(("arbitrary", "arbitrary")),
    )(src, src, src, w, b, dy, dy)


def _iota2(q):
    return (lax.broadcasted_iota(jnp.int32, (q, q), 0), lax.broadcasted_iota(jnp.int32, (q, q), 1))


def _lane_pick(blk, idx):
    lane = lax.broadcasted_iota(jnp.int32, (1, LANE), 1)
    return jnp.sum(jnp.where(lane == idx, blk, 0.0), axis=1, keepdims=True)


class _Decay:
    def __init__(self, a):
        q = a.shape[0]
        r, c = _iota2(q)
        self.r, self.c = r, c
        self.cum = _dot_hi((c <= r).astype(F32), a)
        self.cum_t = _dot_hi(a, (r <= c).astype(F32), TN)
        self.tot = self.cum[q - 1:q, :]
        self.e_cum = jnp.exp(self.cum)
        self.e_rest = jnp.exp(self.tot - self.cum)
        self.e_tot = jnp.exp(self.tot)

    def mask(self, lane):
        rows = lax.broadcasted_iota(jnp.int32, (LANE, 1), 0)
        cum_row = jnp.sum(jnp.where(rows == lane, self.cum_t, 0.0), axis=0, keepdims=True)
        return jnp.exp(jnp.where(self.r >= self.c, _lane_pick(self.cum, lane) - cum_row, NEG_BIG))


_SSD_B = SSM_D_INNER
_SSD_C = SSM_D_INNER + 2 * SSM_STATE


def _interleave(gens):
    results = [None] * len(gens)
    live = list(range(len(gens)))
    while live:
        for i in list(live):
            try:
                next(gens[i])
            except StopIteration as stop:
                results[i] = stop.value
                live.remove(i)
    return results


def ssd_chunk(xs, bm, cm, dt_all, dsk, dec, state, p, cb):
    lane = lax.broadcasted_iota(jnp.int32, (1, LANE), 1)
    m0 = lane < SSM_HEAD_DIM
    h0, h1 = 2 * p, 2 * p + 1

    def both(blk):
        return jnp.where(m0, _lane_pick(blk, h0), _lane_pick(blk, h1))

    xdt = xs * both(dt_all)
    l0, l1 = dec.mask(h0), dec.mask(h1)
    yield
    y_diag = _dot(cb * l0, jnp.where(m0, xdt, 0.0)) + _dot(cb * l1, jnp.where(m0, 0.0, xdt))
    y_off = _dot(cm, state, NT) * both(dec.e_cum)
    yield
    rowm = lax.broadcasted_iota(jnp.int32, (LANE, 1), 0) < SSM_HEAD_DIM
    new_state = (state * jnp.where(rowm, _lane_pick(dec.e_tot, h0), _lane_pick(dec.e_tot, h1))
                 + _dot(xdt * both(dec.e_rest), bm, TN))
    y = y_diag + y_off + both(dsk) * xs
    return y, new_state


def ssd_pairs(xs, bms, cms, small, dtb, alog, dsk, states):
    dt_all = _softplus(small + dtb)
    dec = _Decay(dt_all * (-jnp.exp(alog)))
    cbs = [_dot(cm, bm, NT) for cm, bm in zip(cms, bms)]
    res = _interleave([ssd_chunk(x, bms[p // 4], cms[p // 4], dt_all, dsk, dec, st, p, cbs[p // 4])
                       for p, (x, st) in enumerate(zip(xs, states))])
    return tuple(y for y, _ in res), tuple(s for _, s in res)


def tri_inverse(a):
    q = a.shape[0]
    r, c = _iota2(q)
    eye = (r == c).astype(F32)
    diag = (r // 16) == (c // 16)
    bd = jnp.where(diag, a, 0.0)
    off = jnp.where(diag, 0.0, a)
    b2 = _dot3(bd, bd)
    d1 = _dot3(eye - bd, eye + b2)
    yield
    b4 = _dot3(b2, b2)
    yield
    b8 = _dot3(b4, b4)
    d2 = _dot3(d1, eye + b4)
    yield
    dinv = _dot3(d2, eye + b8)
    yield
    n = _dot3(dinv, off)
    yield
    powers = [n]
    while 16 * 2 ** len(powers) < q:
        powers.append(_dot3(powers[-1], powers[-1]))
        yield
    m = dinv
    for pw in reversed(powers[1:]):
        m = _dot3(eye + pw, m)
        yield
    return _dot3(eye - n, m)


@jax.custom_vjp
def _solve_with(xinv, a, rhs):
    del a
    return _dot3(xinv, rhs)


def _solve_with_fwd(xinv, a, rhs):
    t = _dot3(xinv, rhs)
    return t, (xinv, t)


def _solve_with_bwd(res, dt):
    xinv, t = res
    d_rhs = _dot3(xinv, dt, TN)
    d_a = -_dot(d_rhs, t, NT)
    return jnp.zeros_like(xinv), d_a, d_rhs


_solve_with.defvjp(_solve_with_fwd, _solve_with_bwd)


_GDN_K = GDN_HEADS * GDN_HEAD
_GDN_V = 2 * GDN_HEADS * GDN_HEAD


def gdn_chunk(qh, kh, vh, beta_all, dec, state, h, xinv=None):
    r, c = dec.r, dec.c
    qn = qh * lax.rsqrt(jnp.sum(qh * qh, axis=-1, keepdims=True) + EPS) * (GDN_HEAD ** -0.5)
    kn = kh * lax.rsqrt(jnp.sum(kh * kh, axis=-1, keepdims=True) + EPS)
    beta = _lane_pick(beta_all, SM_B + h)
    decay = dec.mask(SM_A + h)
    yield
    kk = _dot(kn, kn, NT)
    qk = _dot(qn, kn, NT) * decay
    amat = jnp.where(r > c, kk * decay * beta, 0.0)
    eg = _lane_pick(dec.e_cum, SM_A + h)
    rhs = jnp.concatenate([vh * beta, kn * (beta * eg)], axis=1)
    qs = _dot(qn * eg, state)
    yield
    if xinv is None:
        xinv = yield from tri_inverse(amat)
        t = _dot3(xinv, rhs)
    else:
        t = _solve_with(xinv, amat, rhs)
    yield
    u, w = t[:, :GDN_HEAD], t[:, GDN_HEAD:]
    v_new = u - _dot(w, state)
    yield
    o = qs + _dot(qk, v_new)
    new_state = (state * _lane_pick(dec.e_tot, SM_A + h)
                 + _dot(kn * _lane_pick(dec.e_rest, SM_A + h), v_new, TN))
    return o, new_state, xinv


def gdn_heads(qs, ks, vs, small, alog, dtb, states, xinvs=None):
    nh = len(qs)
    beta_all = _sigmoid(small)
    dec = _Decay(-jnp.exp(alog) * _softplus(small + dtb))
    res = _interleave([gdn_chunk(qs[h], ks[h], vs[h], beta_all, dec, states[h], h,
                                 None if xinvs is None else xinvs[h]) for h in range(nh)])
    return tuple(o for o, _, _ in res), tuple(s for _, s, _ in res), tuple(x for _, _, x in res)


def _acc(ref, val, first):
    @pl.when(first)
    def _():
        ref[...] = val

    @pl.when(jnp.logical_not(first))
    def _():
        ref[...] += val


def ssd_scan_fwd(name, xbc, proj, dtb, alog, dsk):
    t = xbc.shape[0]
    nc, npair = t // SSD_CHUNK, SSM_HEADS // 2
    small_blk = AL_SMALL // LANE

    def body(xbc_ref, sm_ref, dtb_ref, alog_ref, dsk_ref, y_ref, sin_ref, st_ref):
        ci = pl.program_id(0)

        @pl.when(ci == 0)
        def _():
            st_ref[...] = jnp.zeros_like(st_ref)

        s_in = tuple(st_ref[p] for p in range(npair))
        ys, s_new = ssd_pairs(tuple(xbc_ref[:, p * LANE:(p + 1) * LANE] for p in range(npair)),
                              tuple(xbc_ref[:, _SSD_B + g * LANE:_SSD_B + (g + 1) * LANE] for g in range(2)),
                              tuple(xbc_ref[:, _SSD_C + g * LANE:_SSD_C + (g + 1) * LANE] for g in range(2)),
                              sm_ref[...], dtb_ref[...], alog_ref[...], dsk_ref[...], s_in)
        for p in range(npair):
            sin_ref[0, p] = s_in[p]
            y_ref[:, p * LANE:(p + 1) * LANE] = ys[p]
            st_ref[p] = s_new[p]

    par = pl.BlockSpec((1, LANE), lambda ci: (0, 0))
    return pl.pallas_call(
        body, name=name,
        out_shape=[jax.ShapeDtypeStruct((t, SSM_D_INNER), F32),
                   jax.ShapeDtypeStruct((nc, npair, LANE, LANE), F32)],
        grid=(nc,),
        in_specs=[pl.BlockSpec((SSD_CHUNK, SSM_CONV_DIM), lambda ci: (ci, 0)),
                  pl.BlockSpec((SSD_CHUNK, LANE), lambda ci: (ci, small_blk)),
                  par, par, par],
        out_specs=[pl.BlockSpec((SSD_CHUNK, SSM_D_INNER), lambda ci: (ci, 0)),
                   pl.BlockSpec((1, npair, LANE, LANE), lambda ci: (ci, 0, 0, 0))],
        scratch_shapes=[pltpu.VMEM((npair, LANE, LANE), F32)],
        compiler_params=_cparams(("arbitrary",)),
    )(xbc, proj, dtb, alog, dsk)


def ssd_scan_bwd(name, xbc, proj, dtb, alog, dsk, s_in, dy):
    t = xbc.shape[0]
    nc, npair = t // SSD_CHUNK, SSM_HEADS // 2
    small_blk = AL_SMALL // LANE

    def body(xbc_ref, sm_ref, dtb_ref, alog_ref, dsk_ref, sin_ref, dy_ref,
             dxbc_ref, dsm_ref, ddtb_ref, dalog_ref, ddsk_ref, dst_ref):
        ci = pl.program_id(0)

        @pl.when(ci == 0)
        def _():
            dst_ref[...] = jnp.zeros_like(dst_ref)

        _, vjp = jax.vjp(ssd_pairs, tuple(xbc_ref[:, p * LANE:(p + 1) * LANE] for p in range(npair)),
                         tuple(xbc_ref[:, _SSD_B + g * LANE:_SSD_B + (g + 1) * LANE] for g in range(2)),
                         tuple(xbc_ref[:, _SSD_C + g * LANE:_SSD_C + (g + 1) * LANE] for g in range(2)),
                         sm_ref[...], dtb_ref[...], alog_ref[...], dsk_ref[...],
                         tuple(sin_ref[0, p] for p in range(npair)))
        dxs, dbms, dcms, dsm, ddtb, dalog, ddsk, dsts = vjp(
            (tuple(dy_ref[:, p * LANE:(p + 1) * LANE] for p in range(npair)),
             tuple(dst_ref[p] for p in range(npair))))
        for p in range(npair):
            dxbc_ref[:, p * LANE:(p + 1) * LANE] = dxs[p]
            dst_ref[p] = dsts[p]
        for g in range(2):
            dxbc_ref[:, _SSD_B + g * LANE:_SSD_B + (g + 1) * LANE] = dbms[g]
            dxbc_ref[:, _SSD_C + g * LANE:_SSD_C + (g + 1) * LANE] = dcms[g]
        dsm_ref[...] = dsm
        _acc(ddtb_ref, ddtb, ci == 0)
        _acc(dalog_ref, dalog, ci == 0)
        _acc(ddsk_ref, ddsk, ci == 0)

    par = pl.BlockSpec((1, LANE), lambda ci: (0, 0))
    rev = lambda ci: nc - 1 - ci
    return pl.pallas_call(
        body, name=name,
        out_shape=[jax.ShapeDtypeStruct((t, SSM_CONV_DIM), F32),
                   jax.ShapeDtypeStruct((t, LANE), F32),
                   jax.ShapeDtypeStruct((1, LANE), F32), jax.ShapeDtypeStruct((1, LANE), F32),
                   jax.ShapeDtypeStruct((1, LANE), F32)],
        grid=(nc,),
        in_specs=[pl.BlockSpec((SSD_CHUNK, SSM_CONV_DIM), lambda ci: (rev(ci), 0)),
                  pl.BlockSpec((SSD_CHUNK, LANE), lambda ci: (rev(ci), small_blk)),
                  par, par, par,
                  pl.BlockSpec((1, npair, LANE, LANE), lambda ci: (rev(ci), 0, 0, 0)),
                  pl.BlockSpec((SSD_CHUNK, SSM_D_INNER), lambda ci: (rev(ci), 0))],
        out_specs=[pl.BlockSpec((SSD_CHUNK, SSM_CONV_DIM), lambda ci: (rev(ci), 0)),
                   pl.BlockSpec((SSD_CHUNK, LANE), lambda ci: (rev(ci), 0)),
                   par, par, par],
        scratch_shapes=[pltpu.VMEM((npair, LANE, LANE), F32)],
        compiler_params=_cparams(("arbitrary",)),
    )(xbc, proj, dtb, alog, dsk, s_in, dy)


def gdn_scan_fwd(name, qkv, proj, alog, dtb):
    t = qkv.shape[0]
    nc, nh = t // CHUNK, GDN_HEADS
    small_blk = AL_SMALL // LANE

    def body(qkv_ref, sm_ref, alog_ref, dtb_ref, o_ref, sin_ref, x_ref, st_ref):
        ci = pl.program_id(0)

        @pl.when(ci == 0)
        def _():
            st_ref[...] = jnp.zeros_like(st_ref)

        s_in = tuple(st_ref[h] for h in range(nh))
        os, s_new, xinvs = gdn_heads(
            tuple(qkv_ref[:, h * LANE:(h + 1) * LANE] for h in range(nh)),
            tuple(qkv_ref[:, _GDN_K + h * LANE:_GDN_K + (h + 1) * LANE] for h in range(nh)),
            tuple(qkv_ref[:, _GDN_V + h * LANE:_GDN_V + (h + 1) * LANE] for h in range(nh)),
            sm_ref[...], alog_ref[...], dtb_ref[...], s_in)
        for h in range(nh):
            sin_ref[0, h] = s_in[h]
            o_ref[:, h * LANE:(h + 1) * LANE] = os[h]
            x_ref[0, h] = xinvs[h]
            st_ref[h] = s_new[h]

    par = pl.BlockSpec((1, LANE), lambda ci: (0, 0))
    return pl.pallas_call(
        body, name=name,
        out_shape=[jax.ShapeDtypeStruct((t, GDN_HEADS * GDN_HEAD), F32),
                   jax.ShapeDtypeStruct((nc, nh, LANE, LANE), F32),
                   jax.ShapeDtypeStruct((nc, nh, CHUNK, CHUNK), F32)],
        grid=(nc,),
        in_specs=[pl.BlockSpec((CHUNK, GDN_QKV_DIM), lambda ci: (ci, 0)),
                  pl.BlockSpec((CHUNK, LANE), lambda ci: (ci, small_blk)),
                  par, par],
        out_specs=[pl.BlockSpec((CHUNK, GDN_HEADS * GDN_HEAD), lambda ci: (ci, 0)),
                   pl.BlockSpec((1, nh, LANE, LANE), lambda ci: (ci, 0, 0, 0)),
                   pl.BlockSpec((1, nh, CHUNK, CHUNK), lambda ci: (ci, 0, 0, 0))],
        scratch_shapes=[pltpu.VMEM((nh, LANE, LANE), F32)],
        compiler_params=_cparams(("arbitrary",)),
    )(qkv, proj, alog, dtb)


def gdn_scan_bwd(name, qkv, proj, alog, dtb, s_in, xinv, do, dsm_in):
    t = qkv.shape[0]
    nc, nh = t // CHUNK, GDN_HEADS
    small_blk = AL_SMALL // LANE

    def body(qkv_ref, sm_ref, alog_ref, dtb_ref, sin_ref, x_ref, do_ref, dsmi_ref,
             dqkv_ref, dsm_ref, dalog_ref, ddtb_ref, dst_ref):
        ci = pl.program_id(0)

        @pl.when(ci == 0)
        def _():
            dst_ref[...] = jnp.zeros_like(dst_ref)

        xis = tuple(x_ref[0, h] for h in range(nh))

        def fn(qs, ks, vs, sm, alog_, dtb_, sts):
            os, s_new, _ = gdn_heads(qs, ks, vs, sm, alog_, dtb_, sts, xinvs=xis)
            return os, s_new

        _, vjp = jax.vjp(fn, tuple(qkv_ref[:, h * LANE:(h + 1) * LANE] for h in range(nh)),
                         tuple(qkv_ref[:, _GDN_K + h * LANE:_GDN_K + (h + 1) * LANE] for h in range(nh)),
                         tuple(qkv_ref[:, _GDN_V + h * LANE:_GDN_V + (h + 1) * LANE] for h in range(nh)),
                         sm_ref[...], alog_ref[...], dtb_ref[...], tuple(sin_ref[0, h] for h in range(nh)))
        dqs, dks, dvs, dsm, dalog, ddtb, dsts = vjp(
            (tuple(do_ref[:, h * LANE:(h + 1) * LANE] for h in range(nh)), tuple(dst_ref[h] for h in range(nh))))
        for h in range(nh):
            dqkv_ref[:, h * LANE:(h + 1) * LANE] = dqs[h]
            dqkv_ref[:, _GDN_K + h * LANE:_GDN_K + (h + 1) * LANE] = dks[h]
            dqkv_ref[:, _GDN_V + h * LANE:_GDN_V + (h + 1) * LANE] = dvs[h]
            dst_ref[h] = dsts[h]
        dsm_ref[...] = dsmi_ref[...] + dsm
        _acc(dalog_ref, dalog, ci == 0)
        _acc(ddtb_ref, ddtb, ci == 0)

    par = pl.BlockSpec((1, LANE), lambda ci: (0, 0))
    rev = lambda ci: nc - 1 - ci
    return pl.pallas_call(
        body, name=name,
        out_shape=[jax.ShapeDtypeStruct((t, GDN_QKV_DIM), F32), jax.ShapeDtypeStruct((t, LANE), F32),
                   jax.ShapeDtypeStruct((1, LANE), F32), jax.ShapeDtypeStruct((1, LANE), F32)],
        grid=(nc,),
        in_specs=[pl.BlockSpec((CHUNK, GDN_QKV_DIM), lambda ci: (rev(ci), 0)),
                  pl.BlockSpec((CHUNK, LANE), lambda ci: (rev(ci), small_blk)),
                  par, par,
                  pl.BlockSpec((1, nh, LANE, LANE), lambda ci: (rev(ci), 0, 0, 0)),
                  pl.BlockSpec((1, nh, CHUNK, CHUNK), lambda ci: (rev(ci), 0, 0, 0)),
                  pl.BlockSpec((CHUNK, GDN_HEADS * GDN_HEAD), lambda ci: (rev(ci), 0)),
                  pl.BlockSpec((CHUNK, LANE), lambda ci: (rev(ci), 0))],
        out_specs=[pl.BlockSpec((CHUNK, GDN_QKV_DIM), lambda ci: (rev(ci), 0)),
                   pl.BlockSpec((CHUNK, LANE), lambda ci: (rev(ci), 0)),
                   par, par],
        scratch_shapes=[pltpu.VMEM((nh, LANE, LANE), F32)],
        compiler_params=_cparams(("arbitrary",)),
    )(qkv, proj, alog, dtb, s_in, xinv, do, dsm_in)


def _row(arr, w, c0=0, moves=False):
    return (arr, "row", w, c0, moves)


def _par(arr, w, c0=0, moves=False):
    return (arr, "par", w, c0, moves)


def matmul_add(name, a, b, res):
    (m, k), (_, n) = a.shape, b.shape
    tm, tn, tk = _matmul_tiles(m, n, n, k, a.dtype.itemsize, b.dtype.itemsize, 4 + res.dtype.itemsize)
    nk = k // tk

    def body_acc(a_ref, b_ref, r_ref, o_ref, acc_ref):
        kk = pl.program_id(2)

        @pl.when(kk == 0)
        def _():
            acc_ref[...] = r_ref[...]

        acc_ref[...] += _dot(a_ref[...], b_ref[...])

        @pl.when(kk == nk - 1)
        def _():
            o_ref[...] = acc_ref[...]

    def body_one(a_ref, b_ref, r_ref, o_ref):
        o_ref[...] = r_ref[...] + _dot(a_ref[...], b_ref[...])

    body = body_one if nk == 1 else body_acc
    return pl.pallas_call(
        body, name=name,
        out_shape=jax.ShapeDtypeStruct((m, n), F32),
        grid=(m // tm, n // tn, nk),
        in_specs=[pl.BlockSpec((tm, tk), lambda i, j, kk: (i, kk)),
                  pl.BlockSpec((tk, tn), lambda i, j, kk: (kk, j)),
                  pl.BlockSpec((tm, tn), lambda i, j, kk: (i, j))],
        out_specs=pl.BlockSpec((tm, tn), lambda i, j, kk: (i, j)),
        scratch_shapes=[] if nk == 1 else [pltpu.VMEM((tm, tn), F32)],
        compiler_params=_cparams(("parallel", "parallel", "arbitrary")),
    )(a, b, res)


def layer_fwd(l, x, w):
    t = x.shape[0]
    rt = min(256, t)
    s = {"x": x}
    s["h"] = rowwise_fwd(f"norm_mix_l{l}", f_rmsnorm, t, rt, 1,
                         [_row(x, D_MODEL), _par(w["norm_mix_w"], D_MODEL)], [(D_MODEL, BF16)])[0]
    s["proj"] = matmul(f"in_proj_l{l}", s["h"], w["w_in"], "nn")
    s["xbc"] = conv_fwd(f"ssm_conv_l{l}", s["proj"], AL_XBC, SSM_CONV_DIM, w["ssm_conv_w"], w["ssm_conv_b"],
                        tile=min(512, t))
    s["qkv"] = conv_fwd(f"gdn_conv_l{l}", s["proj"], AL_QKV, GDN_QKV_DIM, w["gdn_conv_w"], w["gdn_conv_b"],
                        tile=min(512, t))
    s["y_scan"], s["ssd_sin"] = ssd_scan_fwd(f"ssd_scan_l{l}", s["xbc"], s["proj"], w["ssm_dt_bias"],
                                             w["ssm_a_log"], w["ssm_d"])
    s["o_scan"], s["gdn_sin"], s["gdn_x"] = gdn_scan_fwd(f"gdn_scan_l{l}", s["qkv"], s["proj"],
                                                         w["gdn_a_log"], w["gdn_dt_bias"])
    s["y_ssm"] = rowwise_fwd(f"ssd_post_l{l}", f_ssd_post, t, rt, 2,
                             [_row(s["y_scan"], 512, 0, True), _row(s["proj"], 512, AL_Z // 512, True),
                              _par(w["ssm_norm_w"], 512, 0, True)], [(512, BF16)])[0]
    s["y_gdn"] = rowwise_fwd(f"gdn_post_l{l}", f_gdn_post, t, rt, GDN_HEADS,
                             [_row(s["o_scan"], LANE, 0, True), _row(s["proj"], LANE, AL_GZ // LANE, True),
                              _par(w["gdn_norm_w"], LANE)], [(LANE, BF16)])[0]
    s["p1"] = matmul(f"proj_ssm_l{l}", s["y_ssm"], w["w_proj_ssm"], "nn")
    s["p2"] = matmul(f"proj_gdn_l{l}", s["y_gdn"], w["w_proj_gdn"], "nn")
    s["merged"] = rowwise_fwd(f"merge_l{l}", f_merge, t, rt, 2,
                              [_row(s["proj"], 512, AL_GS // 512, True), _row(s["p1"], 512, 0, True),
                               _row(s["proj"], 512, AL_GG // 512, True), _row(s["p2"], 512, 0, True)],
                              [(512, BF16)])[0]
    s["x1"] = matmul_add(f"out_proj_l{l}", s["merged"], w["w_out"], x)
    s["h2"] = rowwise_fwd(f"norm_ffn_l{l}", f_rmsnorm, t, rt, 1,
                          [_row(s["x1"], D_MODEL), _par(w["norm_ffn_w"], D_MODEL)], [(D_MODEL, BF16)])[0]
    s["gu"] = matmul(f"ffn_in_l{l}", s["h2"], w["w_ffn_in"], "nn")
    s["act"] = rowwise_fwd(f"swiglu_l{l}", f_swiglu, t, rt, FFN_HIDDEN // 256,
                           [_row(s["gu"], 256, 0, True), _row(s["gu"], 256, FFN_HIDDEN // 256, True)],
                           [(256, BF16)])[0]
    x2 = matmul_add(f"ffn_down_l{l}", s["act"], w["w_ffn_down"], s["x1"])
    return x2, s


IN_SHARD = IN_DIM // 4
IN_SHARD_PAD = 2304


def _aligned_to_shards(g):
    orig = jnp.concatenate([g[:, 0:2560], g[:, AL_SMALL:AL_SMALL + 16], g[:, 2560:6656],
                            g[:, AL_SMALL + 16:AL_SMALL + 32], g[:, 6656:8704]], axis=1)
    return jnp.stack([jnp.pad(orig[:, j * IN_SHARD:(j + 1) * IN_SHARD], ((0, 0), (0, IN_SHARD_PAD - IN_SHARD)))
                      for j in range(N_CHIPS)])


def layer_bwd(l, dx2, w, s, gbuf):
    t = dx2.shape[0]
    rt = min(256, t)
    ct = min(512, t)
    g = {}
    dact = matmul(f"ffn_down_dx_l{l}", dx2, w["w_ffn_down"], "nt")
    g["w_ffn_down"] = matmul(f"ffn_down_dw_l{l}", s["act"], dx2, "tn", stack=(l, gbuf.get("w_ffn_down")))
    nf = FFN_HIDDEN // 256
    dgate, dup = rowwise_bwd(f"swiglu_bwd_l{l}", f_swiglu, t, rt, nf,
                             [_row(s["gu"], 256, 0, True), _row(s["gu"], 256, nf, True)], [True, True],
                             [_row(dact, 256, 0, True)], row_dtypes={0: BF16, 1: BF16})
    dgu = jnp.concatenate([dgate, dup], axis=1)
    dh2 = matmul(f"ffn_in_dx_l{l}", dgu, w["w_ffn_in"], "nt")
    g["w_ffn_in"] = matmul(f"ffn_in_dw_l{l}", s["h2"], dgu, "tn", chip_major=True,
                           stack=(l, gbuf.get("w_ffn_in")))
    dx1, g["norm_ffn_w"] = rowwise_bwd(f"norm_ffn_bwd_l{l}", f_rmsnorm, t, rt, 1,
                                       [_row(s["x1"], D_MODEL), _par(w["norm_ffn_w"], D_MODEL)], [True, True],
                                       [_row(dh2, D_MODEL)], addends={0: _row(dx2, D_MODEL)})
    dmerged = matmul(f"out_proj_dx_l{l}", dx1, w["w_out"], "nt")
    g["w_out"] = matmul(f"out_proj_dw_l{l}", s["merged"], dx1, "tn", stack=(l, gbuf.get("w_out")))
    dgs, dp1, dgg, dp2 = rowwise_bwd(
        f"merge_bwd_l{l}", f_merge, t, rt, 2,
        [_row(s["proj"], 512, AL_GS // 512, True), _row(s["p1"], 512, 0, True),
         _row(s["proj"], 512, AL_GG // 512, True), _row(s["p2"], 512, 0, True)], [True] * 4,
        [_row(dmerged, 512, 0, True)], row_dtypes={0: BF16, 1: BF16, 2: BF16, 3: BF16})
    dy_ssm = matmul(f"proj_ssm_dx_l{l}", dp1, w["w_proj_ssm"], "nt")
    g["w_proj_ssm"] = matmul(f"proj_ssm_dw_l{l}", s["y_ssm"], dp1, "tn", stack=(l, gbuf.get("w_proj_ssm")))
    dy_gdn = matmul(f"proj_gdn_dx_l{l}", dp2, w["w_proj_gdn"], "nt")
    g["w_proj_gdn"] = matmul(f"proj_gdn_dw_l{l}", s["y_gdn"], dp2, "tn", stack=(l, gbuf.get("w_proj_gdn")))
    dy_scan, dz, g["ssm_norm_w"] = rowwise_bwd(
        f"ssd_post_bwd_l{l}", f_ssd_post, t, rt, 2,
        [_row(s["y_scan"], 512, 0, True), _row(s["proj"], 512, AL_Z // 512, True),
         _par(w["ssm_norm_w"], 512, 0, True)], [True] * 3, [_row(dy_ssm, 512, 0, True)], row_dtypes={1: BF16})
    dxbc_act, dsm, g["ssm_dt_bias"], g["ssm_a_log"], g["ssm_d"] = ssd_scan_bwd(
        f"ssd_scan_bwd_l{l}", s["xbc"], s["proj"], w["ssm_dt_bias"], w["ssm_a_log"], w["ssm_d"],
        s["ssd_sin"], dy_scan)
    dxbc, g["ssm_conv_w"], g["ssm_conv_b"] = conv_bwd(
        f"ssm_conv_bwd_l{l}", s["proj"], AL_XBC, SSM_CONV_DIM, w["ssm_conv_w"], w["ssm_conv_b"], dxbc_act, tile=ct)
    do_scan, dgz, g["gdn_norm_w"] = rowwise_bwd(
        f"gdn_post_bwd_l{l}", f_gdn_post, t, rt, GDN_HEADS,
        [_row(s["o_scan"], LANE, 0, True), _row(s["proj"], LANE, AL_GZ // LANE, True),
         _par(w["gdn_norm_w"], LANE)], [True] * 3, [_row(dy_gdn, LANE, 0, True)], row_dtypes={1: BF16})
    dqkv_act, dsm, g["gdn_a_log"], g["gdn_dt_bias"] = gdn_scan_bwd(
        f"gdn_scan_bwd_l{l}", s["qkv"], s["proj"], w["gdn_a_log"], w["gdn_dt_bias"], s["gdn_sin"],
        s["gdn_x"], do_scan, dsm)
    dqkv, g["gdn_conv_w"], _ = conv_bwd(
        f"gdn_conv_bwd_l{l}", s["proj"], AL_QKV, GDN_QKV_DIM, w["gdn_conv_w"], w["gdn_conv_b"], dqkv_act, tile=ct)
    dproj = jnp.concatenate([dz, dxbc, dqkv, dgz, dgs, dgg, dsm.astype(BF16),
                             jnp.zeros((t, AL_DIM - AL_SMALL - LANE), BF16)], axis=1)
    dh = matmul(f"in_proj_dx_l{l}", dproj, w["w_in"], "nt")
    g["w_in"] = matmul(f"in_proj_dw_l{l}", s["h"], dproj, "tn", stack=(l, gbuf.get("w_in")))
    dx0, g["norm_mix_w"] = rowwise_bwd(f"norm_mix_bwd_l{l}", f_rmsnorm, t, rt, 1,
                                       [_row(s["x"], D_MODEL), _par(w["norm_mix_w"], D_MODEL)], [True, True],
                                       [_row(dh, D_MODEL)], addends={0: _row(dx1, D_MODEL)})
    return dx0, g


def _align_w_in(w):
    pad = jnp.zeros((w.shape[0], AL_DIM - AL_SMALL - 32), w.dtype)
    return jnp.concatenate([w[:, 0:2560], w[:, 2576:6672], w[:, 6688:8736],
                            w[:, 2560:2576], w[:, 6672:6688], pad], axis=1)


def _pad_lane(v, at=0):
    return jnp.pad(v[None], ((0, 0), (at, LANE - at - v.shape[0])))


def local_step(x, target, full):
    ws = []
    for l in range(DEPTH):
        ws.append({
            "norm_mix_w": full["norm_mix_w"][l][None], "w_in": _align_w_in(full["w_in"][l]),
            "ssm_conv_w": full["ssm_conv_w"][l], "ssm_conv_b": full["ssm_conv_b"][l][None],
            "ssm_dt_bias": _pad_lane(full["ssm_dt_bias"][l]), "ssm_a_log": _pad_lane(full["ssm_a_log"][l]),
            "ssm_d": _pad_lane(full["ssm_d"][l]), "ssm_norm_w": full["ssm_norm_w"][l][None],
            "gdn_conv_w": full["gdn_conv_w"][l], "gdn_conv_b": jnp.zeros((1, GDN_QKV_DIM), F32),
            "gdn_a_log": _pad_lane(full["gdn_a_log"][l], SM_A),
            "gdn_dt_bias": _pad_lane(full["gdn_dt_bias"][l], SM_A),
            "gdn_norm_w": full["gdn_norm_w"][l][None],
            "w_proj_ssm": full["w_proj_ssm"][l], "w_proj_gdn": full["w_proj_gdn"][l], "w_out": full["w_out"][l],
            "norm_ffn_w": full["norm_ffn_w"][l][None], "w_ffn_in": full["w_ffn_in"][l],
            "w_ffn_down": full["w_ffn_down"][l],
        })
    saved = []
    h = x
    for l in range(DEPTH):
        h, s = layer_fwd(l, h, ws[l])
        saved.append(s)
    loss, dx, g_final = final_loss("final_loss", h, target, full["final_norm_w"][None], tile=min(256, x.shape[0]))
    per_layer = [None] * DEPTH
    gbuf = {}
    for l in reversed(range(DEPTH)):
        dx, per_layer[l] = layer_bwd(l, dx, ws[l], saved[l], gbuf)
        gbuf = {n: per_layer[l][n] for n, _ in BIG}
    grads = {"final_norm_w": g_final[0], **gbuf}
    for name in per_layer[0]:
        if name in gbuf:
            continue
        rows = []
        for l in range(DEPTH):
            gl = per_layer[l][name]
            if name in ("ssm_dt_bias", "ssm_a_log", "ssm_d"):
                gl = gl[0, :SSM_HEADS]
            elif name in ("gdn_a_log", "gdn_dt_bias"):
                gl = gl[0, SM_A:SM_A + GDN_HEADS]
            elif name in ("norm_mix_w", "ssm_conv_b", "ssm_norm_w", "gdn_norm_w", "norm_ffn_w"):
                gl = gl[0]
            rows.append(gl)
        grads[name] = jnp.stack(rows)
    return loss, dx, grads


MESH = pl.DeviceIdType.MESH
HBM = pl.BlockSpec(memory_space=pltpu.HBM)
N_DEV = 8


def _pos():
    return lax.axis_index("x"), lax.axis_index("y"), lax.axis_index("c")


def _rcopy(src, dst, send_sem, recv_sem, dev):
    return pltpu.make_async_remote_copy(src_ref=src, dst_ref=dst, send_sem=send_sem, recv_sem=recv_sem,
                                        device_id=dev, device_id_type=MESH)


RELATIONS = (2, 1, 3)


def _related_chip(x, y, mask):
    return (1 - x if mask & 2 else x, 1 - y if mask & 1 else y)


def weights_gather(name, bufs):
    n = len(bufs)

    def body(*refs):
        outs, send_sems, recv_sems = refs[n:2 * n], refs[2 * n], refs[2 * n + 1]
        x, y, c = _pos()
        sib = (x, y, 1 - c)
        sends = []
        for i, a in enumerate(outs):
            for k, m in enumerate(RELATIONS):
                px, py = _related_chip(x, y, m)
                cp = _rcopy(a.at[0, c], a.at[m, c], send_sems.at[6 * i + k], recv_sems.at[6 * i + k], (px, py, c))
                cp.start()
                sends.append(cp)
        for i, a in enumerate(outs):
            for k, m in enumerate(RELATIONS):
                px, py = _related_chip(x, y, m)
                _rcopy(a.at[0, c], a.at[m, c], send_sems.at[6 * i + k], recv_sems.at[6 * i + k],
                       (px, py, c)).wait_recv()
                fw = _rcopy(a.at[m, c], a.at[m, c], send_sems.at[6 * i + 3 + k], recv_sems.at[6 * i + 3 + k], sib)
                fw.start()
                sends.append(fw)
        for i, a in enumerate(outs):
            for k, m in enumerate(RELATIONS):
                _rcopy(a.at[m, 1 - c], a.at[m, 1 - c], send_sems.at[6 * i + 3 + k], recv_sems.at[6 * i + 3 + k],
                       sib).wait_recv()
        for cp in sends:
            cp.wait_send()

    return pl.pallas_call(
        body, name=name, out_shape=[jax.ShapeDtypeStruct(b.shape, b.dtype) for b in bufs],
        in_specs=[HBM] * n, out_specs=[HBM] * n,
        input_output_aliases={i: i for i in range(n)},
        scratch_shapes=[pltpu.SemaphoreType.DMA((6 * n,)), pltpu.SemaphoreType.DMA((6 * n,))],
    )(*bufs)


def pair_swap(name, gs):
    n = len(gs)

    def body(*refs):
        srcs, outs, send_sems, recv_sems = refs[:n], refs[n:2 * n], refs[2 * n], refs[2 * n + 1]
        x, y, c = _pos()
        cps = [_rcopy(s.at[1 - c], o, send_sems.at[i], recv_sems.at[i], (x, y, 1 - c))
               for i, (s, o) in enumerate(zip(srcs, outs))]
        for cp in cps:
            cp.start()
        for cp in cps:
            cp.wait()

    return pl.pallas_call(
        body, name=name, out_shape=[jax.ShapeDtypeStruct(g.shape[1:], g.dtype) for g in gs],
        in_specs=[HBM] * n, out_specs=[HBM] * n,
        scratch_shapes=[pltpu.SemaphoreType.DMA((n,)), pltpu.SemaphoreType.DMA((n,))],
    )(*gs)


def chip_scatter(name, ss):
    n = len(ss)

    def body(*refs):
        srcs, outs, send_sems, recv_sems = refs[:n], refs[n:2 * n], refs[2 * n], refs[2 * n + 1]
        x, y, c = _pos()
        sends = []
        for i, (s, o) in enumerate(zip(srcs, outs)):
            for k, m in enumerate(RELATIONS):
                px, py = _related_chip(x, y, m)
                cp = _rcopy(s.at[2 * px + py], o.at[k], send_sems.at[3 * i + k], recv_sems.at[3 * i + k],
                            (px, py, c))
                cp.start()
                sends.append(cp)
        for cp in sends:
            cp.wait()

    return pl.pallas_call(
        body, name=name, out_shape=[jax.ShapeDtypeStruct((3,) + s.shape[1:], s.dtype) for s in ss],
        in_specs=[HBM] * n, out_specs=[HBM] * n,
        scratch_shapes=[pltpu.SemaphoreType.DMA((3 * n,)), pltpu.SemaphoreType.DMA((3 * n,))],
    )(*ss)


def pair_share(name, bufs):
    n = len(bufs)

    def body(*refs):
        outs, send_sems, recv_sems = refs[n:2 * n], refs[2 * n], refs[2 * n + 1]
        x, y, c = _pos()
        sends = []
        for i, o in enumerate(outs):
            cp = _rcopy(o.at[c], o.at[c], send_sems.at[i], recv_sems.at[i], (x, y, 1 - c))
            cp.start()
            sends.append(cp)
        for i, o in enumerate(outs):
            _rcopy(o.at[1 - c], o.at[1 - c], send_sems.at[i], recv_sems.at[i], (x, y, 1 - c)).wait_recv()
        for cp in sends:
            cp.wait_send()

    return pl.pallas_call(
        body, name=name, out_shape=[jax.ShapeDtypeStruct(b.shape, b.dtype) for b in bufs],
        in_specs=[HBM] * n, out_specs=[HBM] * n,
        input_output_aliases={i: i for i in range(n)},
        scratch_shapes=[pltpu.SemaphoreType.DMA((n,)), pltpu.SemaphoreType.DMA((n,))],
    )(*bufs)


def all_allgather(name, buf):
    r, cd = buf.shape

    def body(src, out, send_sems, recv_sems, lsem):
        x, y, c = _pos()
        me = 4 * x + 2 * y + c
        local = pltpu.make_async_copy(src, out.at[me], lsem)
        local.start()

        def peer(mask):
            px = 1 - x if mask & 4 else x
            py = 1 - y if mask & 2 else y
            pc = 1 - c if mask & 1 else c
            return px, py, pc

        sends = []
        for mask in range(1, N_DEV):
            cp = _rcopy(src, out.at[me], send_sems.at[mask - 1], recv_sems.at[mask - 1], peer(mask))
            cp.start()
            sends.append(cp)
        for mask in range(1, N_DEV):
            px, py, pc = peer(mask)
            _rcopy(src, out.at[4 * px + 2 * py + pc], send_sems.at[mask - 1], recv_sems.at[mask - 1],
                   (px, py, pc)).wait_recv()
        for cp in sends:
            cp.wait_send()
        local.wait()

    return pl.pallas_call(
        body, name=name, out_shape=jax.ShapeDtypeStruct((N_DEV, r, cd), buf.dtype),
        in_specs=[HBM], out_specs=HBM,
        scratch_shapes=[pltpu.SemaphoreType.DMA((N_DEV - 1,)), pltpu.SemaphoreType.DMA((N_DEV - 1,)),
                        pltpu.SemaphoreType.DMA(())],
    )(buf)


ELEMENTWISE_BLOCK_BYTES = 2 << 20


def _row_block(rows, cols):
    for cand in (1024, 512, 256, 128, 64, 32, 16):
        if rows % cand == 0 and cand * cols * 4 <= ELEMENTWISE_BLOCK_BYTES:
            return cand
    return rows


def chip_sum(name, s, r, me, c):
    _, a, b = s.shape
    tr = _row_block(a, b)

    def body(idx_ref, s_ref, r_ref, o_ref):
        del idx_ref
        acc = s_ref[...].astype(F32)
        for k in range(3):
            acc = acc + r_ref[k].astype(F32)
        o_ref[...] = acc

    return pl.pallas_call(
        body, name=name, out_shape=jax.ShapeDtypeStruct((2, a, b), F32),
        grid_spec=pltpu.PrefetchScalarGridSpec(
            num_scalar_prefetch=1, grid=(a // tr,),
            in_specs=[pl.BlockSpec((None, tr, b), lambda i, idx: (idx[0], i, 0)),
                      pl.BlockSpec((3, tr, b), lambda i, idx: (0, i, 0))],
            out_specs=pl.BlockSpec((None, tr, b), lambda i, idx: (idx[1], i, 0))),
        compiler_params=_cparams(("arbitrary",)),
    )(jnp.stack([me, c]).astype(jnp.int32), s, r)


def pair_add(name, p, recv, c):
    _, nj, rh, cd = p.shape
    tr = _row_block(rh, cd)

    def body(c_ref, p_ref, r_ref, o_ref):
        del c_ref
        o_ref[...] = (p_ref[0] + r_ref[...]).astype(o_ref.dtype)

    return pl.pallas_call(
        body, name=name, out_shape=jax.ShapeDtypeStruct((nj, rh, cd), BF16),
        grid_spec=pltpu.PrefetchScalarGridSpec(
            num_scalar_prefetch=1, grid=(nj, rh // tr),
            in_specs=[pl.BlockSpec((1, 1, tr, cd), lambda j, i, c_ref: (c_ref[0], j, i, 0)),
                      pl.BlockSpec((1, tr, cd), lambda j, i, c_ref: (j, i, 0))],
            out_specs=pl.BlockSpec((1, tr, cd), lambda j, i, c_ref: (j, i, 0))),
        compiler_params=_cparams(("arbitrary", "arbitrary")),
    )(jnp.reshape(c, (1,)).astype(jnp.int32), p, recv)


def slab_sum(name, a):
    n, r, cd = a.shape
    tr = _pick(r, (256, 128, 64, 32, 16, 8))

    def body(a_ref, o_ref):
        acc = a_ref[0].astype(F32)
        for j in range(1, n):
            acc = acc + a_ref[j].astype(F32)
        o_ref[...] = acc

    return pl.pallas_call(
        body, name=name, out_shape=jax.ShapeDtypeStruct((r, cd), F32),
        grid=(r // tr,),
        in_specs=[pl.BlockSpec((n, tr, cd), lambda i: (0, i, 0))],
        out_specs=pl.BlockSpec((tr, cd), lambda i: (i, 0)),
        compiler_params=_cparams(("arbitrary",)),
    )(a)


ADAM_C1 = 1.0 - ADAM_B1 ** ADAM_STEP
ADAM_C2 = 1.0 - ADAM_B2 ** ADAM_STEP


def adamw(name, w, g, m, v):
    r, cd = w.shape
    tr = r
    for cand in (512, 256, 128, 64, 32, 16, 8):
        if r % cand == 0 and cand * cd * 4 <= (1 << 20):
            tr = cand
            break

    def body(w_ref, g_ref, m_ref, v_ref, d_ref, nm_ref, nv_ref):
        gv = g_ref[...]
        nm = ADAM_B1 * m_ref[...] + (1.0 - ADAM_B1) * gv
        nv = ADAM_B2 * v_ref[...] + (1.0 - ADAM_B2) * (gv * gv)
        m_hat = nm / ADAM_C1
        v_hat = nv / ADAM_C2
        d_ref[...] = -ADAM_LR * (m_hat / (jnp.sqrt(v_hat) + ADAM_EPS) + ADAM_WD * w_ref[...])
        nm_ref[...] = nm
        nv_ref[...] = nv

    spec = pl.BlockSpec((tr, cd), lambda i: (i, 0))
    sd = jax.ShapeDtypeStruct((r, cd), F32)
    return pl.pallas_call(
        body, name=name, out_shape=[sd, sd, sd], grid=(r // tr,),
        in_specs=[spec] * 4, out_specs=[spec] * 3,
        compiler_params=_cparams(("arbitrary",)),
    )(w, g, m, v)


WEIGHTS = ("norm_mix_w", "w_in", "ssm_conv_w", "ssm_conv_b", "ssm_dt_bias", "ssm_a_log", "ssm_d", "ssm_norm_w",
           "gdn_conv_w", "gdn_a_log", "gdn_dt_bias", "gdn_norm_w", "w_proj_ssm", "w_proj_gdn", "w_out",
           "norm_ffn_w", "w_ffn_in", "w_ffn_down", "final_norm_w")
BIG = (("w_in", 2), ("w_proj_ssm", 1), ("w_proj_gdn", 1), ("w_out", 1), ("w_ffn_in", 2), ("w_ffn_down", 1))
CONVW = (("ssm_conv_w", 2), ("gdn_conv_w", 2))
SHARDED = BIG + CONVW
SMALL = tuple(n for n in WEIGHTS if n not in dict(SHARDED))


def _unpack(buf, shapes, lead=()):
    flat = buf.reshape(lead + (-1,))
    out, o = [], 0
    for shp in shapes:
        n = math.prod(shp)
        out.append(flat[..., o:o + n].reshape(lead + tuple(shp)))
        o += n
    return out


def _pack_rows(arrs, lead=()):
    nl = len(lead)
    flat = jnp.concatenate([a.reshape(lead + (-1,)) for a in arrs], axis=nl)
    n = flat.shape[nl]
    rows = -(-n // (8 * LANE)) * 8
    flat = jnp.pad(flat, [(0, 0)] * nl + [(0, rows * LANE - n)])
    return flat.reshape(lead + (rows, LANE))


def _slot_buffer(shard):
    return jnp.pad(shard[None], [(0, N_CHIPS - 1)] + [(0, 0)] * shard.ndim)


def kernel(x, norm_mix_w, w_in, ssm_conv_w, ssm_conv_b, ssm_dt_bias, ssm_a_log, ssm_d, ssm_norm_w, gdn_conv_w, gdn_a_log, gdn_dt_bias, gdn_norm_w, w_proj_ssm, w_proj_gdn, w_out, norm_ffn_w, w_ffn_in, w_ffn_down, final_norm_w, loss_target, m_norm_mix_w, m_w_in, m_ssm_conv_w, m_ssm_conv_b, m_ssm_dt_bias, m_ssm_a_log, m_ssm_d, m_ssm_norm_w, m_gdn_conv_w, m_gdn_a_log, m_gdn_dt_bias, m_gdn_norm_w, m_w_proj_ssm, m_w_proj_gdn, m_w_out, m_norm_ffn_w, m_w_ffn_in, m_w_ffn_down, m_final_norm_w, v_norm_mix_w, v_w_in, v_ssm_conv_w, v_ssm_conv_b, v_ssm_dt_bias, v_ssm_a_log, v_ssm_d, v_ssm_norm_w, v_gdn_conv_w, v_gdn_a_log, v_gdn_dt_bias, v_gdn_norm_w, v_w_proj_ssm, v_w_proj_gdn, v_w_out, v_norm_ffn_w, v_w_ffn_in, v_w_ffn_down, v_final_norm_w):
    wl = (norm_mix_w, w_in, ssm_conv_w, ssm_conv_b, ssm_dt_bias, ssm_a_log, ssm_d, ssm_norm_w, gdn_conv_w,
          gdn_a_log, gdn_dt_bias, gdn_norm_w, w_proj_ssm, w_proj_gdn, w_out, norm_ffn_w, w_ffn_in, w_ffn_down,
          final_norm_w)
    ml = (m_norm_mix_w, m_w_in, m_ssm_conv_w, m_ssm_conv_b, m_ssm_dt_bias, m_ssm_a_log, m_ssm_d, m_ssm_norm_w,
          m_gdn_conv_w, m_gdn_a_log, m_gdn_dt_bias, m_gdn_norm_w, m_w_proj_ssm, m_w_proj_gdn, m_w_out,
          m_norm_ffn_w, m_w_ffn_in, m_w_ffn_down, m_final_norm_w)
    vl = (v_norm_mix_w, v_w_in, v_ssm_conv_w, v_ssm_conv_b, v_ssm_dt_bias, v_ssm_a_log, v_ssm_d, v_ssm_norm_w,
          v_gdn_conv_w, v_gdn_a_log, v_gdn_dt_bias, v_gdn_norm_w, v_w_proj_ssm, v_w_proj_gdn, v_w_out,
          v_norm_ffn_w, v_w_ffn_in, v_w_ffn_down, v_final_norm_w)
    w = dict(zip(WEIGHTS, wl))
    m = dict(zip(WEIGHTS, ml))
    v = dict(zip(WEIGHTS, vl))
    x_pos, y_pos, c = _pos()
    me = 2 * x_pos + y_pos
    big = [n for n, _ in BIG]

    shards = [w[n].astype(BF16) for n in big]
    shards[0] = jnp.pad(shards[0], ((0, 0), (0, 0), (0, IN_SHARD_PAD - IN_SHARD)))
    conv_shapes = [w[n].shape[1:] for n, _ in CONVW]
    conv_pack = _pack_rows([w[n] for n, _ in CONVW], lead=(DEPTH,))
    gathered = weights_gather("gather_w", [_slot_buffer(s) for s in shards + [conv_pack]])
    by_chip = [[lax.dynamic_index_in_dim(g_, jnp.bitwise_xor(me, j), 0, keepdims=False) for j in range(N_CHIPS)]
               for g_ in gathered]
    full = {n: w[n] for n in SMALL}
    for i, (n, axis) in enumerate(BIG):
        cols = IN_SHARD if n == "w_in" else by_chip[i][0].shape[-1]
        full[n] = [jnp.concatenate([p[l][:, :cols] for p in by_chip[i]], axis=axis - 1) for l in range(DEPTH)]
    conv_parts = [_unpack(by_chip[-1][j], conv_shapes, lead=(DEPTH,)) for j in range(N_CHIPS)]
    for i, (n, axis) in enumerate(CONVW):
        full[n] = jnp.concatenate([conv_parts[j][i] for j in range(N_CHIPS)], axis=axis)

    loss_part, grad_x, grads = local_step(x[0], loss_target[0], full)

    by4 = [grads[n][:, None] if n == "w_in" else grads[n].reshape((DEPTH, N_CHIPS, -1) + grads[n].shape[-1:])
           for n in big]
    from_pair = pair_swap("grad_pair_swap", by4)
    chip_part = [pair_add(f"grad_pair_add_{n}", g_, r_, c) for n, g_, r_ in zip(big, by4, from_pair)]
    chip_part[0] = _aligned_to_shards(chip_part[0][0])
    from_chips = chip_scatter("grad_chip_scatter", chip_part)
    halves = [chip_sum(f"grad_chip_sum_{n}", s_, r_, me, c) for n, s_, r_ in zip(big, chip_part, from_chips)]
    reduced = pair_share("grad_pair_share", halves)
    g_sharded = dict(zip(big, reduced))
    g_sharded["w_in"] = g_sharded["w_in"][:, :, :IN_SHARD]

    small_names = list(SMALL) + [n for n, _ in CONVW]
    small_all = all_allgather("gather_small", _pack_rows([grads[n] for n in small_names] + [loss_part[0, :1]]))
    small_sum = slab_sum("small_sum", small_all)
    small_vals = _unpack(small_sum, [grads[n].shape for n in small_names] + [(1,)])
    g_small = dict(zip(small_names, small_vals[:-1]))
    loss = small_vals[-1].reshape(())
    for n, axis in CONVW:
        size = w[n].shape[axis]
        g_sharded[n] = lax.dynamic_slice_in_dim(g_small.pop(n), me * size, size, axis=axis)

    out_g, out_d, out_m, out_v = {}, {}, {}, {}
    for n, _ in SHARDED:
        shp = w[n].shape
        two = lambda a: a.reshape(-1, shp[-1])
        d_, m_, v_ = adamw(f"adamw_{n}", two(w[n]), two(g_sharded[n]), two(m[n]), two(v[n]))
        out_g[n], out_d[n], out_m[n], out_v[n] = g_sharded[n], d_.reshape(shp), m_.reshape(shp), v_.reshape(shp)
    d_, m_, v_ = adamw("adamw_small", *[_pack_rows([d[n] for n in SMALL]) for d in (w, g_small, m, v)])
    small_shapes = [w[n].shape for n in SMALL]
    for n, dd, mm, vv in zip(SMALL, _unpack(d_, small_shapes), _unpack(m_, small_shapes), _unpack(v_, small_shapes)):
        out_g[n], out_d[n], out_m[n], out_v[n] = g_small[n], dd, mm, vv

    return (loss, grad_x[None], *[out_g[n] for n in WEIGHTS], *[out_d[n] for n in WEIGHTS],
            *[out_m[n] for n in WEIGHTS], *[out_v[n] for n in WEIGHTS])
```

```python
import math

import jax
import jax.numpy as jnp
from jax import lax
from jax.experimental import pallas as pl
from jax.experimental.pallas import tpu as pltpu

F32 = jnp.float32
BF16 = jnp.bfloat16

D_MODEL = 1024
DEPTH = 2
SSM_HEADS = 16
SSM_HEAD_DIM = 64
SSM_D_INNER = 1024
SSM_STATE = 128
SSM_CONV_DIM = 1536
GDN_HEADS = 8
GDN_HEAD = 128
GDN_QKV_DIM = 3072
CONV_K = 4
CHUNK = 128
SSD_CHUNK = 256
FFN_HIDDEN = 2816
EPS = 1e-6
IN_DIM = 8736

ADAM_LR = 0.001
ADAM_B1 = 0.9
ADAM_B2 = 0.999
ADAM_EPS = 1e-08
ADAM_WD = 0.01
ADAM_STEP = 10

LANE = 128
NEG_BIG = -1e30
VMEM_LIMIT = 56 * 1024 * 1024

AL_Z, AL_XBC, AL_QKV, AL_GZ, AL_GS, AL_GG, AL_SMALL, AL_DIM = 0, 1024, 2560, 5632, 6656, 7680, 8704, 9216
SM_DT, SM_A, SM_B = 0, 16, 24

HI = lax.Precision.HIGHEST
NN = (((1,), (0,)), ((), ()))
NT = (((1,), (1,)), ((), ()))
TN = (((0,), (0,)), ((), ()))


def _cparams(sem):
    return pltpu.CompilerParams(dimension_semantics=sem, vmem_limit_bytes=VMEM_LIMIT)


def _pick(n, prefs):
    for p in prefs:
        if n % p == 0:
            return p
    return n


MATMUL_VMEM_BUDGET = 40 << 20
MXU_WIDTH = 256
HBM_BYTES_PER_S = 3.3e12
MXU_FLOPS_PER_S = 9.0e14
GRID_STEP_S = 0.35e-6
N_CHIPS = 4


def _matmul_tiles(m, n, n_dom, k, a_item, b_item, o_item):
    best = None
    def cands(dim, cap):
        return [c for c in range(LANE, min(dim, cap) + 1, LANE) if dim % c == 0] or [dim]

    tms, tns, tks = cands(m, 2048), cands(n_dom, 2304), cands(k, 1 << 30)
    for tm in tms:
        for tn in tns:
            for tk in tks:
                nk = k // tk
                vmem = (2 * (tm * tk * a_item + tk * tn * b_item + tm * tn * o_item) + tm * tn * 4 * (2 if nk > 1 else 1)
                        + (tm * tk * 2 if a_item > 2 else 0) + (tk * tn * 2 if b_item > 2 else 0))
                if vmem > MATMUL_VMEM_BUDGET:
                    continue
                traffic = m * k * a_item * (1 if nk == 1 else n // tn) + k * n * b_item * (m // tm)
                mxu_fill = tn / (-(-tn // MXU_WIDTH) * MXU_WIDTH)
                cost = (max(traffic / HBM_BYTES_PER_S, 2.0 * m * n * k / (MXU_FLOPS_PER_S * mxu_fill))
                        + (m // tm) * (n // tn) * nk * GRID_STEP_S)
                if best is None or cost < best[0]:
                    best = (cost, (tm, tn, tk))
    return best[1]


def _dot(a, b, dims=NN):
    return lax.dot_general(a.astype(BF16), b.astype(BF16), dims, preferred_element_type=F32)


def _dot3(a, b, dims=NN):
    a_hi, b_hi = a.astype(BF16), b.astype(BF16)
    a_lo = (a - a_hi.astype(F32)).astype(BF16)
    b_lo = (b - b_hi.astype(F32)).astype(BF16)

    def dg(u, v):
        return lax.dot_general(u, v, dims, preferred_element_type=F32)

    return dg(a_hi, b_hi) + (dg(a_hi, b_lo) + dg(a_lo, b_hi))


def _dot_hi(a, b, dims=NN):
    return lax.dot_general(a, b, dims, precision=HI, preferred_element_type=F32)


def _sigmoid(x):
    return jax.nn.sigmoid(x)


def _silu(x):
    return x * _sigmoid(x)


def _softplus(x):
    return jnp.maximum(x, 0.0) + jnp.log1p(jnp.exp(-jnp.abs(x)))


def matmul(name, a, b, mode, out_dtype=F32, chip_major=False, stack=None):
    if mode == "nn":
        (m, k), (k2, n) = a.shape, b.shape
    elif mode == "nt":
        (m, k), (n, k2) = a.shape, b.shape
    else:
        (k, m), (k2, n) = a.shape, b.shape
    assert k == k2, (a.shape, b.shape, mode)
    shard = n // N_CHIPS if chip_major else n
    tm, tn, tk = _matmul_tiles(m, n, shard, k, a.dtype.itemsize, b.dtype.itemsize, jnp.dtype(out_dtype).itemsize)
    if chip_major:
        per = shard // tn
        base_shape, base_blk = (N_CHIPS, m, shard), (None, tm, tn)
        base_idx = lambda i, j: (j // per, i, j % per)
    else:
        base_shape, base_blk = (m, n), (tm, tn)
        base_idx = lambda i, j: (i, j)
    nk = k // tk
    dims = {"nn": NN, "nt": NT, "tn": TN}[mode]

    def body_acc(a_ref, b_ref, o_ref, acc_ref):
        kk = pl.program_id(2)

        @pl.when(kk == 0)
        def _():
            acc_ref[...] = jnp.zeros_like(acc_ref)

        acc_ref[...] += _dot(a_ref[...], b_ref[...], dims)

        @pl.when(kk == nk - 1)
        def _():
            o_ref[...] = acc_ref[...].astype(o_ref.dtype)

    def body_one(a_ref, b_ref, o_ref):
        o_ref[...] = _dot(a_ref[...], b_ref[...], dims).astype(o_ref.dtype)

    compute = body_one if nk == 1 else body_acc
    if mode == "tn":
        a_spec = pl.BlockSpec((tk, tm), lambda i, j, kk: (kk, i))
    else:
        a_spec = pl.BlockSpec((tm, tk), lambda i, j, kk: (i, kk))
    if mode == "nt":
        b_spec = pl.BlockSpec((tn, tk), lambda i, j, kk: (j, kk))
    else:
        b_spec = pl.BlockSpec((tk, tn), lambda i, j, kk: (kk, j))
    in_specs, operands, aliases, body = [a_spec, b_spec], [a, b], {}, compute
    if stack is None:
        out_shape, out_blk, out_idx = base_shape, base_blk, (lambda i, j, kk: base_idx(i, j))
    else:
        layer, buf = stack
        out_shape, out_blk = (DEPTH,) + base_shape, (None,) + base_blk
        out_idx = lambda i, j, kk: (layer,) + base_idx(i, j)
        if buf is not None:
            assert buf.shape == out_shape and buf.dtype == out_dtype
            in_specs.append(pl.BlockSpec(memory_space=pl.ANY))
            operands.append(buf)
            aliases = {2: 0}

            def body(a_ref, b_ref, buf_ref, *rest):
                del buf_ref
                compute(a_ref, b_ref, *rest)

    return pl.pallas_call(
        body, name=name,
        out_shape=jax.ShapeDtypeStruct(out_shape, out_dtype),
        grid=(m // tm, n // tn, nk),
        in_specs=in_specs,
        out_specs=pl.BlockSpec(out_blk, out_idx),
        scratch_shapes=[] if nk == 1 else [pltpu.VMEM((tm, tn), F32)],
        input_output_aliases=aliases,
        compiler_params=_cparams(("parallel", "parallel", "arbitrary")),
    )(*operands)


def _row_map(c0, moves):
    return (lambda j, i: (i, c0 + j)) if moves else (lambda j, i: (i, c0))


def _par_map(c0, moves):
    return (lambda j, i: (0, c0 + j)) if moves else (lambda j, i: (0, c0))


def _in_spec(op, tile):
    _, kind, w, c0, moves = op
    if kind == "row":
        return pl.BlockSpec((tile, w), _row_map(c0, moves))
    return pl.BlockSpec((1, w), _par_map(c0, moves))


ROW_BLOCK_ELEMS = 1 << 18


def _row_tile(t, tile, ops):
    width = max(op[2] for op in ops if op[1] == "row")
    return min(t, max(tile, ROW_BLOCK_ELEMS // width))


def rowwise_fwd(name, fn, t, tile, ncol, ins, outs):
    n_in = len(ins)
    tile = _row_tile(t, tile, ins)

    def body(*refs):
        vals = [r[...].astype(F32) for r in refs[:n_in]]
        res = fn(*vals)
        if not isinstance(res, (tuple, list)):
            res = (res,)
        for r, v in zip(refs[n_in:], res):
            r[...] = v.astype(r.dtype)

    res = pl.pallas_call(
        body, name=name,
        out_shape=[jax.ShapeDtypeStruct((t, w * ncol), dt) for w, dt in outs],
        grid=(ncol, t // tile),
        in_specs=[_in_spec(op, tile) for op in ins],
        out_specs=[pl.BlockSpec((tile, w), _row_map(0, True)) for w, _ in outs],
        compiler_params=_cparams(("arbitrary", "arbitrary")),
    )(*[op[0] for op in ins])
    return res


def rowwise_bwd(name, fn, t, tile, ncol, ins, need, cts, addends=None, row_dtypes=None):
    n_in, n_ct = len(ins), len(cts)
    tile = _row_tile(t, tile, ins)
    addends = addends or {}
    row_dtypes = row_dtypes or {}
    didx = [i for i in range(n_in) if need[i]]
    add_ops = [addends[i] for i in didx if i in addends]
    n_add = len(add_ops)

    def body(*refs):
        in_refs = refs[:n_in]
        ct_refs = refs[n_in:n_in + n_ct]
        add_refs = refs[n_in + n_ct:n_in + n_ct + n_add]
        out_refs = refs[n_in + n_ct + n_add:]
        vals = [r[...].astype(F32) for r in in_refs]

        def g(*dv):
            full = list(vals)
            for i, v in zip(didx, dv):
                full[i] = v
            res = fn(*full)
            return tuple(res) if isinstance(res, (tuple, list)) else (res,)

        _, vjp = jax.vjp(g, *[vals[i] for i in didx])
        grads = vjp(tuple(c[...].astype(F32) for c in ct_refs))
        j, i = pl.program_id(0), pl.program_id(1)
        a = 0
        for o_ref, gv, idx in zip(out_refs, grads, didx):
            _, kind, _, _, moves = ins[idx]
            if kind == "row":
                if idx in addends:
                    gv = gv + add_refs[a][...].astype(F32)
                    a += 1
                o_ref[...] = gv.astype(o_ref.dtype)
            else:
                first = (i == 0) if moves else jnp.logical_and(i == 0, j == 0)

                @pl.when(first)
                def _(o_ref=o_ref, gv=gv):
                    o_ref[...] = gv

                @pl.when(jnp.logical_not(first))
                def _(o_ref=o_ref, gv=gv):
                    o_ref[...] += gv

    out_shape, out_specs = [], []
    for idx in didx:
        _, kind, w, _, moves = ins[idx]
        cols = w * (ncol if moves else 1)
        if kind == "row":
            out_shape.append(jax.ShapeDtypeStruct((t, cols), row_dtypes.get(idx, F32)))
            out_specs.append(pl.BlockSpec((tile, w), _row_map(0, moves)))
        else:
            out_shape.append(jax.ShapeDtypeStruct((1, cols), F32))
            out_specs.append(pl.BlockSpec((1, w), _par_map(0, moves)))
    ops = list(ins) + list(cts) + add_ops
    res = pl.pallas_call(
        body, name=name,
        out_shape=out_shape,
        grid=(ncol, t // tile),
        in_specs=[_in_spec(op, tile) for op in ops],
        out_specs=out_specs,
        compiler_params=_cparams(("arbitrary", "arbitrary")),
    )(*[op[0] for op in ops])
    return res


def f_rmsnorm(x, w):
    return x * lax.rsqrt(jnp.mean(x * x, axis=-1, keepdims=True) + EPS) * w


def f_ssd_post(y, z, w):
    y = y * _silu(z)
    return y * lax.rsqrt(jnp.mean(y * y, axis=-1, keepdims=True) + EPS) * w


def f_gdn_post(o, z, w):
    o = o * lax.rsqrt(jnp.mean(o * o, axis=-1, keepdims=True) + EPS) * w
    return o * _silu(z)


def f_merge(gs, p1, gg, p2):
    return _sigmoid(gs) * p1 + _sigmoid(gg) * p2


def f_swiglu(g, u):
    return _silu(g) * u


def final_loss(name, x, tgt, w, tile=256):
    t, d = x.shape

    def body(x_ref, t_ref, w_ref, loss_ref, dx_ref, dw_ref):
        i = pl.program_id(0)
        xv, tv, wv = x_ref[...], t_ref[...], w_ref[...]

        def g(xx, ww):
            err = f_rmsnorm(xx, ww) - tv
            return 0.5 * jnp.sum(jnp.mean(err * err, axis=-1, keepdims=True), axis=0, keepdims=True)

        val, vjp = jax.vjp(g, xv, wv)
        dx, dw = vjp(jnp.ones((1, 1), F32))
        dx_ref[...] = dx
        lv = jnp.broadcast_to(val, (1, LANE))

        @pl.when(i == 0)
        def _():
            loss_ref[...] = lv
            dw_ref[...] = dw

        @pl.when(i != 0)
        def _():
            loss_ref[...] += lv
            dw_ref[...] += dw

    return pl.pallas_call(
        body, name=name,
        out_shape=[jax.ShapeDtypeStruct((1, LANE), F32), jax.ShapeDtypeStruct((t, d), F32),
                   jax.ShapeDtypeStruct((1, d), F32)],
        grid=(t // tile,),
        in_specs=[pl.BlockSpec((tile, d), lambda i: (i, 0)), pl.BlockSpec((tile, d), lambda i: (i, 0)),
                  pl.BlockSpec((1, d), lambda i: (0, 0))],
        out_specs=[pl.BlockSpec((1, LANE), lambda i: (0, 0)), pl.BlockSpec((tile, d), lambda i: (i, 0)),
                   pl.BlockSpec((1, d), lambda i: (0, 0))],
        compiler_params=_cparams(("arbitrary",)),
    )(x, tgt, w)


CONV_W = 512
HALO = 8
STRIPS = 4


def _rows_back(before, cur, d):
    rows = lax.broadcasted_iota(jnp.int32, cur.shape, 0)
    return jnp.where(rows < d, pltpu.roll(before, d, 0), pltpu.roll(cur, d, 0))


def _rows_ahead(cur, after, d):
    rows = lax.broadcasted_iota(jnp.int32, cur.shape, 0)
    return jnp.where(rows < HALO - d, pltpu.roll(cur, HALO - d, 0), pltpu.roll(after, HALO - d, 0))


def _conv_taps(taps, bias, before, cur):
    shifted = [_rows_back(before, cur, CONV_K - 1 - k) for k in range(CONV_K - 1)] + [cur]
    pre = bias + taps[CONV_K - 1] * cur
    for k in range(CONV_K - 1):
        pre = pre + taps[k] * shifted[k]
    return pre, shifted


def conv_fwd(name, src, c0, width, w, b, tile=512):
    t = src.shape[0]
    ncol, nrow = width // CONV_W, t // tile
    cb0 = c0 // CONV_W
    hb = tile // HALO

    def body(prev_ref, cur_ref, w_ref, b_ref, o_ref):
        i = pl.program_id(1)
        taps = [w_ref[k:k + 1, :] for k in range(CONV_K)]
        bias = b_ref[...]

        def strips(g, before):
            for u in range(STRIPS):
                r0 = pl.multiple_of((g * STRIPS + u) * HALO, HALO)
                cur = cur_ref[pl.ds(r0, HALO), :]
                pre, _ = _conv_taps(taps, bias, before, cur)
                o_ref[pl.ds(r0, HALO), :] = _silu(pre)
                before = cur
            return before

        lax.fori_loop(0, tile // (HALO * STRIPS), strips, jnp.where(i == 0, 0.0, prev_ref[...]))

    return pl.pallas_call(
        body, name=name,
        out_shape=jax.ShapeDtypeStruct((t, width), F32),
        grid=(ncol, nrow),
        in_specs=[pl.BlockSpec((HALO, CONV_W), lambda j, i: (jnp.maximum(i * hb - 1, 0), cb0 + j)),
                  pl.BlockSpec((tile, CONV_W), lambda j, i: (i, cb0 + j)),
                  pl.BlockSpec((CONV_K, CONV_W), lambda j, i: (0, j)),
                  pl.BlockSpec((1, CONV_W), lambda j, i: (0, j))],
        out_specs=pl.BlockSpec((tile, CONV_W), lambda j, i: (i, j)),
        compiler_params=_cparams(("arbitrary", "arbitrary")),
    )(src, src, w, b)


def conv_bwd(name, src, c0, width, w, b, dy, tile=512):
    t = src.shape[0]
    ncol, nrow = width // CONV_W, t // tile
    cb0 = c0 // CONV_W
    hb = tile // HALO
    last_hb = t // HALO - 1
    nstrip = tile // HALO

    def body(sprev_ref, scur_ref, snext_ref, w_ref, b_ref, dycur_ref, dynext_ref,
             du_ref, dw_ref, db_ref, dpre_ref):
        i = pl.program_id(1)
        taps = [w_ref[k:k + 1, :] for k in range(CONV_K)]
        bias = b_ref[...]

        def dpre_of(before, cur, dy_strip):
            pre, shifted = _conv_taps(taps, bias, before, cur)
            s = _sigmoid(pre)
            return dy_strip * (s * (1.0 + pre * (1.0 - s))), shifted

        def strips1(g, carry):
            before, dws, dbs = carry
            for u in range(STRIPS):
                r0 = pl.multiple_of((g * STRIPS + u) * HALO, HALO)
                cur = scur_ref[pl.ds(r0, HALO), :]
                dpre, shifted = dpre_of(before, cur, dycur_ref[pl.ds(r0, HALO), :])
                dpre_ref[pl.ds(r0, HALO), :] = dpre
                before, dws, dbs = cur, tuple(a + dpre * v for a, v in zip(dws, shifted)), dbs + dpre
            return before, dws, dbs

        zero = jnp.zeros((HALO, CONV_W), F32)
        before, dws, dbs = lax.fori_loop(0, nstrip // STRIPS, strips1,
                                         (jnp.where(i == 0, 0.0, sprev_ref[...]), (zero,) * CONV_K, zero))
        dpre_next, _ = dpre_of(before, snext_ref[...], dynext_ref[...])
        dpre_ref[pl.ds(tile, HALO), :] = jnp.where(i == nrow - 1, 0.0, dpre_next)

        def strips2(g, _):
            parts = []
            for u in range(STRIPS):
                r0 = pl.multiple_of((g * STRIPS + u) * HALO, HALO)
                cur = dpre_ref[pl.ds(r0, HALO), :]
                after = dpre_ref[pl.ds(r0 + HALO, HALO), :]
                acc = taps[CONV_K - 1] * cur
                for d in range(1, CONV_K):
                    acc = acc + taps[CONV_K - 1 - d] * _rows_ahead(cur, after, d)
                parts.append(acc)
            r0 = pl.multiple_of(g * STRIPS * HALO, STRIPS * HALO)
            du_ref[pl.ds(r0, STRIPS * HALO), :] = jnp.concatenate(parts, axis=0).astype(du_ref.dtype)
            return 0

        lax.fori_loop(0, nstrip // STRIPS, strips2, 0)
        dw_tile = jnp.concatenate([jnp.sum(a, axis=0, keepdims=True) for a in dws], axis=0)
        db_tile = jnp.sum(dbs, axis=0, keepdims=True)
        _acc(dw_ref, dw_tile, i == 0)
        _acc(db_ref, db_tile, i == 0)

    return pl.pallas_call(
        body, name=name,
        out_shape=[jax.ShapeDtypeStruct((t, width), BF16), jax.ShapeDtypeStruct((CONV_K, width), F32),
                   jax.ShapeDtypeStruct((1, width), F32)],
        grid=(ncol, nrow),
        in_specs=[pl.BlockSpec((HALO, CONV_W), lambda j, i: (jnp.maximum(i * hb - 1, 0), cb0 + j)),
                  pl.BlockSpec((tile, CONV_W), lambda j, i: (i, cb0 + j)),
                  pl.BlockSpec((HALO, CONV_W), lambda j, i: (jnp.minimum((i + 1) * hb, last_hb), cb0 + j)),
                  pl.BlockSpec((CONV_K, CONV_W), lambda j, i: (0, j)),
                  pl.BlockSpec((1, CONV_W), lambda j, i: (0, j)),
                  pl.BlockSpec((tile, CONV_W), lambda j, i: (i, j)),
                  pl.BlockSpec((HALO, CONV_W), lambda j, i: (jnp.minimum((i + 1) * hb, last_hb), j))],
        out_specs=[pl.BlockSpec((tile, CONV_W), lambda j, i: (i, j)),
                   pl.BlockSpec((CONV_K, CONV_W), lambda j, i: (0, j)),
                   pl.BlockSpec((1, CONV_W), lambda j, i: (0, j))],
        scratch_shapes=[pltpu.VMEM((tile + HALO, CONV_W), F32)],
        compiler_params=_cparams(("arbitrary", "arbitrary")),
    )(src, src, src, w, b, dy, dy)


def _iota2(q):
    return (lax.broadcasted_iota(jnp.int32, (q, q), 0), lax.broadcasted_iota(jnp.int32, (q, q), 1))


def _lane_pick(blk, idx):
    lane = lax.broadcasted_iota(jnp.int32, (1, LANE), 1)
    return jnp.sum(jnp.where(lane == idx, blk, 0.0), axis=1, keepdims=True)


class _Decay:
    def __init__(self, a):
        q = a.shape[0]
        r, c = _iota2(q)
        self.r, self.c = r, c
        self.cum = _dot_hi((c <= r).astype(F32), a)
        self.cum_t = _dot_hi(a, (r <= c).astype(F32), TN)
        self.tot = self.cum[q - 1:q, :]
        self.e_cum = jnp.exp(self.cum)
        self.e_rest = jnp.exp(self.tot - self.cum)
        self.e_tot = jnp.exp(self.tot)

    def mask(self, lane):
        rows = lax.broadcasted_iota(jnp.int32, (LANE, 1), 0)
        cum_row = jnp.sum(jnp.where(rows == lane, self.cum_t, 0.0), axis=0, keepdims=True)
        return jnp.exp(jnp.where(self.r >= self.c, _lane_pick(self.cum, lane) - cum_row, NEG_BIG))


_SSD_B = SSM_D_INNER
_SSD_C = SSM_D_INNER + 2 * SSM_STATE


def _interleave(gens):
    results = [None] * len(gens)
    live = list(range(len(gens)))
    while live:
        for i in list(live):
            try:
                next(gens[i])
            except StopIteration as stop:
                results[i] = stop.value
                live.remove(i)
    return results


def ssd_chunk(xs, bm, cm, dt_all, dsk, dec, state, p, cb):
    lane = lax.broadcasted_iota(jnp.int32, (1, LANE), 1)
    m0 = lane < SSM_HEAD_DIM
    h0, h1 = 2 * p, 2 * p + 1

    def both(blk):
        return jnp.where(m0, _lane_pick(blk, h0), _lane_pick(blk, h1))

    xdt = xs * both(dt_all)
    l0, l1 = dec.mask(h0), dec.mask(h1)
    yield
    y_diag = _dot(cb * l0, jnp.where(m0, xdt, 0.0)) + _dot(cb * l1, jnp.where(m0, 0.0, xdt))
    y_off = _dot(cm, state, NT) * both(dec.e_cum)
    yield
    rowm = lax.broadcasted_iota(jnp.int32, (LANE, 1), 0) < SSM_HEAD_DIM
    new_state = (state * jnp.where(rowm, _lane_pick(dec.e_tot, h0), _lane_pick(dec.e_tot, h1))
                 + _dot(xdt * both(dec.e_rest), bm, TN))
    y = y_diag + y_off + both(dsk) * xs
    return y, new_state


def ssd_pairs(xs, bms, cms, small, dtb, alog, dsk, states):
    dt_all = _softplus(small + dtb)
    dec = _Decay(dt_all * (-jnp.exp(alog)))
    cbs = [_dot(cm, bm, NT) for cm, bm in zip(cms, bms)]
    res = _interleave([ssd_chunk(x, bms[p // 4], cms[p // 4], dt_all, dsk, dec, st, p, cbs[p // 4])
                       for p, (x, st) in enumerate(zip(xs, states))])
    return tuple(y for y, _ in res), tuple(s for _, s in res)


def tri_inverse(a):
    q = a.shape[0]
    r, c = _iota2(q)
    eye = (r == c).astype(F32)
    diag = (r // 16) == (c // 16)
    bd = jnp.where(diag, a, 0.0)
    off = jnp.where(diag, 0.0, a)
    b2 = _dot3(bd, bd)
    d1 = _dot3(eye - bd, eye + b2)
    yield
    b4 = _dot3(b2, b2)
    yield
    b8 = _dot3(b4, b4)
    d2 = _dot3(d1, eye + b4)
    yield
    dinv = _dot3(d2, eye + b8)
    yield
    n = _dot3(dinv, off)
    yield
    powers = [n]
    while 16 * 2 ** len(powers) < q:
        powers.append(_dot3(powers[-1], powers[-1]))
        yield
    m = dinv
    for pw in reversed(powers[1:]):
        m = _dot3(eye + pw, m)
        yield
    return _dot3(eye - n, m)


@jax.custom_vjp
def _solve_with(xinv, a, rhs):
    del a
    return _dot3(xinv, rhs)


def _solve_with_fwd(xinv, a, rhs):
    t = _dot3(xinv, rhs)
    return t, (xinv, t)


def _solve_with_bwd(res, dt):
    xinv, t = res
    d_rhs = _dot3(xinv, dt, TN)
    d_a = -_dot(d_rhs, t, NT)
    return jnp.zeros_like(xinv), d_a, d_rhs


_solve_with.defvjp(_solve_with_fwd, _solve_with_bwd)


_GDN_K = GDN_HEADS * GDN_HEAD
_GDN_V = 2 * GDN_HEADS * GDN_HEAD


def gdn_chunk(qh, kh, vh, beta_all, dec, state, h, xinv=None):
    r, c = dec.r, dec.c
    qn = qh * lax.rsqrt(jnp.sum(qh * qh, axis=-1, keepdims=True) + EPS) * (GDN_HEAD ** -0.5)
    kn = kh * lax.rsqrt(jnp.sum(kh * kh, axis=-1, keepdims=True) + EPS)
    beta = _lane_pick(beta_all, SM_B + h)
    decay = dec.mask(SM_A + h)
    yield
    kk = _dot(kn, kn, NT)
    qk = _dot(qn, kn, NT) * decay
    amat = jnp.where(r > c, kk * decay * beta, 0.0)
    eg = _lane_pick(dec.e_cum, SM_A + h)
    rhs = jnp.concatenate([vh * beta, kn * (beta * eg)], axis=1)
    qs = _dot(qn * eg, state)
    yield
    if xinv is None:
        xinv = yield from tri_inverse(amat)
        t = _dot3(xinv, rhs)
    else:
        t = _solve_with(xinv, amat, rhs)
    yield
    u, w = t[:, :GDN_HEAD], t[:, GDN_HEAD:]
    v_new = u - _dot(w, state)
    yield
    o = qs + _dot(qk, v_new)
    new_state = (state * _lane_pick(dec.e_tot, SM_A + h)
                 + _dot(kn * _lane_pick(dec.e_rest, SM_A + h), v_new, TN))
    return o, new_state, xinv


def gdn_heads(qs, ks, vs, small, alog, dtb, states, xinvs=None):
    nh = len(qs)
    beta_all = _sigmoid(small)
    dec = _Decay(-jnp.exp(alog) * _softplus(small + dtb))
    res = _interleave([gdn_chunk(qs[h], ks[h], vs[h], beta_all, dec, states[h], h,
                                 None if xinvs is None else xinvs[h]) for h in range(nh)])
    return tuple(o for o, _, _ in res), tuple(s for _, s, _ in res), tuple(x for _, _, x in res)


def _acc(ref, val, first):
    @pl.when(first)
    def _():
        ref[...] = val

    @pl.when(jnp.logical_not(first))
    def _():
        ref[...] += val


def ssd_scan_fwd(name, xbc, proj, dtb, alog, dsk):
    t = xbc.shape[0]
    nc, npair = t // SSD_CHUNK, SSM_HEADS // 2
    small_blk = AL_SMALL // LANE

    def body(xbc_ref, sm_ref, dtb_ref, alog_ref, dsk_ref, y_ref, sin_ref, st_ref):
        ci = pl.program_id(0)

        @pl.when(ci == 0)
        def _():
            st_ref[...] = jnp.zeros_like(st_ref)

        s_in = tuple(st_ref[p] for p in range(npair))
        ys, s_new = ssd_pairs(tuple(xbc_ref[:, p * LANE:(p + 1) * LANE] for p in range(npair)),
                              tuple(xbc_ref[:, _SSD_B + g * LANE:_SSD_B + (g + 1) * LANE] for g in range(2)),
                              tuple(xbc_ref[:, _SSD_C + g * LANE:_SSD_C + (g + 1) * LANE] for g in range(2)),
                              sm_ref[...], dtb_ref[...], alog_ref[...], dsk_ref[...], s_in)
        for p in range(npair):
            sin_ref[0, p] = s_in[p]
            y_ref[:, p * LANE:(p + 1) * LANE] = ys[p]
            st_ref[p] = s_new[p]

    par = pl.BlockSpec((1, LANE), lambda ci: (0, 0))
    return pl.pallas_call(
        body, name=name,
        out_shape=[jax.ShapeDtypeStruct((t, SSM_D_INNER), F32),
                   jax.ShapeDtypeStruct((nc, npair, LANE, LANE), F32)],
        grid=(nc,),
        in_specs=[pl.BlockSpec((SSD_CHUNK, SSM_CONV_DIM), lambda ci: (ci, 0)),
                  pl.BlockSpec((SSD_CHUNK, LANE), lambda ci: (ci, small_blk)),
                  par, par, par],
        out_specs=[pl.BlockSpec((SSD_CHUNK, SSM_D_INNER), lambda ci: (ci, 0)),
                   pl.BlockSpec((1, npair, LANE, LANE), lambda ci: (ci, 0, 0, 0))],
        scratch_shapes=[pltpu.VMEM((npair, LANE, LANE), F32)],
        compiler_params=_cparams(("arbitrary",)),
    )(xbc, proj, dtb, alog, dsk)


def ssd_scan_bwd(name, xbc, proj, dtb, alog, dsk, s_in, dy):
    t = xbc.shape[0]
    nc, npair = t // SSD_CHUNK, SSM_HEADS // 2
    small_blk = AL_SMALL // LANE

    def body(xbc_ref, sm_ref, dtb_ref, alog_ref, dsk_ref, sin_ref, dy_ref,
             dxbc_ref, dsm_ref, ddtb_ref, dalog_ref, ddsk_ref, dst_ref):
        ci = pl.program_id(0)

        @pl.when(ci == 0)
        def _():
            dst_ref[...] = jnp.zeros_like(dst_ref)

        _, vjp = jax.vjp(ssd_pairs, tuple(xbc_ref[:, p * LANE:(p + 1) * LANE] for p in range(npair)),
                         tuple(xbc_ref[:, _SSD_B + g * LANE:_SSD_B + (g + 1) * LANE] for g in range(2)),
                         tuple(xbc_ref[:, _SSD_C + g * LANE:_SSD_C + (g + 1) * LANE] for g in range(2)),
                         sm_ref[...], dtb_ref[...], alog_ref[...], dsk_ref[...],
                         tuple(sin_ref[0, p] for p in range(npair)))
        dxs, dbms, dcms, dsm, ddtb, dalog, ddsk, dsts = vjp(
            (tuple(dy_ref[:, p * LANE:(p + 1) * LANE] for p in range(npair)),
             tuple(dst_ref[p] for p in range(npair))))
        for p in range(npair):
            dxbc_ref[:, p * LANE:(p + 1) * LANE] = dxs[p]
            dst_ref[p] = dsts[p]
        for g in range(2):
            dxbc_ref[:, _SSD_B + g * LANE:_SSD_B + (g + 1) * LANE] = dbms[g]
            dxbc_ref[:, _SSD_C + g * LANE:_SSD_C + (g + 1) * LANE] = dcms[g]
        dsm_ref[...] = dsm
        _acc(ddtb_ref, ddtb, ci == 0)
        _acc(dalog_ref, dalog, ci == 0)
        _acc(ddsk_ref, ddsk, ci == 0)

    par = pl.BlockSpec((1, LANE), lambda ci: (0, 0))
    rev = lambda ci: nc - 1 - ci
    return pl.pallas_call(
        body, name=name,
        out_shape=[jax.ShapeDtypeStruct((t, SSM_CONV_DIM), F32),
                   jax.ShapeDtypeStruct((t, LANE), F32),
                   jax.ShapeDtypeStruct((1, LANE), F32), jax.ShapeDtypeStruct((1, LANE), F32),
                   jax.ShapeDtypeStruct((1, LANE), F32)],
        grid=(nc,),
        in_specs=[pl.BlockSpec((SSD_CHUNK, SSM_CONV_DIM), lambda ci: (rev(ci), 0)),
                  pl.BlockSpec((SSD_CHUNK, LANE), lambda ci: (rev(ci), small_blk)),
                  par, par, par,
                  pl.BlockSpec((1, npair, LANE, LANE), lambda ci: (rev(ci), 0, 0, 0)),
                  pl.BlockSpec((SSD_CHUNK, SSM_D_INNER), lambda ci: (rev(ci), 0))],
        out_specs=[pl.BlockSpec((SSD_CHUNK, SSM_CONV_DIM), lambda ci: (rev(ci), 0)),
                   pl.BlockSpec((SSD_CHUNK, LANE), lambda ci: (rev(ci), 0)),
                   par, par, par],
        scratch_shapes=[pltpu.VMEM((npair, LANE, LANE), F32)],
        compiler_params=_cparams(("arbitrary",)),
    )(xbc, proj, dtb, alog, dsk, s_in, dy)


def gdn_scan_fwd(name, qkv, proj, alog, dtb):
    t = qkv.shape[0]
    nc, nh = t // CHUNK, GDN_HEADS
    small_blk = AL_SMALL // LANE

    def body(qkv_ref, sm_ref, alog_ref, dtb_ref, o_ref, sin_ref, x_ref, st_ref):
        ci = pl.program_id(0)

        @pl.when(ci == 0)
        def _():
            st_ref[...] = jnp.zeros_like(st_ref)

        s_in = tuple(st_ref[h] for h in range(nh))
        os, s_new, xinvs = gdn_heads(
            tuple(qkv_ref[:, h * LANE:(h + 1) * LANE] for h in range(nh)),
            tuple(qkv_ref[:, _GDN_K + h * LANE:_GDN_K + (h + 1) * LANE] for h in range(nh)),
            tuple(qkv_ref[:, _GDN_V + h * LANE:_GDN_V + (h + 1) * LANE] for h in range(nh)),
            sm_ref[...], alog_ref[...], dtb_ref[...], s_in)
        for h in range(nh):
            sin_ref[0, h] = s_in[h]
            o_ref[:, h * LANE:(h + 1) * LANE] = os[h]
            x_ref[0, h] = xinvs[h]
            st_ref[h] = s_new[h]

    par = pl.BlockSpec((1, LANE), lambda ci: (0, 0))
    return pl.pallas_call(
        body, name=name,
        out_shape=[jax.ShapeDtypeStruct((t, GDN_HEADS * GDN_HEAD), F32),
                   jax.ShapeDtypeStruct((nc, nh, LANE, LANE), F32),
                   jax.ShapeDtypeStruct((nc, nh, CHUNK, CHUNK), F32)],
        grid=(nc,),
        in_specs=[pl.BlockSpec((CHUNK, GDN_QKV_DIM), lambda ci: (ci, 0)),
                  pl.BlockSpec((CHUNK, LANE), lambda ci: (ci, small_blk)),
                  par, par],
        out_specs=[pl.BlockSpec((CHUNK, GDN_HEADS * GDN_HEAD), lambda ci: (ci, 0)),
                   pl.BlockSpec((1, nh, LANE, LANE), lambda ci: (ci, 0, 0, 0)),
                   pl.BlockSpec((1, nh, CHUNK, CHUNK), lambda ci: (ci, 0, 0, 0))],
        scratch_shapes=[pltpu.VMEM((nh, LANE, LANE), F32)],
        compiler_params=_cparams(("arbitrary",)),
    )(qkv, proj, alog, dtb)


def gdn_scan_bwd(name, qkv, proj, alog, dtb, s_in, xinv, do, dsm_in):
    t = qkv.shape[0]
    nc, nh = t // CHUNK, GDN_HEADS
    small_blk = AL_SMALL // LANE

    def body(qkv_ref, sm_ref, alog_ref, dtb_ref, sin_ref, x_ref, do_ref, dsmi_ref,
             dqkv_ref, dsm_ref, dalog_ref, ddtb_ref, dst_ref):
        ci = pl.program_id(0)

        @pl.when(ci == 0)
        def _():
            dst_ref[...] = jnp.zeros_like(dst_ref)

        xis = tuple(x_ref[0, h] for h in range(nh))

        def fn(qs, ks, vs, sm, alog_, dtb_, sts):
            os, s_new, _ = gdn_heads(qs, ks, vs, sm, alog_, dtb_, sts, xinvs=xis)
            return os, s_new

        _, vjp = jax.vjp(fn, tuple(qkv_ref[:, h * LANE:(h + 1) * LANE] for h in range(nh)),
                         tuple(qkv_ref[:, _GDN_K + h * LANE:_GDN_K + (h + 1) * LANE] for h in range(nh)),
                         tuple(qkv_ref[:, _GDN_V + h * LANE:_GDN_V + (h + 1) * LANE] for h in range(nh)),
                         sm_ref[...], alog_ref[...], dtb_ref[...], tuple(sin_ref[0, h] for h in range(nh)))
        dqs, dks, dvs, dsm, dalog, ddtb, dsts = vjp(
            (tuple(do_ref[:, h * LANE:(h + 1) * LANE] for h in range(nh)), tuple(dst_ref[h] for h in range(nh))))
        for h in range(nh):
            dqkv_ref[:, h * LANE:(h + 1) * LANE] = dqs[h]
            dqkv_ref[:, _GDN_K + h * LANE:_GDN_K + (h + 1) * LANE] = dks[h]
            dqkv_ref[:, _GDN_V + h * LANE:_GDN_V + (h + 1) * LANE] = dvs[h]
            dst_ref[h] = dsts[h]
        dsm_ref[...] = dsmi_ref[...] + dsm
        _acc(dalog_ref, dalog, ci == 0)
        _acc(ddtb_ref, ddtb, ci == 0)

    par = pl.BlockSpec((1, LANE), lambda ci: (0, 0))
    rev = lambda ci: nc - 1 - ci
    return pl.pallas_call(
        body, name=name,
        out_shape=[jax.ShapeDtypeStruct((t, GDN_QKV_DIM), F32), jax.ShapeDtypeStruct((t, LANE), F32),
                   jax.ShapeDtypeStruct((1, LANE), F32), jax.ShapeDtypeStruct((1, LANE), F32)],
        grid=(nc,),
        in_specs=[pl.BlockSpec((CHUNK, GDN_QKV_DIM), lambda ci: (rev(ci), 0)),
                  pl.BlockSpec((CHUNK, LANE), lambda ci: (rev(ci), small_blk)),
                  par, par,
                  pl.BlockSpec((1, nh, LANE, LANE), lambda ci: (rev(ci), 0, 0, 0)),
                  pl.BlockSpec((1, nh, CHUNK, CHUNK), lambda ci: (rev(ci), 0, 0, 0)),
                  pl.BlockSpec((CHUNK, GDN_HEADS * GDN_HEAD), lambda ci: (rev(ci), 0)),
                  pl.BlockSpec((CHUNK, LANE), lambda ci: (rev(ci), 0))],
        out_specs=[pl.BlockSpec((CHUNK, GDN_QKV_DIM), lambda ci: (rev(ci), 0)),
                   pl.BlockSpec((CHUNK, LANE), lambda ci: (rev(ci), 0)),
                   par, par],
        scratch_shapes=[pltpu.VMEM((nh, LANE, LANE), F32)],
        compiler_params=_cparams(("arbitrary",)),
    )(qkv, proj, alog, dtb, s_in, xinv, do, dsm_in)


def _row(arr, w, c0=0, moves=False):
    return (arr, "row", w, c0, moves)


def _par(arr, w, c0=0, moves=False):
    return (arr, "par", w, c0, moves)


def matmul_add(name, a, b, res):
    (m, k), (_, n) = a.shape, b.shape
    tm, tn, tk = _matmul_tiles(m, n, n, k, a.dtype.itemsize, b.dtype.itemsize, 4 + res.dtype.itemsize)
    nk = k // tk

    def body_acc(a_ref, b_ref, r_ref, o_ref, acc_ref):
        kk = pl.program_id(2)

        @pl.when(kk == 0)
        def _():
            acc_ref[...] = r_ref[...]

        acc_ref[...] += _dot(a_ref[...], b_ref[...])

        @pl.when(kk == nk - 1)
        def _():
            o_ref[...] = acc_ref[...]

    def body_one(a_ref, b_ref, r_ref, o_ref):
        o_ref[...] = r_ref[...] + _dot(a_ref[...], b_ref[...])

    body = body_one if nk == 1 else body_acc
    return pl.pallas_call(
        body, name=name,
        out_shape=jax.ShapeDtypeStruct((m, n), F32),
        grid=(m // tm, n // tn, nk),
        in_specs=[pl.BlockSpec((tm, tk), lambda i, j, kk: (i, kk)),
                  pl.BlockSpec((tk, tn), lambda i, j, kk: (kk, j)),
                  pl.BlockSpec((tm, tn), lambda i, j, kk: (i, j))],
        out_specs=pl.BlockSpec((tm, tn), lambda i, j, kk: (i, j)),
        scratch_shapes=[] if nk == 1 else [pltpu.VMEM((tm, tn), F32)],
        compiler_params=_cparams(("parallel", "parallel", "arbitrary")),
    )(a, b, res)


def layer_fwd(l, x, w):
    t = x.shape[0]
    rt = min(256, t)
    s = {"x": x}
    s["h"] = rowwise_fwd(f"norm_mix_l{l}", f_rmsnorm, t, rt, 1,
                         [_row(x, D_MODEL), _par(w["norm_mix_w"], D_MODEL)], [(D_MODEL, BF16)])[0]
    s["proj"] = matmul(f"in_proj_l{l}", s["h"], w["w_in"], "nn")
    s["xbc"] = conv_fwd(f"ssm_conv_l{l}", s["proj"], AL_XBC, SSM_CONV_DIM, w["ssm_conv_w"], w["ssm_conv_b"],
                        tile=min(512, t))
    s["qkv"] = conv_fwd(f"gdn_conv_l{l}", s["proj"], AL_QKV, GDN_QKV_DIM, w["gdn_conv_w"], w["gdn_conv_b"],
                        tile=min(512, t))
    s["y_scan"], s["ssd_sin"] = ssd_scan_fwd(f"ssd_scan_l{l}", s["xbc"], s["proj"], w["ssm_dt_bias"],
                                             w["ssm_a_log"], w["ssm_d"])
    s["o_scan"], s["gdn_sin"], s["gdn_x"] = gdn_scan_fwd(f"gdn_scan_l{l}", s["qkv"], s["proj"],
                                                         w["gdn_a_log"], w["gdn_dt_bias"])
    s["y_ssm"] = rowwise_fwd(f"ssd_post_l{l}", f_ssd_post, t, rt, 2,
                             [_row(s["y_scan"], 512, 0, True), _row(s["proj"], 512, AL_Z // 512, True),
                              _par(w["ssm_norm_w"], 512, 0, True)], [(512, BF16)])[0]
    s["y_gdn"] = rowwise_fwd(f"gdn_post_l{l}", f_gdn_post, t, rt, GDN_HEADS,
                             [_row(s["o_scan"], LANE, 0, True), _row(s["proj"], LANE, AL_GZ // LANE, True),
                              _par(w["gdn_norm_w"], LANE)], [(LANE, BF16)])[0]
    s["p1"] = matmul(f"proj_ssm_l{l}", s["y_ssm"], w["w_proj_ssm"], "nn")
    s["p2"] = matmul(f"proj_gdn_l{l}", s["y_gdn"], w["w_proj_gdn"], "nn")
    s["merged"] = rowwise_fwd(f"merge_l{l}", f_merge, t, rt, 2,
                              [_row(s["proj"], 512, AL_GS // 512, True), _row(s["p1"], 512, 0, True),
                               _row(s["proj"], 512, AL_GG // 512, True), _row(s["p2"], 512, 0, True)],
                              [(512, BF16)])[0]
    s["x1"] = matmul_add(f"out_proj_l{l}", s["merged"], w["w_out"], x)
    s["h2"] = rowwise_fwd(f"norm_ffn_l{l}", f_rmsnorm, t, rt, 1,
                          [_row(s["x1"], D_MODEL), _par(w["norm_ffn_w"], D_MODEL)], [(D_MODEL, BF16)])[0]
    s["gu"] = matmul(f"ffn_in_l{l}", s["h2"], w["w_ffn_in"], "nn")
    s["act"] = rowwise_fwd(f"swiglu_l{l}", f_swiglu, t, rt, FFN_HIDDEN // 256,
                           [_row(s["gu"], 256, 0, True), _row(s["gu"], 256, FFN_HIDDEN // 256, True)],
                           [(256, BF16)])[0]
    x2 = matmul_add(f"ffn_down_l{l}", s["act"], w["w_ffn_down"], s["x1"])
    return x2, s


IN_SHARD = IN_DIM // 4
IN_SHARD_PAD = 2304


def _aligned_to_shards(g):
    orig = jnp.concatenate([g[:, 0:2560], g[:, AL_SMALL:AL_SMALL + 16], g[:, 2560:6656],
                            g[:, AL_SMALL + 16:AL_SMALL + 32], g[:, 6656:8704]], axis=1)
    return jnp.stack([jnp.pad(orig[:, j * IN_SHARD:(j + 1) * IN_SHARD], ((0, 0), (0, IN_SHARD_PAD - IN_SHARD)))
                      for j in range(N_CHIPS)])


def layer_bwd(l, dx2, w, s, gbuf):
    t = dx2.shape[0]
    rt = min(256, t)
    ct = min(512, t)
    g = {}
    dact = matmul(f"ffn_down_dx_l{l}", dx2, w["w_ffn_down"], "nt")
    g["w_ffn_down"] = matmul(f"ffn_down_dw_l{l}", s["act"], dx2, "tn", stack=(l, gbuf.get("w_ffn_down")))
    nf = FFN_HIDDEN // 256
    dgate, dup = rowwise_bwd(f"swiglu_bwd_l{l}", f_swiglu, t, rt, nf,
                             [_row(s["gu"], 256, 0, True), _row(s["gu"], 256, nf, True)], [True, True],
                             [_row(dact, 256, 0, True)], row_dtypes={0: BF16, 1: BF16})
    dgu = jnp.concatenate([dgate, dup], axis=1)
    dh2 = matmul(f"ffn_in_dx_l{l}", dgu, w["w_ffn_in"], "nt")
    g["w_ffn_in"] = matmul(f"ffn_in_dw_l{l}", s["h2"], dgu, "tn", chip_major=True,
                           stack=(l, gbuf.get("w_ffn_in")))
    dx1, g["norm_ffn_w"] = rowwise_bwd(f"norm_ffn_bwd_l{l}", f_rmsnorm, t, rt, 1,
                                       [_row(s["x1"], D_MODEL), _par(w["norm_ffn_w"], D_MODEL)], [True, True],
                                       [_row(dh2, D_MODEL)], addends={0: _row(dx2, D_MODEL)})
    dmerged = matmul(f"out_proj_dx_l{l}", dx1, w["w_out"], "nt")
    g["w_out"] = matmul(f"out_proj_dw_l{l}", s["merged"], dx1, "tn", stack=(l, gbuf.get("w_out")))
    dgs, dp1, dgg, dp2 = rowwise_bwd(
        f"merge_bwd_l{l}", f_merge, t, rt, 2,
        [_row(s["proj"], 512, AL_GS // 512, True), _row(s["p1"], 512, 0, True),
         _row(s["proj"], 512, AL_GG // 512, True), _row(s["p2"], 512, 0, True)], [True] * 4,
        [_row(dmerged, 512, 0, True)], row_dtypes={0: BF16, 1: BF16, 2: BF16, 3: BF16})
    dy_ssm = matmul(f"proj_ssm_dx_l{l}", dp1, w["w_proj_ssm"], "nt")
    g["w_proj_ssm"] = matmul(f"proj_ssm_dw_l{l}", s["y_ssm"], dp1, "tn", stack=(l, gbuf.get("w_proj_ssm")))
    dy_gdn = matmul(f"proj_gdn_dx_l{l}", dp2, w["w_proj_gdn"], "nt")
    g["w_proj_gdn"] = matmul(f"proj_gdn_dw_l{l}", s["y_gdn"], dp2, "tn", stack=(l, gbuf.get("w_proj_gdn")))
    dy_scan, dz, g["ssm_norm_w"] = rowwise_bwd(
        f"ssd_post_bwd_l{l}", f_ssd_post, t, rt, 2,
        [_row(s["y_scan"], 512, 0, True), _row(s["proj"], 512, AL_Z // 512, True),
         _par(w["ssm_norm_w"], 512, 0, True)], [True] * 3, [_row(dy_ssm, 512, 0, True)], row_dtypes={1: BF16})
    dxbc_act, dsm, g["ssm_dt_bias"], g["ssm_a_log"], g["ssm_d"] = ssd_scan_bwd(
        f"ssd_scan_bwd_l{l}", s["xbc"], s["proj"], w["ssm_dt_bias"], w["ssm_a_log"], w["ssm_d"],
        s["ssd_sin"], dy_scan)
    dxbc, g["ssm_conv_w"], g["ssm_conv_b"] = conv_bwd(
        f"ssm_conv_bwd_l{l}", s["proj"], AL_XBC, SSM_CONV_DIM, w["ssm_conv_w"], w["ssm_conv_b"], dxbc_act, tile=ct)
    do_scan, dgz, g["gdn_norm_w"] = rowwise_bwd(
        f"gdn_post_bwd_l{l}", f_gdn_post, t, rt, GDN_HEADS,
        [_row(s["o_scan"], LANE, 0, True), _row(s["proj"], LANE, AL_GZ // LANE, True),
         _par(w["gdn_norm_w"], LANE)], [True] * 3, [_row(dy_gdn, LANE, 0, True)], row_dtypes={1: BF16})
    dqkv_act, dsm, g["gdn_a_log"], g["gdn_dt_bias"] = gdn_scan_bwd(
        f"gdn_scan_bwd_l{l}", s["qkv"], s["proj"], w["gdn_a_log"], w["gdn_dt_bias"], s["gdn_sin"],
        s["gdn_x"], do_scan, dsm)
    dqkv, g["gdn_conv_w"], _ = conv_bwd(
        f"gdn_conv_bwd_l{l}", s["proj"], AL_QKV, GDN_QKV_DIM, w["gdn_conv_w"], w["gdn_conv_b"], dqkv_act, tile=ct)
    dproj = jnp.concatenate([dz, dxbc, dqkv, dgz, dgs, dgg, dsm.astype(BF16),
                             jnp.zeros((t, AL_DIM - AL_SMALL - LANE), BF16)], axis=1)
    dh = matmul(f"in_proj_dx_l{l}", dproj, w["w_in"], "nt")
    g["w_in"] = matmul(f"in_proj_dw_l{l}", s["h"], dproj, "tn", stack=(l, gbuf.get("w_in")))
    dx0, g["norm_mix_w"] = rowwise_bwd(f"norm_mix_bwd_l{l}", f_rmsnorm, t, rt, 1,
                                       [_row(s["x"], D_MODEL), _par(w["norm_mix_w"], D_MODEL)], [True, True],
                                       [_row(dh, D_MODEL)], addends={0: _row(dx1, D_MODEL)})
    return dx0, g


def _align_w_in(w):
    pad = jnp.zeros((w.shape[0], AL_DIM - AL_SMALL - 32), w.dtype)
    return jnp.concatenate([w[:, 0:2560], w[:, 2576:6672], w[:, 6688:8736],
                            w[:, 2560:2576], w[:, 6672:6688], pad], axis=1)


def _pad_lane(v, at=0):
    return jnp.pad(v[None], ((0, 0), (at, LANE - at - v.shape[0])))


def local_step(x, target, full, gathered_of=None):
    if gathered_of is None:
        gathered_of = lambda l, after: {n: full[n][l] for n, _ in SHARDED}
    ws, saved = [], []
    h = x
    for l in range(DEPTH):
        gw = gathered_of(l, h)
        ws.append({
            "norm_mix_w": full["norm_mix_w"][l][None], "w_in": _align_w_in(gw["w_in"]),
            "ssm_conv_w": gw["ssm_conv_w"], "ssm_conv_b": full["ssm_conv_b"][l][None],
            "ssm_dt_bias": _pad_lane(full["ssm_dt_bias"][l]), "ssm_a_log": _pad_lane(full["ssm_a_log"][l]),
            "ssm_d": _pad_lane(full["ssm_d"][l]), "ssm_norm_w": full["ssm_norm_w"][l][None],
            "gdn_conv_w": gw["gdn_conv_w"], "gdn_conv_b": jnp.zeros((1, GDN_QKV_DIM), F32),
            "gdn_a_log": _pad_lane(full["gdn_a_log"][l], SM_A),
            "gdn_dt_bias": _pad_lane(full["gdn_dt_bias"][l], SM_A),
            "gdn_norm_w": full["gdn_norm_w"][l][None],
            "w_proj_ssm": gw["w_proj_ssm"], "w_proj_gdn": gw["w_proj_gdn"], "w_out": gw["w_out"],
            "norm_ffn_w": full["norm_ffn_w"][l][None], "w_ffn_in": gw["w_ffn_in"],
            "w_ffn_down": gw["w_ffn_down"],
        })
        h, s = layer_fwd(l, h, ws[l])
        saved.append(s)
    loss, dx, g_final = final_loss("final_loss", h, target, full["final_norm_w"][None], tile=min(256, x.shape[0]))
    per_layer = [None] * DEPTH
    gbuf = {}
    for l in reversed(range(DEPTH)):
        dx, per_layer[l] = layer_bwd(l, dx, ws[l], saved[l], gbuf)
        gbuf = {n: per_layer[l][n] for n, _ in BIG}
    grads = {"final_norm_w": g_final[0], **gbuf}
    for name in per_layer[0]:
        if name in gbuf:
            continue
        rows = []
        for l in range(DEPTH):
            gl = per_layer[l][name]
            if name in ("ssm_dt_bias", "ssm_a_log", "ssm_d"):
                gl = gl[0, :SSM_HEADS]
            elif name in ("gdn_a_log", "gdn_dt_bias"):
                gl = gl[0, SM_A:SM_A + GDN_HEADS]
            elif name in ("norm_mix_w", "ssm_conv_b", "ssm_norm_w", "gdn_norm_w", "norm_ffn_w"):
                gl = gl[0]
            rows.append(gl)
        grads[name] = jnp.stack(rows)
    return loss, dx, grads


MESH = pl.DeviceIdType.MESH
HBM = pl.BlockSpec(memory_space=pltpu.HBM)
N_DEV = 8


def _pos():
    return lax.axis_index("x"), lax.axis_index("y"), lax.axis_index("c")


def _rcopy(src, dst, send_sem, recv_sem, dev):
    return pltpu.make_async_remote_copy(src_ref=src, dst_ref=dst, send_sem=send_sem, recv_sem=recv_sem,
                                        device_id=dev, device_id_type=MESH)


RELATIONS = (2, 1, 3)


def _related_chip(x, y, mask):
    return (1 - x if mask & 2 else x, 1 - y if mask & 1 else y)


def weights_gather(name, bufs):
    n = len(bufs)

    def body(*refs):
        outs, send_sems, recv_sems = refs[n:2 * n], refs[2 * n], refs[2 * n + 1]
        x, y, c = _pos()
        sib = (x, y, 1 - c)
        sends = []
        for i, a in enumerate(outs):
            for k, m in enumerate(RELATIONS):
                px, py = _related_chip(x, y, m)
                cp = _rcopy(a.at[0, c], a.at[m, c], send_sems.at[6 * i + k], recv_sems.at[6 * i + k], (px, py, c))
                cp.start()
                sends.append(cp)
        for i, a in enumerate(outs):
            for k, m in enumerate(RELATIONS):
                px, py = _related_chip(x, y, m)
                _rcopy(a.at[0, c], a.at[m, c], send_sems.at[6 * i + k], recv_sems.at[6 * i + k],
                       (px, py, c)).wait_recv()
                fw = _rcopy(a.at[m, c], a.at[m, c], send_sems.at[6 * i + 3 + k], recv_sems.at[6 * i + 3 + k], sib)
                fw.start()
                sends.append(fw)
        for i, a in enumerate(outs):
            for k, m in enumerate(RELATIONS):
                _rcopy(a.at[m, 1 - c], a.at[m, 1 - c], send_sems.at[6 * i + 3 + k], recv_sems.at[6 * i + 3 + k],
                       sib).wait_recv()
        for cp in sends:
            cp.wait_send()

    return pl.pallas_call(
        body, name=name, out_shape=[jax.ShapeDtypeStruct(b.shape, b.dtype) for b in bufs],
        in_specs=[HBM] * n, out_specs=[HBM] * n,
        input_output_aliases={i: i for i in range(n)},
        scratch_shapes=[pltpu.SemaphoreType.DMA((6 * n,)), pltpu.SemaphoreType.DMA((6 * n,))],
    )(*bufs)


SEM = pl.BlockSpec(memory_space=pltpu.SEMAPHORE)
DATAFLOW = pltpu.SideEffectType.DATAFLOW_SIDE_EFFECTING


def gather_start(name, bufs, after):
    n = len(bufs)

    def body(*refs):
        ins = refs[:n]
        send_sems, recv_sems, token = refs[n + 1], refs[n + 2], refs[2 * n + 3]
        x, y, c = _pos()
        for i, a in enumerate(ins):
            for k, m in enumerate(RELATIONS):
                px, py = _related_chip(x, y, m)
                _rcopy(a.at[0, c], a.at[m, c], send_sems.at[3 * i + k], recv_sems.at[3 * i + k], (px, py, c)).start()
        token[...] = jnp.zeros_like(token)

    res = pl.pallas_call(
        body, name=name,
        out_shape=(pltpu.SemaphoreType.DMA((3 * n,)), pltpu.SemaphoreType.DMA((3 * n,)),
                   *[pltpu.HBM(b.shape, b.dtype) for b in bufs], jax.ShapeDtypeStruct((8, LANE), F32)),
        in_specs=[HBM] * n + [pl.BlockSpec(memory_space=pl.ANY)],
        out_specs=(SEM, SEM, *[HBM] * n, pl.BlockSpec(memory_space=pltpu.VMEM)),
        input_output_aliases={i: 2 + i for i in range(n)},
        compiler_params=pltpu.CompilerParams(has_side_effects=DATAFLOW),
    )(*[pltpu.with_memory_space_constraint(b, pltpu.HBM) for b in bufs], after)
    return res[0], res[1], list(res[2:2 + n]), res[2 + n]


def gather_wait(name, send_sems, recv_sems, bufs, after):
    n = len(bufs)

    def body(*refs):
        ins, ssem, rsem = refs[:n], refs[n], refs[n + 1]
        x, y, c = _pos()
        for i, a in enumerate(ins):
            for k, m in enumerate(RELATIONS):
                px, py = _related_chip(x, y, m)
                cp = _rcopy(a.at[0, c], a.at[m, c], ssem.at[3 * i + k], rsem.at[3 * i + k], (px, py, c))
                cp.wait_send()
                cp.wait_recv()

    return pl.pallas_call(
        body, name=name,
        out_shape=[pltpu.HBM(b.shape, b.dtype) for b in bufs],
        in_specs=[HBM] * n + [SEM, SEM, pl.BlockSpec(memory_space=pl.ANY)],
        out_specs=[HBM] * n,
        input_output_aliases={i: i for i in range(n)},
        compiler_params=pltpu.CompilerParams(has_side_effects=DATAFLOW),
    )(*bufs, send_sems, recv_sems, after)


def weights_forward(name, bufs):
    n = len(bufs)

    def body(*refs):
        outs, send_sems, recv_sems = refs[n:2 * n], refs[2 * n], refs[2 * n + 1]
        x, y, c = _pos()
        sib = (x, y, 1 - c)
        sends = []
        for i, a in enumerate(outs):
            for k, m in enumerate(RELATIONS):
                fw = _rcopy(a.at[m, c], a.at[m, c], send_sems.at[3 * i + k], recv_sems.at[3 * i + k], sib)
                fw.start()
                sends.append(fw)
        for i, a in enumerate(outs):
            for k, m in enumerate(RELATIONS):
                _rcopy(a.at[m, 1 - c], a.at[m, 1 - c], send_sems.at[3 * i + k], recv_sems.at[3 * i + k],
                       sib).wait_recv()
        for cp in sends:
            cp.wait_send()

    return pl.pallas_call(
        body, name=name, out_shape=[jax.ShapeDtypeStruct(b.shape, b.dtype) for b in bufs],
        in_specs=[HBM] * n, out_specs=[HBM] * n,
        input_output_aliases={i: i for i in range(n)},
        scratch_shapes=[pltpu.SemaphoreType.DMA((3 * n,)), pltpu.SemaphoreType.DMA((3 * n,))],
    )(*bufs)


def pair_swap(name, gs):
    n = len(gs)

    def body(*refs):
        srcs, outs, send_sems, recv_sems = refs[:n], refs[n:2 * n], refs[2 * n], refs[2 * n + 1]
        x, y, c = _pos()
        cps = [_rcopy(s.at[1 - c], o, send_sems.at[i], recv_sems.at[i], (x, y, 1 - c))
               for i, (s, o) in enumerate(zip(srcs, outs))]
        for cp in cps:
            cp.start()
        for cp in cps:
            cp.wait()

    return pl.pallas_call(
        body, name=name, out_shape=[jax.ShapeDtypeStruct(g.shape[1:], g.dtype) for g in gs],
        in_specs=[HBM] * n, out_specs=[HBM] * n,
        scratch_shapes=[pltpu.SemaphoreType.DMA((n,)), pltpu.SemaphoreType.DMA((n,))],
    )(*gs)


def chip_scatter(name, ss):
    n = len(ss)

    def body(*refs):
        srcs, outs, send_sems, recv_sems = refs[:n], refs[n:2 * n], refs[2 * n], refs[2 * n + 1]
        x, y, c = _pos()
        sends = []
        for i, (s, o) in enumerate(zip(srcs, outs)):
            for k, m in enumerate(RELATIONS):
                px, py = _related_chip(x, y, m)
                cp = _rcopy(s.at[2 * px + py], o.at[k], send_sems.at[3 * i + k], recv_sems.at[3 * i + k],
                            (px, py, c))
                cp.start()
                sends.append(cp)
        for cp in sends:
            cp.wait()

    return pl.pallas_call(
        body, name=name, out_shape=[jax.ShapeDtypeStruct((3,) + s.shape[1:], s.dtype) for s in ss],
        in_specs=[HBM] * n, out_specs=[HBM] * n,
        scratch_shapes=[pltpu.SemaphoreType.DMA((3 * n,)), pltpu.SemaphoreType.DMA((3 * n,))],
    )(*ss)


def pair_share(name, bufs):
    n = len(bufs)

    def body(*refs):
        outs, send_sems, recv_sems = refs[n:2 * n], refs[2 * n], refs[2 * n + 1]
        x, y, c = _pos()
        sends = []
        for i, o in enumerate(outs):
            cp = _rcopy(o.at[c], o.at[c], send_sems.at[i], recv_sems.at[i], (x, y, 1 - c))
            cp.start()
            sends.append(cp)
        for i, o in enumerate(outs):
            _rcopy(o.at[1 - c], o.at[1 - c], send_sems.at[i], recv_sems.at[i], (x, y, 1 - c)).wait_recv()
        for cp in sends:
            cp.wait_send()

    return pl.pallas_call(
        body, name=name, out_shape=[jax.ShapeDtypeStruct(b.shape, b.dtype) for b in bufs],
        in_specs=[HBM] * n, out_specs=[HBM] * n,
        input_output_aliases={i: i for i in range(n)},
        scratch_shapes=[pltpu.SemaphoreType.DMA((n,)), pltpu.SemaphoreType.DMA((n,))],
    )(*bufs)


def all_allgather(name, buf):
    r, cd = buf.shape

    def body(src, out, send_sems, recv_sems, lsem):
        x, y, c = _pos()
        me = 4 * x + 2 * y + c
        local = pltpu.make_async_copy(src, out.at[me], lsem)
        local.start()

        def peer(mask):
            px = 1 - x if mask & 4 else x
            py = 1 - y if mask & 2 else y
            pc = 1 - c if mask & 1 else c
            return px, py, pc

        sends = []
        for mask in range(1, N_DEV):
            cp = _rcopy(src, out.at[me], send_sems.at[mask - 1], recv_sems.at[mask - 1], peer(mask))
            cp.start()
            sends.append(cp)
        for mask in range(1, N_DEV):
            px, py, pc = peer(mask)
            _rcopy(src, out.at[4 * px + 2 * py + pc], send_sems.at[mask - 1], recv_sems.at[mask - 1],
                   (px, py, pc)).wait_recv()
        for cp in sends:
            cp.wait_send()
        local.wait()

    return pl.pallas_call(
        body, name=name, out_shape=jax.ShapeDtypeStruct((N_DEV, r, cd), buf.dtype),
        in_specs=[HBM], out_specs=HBM,
        scratch_shapes=[pltpu.SemaphoreType.DMA((N_DEV - 1,)), pltpu.SemaphoreType.DMA((N_DEV - 1,)),
                        pltpu.SemaphoreType.DMA(())],
    )(buf)


ELEMENTWISE_BLOCK_BYTES = 2 << 20


def _row_block(rows, cols):
    for cand in (1024, 512, 256, 128, 64, 32, 16):
        if rows % cand == 0 and cand * cols * 4 <= ELEMENTWISE_BLOCK_BYTES:
            return cand
    return rows


def chip_sum(name, s, r, me, c):
    _, a, b = s.shape
    tr = _row_block(a, b)

    def body(idx_ref, s_ref, r_ref, o_ref):
        del idx_ref
        acc = s_ref[...].astype(F32)
        for k in range(3):
            acc = acc + r_ref[k].astype(F32)
        o_ref[...] = acc

    return pl.pallas_call(
        body, name=name, out_shape=jax.ShapeDtypeStruct((2, a, b), F32),
        grid_spec=pltpu.PrefetchScalarGridSpec(
            num_scalar_prefetch=1, grid=(a // tr,),
            in_specs=[pl.BlockSpec((None, tr, b), lambda i, idx: (idx[0], i, 0)),
                      pl.BlockSpec((3, tr, b), lambda i, idx: (0, i, 0))],
            out_specs=pl.BlockSpec((None, tr, b), lambda i, idx: (idx[1], i, 0))),
        compiler_params=_cparams(("arbitrary",)),
    )(jnp.stack([me, c]).astype(jnp.int32), s, r)


def pair_add(name, p, recv, c):
    _, nj, rh, cd = p.shape
    tr = _row_block(rh, cd)

    def body(c_ref, p_ref, r_ref, o_ref):
        del c_ref
        o_ref[...] = (p_ref[0] + r_ref[...]).astype(o_ref.dtype)

    return pl.pallas_call(
        body, name=name, out_shape=jax.ShapeDtypeStruct((nj, rh, cd), BF16),
        grid_spec=pltpu.PrefetchScalarGridSpec(
            num_scalar_prefetch=1, grid=(nj, rh // tr),
            in_specs=[pl.BlockSpec((1, 1, tr, cd), lambda j, i, c_ref: (c_ref[0], j, i, 0)),
                      pl.BlockSpec((1, tr, cd), lambda j, i, c_ref: (j, i, 0))],
            out_specs=pl.BlockSpec((1, tr, cd), lambda j, i, c_ref: (j, i, 0))),
        compiler_params=_cparams(("arbitrary", "arbitrary")),
    )(jnp.reshape(c, (1,)).astype(jnp.int32), p, recv)


def slab_sum(name, a):
    n, r, cd = a.shape
    tr = _pick(r, (256, 128, 64, 32, 16, 8))

    def body(a_ref, o_ref):
        acc = a_ref[0].astype(F32)
        for j in range(1, n):
            acc = acc + a_ref[j].astype(F32)
        o_ref[...] = acc

    return pl.pallas_call(
        body, name=name, out_shape=jax.ShapeDtypeStruct((r, cd), F32),
        grid=(r // tr,),
        in_specs=[pl.BlockSpec((n, tr, cd), lambda i: (0, i, 0))],
        out_specs=pl.BlockSpec((tr, cd), lambda i: (i, 0)),
        compiler_params=_cparams(("arbitrary",)),
    )(a)


ADAM_C1 = 1.0 - ADAM_B1 ** ADAM_STEP
ADAM_C2 = 1.0 - ADAM_B2 ** ADAM_STEP


def adamw(name, w, g, m, v):
    r, cd = w.shape
    tr = r
    for cand in (512, 256, 128, 64, 32, 16, 8):
        if r % cand == 0 and cand * cd * 4 <= (1 << 20):
            tr = cand
            break

    def body(w_ref, g_ref, m_ref, v_ref, d_ref, nm_ref, nv_ref):
        gv = g_ref[...]
        nm = ADAM_B1 * m_ref[...] + (1.0 - ADAM_B1) * gv
        nv = ADAM_B2 * v_ref[...] + (1.0 - ADAM_B2) * (gv * gv)
        m_hat = nm / ADAM_C1
        v_hat = nv / ADAM_C2
        d_ref[...] = -ADAM_LR * (m_hat / (jnp.sqrt(v_hat) + ADAM_EPS) + ADAM_WD * w_ref[...])
        nm_ref[...] = nm
        nv_ref[...] = nv

    spec = pl.BlockSpec((tr, cd), lambda i: (i, 0))
    sd = jax.ShapeDtypeStruct((r, cd), F32)
    return pl.pallas_call(
        body, name=name, out_shape=[sd, sd, sd], grid=(r // tr,),
        in_specs=[spec] * 4, out_specs=[spec] * 3,
        compiler_params=_cparams(("arbitrary",)),
    )(w, g, m, v)


WEIGHTS = ("norm_mix_w", "w_in", "ssm_conv_w", "ssm_conv_b", "ssm_dt_bias", "ssm_a_log", "ssm_d", "ssm_norm_w",
           "gdn_conv_w", "gdn_a_log", "gdn_dt_bias", "gdn_norm_w", "w_proj_ssm", "w_proj_gdn", "w_out",
           "norm_ffn_w", "w_ffn_in", "w_ffn_down", "final_norm_w")
BIG = (("w_in", 2), ("w_proj_ssm", 1), ("w_proj_gdn", 1), ("w_out", 1), ("w_ffn_in", 2), ("w_ffn_down", 1))
CONVW = (("ssm_conv_w", 2), ("gdn_conv_w", 2))
SHARDED = BIG + CONVW
SMALL = tuple(n for n in WEIGHTS if n not in dict(SHARDED))


def _unpack(buf, shapes, lead=()):
    flat = buf.reshape(lead + (-1,))
    out, o = [], 0
    for shp in shapes:
        n = math.prod(shp)
        out.append(flat[..., o:o + n].reshape(lead + tuple(shp)))
        o += n
    return out


def _pack_rows(arrs, lead=(), mult=8):
    nl = len(lead)
    flat = jnp.concatenate([a.reshape(lead + (-1,)) for a in arrs], axis=nl)
    n = flat.shape[nl]
    rows = -(-n // (mult * LANE)) * mult
    flat = jnp.pad(flat, [(0, 0)] * nl + [(0, rows * LANE - n)])
    return flat.reshape(lead + (rows, LANE))


def _slot_buffer(shard):
    return jnp.pad(shard[None], [(0, N_CHIPS - 1)] + [(0, 0)] * shard.ndim)


def kernel(x, norm_mix_w, w_in, ssm_conv_w, ssm_conv_b, ssm_dt_bias, ssm_a_log, ssm_d, ssm_norm_w, gdn_conv_w, gdn_a_log, gdn_dt_bias, gdn_norm_w, w_proj_ssm, w_proj_gdn, w_out, norm_ffn_w, w_ffn_in, w_ffn_down, final_norm_w, loss_target, m_norm_mix_w, m_w_in, m_ssm_conv_w, m_ssm_conv_b, m_ssm_dt_bias, m_ssm_a_log, m_ssm_d, m_ssm_norm_w, m_gdn_conv_w, m_gdn_a_log, m_gdn_dt_bias, m_gdn_norm_w, m_w_proj_ssm, m_w_proj_gdn, m_w_out, m_norm_ffn_w, m_w_ffn_in, m_w_ffn_down, m_final_norm_w, v_norm_mix_w, v_w_in, v_ssm_conv_w, v_ssm_conv_b, v_ssm_dt_bias, v_ssm_a_log, v_ssm_d, v_ssm_norm_w, v_gdn_conv_w, v_gdn_a_log, v_gdn_dt_bias, v_gdn_norm_w, v_w_proj_ssm, v_w_proj_gdn, v_w_out, v_norm_ffn_w, v_w_ffn_in, v_w_ffn_down, v_final_norm_w):
    wl = (norm_mix_w, w_in, ssm_conv_w, ssm_conv_b, ssm_dt_bias, ssm_a_log, ssm_d, ssm_norm_w, gdn_conv_w,
          gdn_a_log, gdn_dt_bias, gdn_norm_w, w_proj_ssm, w_proj_gdn, w_out, norm_ffn_w, w_ffn_in, w_ffn_down,
          final_norm_w)
    ml = (m_norm_mix_w, m_w_in, m_ssm_conv_w, m_ssm_conv_b, m_ssm_dt_bias, m_ssm_a_log, m_ssm_d, m_ssm_norm_w,
          m_gdn_conv_w, m_gdn_a_log, m_gdn_dt_bias, m_gdn_norm_w, m_w_proj_ssm, m_w_proj_gdn, m_w_out,
          m_norm_ffn_w, m_w_ffn_in, m_w_ffn_down, m_final_norm_w)
    vl = (v_norm_mix_w, v_w_in, v_ssm_conv_w, v_ssm_conv_b, v_ssm_dt_bias, v_ssm_a_log, v_ssm_d, v_ssm_norm_w,
          v_gdn_conv_w, v_gdn_a_log, v_gdn_dt_bias, v_gdn_norm_w, v_w_proj_ssm, v_w_proj_gdn, v_w_out,
          v_norm_ffn_w, v_w_ffn_in, v_w_ffn_down, v_final_norm_w)
    w = dict(zip(WEIGHTS, wl))
    m = dict(zip(WEIGHTS, ml))
    v = dict(zip(WEIGHTS, vl))
    x_pos, y_pos, c = _pos()
    me = 2 * x_pos + y_pos
    big = [n for n, _ in BIG]

    shards = [w[n].astype(BF16) for n in big]
    shards[0] = jnp.pad(shards[0], ((0, 0), (0, 0), (0, IN_SHARD_PAD - IN_SHARD)))
    conv_shapes = [w[n].shape[1:] for n, _ in CONVW]
    conv_pack = _pack_rows([w[n] for n, _ in CONVW], lead=(DEPTH,), mult=16)

    def slot_buffers(l):
        return [_slot_buffer(s[l].reshape((2, s.shape[1] // 2) + s.shape[2:])) for s in shards + [conv_pack]]

    def assemble(bufs):
        by_chip = [[lax.dynamic_index_in_dim(g_, jnp.bitwise_xor(me, j), 0, keepdims=False)
                    .reshape((-1,) + g_.shape[3:]) for j in range(N_CHIPS)] for g_ in bufs]
        out = {}
        for i, (n, axis) in enumerate(BIG):
            cols = IN_SHARD if n == "w_in" else by_chip[i][0].shape[-1]
            out[n] = jnp.concatenate([p[:, :cols] for p in by_chip[i]], axis=axis - 1)
        conv_parts = [_unpack(p, conv_shapes) for p in by_chip[-1]]
        for i, (n, axis) in enumerate(CONVW):
            out[n] = jnp.concatenate([conv_parts[j][i] for j in range(N_CHIPS)], axis=axis - 1)
        return out

    landed0 = weights_gather("gather_w_l0", slot_buffers(0))
    send_sems, recv_sems, flying, token = gather_start("gather_w_l1_start", slot_buffers(1), landed0[0])

    def gathered_of(l, after):
        if l == 0:
            return assemble(landed0)
        landed1 = gather_wait("gather_w_l1_wait", send_sems, recv_sems, flying, after)
        return assemble(weights_forward("gather_w_l1_forward", landed1))

    full = {n: w[n] for n in SMALL}
    loss_part, grad_x, grads = local_step(x[0] + token[0, 0], loss_target[0], full, gathered_of)

    by4 = [grads[n][:, None] if n == "w_in" else grads[n].reshape((DEPTH, N_CHIPS, -1) + grads[n].shape[-1:])
           for n in big]
    from_pair = pair_swap("grad_pair_swap", by4)
    chip_part = [pair_add(f"grad_pair_add_{n}", g_, r_, c) for n, g_, r_ in zip(big, by4, from_pair)]
    chip_part[0] = _aligned_to_shards(chip_part[0][0])
    from_chips = chip_scatter("grad_chip_scatter", chip_part)
    halves = [chip_sum(f"grad_chip_sum_{n}", s_, r_, me, c) for n, s_, r_ in zip(big, chip_part, from_chips)]
    reduced = pair_share("grad_pair_share", halves)
    g_sharded = dict(zip(big, reduced))
    g_sharded["w_in"] = g_sharded["w_in"][:, :, :IN_SHARD]

    small_names = list(SMALL) + [n for n, _ in CONVW]
    small_all = all_allgather("gather_small", _pack_rows([grads[n] for n in small_names] + [loss_part[0, :1]]))
    small_sum = slab_sum("small_sum", small_all)
    small_vals = _unpack(small_sum, [grads[n].shape for n in small_names] + [(1,)])
    g_small = dict(zip(small_names, small_vals[:-1]))
    loss = small_vals[-1].reshape(())
    for n, axis in CONVW:
        size = w[n].shape[axis]
        g_sharded[n] = lax.dynamic_slice_in_dim(g_small.pop(n), me * size, size, axis=axis)

    out_g, out_d, out_m, out_v = {}, {}, {}, {}
    for n, _ in SHARDED:
        shp = w[n].shape
        two = lambda a: a.reshape(-1, shp[-1])
        d_, m_, v_ = adamw(f"adamw_{n}", two(w[n]), two(g_sharded[n]), two(m[n]), two(v[n]))
        out_g[n], out_d[n], out_m[n], out_v[n] = g_sharded[n], d_.reshape(shp), m_.reshape(shp), v_.reshape(shp)
    d_, m_, v_ = adamw("adamw_small", *[_pack_rows([d[n] for n in SMALL]) for d in (w, g_small, m, v)])
    small_shapes = [w[n].shape for n in SMALL]
    for n, dd, mm, vv in zip(SMALL, _unpack(d_, small_shapes), _unpack(m_, small_shapes), _unpack(v_, small_shapes)):
        out_g[n], out_d[n], out_m[n], out_v[n] = g_small[n], dd, mm, vv

    return (loss, grad_x[None], *[out_g[n] for n in WEIGHTS], *[out_d[n] for n in WEIGHTS],
            *[out_m[n] for n in WEIGHTS], *[out_v[n] for n in WEIGHTS])
```

```python
import math

import jax
import jax.numpy as jnp
from jax import lax
from jax.experimental import pallas as pl
from jax.experimental.pallas import tpu as pltpu

F32 = jnp.float32
BF16 = jnp.bfloat16

D_MODEL = 1024
DEPTH = 2
SSM_HEADS = 16
SSM_HEAD_DIM = 64
SSM_D_INNER = 1024
SSM_STATE = 128
SSM_CONV_DIM = 1536
GDN_HEADS = 8
GDN_HEAD = 128
GDN_QKV_DIM = 3072
CONV_K = 4
CHUNK = 128
SSD_CHUNK = 256
FFN_HIDDEN = 2816
EPS = 1e-6
IN_DIM = 8736

ADAM_LR = 0.001
ADAM_B1 = 0.9
ADAM_B2 = 0.999
ADAM_EPS = 1e-08
ADAM_WD = 0.01
ADAM_STEP = 10

LANE = 128
NEG_BIG = -1e30
VMEM_LIMIT = 56 * 1024 * 1024

AL_Z, AL_XBC, AL_QKV, AL_GZ, AL_GS, AL_GG, AL_SMALL, AL_DIM = 0, 1024, 2560, 5632, 6656, 7680, 8704, 9216
SM_DT, SM_A, SM_B = 0, 16, 24

HI = lax.Precision.HIGHEST
NN = (((1,), (0,)), ((), ()))
NT = (((1,), (1,)), ((), ()))
TN = (((0,), (0,)), ((), ()))


def _cparams(sem):
    return pltpu.CompilerParams(dimension_semantics=sem, vmem_limit_bytes=VMEM_LIMIT)


def _pick(n, prefs):
    for p in prefs:
        if n % p == 0:
            return p
    return n


MATMUL_VMEM_BUDGET = 40 << 20
MXU_WIDTH = 256
HBM_BYTES_PER_S = 3.3e12
MXU_FLOPS_PER_S = 9.0e14
GRID_STEP_S = 0.35e-6
N_CHIPS = 4


def _matmul_tiles(m, n, n_dom, k, a_item, b_item, o_item):
    best = None
    def cands(dim, cap):
        return [c for c in range(LANE, min(dim, cap) + 1, LANE) if dim % c == 0] or [dim]

    tms, tns, tks = cands(m, 2048), cands(n_dom, 2304), cands(k, 1 << 30)
    for tm in tms:
        for tn in tns:
            for tk in tks:
                nk = k // tk
                vmem = (2 * (tm * tk * a_item + tk * tn * b_item + tm * tn * o_item) + tm * tn * 4 * (2 if nk > 1 else 1)
                        + (tm * tk * 2 if a_item > 2 else 0) + (tk * tn * 2 if b_item > 2 else 0))
                if vmem > MATMUL_VMEM_BUDGET:
                    continue
                traffic = m * k * a_item * (1 if nk == 1 else n // tn) + k * n * b_item * (m // tm)
                mxu_fill = tn / (-(-tn // MXU_WIDTH) * MXU_WIDTH)
                cost = (max(traffic / HBM_BYTES_PER_S, 2.0 * m * n * k / (MXU_FLOPS_PER_S * mxu_fill))
                        + (m // tm) * (n // tn) * nk * GRID_STEP_S)
                if best is None or cost < best[0]:
                    best = (cost, (tm, tn, tk))
    return best[1]


def _dot(a, b, dims=NN):
    return lax.dot_general(a.astype(BF16), b.astype(BF16), dims, preferred_element_type=F32)


def _dot3(a, b, dims=NN):
    a_hi, b_hi = a.astype(BF16), b.astype(BF16)
    a_lo = (a - a_hi.astype(F32)).astype(BF16)
    b_lo = (b - b_hi.astype(F32)).astype(BF16)

    def dg(u, v):
        return lax.dot_general(u, v, dims, preferred_element_type=F32)

    return dg(a_hi, b_hi) + (dg(a_hi, b_lo) + dg(a_lo, b_hi))


def _dot_hi(a, b, dims=NN):
    return lax.dot_general(a, b, dims, precision=HI, preferred_element_type=F32)


def _sigmoid(x):
    return jax.nn.sigmoid(x)


def _silu(x):
    return x * _sigmoid(x)


def _softplus(x):
    return jnp.maximum(x, 0.0) + jnp.log1p(jnp.exp(-jnp.abs(x)))


def matmul(name, a, b, mode, out_dtype=F32, chip_major=False, stack=None):
    if mode == "nn":
        (m, k), (k2, n) = a.shape, b.shape
    elif mode == "nt":
        (m, k), (n, k2) = a.shape, b.shape
    else:
        (k, m), (k2, n) = a.shape, b.shape
    assert k == k2, (a.shape, b.shape, mode)
    shard = n // N_CHIPS if chip_major else n
    tm, tn, tk = _matmul_tiles(m, n, shard, k, a.dtype.itemsize, b.dtype.itemsize, jnp.dtype(out_dtype).itemsize)
    if chip_major:
        per = shard // tn
        base_shape, base_blk = (N_CHIPS, m, shard), (None, tm, tn)
        base_idx = lambda i, j: (j // per, i, j % per)
    else:
        base_shape, base_blk = (m, n), (tm, tn)
        base_idx = lambda i, j: (i, j)
    nk = k // tk
    dims = {"nn": NN, "nt": NT, "tn": TN}[mode]

    def body_acc(a_ref, b_ref, o_ref, acc_ref):
        kk = pl.program_id(2)

        @pl.when(kk == 0)
        def _():
            acc_ref[...] = jnp.zeros_like(acc_ref)

        acc_ref[...] += _dot(a_ref[...], b_ref[...], dims)

        @pl.when(kk == nk - 1)
        def _():
            o_ref[...] = acc_ref[...].astype(o_ref.dtype)

    def body_one(a_ref, b_ref, o_ref):
        o_ref[...] = _dot(a_ref[...], b_ref[...], dims).astype(o_ref.dtype)

    compute = body_one if nk == 1 else body_acc
    if mode == "tn":
        a_spec = pl.BlockSpec((tk, tm), lambda i, j, kk: (kk, i))
    else:
        a_spec = pl.BlockSpec((tm, tk), lambda i, j, kk: (i, kk))
    if mode == "nt":
        b_spec = pl.BlockSpec((tn, tk), lambda i, j, kk: (j, kk))
    else:
        b_spec = pl.BlockSpec((tk, tn), lambda i, j, kk: (kk, j))
    in_specs, operands, aliases, body = [a_spec, b_spec], [a, b], {}, compute
    if stack is None:
        out_shape, out_blk, out_idx = base_shape, base_blk, (lambda i, j, kk: base_idx(i, j))
    else:
        layer, buf = stack
        out_shape, out_blk = (DEPTH,) + base_shape, (None,) + base_blk
        out_idx = lambda i, j, kk: (layer,) + base_idx(i, j)
        if buf is not None:
            assert buf.shape == out_shape and buf.dtype == out_dtype
            in_specs.append(pl.BlockSpec(memory_space=pl.ANY))
            operands.append(buf)
            aliases = {2: 0}

            def body(a_ref, b_ref, buf_ref, *rest):
                del buf_ref
                compute(a_ref, b_ref, *rest)

    return pl.pallas_call(
        body, name=name,
        out_shape=jax.ShapeDtypeStruct(out_shape, out_dtype),
        grid=(m // tm, n // tn, nk),
        in_specs=in_specs,
        out_specs=pl.BlockSpec(out_blk, out_idx),
        scratch_shapes=[] if nk == 1 else [pltpu.VMEM((tm, tn), F32)],
        input_output_aliases=aliases,
        compiler_params=_cparams(("parallel", "parallel", "arbitrary")),
    )(*operands)


def _row_map(c0, moves):
    return (lambda j, i: (i, c0 + j)) if moves else (lambda j, i: (i, c0))


def _par_map(c0, moves):
    return (lambda j, i: (0, c0 + j)) if moves else (lambda j, i: (0, c0))


def _in_spec(op, tile):
    _, kind, w, c0, moves = op
    if kind == "row":
        return pl.BlockSpec((tile, w), _row_map(c0, moves))
    return pl.BlockSpec((1, w), _par_map(c0, moves))


ROW_BLOCK_ELEMS = 1 << 18


def _row_tile(t, tile, ops):
    width = max(op[2] for op in ops if op[1] == "row")
    return min(t, max(tile, ROW_BLOCK_ELEMS // width))


def rowwise_fwd(name, fn, t, tile, ncol, ins, outs):
    n_in = len(ins)
    tile = _row_tile(t, tile, ins)

    def body(*refs):
        vals = [r[...].astype(F32) for r in refs[:n_in]]
        res = fn(*vals)
        if not isinstance(res, (tuple, list)):
            res = (res,)
        for r, v in zip(refs[n_in:], res):
            r[...] = v.astype(r.dtype)

    res = pl.pallas_call(
        body, name=name,
        out_shape=[jax.ShapeDtypeStruct((t, w * ncol), dt) for w, dt in outs],
        grid=(ncol, t // tile),
        in_specs=[_in_spec(op, tile) for op in ins],
        out_specs=[pl.BlockSpec((tile, w), _row_map(0, True)) for w, _ in outs],
        compiler_params=_cparams(("arbitrary", "arbitrary")),
    )(*[op[0] for op in ins])
    return res


def rowwise_bwd(name, fn, t, tile, ncol, ins, need, cts, addends=None, row_dtypes=None):
    n_in, n_ct = len(ins), len(cts)
    tile = _row_tile(t, tile, ins)
    addends = addends or {}
    row_dtypes = row_dtypes or {}
    didx = [i for i in range(n_in) if need[i]]
    add_ops = [addends[i] for i in didx if i in addends]
    n_add = len(add_ops)

    def body(*refs):
        in_refs = refs[:n_in]
        ct_refs = refs[n_in:n_in + n_ct]
        add_refs = refs[n_in + n_ct:n_in + n_ct + n_add]
        out_refs = refs[n_in + n_ct + n_add:]
        vals = [r[...].astype(F32) for r in in_refs]

        def g(*dv):
            full = list(vals)
            for i, v in zip(didx, dv):
                full[i] = v
            res = fn(*full)
            return tuple(res) if isinstance(res, (tuple, list)) else (res,)

        _, vjp = jax.vjp(g, *[vals[i] for i in didx])
        grads = vjp(tuple(c[...].astype(F32) for c in ct_refs))
        j, i = pl.program_id(0), pl.program_id(1)
        a = 0
        for o_ref, gv, idx in zip(out_refs, grads, didx):
            _, kind, _, _, moves = ins[idx]
            if kind == "row":
                if idx in addends:
                    gv = gv + add_refs[a][...].astype(F32)
                    a += 1
                o_ref[...] = gv.astype(o_ref.dtype)
            else:
                first = (i == 0) if moves else jnp.logical_and(i == 0, j == 0)

                @pl.when(first)
                def _(o_ref=o_ref, gv=gv):
                    o_ref[...] = gv

                @pl.when(jnp.logical_not(first))
                def _(o_ref=o_ref, gv=gv):
                    o_ref[...] += gv

    out_shape, out_specs = [], []
    for idx in didx:
        _, kind, w, _, moves = ins[idx]
        cols = w * (ncol if moves else 1)
        if kind == "row":
            out_shape.append(jax.ShapeDtypeStruct((t, cols), row_dtypes.get(idx, F32)))
            out_specs.append(pl.BlockSpec((tile, w), _row_map(0, moves)))
        else:
            out_shape.append(jax.ShapeDtypeStruct((1, cols), F32))
            out_specs.append(pl.BlockSpec((1, w), _par_map(0, moves)))
    ops = list(ins) + list(cts) + add_ops
    res = pl.pallas_call(
        body, name=name,
        out_shape=out_shape,
        grid=(ncol, t // tile),
        in_specs=[_in_spec(op, tile) for op in ops],
        out_specs=out_specs,
        compiler_params=_cparams(("arbitrary", "arbitrary")),
    )(*[op[0] for op in ops])
    return res


def f_rmsnorm(x, w):
    return x * lax.rsqrt(jnp.mean(x * x, axis=-1, keepdims=True) + EPS) * w


def f_ssd_post(y, z, w):
    y = y * _silu(z)
    return y * lax.rsqrt(jnp.mean(y * y, axis=-1, keepdims=True) + EPS) * w


def f_gdn_post(o, z, w):
    o = o * lax.rsqrt(jnp.mean(o * o, axis=-1, keepdims=True) + EPS) * w
    return o * _silu(z)


def f_merge(gs, p1, gg, p2):
    return _sigmoid(gs) * p1 + _sigmoid(gg) * p2


def f_swiglu(g, u):
    return _silu(g) * u


def final_loss(name, x, tgt, w, tile=256):
    t, d = x.shape

    def body(x_ref, t_ref, w_ref, loss_ref, dx_ref, dw_ref):
        i = pl.program_id(0)
        xv, tv, wv = x_ref[...], t_ref[...], w_ref[...]

        def g(xx, ww):
            err = f_rmsnorm(xx, ww) - tv
            return 0.5 * jnp.sum(jnp.mean(err * err, axis=-1, keepdims=True), axis=0, keepdims=True)

        val, vjp = jax.vjp(g, xv, wv)
        dx, dw = vjp(jnp.ones((1, 1), F32))
        dx_ref[...] = dx
        lv = jnp.broadcast_to(val, (1, LANE))

        @pl.when(i == 0)
        def _():
            loss_ref[...] = lv
            dw_ref[...] = dw

        @pl.when(i != 0)
        def _():
            loss_ref[...] += lv
            dw_ref[...] += dw

    return pl.pallas_call(
        body, name=name,
        out_shape=[jax.ShapeDtypeStruct((1, LANE), F32), jax.ShapeDtypeStruct((t, d), F32),
                   jax.ShapeDtypeStruct((1, d), F32)],
        grid=(t // tile,),
        in_specs=[pl.BlockSpec((tile, d), lambda i: (i, 0)), pl.BlockSpec((tile, d), lambda i: (i, 0)),
                  pl.BlockSpec((1, d), lambda i: (0, 0))],
        out_specs=[pl.BlockSpec((1, LANE), lambda i: (0, 0)), pl.BlockSpec((tile, d), lambda i: (i, 0)),
                   pl.BlockSpec((1, d), lambda i: (0, 0))],
        compiler_params=_cparams(("arbitrary",)),
    )(x, tgt, w)


CONV_W = 512
HALO = 8
STRIPS = 4


def _rows_back(before, cur, d):
    rows = lax.broadcasted_iota(jnp.int32, cur.shape, 0)
    return jnp.where(rows < d, pltpu.roll(before, d, 0), pltpu.roll(cur, d, 0))


def _rows_ahead(cur, after, d):
    rows = lax.broadcasted_iota(jnp.int32, cur.shape, 0)
    return jnp.where(rows < HALO - d, pltpu.roll(cur, HALO - d, 0), pltpu.roll(after, HALO - d, 0))


def _conv_taps(taps, bias, before, cur):
    shifted = [_rows_back(before, cur, CONV_K - 1 - k) for k in range(CONV_K - 1)] + [cur]
    pre = bias + taps[CONV_K - 1] * cur
    for k in range(CONV_K - 1):
        pre = pre + taps[k] * shifted[k]
    return pre, shifted


def conv_fwd(name, src, c0, width, w, b, tile=512):
    t = src.shape[0]
    ncol, nrow = width // CONV_W, t // tile
    cb0 = c0 // CONV_W
    hb = tile // HALO

    def body(prev_ref, cur_ref, w_ref, b_ref, o_ref):
        i = pl.program_id(1)
        taps = [w_ref[k:k + 1, :] for k in range(CONV_K)]
        bias = b_ref[...]

        def strips(g, before):
            for u in range(STRIPS):
                r0 = pl.multiple_of((g * STRIPS + u) * HALO, HALO)
                cur = cur_ref[pl.ds(r0, HALO), :]
                pre, _ = _conv_taps(taps, bias, before, cur)
                o_ref[pl.ds(r0, HALO), :] = _silu(pre)
                before = cur
            return before

        lax.fori_loop(0, tile // (HALO * STRIPS), strips, jnp.where(i == 0, 0.0, prev_ref[...]))

    return pl.pallas_call(
        body, name=name,
        out_shape=jax.ShapeDtypeStruct((t, width), F32),
        grid=(ncol, nrow),
        in_specs=[pl.BlockSpec((HALO, CONV_W), lambda j, i: (jnp.maximum(i * hb - 1, 0), cb0 + j)),
                  pl.BlockSpec((tile, CONV_W), lambda j, i: (i, cb0 + j)),
                  pl.BlockSpec((CONV_K, CONV_W), lambda j, i: (0, j)),
                  pl.BlockSpec((1, CONV_W), lambda j, i: (0, j))],
        out_specs=pl.BlockSpec((tile, CONV_W), lambda j, i: (i, j)),
        compiler_params=_cparams(("arbitrary", "arbitrary")),
    )(src, src, w, b)


def conv_bwd(name, src, c0, width, w, b, dy, tile=512):
    t = src.shape[0]
    ncol, nrow = width // CONV_W, t // tile
    cb0 = c0 // CONV_W
    hb = tile // HALO
    last_hb = t // HALO - 1
    nstrip = tile // HALO

    def body(sprev_ref, scur_ref, snext_ref, w_ref, b_ref, dycur_ref, dynext_ref,
             du_ref, dw_ref, db_ref, dpre_ref):
        i = pl.program_id(1)
        taps = [w_ref[k:k + 1, :] for k in range(CONV_K)]
        bias = b_ref[...]

        def dpre_of(before, cur, dy_strip):
            pre, shifted = _conv_taps(taps, bias, before, cur)
            s = _sigmoid(pre)
            return dy_strip * (s * (1.0 + pre * (1.0 - s))), shifted

        def strips1(g, carry):
            before, dws, dbs = carry
            for u in range(STRIPS):
                r0 = pl.multiple_of((g * STRIPS + u) * HALO, HALO)
                cur = scur_ref[pl.ds(r0, HALO), :]
                dpre, shifted = dpre_of(before, cur, dycur_ref[pl.ds(r0, HALO), :])
                dpre_ref[pl.ds(r0, HALO), :] = dpre
                before, dws, dbs = cur, tuple(a + dpre * v for a, v in zip(dws, shifted)), dbs + dpre
            return before, dws, dbs

        zero = jnp.zeros((HALO, CONV_W), F32)
        before, dws, dbs = lax.fori_loop(0, nstrip // STRIPS, strips1,
                                         (jnp.where(i == 0, 0.0, sprev_ref[...]), (zero,) * CONV_K, zero))
        dpre_next, _ = dpre_of(before, snext_ref[...], dynext_ref[...])
        dpre_ref[pl.ds(tile, HALO), :] = jnp.where(i == nrow - 1, 0.0, dpre_next)

        def strips2(g, _):
            parts = []
            for u in range(STRIPS):
                r0 = pl.multiple_of((g * STRIPS + u) * HALO, HALO)
                cur = dpre_ref[pl.ds(r0, HALO), :]
                after = dpre_ref[pl.ds(r0 + HALO, HALO), :]
                acc = taps[CONV_K - 1] * cur
                for d in range(1, CONV_K):
                    acc = acc + taps[CONV_K - 1 - d] * _rows_ahead(cur, after, d)
                parts.append(acc)
            r0 = pl.multiple_of(g * STRIPS * HALO, STRIPS * HALO)
            du_ref[pl.ds(r0, STRIPS * HALO), :] = jnp.concatenate(parts, axis=0).astype(du_ref.dtype)
            return 0

        lax.fori_loop(0, nstrip // STRIPS, strips2, 0)
        dw_tile = jnp.concatenate([jnp.sum(a, axis=0, keepdims=True) for a in dws], axis=0)
        db_tile = jnp.sum(dbs, axis=0, keepdims=True)
        _acc(dw_ref, dw_tile, i == 0)
        _acc(db_ref, db_tile, i == 0)

    return pl.pallas_call(
        body, name=name,
        out_shape=[jax.ShapeDtypeStruct((t, width), BF16), jax.ShapeDtypeStruct((CONV_K, width), F32),
                   jax.ShapeDtypeStruct((1, width), F32)],
        grid=(ncol, nrow),
        in_specs=[pl.BlockSpec((HALO, CONV_W), lambda j, i: (jnp.maximum(i * hb - 1, 0), cb0 + j)),
                  pl.BlockSpec((tile, CONV_W), lambda j, i: (i, cb0 + j)),
                  pl.BlockSpec((HALO, CONV_W), lambda j, i: (jnp.minimum((i + 1) * hb, last_hb), cb0 + j)),
                  pl.BlockSpec((CONV_K, CONV_W), lambda j, i: (0, j)),
                  pl.BlockSpec((1, CONV_W), lambda j, i: (0, j)),
                  pl.BlockSpec((tile, CONV_W), lambda j, i: (i, j)),
                  pl.BlockSpec((HALO, CONV_W), lambda j, i: (jnp.minimum((i + 1) * hb, last_hb), j))],
        out_specs=[pl.BlockSpec((tile, CONV_W), lambda j, i: (i, j)),
                   pl.BlockSpec((CONV_K, CONV_W), lambda j, i: (0, j)),
                   pl.BlockSpec((1, CONV_W), lambda j, i: (0, j))],
        scratch_shapes=[pltpu.VMEM((tile + HALO, CONV_W), F32)],
        compiler_params=_cparams(("arbitrary", "arbitrary")),
    )(src, src, src, w, b, dy, dy)


def _iota2(q):
    return (lax.broadcasted_iota(jnp.int32, (q, q), 0), lax.broadcasted_iota(jnp.int32, (q, q), 1))


def _lane_pick(blk, idx):
    lane = lax.broadcasted_iota(jnp.int32, (1, LANE), 1)
    return jnp.sum(jnp.where(lane == idx, blk, 0.0), axis=1, keepdims=True)


class _Decay:
    def __init__(self, a):
        q = a.shape[0]
        r, c = _iota2(q)
        self.r, self.c = r, c
        self.cum = _dot_hi((c <= r).astype(F32), a)
        self.cum_t = _dot_hi(a, (r <= c).astype(F32), TN)
        self.tot = self.cum[q - 1:q, :]
        self.e_cum = jnp.exp(self.cum)
        self.e_rest = jnp.exp(self.tot - self.cum)
        self.e_tot = jnp.exp(self.tot)

    def mask(self, lane):
        rows = lax.broadcasted_iota(jnp.int32, (LANE, 1), 0)
        cum_row = jnp.sum(jnp.where(rows == lane, self.cum_t, 0.0), axis=0, keepdims=True)
        return jnp.exp(jnp.where(self.r >= self.c, _lane_pick(self.cum, lane) - cum_row, NEG_BIG))


_SSD_B = SSM_D_INNER
_SSD_C = SSM_D_INNER + 2 * SSM_STATE


def _interleave(gens):
    results = [None] * len(gens)
    live = list(range(len(gens)))
    while live:
        for i in list(live):
            try:
                next(gens[i])
            except StopIteration as stop:
                results[i] = stop.value
                live.remove(i)
    return results


def ssd_chunk(xs, bm, cm, dt_all, dsk, dec, state, p, cb):
    lane = lax.broadcasted_iota(jnp.int32, (1, LANE), 1)
    m0 = lane < SSM_HEAD_DIM
    h0, h1 = 2 * p, 2 * p + 1

    def both(blk):
        return jnp.where(m0, _lane_pick(blk, h0), _lane_pick(blk, h1))

    xdt = xs * both(dt_all)
    l0, l1 = dec.mask(h0), dec.mask(h1)
    yield
    y_diag = _dot(cb * l0, jnp.where(m0, xdt, 0.0)) + _dot(cb * l1, jnp.where(m0, 0.0, xdt))
    y_off = _dot(cm, state, NT) * both(dec.e_cum)
    yield
    rowm = lax.broadcasted_iota(jnp.int32, (LANE, 1), 0) < SSM_HEAD_DIM
    new_state = (state * jnp.where(rowm, _lane_pick(dec.e_tot, h0), _lane_pick(dec.e_tot, h1))
                 + _dot(xdt * both(dec.e_rest), bm, TN))
    y = y_diag + y_off + both(dsk) * xs
    return y, new_state


def ssd_pairs(xs, bms, cms, small, dtb, alog, dsk, states):
    dt_all = _softplus(small + dtb)
    dec = _Decay(dt_all * (-jnp.exp(alog)))
    cbs = [_dot(cm, bm, NT) for cm, bm in zip(cms, bms)]
    res = _interleave([ssd_chunk(x, bms[p // 4], cms[p // 4], dt_all, dsk, dec, st, p, cbs[p // 4])
                       for p, (x, st) in enumerate(zip(xs, states))])
    return tuple(y for y, _ in res), tuple(s for _, s in res)


def tri_inverse(a):
    q = a.shape[0]
    r, c = _iota2(q)
    eye = (r == c).astype(F32)
    diag = (r // 16) == (c // 16)
    bd = jnp.where(diag, a, 0.0)
    off = jnp.where(diag, 0.0, a)
    b2 = _dot3(bd, bd)
    d1 = _dot3(eye - bd, eye + b2)
    yield
    b4 = _dot3(b2, b2)
    yield
    b8 = _dot3(b4, b4)
    d2 = _dot3(d1, eye + b4)
    yield
    dinv = _dot3(d2, eye + b8)
    yield
    n = _dot3(dinv, off)
    yield
    powers = [n]
    while 16 * 2 ** len(powers) < q:
        powers.append(_dot3(powers[-1], powers[-1]))
        yield
    m = dinv
    for pw in reversed(powers[1:]):
        m = _dot3(eye + pw, m)
        yield
    return _dot3(eye - n, m)


@jax.custom_vjp
def _solve_with(xinv, a, rhs):
    del a
    return _dot3(xinv, rhs)


def _solve_with_fwd(xinv, a, rhs):
    t = _dot3(xinv, rhs)
    return t, (xinv, t)


def _solve_with_bwd(res, dt):
    xinv, t = res
    d_rhs = _dot3(xinv, dt, TN)
    d_a = -_dot(d_rhs, t, NT)
    return jnp.zeros_like(xinv), d_a, d_rhs


_solve_with.defvjp(_solve_with_fwd, _solve_with_bwd)


_GDN_K = GDN_HEADS * GDN_HEAD
_GDN_V = 2 * GDN_HEADS * GDN_HEAD


def gdn_chunk(qh, kh, vh, beta_all, dec, state, h, xinv=None):
    r, c = dec.r, dec.c
    qn = qh * lax.rsqrt(jnp.sum(qh * qh, axis=-1, keepdims=True) + EPS) * (GDN_HEAD ** -0.5)
    kn = kh * lax.rsqrt(jnp.sum(kh * kh, axis=-1, keepdims=True) + EPS)
    beta = _lane_pick(beta_all, SM_B + h)
    decay = dec.mask(SM_A + h)
    yield
    kk = _dot(kn, kn, NT)
    qk = _dot(qn, kn, NT) * decay
    amat = jnp.where(r > c, kk * decay * beta, 0.0)
    eg = _lane_pick(dec.e_cum, SM_A + h)
    rhs = jnp.concatenate([vh * beta, kn * (beta * eg)], axis=1)
    qs = _dot(qn * eg, state)
    yield
    if xinv is None:
        xinv = yield from tri_inverse(amat)
        t = _dot3(xinv, rhs)
    else:
        t = _solve_with(xinv, amat, rhs)
    yield
    u, w = t[:, :GDN_HEAD], t[:, GDN_HEAD:]
    v_new = u - _dot(w, state)
    yield
    o = qs + _dot(qk, v_new)
    new_state = (state * _lane_pick(dec.e_tot, SM_A + h)
                 + _dot(kn * _lane_pick(dec.e_rest, SM_A + h), v_new, TN))
    return o, new_state, xinv


def gdn_heads(qs, ks, vs, small, alog, dtb, states, xinvs=None):
    nh = len(qs)
    beta_all = _sigmoid(small)
    dec = _Decay(-jnp.exp(alog) * _softplus(small + dtb))
    res = _interleave([gdn_chunk(qs[h], ks[h], vs[h], beta_all, dec, states[h], h,
                                 None if xinvs is None else xinvs[h]) for h in range(nh)])
    return tuple(o for o, _, _ in res), tuple(s for _, s, _ in res), tuple(x for _, _, x in res)


def _acc(ref, val, first):
    @pl.when(first)
    def _():
        ref[...] = val

    @pl.when(jnp.logical_not(first))
    def _():
        ref[...] += val


def ssd_scan_fwd(name, xbc, proj, dtb, alog, dsk):
    t = xbc.shape[0]
    nc, npair = t // SSD_CHUNK, SSM_HEADS // 2
    small_blk = AL_SMALL // LANE

    def body(xbc_ref, sm_ref, dtb_ref, alog_ref, dsk_ref, y_ref, sin_ref, st_ref):
        ci = pl.program_id(0)

        @pl.when(ci == 0)
        def _():
            st_ref[...] = jnp.zeros_like(st_ref)

        s_in = tuple(st_ref[p] for p in range(npair))
        ys, s_new = ssd_pairs(tuple(xbc_ref[:, p * LANE:(p + 1) * LANE] for p in range(npair)),
                              tuple(xbc_ref[:, _SSD_B + g * LANE:_SSD_B + (g + 1) * LANE] for g in range(2)),
                              tuple(xbc_ref[:, _SSD_C + g * LANE:_SSD_C + (g + 1) * LANE] for g in range(2)),
                              sm_ref[...], dtb_ref[...], alog_ref[...], dsk_ref[...], s_in)
        for p in range(npair):
            sin_ref[0, p] = s_in[p]
            y_ref[:, p * LANE:(p + 1) * LANE] = ys[p]
            st_ref[p] = s_new[p]

    par = pl.BlockSpec((1, LANE), lambda ci: (0, 0))
    return pl.pallas_call(
        body, name=name,
        out_shape=[jax.ShapeDtypeStruct((t, SSM_D_INNER), F32),
                   jax.ShapeDtypeStruct((nc, npair, LANE, LANE), F32)],
        grid=(nc,),
        in_specs=[pl.BlockSpec((SSD_CHUNK, SSM_CONV_DIM), lambda ci: (ci, 0)),
                  pl.BlockSpec((SSD_CHUNK, LANE), lambda ci: (ci, small_blk)),
                  par, par, par],
        out_specs=[pl.BlockSpec((SSD_CHUNK, SSM_D_INNER), lambda ci: (ci, 0)),
                   pl.BlockSpec((1, npair, LANE, LANE), lambda ci: (ci, 0, 0, 0))],
        scratch_shapes=[pltpu.VMEM((npair, LANE, LANE), F32)],
        compiler_params=_cparams(("arbitrary",)),
    )(xbc, proj, dtb, alog, dsk)


def ssd_scan_bwd(name, xbc, proj, dtb, alog, dsk, s_in, dy):
    t = xbc.shape[0]
    nc, npair = t // SSD_CHUNK, SSM_HEADS // 2
    small_blk = AL_SMALL // LANE

    def body(xbc_ref, sm_ref, dtb_ref, alog_ref, dsk_ref, sin_ref, dy_ref,
             dxbc_ref, dsm_ref, ddtb_ref, dalog_ref, ddsk_ref, dst_ref):
        ci = pl.program_id(0)

        @pl.when(ci == 0)
        def _():
            dst_ref[...] = jnp.zeros_like(dst_ref)

        _, vjp = jax.vjp(ssd_pairs, tuple(xbc_ref[:, p * LANE:(p + 1) * LANE] for p in range(npair)),
                         tuple(xbc_ref[:, _SSD_B + g * LANE:_SSD_B + (g + 1) * LANE] for g in range(2)),
                         tuple(xbc_ref[:, _SSD_C + g * LANE:_SSD_C + (g + 1) * LANE] for g in range(2)),
                         sm_ref[...], dtb_ref[...], alog_ref[...], dsk_ref[...],
                         tuple(sin_ref[0, p] for p in range(npair)))
        dxs, dbms, dcms, dsm, ddtb, dalog, ddsk, dsts = vjp(
            (tuple(dy_ref[:, p * LANE:(p + 1) * LANE] for p in range(npair)),
             tuple(dst_ref[p] for p in range(npair))))
        for p in range(npair):
            dxbc_ref[:, p * LANE:(p + 1) * LANE] = dxs[p]
            dst_ref[p] = dsts[p]
        for g in range(2):
            dxbc_ref[:, _SSD_B + g * LANE:_SSD_B + (g + 1) * LANE] = dbms[g]
            dxbc_ref[:, _SSD_C + g * LANE:_SSD_C + (g + 1) * LANE] = dcms[g]
        dsm_ref[...] = dsm
        _acc(ddtb_ref, ddtb, ci == 0)
        _acc(dalog_ref, dalog, ci == 0)
        _acc(ddsk_ref, ddsk, ci == 0)

    par = pl.BlockSpec((1, LANE), lambda ci: (0, 0))
    rev = lambda ci: nc - 1 - ci
    return pl.pallas_call(
        body, name=name,
        out_shape=[jax.ShapeDtypeStruct((t, SSM_CONV_DIM), F32),
                   jax.ShapeDtypeStruct((t, LANE), F32),
                   jax.ShapeDtypeStruct((1, LANE), F32), jax.ShapeDtypeStruct((1, LANE), F32),
                   jax.ShapeDtypeStruct((1, LANE), F32)],
        grid=(nc,),
        in_specs=[pl.BlockSpec((SSD_CHUNK, SSM_CONV_DIM), lambda ci: (rev(ci), 0)),
                  pl.BlockSpec((SSD_CHUNK, LANE), lambda ci: (rev(ci), small_blk)),
                  par, par, par,
                  pl.BlockSpec((1, npair, LANE, LANE), lambda ci: (rev(ci), 0, 0, 0)),
                  pl.BlockSpec((SSD_CHUNK, SSM_D_INNER), lambda ci: (rev(ci), 0))],
        out_specs=[pl.BlockSpec((SSD_CHUNK, SSM_CONV_DIM), lambda ci: (rev(ci), 0)),
                   pl.BlockSpec((SSD_CHUNK, LANE), lambda ci: (rev(ci), 0)),
                   par, par, par],
        scratch_shapes=[pltpu.VMEM((npair, LANE, LANE), F32)],
        compiler_params=_cparams(("arbitrary",)),
    )(xbc, proj, dtb, alog, dsk, s_in, dy)


def gdn_scan_fwd(name, qkv, proj, alog, dtb):
    t = qkv.shape[0]
    nc, nh = t // CHUNK, GDN_HEADS
    small_blk = AL_SMALL // LANE

    def body(qkv_ref, sm_ref, alog_ref, dtb_ref, o_ref, sin_ref, x_ref, st_ref):
        ci = pl.program_id(0)

        @pl.when(ci == 0)
        def _():
            st_ref[...] = jnp.zeros_like(st_ref)

        s_in = tuple(st_ref[h] for h in range(nh))
        os, s_new, xinvs = gdn_heads(
            tuple(qkv_ref[:, h * LANE:(h + 1) * LANE] for h in range(nh)),
            tuple(qkv_ref[:, _GDN_K + h * LANE:_GDN_K + (h + 1) * LANE] for h in range(nh)),
            tuple(qkv_ref[:, _GDN_V + h * LANE:_GDN_V + (h + 1) * LANE] for h in range(nh)),
            sm_ref[...], alog_ref[...], dtb_ref[...], s_in)
        for h in range(nh):
            sin_ref[0, h] = s_in[h]
            o_ref[:, h * LANE:(h + 1) * LANE] = os[h]
            x_ref[0, h] = xinvs[h]
            st_ref[h] = s_new[h]

    par = pl.BlockSpec((1, LANE), lambda ci: (0, 0))
    return pl.pallas_call(
        body, name=name,
        out_shape=[jax.ShapeDtypeStruct((t, GDN_HEADS * GDN_HEAD), F32),
                   jax.ShapeDtypeStruct((nc, nh, LANE, LANE), F32),
                   jax.ShapeDtypeStruct((nc, nh, CHUNK, CHUNK), F32)],
        grid=(nc,),
        in_specs=[pl.BlockSpec((CHUNK, GDN_QKV_DIM), lambda ci: (ci, 0)),
                  pl.BlockSpec((CHUNK, LANE), lambda ci: (ci, small_blk)),
                  par, par],
        out_specs=[pl.BlockSpec((CHUNK, GDN_HEADS * GDN_HEAD), lambda ci: (ci, 0)),
                   pl.BlockSpec((1, nh, LANE, LANE), lambda ci: (ci, 0, 0, 0)),
                   pl.BlockSpec((1, nh, CHUNK, CHUNK), lambda ci: (ci, 0, 0, 0))],
        scratch_shapes=[pltpu.VMEM((nh, LANE, LANE), F32)],
        compiler_params=_cparams(("arbitrary",)),
    )(qkv, proj, alog, dtb)


def gdn_scan_bwd(name, qkv, proj, alog, dtb, s_in, xinv, do, dsm_in):
    t = qkv.shape[0]
    nc, nh = t // CHUNK, GDN_HEADS
    small_blk = AL_SMALL // LANE

    def body(qkv_ref, sm_ref, alog_ref, dtb_ref, sin_ref, x_ref, do_ref, dsmi_ref,
             dqkv_ref, dsm_ref, dalog_ref, ddtb_ref, dst_ref):
        ci = pl.program_id(0)

        @pl.when(ci == 0)
        def _():
            dst_ref[...] = jnp.zeros_like(dst_ref)

        xis = tuple(x_ref[0, h] for h in range(nh))

        def fn(qs, ks, vs, sm, alog_, dtb_, sts):
            os, s_new, _ = gdn_heads(qs, ks, vs, sm, alog_, dtb_, sts, xinvs=xis)
            return os, s_new

        _, vjp = jax.vjp(fn, tuple(qkv_ref[:, h * LANE:(h + 1) * LANE] for h in range(nh)),
                         tuple(qkv_ref[:, _GDN_K + h * LANE:_GDN_K + (h + 1) * LANE] for h in range(nh)),
                         tuple(qkv_ref[:, _GDN_V + h * LANE:_GDN_V + (h + 1) * LANE] for h in range(nh)),
                         sm_ref[...], alog_ref[...], dtb_ref[...], tuple(sin_ref[0, h] for h in range(nh)))
        dqs, dks, dvs, dsm, dalog, ddtb, dsts = vjp(
            (tuple(do_ref[:, h * LANE:(h + 1) * LANE] for h in range(nh)), tuple(dst_ref[h] for h in range(nh))))
        for h in range(nh):
            dqkv_ref[:, h * LANE:(h + 1) * LANE] = dqs[h]
            dqkv_ref[:, _GDN_K + h * LANE:_GDN_K + (h + 1) * LANE] = dks[h]
            dqkv_ref[:, _GDN_V + h * LANE:_GDN_V + (h + 1) * LANE] = dvs[h]
            dst_ref[h] = dsts[h]
        dsm_ref[...] = dsmi_ref[...] + dsm
        _acc(dalog_ref, dalog, ci == 0)
        _acc(ddtb_ref, ddtb, ci == 0)

    par = pl.BlockSpec((1, LANE), lambda ci: (0, 0))
    rev = lambda ci: nc - 1 - ci
    return pl.pallas_call(
        body, name=name,
        out_shape=[jax.ShapeDtypeStruct((t, GDN_QKV_DIM), F32), jax.ShapeDtypeStruct((t, LANE), F32),
                   jax.ShapeDtypeStruct((1, LANE), F32), jax.ShapeDtypeStruct((1, LANE), F32)],
        grid=(nc,),
        in_specs=[pl.BlockSpec((CHUNK, GDN_QKV_DIM), lambda ci: (rev(ci), 0)),
                  pl.BlockSpec((CHUNK, LANE), lambda ci: (rev(ci), small_blk)),
                  par, par,
                  pl.BlockSpec((1, nh, LANE, LANE), lambda ci: (rev(ci), 0, 0, 0)),
                  pl.BlockSpec((1, nh, CHUNK, CHUNK), lambda ci: (rev(ci), 0, 0, 0)),
                  pl.BlockSpec((CHUNK, GDN_HEADS * GDN_HEAD), lambda ci: (rev(ci), 0)),
                  pl.BlockSpec((CHUNK, LANE), lambda ci: (rev(ci), 0))],
        out_specs=[pl.BlockSpec((CHUNK, GDN_QKV_DIM), lambda ci: (rev(ci), 0)),
                   pl.BlockSpec((CHUNK, LANE), lambda ci: (rev(ci), 0)),
                   par, par],
        scratch_shapes=[pltpu.VMEM((nh, LANE, LANE), F32)],
        compiler_params=_cparams(("arbitrary",)),
    )(qkv, proj, alog, dtb, s_in, xinv, do, dsm_in)


def _row(arr, w, c0=0, moves=False):
    return (arr, "row", w, c0, moves)


def _par(arr, w, c0=0, moves=False):
    return (arr, "par", w, c0, moves)


def matmul_add(name, a, b, res):
    (m, k), (_, n) = a.shape, b.shape
    tm, tn, tk = _matmul_tiles(m, n, n, k, a.dtype.itemsize, b.dtype.itemsize, 4 + res.dtype.itemsize)
    nk = k // tk

    def body_acc(a_ref, b_ref, r_ref, o_ref, acc_ref):
        kk = pl.program_id(2)

        @pl.when(kk == 0)
        def _():
            acc_ref[...] = r_ref[...]

        acc_ref[...] += _dot(a_ref[...], b_ref[...])

        @pl.when(kk == nk - 1)
        def _():
            o_ref[...] = acc_ref[...]

    def body_one(a_ref, b_ref, r_ref, o_ref):
        o_ref[...] = r_ref[...] + _dot(a_ref[...], b_ref[...])

    body = body_one if nk == 1 else body_acc
    return pl.pallas_call(
        body, name=name,
        out_shape=jax.ShapeDtypeStruct((m, n), F32),
        grid=(m // tm, n // tn, nk),
        in_specs=[pl.BlockSpec((tm, tk), lambda i, j, kk: (i, kk)),
                  pl.BlockSpec((tk, tn), lambda i, j, kk: (kk, j)),
                  pl.BlockSpec((tm, tn), lambda i, j, kk: (i, j))],
        out_specs=pl.BlockSpec((tm, tn), lambda i, j, kk: (i, j)),
        scratch_shapes=[] if nk == 1 else [pltpu.VMEM((tm, tn), F32)],
        compiler_params=_cparams(("parallel", "parallel", "arbitrary")),
    )(a, b, res)


def layer_fwd(l, x, w):
    t = x.shape[0]
    rt = min(256, t)
    s = {"x": x}
    s["h"] = rowwise_fwd(f"norm_mix_l{l}", f_rmsnorm, t, rt, 1,
                         [_row(x, D_MODEL), _par(w["norm_mix_w"], D_MODEL)], [(D_MODEL, BF16)])[0]
    s["proj"] = matmul(f"in_proj_l{l}", s["h"], w["w_in"], "nn")
    s["xbc"] = conv_fwd(f"ssm_conv_l{l}", s["proj"], AL_XBC, SSM_CONV_DIM, w["ssm_conv_w"], w["ssm_conv_b"],
                        tile=min(512, t))
    s["qkv"] = conv_fwd(f"gdn_conv_l{l}", s["proj"], AL_QKV, GDN_QKV_DIM, w["gdn_conv_w"], w["gdn_conv_b"],
                        tile=min(512, t))
    s["y_scan"], s["ssd_sin"] = ssd_scan_fwd(f"ssd_scan_l{l}", s["xbc"], s["proj"], w["ssm_dt_bias"],
                                             w["ssm_a_log"], w["ssm_d"])
    s["o_scan"], s["gdn_sin"], s["gdn_x"] = gdn_scan_fwd(f"gdn_scan_l{l}", s["qkv"], s["proj"],
                                                         w["gdn_a_log"], w["gdn_dt_bias"])
    s["y_ssm"] = rowwise_fwd(f"ssd_post_l{l}", f_ssd_post, t, rt, 2,
                             [_row(s["y_scan"], 512, 0, True), _row(s["proj"], 512, AL_Z // 512, True),
                              _par(w["ssm_norm_w"], 512, 0, True)], [(512, BF16)])[0]
    s["y_gdn"] = rowwise_fwd(f"gdn_post_l{l}", f_gdn_post, t, rt, GDN_HEADS,
                             [_row(s["o_scan"], LANE, 0, True), _row(s["proj"], LANE, AL_GZ // LANE, True),
                              _par(w["gdn_norm_w"], LANE)], [(LANE, BF16)])[0]
    s["p1"] = matmul(f"proj_ssm_l{l}", s["y_ssm"], w["w_proj_ssm"], "nn")
    s["p2"] = matmul(f"proj_gdn_l{l}", s["y_gdn"], w["w_proj_gdn"], "nn")
    s["merged"] = rowwise_fwd(f"merge_l{l}", f_merge, t, rt, 2,
                              [_row(s["proj"], 512, AL_GS // 512, True), _row(s["p1"], 512, 0, True),
                               _row(s["proj"], 512, AL_GG // 512, True), _row(s["p2"], 512, 0, True)],
                              [(512, BF16)])[0]
    s["x1"] = matmul_add(f"out_proj_l{l}", s["merged"], w["w_out"], x)
    s["h2"] = rowwise_fwd(f"norm_ffn_l{l}", f_rmsnorm, t, rt, 1,
                          [_row(s["x1"], D_MODEL), _par(w["norm_ffn_w"], D_MODEL)], [(D_MODEL, BF16)])[0]
    s["gu"] = matmul(f"ffn_in_l{l}", s["h2"], w["w_ffn_in"], "nn")
    s["act"] = rowwise_fwd(f"swiglu_l{l}", f_swiglu, t, rt, FFN_HIDDEN // 256,
                           [_row(s["gu"], 256, 0, True), _row(s["gu"], 256, FFN_HIDDEN // 256, True)],
                           [(256, BF16)])[0]
    x2 = matmul_add(f"ffn_down_l{l}", s["act"], w["w_ffn_down"], s["x1"])
    return x2, s


IN_SHARD = IN_DIM // 4
IN_SHARD_PAD = 2304


def _aligned_to_shards(g):
    orig = jnp.concatenate([g[:, 0:2560], g[:, AL_SMALL:AL_SMALL + 16], g[:, 2560:6656],
                            g[:, AL_SMALL + 16:AL_SMALL + 32], g[:, 6656:8704]], axis=1)
    return jnp.stack([jnp.pad(orig[:, j * IN_SHARD:(j + 1) * IN_SHARD], ((0, 0), (0, IN_SHARD_PAD - IN_SHARD)))
                      for j in range(N_CHIPS)])


def layer_bwd(l, dx2, w, s):
    t = dx2.shape[0]
    rt = min(256, t)
    ct = min(512, t)
    g = {}
    dact = matmul(f"ffn_down_dx_l{l}", dx2, w["w_ffn_down"], "nt")
    g["w_ffn_down"] = matmul(f"ffn_down_dw_l{l}", s["act"], dx2, "tn")
    nf = FFN_HIDDEN // 256
    dgate, dup = rowwise_bwd(f"swiglu_bwd_l{l}", f_swiglu, t, rt, nf,
                             [_row(s["gu"], 256, 0, True), _row(s["gu"], 256, nf, True)], [True, True],
                             [_row(dact, 256, 0, True)], row_dtypes={0: BF16, 1: BF16})
    dgu = jnp.concatenate([dgate, dup], axis=1)
    dh2 = matmul(f"ffn_in_dx_l{l}", dgu, w["w_ffn_in"], "nt")
    g["w_ffn_in"] = matmul(f"ffn_in_dw_l{l}", s["h2"], dgu, "tn", chip_major=True)
    dx1, g["norm_ffn_w"] = rowwise_bwd(f"norm_ffn_bwd_l{l}", f_rmsnorm, t, rt, 1,
                                       [_row(s["x1"], D_MODEL), _par(w["norm_ffn_w"], D_MODEL)], [True, True],
                                       [_row(dh2, D_MODEL)], addends={0: _row(dx2, D_MODEL)})
    dmerged = matmul(f"out_proj_dx_l{l}", dx1, w["w_out"], "nt")
    g["w_out"] = matmul(f"out_proj_dw_l{l}", s["merged"], dx1, "tn")
    dgs, dp1, dgg, dp2 = rowwise_bwd(
        f"merge_bwd_l{l}", f_merge, t, rt, 2,
        [_row(s["proj"], 512, AL_GS // 512, True), _row(s["p1"], 512, 0, True),
         _row(s["proj"], 512, AL_GG // 512, True), _row(s["p2"], 512, 0, True)], [True] * 4,
        [_row(dmerged, 512, 0, True)], row_dtypes={0: BF16, 1: BF16, 2: BF16, 3: BF16})
    dy_ssm = matmul(f"proj_ssm_dx_l{l}", dp1, w["w_proj_ssm"], "nt")
    g["w_proj_ssm"] = matmul(f"proj_ssm_dw_l{l}", s["y_ssm"], dp1, "tn")
    dy_gdn = matmul(f"proj_gdn_dx_l{l}", dp2, w["w_proj_gdn"], "nt")
    g["w_proj_gdn"] = matmul(f"proj_gdn_dw_l{l}", s["y_gdn"], dp2, "tn")
    dy_scan, dz, g["ssm_norm_w"] = rowwise_bwd(
        f"ssd_post_bwd_l{l}", f_ssd_post, t, rt, 2,
        [_row(s["y_scan"], 512, 0, True), _row(s["proj"], 512, AL_Z // 512, True),
         _par(w["ssm_norm_w"], 512, 0, True)], [True] * 3, [_row(dy_ssm, 512, 0, True)], row_dtypes={1: BF16})
    dxbc_act, dsm, g["ssm_dt_bias"], g["ssm_a_log"], g["ssm_d"] = ssd_scan_bwd(
        f"ssd_scan_bwd_l{l}", s["xbc"], s["proj"], w["ssm_dt_bias"], w["ssm_a_log"], w["ssm_d"],
        s["ssd_sin"], dy_scan)
    dxbc, g["ssm_conv_w"], g["ssm_conv_b"] = conv_bwd(
        f"ssm_conv_bwd_l{l}", s["proj"], AL_XBC, SSM_CONV_DIM, w["ssm_conv_w"], w["ssm_conv_b"], dxbc_act, tile=ct)
    do_scan, dgz, g["gdn_norm_w"] = rowwise_bwd(
        f"gdn_post_bwd_l{l}", f_gdn_post, t, rt, GDN_HEADS,
        [_row(s["o_scan"], LANE, 0, True), _row(s["proj"], LANE, AL_GZ // LANE, True),
         _par(w["gdn_norm_w"], LANE)], [True] * 3, [_row(dy_gdn, LANE, 0, True)], row_dtypes={1: BF16})
    dqkv_act, dsm, g["gdn_a_log"], g["gdn_dt_bias"] = gdn_scan_bwd(
        f"gdn_scan_bwd_l{l}", s["qkv"], s["proj"], w["gdn_a_log"], w["gdn_dt_bias"], s["gdn_sin"],
        s["gdn_x"], do_scan, dsm)
    dqkv, g["gdn_conv_w"], _ = conv_bwd(
        f"gdn_conv_bwd_l{l}", s["proj"], AL_QKV, GDN_QKV_DIM, w["gdn_conv_w"], w["gdn_conv_b"], dqkv_act, tile=ct)
    dproj = jnp.concatenate([dz, dxbc, dqkv, dgz, dgs, dgg, dsm.astype(BF16),
                             jnp.zeros((t, AL_DIM - AL_SMALL - LANE), BF16)], axis=1)
    dh = matmul(f"in_proj_dx_l{l}", dproj, w["w_in"], "nt")
    g["w_in"] = matmul(f"in_proj_dw_l{l}", s["h"], dproj, "tn")
    dx0, g["norm_mix_w"] = rowwise_bwd(f"norm_mix_bwd_l{l}", f_rmsnorm, t, rt, 1,
                                       [_row(s["x"], D_MODEL), _par(w["norm_mix_w"], D_MODEL)], [True, True],
                                       [_row(dh, D_MODEL)], addends={0: _row(dx1, D_MODEL)})
    return dx0, g


def _align_w_in(w):
    pad = jnp.zeros((w.shape[0], AL_DIM - AL_SMALL - 32), w.dtype)
    return jnp.concatenate([w[:, 0:2560], w[:, 2576:6672], w[:, 6688:8736],
                            w[:, 2560:2576], w[:, 6672:6688], pad], axis=1)


def _pad_lane(v, at=0):
    return jnp.pad(v[None], ((0, 0), (at, LANE - at - v.shape[0])))


def local_step(x, target, full, gathered_of=None, on_layer_grads=None):
    if gathered_of is None:
        gathered_of = lambda l, after: {n: full[n][l] for n, _ in SHARDED}
    ws, saved = [], []
    h = x
    for l in range(DEPTH):
        gw = gathered_of(l, h)
        ws.append({
            "norm_mix_w": full["norm_mix_w"][l][None], "w_in": _align_w_in(gw["w_in"]),
            "ssm_conv_w": gw["ssm_conv_w"], "ssm_conv_b": full["ssm_conv_b"][l][None],
            "ssm_dt_bias": _pad_lane(full["ssm_dt_bias"][l]), "ssm_a_log": _pad_lane(full["ssm_a_log"][l]),
            "ssm_d": _pad_lane(full["ssm_d"][l]), "ssm_norm_w": full["ssm_norm_w"][l][None],
            "gdn_conv_w": gw["gdn_conv_w"], "gdn_conv_b": jnp.zeros((1, GDN_QKV_DIM), F32),
            "gdn_a_log": _pad_lane(full["gdn_a_log"][l], SM_A),
            "gdn_dt_bias": _pad_lane(full["gdn_dt_bias"][l], SM_A),
            "gdn_norm_w": full["gdn_norm_w"][l][None],
            "w_proj_ssm": gw["w_proj_ssm"], "w_proj_gdn": gw["w_proj_gdn"], "w_out": gw["w_out"],
            "norm_ffn_w": full["norm_ffn_w"][l][None], "w_ffn_in": gw["w_ffn_in"],
            "w_ffn_down": gw["w_ffn_down"],
        })
        h, s = layer_fwd(l, h, ws[l])
        saved.append(s)
    loss, dx, g_final = final_loss("final_loss", h, target, full["final_norm_w"][None], tile=min(256, x.shape[0]))
    per_layer = [None] * DEPTH
    matmul_grads = [None] * DEPTH
    for l in reversed(range(DEPTH)):
        dx, per_layer[l] = layer_bwd(l, dx, ws[l], saved[l])
        matmul_grads[l] = {n: per_layer[l].pop(n) for n, _ in BIG}
        if on_layer_grads is not None:
            dx = on_layer_grads(l, matmul_grads[l], dx)
    grads = {"final_norm_w": g_final[0]}
    if on_layer_grads is None:
        grads.update({n: jnp.stack([matmul_grads[l][n] for l in range(DEPTH)]) for n, _ in BIG})
    for name in per_layer[0]:
        rows = []
        for l in range(DEPTH):
            gl = per_layer[l][name]
            if name in ("ssm_dt_bias", "ssm_a_log", "ssm_d"):
                gl = gl[0, :SSM_HEADS]
            elif name in ("gdn_a_log", "gdn_dt_bias"):
                gl = gl[0, SM_A:SM_A + GDN_HEADS]
            elif name in ("norm_mix_w", "ssm_conv_b", "ssm_norm_w", "gdn_norm_w", "norm_ffn_w"):
                gl = gl[0]
            rows.append(gl)
        grads[name] = jnp.stack(rows)
    return loss, dx, grads


MESH = pl.DeviceIdType.MESH
HBM = pl.BlockSpec(memory_space=pltpu.HBM)
N_DEV = 8


def _pos():
    return lax.axis_index("x"), lax.axis_index("y"), lax.axis_index("c")


def _rcopy(src, dst, send_sem, recv_sem, dev):
    return pltpu.make_async_remote_copy(src_ref=src, dst_ref=dst, send_sem=send_sem, recv_sem=recv_sem,
                                        device_id=dev, device_id_type=MESH)


RELATIONS = (2, 1, 3)


def _related_chip(x, y, mask):
    return (1 - x if mask & 2 else x, 1 - y if mask & 1 else y)


def weights_gather(name, bufs):
    n = len(bufs)

    def body(*refs):
        outs, send_sems, recv_sems = refs[n:2 * n], refs[2 * n], refs[2 * n + 1]
        x, y, c = _pos()
        sib = (x, y, 1 - c)
        sends = []
        for i, a in enumerate(outs):
            for k, m in enumerate(RELATIONS):
                px, py = _related_chip(x, y, m)
                cp = _rcopy(a.at[0, c], a.at[m, c], send_sems.at[6 * i + k], recv_sems.at[6 * i + k], (px, py, c))
                cp.start()
                sends.append(cp)
        for i, a in enumerate(outs):
            for k, m in enumerate(RELATIONS):
                px, py = _related_chip(x, y, m)
                _rcopy(a.at[0, c], a.at[m, c], send_sems.at[6 * i + k], recv_sems.at[6 * i + k],
                       (px, py, c)).wait_recv()
                fw = _rcopy(a.at[m, c], a.at[m, c], send_sems.at[6 * i + 3 + k], recv_sems.at[6 * i + 3 + k], sib)
                fw.start()
                sends.append(fw)
        for i, a in enumerate(outs):
            for k, m in enumerate(RELATIONS):
                _rcopy(a.at[m, 1 - c], a.at[m, 1 - c], send_sems.at[6 * i + 3 + k], recv_sems.at[6 * i + 3 + k],
                       sib).wait_recv()
        for cp in sends:
            cp.wait_send()

    return pl.pallas_call(
        body, name=name, out_shape=[jax.ShapeDtypeStruct(b.shape, b.dtype) for b in bufs],
        in_specs=[HBM] * n, out_specs=[HBM] * n,
        input_output_aliases={i: i for i in range(n)},
        scratch_shapes=[pltpu.SemaphoreType.DMA((6 * n,)), pltpu.SemaphoreType.DMA((6 * n,))],
    )(*bufs)


SEM = pl.BlockSpec(memory_space=pltpu.SEMAPHORE)
DATAFLOW = pltpu.SideEffectType.DATAFLOW_SIDE_EFFECTING


def gather_start(name, bufs, after):
    n = len(bufs)

    def body(*refs):
        ins = refs[:n]
        send_sems, recv_sems, token = refs[n + 1], refs[n + 2], refs[2 * n + 3]
        x, y, c = _pos()
        for i, a in enumerate(ins):
            for k, m in enumerate(RELATIONS):
                px, py = _related_chip(x, y, m)
                _rcopy(a.at[0, c], a.at[m, c], send_sems.at[3 * i + k], recv_sems.at[3 * i + k], (px, py, c)).start()
        token[...] = jnp.zeros_like(token)

    res = pl.pallas_call(
        body, name=name,
        out_shape=(pltpu.SemaphoreType.DMA((3 * n,)), pltpu.SemaphoreType.DMA((3 * n,)),
                   *[pltpu.HBM(b.shape, b.dtype) for b in bufs], jax.ShapeDtypeStruct((8, LANE), F32)),
        in_specs=[HBM] * n + [pl.BlockSpec(memory_space=pl.ANY)],
        out_specs=(SEM, SEM, *[HBM] * n, pl.BlockSpec(memory_space=pltpu.VMEM)),
        input_output_aliases={i: 2 + i for i in range(n)},
        compiler_params=pltpu.CompilerParams(has_side_effects=DATAFLOW),
    )(*[pltpu.with_memory_space_constraint(b, pltpu.HBM) for b in bufs], after)
    return res[0], res[1], list(res[2:2 + n]), res[2 + n]


def gather_wait(name, send_sems, recv_sems, bufs, after):
    n = len(bufs)

    def body(*refs):
        ins, ssem, rsem = refs[:n], refs[n], refs[n + 1]
        x, y, c = _pos()
        for i, a in enumerate(ins):
            for k, m in enumerate(RELATIONS):
                px, py = _related_chip(x, y, m)
                cp = _rcopy(a.at[0, c], a.at[m, c], ssem.at[3 * i + k], rsem.at[3 * i + k], (px, py, c))
                cp.wait_send()
                cp.wait_recv()

    return pl.pallas_call(
        body, name=name,
        out_shape=[pltpu.HBM(b.shape, b.dtype) for b in bufs],
        in_specs=[HBM] * n + [SEM, SEM, pl.BlockSpec(memory_space=pl.ANY)],
        out_specs=[HBM] * n,
        input_output_aliases={i: i for i in range(n)},
        compiler_params=pltpu.CompilerParams(has_side_effects=DATAFLOW),
    )(*bufs, send_sems, recv_sems, after)


def weights_forward(name, bufs):
    n = len(bufs)

    def body(*refs):
        outs, send_sems, recv_sems = refs[n:2 * n], refs[2 * n], refs[2 * n + 1]
        x, y, c = _pos()
        sib = (x, y, 1 - c)
        sends = []
        for i, a in enumerate(outs):
            for k, m in enumerate(RELATIONS):
                fw = _rcopy(a.at[m, c], a.at[m, c], send_sems.at[3 * i + k], recv_sems.at[3 * i + k], sib)
                fw.start()
                sends.append(fw)
        for i, a in enumerate(outs):
            for k, m in enumerate(RELATIONS):
                _rcopy(a.at[m, 1 - c], a.at[m, 1 - c], send_sems.at[3 * i + k], recv_sems.at[3 * i + k],
                       sib).wait_recv()
        for cp in sends:
            cp.wait_send()

    return pl.pallas_call(
        body, name=name, out_shape=[jax.ShapeDtypeStruct(b.shape, b.dtype) for b in bufs],
        in_specs=[HBM] * n, out_specs=[HBM] * n,
        input_output_aliases={i: i for i in range(n)},
        scratch_shapes=[pltpu.SemaphoreType.DMA((3 * n,)), pltpu.SemaphoreType.DMA((3 * n,))],
    )(*bufs)


def pair_swap(name, gs):
    n = len(gs)
    offs = [0]
    for g in gs:
        offs.append(offs[-1] + g.shape[0])

    def body(*refs):
        srcs, outs, send_sems, recv_sems = refs[:n], refs[n:2 * n], refs[2 * n], refs[2 * n + 1]
        x, y, c = _pos()
        cps = [_rcopy(s.at[j, 1 - c], o.at[j], send_sems.at[offs[i] + j], recv_sems.at[offs[i] + j], (x, y, 1 - c))
               for i, (s, o) in enumerate(zip(srcs, outs)) for j in range(s.shape[0])]
        for cp in cps:
            cp.start()
        for cp in cps:
            cp.wait()

    return pl.pallas_call(
        body, name=name, out_shape=[jax.ShapeDtypeStruct(g.shape[:1] + g.shape[2:], g.dtype) for g in gs],
        in_specs=[HBM] * n, out_specs=[HBM] * n,
        scratch_shapes=[pltpu.SemaphoreType.DMA((offs[-1],)), pltpu.SemaphoreType.DMA((offs[-1],))],
    )(*gs)


def scatter_start(name, ss, after):
    n = len(ss)
    lands = [lax.empty((3,) + s.shape[1:], s.dtype) for s in ss]

    def body(*refs):
        srcs, dsts = refs[:n], refs[n:2 * n]
        send_sems, recv_sems, token = refs[2 * n + 1], refs[2 * n + 2], refs[4 * n + 3]
        x, y, c = _pos()
        for i, (s, o) in enumerate(zip(srcs, dsts)):
            for k, m in enumerate(RELATIONS):
                px, py = _related_chip(x, y, m)
                _rcopy(s.at[2 * px + py], o.at[k], send_sems.at[3 * i + k], recv_sems.at[3 * i + k],
                       (px, py, c)).start()
        token[...] = jnp.zeros_like(token)

    both = list(ss) + lands
    res = pl.pallas_call(
        body, name=name,
        out_shape=(pltpu.SemaphoreType.DMA((3 * n,)), pltpu.SemaphoreType.DMA((3 * n,)),
                   *[pltpu.HBM(b.shape, b.dtype) for b in both], jax.ShapeDtypeStruct((8, LANE), F32)),
        in_specs=[HBM] * (2 * n) + [pl.BlockSpec(memory_space=pl.ANY)],
        out_specs=(SEM, SEM, *[HBM] * (2 * n), pl.BlockSpec(memory_space=pltpu.VMEM)),
        input_output_aliases={i: 2 + i for i in range(2 * n)},
        compiler_params=pltpu.CompilerParams(has_side_effects=DATAFLOW),
    )(*[pltpu.with_memory_space_constraint(b, pltpu.HBM) for b in both], after)
    return res[0], res[1], list(res[2:2 + n]), list(res[2 + n:2 + 2 * n]), res[2 + 2 * n]


def scatter_wait(name, send_sems, recv_sems, ss, lands, after):
    n = len(ss)

    def body(*refs):
        srcs, dsts, ssem, rsem = refs[:n], refs[n:2 * n], refs[2 * n], refs[2 * n + 1]
        x, y, c = _pos()
        for i, (s, o) in enumerate(zip(srcs, dsts)):
            for k, m in enumerate(RELATIONS):
                px, py = _related_chip(x, y, m)
                cp = _rcopy(s.at[2 * px + py], o.at[k], ssem.at[3 * i + k], rsem.at[3 * i + k], (px, py, c))
                cp.wait_send()
                cp.wait_recv()

    both = list(ss) + list(lands)
    res = pl.pallas_call(
        body, name=name,
        out_shape=[pltpu.HBM(b.shape, b.dtype) for b in both],
        in_specs=[HBM] * (2 * n) + [SEM, SEM, pl.BlockSpec(memory_space=pl.ANY)],
        out_specs=[HBM] * (2 * n),
        input_output_aliases={i: i for i in range(2 * n)},
        compiler_params=pltpu.CompilerParams(has_side_effects=DATAFLOW),
    )(*both, send_sems, recv_sems, after)
    return list(res[:n]), list(res[n:])


def pair_share(name, bufs):
    n = len(bufs)

    def body(*refs):
        outs, send_sems, recv_sems = refs[n:2 * n], refs[2 * n], refs[2 * n + 1]
        x, y, c = _pos()
        sends = []
        for i, o in enumerate(outs):
            cp = _rcopy(o.at[c], o.at[c], send_sems.at[i], recv_sems.at[i], (x, y, 1 - c))
            cp.start()
            sends.append(cp)
        for i, o in enumerate(outs):
            _rcopy(o.at[1 - c], o.at[1 - c], send_sems.at[i], recv_sems.at[i], (x, y, 1 - c)).wait_recv()
        for cp in sends:
            cp.wait_send()

    return pl.pallas_call(
        body, name=name, out_shape=[jax.ShapeDtypeStruct(b.shape, b.dtype) for b in bufs],
        in_specs=[HBM] * n, out_specs=[HBM] * n,
        input_output_aliases={i: i for i in range(n)},
        scratch_shapes=[pltpu.SemaphoreType.DMA((n,)), pltpu.SemaphoreType.DMA((n,))],
    )(*bufs)


def all_allgather(name, buf):
    r, cd = buf.shape

    def body(src, out, send_sems, recv_sems, lsem):
        x, y, c = _pos()
        me = 4 * x + 2 * y + c
        local = pltpu.make_async_copy(src, out.at[me], lsem)
        local.start()

        def peer(mask):
            px = 1 - x if mask & 4 else x
            py = 1 - y if mask & 2 else y
            pc = 1 - c if mask & 1 else c
            return px, py, pc

        sends = []
        for mask in range(1, N_DEV):
            cp = _rcopy(src, out.at[me], send_sems.at[mask - 1], recv_sems.at[mask - 1], peer(mask))
            cp.start()
            sends.append(cp)
        for mask in range(1, N_DEV):
            px, py, pc = peer(mask)
            _rcopy(src, out.at[4 * px + 2 * py + pc], send_sems.at[mask - 1], recv_sems.at[mask - 1],
                   (px, py, pc)).wait_recv()
        for cp in sends:
            cp.wait_send()
        local.wait()

    return pl.pallas_call(
        body, name=name, out_shape=jax.ShapeDtypeStruct((N_DEV, r, cd), buf.dtype),
        in_specs=[HBM], out_specs=HBM,
        scratch_shapes=[pltpu.SemaphoreType.DMA((N_DEV - 1,)), pltpu.SemaphoreType.DMA((N_DEV - 1,)),
                        pltpu.SemaphoreType.DMA(())],
    )(buf)


ELEMENTWISE_BLOCK_BYTES = 2 << 20


def _row_block(rows, cols):
    for cand in (1024, 512, 256, 128, 64, 32, 16):
        if rows % cand == 0 and cand * cols * 4 <= ELEMENTWISE_BLOCK_BYTES:
            return cand
    return rows


def chip_sum(name, s, r, me, c):
    _, a, b = s.shape
    tr = _row_block(a, b)

    def body(idx_ref, s_ref, r_ref, o_ref):
        del idx_ref
        acc = s_ref[...].astype(F32)
        for k in range(3):
            acc = acc + r_ref[k].astype(F32)
        o_ref[...] = acc

    return pl.pallas_call(
        body, name=name, out_shape=jax.ShapeDtypeStruct((2, a, b), F32),
        grid_spec=pltpu.PrefetchScalarGridSpec(
            num_scalar_prefetch=1, grid=(a // tr,),
            in_specs=[pl.BlockSpec((None, tr, b), lambda i, idx: (idx[0], i, 0)),
                      pl.BlockSpec((3, tr, b), lambda i, idx: (0, i, 0))],
            out_specs=pl.BlockSpec((None, tr, b), lambda i, idx: (idx[1], i, 0))),
        compiler_params=_cparams(("arbitrary",)),
    )(jnp.stack([me, c]).astype(jnp.int32), s, r)


def pair_add(name, p, recv, c):
    nj, _, rh, cd = p.shape
    tr = _row_block(rh, cd)

    def body(c_ref, p_ref, r_ref, o_ref):
        del c_ref
        o_ref[...] = (p_ref[0] + r_ref[...]).astype(o_ref.dtype)

    return pl.pallas_call(
        body, name=name, out_shape=jax.ShapeDtypeStruct((nj, rh, cd), BF16),
        grid_spec=pltpu.PrefetchScalarGridSpec(
            num_scalar_prefetch=1, grid=(nj, rh // tr),
            in_specs=[pl.BlockSpec((1, 1, tr, cd), lambda j, i, c_ref: (j, c_ref[0], i, 0)),
                      pl.BlockSpec((1, tr, cd), lambda j, i, c_ref: (j, i, 0))],
            out_specs=pl.BlockSpec((1, tr, cd), lambda j, i, c_ref: (j, i, 0))),
        compiler_params=_cparams(("arbitrary", "arbitrary")),
    )(jnp.reshape(c, (1,)).astype(jnp.int32), p, recv)


def slab_sum(name, a):
    n, r, cd = a.shape
    tr = _pick(r, (256, 128, 64, 32, 16, 8))

    def body(a_ref, o_ref):
        acc = a_ref[0].astype(F32)
        for j in range(1, n):
            acc = acc + a_ref[j].astype(F32)
        o_ref[...] = acc

    return pl.pallas_call(
        body, name=name, out_shape=jax.ShapeDtypeStruct((r, cd), F32),
        grid=(r // tr,),
        in_specs=[pl.BlockSpec((n, tr, cd), lambda i: (0, i, 0))],
        out_specs=pl.BlockSpec((tr, cd), lambda i: (i, 0)),
        compiler_params=_cparams(("arbitrary",)),
    )(a)


ADAM_C1 = 1.0 - ADAM_B1 ** ADAM_STEP
ADAM_C2 = 1.0 - ADAM_B2 ** ADAM_STEP


def adamw(name, w, g, m, v):
    r, cd = w.shape
    tr = r
    for cand in (512, 256, 128, 64, 32, 16, 8):
        if r % cand == 0 and cand * cd * 4 <= (1 << 20):
            tr = cand
            break

    def body(w_ref, g_ref, m_ref, v_ref, d_ref, nm_ref, nv_ref):
        gv = g_ref[...]
        nm = ADAM_B1 * m_ref[...] + (1.0 - ADAM_B1) * gv
        nv = ADAM_B2 * v_ref[...] + (1.0 - ADAM_B2) * (gv * gv)
        m_hat = nm / ADAM_C1
        v_hat = nv / ADAM_C2
        d_ref[...] = -ADAM_LR * (m_hat / (jnp.sqrt(v_hat) + ADAM_EPS) + ADAM_WD * w_ref[...])
        nm_ref[...] = nm
        nv_ref[...] = nv

    spec = pl.BlockSpec((tr, cd), lambda i: (i, 0))
    sd = jax.ShapeDtypeStruct((r, cd), F32)
    return pl.pallas_call(
        body, name=name, out_shape=[sd, sd, sd], grid=(r // tr,),
        in_specs=[spec] * 4, out_specs=[spec] * 3,
        compiler_params=_cparams(("arbitrary",)),
    )(w, g, m, v)


WEIGHTS = ("norm_mix_w", "w_in", "ssm_conv_w", "ssm_conv_b", "ssm_dt_bias", "ssm_a_log", "ssm_d", "ssm_norm_w",
           "gdn_conv_w", "gdn_a_log", "gdn_dt_bias", "gdn_norm_w", "w_proj_ssm", "w_proj_gdn", "w_out",
           "norm_ffn_w", "w_ffn_in", "w_ffn_down", "final_norm_w")
BIG = (("w_in", 2), ("w_proj_ssm", 1), ("w_proj_gdn", 1), ("w_out", 1), ("w_ffn_in", 2), ("w_ffn_down", 1))
CONVW = (("ssm_conv_w", 2), ("gdn_conv_w", 2))
SHARDED = BIG + CONVW
SMALL = tuple(n for n in WEIGHTS if n not in dict(SHARDED))


def _unpack(buf, shapes, lead=()):
    flat = buf.reshape(lead + (-1,))
    out, o = [], 0
    for shp in shapes:
        n = math.prod(shp)
        out.append(flat[..., o:o + n].reshape(lead + tuple(shp)))
        o += n
    return out


def _pack_rows(arrs, lead=(), mult=8):
    nl = len(lead)
    flat = jnp.concatenate([a.reshape(lead + (-1,)) for a in arrs], axis=nl)
    n = flat.shape[nl]
    rows = -(-n // (mult * LANE)) * mult
    flat = jnp.pad(flat, [(0, 0)] * nl + [(0, rows * LANE - n)])
    return flat.reshape(lead + (rows, LANE))


def _slot_buffer(shard):
    return jnp.pad(shard[None], [(0, N_CHIPS - 1)] + [(0, 0)] * shard.ndim)


def kernel(x, norm_mix_w, w_in, ssm_conv_w, ssm_conv_b, ssm_dt_bias, ssm_a_log, ssm_d, ssm_norm_w, gdn_conv_w, gdn_a_log, gdn_dt_bias, gdn_norm_w, w_proj_ssm, w_proj_gdn, w_out, norm_ffn_w, w_ffn_in, w_ffn_down, final_norm_w, loss_target, m_norm_mix_w, m_w_in, m_ssm_conv_w, m_ssm_conv_b, m_ssm_dt_bias, m_ssm_a_log, m_ssm_d, m_ssm_norm_w, m_gdn_conv_w, m_gdn_a_log, m_gdn_dt_bias, m_gdn_norm_w, m_w_proj_ssm, m_w_proj_gdn, m_w_out, m_norm_ffn_w, m_w_ffn_in, m_w_ffn_down, m_final_norm_w, v_norm_mix_w, v_w_in, v_ssm_conv_w, v_ssm_conv_b, v_ssm_dt_bias, v_ssm_a_log, v_ssm_d, v_ssm_norm_w, v_gdn_conv_w, v_gdn_a_log, v_gdn_dt_bias, v_gdn_norm_w, v_w_proj_ssm, v_w_proj_gdn, v_w_out, v_norm_ffn_w, v_w_ffn_in, v_w_ffn_down, v_final_norm_w):
    wl = (norm_mix_w, w_in, ssm_conv_w, ssm_conv_b, ssm_dt_bias, ssm_a_log, ssm_d, ssm_norm_w, gdn_conv_w,
          gdn_a_log, gdn_dt_bias, gdn_norm_w, w_proj_ssm, w_proj_gdn, w_out, norm_ffn_w, w_ffn_in, w_ffn_down,
          final_norm_w)
    ml = (m_norm_mix_w, m_w_in, m_ssm_conv_w, m_ssm_conv_b, m_ssm_dt_bias, m_ssm_a_log, m_ssm_d, m_ssm_norm_w,
          m_gdn_conv_w, m_gdn_a_log, m_gdn_dt_bias, m_gdn_norm_w, m_w_proj_ssm, m_w_proj_gdn, m_w_out,
          m_norm_ffn_w, m_w_ffn_in, m_w_ffn_down, m_final_norm_w)
    vl = (v_norm_mix_w, v_w_in, v_ssm_conv_w, v_ssm_conv_b, v_ssm_dt_bias, v_ssm_a_log, v_ssm_d, v_ssm_norm_w,
          v_gdn_conv_w, v_gdn_a_log, v_gdn_dt_bias, v_gdn_norm_w, v_w_proj_ssm, v_w_proj_gdn, v_w_out,
          v_norm_ffn_w, v_w_ffn_in, v_w_ffn_down, v_final_norm_w)
    w = dict(zip(WEIGHTS, wl))
    m = dict(zip(WEIGHTS, ml))
    v = dict(zip(WEIGHTS, vl))
    x_pos, y_pos, c = _pos()
    me = 2 * x_pos + y_pos
    big = [n for n, _ in BIG]

    shards = [w[n].astype(BF16) for n in big]
    shards[0] = jnp.pad(shards[0], ((0, 0), (0, 0), (0, IN_SHARD_PAD - IN_SHARD)))
    conv_shapes = [w[n].shape[1:] for n, _ in CONVW]
    conv_pack = _pack_rows([w[n] for n, _ in CONVW], lead=(DEPTH,), mult=16)

    def slot_buffers(l):
        return [_slot_buffer(s[l].reshape((2, s.shape[1] // 2) + s.shape[2:])) for s in shards + [conv_pack]]

    def assemble(bufs):
        by_chip = [[lax.dynamic_index_in_dim(g_, jnp.bitwise_xor(me, j), 0, keepdims=False)
                    .reshape((-1,) + g_.shape[3:]) for j in range(N_CHIPS)] for g_ in bufs]
        out = {}
        for i, (n, axis) in enumerate(BIG):
            cols = IN_SHARD if n == "w_in" else by_chip[i][0].shape[-1]
            out[n] = jnp.concatenate([p[:, :cols] for p in by_chip[i]], axis=axis - 1)
        conv_parts = [_unpack(p, conv_shapes) for p in by_chip[-1]]
        for i, (n, axis) in enumerate(CONVW):
            out[n] = jnp.concatenate([conv_parts[j][i] for j in range(N_CHIPS)], axis=axis - 1)
        return out

    landed0 = weights_gather("gather_w_l0", slot_buffers(0))
    send_sems, recv_sems, flying, token = gather_start("gather_w_l1_start", slot_buffers(1), landed0[0])

    def gathered_of(l, after):
        if l == 0:
            return assemble(landed0)
        landed1 = gather_wait("gather_w_l1_wait", send_sems, recv_sems, flying, after)
        return assemble(weights_forward("gather_w_l1_forward", landed1))

    in_flight = {}

    def start_reduction(l, g_layer, dx):
        halves = [g_layer[n].reshape((1, 2, -1, AL_DIM)) if n == "w_in"
                  else g_layer[n].reshape((N_CHIPS, 2, -1) + g_layer[n].shape[-1:]) for n in big]
        from_pair = pair_swap(f"grad_pair_swap_l{l}", halves)
        chip_part = [pair_add(f"grad_pair_add_{n}_l{l}", g_, r_, c) for n, g_, r_ in zip(big, halves, from_pair)]
        chip_part[0] = _aligned_to_shards(chip_part[0][0])
        *in_flight[l], token_l = scatter_start(f"grad_scatter_start_l{l}", chip_part, dx)
        return dx + token_l[0, 0]

    def finish_reduction(l, after):
        parts, landed = scatter_wait(f"grad_scatter_wait_l{l}", *in_flight[l], after)
        sums = [chip_sum(f"grad_chip_sum_{n}_l{l}", s_, r_, me, c) for n, s_, r_ in zip(big, parts, landed)]
        return [r_.reshape((-1,) + r_.shape[2:]) for r_ in pair_share(f"grad_pair_share_l{l}", sums)]

    full = {n: w[n] for n in SMALL}
    loss_part, grad_x, grads = local_step(x[0] + token[0, 0], loss_target[0], full, gathered_of, start_reduction)
    reduced = [None] * DEPTH
    after = grad_x
    for l in reversed(range(DEPTH)):
        reduced[l] = finish_reduction(l, after)
        after = reduced[l][0]
    g_sharded = {n: jnp.stack([reduced[l][i] for l in range(DEPTH)]) for i, n in enumerate(big)}
    g_sharded["w_in"] = g_sharded["w_in"][:, :, :IN_SHARD]

    small_names = list(SMALL) + [n for n, _ in CONVW]
    small_all = all_allgather("gather_small", _pack_rows([grads[n] for n in small_names] + [loss_part[0, :1]]))
    small_sum = slab_sum("small_sum", small_all)
    small_vals = _unpack(small_sum, [grads[n].shape for n in small_names] + [(1,)])
    g_small = dict(zip(small_names, small_vals[:-1]))
    loss = small_vals[-1].reshape(())
    for n, axis in CONVW:
        size = w[n].shape[axis]
        g_sharded[n] = lax.dynamic_slice_in_dim(g_small.pop(n), me * size, size, axis=axis)

    out_g, out_d, out_m, out_v = {}, {}, {}, {}
    for n, _ in SHARDED:
        shp = w[n].shape
        two = lambda a: a.reshape(-1, shp[-1])
        d_, m_, v_ = adamw(f"adamw_{n}", two(w[n]), two(g_sharded[n]), two(m[n]), two(v[n]))
        out_g[n], out_d[n], out_m[n], out_v[n] = g_sharded[n], d_.reshape(shp), m_.reshape(shp), v_.reshape(shp)
    d_, m_, v_ = adamw("adamw_small", *[_pack_rows([d[n] for n in SMALL]) for d in (w, g_small, m, v)])
    small_shapes = [w[n].shape for n in SMALL]
    for n, dd, mm, vv in zip(SMALL, _unpack(d_, small_shapes), _unpack(m_, small_shapes), _unpack(v_, small_shapes)):
        out_g[n], out_d[n], out_m[n], out_v[n] = g_small[n], dd, mm, vv

    return (loss, grad_x[None], *[out_g[n] for n in WEIGHTS], *[out_d[n] for n in WEIGHTS],
            *[out_m[n] for n in WEIGHTS], *[out_v[n] for n in WEIGHTS])
```

```python
import math

import jax
import jax.numpy as jnp
from jax import lax
from jax.experimental import pallas as pl
from jax.experimental.pallas import tpu as pltpu

F32 = jnp.float32
BF16 = jnp.bfloat16

D_MODEL = 1024
DEPTH = 2
SSM_HEADS = 16
SSM_HEAD_DIM = 64
SSM_D_INNER = 1024
SSM_STATE = 128
SSM_CONV_DIM = 1536
GDN_HEADS = 8
GDN_HEAD = 128
GDN_QKV_DIM = 3072
CONV_K = 4
CHUNK = 128
SSD_CHUNK = 256
FFN_HIDDEN = 2816
EPS = 1e-6
IN_DIM = 8736

ADAM_LR = 0.001
ADAM_B1 = 0.9
ADAM_B2 = 0.999
ADAM_EPS = 1e-08
ADAM_WD = 0.01
ADAM_STEP = 10

LANE = 128
NEG_BIG = -1e30
VMEM_LIMIT = 56 * 1024 * 1024

AL_Z, AL_XBC, AL_QKV, AL_GZ, AL_GS, AL_GG, AL_SMALL, AL_DIM = 0, 1024, 2560, 5632, 6656, 7680, 8704, 9216
SM_DT, SM_A, SM_B = 0, 16, 24

HI = lax.Precision.HIGHEST
NN = (((1,), (0,)), ((), ()))
NT = (((1,), (1,)), ((), ()))
TN = (((0,), (0,)), ((), ()))


def _cparams(sem):
    return pltpu.CompilerParams(dimension_semantics=sem, vmem_limit_bytes=VMEM_LIMIT)


def _pick(n, prefs):
    for p in prefs:
        if n % p == 0:
            return p
    return n


MATMUL_VMEM_BUDGET = 40 << 20
MXU_WIDTH = 256
HBM_BYTES_PER_S = 3.3e12
MXU_FLOPS_PER_S = 9.0e14
GRID_STEP_S = 0.35e-6
N_CHIPS = 4


def _matmul_tiles(m, n, n_dom, k, a_item, b_item, o_item):
    best = None
    def cands(dim, cap):
        return [c for c in range(LANE, min(dim, cap) + 1, LANE) if dim % c == 0] or [dim]

    tms, tns, tks = cands(m, 2048), cands(n_dom, 2304), cands(k, 1 << 30)
    for tm in tms:
        for tn in tns:
            for tk in tks:
                nk = k // tk
                vmem = (2 * (tm * tk * a_item + tk * tn * b_item + tm * tn * o_item) + tm * tn * 4 * (2 if nk > 1 else 1)
                        + (tm * tk * 2 if a_item > 2 else 0) + (tk * tn * 2 if b_item > 2 else 0))
                if vmem > MATMUL_VMEM_BUDGET:
                    continue
                traffic = m * k * a_item * (1 if nk == 1 else n // tn) + k * n * b_item * (m // tm)
                mxu_fill = tn / (-(-tn // MXU_WIDTH) * MXU_WIDTH)
                cost = (max(traffic / HBM_BYTES_PER_S, 2.0 * m * n * k / (MXU_FLOPS_PER_S * mxu_fill))
                        + (m // tm) * (n // tn) * nk * GRID_STEP_S)
                if best is None or cost < best[0]:
                    best = (cost, (tm, tn, tk))
    return best[1]


def _dot(a, b, dims=NN):
    return lax.dot_general(a.astype(BF16), b.astype(BF16), dims, preferred_element_type=F32)


def _dot3(a, b, dims=NN):
    a_hi, b_hi = a.astype(BF16), b.astype(BF16)
    a_lo = (a - a_hi.astype(F32)).astype(BF16)
    b_lo = (b - b_hi.astype(F32)).astype(BF16)

    def dg(u, v):
        return lax.dot_general(u, v, dims, preferred_element_type=F32)

    return dg(a_hi, b_hi) + (dg(a_hi, b_lo) + dg(a_lo, b_hi))


def _dot_hi(a, b, dims=NN):
    return lax.dot_general(a, b, dims, precision=HI, preferred_element_type=F32)


def _sigmoid(x):
    return jax.nn.sigmoid(x)


def _silu(x):
    return x * _sigmoid(x)


def _softplus(x):
    return jnp.maximum(x, 0.0) + jnp.log1p(jnp.exp(-jnp.abs(x)))


def matmul(name, a, b, mode, out_dtype=F32, chip_major=False, stack=None):
    if mode == "nn":
        (m, k), (k2, n) = a.shape, b.shape
    elif mode == "nt":
        (m, k), (n, k2) = a.shape, b.shape
    else:
        (k, m), (k2, n) = a.shape, b.shape
    assert k == k2, (a.shape, b.shape, mode)
    shard = n // N_CHIPS if chip_major else n
    tm, tn, tk = _matmul_tiles(m, n, shard, k, a.dtype.itemsize, b.dtype.itemsize, jnp.dtype(out_dtype).itemsize)
    if chip_major:
        per = shard // tn
        base_shape, base_blk = (N_CHIPS, m, shard), (None, tm, tn)
        base_idx = lambda i, j: (j // per, i, j % per)
    else:
        base_shape, base_blk = (m, n), (tm, tn)
        base_idx = lambda i, j: (i, j)
    nk = k // tk
    dims = {"nn": NN, "nt": NT, "tn": TN}[mode]

    def body_acc(a_ref, b_ref, o_ref, acc_ref):
        kk = pl.program_id(2)

        @pl.when(kk == 0)
        def _():
            acc_ref[...] = jnp.zeros_like(acc_ref)

        acc_ref[...] += _dot(a_ref[...], b_ref[...], dims)

        @pl.when(kk == nk - 1)
        def _():
            o_ref[...] = acc_ref[...].astype(o_ref.dtype)

    def body_one(a_ref, b_ref, o_ref):
        o_ref[...] = _dot(a_ref[...], b_ref[...], dims).astype(o_ref.dtype)

    compute = body_one if nk == 1 else body_acc
    if mode == "tn":
        a_spec = pl.BlockSpec((tk, tm), lambda i, j, kk: (kk, i))
    else:
        a_spec = pl.BlockSpec((tm, tk), lambda i, j, kk: (i, kk))
    if mode == "nt":
        b_spec = pl.BlockSpec((tn, tk), lambda i, j, kk: (j, kk))
    else:
        b_spec = pl.BlockSpec((tk, tn), lambda i, j, kk: (kk, j))
    in_specs, operands, aliases, body = [a_spec, b_spec], [a, b], {}, compute
    if stack is None:
        out_shape, out_blk, out_idx = base_shape, base_blk, (lambda i, j, kk: base_idx(i, j))
    else:
        layer, buf = stack
        out_shape, out_blk = (DEPTH,) + base_shape, (None,) + base_blk
        out_idx = lambda i, j, kk: (layer,) + base_idx(i, j)
        if buf is not None:
            assert buf.shape == out_shape and buf.dtype == out_dtype
            in_specs.append(pl.BlockSpec(memory_space=pl.ANY))
            operands.append(buf)
            aliases = {2: 0}

            def body(a_ref, b_ref, buf_ref, *rest):
                del buf_ref
                compute(a_ref, b_ref, *rest)

    return pl.pallas_call(
        body, name=name,
        out_shape=jax.ShapeDtypeStruct(out_shape, out_dtype),
        grid=(m // tm, n // tn, nk),
        in_specs=in_specs,
        out_specs=pl.BlockSpec(out_blk, out_idx),
        scratch_shapes=[] if nk == 1 else [pltpu.VMEM((tm, tn), F32)],
        input_output_aliases=aliases,
        compiler_params=_cparams(("parallel", "parallel", "arbitrary")),
    )(*operands)


def _row_map(c0, moves):
    return (lambda j, i: (i, c0 + j)) if moves else (lambda j, i: (i, c0))


def _par_map(c0, moves):
    return (lambda j, i: (0, c0 + j)) if moves else (lambda j, i: (0, c0))


def _in_spec(op, tile):
    _, kind, w, c0, moves = op
    if kind == "row":
        return pl.BlockSpec((tile, w), _row_map(c0, moves))
    return pl.BlockSpec((1, w), _par_map(c0, moves))


ROW_BLOCK_ELEMS = 1 << 18


def _row_tile(t, tile, ops):
    width = max(op[2] for op in ops if op[1] == "row")
    return min(t, max(tile, ROW_BLOCK_ELEMS // width))


def rowwise_fwd(name, fn, t, tile, ncol, ins, outs):
    n_in = len(ins)
    tile = _row_tile(t, tile, ins)

    def body(*refs):
        vals = [r[...].astype(F32) for r in refs[:n_in]]
        res = fn(*vals)
        if not isinstance(res, (tuple, list)):
            res = (res,)
        for r, v in zip(refs[n_in:], res):
            r[...] = v.astype(r.dtype)

    res = pl.pallas_call(
        body, name=name,
        out_shape=[jax.ShapeDtypeStruct((t, w * ncol), dt) for w, dt in outs],
        grid=(ncol, t // tile),
        in_specs=[_in_spec(op, tile) for op in ins],
        out_specs=[pl.BlockSpec((tile, w), _row_map(0, True)) for w, _ in outs],
        compiler_params=_cparams(("arbitrary", "arbitrary")),
    )(*[op[0] for op in ins])
    return res


def rowwise_bwd(name, fn, t, tile, ncol, ins, need, cts, addends=None, row_dtypes=None):
    n_in, n_ct = len(ins), len(cts)
    tile = _row_tile(t, tile, ins)
    addends = addends or {}
    row_dtypes = row_dtypes or {}
    didx = [i for i in range(n_in) if need[i]]
    add_ops = [addends[i] for i in didx if i in addends]
    n_add = len(add_ops)

    def body(*refs):
        in_refs = refs[:n_in]
        ct_refs = refs[n_in:n_in + n_ct]
        add_refs = refs[n_in + n_ct:n_in + n_ct + n_add]
        out_refs = refs[n_in + n_ct + n_add:]
        vals = [r[...].astype(F32) for r in in_refs]

        def g(*dv):
            full = list(vals)
            for i, v in zip(didx, dv):
                full[i] = v
            res = fn(*full)
            return tuple(res) if isinstance(res, (tuple, list)) else (res,)

        _, vjp = jax.vjp(g, *[vals[i] for i in didx])
        grads = vjp(tuple(c[...].astype(F32) for c in ct_refs))
        j, i = pl.program_id(0), pl.program_id(1)
        a = 0
        for o_ref, gv, idx in zip(out_refs, grads, didx):
            _, kind, _, _, moves = ins[idx]
            if kind == "row":
                if idx in addends:
                    gv = gv + add_refs[a][...].astype(F32)
                    a += 1
                o_ref[...] = gv.astype(o_ref.dtype)
            else:
                first = (i == 0) if moves else jnp.logical_and(i == 0, j == 0)

                @pl.when(first)
                def _(o_ref=o_ref, gv=gv):
                    o_ref[...] = gv

                @pl.when(jnp.logical_not(first))
                def _(o_ref=o_ref, gv=gv):
                    o_ref[...] += gv

    out_shape, out_specs = [], []
    for idx in didx:
        _, kind, w, _, moves = ins[idx]
        cols = w * (ncol if moves else 1)
        if kind == "row":
            out_shape.append(jax.ShapeDtypeStruct((t, cols), row_dtypes.get(idx, F32)))
            out_specs.append(pl.BlockSpec((tile, w), _row_map(0, moves)))
        else:
            out_shape.append(jax.ShapeDtypeStruct((1, cols), F32))
            out_specs.append(pl.BlockSpec((1, w), _par_map(0, moves)))
    ops = list(ins) + list(cts) + add_ops
    res = pl.pallas_call(
        body, name=name,
        out_shape=out_shape,
        grid=(ncol, t // tile),
        in_specs=[_in_spec(op, tile) for op in ops],
        out_specs=out_specs,
        compiler_params=_cparams(("arbitrary", "arbitrary")),
    )(*[op[0] for op in ops])
    return res


def f_rmsnorm(x, w):
    return x * lax.rsqrt(jnp.mean(x * x, axis=-1, keepdims=True) + EPS) * w


def f_ssd_post(y, z, w):
    y = y * _silu(z)
    return y * lax.rsqrt(jnp.mean(y * y, axis=-1, keepdims=True) + EPS) * w


def f_gdn_post(o, z, w):
    o = o * lax.rsqrt(jnp.mean(o * o, axis=-1, keepdims=True) + EPS) * w
    return o * _silu(z)


def f_merge(gs, p1, gg, p2):
    return _sigmoid(gs) * p1 + _sigmoid(gg) * p2


def f_swiglu(g, u):
    return _silu(g) * u


def final_loss(name, x, tgt, w, tile=256):
    t, d = x.shape

    def body(x_ref, t_ref, w_ref, loss_ref, dx_ref, dw_ref):
        i = pl.program_id(0)
        xv, tv, wv = x_ref[...], t_ref[...], w_ref[...]

        def g(xx, ww):
            err = f_rmsnorm(xx, ww) - tv
            return 0.5 * jnp.sum(jnp.mean(err * err, axis=-1, keepdims=True), axis=0, keepdims=True)

        val, vjp = jax.vjp(g, xv, wv)
        dx, dw = vjp(jnp.ones((1, 1), F32))
        dx_ref[...] = dx
        lv = jnp.broadcast_to(val, (1, LANE))

        @pl.when(i == 0)
        def _():
            loss_ref[...] = lv
            dw_ref[...] = dw

        @pl.when(i != 0)
        def _():
            loss_ref[...] += lv
            dw_ref[...] += dw

    return pl.pallas_call(
        body, name=name,
        out_shape=[jax.ShapeDtypeStruct((1, LANE), F32), jax.ShapeDtypeStruct((t, d), F32),
                   jax.ShapeDtypeStruct((1, d), F32)],
        grid=(t // tile,),
        in_specs=[pl.BlockSpec((tile, d), lambda i: (i, 0)), pl.BlockSpec((tile, d), lambda i: (i, 0)),
                  pl.BlockSpec((1, d), lambda i: (0, 0))],
        out_specs=[pl.BlockSpec((1, LANE), lambda i: (0, 0)), pl.BlockSpec((tile, d), lambda i: (i, 0)),
                   pl.BlockSpec((1, d), lambda i: (0, 0))],
        compiler_params=_cparams(("arbitrary",)),
    )(x, tgt, w)


CONV_W = 512
HALO = 8
STRIPS = 4


def _rows_back(before, cur, d):
    rows = lax.broadcasted_iota(jnp.int32, cur.shape, 0)
    return jnp.where(rows < d, pltpu.roll(before, d, 0), pltpu.roll(cur, d, 0))


def _rows_ahead(cur, after, d):
    rows = lax.broadcasted_iota(jnp.int32, cur.shape, 0)
    return jnp.where(rows < HALO - d, pltpu.roll(cur, HALO - d, 0), pltpu.roll(after, HALO - d, 0))


def _conv_taps(taps, bias, before, cur):
    shifted = [_rows_back(before, cur, CONV_K - 1 - k) for k in range(CONV_K - 1)] + [cur]
    pre = bias + taps[CONV_K - 1] * cur
    for k in range(CONV_K - 1):
        pre = pre + taps[k] * shifted[k]
    return pre, shifted


def conv_fwd(name, src, c0, width, w, b, tile=512):
    t = src.shape[0]
    ncol, nrow = width // CONV_W, t // tile
    cb0 = c0 // CONV_W
    hb = tile // HALO

    def body(prev_ref, cur_ref, w_ref, b_ref, o_ref):
        i = pl.program_id(1)
        taps = [w_ref[k:k + 1, :] for k in range(CONV_K)]
        bias = b_ref[...]

        def strips(g, before):
            for u in range(STRIPS):
                r0 = pl.multiple_of((g * STRIPS + u) * HALO, HALO)
                cur = cur_ref[pl.ds(r0, HALO), :]
                pre, _ = _conv_taps(taps, bias, before, cur)
                o_ref[pl.ds(r0, HALO), :] = _silu(pre)
                before = cur
            return before

        lax.fori_loop(0, tile // (HALO * STRIPS), strips, jnp.where(i == 0, 0.0, prev_ref[...]))

    return pl.pallas_call(
        body, name=name,
        out_shape=jax.ShapeDtypeStruct((t, width), F32),
        grid=(ncol, nrow),
        in_specs=[pl.BlockSpec((HALO, CONV_W), lambda j, i: (jnp.maximum(i * hb - 1, 0), cb0 + j)),
                  pl.BlockSpec((tile, CONV_W), lambda j, i: (i, cb0 + j)),
                  pl.BlockSpec((CONV_K, CONV_W), lambda j, i: (0, j)),
                  pl.BlockSpec((1, CONV_W), lambda j, i: (0, j))],
        out_specs=pl.BlockSpec((tile, CONV_W), lambda j, i: (i, j)),
        compiler_params=_cparams(("arbitrary", "arbitrary")),
    )(src, src, w, b)


def conv_bwd(name, src, c0, width, w, b, dy, tile=512):
    t = src.shape[0]
    ncol, nrow = width // CONV_W, t // tile
    cb0 = c0 // CONV_W
    hb = tile // HALO
    last_hb = t // HALO - 1
    nstrip = tile // HALO

    def body(sprev_ref, scur_ref, snext_ref, w_ref, b_ref, dycur_ref, dynext_ref,
             du_ref, dw_ref, db_ref, dpre_ref):
        i = pl.program_id(1)
        taps = [w_ref[k:k + 1, :] for k in range(CONV_K)]
        bias = b_ref[...]

        def dpre_of(before, cur, dy_strip):
            pre, shifted = _conv_taps(taps, bias, before, cur)
            s = _sigmoid(pre)
            return dy_strip * (s * (1.0 + pre * (1.0 - s))), shifted

        def strips1(g, carry):
            before, dws, dbs = carry
            for u in range(STRIPS):
                r0 = pl.multiple_of((g * STRIPS + u) * HALO, HALO)
                cur = scur_ref[pl.ds(r0, HALO), :]
                dpre, shifted = dpre_of(before, cur, dycur_ref[pl.ds(r0, HALO), :])
                dpre_ref[pl.ds(r0, HALO), :] = dpre
                before, dws, dbs = cur, tuple(a + dpre * v for a, v in zip(dws, shifted)), dbs + dpre
            return before, dws, dbs

        zero = jnp.zeros((HALO, CONV_W), F32)
        before, dws, dbs = lax.fori_loop(0, nstrip // STRIPS, strips1,
                                         (jnp.where(i == 0, 0.0, sprev_ref[...]), (zero,) * CONV_K, zero))
        dpre_next, _ = dpre_of(before, snext_ref[...], dynext_ref[...])
        dpre_ref[pl.ds(tile, HALO), :] = jnp.where(i == nrow - 1, 0.0, dpre_next)

        def strips2(g, _):
            parts = []
            for u in range(STRIPS):
                r0 = pl.multiple_of((g * STRIPS + u) * HALO, HALO)
                cur = dpre_ref[pl.ds(r0, HALO), :]
                after = dpre_ref[pl.ds(r0 + HALO, HALO), :]
                acc = taps[CONV_K - 1] * cur
                for d in range(1, CONV_K):
                    acc = acc + taps[CONV_K - 1 - d] * _rows_ahead(cur, after, d)
                parts.append(acc)
            r0 = pl.multiple_of(g * STRIPS * HALO, STRIPS * HALO)
            du_ref[pl.ds(r0, STRIPS * HALO), :] = jnp.concatenate(parts, axis=0).astype(du_ref.dtype)
            return 0

        lax.fori_loop(0, nstrip // STRIPS, strips2, 0)
        dw_tile = jnp.concatenate([jnp.sum(a, axis=0, keepdims=True) for a in dws], axis=0)
        db_tile = jnp.sum(dbs, axis=0, keepdims=True)
        _acc(dw_ref, dw_tile, i == 0)
        _acc(db_ref, db_tile, i == 0)

    return pl.pallas_call(
        body, name=name,
        out_shape=[jax.ShapeDtypeStruct((t, width), BF16), jax.ShapeDtypeStruct((CONV_K, width), F32),
                   jax.ShapeDtypeStruct((1, width), F32)],
        grid=(ncol, nrow),
        in_specs=[pl.BlockSpec((HALO, CONV_W), lambda j, i: (jnp.maximum(i * hb - 1, 0), cb0 + j)),
                  pl.BlockSpec((tile, CONV_W), lambda j, i: (i, cb0 + j)),
                  pl.BlockSpec((HALO, CONV_W), lambda j, i: (jnp.minimum((i + 1) * hb, last_hb), cb0 + j)),
                  pl.BlockSpec((CONV_K, CONV_W), lambda j, i: (0, j)),
                  pl.BlockSpec((1, CONV_W), lambda j, i: (0, j)),
                  pl.BlockSpec((tile, CONV_W), lambda j, i: (i, j)),
                  pl.BlockSpec((HALO, CONV_W), lambda j, i: (jnp.minimum((i + 1) * hb, last_hb), j))],
        out_specs=[pl.BlockSpec((tile, CONV_W), lambda j, i: (i, j)),
                   pl.BlockSpec((CONV_K, CONV_W), lambda j, i: (0, j)),
                   pl.BlockSpec((1, CONV_W), lambda j, i: (0, j))],
        scratch_shapes=[pltpu.VMEM((tile + HALO, CONV_W), F32)],
        compiler_params=_cparams(("arbitrary", "arbitrary")),
    )(src, src, src, w, b, dy, dy)


def _iota2(q):
    return (lax.broadcasted_iota(jnp.int32, (q, q), 0), lax.broadcasted_iota(jnp.int32, (q, q), 1))


def _lane_pick(blk, idx):
    lane = lax.broadcasted_iota(jnp.int32, (1, LANE), 1)
    return jnp.sum(jnp.where(lane == idx, blk, 0.0), axis=1, keepdims=True)


class _Decay:
    def __init__(self, a):
        q = a.shape[0]
        r, c = _iota2(q)
        self.r, self.c = r, c
        self.cum = _dot_hi((c <= r).astype(F32), a)
        self.cum_t = _dot_hi(a, (r <= c).astype(F32), TN)
        self.tot = self.cum[q - 1:q, :]
        self.e_cum = jnp.exp(self.cum)
        self.e_rest = jnp.exp(self.tot - self.cum)
        self.e_tot = jnp.exp(self.tot)

    def mask(self, lane):
        rows = lax.broadcasted_iota(jnp.int32, (LANE, 1), 0)
        cum_row = jnp.sum(jnp.where(rows == lane, self.cum_t, 0.0), axis=0, keepdims=True)
        return jnp.exp(jnp.where(self.r >= self.c, _lane_pick(self.cum, lane) - cum_row, NEG_BIG))


_SSD_B = SSM_D_INNER
_SSD_C = SSM_D_INNER + 2 * SSM_STATE


def _interleave(gens):
    results = [None] * len(gens)
    live = list(range(len(gens)))
    while live:
        for i in list(live):
            try:
                next(gens[i])
            except StopIteration as stop:
                results[i] = stop.value
                live.remove(i)
    return results


def ssd_chunk(xs, bm, cm, dt_all, dsk, dec, state, p, cb):
    lane = lax.broadcasted_iota(jnp.int32, (1, LANE), 1)
    m0 = lane < SSM_HEAD_DIM
    h0, h1 = 2 * p, 2 * p + 1

    def both(blk):
        return jnp.where(m0, _lane_pick(blk, h0), _lane_pick(blk, h1))

    xdt = xs * both(dt_all)
    l0, l1 = dec.mask(h0), dec.mask(h1)
    yield
    y_diag = _dot(cb * l0, jnp.where(m0, xdt, 0.0)) + _dot(cb * l1, jnp.where(m0, 0.0, xdt))
    y_off = _dot(cm, state, NT) * both(dec.e_cum)
    yield
    rowm = lax.broadcasted_iota(jnp.int32, (LANE, 1), 0) < SSM_HEAD_DIM
    new_state = (state * jnp.where(rowm, _lane_pick(dec.e_tot, h0), _lane_pick(dec.e_tot, h1))
                 + _dot(xdt * both(dec.e_rest), bm, TN))
    y = y_diag + y_off + both(dsk) * xs
    return y, new_state


def ssd_pairs(xs, bms, cms, small, dtb, alog, dsk, states):
    dt_all = _softplus(small + dtb)
    dec = _Decay(dt_all * (-jnp.exp(alog)))
    cbs = [_dot(cm, bm, NT) for cm, bm in zip(cms, bms)]
    res = _interleave([ssd_chunk(x, bms[p // 4], cms[p // 4], dt_all, dsk, dec, st, p, cbs[p // 4])
                       for p, (x, st) in enumerate(zip(xs, states))])
    return tuple(y for y, _ in res), tuple(s for _, s in res)


def tri_inverse(a):
    q = a.shape[0]
    r, c = _iota2(q)
    eye = (r == c).astype(F32)
    diag = (r // 16) == (c // 16)
    bd = jnp.where(diag, a, 0.0)
    off = jnp.where(diag, 0.0, a)
    b2 = _dot3(bd, bd)
    d1 = _dot3(eye - bd, eye + b2)
    yield
    b4 = _dot3(b2, b2)
    yield
    b8 = _dot3(b4, b4)
    d2 = _dot3(d1, eye + b4)
    yield
    dinv = _dot3(d2, eye + b8)
    yield
    n = _dot3(dinv, off)
    yield
    powers = [n]
    while 16 * 2 ** len(powers) < q:
        powers.append(_dot3(powers[-1], powers[-1]))
        yield
    m = dinv
    for pw in reversed(powers[1:]):
        m = _dot3(eye + pw, m)
        yield
    return _dot3(eye - n, m)


@jax.custom_vjp
def _solve_with(xinv, a, rhs):
    del a
    return _dot3(xinv, rhs)


def _solve_with_fwd(xinv, a, rhs):
    t = _dot3(xinv, rhs)
    return t, (xinv, t)


def _solve_with_bwd(res, dt):
    xinv, t = res
    d_rhs = _dot3(xinv, dt, TN)
    d_a = -_dot(d_rhs, t, NT)
    return jnp.zeros_like(xinv), d_a, d_rhs


_solve_with.defvjp(_solve_with_fwd, _solve_with_bwd)


_GDN_K = GDN_HEADS * GDN_HEAD
_GDN_V = 2 * GDN_HEADS * GDN_HEAD


def gdn_chunk(qh, kh, vh, beta_all, dec, state, h, xinv=None):
    r, c = dec.r, dec.c
    qn = qh * lax.rsqrt(jnp.sum(qh * qh, axis=-1, keepdims=True) + EPS) * (GDN_HEAD ** -0.5)
    kn = kh * lax.rsqrt(jnp.sum(kh * kh, axis=-1, keepdims=True) + EPS)
    beta = _lane_pick(beta_all, SM_B + h)
    decay = dec.mask(SM_A + h)
    yield
    kk = _dot(kn, kn, NT)
    qk = _dot(qn, kn, NT) * decay
    amat = jnp.where(r > c, kk * decay * beta, 0.0)
    eg = _lane_pick(dec.e_cum, SM_A + h)
    rhs = jnp.concatenate([vh * beta, kn * (beta * eg)], axis=1)
    qs = _dot(qn * eg, state)
    yield
    if xinv is None:
        xinv = yield from tri_inverse(amat)
        t = _dot3(xinv, rhs)
    else:
        t = _solve_with(xinv, amat, rhs)
    yield
    u, w = t[:, :GDN_HEAD], t[:, GDN_HEAD:]
    v_new = u - _dot(w, state)
    yield
    o = qs + _dot(qk, v_new)
    new_state = (state * _lane_pick(dec.e_tot, SM_A + h)
                 + _dot(kn * _lane_pick(dec.e_rest, SM_A + h), v_new, TN))
    return o, new_state, xinv


def gdn_heads(qs, ks, vs, small, alog, dtb, states, xinvs=None):
    nh = len(qs)
    beta_all = _sigmoid(small)
    dec = _Decay(-jnp.exp(alog) * _softplus(small + dtb))
    res = _interleave([gdn_chunk(qs[h], ks[h], vs[h], beta_all, dec, states[h], h,
                                 None if xinvs is None else xinvs[h]) for h in range(nh)])
    return tuple(o for o, _, _ in res), tuple(s for _, s, _ in res), tuple(x for _, _, x in res)


def _acc(ref, val, first):
    @pl.when(first)
    def _():
        ref[...] = val

    @pl.when(jnp.logical_not(first))
    def _():
        ref[...] += val


def ssd_scan_fwd(name, xbc, proj, dtb, alog, dsk):
    t = xbc.shape[0]
    nc, npair = t // SSD_CHUNK, SSM_HEADS // 2
    small_blk = AL_SMALL // LANE

    def body(xbc_ref, sm_ref, dtb_ref, alog_ref, dsk_ref, y_ref, sin_ref, st_ref):
        ci = pl.program_id(0)

        @pl.when(ci == 0)
        def _():
            st_ref[...] = jnp.zeros_like(st_ref)

        s_in = tuple(st_ref[p] for p in range(npair))
        ys, s_new = ssd_pairs(tuple(xbc_ref[:, p * LANE:(p + 1) * LANE] for p in range(npair)),
                              tuple(xbc_ref[:, _SSD_B + g * LANE:_SSD_B + (g + 1) * LANE] for g in range(2)),
                              tuple(xbc_ref[:, _SSD_C + g * LANE:_SSD_C + (g + 1) * LANE] for g in range(2)),
                              sm_ref[...], dtb_ref[...], alog_ref[...], dsk_ref[...], s_in)
        for p in range(npair):
            sin_ref[0, p] = s_in[p]
            y_ref[:, p * LANE:(p + 1) * LANE] = ys[p]
            st_ref[p] = s_new[p]

    par = pl.BlockSpec((1, LANE), lambda ci: (0, 0))
    return pl.pallas_call(
        body, name=name,
        out_shape=[jax.ShapeDtypeStruct((t, SSM_D_INNER), F32),
                   jax.ShapeDtypeStruct((nc, npair, LANE, LANE), F32)],
        grid=(nc,),
        in_specs=[pl.BlockSpec((SSD_CHUNK, SSM_CONV_DIM), lambda ci: (ci, 0)),
                  pl.BlockSpec((SSD_CHUNK, LANE), lambda ci: (ci, small_blk)),
                  par, par, par],
        out_specs=[pl.BlockSpec((SSD_CHUNK, SSM_D_INNER), lambda ci: (ci, 0)),
                   pl.BlockSpec((1, npair, LANE, LANE), lambda ci: (ci, 0, 0, 0))],
        scratch_shapes=[pltpu.VMEM((npair, LANE, LANE), F32)],
        compiler_params=_cparams(("arbitrary",)),
    )(xbc, proj, dtb, alog, dsk)


def ssd_scan_bwd(name, xbc, proj, dtb, alog, dsk, s_in, dy):
    t = xbc.shape[0]
    nc, npair = t // SSD_CHUNK, SSM_HEADS // 2
    small_blk = AL_SMALL // LANE

    def body(xbc_ref, sm_ref, dtb_ref, alog_ref, dsk_ref, sin_ref, dy_ref,
             dxbc_ref, dsm_ref, ddtb_ref, dalog_ref, ddsk_ref, dst_ref):
        ci = pl.program_id(0)

        @pl.when(ci == 0)
        def _():
            dst_ref[...] = jnp.zeros_like(dst_ref)

        _, vjp = jax.vjp(ssd_pairs, tuple(xbc_ref[:, p * LANE:(p + 1) * LANE] for p in range(npair)),
                         tuple(xbc_ref[:, _SSD_B + g * LANE:_SSD_B + (g + 1) * LANE] for g in range(2)),
                         tuple(xbc_ref[:, _SSD_C + g * LANE:_SSD_C + (g + 1) * LANE] for g in range(2)),
                         sm_ref[...], dtb_ref[...], alog_ref[...], dsk_ref[...],
                         tuple(sin_ref[0, p] for p in range(npair)))
        dxs, dbms, dcms, dsm, ddtb, dalog, ddsk, dsts = vjp(
            (tuple(dy_ref[:, p * LANE:(p + 1) * LANE] for p in range(npair)),
             tuple(dst_ref[p] for p in range(npair))))
        for p in range(npair):
            dxbc_ref[:, p * LANE:(p + 1) * LANE] = dxs[p]
            dst_ref[p] = dsts[p]
        for g in range(2):
            dxbc_ref[:, _SSD_B + g * LANE:_SSD_B + (g + 1) * LANE] = dbms[g]
            dxbc_ref[:, _SSD_C + g * LANE:_SSD_C + (g + 1) * LANE] = dcms[g]
        dsm_ref[...] = dsm
        _acc(ddtb_ref, ddtb, ci == 0)
        _acc(dalog_ref, dalog, ci == 0)
        _acc(ddsk_ref, ddsk, ci == 0)

    par = pl.BlockSpec((1, LANE), lambda ci: (0, 0))
    rev = lambda ci: nc - 1 - ci
    return pl.pallas_call(
        body, name=name,
        out_shape=[jax.ShapeDtypeStruct((t, SSM_CONV_DIM), F32),
                   jax.ShapeDtypeStruct((t, LANE), F32),
                   jax.ShapeDtypeStruct((1, LANE), F32), jax.ShapeDtypeStruct((1, LANE), F32),
                   jax.ShapeDtypeStruct((1, LANE), F32)],
        grid=(nc,),
        in_specs=[pl.BlockSpec((SSD_CHUNK, SSM_CONV_DIM), lambda ci: (rev(ci), 0)),
                  pl.BlockSpec((SSD_CHUNK, LANE), lambda ci: (rev(ci), small_blk)),
                  par, par, par,
                  pl.BlockSpec((1, npair, LANE, LANE), lambda ci: (rev(ci), 0, 0, 0)),
                  pl.BlockSpec((SSD_CHUNK, SSM_D_INNER), lambda ci: (rev(ci), 0))],
        out_specs=[pl.BlockSpec((SSD_CHUNK, SSM_CONV_DIM), lambda ci: (rev(ci), 0)),
                   pl.BlockSpec((SSD_CHUNK, LANE), lambda ci: (rev(ci), 0)),
                   par, par, par],
        scratch_shapes=[pltpu.VMEM((npair, LANE, LANE), F32)],
        compiler_params=_cparams(("arbitrary",)),
    )(xbc, proj, dtb, alog, dsk, s_in, dy)


def gdn_scan_fwd(name, qkv, proj, alog, dtb):
    t = qkv.shape[0]
    nc, nh = t // CHUNK, GDN_HEADS
    small_blk = AL_SMALL // LANE

    def body(qkv_ref, sm_ref, alog_ref, dtb_ref, o_ref, sin_ref, x_ref, st_ref):
        ci = pl.program_id(0)

        @pl.when(ci == 0)
        def _():
            st_ref[...] = jnp.zeros_like(st_ref)

        s_in = tuple(st_ref[h] for h in range(nh))
        os, s_new, xinvs = gdn_heads(
            tuple(qkv_ref[:, h * LANE:(h + 1) * LANE] for h in range(nh)),
            tuple(qkv_ref[:, _GDN_K + h * LANE:_GDN_K + (h + 1) * LANE] for h in range(nh)),
            tuple(qkv_ref[:, _GDN_V + h * LANE:_GDN_V + (h + 1) * LANE] for h in range(nh)),
            sm_ref[...], alog_ref[...], dtb_ref[...], s_in)
        for h in range(nh):
            sin_ref[0, h] = s_in[h]
            o_ref[:, h * LANE:(h + 1) * LANE] = os[h]
            x_ref[0, h] = xinvs[h]
            st_ref[h] = s_new[h]

    par = pl.BlockSpec((1, LANE), lambda ci: (0, 0))
    return pl.pallas_call(
        body, name=name,
        out_shape=[jax.ShapeDtypeStruct((t, GDN_HEADS * GDN_HEAD), F32),
                   jax.ShapeDtypeStruct((nc, nh, LANE, LANE), F32),
                   jax.ShapeDtypeStruct((nc, nh, CHUNK, CHUNK), F32)],
        grid=(nc,),
        in_specs=[pl.BlockSpec((CHUNK, GDN_QKV_DIM), lambda ci: (ci, 0)),
                  pl.BlockSpec((CHUNK, LANE), lambda ci: (ci, small_blk)),
                  par, par],
        out_specs=[pl.BlockSpec((CHUNK, GDN_HEADS * GDN_HEAD), lambda ci: (ci, 0)),
                   pl.BlockSpec((1, nh, LANE, LANE), lambda ci: (ci, 0, 0, 0)),
                   pl.BlockSpec((1, nh, CHUNK, CHUNK), lambda ci: (ci, 0, 0, 0))],
        scratch_shapes=[pltpu.VMEM((nh, LANE, LANE), F32)],
        compiler_params=_cparams(("arbitrary",)),
    )(qkv, proj, alog, dtb)


def gdn_scan_bwd(name, qkv, proj, alog, dtb, s_in, xinv, do, dsm_in):
    t = qkv.shape[0]
    nc, nh = t // CHUNK, GDN_HEADS
    small_blk = AL_SMALL // LANE

    def body(qkv_ref, sm_ref, alog_ref, dtb_ref, sin_ref, x_ref, do_ref, dsmi_ref,
             dqkv_ref, dsm_ref, dalog_ref, ddtb_ref, dst_ref):
        ci = pl.program_id(0)

        @pl.when(ci == 0)
        def _():
            dst_ref[...] = jnp.zeros_like(dst_ref)

        xis = tuple(x_ref[0, h] for h in range(nh))

        def fn(qs, ks, vs, sm, alog_, dtb_, sts):
            os, s_new, _ = gdn_heads(qs, ks, vs, sm, alog_, dtb_, sts, xinvs=xis)
            return os, s_new

        _, vjp = jax.vjp(fn, tuple(qkv_ref[:, h * LANE:(h + 1) * LANE] for h in range(nh)),
                         tuple(qkv_ref[:, _GDN_K + h * LANE:_GDN_K + (h + 1) * LANE] for h in range(nh)),
                         tuple(qkv_ref[:, _GDN_V + h * LANE:_GDN_V + (h + 1) * LANE] for h in range(nh)),
                         sm_ref[...], alog_ref[...], dtb_ref[...], tuple(sin_ref[0, h] for h in range(nh)))
        dqs, dks, dvs, dsm, dalog, ddtb, dsts = vjp(
            (tuple(do_ref[:, h * LANE:(h + 1) * LANE] for h in range(nh)), tuple(dst_ref[h] for h in range(nh))))
        for h in range(nh):
            dqkv_ref[:, h * LANE:(h + 1) * LANE] = dqs[h]
            dqkv_ref[:, _GDN_K + h * LANE:_GDN_K + (h + 1) * LANE] = dks[h]
            dqkv_ref[:, _GDN_V + h * LANE:_GDN_V + (h + 1) * LANE] = dvs[h]
            dst_ref[h] = dsts[h]
        dsm_ref[...] = dsmi_ref[...] + dsm
        _acc(dalog_ref, dalog, ci == 0)
        _acc(ddtb_ref, ddtb, ci == 0)

    par = pl.BlockSpec((1, LANE), lambda ci: (0, 0))
    rev = lambda ci: nc - 1 - ci
    return pl.pallas_call(
        body, name=name,
        out_shape=[jax.ShapeDtypeStruct((t, GDN_QKV_DIM), F32), jax.ShapeDtypeStruct((t, LANE), F32),
                   jax.ShapeDtypeStruct((1, LANE), F32), jax.ShapeDtypeStruct((1, LANE), F32)],
        grid=(nc,),
        in_specs=[pl.BlockSpec((CHUNK, GDN_QKV_DIM), lambda ci: (rev(ci), 0)),
                  pl.BlockSpec((CHUNK, LANE), lambda ci: (rev(ci), small_blk)),
                  par, par,
                  pl.BlockSpec((1, nh, LANE, LANE), lambda ci: (rev(ci), 0, 0, 0)),
                  pl.BlockSpec((1, nh, CHUNK, CHUNK), lambda ci: (rev(ci), 0, 0, 0)),
                  pl.BlockSpec((CHUNK, GDN_HEADS * GDN_HEAD), lambda ci: (rev(ci), 0)),
                  pl.BlockSpec((CHUNK, LANE), lambda ci: (rev(ci), 0))],
        out_specs=[pl.BlockSpec((CHUNK, GDN_QKV_DIM), lambda ci: (rev(ci), 0)),
                   pl.BlockSpec((CHUNK, LANE), lambda ci: (rev(ci), 0)),
                   par, par],
        scratch_shapes=[pltpu.VMEM((nh, LANE, LANE), F32)],
        compiler_params=_cparams(("arbitrary",)),
    )(qkv, proj, alog, dtb, s_in, xinv, do, dsm_in)


def _row(arr, w, c0=0, moves=False):
    return (arr, "row", w, c0, moves)


def _par(arr, w, c0=0, moves=False):
    return (arr, "par", w, c0, moves)


def matmul_add(name, a, b, res):
    (m, k), (_, n) = a.shape, b.shape
    tm, tn, tk = _matmul_tiles(m, n, n, k, a.dtype.itemsize, b.dtype.itemsize, 4 + res.dtype.itemsize)
    nk = k // tk

    def body_acc(a_ref, b_ref, r_ref, o_ref, acc_ref):
        kk = pl.program_id(2)

        @pl.when(kk == 0)
        def _():
            acc_ref[...] = r_ref[...]

        acc_ref[...] += _dot(a_ref[...], b_ref[...])

        @pl.when(kk == nk - 1)
        def _():
            o_ref[...] = acc_ref[...]

    def body_one(a_ref, b_ref, r_ref, o_ref):
        o_ref[...] = r_ref[...] + _dot(a_ref[...], b_ref[...])

    body = body_one if nk == 1 else body_acc
    return pl.pallas_call(
        body, name=name,
        out_shape=jax.ShapeDtypeStruct((m, n), F32),
        grid=(m // tm, n // tn, nk),
        in_specs=[pl.BlockSpec((tm, tk), lambda i, j, kk: (i, kk)),
                  pl.BlockSpec((tk, tn), lambda i, j, kk: (kk, j)),
                  pl.BlockSpec((tm, tn), lambda i, j, kk: (i, j))],
        out_specs=pl.BlockSpec((tm, tn), lambda i, j, kk: (i, j)),
        scratch_shapes=[] if nk == 1 else [pltpu.VMEM((tm, tn), F32)],
        compiler_params=_cparams(("parallel", "parallel", "arbitrary")),
    )(a, b, res)


def layer_fwd(l, x, w):
    t = x.shape[0]
    rt = min(256, t)
    s = {"x": x}
    s["h"] = rowwise_fwd(f"norm_mix_l{l}", f_rmsnorm, t, rt, 1,
                         [_row(x, D_MODEL), _par(w["norm_mix_w"], D_MODEL)], [(D_MODEL, BF16)])[0]
    s["proj"] = matmul(f"in_proj_l{l}", s["h"], w["w_in"], "nn")
    s["xbc"] = conv_fwd(f"ssm_conv_l{l}", s["proj"], AL_XBC, SSM_CONV_DIM, w["ssm_conv_w"], w["ssm_conv_b"],
                        tile=min(512, t))
    s["qkv"] = conv_fwd(f"gdn_conv_l{l}", s["proj"], AL_QKV, GDN_QKV_DIM, w["gdn_conv_w"], w["gdn_conv_b"],
                        tile=min(512, t))
    s["y_scan"], s["ssd_sin"] = ssd_scan_fwd(f"ssd_scan_l{l}", s["xbc"], s["proj"], w["ssm_dt_bias"],
                                             w["ssm_a_log"], w["ssm_d"])
    s["o_scan"], s["gdn_sin"], s["gdn_x"] = gdn_scan_fwd(f"gdn_scan_l{l}", s["qkv"], s["proj"],
                                                         w["gdn_a_log"], w["gdn_dt_bias"])
    s["y_ssm"] = rowwise_fwd(f"ssd_post_l{l}", f_ssd_post, t, rt, 2,
                             [_row(s["y_scan"], 512, 0, True), _row(s["proj"], 512, AL_Z // 512, True),
                              _par(w["ssm_norm_w"], 512, 0, True)], [(512, BF16)])[0]
    s["y_gdn"] = rowwise_fwd(f"gdn_post_l{l}", f_gdn_post, t, rt, GDN_HEADS,
                             [_row(s["o_scan"], LANE, 0, True), _row(s["proj"], LANE, AL_GZ // LANE, True),
                              _par(w["gdn_norm_w"], LANE)], [(LANE, BF16)])[0]
    if "late_weights" in w:
        late = w["late_weights"](s["y_gdn"])
        w = {k: v for k, v in {**w, **late}.items() if k != "late_weights"}
    s["w"] = w
    s["p1"] = matmul(f"proj_ssm_l{l}", s["y_ssm"], w["w_proj_ssm"], "nn")
    s["p2"] = matmul(f"proj_gdn_l{l}", s["y_gdn"], w["w_proj_gdn"], "nn")
    s["merged"] = rowwise_fwd(f"merge_l{l}", f_merge, t, rt, 2,
                              [_row(s["proj"], 512, AL_GS // 512, True), _row(s["p1"], 512, 0, True),
                               _row(s["proj"], 512, AL_GG // 512, True), _row(s["p2"], 512, 0, True)],
                              [(512, BF16)])[0]
    s["x1"] = matmul_add(f"out_proj_l{l}", s["merged"], w["w_out"], x)
    s["h2"] = rowwise_fwd(f"norm_ffn_l{l}", f_rmsnorm, t, rt, 1,
                          [_row(s["x1"], D_MODEL), _par(w["norm_ffn_w"], D_MODEL)], [(D_MODEL, BF16)])[0]
    s["gu"] = matmul(f"ffn_in_l{l}", s["h2"], w["w_ffn_in"], "nn")
    s["act"] = rowwise_fwd(f"swiglu_l{l}", f_swiglu, t, rt, FFN_HIDDEN // 256,
                           [_row(s["gu"], 256, 0, True), _row(s["gu"], 256, FFN_HIDDEN // 256, True)],
                           [(256, BF16)])[0]
    x2 = matmul_add(f"ffn_down_l{l}", s["act"], w["w_ffn_down"], s["x1"])
    return x2, s


IN_SHARD = IN_DIM // 4
IN_SHARD_PAD = 2304


def _aligned_to_shards(g):
    orig = jnp.concatenate([g[:, 0:2560], g[:, AL_SMALL:AL_SMALL + 16], g[:, 2560:6656],
                            g[:, AL_SMALL + 16:AL_SMALL + 32], g[:, 6656:8704]], axis=1)
    return jnp.stack([jnp.pad(orig[:, j * IN_SHARD:(j + 1) * IN_SHARD], ((0, 0), (0, IN_SHARD_PAD - IN_SHARD)))
                      for j in range(N_CHIPS)])


def layer_bwd(l, dx2, w, s):
    t = dx2.shape[0]
    rt = min(256, t)
    ct = min(512, t)
    g = {}
    dact = matmul(f"ffn_down_dx_l{l}", dx2, w["w_ffn_down"], "nt")
    g["w_ffn_down"] = matmul(f"ffn_down_dw_l{l}", s["act"], dx2, "tn")
    nf = FFN_HIDDEN // 256
    dgate, dup = rowwise_bwd(f"swiglu_bwd_l{l}", f_swiglu, t, rt, nf,
                             [_row(s["gu"], 256, 0, True), _row(s["gu"], 256, nf, True)], [True, True],
                             [_row(dact, 256, 0, True)], row_dtypes={0: BF16, 1: BF16})
    dgu = jnp.concatenate([dgate, dup], axis=1)
    dh2 = matmul(f"ffn_in_dx_l{l}", dgu, w["w_ffn_in"], "nt")
    g["w_ffn_in"] = matmul(f"ffn_in_dw_l{l}", s["h2"], dgu, "tn", chip_major=True)
    dx1, g["norm_ffn_w"] = rowwise_bwd(f"norm_ffn_bwd_l{l}", f_rmsnorm, t, rt, 1,
                                       [_row(s["x1"], D_MODEL), _par(w["norm_ffn_w"], D_MODEL)], [True, True],
                                       [_row(dh2, D_MODEL)], addends={0: _row(dx2, D_MODEL)})
    dmerged = matmul(f"out_proj_dx_l{l}", dx1, w["w_out"], "nt")
    g["w_out"] = matmul(f"out_proj_dw_l{l}", s["merged"], dx1, "tn")
    dgs, dp1, dgg, dp2 = rowwise_bwd(
        f"merge_bwd_l{l}", f_merge, t, rt, 2,
        [_row(s["proj"], 512, AL_GS // 512, True), _row(s["p1"], 512, 0, True),
         _row(s["proj"], 512, AL_GG // 512, True), _row(s["p2"], 512, 0, True)], [True] * 4,
        [_row(dmerged, 512, 0, True)], row_dtypes={0: BF16, 1: BF16, 2: BF16, 3: BF16})
    dy_ssm = matmul(f"proj_ssm_dx_l{l}", dp1, w["w_proj_ssm"], "nt")
    g["w_proj_ssm"] = matmul(f"proj_ssm_dw_l{l}", s["y_ssm"], dp1, "tn")
    dy_gdn = matmul(f"proj_gdn_dx_l{l}", dp2, w["w_proj_gdn"], "nt")
    g["w_proj_gdn"] = matmul(f"proj_gdn_dw_l{l}", s["y_gdn"], dp2, "tn")
    dy_scan, dz, g["ssm_norm_w"] = rowwise_bwd(
        f"ssd_post_bwd_l{l}", f_ssd_post, t, rt, 2,
        [_row(s["y_scan"], 512, 0, True), _row(s["proj"], 512, AL_Z // 512, True),
         _par(w["ssm_norm_w"], 512, 0, True)], [True] * 3, [_row(dy_ssm, 512, 0, True)], row_dtypes={1: BF16})
    dxbc_act, dsm, g["ssm_dt_bias"], g["ssm_a_log"], g["ssm_d"] = ssd_scan_bwd(
        f"ssd_scan_bwd_l{l}", s["xbc"], s["proj"], w["ssm_dt_bias"], w["ssm_a_log"], w["ssm_d"],
        s["ssd_sin"], dy_scan)
    dxbc, g["ssm_conv_w"], g["ssm_conv_b"] = conv_bwd(
        f"ssm_conv_bwd_l{l}", s["proj"], AL_XBC, SSM_CONV_DIM, w["ssm_conv_w"], w["ssm_conv_b"], dxbc_act, tile=ct)
    do_scan, dgz, g["gdn_norm_w"] = rowwise_bwd(
        f"gdn_post_bwd_l{l}", f_gdn_post, t, rt, GDN_HEADS,
        [_row(s["o_scan"], LANE, 0, True), _row(s["proj"], LANE, AL_GZ // LANE, True),
         _par(w["gdn_norm_w"], LANE)], [True] * 3, [_row(dy_gdn, LANE, 0, True)], row_dtypes={1: BF16})
    dqkv_act, dsm, g["gdn_a_log"], g["gdn_dt_bias"] = gdn_scan_bwd(
        f"gdn_scan_bwd_l{l}", s["qkv"], s["proj"], w["gdn_a_log"], w["gdn_dt_bias"], s["gdn_sin"],
        s["gdn_x"], do_scan, dsm)
    dqkv, g["gdn_conv_w"], _ = conv_bwd(
        f"gdn_conv_bwd_l{l}", s["proj"], AL_QKV, GDN_QKV_DIM, w["gdn_conv_w"], w["gdn_conv_b"], dqkv_act, tile=ct)
    dproj = jnp.concatenate([dz, dxbc, dqkv, dgz, dgs, dgg, dsm.astype(BF16),
                             jnp.zeros((t, AL_DIM - AL_SMALL - LANE), BF16)], axis=1)
    dh = matmul(f"in_proj_dx_l{l}", dproj, w["w_in"], "nt")
    g["w_in"] = matmul(f"in_proj_dw_l{l}", s["h"], dproj, "tn")
    dx0, g["norm_mix_w"] = rowwise_bwd(f"norm_mix_bwd_l{l}", f_rmsnorm, t, rt, 1,
                                       [_row(s["x"], D_MODEL), _par(w["norm_mix_w"], D_MODEL)], [True, True],
                                       [_row(dh, D_MODEL)], addends={0: _row(dx1, D_MODEL)})
    return dx0, g


def _align_w_in(w):
    pad = jnp.zeros((w.shape[0], AL_DIM - AL_SMALL - 32), w.dtype)
    return jnp.concatenate([w[:, 0:2560], w[:, 2576:6672], w[:, 6688:8736],
                            w[:, 2560:2576], w[:, 6672:6688], pad], axis=1)


def _pad_lane(v, at=0):
    return jnp.pad(v[None], ((0, 0), (at, LANE - at - v.shape[0])))


def local_step(x, target, full, gathered_of=None, on_layer_grads=None):
    if gathered_of is None:
        gathered_of = lambda l, after: {n: full[n][l] for n, _ in SHARDED}
    ws, saved = [], []
    h = x
    for l in range(DEPTH):
        gw = gathered_of(l, h)
        ws.append({
            "norm_mix_w": full["norm_mix_w"][l][None], "w_in": _align_w_in(gw["w_in"]),
            "ssm_conv_w": gw["ssm_conv_w"], "ssm_conv_b": full["ssm_conv_b"][l][None],
            "ssm_dt_bias": _pad_lane(full["ssm_dt_bias"][l]), "ssm_a_log": _pad_lane(full["ssm_a_log"][l]),
            "ssm_d": _pad_lane(full["ssm_d"][l]), "ssm_norm_w": full["ssm_norm_w"][l][None],
            "gdn_conv_w": gw["gdn_conv_w"], "gdn_conv_b": jnp.zeros((1, GDN_QKV_DIM), F32),
            "gdn_a_log": _pad_lane(full["gdn_a_log"][l], SM_A),
            "gdn_dt_bias": _pad_lane(full["gdn_dt_bias"][l], SM_A),
            "gdn_norm_w": full["gdn_norm_w"][l][None],
            "norm_ffn_w": full["norm_ffn_w"][l][None],
            **{n: gw[n] for n in ("w_proj_ssm", "w_proj_gdn", "w_out", "w_ffn_in", "w_ffn_down", "late_weights")
               if n in gw},
        })
        h, s = layer_fwd(l, h, ws[l])
        ws[l] = s.pop("w")
        saved.append(s)
    loss, dx, g_final = final_loss("final_loss", h, target, full["final_norm_w"][None], tile=min(256, x.shape[0]))
    per_layer = [None] * DEPTH
    matmul_grads = [None] * DEPTH
    for l in reversed(range(DEPTH)):
        dx, per_layer[l] = layer_bwd(l, dx, ws[l], saved[l])
        matmul_grads[l] = {n: per_layer[l].pop(n) for n, _ in BIG}
        if on_layer_grads is not None:
            dx = on_layer_grads(l, matmul_grads[l], dx)
    grads = {"final_norm_w": g_final[0]}
    if on_layer_grads is None:
        grads.update({n: jnp.stack([matmul_grads[l][n] for l in range(DEPTH)]) for n, _ in BIG})
    for name in per_layer[0]:
        rows = []
        for l in range(DEPTH):
            gl = per_layer[l][name]
            if name in ("ssm_dt_bias", "ssm_a_log", "ssm_d"):
                gl = gl[0, :SSM_HEADS]
            elif name in ("gdn_a_log", "gdn_dt_bias"):
                gl = gl[0, SM_A:SM_A + GDN_HEADS]
            elif name in ("norm_mix_w", "ssm_conv_b", "ssm_norm_w", "gdn_norm_w", "norm_ffn_w"):
                gl = gl[0]
            rows.append(gl)
        grads[name] = jnp.stack(rows)
    return loss, dx, grads


MESH = pl.DeviceIdType.MESH
HBM = pl.BlockSpec(memory_space=pltpu.HBM)
N_DEV = 8


def _pos():
    return lax.axis_index("x"), lax.axis_index("y"), lax.axis_index("c")


def _rcopy(src, dst, send_sem, recv_sem, dev):
    return pltpu.make_async_remote_copy(src_ref=src, dst_ref=dst, send_sem=send_sem, recv_sem=recv_sem,
                                        device_id=dev, device_id_type=MESH)


RELATIONS = (2, 1, 3)


def _related_chip(x, y, mask):
    return (1 - x if mask & 2 else x, 1 - y if mask & 1 else y)


def weights_gather(name, bufs):
    n = len(bufs)

    def body(*refs):
        outs, send_sems, recv_sems = refs[n:2 * n], refs[2 * n], refs[2 * n + 1]
        x, y, c = _pos()
        sib = (x, y, 1 - c)
        sends = []
        for i, a in enumerate(outs):
            for k, m in enumerate(RELATIONS):
                px, py = _related_chip(x, y, m)
                cp = _rcopy(a.at[0, c], a.at[m, c], send_sems.at[6 * i + k], recv_sems.at[6 * i + k], (px, py, c))
                cp.start()
                sends.append(cp)
        for i, a in enumerate(outs):
            for k, m in enumerate(RELATIONS):
                px, py = _related_chip(x, y, m)
                _rcopy(a.at[0, c], a.at[m, c], send_sems.at[6 * i + k], recv_sems.at[6 * i + k],
                       (px, py, c)).wait_recv()
                fw = _rcopy(a.at[m, c], a.at[m, c], send_sems.at[6 * i + 3 + k], recv_sems.at[6 * i + 3 + k], sib)
                fw.start()
                sends.append(fw)
        for i, a in enumerate(outs):
            for k, m in enumerate(RELATIONS):
                _rcopy(a.at[m, 1 - c], a.at[m, 1 - c], send_sems.at[6 * i + 3 + k], recv_sems.at[6 * i + 3 + k],
                       sib).wait_recv()
        for cp in sends:
            cp.wait_send()

    return pl.pallas_call(
        body, name=name, out_shape=[jax.ShapeDtypeStruct(b.shape, b.dtype) for b in bufs],
        in_specs=[HBM] * n, out_specs=[HBM] * n,
        input_output_aliases={i: i for i in range(n)},
        scratch_shapes=[pltpu.SemaphoreType.DMA((6 * n,)), pltpu.SemaphoreType.DMA((6 * n,))],
    )(*bufs)


SEM = pl.BlockSpec(memory_space=pltpu.SEMAPHORE)
DATAFLOW = pltpu.SideEffectType.DATAFLOW_SIDE_EFFECTING


def gather_start(name, bufs, after):
    n = len(bufs)

    def body(*refs):
        ins = refs[:n]
        send_sems, recv_sems, token = refs[n + 1], refs[n + 2], refs[2 * n + 3]
        x, y, c = _pos()
        for i, a in enumerate(ins):
            for k, m in enumerate(RELATIONS):
                px, py = _related_chip(x, y, m)
                _rcopy(a.at[0, c], a.at[m, c], send_sems.at[3 * i + k], recv_sems.at[3 * i + k], (px, py, c)).start()
        token[...] = jnp.zeros_like(token)

    res = pl.pallas_call(
        body, name=name,
        out_shape=(pltpu.SemaphoreType.DMA((3 * n,)), pltpu.SemaphoreType.DMA((3 * n,)),
                   *[pltpu.HBM(b.shape, b.dtype) for b in bufs], jax.ShapeDtypeStruct((8, LANE), F32)),
        in_specs=[HBM] * n + [pl.BlockSpec(memory_space=pl.ANY)],
        out_specs=(SEM, SEM, *[HBM] * n, pl.BlockSpec(memory_space=pltpu.VMEM)),
        input_output_aliases={i: 2 + i for i in range(n)},
        compiler_params=pltpu.CompilerParams(has_side_effects=DATAFLOW),
    )(*[pltpu.with_memory_space_constraint(b, pltpu.HBM) for b in bufs], after)
    return res[0], res[1], list(res[2:2 + n]), res[2 + n]


def gather_wait(name, send_sems, recv_sems, bufs, after):
    n = len(bufs)

    def body(*refs):
        ins, ssem, rsem = refs[:n], refs[n], refs[n + 1]
        x, y, c = _pos()
        for i, a in enumerate(ins):
            for k, m in enumerate(RELATIONS):
                px, py = _related_chip(x, y, m)
                cp = _rcopy(a.at[0, c], a.at[m, c], ssem.at[3 * i + k], rsem.at[3 * i + k], (px, py, c))
                cp.wait_send()
                cp.wait_recv()

    return pl.pallas_call(
        body, name=name,
        out_shape=[pltpu.HBM(b.shape, b.dtype) for b in bufs],
        in_specs=[HBM] * n + [SEM, SEM, pl.BlockSpec(memory_space=pl.ANY)],
        out_specs=[HBM] * n,
        input_output_aliases={i: i for i in range(n)},
        compiler_params=pltpu.CompilerParams(has_side_effects=DATAFLOW),
    )(*bufs, send_sems, recv_sems, after)


def weights_forward(name, bufs):
    n = len(bufs)

    def body(*refs):
        outs, send_sems, recv_sems = refs[n:2 * n], refs[2 * n], refs[2 * n + 1]
        x, y, c = _pos()
        sib = (x, y, 1 - c)
        sends = []
        for i, a in enumerate(outs):
            for k, m in enumerate(RELATIONS):
                fw = _rcopy(a.at[m, c], a.at[m, c], send_sems.at[3 * i + k], recv_sems.at[3 * i + k], sib)
                fw.start()
                sends.append(fw)
        for i, a in enumerate(outs):
            for k, m in enumerate(RELATIONS):
                _rcopy(a.at[m, 1 - c], a.at[m, 1 - c], send_sems.at[3 * i + k], recv_sems.at[3 * i + k],
                       sib).wait_recv()
        for cp in sends:
            cp.wait_send()

    return pl.pallas_call(
        body, name=name, out_shape=[jax.ShapeDtypeStruct(b.shape, b.dtype) for b in bufs],
        in_specs=[HBM] * n, out_specs=[HBM] * n,
        input_output_aliases={i: i for i in range(n)},
        scratch_shapes=[pltpu.SemaphoreType.DMA((3 * n,)), pltpu.SemaphoreType.DMA((3 * n,))],
    )(*bufs)


def pair_swap(name, gs):
    n = len(gs)
    offs = [0]
    for g in gs:
        offs.append(offs[-1] + g.shape[0])

    def body(*refs):
        srcs, outs, send_sems, recv_sems = refs[:n], refs[n:2 * n], refs[2 * n], refs[2 * n + 1]
        x, y, c = _pos()
        cps = [_rcopy(s.at[j, 1 - c], o.at[j], send_sems.at[offs[i] + j], recv_sems.at[offs[i] + j], (x, y, 1 - c))
               for i, (s, o) in enumerate(zip(srcs, outs)) for j in range(s.shape[0])]
        for cp in cps:
            cp.start()
        for cp in cps:
            cp.wait()

    return pl.pallas_call(
        body, name=name, out_shape=[jax.ShapeDtypeStruct(g.shape[:1] + g.shape[2:], g.dtype) for g in gs],
        in_specs=[HBM] * n, out_specs=[HBM] * n,
        scratch_shapes=[pltpu.SemaphoreType.DMA((offs[-1],)), pltpu.SemaphoreType.DMA((offs[-1],))],
    )(*gs)


def scatter_start(name, ss, after):
    n = len(ss)
    lands = [lax.empty((3,) + s.shape[1:], s.dtype) for s in ss]

    def body(*refs):
        srcs, dsts = refs[:n], refs[n:2 * n]
        send_sems, recv_sems, token = refs[2 * n + 1], refs[2 * n + 2], refs[4 * n + 3]
        x, y, c = _pos()
        for i, (s, o) in enumerate(zip(srcs, dsts)):
            for k, m in enumerate(RELATIONS):
                px, py = _related_chip(x, y, m)
                _rcopy(s.at[2 * px + py], o.at[k], send_sems.at[3 * i + k], recv_sems.at[3 * i + k],
                       (px, py, c)).start()
        token[...] = jnp.zeros_like(token)

    both = list(ss) + lands
    res = pl.pallas_call(
        body, name=name,
        out_shape=(pltpu.SemaphoreType.DMA((3 * n,)), pltpu.SemaphoreType.DMA((3 * n,)),
                   *[pltpu.HBM(b.shape, b.dtype) for b in both], jax.ShapeDtypeStruct((8, LANE), F32)),
        in_specs=[HBM] * (2 * n) + [pl.BlockSpec(memory_space=pl.ANY)],
        out_specs=(SEM, SEM, *[HBM] * (2 * n), pl.BlockSpec(memory_space=pltpu.VMEM)),
        input_output_aliases={i: 2 + i for i in range(2 * n)},
        compiler_params=pltpu.CompilerParams(has_side_effects=DATAFLOW),
    )(*[pltpu.with_memory_space_constraint(b, pltpu.HBM) for b in both], after)
    return res[0], res[1], list(res[2:2 + n]), list(res[2 + n:2 + 2 * n]), res[2 + 2 * n]


def scatter_wait(name, send_sems, recv_sems, ss, lands, after):
    n = len(ss)

    def body(*refs):
        srcs, dsts, ssem, rsem = refs[:n], refs[n:2 * n], refs[2 * n], refs[2 * n + 1]
        x, y, c = _pos()
        for i, (s, o) in enumerate(zip(srcs, dsts)):
            for k, m in enumerate(RELATIONS):
                px, py = _related_chip(x, y, m)
                cp = _rcopy(s.at[2 * px + py], o.at[k], ssem.at[3 * i + k], rsem.at[3 * i + k], (px, py, c))
                cp.wait_send()
                cp.wait_recv()

    both = list(ss) + list(lands)
    res = pl.pallas_call(
        body, name=name,
        out_shape=[pltpu.HBM(b.shape, b.dtype) for b in both],
        in_specs=[HBM] * (2 * n) + [SEM, SEM, pl.BlockSpec(memory_space=pl.ANY)],
        out_specs=[HBM] * (2 * n),
        input_output_aliases={i: i for i in range(2 * n)},
        compiler_params=pltpu.CompilerParams(has_side_effects=DATAFLOW),
    )(*both, send_sems, recv_sems, after)
    return list(res[:n]), list(res[n:])


def pair_share(name, bufs):
    n = len(bufs)

    def body(*refs):
        outs, send_sems, recv_sems = refs[n:2 * n], refs[2 * n], refs[2 * n + 1]
        x, y, c = _pos()
        sends = []
        for i, o in enumerate(outs):
            cp = _rcopy(o.at[c], o.at[c], send_sems.at[i], recv_sems.at[i], (x, y, 1 - c))
            cp.start()
            sends.append(cp)
        for i, o in enumerate(outs):
            _rcopy(o.at[1 - c], o.at[1 - c], send_sems.at[i], recv_sems.at[i], (x, y, 1 - c)).wait_recv()
        for cp in sends:
            cp.wait_send()

    return pl.pallas_call(
        body, name=name, out_shape=[jax.ShapeDtypeStruct(b.shape, b.dtype) for b in bufs],
        in_specs=[HBM] * n, out_specs=[HBM] * n,
        input_output_aliases={i: i for i in range(n)},
        scratch_shapes=[pltpu.SemaphoreType.DMA((n,)), pltpu.SemaphoreType.DMA((n,))],
    )(*bufs)


def all_allgather(name, buf):
    r, cd = buf.shape

    def body(src, out, send_sems, recv_sems, lsem):
        x, y, c = _pos()
        me = 4 * x + 2 * y + c
        local = pltpu.make_async_copy(src, out.at[me], lsem)
        local.start()

        def peer(mask):
            px = 1 - x if mask & 4 else x
            py = 1 - y if mask & 2 else y
            pc = 1 - c if mask & 1 else c
            return px, py, pc

        sends = []
        for mask in range(1, N_DEV):
            cp = _rcopy(src, out.at[me], send_sems.at[mask - 1], recv_sems.at[mask - 1], peer(mask))
            cp.start()
            sends.append(cp)
        for mask in range(1, N_DEV):
            px, py, pc = peer(mask)
            _rcopy(src, out.at[4 * px + 2 * py + pc], send_sems.at[mask - 1], recv_sems.at[mask - 1],
                   (px, py, pc)).wait_recv()
        for cp in sends:
            cp.wait_send()
        local.wait()

    return pl.pallas_call(
        body, name=name, out_shape=jax.ShapeDtypeStruct((N_DEV, r, cd), buf.dtype),
        in_specs=[HBM], out_specs=HBM,
        scratch_shapes=[pltpu.SemaphoreType.DMA((N_DEV - 1,)), pltpu.SemaphoreType.DMA((N_DEV - 1,)),
                        pltpu.SemaphoreType.DMA(())],
    )(buf)


ELEMENTWISE_BLOCK_BYTES = 2 << 20


def _row_block(rows, cols):
    for cand in (1024, 512, 256, 128, 64, 32, 16):
        if rows % cand == 0 and cand * cols * 4 <= ELEMENTWISE_BLOCK_BYTES:
            return cand
    return rows


def chip_sum(name, s, r, me, c):
    _, a, b = s.shape
    tr = _row_block(a, b)

    def body(idx_ref, s_ref, r_ref, o_ref):
        del idx_ref
        acc = s_ref[...].astype(F32)
        for k in range(3):
            acc = acc + r_ref[k].astype(F32)
        o_ref[...] = acc

    return pl.pallas_call(
        body, name=name, out_shape=jax.ShapeDtypeStruct((2, a, b), F32),
        grid_spec=pltpu.PrefetchScalarGridSpec(
            num_scalar_prefetch=1, grid=(a // tr,),
            in_specs=[pl.BlockSpec((None, tr, b), lambda i, idx: (idx[0], i, 0)),
                      pl.BlockSpec((3, tr, b), lambda i, idx: (0, i, 0))],
            out_specs=pl.BlockSpec((None, tr, b), lambda i, idx: (idx[1], i, 0))),
        compiler_params=_cparams(("arbitrary",)),
    )(jnp.stack([me, c]).astype(jnp.int32), s, r)


def pair_add(name, p, recv, c):
    nj, _, rh, cd = p.shape
    tr = _row_block(rh, cd)

    def body(c_ref, p_ref, r_ref, o_ref):
        del c_ref
        o_ref[...] = (p_ref[0] + r_ref[...]).astype(o_ref.dtype)

    return pl.pallas_call(
        body, name=name, out_shape=jax.ShapeDtypeStruct((nj, rh, cd), BF16),
        grid_spec=pltpu.PrefetchScalarGridSpec(
            num_scalar_prefetch=1, grid=(nj, rh // tr),
            in_specs=[pl.BlockSpec((1, 1, tr, cd), lambda j, i, c_ref: (j, c_ref[0], i, 0)),
                      pl.BlockSpec((1, tr, cd), lambda j, i, c_ref: (j, i, 0))],
            out_specs=pl.BlockSpec((1, tr, cd), lambda j, i, c_ref: (j, i, 0))),
        compiler_params=_cparams(("arbitrary", "arbitrary")),
    )(jnp.reshape(c, (1,)).astype(jnp.int32), p, recv)


def slab_sum(name, a):
    n, r, cd = a.shape
    tr = _pick(r, (256, 128, 64, 32, 16, 8))

    def body(a_ref, o_ref):
        acc = a_ref[0].astype(F32)
        for j in range(1, n):
            acc = acc + a_ref[j].astype(F32)
        o_ref[...] = acc

    return pl.pallas_call(
        body, name=name, out_shape=jax.ShapeDtypeStruct((r, cd), F32),
        grid=(r // tr,),
        in_specs=[pl.BlockSpec((n, tr, cd), lambda i: (0, i, 0))],
        out_specs=pl.BlockSpec((tr, cd), lambda i: (i, 0)),
        compiler_params=_cparams(("arbitrary",)),
    )(a)


ADAM_C1 = 1.0 - ADAM_B1 ** ADAM_STEP
ADAM_C2 = 1.0 - ADAM_B2 ** ADAM_STEP


def adamw(name, w, g, m, v):
    r, cd = w.shape
    tr = r
    for cand in (512, 256, 128, 64, 32, 16, 8):
        if r % cand == 0 and cand * cd * 4 <= (1 << 20):
            tr = cand
            break

    def body(w_ref, g_ref, m_ref, v_ref, d_ref, nm_ref, nv_ref):
        gv = g_ref[...]
        nm = ADAM_B1 * m_ref[...] + (1.0 - ADAM_B1) * gv
        nv = ADAM_B2 * v_ref[...] + (1.0 - ADAM_B2) * (gv * gv)
        m_hat = nm / ADAM_C1
        v_hat = nv / ADAM_C2
        d_ref[...] = -ADAM_LR * (m_hat / (jnp.sqrt(v_hat) + ADAM_EPS) + ADAM_WD * w_ref[...])
        nm_ref[...] = nm
        nv_ref[...] = nv

    spec = pl.BlockSpec((tr, cd), lambda i: (i, 0))
    sd = jax.ShapeDtypeStruct((r, cd), F32)
    return pl.pallas_call(
        body, name=name, out_shape=[sd, sd, sd], grid=(r // tr,),
        in_specs=[spec] * 4, out_specs=[spec] * 3,
        compiler_params=_cparams(("arbitrary",)),
    )(w, g, m, v)


WEIGHTS = ("norm_mix_w", "w_in", "ssm_conv_w", "ssm_conv_b", "ssm_dt_bias", "ssm_a_log", "ssm_d", "ssm_norm_w",
           "gdn_conv_w", "gdn_a_log", "gdn_dt_bias", "gdn_norm_w", "w_proj_ssm", "w_proj_gdn", "w_out",
           "norm_ffn_w", "w_ffn_in", "w_ffn_down", "final_norm_w")
BIG = (("w_in", 2), ("w_proj_ssm", 1), ("w_proj_gdn", 1), ("w_out", 1), ("w_ffn_in", 2), ("w_ffn_down", 1))
CONVW = (("ssm_conv_w", 2), ("gdn_conv_w", 2))
SHARDED = BIG + CONVW
SMALL = tuple(n for n in WEIGHTS if n not in dict(SHARDED))


def _unpack(buf, shapes, lead=()):
    flat = buf.reshape(lead + (-1,))
    out, o = [], 0
    for shp in shapes:
        n = math.prod(shp)
        out.append(flat[..., o:o + n].reshape(lead + tuple(shp)))
        o += n
    return out


def _pack_rows(arrs, lead=(), mult=8):
    nl = len(lead)
    flat = jnp.concatenate([a.reshape(lead + (-1,)) for a in arrs], axis=nl)
    n = flat.shape[nl]
    rows = -(-n // (mult * LANE)) * mult
    flat = jnp.pad(flat, [(0, 0)] * nl + [(0, rows * LANE - n)])
    return flat.reshape(lead + (rows, LANE))


def _slot_buffer(shard):
    return jnp.pad(shard[None], [(0, N_CHIPS - 1)] + [(0, 0)] * shard.ndim)


def kernel(x, norm_mix_w, w_in, ssm_conv_w, ssm_conv_b, ssm_dt_bias, ssm_a_log, ssm_d, ssm_norm_w, gdn_conv_w, gdn_a_log, gdn_dt_bias, gdn_norm_w, w_proj_ssm, w_proj_gdn, w_out, norm_ffn_w, w_ffn_in, w_ffn_down, final_norm_w, loss_target, m_norm_mix_w, m_w_in, m_ssm_conv_w, m_ssm_conv_b, m_ssm_dt_bias, m_ssm_a_log, m_ssm_d, m_ssm_norm_w, m_gdn_conv_w, m_gdn_a_log, m_gdn_dt_bias, m_gdn_norm_w, m_w_proj_ssm, m_w_proj_gdn, m_w_out, m_norm_ffn_w, m_w_ffn_in, m_w_ffn_down, m_final_norm_w, v_norm_mix_w, v_w_in, v_ssm_conv_w, v_ssm_conv_b, v_ssm_dt_bias, v_ssm_a_log, v_ssm_d, v_ssm_norm_w, v_gdn_conv_w, v_gdn_a_log, v_gdn_dt_bias, v_gdn_norm_w, v_w_proj_ssm, v_w_proj_gdn, v_w_out, v_norm_ffn_w, v_w_ffn_in, v_w_ffn_down, v_final_norm_w):
    wl = (norm_mix_w, w_in, ssm_conv_w, ssm_conv_b, ssm_dt_bias, ssm_a_log, ssm_d, ssm_norm_w, gdn_conv_w,
          gdn_a_log, gdn_dt_bias, gdn_norm_w, w_proj_ssm, w_proj_gdn, w_out, norm_ffn_w, w_ffn_in, w_ffn_down,
          final_norm_w)
    ml = (m_norm_mix_w, m_w_in, m_ssm_conv_w, m_ssm_conv_b, m_ssm_dt_bias, m_ssm_a_log, m_ssm_d, m_ssm_norm_w,
          m_gdn_conv_w, m_gdn_a_log, m_gdn_dt_bias, m_gdn_norm_w, m_w_proj_ssm, m_w_proj_gdn, m_w_out,
          m_norm_ffn_w, m_w_ffn_in, m_w_ffn_down, m_final_norm_w)
    vl = (v_norm_mix_w, v_w_in, v_ssm_conv_w, v_ssm_conv_b, v_ssm_dt_bias, v_ssm_a_log, v_ssm_d, v_ssm_norm_w,
          v_gdn_conv_w, v_gdn_a_log, v_gdn_dt_bias, v_gdn_norm_w, v_w_proj_ssm, v_w_proj_gdn, v_w_out,
          v_norm_ffn_w, v_w_ffn_in, v_w_ffn_down, v_final_norm_w)
    w = dict(zip(WEIGHTS, wl))
    m = dict(zip(WEIGHTS, ml))
    v = dict(zip(WEIGHTS, vl))
    x_pos, y_pos, c = _pos()
    me = 2 * x_pos + y_pos
    big = [n for n, _ in BIG]

    shards = [w[n].astype(BF16) for n in big]
    shards[0] = jnp.pad(shards[0], ((0, 0), (0, 0), (0, IN_SHARD_PAD - IN_SHARD)))
    conv_shapes = [w[n].shape[1:] for n, _ in CONVW]
    conv_pack = _pack_rows([w[n] for n, _ in CONVW], lead=(DEPTH,), mult=16)

    def slot_buffers(l):
        return [_slot_buffer(s[l].reshape((2, s.shape[1] // 2) + s.shape[2:])) for s in shards + [conv_pack]]

    def assemble(bufs, which):
        out = {}
        for g_, i in zip(bufs, which):
            by_chip = [lax.dynamic_index_in_dim(g_, jnp.bitwise_xor(me, j), 0, keepdims=False)
                       .reshape((-1,) + g_.shape[3:]) for j in range(N_CHIPS)]
            if i < 0:
                conv_parts = [_unpack(p, conv_shapes) for p in by_chip]
                for k, (n, axis) in enumerate(CONVW):
                    out[n] = jnp.concatenate([conv_parts[j][k] for j in range(N_CHIPS)], axis=axis - 1)
            else:
                n, axis = BIG[i]
                cols = IN_SHARD if n == "w_in" else by_chip[0].shape[-1]
                out[n] = jnp.concatenate([p[:, :cols] for p in by_chip], axis=axis - 1)
        return out

    everything = list(range(len(BIG))) + [-1]
    first, rest = [0, -1], everything[1:-1]
    bufs0 = slot_buffers(0)
    landed_first = weights_gather("gather_w_l0_first", [bufs0[i] for i in first])
    rest_sems = gather_start("gather_w_l0_rest_start", [bufs0[i] for i in rest], landed_first[0])
    l1_sems = gather_start("gather_w_l1_start", slot_buffers(1), rest_sems[2][0])

    def land(name, flying, after):
        send_sems, recv_sems, bufs, _ = flying
        return weights_forward(f"{name}_forward", gather_wait(f"{name}_wait", send_sems, recv_sems, bufs, after))

    def gathered_of(l, after):
        if l == 0:
            return {**assemble(landed_first, first),
                    "late_weights": lambda later: assemble(land("gather_w_l0_rest", rest_sems, later), rest)}
        return assemble(land("gather_w_l1", l1_sems, after), everything)

    token = rest_sems[3] + l1_sems[3]

    in_flight = {}

    def start_reduction(l, g_layer, dx):
        halves = [g_layer[n].reshape((1, 2, -1, AL_DIM)) if n == "w_in"
                  else g_layer[n].reshape((N_CHIPS, 2, -1) + g_layer[n].shape[-1:]) for n in big]
        from_pair = pair_swap(f"grad_pair_swap_l{l}", halves)
        chip_part = [pair_add(f"grad_pair_add_{n}_l{l}", g_, r_, c) for n, g_, r_ in zip(big, halves, from_pair)]
        chip_part[0] = _aligned_to_shards(chip_part[0][0])
        *in_flight[l], token_l = scatter_start(f"grad_scatter_start_l{l}", chip_part, dx)
        return dx + token_l[0, 0]

    def finish_reduction(l, after):
        parts, landed = scatter_wait(f"grad_scatter_wait_l{l}", *in_flight[l], after)
        sums = [chip_sum(f"grad_chip_sum_{n}_l{l}", s_, r_, me, c) for n, s_, r_ in zip(big, parts, landed)]
        return [r_.reshape((-1,) + r_.shape[2:]) for r_ in pair_share(f"grad_pair_share_l{l}", sums)]

    full = {n: w[n] for n in SMALL}
    loss_part, grad_x, grads = local_step(x[0] + token[0, 0], loss_target[0], full, gathered_of, start_reduction)
    reduced = [None] * DEPTH
    after = grad_x
    for l in reversed(range(DEPTH)):
        reduced[l] = finish_reduction(l, after)
        after = reduced[l][0]
    g_sharded = {n: jnp.stack([reduced[l][i] for l in range(DEPTH)]) for i, n in enumerate(big)}
    g_sharded["w_in"] = g_sharded["w_in"][:, :, :IN_SHARD]

    small_names = list(SMALL) + [n for n, _ in CONVW]
    small_all = all_allgather("gather_small", _pack_rows([grads[n] for n in small_names] + [loss_part[0, :1]]))
    small_sum = slab_sum("small_sum", small_all)
    small_vals = _unpack(small_sum, [grads[n].shape for n in small_names] + [(1,)])
    g_small = dict(zip(small_names, small_vals[:-1]))
    loss = small_vals[-1].reshape(())
    for n, axis in CONVW:
        size = w[n].shape[axis]
        g_sharded[n] = lax.dynamic_slice_in_dim(g_small.pop(n), me * size, size, axis=axis)

    out_g, out_d, out_m, out_v = {}, {}, {}, {}
    for n, _ in SHARDED:
        shp = w[n].shape
        two = lambda a: a.reshape(-1, shp[-1])
        d_, m_, v_ = adamw(f"adamw_{n}", two(w[n]), two(g_sharded[n]), two(m[n]), two(v[n]))
        out_g[n], out_d[n], out_m[n], out_v[n] = g_sharded[n], d_.reshape(shp), m_.reshape(shp), v_.reshape(shp)
    d_, m_, v_ = adamw("adamw_small", *[_pack_rows([d[n] for n in SMALL]) for d in (w, g_small, m, v)])
    small_shapes = [w[n].shape for n in SMALL]
    for n, dd, mm, vv in zip(SMALL, _unpack(d_, small_shapes), _unpack(m_, small_shapes), _unpack(v_, small_shapes)):
        out_g[n], out_d[n], out_m[n], out_v[n] = g_small[n], dd, mm, vv

    return (loss, grad_x[None], *[out_g[n] for n in WEIGHTS], *[out_d[n] for n in WEIGHTS],
            *[out_m[n] for n in WEIGHTS], *[out_v[n] for n in WEIGHTS])
```

```python
import math

import jax
import jax.numpy as jnp
from jax import lax
from jax.experimental import pallas as pl
from jax.experimental.pallas import tpu as pltpu

F32 = jnp.float32
BF16 = jnp.bfloat16

D_MODEL = 1024
DEPTH = 2
SSM_HEADS = 16
SSM_HEAD_DIM = 64
SSM_D_INNER = 1024
SSM_STATE = 128
SSM_CONV_DIM = 1536
GDN_HEADS = 8
GDN_HEAD = 128
GDN_QKV_DIM = 3072
CONV_K = 4
CHUNK = 128
SSD_CHUNK = 256
FFN_HIDDEN = 2816
EPS = 1e-6
IN_DIM = 8736

ADAM_LR = 0.001
ADAM_B1 = 0.9
ADAM_B2 = 0.999
ADAM_EPS = 1e-08
ADAM_WD = 0.01
ADAM_STEP = 10

LANE = 128
NEG_BIG = -1e30
VMEM_LIMIT = 56 * 1024 * 1024

AL_Z, AL_XBC, AL_QKV, AL_GZ, AL_GS, AL_GG, AL_SMALL, AL_DIM = 0, 1024, 2560, 5632, 6656, 7680, 8704, 9216
SM_DT, SM_A, SM_B = 0, 16, 24

HI = lax.Precision.HIGHEST
NN = (((1,), (0,)), ((), ()))
NT = (((1,), (1,)), ((), ()))
TN = (((0,), (0,)), ((), ()))


def _cparams(sem):
    return pltpu.CompilerParams(dimension_semantics=sem, vmem_limit_bytes=VMEM_LIMIT)


def _pick(n, prefs):
    for p in prefs:
        if n % p == 0:
            return p
    return n


MATMUL_VMEM_BUDGET = 40 << 20
MXU_WIDTH = 256
HBM_BYTES_PER_S = 3.3e12
MXU_FLOPS_PER_S = 9.0e14
GRID_STEP_S = 0.35e-6
N_CHIPS = 4


def _matmul_tiles(m, n, n_dom, k, a_item, b_item, o_item):
    best = None
    def cands(dim, cap):
        return [c for c in range(LANE, min(dim, cap) + 1, LANE) if dim % c == 0] or [dim]

    tms, tns, tks = cands(m, 2048), cands(n_dom, 2304), cands(k, 1 << 30)
    for tm in tms:
        for tn in tns:
            for tk in tks:
                nk = k // tk
                vmem = (2 * (tm * tk * a_item + tk * tn * b_item + tm * tn * o_item) + tm * tn * 4 * (2 if nk > 1 else 1)
                        + (tm * tk * 2 if a_item > 2 else 0) + (tk * tn * 2 if b_item > 2 else 0))
                if vmem > MATMUL_VMEM_BUDGET:
                    continue
                traffic = m * k * a_item * (1 if nk == 1 else n // tn) + k * n * b_item * (m // tm)
                mxu_fill = tn / (-(-tn // MXU_WIDTH) * MXU_WIDTH)
                cost = (max(traffic / HBM_BYTES_PER_S, 2.0 * m * n * k / (MXU_FLOPS_PER_S * mxu_fill))
                        + (m // tm) * (n // tn) * nk * GRID_STEP_S)
                if best is None or cost < best[0]:
                    best = (cost, (tm, tn, tk))
    return best[1]


def _dot(a, b, dims=NN):
    return lax.dot_general(a.astype(BF16), b.astype(BF16), dims, preferred_element_type=F32)


def _dot3(a, b, dims=NN):
    a_hi, b_hi = a.astype(BF16), b.astype(BF16)
    a_lo = (a - a_hi.astype(F32)).astype(BF16)
    b_lo = (b - b_hi.astype(F32)).astype(BF16)

    def dg(u, v):
        return lax.dot_general(u, v, dims, preferred_element_type=F32)

    return dg(a_hi, b_hi) + (dg(a_hi, b_lo) + dg(a_lo, b_hi))


def _dot_hi(a, b, dims=NN):
    return lax.dot_general(a, b, dims, precision=HI, preferred_element_type=F32)


def _sigmoid(x):
    return jax.nn.sigmoid(x)


def _silu(x):
    return x * _sigmoid(x)


def _softplus(x):
    return jnp.maximum(x, 0.0) + jnp.log1p(jnp.exp(-jnp.abs(x)))


def matmul(name, a, b, mode, out_dtype=F32, chip_major=False, stack=None):
    if mode == "nn":
        (m, k), (k2, n) = a.shape, b.shape
    elif mode == "nt":
        (m, k), (n, k2) = a.shape, b.shape
    else:
        (k, m), (k2, n) = a.shape, b.shape
    assert k == k2, (a.shape, b.shape, mode)
    shard = n // N_CHIPS if chip_major else n
    tm, tn, tk = _matmul_tiles(m, n, shard, k, a.dtype.itemsize, b.dtype.itemsize, jnp.dtype(out_dtype).itemsize)
    if chip_major:
        per = shard // tn
        base_shape, base_blk = (N_CHIPS, m, shard), (None, tm, tn)
        base_idx = lambda i, j: (j // per, i, j % per)
    else:
        base_shape, base_blk = (m, n), (tm, tn)
        base_idx = lambda i, j: (i, j)
    nk = k // tk
    dims = {"nn": NN, "nt": NT, "tn": TN}[mode]

    def body_acc(a_ref, b_ref, o_ref, acc_ref):
        kk = pl.program_id(2)

        @pl.when(kk == 0)
        def _():
            acc_ref[...] = jnp.zeros_like(acc_ref)

        acc_ref[...] += _dot(a_ref[...], b_ref[...], dims)

        @pl.when(kk == nk - 1)
        def _():
            o_ref[...] = acc_ref[...].astype(o_ref.dtype)

    def body_one(a_ref, b_ref, o_ref):
        o_ref[...] = _dot(a_ref[...], b_ref[...], dims).astype(o_ref.dtype)

    compute = body_one if nk == 1 else body_acc
    if mode == "tn":
        a_spec = pl.BlockSpec((tk, tm), lambda i, j, kk: (kk, i))
    else:
        a_spec = pl.BlockSpec((tm, tk), lambda i, j, kk: (i, kk))
    if mode == "nt":
        b_spec = pl.BlockSpec((tn, tk), lambda i, j, kk: (j, kk))
    else:
        b_spec = pl.BlockSpec((tk, tn), lambda i, j, kk: (kk, j))
    in_specs, operands, aliases, body = [a_spec, b_spec], [a, b], {}, compute
    if stack is None:
        out_shape, out_blk, out_idx = base_shape, base_blk, (lambda i, j, kk: base_idx(i, j))
    else:
        layer, buf = stack
        out_shape, out_blk = (DEPTH,) + base_shape, (None,) + base_blk
        out_idx = lambda i, j, kk: (layer,) + base_idx(i, j)
        if buf is not None:
            assert buf.shape == out_shape and buf.dtype == out_dtype
            in_specs.append(pl.BlockSpec(memory_space=pl.ANY))
            operands.append(buf)
            aliases = {2: 0}

            def body(a_ref, b_ref, buf_ref, *rest):
                del buf_ref
                compute(a_ref, b_ref, *rest)

    return pl.pallas_call(
        body, name=name,
        out_shape=jax.ShapeDtypeStruct(out_shape, out_dtype),
        grid=(m // tm, n // tn, nk),
        in_specs=in_specs,
        out_specs=pl.BlockSpec(out_blk, out_idx),
        scratch_shapes=[] if nk == 1 else [pltpu.VMEM((tm, tn), F32)],
        input_output_aliases=aliases,
        compiler_params=_cparams(("parallel", "parallel", "arbitrary")),
    )(*operands)


def _row_map(c0, moves):
    return (lambda j, i: (i, c0 + j)) if moves else (lambda j, i: (i, c0))


def _par_map(c0, moves):
    return (lambda j, i: (0, c0 + j)) if moves else (lambda j, i: (0, c0))


def _in_spec(op, tile):
    _, kind, w, c0, moves = op
    if kind == "row":
        return pl.BlockSpec((tile, w), _row_map(c0, moves))
    return pl.BlockSpec((1, w), _par_map(c0, moves))


ROW_BLOCK_ELEMS = 1 << 18


def _row_tile(t, tile, ops):
    width = max(op[2] for op in ops if op[1] == "row")
    return min(t, max(tile, ROW_BLOCK_ELEMS // width))


def rowwise_fwd(name, fn, t, tile, ncol, ins, outs):
    n_in = len(ins)
    tile = _row_tile(t, tile, ins)

    def body(*refs):
        vals = [r[...].astype(F32) for r in refs[:n_in]]
        res = fn(*vals)
        if not isinstance(res, (tuple, list)):
            res = (res,)
        for r, v in zip(refs[n_in:], res):
            r[...] = v.astype(r.dtype)

    res = pl.pallas_call(
        body, name=name,
        out_shape=[jax.ShapeDtypeStruct((t, w * ncol), dt) for w, dt in outs],
        grid=(ncol, t // tile),
        in_specs=[_in_spec(op, tile) for op in ins],
        out_specs=[pl.BlockSpec((tile, w), _row_map(0, True)) for w, _ in outs],
        compiler_params=_cparams(("arbitrary", "arbitrary")),
    )(*[op[0] for op in ins])
    return res


def rowwise_bwd(name, fn, t, tile, ncol, ins, need, cts, addends=None, row_dtypes=None):
    n_in, n_ct = len(ins), len(cts)
    tile = _row_tile(t, tile, ins)
    addends = addends or {}
    row_dtypes = row_dtypes or {}
    didx = [i for i in range(n_in) if need[i]]
    add_ops = [addends[i] for i in didx if i in addends]
    n_add = len(add_ops)

    def body(*refs):
        in_refs = refs[:n_in]
        ct_refs = refs[n_in:n_in + n_ct]
        add_refs = refs[n_in + n_ct:n_in + n_ct + n_add]
        out_refs = refs[n_in + n_ct + n_add:]
        vals = [r[...].astype(F32) for r in in_refs]

        def g(*dv):
            full = list(vals)
            for i, v in zip(didx, dv):
                full[i] = v
            res = fn(*full)
            return tuple(res) if isinstance(res, (tuple, list)) else (res,)

        _, vjp = jax.vjp(g, *[vals[i] for i in didx])
        grads = vjp(tuple(c[...].astype(F32) for c in ct_refs))
        j, i = pl.program_id(0), pl.program_id(1)
        a = 0
        for o_ref, gv, idx in zip(out_refs, grads, didx):
            _, kind, _, _, moves = ins[idx]
            if kind == "row":
                if idx in addends:
                    gv = gv + add_refs[a][...].astype(F32)
                    a += 1
                o_ref[...] = gv.astype(o_ref.dtype)
            else:
                first = (i == 0) if moves else jnp.logical_and(i == 0, j == 0)

                @pl.when(first)
                def _(o_ref=o_ref, gv=gv):
                    o_ref[...] = gv

                @pl.when(jnp.logical_not(first))
                def _(o_ref=o_ref, gv=gv):
                    o_ref[...] += gv

    out_shape, out_specs = [], []
    for idx in didx:
        _, kind, w, _, moves = ins[idx]
        cols = w * (ncol if moves else 1)
        if kind == "row":
            out_shape.append(jax.ShapeDtypeStruct((t, cols), row_dtypes.get(idx, F32)))
            out_specs.append(pl.BlockSpec((tile, w), _row_map(0, moves)))
        else:
            out_shape.append(jax.ShapeDtypeStruct((1, cols), F32))
            out_specs.append(pl.BlockSpec((1, w), _par_map(0, moves)))
    ops = list(ins) + list(cts) + add_ops
    res = pl.pallas_call(
        body, name=name,
        out_shape=out_shape,
        grid=(ncol, t // tile),
        in_specs=[_in_spec(op, tile) for op in ops],
        out_specs=out_specs,
        compiler_params=_cparams(("arbitrary", "arbitrary")),
    )(*[op[0] for op in ops])
    return res


def f_rmsnorm(x, w):
    return x * lax.rsqrt(jnp.mean(x * x, axis=-1, keepdims=True) + EPS) * w


def f_ssd_post(y, z, w):
    y = y * _silu(z)
    return y * lax.rsqrt(jnp.mean(y * y, axis=-1, keepdims=True) + EPS) * w


def f_gdn_post(o, z, w):
    o = o * lax.rsqrt(jnp.mean(o * o, axis=-1, keepdims=True) + EPS) * w
    return o * _silu(z)


def f_merge(gs, p1, gg, p2):
    return _sigmoid(gs) * p1 + _sigmoid(gg) * p2


def f_swiglu(g, u):
    return _silu(g) * u


def final_loss(name, x, tgt, w, tile=256):
    t, d = x.shape

    def body(x_ref, t_ref, w_ref, loss_ref, dx_ref, dw_ref):
        i = pl.program_id(0)
        xv, tv, wv = x_ref[...], t_ref[...], w_ref[...]

        def g(xx, ww):
            err = f_rmsnorm(xx, ww) - tv
            return 0.5 * jnp.sum(jnp.mean(err * err, axis=-1, keepdims=True), axis=0, keepdims=True)

        val, vjp = jax.vjp(g, xv, wv)
        dx, dw = vjp(jnp.ones((1, 1), F32))
        dx_ref[...] = dx
        lv = jnp.broadcast_to(val, (1, LANE))

        @pl.when(i == 0)
        def _():
            loss_ref[...] = lv
            dw_ref[...] = dw

        @pl.when(i != 0)
        def _():
            loss_ref[...] += lv
            dw_ref[...] += dw

    return pl.pallas_call(
        body, name=name,
        out_shape=[jax.ShapeDtypeStruct((1, LANE), F32), jax.ShapeDtypeStruct((t, d), F32),
                   jax.ShapeDtypeStruct((1, d), F32)],
        grid=(t // tile,),
        in_specs=[pl.BlockSpec((tile, d), lambda i: (i, 0)), pl.BlockSpec((tile, d), lambda i: (i, 0)),
                  pl.BlockSpec((1, d), lambda i: (0, 0))],
        out_specs=[pl.BlockSpec((1, LANE), lambda i: (0, 0)), pl.BlockSpec((tile, d), lambda i: (i, 0)),
                   pl.BlockSpec((1, d), lambda i: (0, 0))],
        compiler_params=_cparams(("arbitrary",)),
    )(x, tgt, w)


CONV_W = 512
HALO = 8
STRIPS = 4


def _rows_back(before, cur, d):
    rows = lax.broadcasted_iota(jnp.int32, cur.shape, 0)
    return jnp.where(rows < d, pltpu.roll(before, d, 0), pltpu.roll(cur, d, 0))


def _rows_ahead(cur, after, d):
    rows = lax.broadcasted_iota(jnp.int32, cur.shape, 0)
    return jnp.where(rows < HALO - d, pltpu.roll(cur, HALO - d, 0), pltpu.roll(after, HALO - d, 0))


def _conv_taps(taps, bias, before, cur):
    shifted = [_rows_back(before, cur, CONV_K - 1 - k) for k in range(CONV_K - 1)] + [cur]
    pre = bias + taps[CONV_K - 1] * cur
    for k in range(CONV_K - 1):
        pre = pre + taps[k] * shifted[k]
    return pre, shifted


def conv_fwd(name, src, c0, width, w, b, tile=512):
    t = src.shape[0]
    ncol, nrow = width // CONV_W, t // tile
    cb0 = c0 // CONV_W
    hb = tile // HALO

    def body(prev_ref, cur_ref, w_ref, b_ref, o_ref):
        i = pl.program_id(1)
        taps = [w_ref[k:k + 1, :] for k in range(CONV_K)]
        bias = b_ref[...]

        def strips(g, before):
            for u in range(STRIPS):
                r0 = pl.multiple_of((g * STRIPS + u) * HALO, HALO)
                cur = cur_ref[pl.ds(r0, HALO), :]
                pre, _ = _conv_taps(taps, bias, before, cur)
                o_ref[pl.ds(r0, HALO), :] = _silu(pre)
                before = cur
            return before

        lax.fori_loop(0, tile // (HALO * STRIPS), strips, jnp.where(i == 0, 0.0, prev_ref[...]))

    return pl.pallas_call(
        body, name=name,
        out_shape=jax.ShapeDtypeStruct((t, width), F32),
        grid=(ncol, nrow),
        in_specs=[pl.BlockSpec((HALO, CONV_W), lambda j, i: (jnp.maximum(i * hb - 1, 0), cb0 + j)),
                  pl.BlockSpec((tile, CONV_W), lambda j, i: (i, cb0 + j)),
                  pl.BlockSpec((CONV_K, CONV_W), lambda j, i: (0, j)),
                  pl.BlockSpec((1, CONV_W), lambda j, i: (0, j))],
        out_specs=pl.BlockSpec((tile, CONV_W), lambda j, i: (i, j)),
        compiler_params=_cparams(("arbitrary", "arbitrary")),
    )(src, src, w, b)


def conv_bwd(name, src, c0, width, w, b, dy, tile=512):
    t = src.shape[0]
    ncol, nrow = width // CONV_W, t // tile
    cb0 = c0 // CONV_W
    hb = tile // HALO
    last_hb = t // HALO - 1
    nstrip = tile // HALO

    def body(sprev_ref, scur_ref, snext_ref, w_ref, b_ref, dycur_ref, dynext_ref,
             du_ref, dw_ref, db_ref, dpre_ref):
        i = pl.program_id(1)
        taps = [w_ref[k:k + 1, :] for k in range(CONV_K)]
        bias = b_ref[...]

        def dpre_of(before, cur, dy_strip):
            pre, shifted = _conv_taps(taps, bias, before, cur)
            s = _sigmoid(pre)
            return dy_strip * (s * (1.0 + pre * (1.0 - s))), shifted

        def strips1(g, carry):
            before, dws, dbs = carry
            for u in range(STRIPS):
                r0 = pl.multiple_of((g * STRIPS + u) * HALO, HALO)
                cur = scur_ref[pl.ds(r0, HALO), :]
                dpre, shifted = dpre_of(before, cur, dycur_ref[pl.ds(r0, HALO), :])
                dpre_ref[pl.ds(r0, HALO), :] = dpre
                before, dws, dbs = cur, tuple(a + dpre * v for a, v in zip(dws, shifted)), dbs + dpre
            return before, dws, dbs

        zero = jnp.zeros((HALO, CONV_W), F32)
        before, dws, dbs = lax.fori_loop(0, nstrip // STRIPS, strips1,
                                         (jnp.where(i == 0, 0.0, sprev_ref[...]), (zero,) * CONV_K, zero))
        dpre_next, _ = dpre_of(before, snext_ref[...], dynext_ref[...])
        dpre_ref[pl.ds(tile, HALO), :] = jnp.where(i == nrow - 1, 0.0, dpre_next)

        def strips2(g, _):
            parts = []
            for u in range(STRIPS):
                r0 = pl.multiple_of((g * STRIPS + u) * HALO, HALO)
                cur = dpre_ref[pl.ds(r0, HALO), :]
                after = dpre_ref[pl.ds(r0 + HALO, HALO), :]
                acc = taps[CONV_K - 1] * cur
                for d in range(1, CONV_K):
                    acc = acc + taps[CONV_K - 1 - d] * _rows_ahead(cur, after, d)
                parts.append(acc)
            r0 = pl.multiple_of(g * STRIPS * HALO, STRIPS * HALO)
            du_ref[pl.ds(r0, STRIPS * HALO), :] = jnp.concatenate(parts, axis=0).astype(du_ref.dtype)
            return 0

        lax.fori_loop(0, nstrip // STRIPS, strips2, 0)
        dw_tile = jnp.concatenate([jnp.sum(a, axis=0, keepdims=True) for a in dws], axis=0)
        db_tile = jnp.sum(dbs, axis=0, keepdims=True)
        _acc(dw_ref, dw_tile, i == 0)
        _acc(db_ref, db_tile, i == 0)

    return pl.pallas_call(
        body, name=name,
        out_shape=[jax.ShapeDtypeStruct((t, width), BF16), jax.ShapeDtypeStruct((CONV_K, width), F32),
                   jax.ShapeDtypeStruct((1, width), F32)],
        grid=(ncol, nrow),
        in_specs=[pl.BlockSpec((HALO, CONV_W), lambda j, i: (jnp.maximum(i * hb - 1, 0), cb0 + j)),
                  pl.BlockSpec((tile, CONV_W), lambda j, i: (i, cb0 + j)),
                  pl.BlockSpec((HALO, CONV_W), lambda j, i: (jnp.minimum((i + 1) * hb, last_hb), cb0 + j)),
                  pl.BlockSpec((CONV_K, CONV_W), lambda j, i: (0, j)),
                  pl.BlockSpec((1, CONV_W), lambda j, i: (0, j)),
                  pl.BlockSpec((tile, CONV_W), lambda j, i: (i, j)),
                  pl.BlockSpec((HALO, CONV_W), lambda j, i: (jnp.minimum((i + 1) * hb, last_hb), j))],
        out_specs=[pl.BlockSpec((tile, CONV_W), lambda j, i: (i, j)),
                   pl.BlockSpec((CONV_K, CONV_W), lambda j, i: (0, j)),
                   pl.BlockSpec((1, CONV_W), lambda j, i: (0, j))],
        scratch_shapes=[pltpu.VMEM((tile + HALO, CONV_W), F32)],
        compiler_params=_cparams(("arbitrary", "arbitrary")),
    )(src, src, src, w, b, dy, dy)


def _iota2(q):
    return (lax.broadcasted_iota(jnp.int32, (q, q), 0), lax.broadcasted_iota(jnp.int32, (q, q), 1))


def _lane_pick(blk, idx):
    lane = lax.broadcasted_iota(jnp.int32, (1, LANE), 1)
    return jnp.sum(jnp.where(lane == idx, blk, 0.0), axis=1, keepdims=True)


class _Decay:
    def __init__(self, a):
        q = a.shape[0]
        r, c = _iota2(q)
        self.r, self.c = r, c
        self.cum = _dot_hi((c <= r).astype(F32), a)
        self.cum_t = _dot_hi(a, (r <= c).astype(F32), TN)
        self.tot = self.cum[q - 1:q, :]
        self.e_cum = jnp.exp(self.cum)
        self.e_rest = jnp.exp(self.tot - self.cum)
        self.e_tot = jnp.exp(self.tot)

    def mask(self, lane):
        rows = lax.broadcasted_iota(jnp.int32, (LANE, 1), 0)
        cum_row = jnp.sum(jnp.where(rows == lane, self.cum_t, 0.0), axis=0, keepdims=True)
        return jnp.exp(jnp.where(self.r >= self.c, _lane_pick(self.cum, lane) - cum_row, NEG_BIG))


_SSD_B = SSM_D_INNER
_SSD_C = SSM_D_INNER + 2 * SSM_STATE


def _interleave(gens):
    results = [None] * len(gens)
    live = list(range(len(gens)))
    while live:
        for i in list(live):
            try:
                next(gens[i])
            except StopIteration as stop:
                results[i] = stop.value
                live.remove(i)
    return results


def ssd_chunk(xs, bm, cm, dt_all, dsk, dec, state, p, cb):
    lane = lax.broadcasted_iota(jnp.int32, (1, LANE), 1)
    m0 = lane < SSM_HEAD_DIM
    h0, h1 = 2 * p, 2 * p + 1

    def both(blk):
        return jnp.where(m0, _lane_pick(blk, h0), _lane_pick(blk, h1))

    xdt = xs * both(dt_all)
    l0, l1 = dec.mask(h0), dec.mask(h1)
    yield
    y_diag = _dot(cb * l0, jnp.where(m0, xdt, 0.0)) + _dot(cb * l1, jnp.where(m0, 0.0, xdt))
    y_off = _dot(cm, state, NT) * both(dec.e_cum)
    yield
    rowm = lax.broadcasted_iota(jnp.int32, (LANE, 1), 0) < SSM_HEAD_DIM
    new_state = (state * jnp.where(rowm, _lane_pick(dec.e_tot, h0), _lane_pick(dec.e_tot, h1))
                 + _dot(xdt * both(dec.e_rest), bm, TN))
    y = y_diag + y_off + both(dsk) * xs
    return y, new_state


def ssd_pairs(xs, bms, cms, small, dtb, alog, dsk, states):
    dt_all = _softplus(small + dtb)
    dec = _Decay(dt_all * (-jnp.exp(alog)))
    cbs = [_dot(cm, bm, NT) for cm, bm in zip(cms, bms)]
    res = _interleave([ssd_chunk(x, bms[p // 4], cms[p // 4], dt_all, dsk, dec, st, p, cbs[p // 4])
                       for p, (x, st) in enumerate(zip(xs, states))])
    return tuple(y for y, _ in res), tuple(s for _, s in res)


def tri_inverse(a):
    q = a.shape[0]
    r, c = _iota2(q)
    eye = (r == c).astype(F32)
    diag = (r // 16) == (c // 16)
    bd = jnp.where(diag, a, 0.0)
    off = jnp.where(diag, 0.0, a)
    b2 = _dot3(bd, bd)
    d1 = _dot3(eye - bd, eye + b2)
    yield
    b4 = _dot3(b2, b2)
    yield
    b8 = _dot3(b4, b4)
    d2 = _dot3(d1, eye + b4)
    yield
    dinv = _dot3(d2, eye + b8)
    yield
    n = _dot3(dinv, off)
    yield
    powers = [n]
    while 16 * 2 ** len(powers) < q:
        powers.append(_dot3(powers[-1], powers[-1]))
        yield
    m = dinv
    for pw in reversed(powers[1:]):
        m = _dot3(eye + pw, m)
        yield
    return _dot3(eye - n, m)


@jax.custom_vjp
def _solve_with(xinv, a, rhs):
    del a
    return _dot3(xinv, rhs)


def _solve_with_fwd(xinv, a, rhs):
    t = _dot3(xinv, rhs)
    return t, (xinv, t)


def _solve_with_bwd(res, dt):
    xinv, t = res
    d_rhs = _dot3(xinv, dt, TN)
    d_a = -_dot(d_rhs, t, NT)
    return jnp.zeros_like(xinv), d_a, d_rhs


_solve_with.defvjp(_solve_with_fwd, _solve_with_bwd)


_GDN_K = GDN_HEADS * GDN_HEAD
_GDN_V = 2 * GDN_HEADS * GDN_HEAD


def gdn_chunk(qh, kh, vh, beta_all, dec, state, h, xinv=None):
    r, c = dec.r, dec.c
    qn = qh * lax.rsqrt(jnp.sum(qh * qh, axis=-1, keepdims=True) + EPS) * (GDN_HEAD ** -0.5)
    kn = kh * lax.rsqrt(jnp.sum(kh * kh, axis=-1, keepdims=True) + EPS)
    beta = _lane_pick(beta_all, SM_B + h)
    decay = dec.mask(SM_A + h)
    yield
    kk = _dot(kn, kn, NT)
    qk = _dot(qn, kn, NT) * decay
    amat = jnp.where(r > c, kk * decay * beta, 0.0)
    eg = _lane_pick(dec.e_cum, SM_A + h)
    rhs = jnp.concatenate([vh * beta, kn * (beta * eg)], axis=1)
    qs = _dot(qn * eg, state)
    yield
    if xinv is None:
        xinv = yield from tri_inverse(amat)
        t = _dot3(xinv, rhs)
    else:
        t = _solve_with(xinv, amat, rhs)
    yield
    u, w = t[:, :GDN_HEAD], t[:, GDN_HEAD:]
    v_new = u - _dot(w, state)
    yield
    o = qs + _dot(qk, v_new)
    new_state = (state * _lane_pick(dec.e_tot, SM_A + h)
                 + _dot(kn * _lane_pick(dec.e_rest, SM_A + h), v_new, TN))
    return o, new_state, xinv


def gdn_heads(qs, ks, vs, small, alog, dtb, states, xinvs=None):
    nh = len(qs)
    beta_all = _sigmoid(small)
    dec = _Decay(-jnp.exp(alog) * _softplus(small + dtb))
    res = _interleave([gdn_chunk(qs[h], ks[h], vs[h], beta_all, dec, states[h], h,
                                 None if xinvs is None else xinvs[h]) for h in range(nh)])
    return tuple(o for o, _, _ in res), tuple(s for _, s, _ in res), tuple(x for _, _, x in res)


def _acc(ref, val, first):
    @pl.when(first)
    def _():
        ref[...] = val

    @pl.when(jnp.logical_not(first))
    def _():
        ref[...] += val


def ssd_scan_fwd(name, xbc, proj, dtb, alog, dsk):
    t = xbc.shape[0]
    nc, npair = t // SSD_CHUNK, SSM_HEADS // 2
    small_blk = AL_SMALL // LANE

    def body(xbc_ref, sm_ref, dtb_ref, alog_ref, dsk_ref, y_ref, sin_ref, st_ref):
        ci = pl.program_id(0)

        @pl.when(ci == 0)
        def _():
            st_ref[...] = jnp.zeros_like(st_ref)

        s_in = tuple(st_ref[p] for p in range(npair))
        ys, s_new = ssd_pairs(tuple(xbc_ref[:, p * LANE:(p + 1) * LANE] for p in range(npair)),
                              tuple(xbc_ref[:, _SSD_B + g * LANE:_SSD_B + (g + 1) * LANE] for g in range(2)),
                              tuple(xbc_ref[:, _SSD_C + g * LANE:_SSD_C + (g + 1) * LANE] for g in range(2)),
                              sm_ref[...], dtb_ref[...], alog_ref[...], dsk_ref[...], s_in)
        for p in range(npair):
            sin_ref[0, p] = s_in[p]
            y_ref[:, p * LANE:(p + 1) * LANE] = ys[p]
            st_ref[p] = s_new[p]

    par = pl.BlockSpec((1, LANE), lambda ci: (0, 0))
    return pl.pallas_call(
        body, name=name,
        out_shape=[jax.ShapeDtypeStruct((t, SSM_D_INNER), F32),
                   jax.ShapeDtypeStruct((nc, npair, LANE, LANE), F32)],
        grid=(nc,),
        in_specs=[pl.BlockSpec((SSD_CHUNK, SSM_CONV_DIM), lambda ci: (ci, 0)),
                  pl.BlockSpec((SSD_CHUNK, LANE), lambda ci: (ci, small_blk)),
                  par, par, par],
        out_specs=[pl.BlockSpec((SSD_CHUNK, SSM_D_INNER), lambda ci: (ci, 0)),
                   pl.BlockSpec((1, npair, LANE, LANE), lambda ci: (ci, 0, 0, 0))],
        scratch_shapes=[pltpu.VMEM((npair, LANE, LANE), F32)],
        compiler_params=_cparams(("arbitrary",)),
    )(xbc, proj, dtb, alog, dsk)


def ssd_scan_bwd(name, xbc, proj, dtb, alog, dsk, s_in, dy):
    t = xbc.shape[0]
    nc, npair = t // SSD_CHUNK, SSM_HEADS // 2
    small_blk = AL_SMALL // LANE

    def body(xbc_ref, sm_ref, dtb_ref, alog_ref, dsk_ref, sin_ref, dy_ref,
             dxbc_ref, dsm_ref, ddtb_ref, dalog_ref, ddsk_ref, dst_ref):
        ci = pl.program_id(0)

        @pl.when(ci == 0)
        def _():
            dst_ref[...] = jnp.zeros_like(dst_ref)

        _, vjp = jax.vjp(ssd_pairs, tuple(xbc_ref[:, p * LANE:(p + 1) * LANE] for p in range(npair)),
                         tuple(xbc_ref[:, _SSD_B + g * LANE:_SSD_B + (g + 1) * LANE] for g in range(2)),
                         tuple(xbc_ref[:, _SSD_C + g * LANE:_SSD_C + (g + 1) * LANE] for g in range(2)),
                         sm_ref[...], dtb_ref[...], alog_ref[...], dsk_ref[...],
                         tuple(sin_ref[0, p] for p in range(npair)))
        dxs, dbms, dcms, dsm, ddtb, dalog, ddsk, dsts = vjp(
            (tuple(dy_ref[:, p * LANE:(p + 1) * LANE] for p in range(npair)),
             tuple(dst_ref[p] for p in range(npair))))
        for p in range(npair):
            dxbc_ref[:, p * LANE:(p + 1) * LANE] = dxs[p]
            dst_ref[p] = dsts[p]
        for g in range(2):
            dxbc_ref[:, _SSD_B + g * LANE:_SSD_B + (g + 1) * LANE] = dbms[g]
            dxbc_ref[:, _SSD_C + g * LANE:_SSD_C + (g + 1) * LANE] = dcms[g]
        dsm_ref[...] = dsm
        _acc(ddtb_ref, ddtb, ci == 0)
        _acc(dalog_ref, dalog, ci == 0)
        _acc(ddsk_ref, ddsk, ci == 0)

    par = pl.BlockSpec((1, LANE), lambda ci: (0, 0))
    rev = lambda ci: nc - 1 - ci
    return pl.pallas_call(
        body, name=name,
        out_shape=[jax.ShapeDtypeStruct((t, SSM_CONV_DIM), F32),
                   jax.ShapeDtypeStruct((t, LANE), F32),
                   jax.ShapeDtypeStruct((1, LANE), F32), jax.ShapeDtypeStruct((1, LANE), F32),
                   jax.ShapeDtypeStruct((1, LANE), F32)],
        grid=(nc,),
        in_specs=[pl.BlockSpec((SSD_CHUNK, SSM_CONV_DIM), lambda ci: (rev(ci), 0)),
                  pl.BlockSpec((SSD_CHUNK, LANE), lambda ci: (rev(ci), small_blk)),
                  par, par, par,
                  pl.BlockSpec((1, npair, LANE, LANE), lambda ci: (rev(ci), 0, 0, 0)),
                  pl.BlockSpec((SSD_CHUNK, SSM_D_INNER), lambda ci: (rev(ci), 0))],
        out_specs=[pl.BlockSpec((SSD_CHUNK, SSM_CONV_DIM), lambda ci: (rev(ci), 0)),
                   pl.BlockSpec((SSD_CHUNK, LANE), lambda ci: (rev(ci), 0)),
                   par, par, par],
        scratch_shapes=[pltpu.VMEM((npair, LANE, LANE), F32)],
        compiler_params=_cparams(("arbitrary",)),
    )(xbc, proj, dtb, alog, dsk, s_in, dy)


def gdn_scan_fwd(name, qkv, proj, alog, dtb):
    t = qkv.shape[0]
    nc, nh = t // CHUNK, GDN_HEADS
    small_blk = AL_SMALL // LANE

    def body(qkv_ref, sm_ref, alog_ref, dtb_ref, o_ref, sin_ref, x_ref, st_ref):
        ci = pl.program_id(0)

        @pl.when(ci == 0)
        def _():
            st_ref[...] = jnp.zeros_like(st_ref)

        s_in = tuple(st_ref[h] for h in range(nh))
        os, s_new, xinvs = gdn_heads(
            tuple(qkv_ref[:, h * LANE:(h + 1) * LANE] for h in range(nh)),
            tuple(qkv_ref[:, _GDN_K + h * LANE:_GDN_K + (h + 1) * LANE] for h in range(nh)),
            tuple(qkv_ref[:, _GDN_V + h * LANE:_GDN_V + (h + 1) * LANE] for h in range(nh)),
            sm_ref[...], alog_ref[...], dtb_ref[...], s_in)
        for h in range(nh):
            sin_ref[0, h] = s_in[h]
            o_ref[:, h * LANE:(h + 1) * LANE] = os[h]
            x_ref[0, h] = xinvs[h]
            st_ref[h] = s_new[h]

    par = pl.BlockSpec((1, LANE), lambda ci: (0, 0))
    return pl.pallas_call(
        body, name=name,
        out_shape=[jax.ShapeDtypeStruct((t, GDN_HEADS * GDN_HEAD), F32),
                   jax.ShapeDtypeStruct((nc, nh, LANE, LANE), F32),
                   jax.ShapeDtypeStruct((nc, nh, CHUNK, CHUNK), F32)],
        grid=(nc,),
        in_specs=[pl.BlockSpec((CHUNK, GDN_QKV_DIM), lambda ci: (ci, 0)),
                  pl.BlockSpec((CHUNK, LANE), lambda ci: (ci, small_blk)),
                  par, par],
        out_specs=[pl.BlockSpec((CHUNK, GDN_HEADS * GDN_HEAD), lambda ci: (ci, 0)),
                   pl.BlockSpec((1, nh, LANE, LANE), lambda ci: (ci, 0, 0, 0)),
                   pl.BlockSpec((1, nh, CHUNK, CHUNK), lambda ci: (ci, 0, 0, 0))],
        scratch_shapes=[pltpu.VMEM((nh, LANE, LANE), F32)],
        compiler_params=_cparams(("arbitrary",)),
    )(qkv, proj, alog, dtb)


def gdn_scan_bwd(name, qkv, proj, alog, dtb, s_in, xinv, do, dsm_in):
    t = qkv.shape[0]
    nc, nh = t // CHUNK, GDN_HEADS
    small_blk = AL_SMALL // LANE

    def body(qkv_ref, sm_ref, alog_ref, dtb_ref, sin_ref, x_ref, do_ref, dsmi_ref,
             dqkv_ref, dsm_ref, dalog_ref, ddtb_ref, dst_ref):
        ci = pl.program_id(0)

        @pl.when(ci == 0)
        def _():
            dst_ref[...] = jnp.zeros_like(dst_ref)

        xis = tuple(x_ref[0, h] for h in range(nh))

        def fn(qs, ks, vs, sm, alog_, dtb_, sts):
            os, s_new, _ = gdn_heads(qs, ks, vs, sm, alog_, dtb_, sts, xinvs=xis)
            return os, s_new

        _, vjp = jax.vjp(fn, tuple(qkv_ref[:, h * LANE:(h + 1) * LANE] for h in range(nh)),
                         tuple(qkv_ref[:, _GDN_K + h * LANE:_GDN_K + (h + 1) * LANE] for h in range(nh)),
                         tuple(qkv_ref[:, _GDN_V + h * LANE:_GDN_V + (h + 1) * LANE] for h in range(nh)),
                         sm_ref[...], alog_ref[...], dtb_ref[...], tuple(sin_ref[0, h] for h in range(nh)))
        dqs, dks, dvs, dsm, dalog, ddtb, dsts = vjp(
            (tuple(do_ref[:, h * LANE:(h + 1) * LANE] for h in range(nh)), tuple(dst_ref[h] for h in range(nh))))
        for h in range(nh):
            dqkv_ref[:, h * LANE:(h + 1) * LANE] = dqs[h]
            dqkv_ref[:, _GDN_K + h * LANE:_GDN_K + (h + 1) * LANE] = dks[h]
            dqkv_ref[:, _GDN_V + h * LANE:_GDN_V + (h + 1) * LANE] = dvs[h]
            dst_ref[h] = dsts[h]
        dsm_ref[...] = dsmi_ref[...] + dsm
        _acc(dalog_ref, dalog, ci == 0)
        _acc(ddtb_ref, ddtb, ci == 0)

    par = pl.BlockSpec((1, LANE), lambda ci: (0, 0))
    rev = lambda ci: nc - 1 - ci
    return pl.pallas_call(
        body, name=name,
        out_shape=[jax.ShapeDtypeStruct((t, GDN_QKV_DIM), F32), jax.ShapeDtypeStruct((t, LANE), F32),
                   jax.ShapeDtypeStruct((1, LANE), F32), jax.ShapeDtypeStruct((1, LANE), F32)],
        grid=(nc,),
        in_specs=[pl.BlockSpec((CHUNK, GDN_QKV_DIM), lambda ci: (rev(ci), 0)),
                  pl.BlockSpec((CHUNK, LANE), lambda ci: (rev(ci), small_blk)),
                  par, par,
                  pl.BlockSpec((1, nh, LANE, LANE), lambda ci: (rev(ci), 0, 0, 0)),
                  pl.BlockSpec((1, nh, CHUNK, CHUNK), lambda ci: (rev(ci), 0, 0, 0)),
                  pl.BlockSpec((CHUNK, GDN_HEADS * GDN_HEAD), lambda ci: (rev(ci), 0)),
                  pl.BlockSpec((CHUNK, LANE), lambda ci: (rev(ci), 0))],
        out_specs=[pl.BlockSpec((CHUNK, GDN_QKV_DIM), lambda ci: (rev(ci), 0)),
                   pl.BlockSpec((CHUNK, LANE), lambda ci: (rev(ci), 0)),
                   par, par],
        scratch_shapes=[pltpu.VMEM((nh, LANE, LANE), F32)],
        compiler_params=_cparams(("arbitrary",)),
    )(qkv, proj, alog, dtb, s_in, xinv, do, dsm_in)


def _row(arr, w, c0=0, moves=False):
    return (arr, "row", w, c0, moves)


def _par(arr, w, c0=0, moves=False):
    return (arr, "par", w, c0, moves)


def matmul_add(name, a, b, res):
    (m, k), (_, n) = a.shape, b.shape
    tm, tn, tk = _matmul_tiles(m, n, n, k, a.dtype.itemsize, b.dtype.itemsize, 4 + res.dtype.itemsize)
    nk = k // tk

    def body_acc(a_ref, b_ref, r_ref, o_ref, acc_ref):
        kk = pl.program_id(2)

        @pl.when(kk == 0)
        def _():
            acc_ref[...] = r_ref[...]

        acc_ref[...] += _dot(a_ref[...], b_ref[...])

        @pl.when(kk == nk - 1)
        def _():
            o_ref[...] = acc_ref[...]

    def body_one(a_ref, b_ref, r_ref, o_ref):
        o_ref[...] = r_ref[...] + _dot(a_ref[...], b_ref[...])

    body = body_one if nk == 1 else body_acc
    return pl.pallas_call(
        body, name=name,
        out_shape=jax.ShapeDtypeStruct((m, n), F32),
        grid=(m // tm, n // tn, nk),
        in_specs=[pl.BlockSpec((tm, tk), lambda i, j, kk: (i, kk)),
                  pl.BlockSpec((tk, tn), lambda i, j, kk: (kk, j)),
                  pl.BlockSpec((tm, tn), lambda i, j, kk: (i, j))],
        out_specs=pl.BlockSpec((tm, tn), lambda i, j, kk: (i, j)),
        scratch_shapes=[] if nk == 1 else [pltpu.VMEM((tm, tn), F32)],
        compiler_params=_cparams(("parallel", "parallel", "arbitrary")),
    )(a, b, res)


def layer_fwd(l, x, w):
    t = x.shape[0]
    rt = min(256, t)
    s = {"x": x}
    s["h"] = rowwise_fwd(f"norm_mix_l{l}", f_rmsnorm, t, rt, 1,
                         [_row(x, D_MODEL), _par(w["norm_mix_w"], D_MODEL)], [(D_MODEL, BF16)])[0]
    s["proj"] = matmul(f"in_proj_l{l}", s["h"], w["w_in"], "nn")
    s["xbc"] = conv_fwd(f"ssm_conv_l{l}", s["proj"], AL_XBC, SSM_CONV_DIM, w["ssm_conv_w"], w["ssm_conv_b"],
                        tile=min(512, t))
    s["qkv"] = conv_fwd(f"gdn_conv_l{l}", s["proj"], AL_QKV, GDN_QKV_DIM, w["gdn_conv_w"], w["gdn_conv_b"],
                        tile=min(512, t))
    s["y_scan"], s["ssd_sin"] = ssd_scan_fwd(f"ssd_scan_l{l}", s["xbc"], s["proj"], w["ssm_dt_bias"],
                                             w["ssm_a_log"], w["ssm_d"])
    s["o_scan"], s["gdn_sin"], s["gdn_x"] = gdn_scan_fwd(f"gdn_scan_l{l}", s["qkv"], s["proj"],
                                                         w["gdn_a_log"], w["gdn_dt_bias"])
    s["y_ssm"] = rowwise_fwd(f"ssd_post_l{l}", f_ssd_post, t, rt, 2,
                             [_row(s["y_scan"], 512, 0, True), _row(s["proj"], 512, AL_Z // 512, True),
                              _par(w["ssm_norm_w"], 512, 0, True)], [(512, BF16)])[0]
    s["y_gdn"] = rowwise_fwd(f"gdn_post_l{l}", f_gdn_post, t, rt, GDN_HEADS,
                             [_row(s["o_scan"], LANE, 0, True), _row(s["proj"], LANE, AL_GZ // LANE, True),
                              _par(w["gdn_norm_w"], LANE)], [(LANE, BF16)])[0]
    if "late_weights" in w:
        late = w["late_weights"](s["y_gdn"])
        w = {k: v for k, v in {**w, **late}.items() if k != "late_weights"}
    s["w"] = w
    s["p1"] = matmul(f"proj_ssm_l{l}", s["y_ssm"], w["w_proj_ssm"], "nn")
    s["p2"] = matmul(f"proj_gdn_l{l}", s["y_gdn"], w["w_proj_gdn"], "nn")
    s["merged"] = rowwise_fwd(f"merge_l{l}", f_merge, t, rt, 2,
                              [_row(s["proj"], 512, AL_GS // 512, True), _row(s["p1"], 512, 0, True),
                               _row(s["proj"], 512, AL_GG // 512, True), _row(s["p2"], 512, 0, True)],
                              [(512, BF16)])[0]
    s["x1"] = matmul_add(f"out_proj_l{l}", s["merged"], w["w_out"], x)
    s["h2"] = rowwise_fwd(f"norm_ffn_l{l}", f_rmsnorm, t, rt, 1,
                          [_row(s["x1"], D_MODEL), _par(w["norm_ffn_w"], D_MODEL)], [(D_MODEL, BF16)])[0]
    s["gu"] = matmul(f"ffn_in_l{l}", s["h2"], w["w_ffn_in"], "nn")
    s["act"] = rowwise_fwd(f"swiglu_l{l}", f_swiglu, t, rt, FFN_HIDDEN // 256,
                           [_row(s["gu"], 256, 0, True), _row(s["gu"], 256, FFN_HIDDEN // 256, True)],
                           [(256, BF16)])[0]
    x2 = matmul_add(f"ffn_down_l{l}", s["act"], w["w_ffn_down"], s["x1"])
    return x2, s


IN_SHARD = IN_DIM // 4
IN_SHARD_PAD = 2304


def _aligned_to_shards(g):
    orig = jnp.concatenate([g[:, 0:2560], g[:, AL_SMALL:AL_SMALL + 16], g[:, 2560:6656],
                            g[:, AL_SMALL + 16:AL_SMALL + 32], g[:, 6656:8704]], axis=1)
    return jnp.stack([jnp.pad(orig[:, j * IN_SHARD:(j + 1) * IN_SHARD], ((0, 0), (0, IN_SHARD_PAD - IN_SHARD)))
                      for j in range(N_CHIPS)])


def layer_bwd(l, dx2, w, s):
    t = dx2.shape[0]
    rt = min(256, t)
    ct = min(512, t)
    g = {}
    dact = matmul(f"ffn_down_dx_l{l}", dx2, w["w_ffn_down"], "nt")
    g["w_ffn_down"] = matmul(f"ffn_down_dw_l{l}", s["act"], dx2, "tn")
    nf = FFN_HIDDEN // 256
    dgate, dup = rowwise_bwd(f"swiglu_bwd_l{l}", f_swiglu, t, rt, nf,
                             [_row(s["gu"], 256, 0, True), _row(s["gu"], 256, nf, True)], [True, True],
                             [_row(dact, 256, 0, True)], row_dtypes={0: BF16, 1: BF16})
    dgu = jnp.concatenate([dgate, dup], axis=1)
    dh2 = matmul(f"ffn_in_dx_l{l}", dgu, w["w_ffn_in"], "nt")
    g["w_ffn_in"] = matmul(f"ffn_in_dw_l{l}", s["h2"], dgu, "tn", chip_major=True)
    dx1, g["norm_ffn_w"] = rowwise_bwd(f"norm_ffn_bwd_l{l}", f_rmsnorm, t, rt, 1,
                                       [_row(s["x1"], D_MODEL), _par(w["norm_ffn_w"], D_MODEL)], [True, True],
                                       [_row(dh2, D_MODEL)], addends={0: _row(dx2, D_MODEL)})
    dmerged = matmul(f"out_proj_dx_l{l}", dx1, w["w_out"], "nt")
    g["w_out"] = matmul(f"out_proj_dw_l{l}", s["merged"], dx1, "tn")
    dgs, dp1, dgg, dp2 = rowwise_bwd(
        f"merge_bwd_l{l}", f_merge, t, rt, 2,
        [_row(s["proj"], 512, AL_GS // 512, True), _row(s["p1"], 512, 0, True),
         _row(s["proj"], 512, AL_GG // 512, True), _row(s["p2"], 512, 0, True)], [True] * 4,
        [_row(dmerged, 512, 0, True)], row_dtypes={0: BF16, 1: BF16, 2: BF16, 3: BF16})
    dy_ssm = matmul(f"proj_ssm_dx_l{l}", dp1, w["w_proj_ssm"], "nt")
    g["w_proj_ssm"] = matmul(f"proj_ssm_dw_l{l}", s["y_ssm"], dp1, "tn")
    dy_gdn = matmul(f"proj_gdn_dx_l{l}", dp2, w["w_proj_gdn"], "nt")
    g["w_proj_gdn"] = matmul(f"proj_gdn_dw_l{l}", s["y_gdn"], dp2, "tn")
    dy_scan, dz, g["ssm_norm_w"] = rowwise_bwd(
        f"ssd_post_bwd_l{l}", f_ssd_post, t, rt, 2,
        [_row(s["y_scan"], 512, 0, True), _row(s["proj"], 512, AL_Z // 512, True),
         _par(w["ssm_norm_w"], 512, 0, True)], [True] * 3, [_row(dy_ssm, 512, 0, True)], row_dtypes={1: BF16})
    dxbc_act, dsm, g["ssm_dt_bias"], g["ssm_a_log"], g["ssm_d"] = ssd_scan_bwd(
        f"ssd_scan_bwd_l{l}", s["xbc"], s["proj"], w["ssm_dt_bias"], w["ssm_a_log"], w["ssm_d"],
        s["ssd_sin"], dy_scan)
    dxbc, g["ssm_conv_w"], g["ssm_conv_b"] = conv_bwd(
        f"ssm_conv_bwd_l{l}", s["proj"], AL_XBC, SSM_CONV_DIM, w["ssm_conv_w"], w["ssm_conv_b"], dxbc_act, tile=ct)
    do_scan, dgz, g["gdn_norm_w"] = rowwise_bwd(
        f"gdn_post_bwd_l{l}", f_gdn_post, t, rt, GDN_HEADS,
        [_row(s["o_scan"], LANE, 0, True), _row(s["proj"], LANE, AL_GZ // LANE, True),
         _par(w["gdn_norm_w"], LANE)], [True] * 3, [_row(dy_gdn, LANE, 0, True)], row_dtypes={1: BF16})
    dqkv_act, dsm, g["gdn_a_log"], g["gdn_dt_bias"] = gdn_scan_bwd(
        f"gdn_scan_bwd_l{l}", s["qkv"], s["proj"], w["gdn_a_log"], w["gdn_dt_bias"], s["gdn_sin"],
        s["gdn_x"], do_scan, dsm)
    dqkv, g["gdn_conv_w"], _ = conv_bwd(
        f"gdn_conv_bwd_l{l}", s["proj"], AL_QKV, GDN_QKV_DIM, w["gdn_conv_w"], w["gdn_conv_b"], dqkv_act, tile=ct)
    dproj = jnp.concatenate([dz, dxbc, dqkv, dgz, dgs, dgg, dsm.astype(BF16),
                             jnp.zeros((t, AL_DIM - AL_SMALL - LANE), BF16)], axis=1)
    dh = matmul(f"in_proj_dx_l{l}", dproj, w["w_in"], "nt")
    g["w_in"] = matmul(f"in_proj_dw_l{l}", s["h"], dproj, "tn")
    dx0, g["norm_mix_w"] = rowwise_bwd(f"norm_mix_bwd_l{l}", f_rmsnorm, t, rt, 1,
                                       [_row(s["x"], D_MODEL), _par(w["norm_mix_w"], D_MODEL)], [True, True],
                                       [_row(dh, D_MODEL)], addends={0: _row(dx1, D_MODEL)})
    return dx0, g


def _align_w_in(w):
    pad = jnp.zeros((w.shape[0], AL_DIM - AL_SMALL - 32), w.dtype)
    return jnp.concatenate([w[:, 0:2560], w[:, 2576:6672], w[:, 6688:8736],
                            w[:, 2560:2576], w[:, 6672:6688], pad], axis=1)


def _pad_lane(v, at=0):
    return jnp.pad(v[None], ((0, 0), (at, LANE - at - v.shape[0])))


def local_step(x, target, full, gathered_of=None, on_layer_grads=None):
    if gathered_of is None:
        gathered_of = lambda l, after: {n: full[n][l] for n, _ in SHARDED}
    ws, saved = [], []
    h = x
    for l in range(DEPTH):
        gw = gathered_of(l, h)
        ws.append({
            "norm_mix_w": full["norm_mix_w"][l][None], "w_in": _align_w_in(gw["w_in"]),
            "ssm_conv_w": gw["ssm_conv_w"], "ssm_conv_b": full["ssm_conv_b"][l][None],
            "ssm_dt_bias": _pad_lane(full["ssm_dt_bias"][l]), "ssm_a_log": _pad_lane(full["ssm_a_log"][l]),
            "ssm_d": _pad_lane(full["ssm_d"][l]), "ssm_norm_w": full["ssm_norm_w"][l][None],
            "gdn_conv_w": gw["gdn_conv_w"], "gdn_conv_b": jnp.zeros((1, GDN_QKV_DIM), F32),
            "gdn_a_log": _pad_lane(full["gdn_a_log"][l], SM_A),
            "gdn_dt_bias": _pad_lane(full["gdn_dt_bias"][l], SM_A),
            "gdn_norm_w": full["gdn_norm_w"][l][None],
            "norm_ffn_w": full["norm_ffn_w"][l][None],
            **{n: gw[n] for n in ("w_proj_ssm", "w_proj_gdn", "w_out", "w_ffn_in", "w_ffn_down", "late_weights")
               if n in gw},
        })
        h, s = layer_fwd(l, h, ws[l])
        ws[l] = s.pop("w")
        saved.append(s)
    loss, dx, g_final = final_loss("final_loss", h, target, full["final_norm_w"][None], tile=min(256, x.shape[0]))
    per_layer = [None] * DEPTH
    matmul_grads = [None] * DEPTH
    for l in reversed(range(DEPTH)):
        dx, per_layer[l] = layer_bwd(l, dx, ws[l], saved[l])
        matmul_grads[l] = {n: per_layer[l].pop(n) for n, _ in BIG}
        if on_layer_grads is not None:
            dx = on_layer_grads(l, matmul_grads[l], dx)
    grads = {"final_norm_w": g_final[0]}
    if on_layer_grads is None:
        grads.update({n: jnp.stack([matmul_grads[l][n] for l in range(DEPTH)]) for n, _ in BIG})
    for name in per_layer[0]:
        rows = []
        for l in range(DEPTH):
            gl = per_layer[l][name]
            if name in ("ssm_dt_bias", "ssm_a_log", "ssm_d"):
                gl = gl[0, :SSM_HEADS]
            elif name in ("gdn_a_log", "gdn_dt_bias"):
                gl = gl[0, SM_A:SM_A + GDN_HEADS]
            elif name in ("norm_mix_w", "ssm_conv_b", "ssm_norm_w", "gdn_norm_w", "norm_ffn_w"):
                gl = gl[0]
            rows.append(gl)
        grads[name] = jnp.stack(rows)
    return loss, dx, grads


MESH = pl.DeviceIdType.MESH
HBM = pl.BlockSpec(memory_space=pltpu.HBM)
N_DEV = 8


def _pos():
    return lax.axis_index("x"), lax.axis_index("y"), lax.axis_index("c")


def _rcopy(src, dst, send_sem, recv_sem, dev):
    return pltpu.make_async_remote_copy(src_ref=src, dst_ref=dst, send_sem=send_sem, recv_sem=recv_sem,
                                        device_id=dev, device_id_type=MESH)


RELATIONS = (2, 1, 3)


def _related_chip(x, y, mask):
    return (1 - x if mask & 2 else x, 1 - y if mask & 1 else y)


def weights_gather(name, bufs):
    n = len(bufs)

    def body(*refs):
        outs, send_sems, recv_sems = refs[n:2 * n], refs[2 * n], refs[2 * n + 1]
        x, y, c = _pos()
        sib = (x, y, 1 - c)
        sends = []
        for i, a in enumerate(outs):
            for k, m in enumerate(RELATIONS):
                px, py = _related_chip(x, y, m)
                cp = _rcopy(a.at[0, c], a.at[m, c], send_sems.at[6 * i + k], recv_sems.at[6 * i + k], (px, py, c))
                cp.start()
                sends.append(cp)
        for i, a in enumerate(outs):
            for k, m in enumerate(RELATIONS):
                px, py = _related_chip(x, y, m)
                _rcopy(a.at[0, c], a.at[m, c], send_sems.at[6 * i + k], recv_sems.at[6 * i + k],
                       (px, py, c)).wait_recv()
                fw = _rcopy(a.at[m, c], a.at[m, c], send_sems.at[6 * i + 3 + k], recv_sems.at[6 * i + 3 + k], sib)
                fw.start()
                sends.append(fw)
        for i, a in enumerate(outs):
            for k, m in enumerate(RELATIONS):
                _rcopy(a.at[m, 1 - c], a.at[m, 1 - c], send_sems.at[6 * i + 3 + k], recv_sems.at[6 * i + 3 + k],
                       sib).wait_recv()
        for cp in sends:
            cp.wait_send()

    return pl.pallas_call(
        body, name=name, out_shape=[jax.ShapeDtypeStruct(b.shape, b.dtype) for b in bufs],
        in_specs=[HBM] * n, out_specs=[HBM] * n,
        input_output_aliases={i: i for i in range(n)},
        scratch_shapes=[pltpu.SemaphoreType.DMA((6 * n,)), pltpu.SemaphoreType.DMA((6 * n,))],
    )(*bufs)


SEM = pl.BlockSpec(memory_space=pltpu.SEMAPHORE)
DATAFLOW = pltpu.SideEffectType.DATAFLOW_SIDE_EFFECTING


def gather_start(name, bufs, after):
    n = len(bufs)

    def body(*refs):
        ins = refs[:n]
        send_sems, recv_sems, token = refs[n + 1], refs[n + 2], refs[2 * n + 3]
        x, y, c = _pos()
        for i, a in enumerate(ins):
            for k, m in enumerate(RELATIONS):
                px, py = _related_chip(x, y, m)
                _rcopy(a.at[0, c], a.at[m, c], send_sems.at[3 * i + k], recv_sems.at[3 * i + k], (px, py, c)).start()
        token[...] = jnp.zeros_like(token)

    res = pl.pallas_call(
        body, name=name,
        out_shape=(pltpu.SemaphoreType.DMA((3 * n,)), pltpu.SemaphoreType.DMA((3 * n,)),
                   *[pltpu.HBM(b.shape, b.dtype) for b in bufs], jax.ShapeDtypeStruct((8, LANE), F32)),
        in_specs=[HBM] * n + [pl.BlockSpec(memory_space=pl.ANY)],
        out_specs=(SEM, SEM, *[HBM] * n, pl.BlockSpec(memory_space=pltpu.VMEM)),
        input_output_aliases={i: 2 + i for i in range(n)},
        compiler_params=pltpu.CompilerParams(has_side_effects=DATAFLOW),
    )(*[pltpu.with_memory_space_constraint(b, pltpu.HBM) for b in bufs], after)
    return res[0], res[1], list(res[2:2 + n]), res[2 + n]


def gather_wait(name, send_sems, recv_sems, bufs, after):
    n = len(bufs)

    def body(*refs):
        ins, ssem, rsem = refs[:n], refs[n], refs[n + 1]
        x, y, c = _pos()
        for i, a in enumerate(ins):
            for k, m in enumerate(RELATIONS):
                px, py = _related_chip(x, y, m)
                cp = _rcopy(a.at[0, c], a.at[m, c], ssem.at[3 * i + k], rsem.at[3 * i + k], (px, py, c))
                cp.wait_send()
                cp.wait_recv()

    return pl.pallas_call(
        body, name=name,
        out_shape=[pltpu.HBM(b.shape, b.dtype) for b in bufs],
        in_specs=[HBM] * n + [SEM, SEM, pl.BlockSpec(memory_space=pl.ANY)],
        out_specs=[HBM] * n,
        input_output_aliases={i: i for i in range(n)},
        compiler_params=pltpu.CompilerParams(has_side_effects=DATAFLOW),
    )(*bufs, send_sems, recv_sems, after)


def weights_forward(name, bufs):
    n = len(bufs)

    def body(*refs):
        outs, send_sems, recv_sems = refs[n:2 * n], refs[2 * n], refs[2 * n + 1]
        x, y, c = _pos()
        sib = (x, y, 1 - c)
        sends = []
        for i, a in enumerate(outs):
            for k, m in enumerate(RELATIONS):
                fw = _rcopy(a.at[m, c], a.at[m, c], send_sems.at[3 * i + k], recv_sems.at[3 * i + k], sib)
                fw.start()
                sends.append(fw)
        for i, a in enumerate(outs):
            for k, m in enumerate(RELATIONS):
                _rcopy(a.at[m, 1 - c], a.at[m, 1 - c], send_sems.at[3 * i + k], recv_sems.at[3 * i + k],
                       sib).wait_recv()
        for cp in sends:
            cp.wait_send()

    return pl.pallas_call(
        body, name=name, out_shape=[jax.ShapeDtypeStruct(b.shape, b.dtype) for b in bufs],
        in_specs=[HBM] * n, out_specs=[HBM] * n,
        input_output_aliases={i: i for i in range(n)},
        scratch_shapes=[pltpu.SemaphoreType.DMA((3 * n,)), pltpu.SemaphoreType.DMA((3 * n,))],
    )(*bufs)


def pair_swap(name, gs):
    n = len(gs)
    offs = [0]
    for g in gs:
        offs.append(offs[-1] + g.shape[0])

    def body(*refs):
        srcs, outs, send_sems, recv_sems = refs[:n], refs[n:2 * n], refs[2 * n], refs[2 * n + 1]
        x, y, c = _pos()
        cps = [_rcopy(s.at[j, 1 - c], o.at[j], send_sems.at[offs[i] + j], recv_sems.at[offs[i] + j], (x, y, 1 - c))
               for i, (s, o) in enumerate(zip(srcs, outs)) for j in range(s.shape[0])]
        for cp in cps:
            cp.start()
        for cp in cps:
            cp.wait()

    return pl.pallas_call(
        body, name=name, out_shape=[jax.ShapeDtypeStruct(g.shape[:1] + g.shape[2:], g.dtype) for g in gs],
        in_specs=[HBM] * n, out_specs=[HBM] * n,
        scratch_shapes=[pltpu.SemaphoreType.DMA((offs[-1],)), pltpu.SemaphoreType.DMA((offs[-1],))],
    )(*gs)


def scatter_start(name, ss, after):
    n = len(ss)
    lands = [lax.empty((3,) + s.shape[1:], s.dtype) for s in ss]

    def body(*refs):
        srcs, dsts = refs[:n], refs[n:2 * n]
        send_sems, recv_sems, token = refs[2 * n + 1], refs[2 * n + 2], refs[4 * n + 3]
        x, y, c = _pos()
        for i, (s, o) in enumerate(zip(srcs, dsts)):
            for k, m in enumerate(RELATIONS):
                px, py = _related_chip(x, y, m)
                _rcopy(s.at[2 * px + py], o.at[k], send_sems.at[3 * i + k], recv_sems.at[3 * i + k],
                       (px, py, c)).start()
        token[...] = jnp.zeros_like(token)

    both = list(ss) + lands
    res = pl.pallas_call(
        body, name=name,
        out_shape=(pltpu.SemaphoreType.DMA((3 * n,)), pltpu.SemaphoreType.DMA((3 * n,)),
                   *[pltpu.HBM(b.shape, b.dtype) for b in both], jax.ShapeDtypeStruct((8, LANE), F32)),
        in_specs=[HBM] * (2 * n) + [pl.BlockSpec(memory_space=pl.ANY)],
        out_specs=(SEM, SEM, *[HBM] * (2 * n), pl.BlockSpec(memory_space=pltpu.VMEM)),
        input_output_aliases={i: 2 + i for i in range(2 * n)},
        compiler_params=pltpu.CompilerParams(has_side_effects=DATAFLOW),
    )(*[pltpu.with_memory_space_constraint(b, pltpu.HBM) for b in both], after)
    return res[0], res[1], list(res[2:2 + n]), list(res[2 + n:2 + 2 * n]), res[2 + 2 * n]


def scatter_wait(name, send_sems, recv_sems, ss, lands, after):
    n = len(ss)

    def body(*refs):
        srcs, dsts, ssem, rsem = refs[:n], refs[n:2 * n], refs[2 * n], refs[2 * n + 1]
        x, y, c = _pos()
        for i, (s, o) in enumerate(zip(srcs, dsts)):
            for k, m in enumerate(RELATIONS):
                px, py = _related_chip(x, y, m)
                cp = _rcopy(s.at[2 * px + py], o.at[k], ssem.at[3 * i + k], rsem.at[3 * i + k], (px, py, c))
                cp.wait_send()
                cp.wait_recv()

    both = list(ss) + list(lands)
    res = pl.pallas_call(
        body, name=name,
        out_shape=[pltpu.HBM(b.shape, b.dtype) for b in both],
        in_specs=[HBM] * (2 * n) + [SEM, SEM, pl.BlockSpec(memory_space=pl.ANY)],
        out_specs=[HBM] * (2 * n),
        input_output_aliases={i: i for i in range(2 * n)},
        compiler_params=pltpu.CompilerParams(has_side_effects=DATAFLOW),
    )(*both, send_sems, recv_sems, after)
    return list(res[:n]), list(res[n:])


def pair_share(name, bufs):
    n = len(bufs)

    def body(*refs):
        outs, send_sems, recv_sems = refs[n:2 * n], refs[2 * n], refs[2 * n + 1]
        x, y, c = _pos()
        sends = []
        for i, o in enumerate(outs):
            cp = _rcopy(o.at[c], o.at[c], send_sems.at[i], recv_sems.at[i], (x, y, 1 - c))
            cp.start()
            sends.append(cp)
        for i, o in enumerate(outs):
            _rcopy(o.at[1 - c], o.at[1 - c], send_sems.at[i], recv_sems.at[i], (x, y, 1 - c)).wait_recv()
        for cp in sends:
            cp.wait_send()

    return pl.pallas_call(
        body, name=name, out_shape=[jax.ShapeDtypeStruct(b.shape, b.dtype) for b in bufs],
        in_specs=[HBM] * n, out_specs=[HBM] * n,
        input_output_aliases={i: i for i in range(n)},
        scratch_shapes=[pltpu.SemaphoreType.DMA((n,)), pltpu.SemaphoreType.DMA((n,))],
    )(*bufs)


def all_allgather(name, buf):
    r, cd = buf.shape

    def body(src, out, send_sems, recv_sems, lsem):
        x, y, c = _pos()
        me = 4 * x + 2 * y + c
        local = pltpu.make_async_copy(src, out.at[me], lsem)
        local.start()

        def peer(mask):
            px = 1 - x if mask & 4 else x
            py = 1 - y if mask & 2 else y
            pc = 1 - c if mask & 1 else c
            return px, py, pc

        sends = []
        for mask in range(1, N_DEV):
            cp = _rcopy(src, out.at[me], send_sems.at[mask - 1], recv_sems.at[mask - 1], peer(mask))
            cp.start()
            sends.append(cp)
        for mask in range(1, N_DEV):
            px, py, pc = peer(mask)
            _rcopy(src, out.at[4 * px + 2 * py + pc], send_sems.at[mask - 1], recv_sems.at[mask - 1],
                   (px, py, pc)).wait_recv()
        for cp in sends:
            cp.wait_send()
        local.wait()

    return pl.pallas_call(
        body, name=name, out_shape=jax.ShapeDtypeStruct((N_DEV, r, cd), buf.dtype),
        in_specs=[HBM], out_specs=HBM,
        scratch_shapes=[pltpu.SemaphoreType.DMA((N_DEV - 1,)), pltpu.SemaphoreType.DMA((N_DEV - 1,)),
                        pltpu.SemaphoreType.DMA(())],
    )(buf)


ELEMENTWISE_BLOCK_BYTES = 2 << 20


def _row_block(rows, cols):
    for cand in (1024, 512, 256, 128, 64, 32, 16):
        if rows % cand == 0 and cand * cols * 4 <= ELEMENTWISE_BLOCK_BYTES:
            return cand
    return rows


def chip_sum(name, s, r, me, c):
    _, a, b = s.shape
    tr = _row_block(a, b)

    def body(idx_ref, s_ref, r_ref, o_ref):
        del idx_ref
        acc = s_ref[...].astype(F32)
        for k in range(3):
            acc = acc + r_ref[k].astype(F32)
        o_ref[...] = acc

    return pl.pallas_call(
        body, name=name, out_shape=jax.ShapeDtypeStruct((2, a, b), F32),
        grid_spec=pltpu.PrefetchScalarGridSpec(
            num_scalar_prefetch=1, grid=(a // tr,),
            in_specs=[pl.BlockSpec((None, tr, b), lambda i, idx: (idx[0], i, 0)),
                      pl.BlockSpec((3, tr, b), lambda i, idx: (0, i, 0))],
            out_specs=pl.BlockSpec((None, tr, b), lambda i, idx: (idx[1], i, 0))),
        compiler_params=_cparams(("arbitrary",)),
    )(jnp.stack([me, c]).astype(jnp.int32), s, r)


def pair_add(name, p, recv, c):
    nj, _, rh, cd = p.shape
    tr = _row_block(rh, cd)

    def body(c_ref, p_ref, r_ref, o_ref):
        del c_ref
        o_ref[...] = (p_ref[0] + r_ref[...]).astype(o_ref.dtype)

    return pl.pallas_call(
        body, name=name, out_shape=jax.ShapeDtypeStruct((nj, rh, cd), BF16),
        grid_spec=pltpu.PrefetchScalarGridSpec(
            num_scalar_prefetch=1, grid=(nj, rh // tr),
            in_specs=[pl.BlockSpec((1, 1, tr, cd), lambda j, i, c_ref: (j, c_ref[0], i, 0)),
                      pl.BlockSpec((1, tr, cd), lambda j, i, c_ref: (j, i, 0))],
            out_specs=pl.BlockSpec((1, tr, cd), lambda j, i, c_ref: (j, i, 0))),
        compiler_params=_cparams(("arbitrary", "arbitrary")),
    )(jnp.reshape(c, (1,)).astype(jnp.int32), p, recv)


def slab_sum(name, a):
    n, r, cd = a.shape
    tr = _pick(r, (256, 128, 64, 32, 16, 8))

    def body(a_ref, o_ref):
        acc = a_ref[0].astype(F32)
        for j in range(1, n):
            acc = acc + a_ref[j].astype(F32)
        o_ref[...] = acc

    return pl.pallas_call(
        body, name=name, out_shape=jax.ShapeDtypeStruct((r, cd), F32),
        grid=(r // tr,),
        in_specs=[pl.BlockSpec((n, tr, cd), lambda i: (0, i, 0))],
        out_specs=pl.BlockSpec((tr, cd), lambda i: (i, 0)),
        compiler_params=_cparams(("arbitrary",)),
    )(a)


ADAM_C1 = 1.0 - ADAM_B1 ** ADAM_STEP
ADAM_C2 = 1.0 - ADAM_B2 ** ADAM_STEP


def adamw(name, w, g, m, v):
    r, cd = w.shape
    tr = r
    for cand in (512, 256, 128, 64, 32, 16, 8):
        if r % cand == 0 and cand * cd * 4 <= (1 << 20):
            tr = cand
            break

    def body(w_ref, g_ref, m_ref, v_ref, d_ref, nm_ref, nv_ref):
        gv = g_ref[...]
        nm = ADAM_B1 * m_ref[...] + (1.0 - ADAM_B1) * gv
        nv = ADAM_B2 * v_ref[...] + (1.0 - ADAM_B2) * (gv * gv)
        m_hat = nm / ADAM_C1
        v_hat = nv / ADAM_C2
        d_ref[...] = -ADAM_LR * (m_hat / (jnp.sqrt(v_hat) + ADAM_EPS) + ADAM_WD * w_ref[...])
        nm_ref[...] = nm
        nv_ref[...] = nv

    spec = pl.BlockSpec((tr, cd), lambda i: (i, 0))
    sd = jax.ShapeDtypeStruct((r, cd), F32)
    return pl.pallas_call(
        body, name=name, out_shape=[sd, sd, sd], grid=(r // tr,),
        in_specs=[spec] * 4, out_specs=[spec] * 3,
        compiler_params=_cparams(("arbitrary",)),
    )(w, g, m, v)


WEIGHTS = ("norm_mix_w", "w_in", "ssm_conv_w", "ssm_conv_b", "ssm_dt_bias", "ssm_a_log", "ssm_d", "ssm_norm_w",
           "gdn_conv_w", "gdn_a_log", "gdn_dt_bias", "gdn_norm_w", "w_proj_ssm", "w_proj_gdn", "w_out",
           "norm_ffn_w", "w_ffn_in", "w_ffn_down", "final_norm_w")
BIG = (("w_in", 2), ("w_proj_ssm", 1), ("w_proj_gdn", 1), ("w_out", 1), ("w_ffn_in", 2), ("w_ffn_down", 1))
CONVW = (("ssm_conv_w", 2), ("gdn_conv_w", 2))
SHARDED = BIG + CONVW
SMALL = tuple(n for n in WEIGHTS if n not in dict(SHARDED))


def _unpack(buf, shapes, lead=()):
    flat = buf.reshape(lead + (-1,))
    out, o = [], 0
    for shp in shapes:
        n = math.prod(shp)
        out.append(flat[..., o:o + n].reshape(lead + tuple(shp)))
        o += n
    return out


def _pack_rows(arrs, lead=(), mult=8):
    nl = len(lead)
    flat = jnp.concatenate([a.reshape(lead + (-1,)) for a in arrs], axis=nl)
    n = flat.shape[nl]
    rows = -(-n // (mult * LANE)) * mult
    flat = jnp.pad(flat, [(0, 0)] * nl + [(0, rows * LANE - n)])
    return flat.reshape(lead + (rows, LANE))


def _slot_buffer(shard):
    return lax.dynamic_update_slice(lax.empty((N_CHIPS,) + shard.shape, shard.dtype), shard[None],
                                    (0,) * (shard.ndim + 1))


def kernel(x, norm_mix_w, w_in, ssm_conv_w, ssm_conv_b, ssm_dt_bias, ssm_a_log, ssm_d, ssm_norm_w, gdn_conv_w, gdn_a_log, gdn_dt_bias, gdn_norm_w, w_proj_ssm, w_proj_gdn, w_out, norm_ffn_w, w_ffn_in, w_ffn_down, final_norm_w, loss_target, m_norm_mix_w, m_w_in, m_ssm_conv_w, m_ssm_conv_b, m_ssm_dt_bias, m_ssm_a_log, m_ssm_d, m_ssm_norm_w, m_gdn_conv_w, m_gdn_a_log, m_gdn_dt_bias, m_gdn_norm_w, m_w_proj_ssm, m_w_proj_gdn, m_w_out, m_norm_ffn_w, m_w_ffn_in, m_w_ffn_down, m_final_norm_w, v_norm_mix_w, v_w_in, v_ssm_conv_w, v_ssm_conv_b, v_ssm_dt_bias, v_ssm_a_log, v_ssm_d, v_ssm_norm_w, v_gdn_conv_w, v_gdn_a_log, v_gdn_dt_bias, v_gdn_norm_w, v_w_proj_ssm, v_w_proj_gdn, v_w_out, v_norm_ffn_w, v_w_ffn_in, v_w_ffn_down, v_final_norm_w):
    wl = (norm_mix_w, w_in, ssm_conv_w, ssm_conv_b, ssm_dt_bias, ssm_a_log, ssm_d, ssm_norm_w, gdn_conv_w,
          gdn_a_log, gdn_dt_bias, gdn_norm_w, w_proj_ssm, w_proj_gdn, w_out, norm_ffn_w, w_ffn_in, w_ffn_down,
          final_norm_w)
    ml = (m_norm_mix_w, m_w_in, m_ssm_conv_w, m_ssm_conv_b, m_ssm_dt_bias, m_ssm_a_log, m_ssm_d, m_ssm_norm_w,
          m_gdn_conv_w, m_gdn_a_log, m_gdn_dt_bias, m_gdn_norm_w, m_w_proj_ssm, m_w_proj_gdn, m_w_out,
          m_norm_ffn_w, m_w_ffn_in, m_w_ffn_down, m_final_norm_w)
    vl = (v_norm_mix_w, v_w_in, v_ssm_conv_w, v_ssm_conv_b, v_ssm_dt_bias, v_ssm_a_log, v_ssm_d, v_ssm_norm_w,
          v_gdn_conv_w, v_gdn_a_log, v_gdn_dt_bias, v_gdn_norm_w, v_w_proj_ssm, v_w_proj_gdn, v_w_out,
          v_norm_ffn_w, v_w_ffn_in, v_w_ffn_down, v_final_norm_w)
    w = dict(zip(WEIGHTS, wl))
    m = dict(zip(WEIGHTS, ml))
    v = dict(zip(WEIGHTS, vl))
    x_pos, y_pos, c = _pos()
    me = 2 * x_pos + y_pos
    big = [n for n, _ in BIG]

    shards = [w[n].astype(BF16) for n in big]
    shards[0] = jnp.pad(shards[0], ((0, 0), (0, 0), (0, IN_SHARD_PAD - IN_SHARD)))
    conv_shapes = [w[n].shape[1:] for n, _ in CONVW]
    conv_pack = _pack_rows([w[n] for n, _ in CONVW], lead=(DEPTH,), mult=16)

    def slot_buffers(l):
        return [_slot_buffer(s[l].reshape((2, s.shape[1] // 2) + s.shape[2:])) for s in shards + [conv_pack]]

    def assemble(bufs, which):
        out = {}
        for g_, i in zip(bufs, which):
            by_chip = [lax.dynamic_index_in_dim(g_, jnp.bitwise_xor(me, j), 0, keepdims=False)
                       .reshape((-1,) + g_.shape[3:]) for j in range(N_CHIPS)]
            if i < 0:
                conv_parts = [_unpack(p, conv_shapes) for p in by_chip]
                for k, (n, axis) in enumerate(CONVW):
                    out[n] = jnp.concatenate([conv_parts[j][k] for j in range(N_CHIPS)], axis=axis - 1)
            else:
                n, axis = BIG[i]
                cols = IN_SHARD if n == "w_in" else by_chip[0].shape[-1]
                out[n] = jnp.concatenate([p[:, :cols] for p in by_chip], axis=axis - 1)
        return out

    everything = list(range(len(BIG))) + [-1]
    first, rest = [0, -1], everything[1:-1]
    bufs0 = slot_buffers(0)
    landed_first = weights_gather("gather_w_l0_first", [bufs0[i] for i in first])
    rest_sems = gather_start("gather_w_l0_rest_start", [bufs0[i] for i in rest], landed_first[0])
    l1_sems = gather_start("gather_w_l1_start", slot_buffers(1), rest_sems[2][0])

    def land(name, flying, after):
        send_sems, recv_sems, bufs, _ = flying
        return weights_forward(f"{name}_forward", gather_wait(f"{name}_wait", send_sems, recv_sems, bufs, after))

    def gathered_of(l, after):
        if l == 0:
            return {**assemble(landed_first, first),
                    "late_weights": lambda later: assemble(land("gather_w_l0_rest", rest_sems, later), rest)}
        return assemble(land("gather_w_l1", l1_sems, after), everything)

    token = rest_sems[3] + l1_sems[3]

    in_flight = {}

    def start_reduction(l, g_layer, dx):
        halves = [g_layer[n].reshape((1, 2, -1, AL_DIM)) if n == "w_in"
                  else g_layer[n].reshape((N_CHIPS, 2, -1) + g_layer[n].shape[-1:]) for n in big]
        from_pair = pair_swap(f"grad_pair_swap_l{l}", halves)
        chip_part = [pair_add(f"grad_pair_add_{n}_l{l}", g_, r_, c) for n, g_, r_ in zip(big, halves, from_pair)]
        chip_part[0] = _aligned_to_shards(chip_part[0][0])
        *in_flight[l], token_l = scatter_start(f"grad_scatter_start_l{l}", chip_part, dx)
        return dx + token_l[0, 0]

    def finish_reduction(l, after):
        parts, landed = scatter_wait(f"grad_scatter_wait_l{l}", *in_flight[l], after)
        sums = [chip_sum(f"grad_chip_sum_{n}_l{l}", s_, r_, me, c) for n, s_, r_ in zip(big, parts, landed)]
        return [r_.reshape((-1,) + r_.shape[2:]) for r_ in pair_share(f"grad_pair_share_l{l}", sums)]

    full = {n: w[n] for n in SMALL}
    loss_part, grad_x, grads = local_step(x[0] + token[0, 0], loss_target[0], full, gathered_of, start_reduction)
    reduced = [None] * DEPTH
    after = grad_x
    for l in reversed(range(DEPTH)):
        reduced[l] = finish_reduction(l, after)
        after = reduced[l][0]
    g_sharded = {n: jnp.stack([reduced[l][i] for l in range(DEPTH)]) for i, n in enumerate(big)}
    g_sharded["w_in"] = g_sharded["w_in"][:, :, :IN_SHARD]

    small_names = list(SMALL) + [n for n, _ in CONVW]
    small_all = all_allgather("gather_small", _pack_rows([grads[n] for n in small_names] + [loss_part[0, :1]]))
    small_sum = slab_sum("small_sum", small_all)
    small_vals = _unpack(small_sum, [grads[n].shape for n in small_names] + [(1,)])
    g_small = dict(zip(small_names, small_vals[:-1]))
    loss = small_vals[-1].reshape(())
    for n, axis in CONVW:
        size = w[n].shape[axis]
        g_sharded[n] = lax.dynamic_slice_in_dim(g_small.pop(n), me * size, size, axis=axis)

    out_g, out_d, out_m, out_v = {}, {}, {}, {}
    for n, _ in SHARDED:
        shp = w[n].shape
        two = lambda a: a.reshape(-1, shp[-1])
        d_, m_, v_ = adamw(f"adamw_{n}", two(w[n]), two(g_sharded[n]), two(m[n]), two(v[n]))
        out_g[n], out_d[n], out_m[n], out_v[n] = g_sharded[n], d_.reshape(shp), m_.reshape(shp), v_.reshape(shp)
    d_, m_, v_ = adamw("adamw_small", *[_pack_rows([d[n] for n in SMALL]) for d in (w, g_small, m, v)])
    small_shapes = [w[n].shape for n in SMALL]
    for n, dd, mm, vv in zip(SMALL, _unpack(d_, small_shapes), _unpack(m_, small_shapes), _unpack(v_, small_shapes)):
        out_g[n], out_d[n], out_m[n], out_v[n] = g_small[n], dd, mm, vv

    return (loss, grad_x[None], *[out_g[n] for n in WEIGHTS], *[out_d[n] for n in WEIGHTS],
            *[out_m[n] for n in WEIGHTS], *[out_v[n] for n in WEIGHTS])
```

```python
import math

import jax
import jax.numpy as jnp
from jax import lax
from jax.experimental import pallas as pl
from jax.experimental.pallas import tpu as pltpu

F32 = jnp.float32
BF16 = jnp.bfloat16

D_MODEL = 1024
DEPTH = 2
SSM_HEADS = 16
SSM_HEAD_DIM = 64
SSM_D_INNER = 1024
SSM_STATE = 128
SSM_CONV_DIM = 1536
GDN_HEADS = 8
GDN_HEAD = 128
GDN_QKV_DIM = 3072
CONV_K = 4
CHUNK = 128
SSD_CHUNK = 256
FFN_HIDDEN = 2816
EPS = 1e-6
IN_DIM = 8736

ADAM_LR = 0.001
ADAM_B1 = 0.9
ADAM_B2 = 0.999
ADAM_EPS = 1e-08
ADAM_WD = 0.01
ADAM_STEP = 10

LANE = 128
NEG_BIG = -1e30
VMEM_LIMIT = 56 * 1024 * 1024

AL_Z, AL_XBC, AL_QKV, AL_GZ, AL_GS, AL_GG, AL_SMALL, AL_DIM = 0, 1024, 2560, 5632, 6656, 7680, 8704, 9216
SM_DT, SM_A, SM_B = 0, 16, 24

HI = lax.Precision.HIGHEST
NN = (((1,), (0,)), ((), ()))
NT = (((1,), (1,)), ((), ()))
TN = (((0,), (0,)), ((), ()))


def _cparams(sem):
    return pltpu.CompilerParams(dimension_semantics=sem, vmem_limit_bytes=VMEM_LIMIT)


def _pick(n, prefs):
    for p in prefs:
        if n % p == 0:
            return p
    return n


MATMUL_VMEM_BUDGET = 40 << 20
MXU_WIDTH = 256
HBM_BYTES_PER_S = 3.3e12
MXU_FLOPS_PER_S = 9.0e14
GRID_STEP_S = 0.35e-6
N_CHIPS = 4


def _matmul_tiles(m, n, n_dom, k, a_item, b_item, o_item):
    best = None
    def cands(dim, cap):
        return [c for c in range(LANE, min(dim, cap) + 1, LANE) if dim % c == 0] or [dim]

    tms, tns, tks = cands(m, 2048), cands(n_dom, 2304), cands(k, 1 << 30)
    for tm in tms:
        for tn in tns:
            for tk in tks:
                nk = k // tk
                vmem = (2 * (tm * tk * a_item + tk * tn * b_item + tm * tn * o_item) + tm * tn * 4 * (2 if nk > 1 else 1)
                        + (tm * tk * 2 if a_item > 2 else 0) + (tk * tn * 2 if b_item > 2 else 0))
                if vmem > MATMUL_VMEM_BUDGET:
                    continue
                traffic = m * k * a_item * (1 if nk == 1 else n // tn) + k * n * b_item * (m // tm)
                mxu_fill = tn / (-(-tn // MXU_WIDTH) * MXU_WIDTH)
                cost = (max(traffic / HBM_BYTES_PER_S, 2.0 * m * n * k / (MXU_FLOPS_PER_S * mxu_fill))
                        + (m // tm) * (n // tn) * nk * GRID_STEP_S)
                if best is None or cost < best[0]:
                    best = (cost, (tm, tn, tk))
    return best[1]


def _dot(a, b, dims=NN):
    return lax.dot_general(a.astype(BF16), b.astype(BF16), dims, preferred_element_type=F32)


def _dot3(a, b, dims=NN):
    a_hi, b_hi = a.astype(BF16), b.astype(BF16)
    a_lo = (a - a_hi.astype(F32)).astype(BF16)
    b_lo = (b - b_hi.astype(F32)).astype(BF16)

    def dg(u, v):
        return lax.dot_general(u, v, dims, preferred_element_type=F32)

    return dg(a_hi, b_hi) + (dg(a_hi, b_lo) + dg(a_lo, b_hi))


def _dot_hi(a, b, dims=NN):
    return lax.dot_general(a, b, dims, precision=HI, preferred_element_type=F32)


def _sigmoid(x):
    return jax.nn.sigmoid(x)


def _silu(x):
    return x * _sigmoid(x)


def _softplus(x):
    return jnp.maximum(x, 0.0) + jnp.log1p(jnp.exp(-jnp.abs(x)))


def matmul(name, a, b, mode, out_dtype=F32, chip_major=False, stack=None):
    if mode == "nn":
        (m, k), (k2, n) = a.shape, b.shape
    elif mode == "nt":
        (m, k), (n, k2) = a.shape, b.shape
    else:
        (k, m), (k2, n) = a.shape, b.shape
    assert k == k2, (a.shape, b.shape, mode)
    shard = n // N_CHIPS if chip_major else n
    tm, tn, tk = _matmul_tiles(m, n, shard, k, a.dtype.itemsize, b.dtype.itemsize, jnp.dtype(out_dtype).itemsize)
    if chip_major:
        per = shard // tn
        base_shape, base_blk = (N_CHIPS, m, shard), (None, tm, tn)
        base_idx = lambda i, j: (j // per, i, j % per)
    else:
        base_shape, base_blk = (m, n), (tm, tn)
        base_idx = lambda i, j: (i, j)
    nk = k // tk
    dims = {"nn": NN, "nt": NT, "tn": TN}[mode]

    def body_acc(a_ref, b_ref, o_ref, acc_ref):
        kk = pl.program_id(2)

        @pl.when(kk == 0)
        def _():
            acc_ref[...] = jnp.zeros_like(acc_ref)

        acc_ref[...] += _dot(a_ref[...], b_ref[...], dims)

        @pl.when(kk == nk - 1)
        def _():
            o_ref[...] = acc_ref[...].astype(o_ref.dtype)

    def body_one(a_ref, b_ref, o_ref):
        o_ref[...] = _dot(a_ref[...], b_ref[...], dims).astype(o_ref.dtype)

    compute = body_one if nk == 1 else body_acc
    if mode == "tn":
        a_spec = pl.BlockSpec((tk, tm), lambda i, j, kk: (kk, i))
    else:
        a_spec = pl.BlockSpec((tm, tk), lambda i, j, kk: (i, kk))
    if mode == "nt":
        b_spec = pl.BlockSpec((tn, tk), lambda i, j, kk: (j, kk))
    else:
        b_spec = pl.BlockSpec((tk, tn), lambda i, j, kk: (kk, j))
    in_specs, operands, aliases, body = [a_spec, b_spec], [a, b], {}, compute
    if stack is None:
        out_shape, out_blk, out_idx = base_shape, base_blk, (lambda i, j, kk: base_idx(i, j))
    else:
        layer, buf = stack
        out_shape, out_blk = (DEPTH,) + base_shape, (None,) + base_blk
        out_idx = lambda i, j, kk: (layer,) + base_idx(i, j)
        if buf is not None:
            assert buf.shape == out_shape and buf.dtype == out_dtype
            in_specs.append(pl.BlockSpec(memory_space=pl.ANY))
            operands.append(buf)
            aliases = {2: 0}

            def body(a_ref, b_ref, buf_ref, *rest):
                del buf_ref
                compute(a_ref, b_ref, *rest)

    return pl.pallas_call(
        body, name=name,
        out_shape=jax.ShapeDtypeStruct(out_shape, out_dtype),
        grid=(m // tm, n // tn, nk),
        in_specs=in_specs,
        out_specs=pl.BlockSpec(out_blk, out_idx),
        scratch_shapes=[] if nk == 1 else [pltpu.VMEM((tm, tn), F32)],
        input_output_aliases=aliases,
        compiler_params=_cparams(("parallel", "parallel", "arbitrary")),
    )(*operands)


def _row_map(c0, moves):
    return (lambda j, i: (i, c0 + j)) if moves else (lambda j, i: (i, c0))


def _par_map(c0, moves):
    return (lambda j, i: (0, c0 + j)) if moves else (lambda j, i: (0, c0))


def _in_spec(op, tile):
    _, kind, w, c0, moves = op
    if kind == "row":
        return pl.BlockSpec((tile, w), _row_map(c0, moves))
    return pl.BlockSpec((1, w), _par_map(c0, moves))


ROW_BLOCK_ELEMS = 1 << 18


def _row_tile(t, tile, ops):
    width = max(op[2] for op in ops if op[1] == "row")
    return min(t, max(tile, ROW_BLOCK_ELEMS // width))


def rowwise_fwd(name, fn, t, tile, ncol, ins, outs):
    n_in = len(ins)
    tile = _row_tile(t, tile, ins)

    def body(*refs):
        vals = [r[...].astype(F32) for r in refs[:n_in]]
        res = fn(*vals)
        if not isinstance(res, (tuple, list)):
            res = (res,)
        for r, v in zip(refs[n_in:], res):
            r[...] = v.astype(r.dtype)

    res = pl.pallas_call(
        body, name=name,
        out_shape=[jax.ShapeDtypeStruct((t, w * ncol), dt) for w, dt in outs],
        grid=(ncol, t // tile),
        in_specs=[_in_spec(op, tile) for op in ins],
        out_specs=[pl.BlockSpec((tile, w), _row_map(0, True)) for w, _ in outs],
        compiler_params=_cparams(("arbitrary", "arbitrary")),
    )(*[op[0] for op in ins])
    return res


def rowwise_bwd(name, fn, t, tile, ncol, ins, need, cts, addends=None, row_dtypes=None):
    n_in, n_ct = len(ins), len(cts)
    tile = _row_tile(t, tile, ins)
    addends = addends or {}
    row_dtypes = row_dtypes or {}
    didx = [i for i in range(n_in) if need[i]]
    add_ops = [addends[i] for i in didx if i in addends]
    n_add = len(add_ops)

    def body(*refs):
        in_refs = refs[:n_in]
        ct_refs = refs[n_in:n_in + n_ct]
        add_refs = refs[n_in + n_ct:n_in + n_ct + n_add]
        out_refs = refs[n_in + n_ct + n_add:]
        vals = [r[...].astype(F32) for r in in_refs]

        def g(*dv):
            full = list(vals)
            for i, v in zip(didx, dv):
                full[i] = v
            res = fn(*full)
            return tuple(res) if isinstance(res, (tuple, list)) else (res,)

        _, vjp = jax.vjp(g, *[vals[i] for i in didx])
        grads = vjp(tuple(c[...].astype(F32) for c in ct_refs))
        j, i = pl.program_id(0), pl.program_id(1)
        a = 0
        for o_ref, gv, idx in zip(out_refs, grads, didx):
            _, kind, _, _, moves = ins[idx]
            if kind == "row":
                if idx in addends:
                    gv = gv + add_refs[a][...].astype(F32)
                    a += 1
                o_ref[...] = gv.astype(o_ref.dtype)
            else:
                first = (i == 0) if moves else jnp.logical_and(i == 0, j == 0)

                @pl.when(first)
                def _(o_ref=o_ref, gv=gv):
                    o_ref[...] = gv

                @pl.when(jnp.logical_not(first))
                def _(o_ref=o_ref, gv=gv):
                    o_ref[...] += gv

    out_shape, out_specs = [], []
    for idx in didx:
        _, kind, w, _, moves = ins[idx]
        cols = w * (ncol if moves else 1)
        if kind == "row":
            out_shape.append(jax.ShapeDtypeStruct((t, cols), row_dtypes.get(idx, F32)))
            out_specs.append(pl.BlockSpec((tile, w), _row_map(0, moves)))
        else:
            out_shape.append(jax.ShapeDtypeStruct((1, cols), F32))
            out_specs.append(pl.BlockSpec((1, w), _par_map(0, moves)))
    ops = list(ins) + list(cts) + add_ops
    res = pl.pallas_call(
        body, name=name,
        out_shape=out_shape,
        grid=(ncol, t // tile),
        in_specs=[_in_spec(op, tile) for op in ops],
        out_specs=out_specs,
        compiler_params=_cparams(("arbitrary", "arbitrary")),
    )(*[op[0] for op in ops])
    return res


def f_rmsnorm(x, w):
    return x * lax.rsqrt(jnp.mean(x * x, axis=-1, keepdims=True) + EPS) * w


def f_ssd_post(y, z, w):
    y = y * _silu(z)
    return y * lax.rsqrt(jnp.mean(y * y, axis=-1, keepdims=True) + EPS) * w


def f_gdn_post(o, z, w):
    o = o * lax.rsqrt(jnp.mean(o * o, axis=-1, keepdims=True) + EPS) * w
    return o * _silu(z)


def f_merge(gs, p1, gg, p2):
    return _sigmoid(gs) * p1 + _sigmoid(gg) * p2


def f_swiglu(g, u):
    return _silu(g) * u


def final_loss(name, x, tgt, w, tile=256):
    t, d = x.shape

    def body(x_ref, t_ref, w_ref, loss_ref, dx_ref, dw_ref):
        i = pl.program_id(0)
        xv, tv, wv = x_ref[...], t_ref[...], w_ref[...]

        def g(xx, ww):
            err = f_rmsnorm(xx, ww) - tv
            return 0.5 * jnp.sum(jnp.mean(err * err, axis=-1, keepdims=True), axis=0, keepdims=True)

        val, vjp = jax.vjp(g, xv, wv)
        dx, dw = vjp(jnp.ones((1, 1), F32))
        dx_ref[...] = dx
        lv = jnp.broadcast_to(val, (1, LANE))

        @pl.when(i == 0)
        def _():
            loss_ref[...] = lv
            dw_ref[...] = dw

        @pl.when(i != 0)
        def _():
            loss_ref[...] += lv
            dw_ref[...] += dw

    return pl.pallas_call(
        body, name=name,
        out_shape=[jax.ShapeDtypeStruct((1, LANE), F32), jax.ShapeDtypeStruct((t, d), F32),
                   jax.ShapeDtypeStruct((1, d), F32)],
        grid=(t // tile,),
        in_specs=[pl.BlockSpec((tile, d), lambda i: (i, 0)), pl.BlockSpec((tile, d), lambda i: (i, 0)),
                  pl.BlockSpec((1, d), lambda i: (0, 0))],
        out_specs=[pl.BlockSpec((1, LANE), lambda i: (0, 0)), pl.BlockSpec((tile, d), lambda i: (i, 0)),
                   pl.BlockSpec((1, d), lambda i: (0, 0))],
        compiler_params=_cparams(("arbitrary",)),
    )(x, tgt, w)


CONV_W = 512
HALO = 8
STRIPS = 4


def _rows_back(before, cur, d):
    rows = lax.broadcasted_iota(jnp.int32, cur.shape, 0)
    return jnp.where(rows < d, pltpu.roll(before, d, 0), pltpu.roll(cur, d, 0))


def _rows_ahead(cur, after, d):
    rows = lax.broadcasted_iota(jnp.int32, cur.shape, 0)
    return jnp.where(rows < HALO - d, pltpu.roll(cur, HALO - d, 0), pltpu.roll(after, HALO - d, 0))


def _conv_taps(taps, bias, before, cur):
    shifted = [_rows_back(before, cur, CONV_K - 1 - k) for k in range(CONV_K - 1)] + [cur]
    pre = bias + taps[CONV_K - 1] * cur
    for k in range(CONV_K - 1):
        pre = pre + taps[k] * shifted[k]
    return pre, shifted


def conv_fwd(name, src, c0, width, w, b, tile=512):
    t = src.shape[0]
    ncol, nrow = width // CONV_W, t // tile
    cb0 = c0 // CONV_W
    hb = tile // HALO

    def body(prev_ref, cur_ref, w_ref, b_ref, o_ref):
        i = pl.program_id(1)
        taps = [w_ref[k:k + 1, :] for k in range(CONV_K)]
        bias = b_ref[...]

        def strips(g, before):
            for u in range(STRIPS):
                r0 = pl.multiple_of((g * STRIPS + u) * HALO, HALO)
                cur = cur_ref[pl.ds(r0, HALO), :]
                pre, _ = _conv_taps(taps, bias, before, cur)
                o_ref[pl.ds(r0, HALO), :] = _silu(pre)
                before = cur
            return before

        lax.fori_loop(0, tile // (HALO * STRIPS), strips, jnp.where(i == 0, 0.0, prev_ref[...]))

    return pl.pallas_call(
        body, name=name,
        out_shape=jax.ShapeDtypeStruct((t, width), F32),
        grid=(ncol, nrow),
        in_specs=[pl.BlockSpec((HALO, CONV_W), lambda j, i: (jnp.maximum(i * hb - 1, 0), cb0 + j)),
                  pl.BlockSpec((tile, CONV_W), lambda j, i: (i, cb0 + j)),
                  pl.BlockSpec((CONV_K, CONV_W), lambda j, i: (0, j)),
                  pl.BlockSpec((1, CONV_W), lambda j, i: (0, j))],
        out_specs=pl.BlockSpec((tile, CONV_W), lambda j, i: (i, j)),
        compiler_params=_cparams(("arbitrary", "arbitrary")),
    )(src, src, w, b)


def conv_bwd(name, src, c0, width, w, b, dy, tile=512):
    t = src.shape[0]
    ncol, nrow = width // CONV_W, t // tile
    cb0 = c0 // CONV_W
    hb = tile // HALO
    last_hb = t // HALO - 1
    nstrip = tile // HALO

    def body(sprev_ref, scur_ref, snext_ref, w_ref, b_ref, dycur_ref, dynext_ref,
             du_ref, dw_ref, db_ref, dpre_ref):
        i = pl.program_id(1)
        taps = [w_ref[k:k + 1, :] for k in range(CONV_K)]
        bias = b_ref[...]

        def dpre_of(before, cur, dy_strip):
            pre, shifted = _conv_taps(taps, bias, before, cur)
            s = _sigmoid(pre)
            return dy_strip * (s * (1.0 + pre * (1.0 - s))), shifted

        def strips1(g, carry):
            before, dws, dbs = carry
            for u in range(STRIPS):
                r0 = pl.multiple_of((g * STRIPS + u) * HALO, HALO)
                cur = scur_ref[pl.ds(r0, HALO), :]
                dpre, shifted = dpre_of(before, cur, dycur_ref[pl.ds(r0, HALO), :])
                dpre_ref[pl.ds(r0, HALO), :] = dpre
                before, dws, dbs = cur, tuple(a + dpre * v for a, v in zip(dws, shifted)), dbs + dpre
            return before, dws, dbs

        zero = jnp.zeros((HALO, CONV_W), F32)
        before, dws, dbs = lax.fori_loop(0, nstrip // STRIPS, strips1,
                                         (jnp.where(i == 0, 0.0, sprev_ref[...]), (zero,) * CONV_K, zero))
        dpre_next, _ = dpre_of(before, snext_ref[...], dynext_ref[...])
        dpre_ref[pl.ds(tile, HALO), :] = jnp.where(i == nrow - 1, 0.0, dpre_next)

        def strips2(g, _):
            parts = []
            for u in range(STRIPS):
                r0 = pl.multiple_of((g * STRIPS + u) * HALO, HALO)
                cur = dpre_ref[pl.ds(r0, HALO), :]
                after = dpre_ref[pl.ds(r0 + HALO, HALO), :]
                acc = taps[CONV_K - 1] * cur
                for d in range(1, CONV_K):
                    acc = acc + taps[CONV_K - 1 - d] * _rows_ahead(cur, after, d)
                parts.append(acc)
            r0 = pl.multiple_of(g * STRIPS * HALO, STRIPS * HALO)
            du_ref[pl.ds(r0, STRIPS * HALO), :] = jnp.concatenate(parts, axis=0).astype(du_ref.dtype)
            return 0

        lax.fori_loop(0, nstrip // STRIPS, strips2, 0)
        dw_tile = jnp.concatenate([jnp.sum(a, axis=0, keepdims=True) for a in dws], axis=0)
        db_tile = jnp.sum(dbs, axis=0, keepdims=True)
        _acc(dw_ref, dw_tile, i == 0)
        _acc(db_ref, db_tile, i == 0)

    return pl.pallas_call(
        body, name=name,
        out_shape=[jax.ShapeDtypeStruct((t, width), BF16), jax.ShapeDtypeStruct((CONV_K, width), F32),
                   jax.ShapeDtypeStruct((1, width), F32)],
        grid=(ncol, nrow),
        in_specs=[pl.BlockSpec((HALO, CONV_W), lambda j, i: (jnp.maximum(i * hb - 1, 0), cb0 + j)),
                  pl.BlockSpec((tile, CONV_W), lambda j, i: (i, cb0 + j)),
                  pl.BlockSpec((HALO, CONV_W), lambda j, i: (jnp.minimum((i + 1) * hb, last_hb), cb0 + j)),
                  pl.BlockSpec((CONV_K, CONV_W), lambda j, i: (0, j)),
                  pl.BlockSpec((1, CONV_W), lambda j, i: (0, j)),
                  pl.BlockSpec((tile, CONV_W), lambda j, i: (i, j)),
                  pl.BlockSpec((HALO, CONV_W), lambda j, i: (jnp.minimum((i + 1) * hb, last_hb), j))],
        out_specs=[pl.BlockSpec((tile, CONV_W), lambda j, i: (i, j)),
                   pl.BlockSpec((CONV_K, CONV_W), lambda j, i: (0, j)),
                   pl.BlockSpec((1, CONV_W), lambda j, i: (0, j))],
        scratch_shapes=[pltpu.VMEM((tile + HALO, CONV_W), F32)],
        compiler_params=_cparams(("arbitrary", "arbitrary")),
    )(src, src, src, w, b, dy, dy)


def _iota2(q):
    return (lax.broadcasted_iota(jnp.int32, (q, q), 0), lax.broadcasted_iota(jnp.int32, (q, q), 1))


def _lane_pick(blk, idx):
    lane = lax.broadcasted_iota(jnp.int32, (1, LANE), 1)
    return jnp.sum(jnp.where(lane == idx, blk, 0.0), axis=1, keepdims=True)


class _Decay:
    def __init__(self, a):
        q = a.shape[0]
        r, c = _iota2(q)
        self.r, self.c = r, c
        self.cum = _dot_hi((c <= r).astype(F32), a)
        self.cum_t = _dot_hi(a, (r <= c).astype(F32), TN)
        self.tot = self.cum[q - 1:q, :]
        self.e_cum = jnp.exp(self.cum)
        self.e_rest = jnp.exp(self.tot - self.cum)
        self.e_tot = jnp.exp(self.tot)

    def mask(self, lane):
        rows = lax.broadcasted_iota(jnp.int32, (LANE, 1), 0)
        cum_row = jnp.sum(jnp.where(rows == lane, self.cum_t, 0.0), axis=0, keepdims=True)
        return jnp.exp(jnp.where(self.r >= self.c, _lane_pick(self.cum, lane) - cum_row, NEG_BIG))


_SSD_B = SSM_D_INNER
_SSD_C = SSM_D_INNER + 2 * SSM_STATE


def _interleave(gens):
    results = [None] * len(gens)
    live = list(range(len(gens)))
    while live:
        for i in list(live):
            try:
                next(gens[i])
            except StopIteration as stop:
                results[i] = stop.value
                live.remove(i)
    return results


def ssd_chunk(xs, bm, cm, dt_all, dsk, dec, state, p, cb):
    lane = lax.broadcasted_iota(jnp.int32, (1, LANE), 1)
    m0 = lane < SSM_HEAD_DIM
    h0, h1 = 2 * p, 2 * p + 1

    def both(blk):
        return jnp.where(m0, _lane_pick(blk, h0), _lane_pick(blk, h1))

    xdt = xs * both(dt_all)
    l0, l1 = dec.mask(h0), dec.mask(h1)
    yield
    y_diag = _dot(cb * l0, jnp.where(m0, xdt, 0.0)) + _dot(cb * l1, jnp.where(m0, 0.0, xdt))
    y_off = _dot(cm, state, NT) * both(dec.e_cum)
    yield
    rowm = lax.broadcasted_iota(jnp.int32, (LANE, 1), 0) < SSM_HEAD_DIM
    new_state = (state * jnp.where(rowm, _lane_pick(dec.e_tot, h0), _lane_pick(dec.e_tot, h1))
                 + _dot(xdt * both(dec.e_rest), bm, TN))
    y = y_diag + y_off + both(dsk) * xs
    return y, new_state


def ssd_pairs(xs, bms, cms, small, dtb, alog, dsk, states):
    dt_all = _softplus(small + dtb)
    dec = _Decay(dt_all * (-jnp.exp(alog)))
    cbs = [_dot(cm, bm, NT) for cm, bm in zip(cms, bms)]
    res = _interleave([ssd_chunk(x, bms[p // 4], cms[p // 4], dt_all, dsk, dec, st, p, cbs[p // 4])
                       for p, (x, st) in enumerate(zip(xs, states))])
    return tuple(y for y, _ in res), tuple(s for _, s in res)


def tri_inverse(a):
    q = a.shape[0]
    r, c = _iota2(q)
    eye = (r == c).astype(F32)
    diag = (r // 16) == (c // 16)
    bd = jnp.where(diag, a, 0.0)
    off = jnp.where(diag, 0.0, a)
    b2 = _dot3(bd, bd)
    d1 = _dot3(eye - bd, eye + b2)
    yield
    b4 = _dot3(b2, b2)
    yield
    b8 = _dot3(b4, b4)
    d2 = _dot3(d1, eye + b4)
    yield
    dinv = _dot3(d2, eye + b8)
    yield
    n = _dot3(dinv, off)
    yield
    powers = [n]
    while 16 * 2 ** len(powers) < q:
        powers.append(_dot3(powers[-1], powers[-1]))
        yield
    m = dinv
    for pw in reversed(powers[1:]):
        m = _dot3(eye + pw, m)
        yield
    return _dot3(eye - n, m)


@jax.custom_vjp
def _solve_with(xinv, a, rhs):
    del a
    return _dot3(xinv, rhs)


def _solve_with_fwd(xinv, a, rhs):
    t = _dot3(xinv, rhs)
    return t, (xinv, t)


def _solve_with_bwd(res, dt):
    xinv, t = res
    d_rhs = _dot3(xinv, dt, TN)
    d_a = -_dot(d_rhs, t, NT)
    return jnp.zeros_like(xinv), d_a, d_rhs


_solve_with.defvjp(_solve_with_fwd, _solve_with_bwd)


_GDN_K = GDN_HEADS * GDN_HEAD
_GDN_V = 2 * GDN_HEADS * GDN_HEAD


def gdn_chunk(qh, kh, vh, beta_all, dec, state, h, xinv=None):
    r, c = dec.r, dec.c
    qn = qh * lax.rsqrt(jnp.sum(qh * qh, axis=-1, keepdims=True) + EPS) * (GDN_HEAD ** -0.5)
    kn = kh * lax.rsqrt(jnp.sum(kh * kh, axis=-1, keepdims=True) + EPS)
    beta = _lane_pick(beta_all, SM_B + h)
    decay = dec.mask(SM_A + h)
    yield
    kk = _dot(kn, kn, NT)
    qk = _dot(qn, kn, NT) * decay
    amat = jnp.where(r > c, kk * decay * beta, 0.0)
    eg = _lane_pick(dec.e_cum, SM_A + h)
    rhs = jnp.concatenate([vh * beta, kn * (beta * eg)], axis=1)
    qs = _dot(qn * eg, state)
    yield
    if xinv is None:
        xinv = yield from tri_inverse(amat)
        t = _dot3(xinv, rhs)
    else:
        t = _solve_with(xinv, amat, rhs)
    yield
    u, w = t[:, :GDN_HEAD], t[:, GDN_HEAD:]
    v_new = u - _dot(w, state)
    yield
    o = qs + _dot(qk, v_new)
    new_state = (state * _lane_pick(dec.e_tot, SM_A + h)
                 + _dot(kn * _lane_pick(dec.e_rest, SM_A + h), v_new, TN))
    return o, new_state, xinv


def gdn_heads(qs, ks, vs, small, alog, dtb, states, xinvs=None):
    nh = len(qs)
    beta_all = _sigmoid(small)
    dec = _Decay(-jnp.exp(alog) * _softplus(small + dtb))
    res = _interleave([gdn_chunk(qs[h], ks[h], vs[h], beta_all, dec, states[h], h,
                                 None if xinvs is None else xinvs[h]) for h in range(nh)])
    return tuple(o for o, _, _ in res), tuple(s for _, s, _ in res), tuple(x for _, _, x in res)


def _acc(ref, val, first):
    @pl.when(first)
    def _():
        ref[...] = val

    @pl.when(jnp.logical_not(first))
    def _():
        ref[...] += val


def ssd_scan_fwd(name, xbc, proj, dtb, alog, dsk):
    t = xbc.shape[0]
    nc, npair = t // SSD_CHUNK, SSM_HEADS // 2
    small_blk = AL_SMALL // LANE

    def body(xbc_ref, sm_ref, dtb_ref, alog_ref, dsk_ref, y_ref, sin_ref, st_ref):
        ci = pl.program_id(0)

        @pl.when(ci == 0)
        def _():
            st_ref[...] = jnp.zeros_like(st_ref)

        s_in = tuple(st_ref[p] for p in range(npair))
        ys, s_new = ssd_pairs(tuple(xbc_ref[:, p * LANE:(p + 1) * LANE] for p in range(npair)),
                              tuple(xbc_ref[:, _SSD_B + g * LANE:_SSD_B + (g + 1) * LANE] for g in range(2)),
                              tuple(xbc_ref[:, _SSD_C + g * LANE:_SSD_C + (g + 1) * LANE] for g in range(2)),
                              sm_ref[...], dtb_ref[...], alog_ref[...], dsk_ref[...], s_in)
        for p in range(npair):
            sin_ref[0, p] = s_in[p]
            y_ref[:, p * LANE:(p + 1) * LANE] = ys[p]
            st_ref[p] = s_new[p]

    par = pl.BlockSpec((1, LANE), lambda ci: (0, 0))
    return pl.pallas_call(
        body, name=name,
        out_shape=[jax.ShapeDtypeStruct((t, SSM_D_INNER), F32),
                   jax.ShapeDtypeStruct((nc, npair, LANE, LANE), F32)],
        grid=(nc,),
        in_specs=[pl.BlockSpec((SSD_CHUNK, SSM_CONV_DIM), lambda ci: (ci, 0)),
                  pl.BlockSpec((SSD_CHUNK, LANE), lambda ci: (ci, small_blk)),
                  par, par, par],
        out_specs=[pl.BlockSpec((SSD_CHUNK, SSM_D_INNER), lambda ci: (ci, 0)),
                   pl.BlockSpec((1, npair, LANE, LANE), lambda ci: (ci, 0, 0, 0))],
        scratch_shapes=[pltpu.VMEM((npair, LANE, LANE), F32)],
        compiler_params=_cparams(("arbitrary",)),
    )(xbc, proj, dtb, alog, dsk)


def ssd_scan_bwd(name, xbc, proj, dtb, alog, dsk, s_in, dy):
    t = xbc.shape[0]
    nc, npair = t // SSD_CHUNK, SSM_HEADS // 2
    small_blk = AL_SMALL // LANE

    def body(xbc_ref, sm_ref, dtb_ref, alog_ref, dsk_ref, sin_ref, dy_ref,
             dxbc_ref, dsm_ref, ddtb_ref, dalog_ref, ddsk_ref, dst_ref):
        ci = pl.program_id(0)

        @pl.when(ci == 0)
        def _():
            dst_ref[...] = jnp.zeros_like(dst_ref)

        _, vjp = jax.vjp(ssd_pairs, tuple(xbc_ref[:, p * LANE:(p + 1) * LANE] for p in range(npair)),
                         tuple(xbc_ref[:, _SSD_B + g * LANE:_SSD_B + (g + 1) * LANE] for g in range(2)),
                         tuple(xbc_ref[:, _SSD_C + g * LANE:_SSD_C + (g + 1) * LANE] for g in range(2)),
                         sm_ref[...], dtb_ref[...], alog_ref[...], dsk_ref[...],
                         tuple(sin_ref[0, p] for p in range(npair)))
        dxs, dbms, dcms, dsm, ddtb, dalog, ddsk, dsts = vjp(
            (tuple(dy_ref[:, p * LANE:(p + 1) * LANE] for p in range(npair)),
             tuple(dst_ref[p] for p in range(npair))))
        for p in range(npair):
            dxbc_ref[:, p * LANE:(p + 1) * LANE] = dxs[p]
            dst_ref[p] = dsts[p]
        for g in range(2):
            dxbc_ref[:, _SSD_B + g * LANE:_SSD_B + (g + 1) * LANE] = dbms[g]
            dxbc_ref[:, _SSD_C + g * LANE:_SSD_C + (g + 1) * LANE] = dcms[g]
        dsm_ref[...] = dsm
        _acc(ddtb_ref, ddtb, ci == 0)
        _acc(dalog_ref, dalog, ci == 0)
        _acc(ddsk_ref, ddsk, ci == 0)

    par = pl.BlockSpec((1, LANE), lambda ci: (0, 0))
    rev = lambda ci: nc - 1 - ci
    return pl.pallas_call(
        body, name=name,
        out_shape=[jax.ShapeDtypeStruct((t, SSM_CONV_DIM), F32),
                   jax.ShapeDtypeStruct((t, LANE), F32),
                   jax.ShapeDtypeStruct((1, LANE), F32), jax.ShapeDtypeStruct((1, LANE), F32),
                   jax.ShapeDtypeStruct((1, LANE), F32)],
        grid=(nc,),
        in_specs=[pl.BlockSpec((SSD_CHUNK, SSM_CONV_DIM), lambda ci: (rev(ci), 0)),
                  pl.BlockSpec((SSD_CHUNK, LANE), lambda ci: (rev(ci), small_blk)),
                  par, par, par,
                  pl.BlockSpec((1, npair, LANE, LANE), lambda ci: (rev(ci), 0, 0, 0)),
                  pl.BlockSpec((SSD_CHUNK, SSM_D_INNER), lambda ci: (rev(ci), 0))],
        out_specs=[pl.BlockSpec((SSD_CHUNK, SSM_CONV_DIM), lambda ci: (rev(ci), 0)),
                   pl.BlockSpec((SSD_CHUNK, LANE), lambda ci: (rev(ci), 0)),
                   par, par, par],
        scratch_shapes=[pltpu.VMEM((npair, LANE, LANE), F32)],
        compiler_params=_cparams(("arbitrary",)),
    )(xbc, proj, dtb, alog, dsk, s_in, dy)


def gdn_scan_fwd(name, qkv, proj, alog, dtb):
    t = qkv.shape[0]
    nc, nh = t // CHUNK, GDN_HEADS
    small_blk = AL_SMALL // LANE

    def body(qkv_ref, sm_ref, alog_ref, dtb_ref, o_ref, sin_ref, x_ref, st_ref):
        ci = pl.program_id(0)

        @pl.when(ci == 0)
        def _():
            st_ref[...] = jnp.zeros_like(st_ref)

        s_in = tuple(st_ref[h] for h in range(nh))
        os, s_new, xinvs = gdn_heads(
            tuple(qkv_ref[:, h * LANE:(h + 1) * LANE] for h in range(nh)),
            tuple(qkv_ref[:, _GDN_K + h * LANE:_GDN_K + (h + 1) * LANE] for h in range(nh)),
            tuple(qkv_ref[:, _GDN_V + h * LANE:_GDN_V + (h + 1) * LANE] for h in range(nh)),
            sm_ref[...], alog_ref[...], dtb_ref[...], s_in)
        for h in range(nh):
            sin_ref[0, h] = s_in[h]
            o_ref[:, h * LANE:(h + 1) * LANE] = os[h]
            x_ref[0, h] = xinvs[h]
            st_ref[h] = s_new[h]

    par = pl.BlockSpec((1, LANE), lambda ci: (0, 0))
    return pl.pallas_call(
        body, name=name,
        out_shape=[jax.ShapeDtypeStruct((t, GDN_HEADS * GDN_HEAD), F32),
                   jax.ShapeDtypeStruct((nc, nh, LANE, LANE), F32),
                   jax.ShapeDtypeStruct((nc, nh, CHUNK, CHUNK), F32)],
        grid=(nc,),
        in_specs=[pl.BlockSpec((CHUNK, GDN_QKV_DIM), lambda ci: (ci, 0)),
                  pl.BlockSpec((CHUNK, LANE), lambda ci: (ci, small_blk)),
                  par, par],
        out_specs=[pl.BlockSpec((CHUNK, GDN_HEADS * GDN_HEAD), lambda ci: (ci, 0)),
                   pl.BlockSpec((1, nh, LANE, LANE), lambda ci: (ci, 0, 0, 0)),
                   pl.BlockSpec((1, nh, CHUNK, CHUNK), lambda ci: (ci, 0, 0, 0))],
        scratch_shapes=[pltpu.VMEM((nh, LANE, LANE), F32)],
        compiler_params=_cparams(("arbitrary",)),
    )(qkv, proj, alog, dtb)


def gdn_scan_bwd(name, qkv, proj, alog, dtb, s_in, xinv, do, dsm_in):
    t = qkv.shape[0]
    nc, nh = t // CHUNK, GDN_HEADS
    small_blk = AL_SMALL // LANE

    def body(qkv_ref, sm_ref, alog_ref, dtb_ref, sin_ref, x_ref, do_ref, dsmi_ref,
             dqkv_ref, dsm_ref, dalog_ref, ddtb_ref, dst_ref):
        ci = pl.program_id(0)

        @pl.when(ci == 0)
        def _():
            dst_ref[...] = jnp.zeros_like(dst_ref)

        xis = tuple(x_ref[0, h] for h in range(nh))

        def fn(qs, ks, vs, sm, alog_, dtb_, sts):
            os, s_new, _ = gdn_heads(qs, ks, vs, sm, alog_, dtb_, sts, xinvs=xis)
            return os, s_new

        _, vjp = jax.vjp(fn, tuple(qkv_ref[:, h * LANE:(h + 1) * LANE] for h in range(nh)),
                         tuple(qkv_ref[:, _GDN_K + h * LANE:_GDN_K + (h + 1) * LANE] for h in range(nh)),
                         tuple(qkv_ref[:, _GDN_V + h * LANE:_GDN_V + (h + 1) * LANE] for h in range(nh)),
                         sm_ref[...], alog_ref[...], dtb_ref[...], tuple(sin_ref[0, h] for h in range(nh)))
        dqs, dks, dvs, dsm, dalog, ddtb, dsts = vjp(
            (tuple(do_ref[:, h * LANE:(h + 1) * LANE] for h in range(nh)), tuple(dst_ref[h] for h in range(nh))))
        for h in range(nh):
            dqkv_ref[:, h * LANE:(h + 1) * LANE] = dqs[h]
            dqkv_ref[:, _GDN_K + h * LANE:_GDN_K + (h + 1) * LANE] = dks[h]
            dqkv_ref[:, _GDN_V + h * LANE:_GDN_V + (h + 1) * LANE] = dvs[h]
            dst_ref[h] = dsts[h]
        dsm_ref[...] = dsmi_ref[...] + dsm
        _acc(dalog_ref, dalog, ci == 0)
        _acc(ddtb_ref, ddtb, ci == 0)

    par = pl.BlockSpec((1, LANE), lambda ci: (0, 0))
    rev = lambda ci: nc - 1 - ci
    return pl.pallas_call(
        body, name=name,
        out_shape=[jax.ShapeDtypeStruct((t, GDN_QKV_DIM), F32), jax.ShapeDtypeStruct((t, LANE), F32),
                   jax.ShapeDtypeStruct((1, LANE), F32), jax.ShapeDtypeStruct((1, LANE), F32)],
        grid=(nc,),
        in_specs=[pl.BlockSpec((CHUNK, GDN_QKV_DIM), lambda ci: (rev(ci), 0)),
                  pl.BlockSpec((CHUNK, LANE), lambda ci: (rev(ci), small_blk)),
                  par, par,
                  pl.BlockSpec((1, nh, LANE, LANE), lambda ci: (rev(ci), 0, 0, 0)),
                  pl.BlockSpec((1, nh, CHUNK, CHUNK), lambda ci: (rev(ci), 0, 0, 0)),
                  pl.BlockSpec((CHUNK, GDN_HEADS * GDN_HEAD), lambda ci: (rev(ci), 0)),
                  pl.BlockSpec((CHUNK, LANE), lambda ci: (rev(ci), 0))],
        out_specs=[pl.BlockSpec((CHUNK, GDN_QKV_DIM), lambda ci: (rev(ci), 0)),
                   pl.BlockSpec((CHUNK, LANE), lambda ci: (rev(ci), 0)),
                   par, par],
        scratch_shapes=[pltpu.VMEM((nh, LANE, LANE), F32)],
        compiler_params=_cparams(("arbitrary",)),
    )(qkv, proj, alog, dtb, s_in, xinv, do, dsm_in)


def _row(arr, w, c0=0, moves=False):
    return (arr, "row", w, c0, moves)


def _par(arr, w, c0=0, moves=False):
    return (arr, "par", w, c0, moves)


def matmul_add(name, a, b, res):
    (m, k), (_, n) = a.shape, b.shape
    tm, tn, tk = _matmul_tiles(m, n, n, k, a.dtype.itemsize, b.dtype.itemsize, 4 + res.dtype.itemsize)
    nk = k // tk

    def body_acc(a_ref, b_ref, r_ref, o_ref, acc_ref):
        kk = pl.program_id(2)

        @pl.when(kk == 0)
        def _():
            acc_ref[...] = r_ref[...]

        acc_ref[...] += _dot(a_ref[...], b_ref[...])

        @pl.when(kk == nk - 1)
        def _():
            o_ref[...] = acc_ref[...]

    def body_one(a_ref, b_ref, r_ref, o_ref):
        o_ref[...] = r_ref[...] + _dot(a_ref[...], b_ref[...])

    body = body_one if nk == 1 else body_acc
    return pl.pallas_call(
        body, name=name,
        out_shape=jax.ShapeDtypeStruct((m, n), F32),
        grid=(m // tm, n // tn, nk),
        in_specs=[pl.BlockSpec((tm, tk), lambda i, j, kk: (i, kk)),
                  pl.BlockSpec((tk, tn), lambda i, j, kk: (kk, j)),
                  pl.BlockSpec((tm, tn), lambda i, j, kk: (i, j))],
        out_specs=pl.BlockSpec((tm, tn), lambda i, j, kk: (i, j)),
        scratch_shapes=[] if nk == 1 else [pltpu.VMEM((tm, tn), F32)],
        compiler_params=_cparams(("parallel", "parallel", "arbitrary")),
    )(a, b, res)


def layer_fwd(l, x, w):
    t = x.shape[0]
    rt = min(256, t)
    s = {"x": x}
    s["h"] = rowwise_fwd(f"norm_mix_l{l}", f_rmsnorm, t, rt, 1,
                         [_row(x, D_MODEL), _par(w["norm_mix_w"], D_MODEL)], [(D_MODEL, BF16)])[0]
    s["proj"] = matmul(f"in_proj_l{l}", s["h"], w["w_in"], "nn")
    s["xbc"] = conv_fwd(f"ssm_conv_l{l}", s["proj"], AL_XBC, SSM_CONV_DIM, w["ssm_conv_w"], w["ssm_conv_b"],
                        tile=min(512, t))
    s["qkv"] = conv_fwd(f"gdn_conv_l{l}", s["proj"], AL_QKV, GDN_QKV_DIM, w["gdn_conv_w"], w["gdn_conv_b"],
                        tile=min(512, t))
    s["y_scan"], s["ssd_sin"] = ssd_scan_fwd(f"ssd_scan_l{l}", s["xbc"], s["proj"], w["ssm_dt_bias"],
                                             w["ssm_a_log"], w["ssm_d"])
    s["o_scan"], s["gdn_sin"], s["gdn_x"] = gdn_scan_fwd(f"gdn_scan_l{l}", s["qkv"], s["proj"],
                                                         w["gdn_a_log"], w["gdn_dt_bias"])
    s["y_ssm"] = rowwise_fwd(f"ssd_post_l{l}", f_ssd_post, t, rt, 2,
                             [_row(s["y_scan"], 512, 0, True), _row(s["proj"], 512, AL_Z // 512, True),
                              _par(w["ssm_norm_w"], 512, 0, True)], [(512, BF16)])[0]
    s["y_gdn"] = rowwise_fwd(f"gdn_post_l{l}", f_gdn_post, t, rt, GDN_HEADS,
                             [_row(s["o_scan"], LANE, 0, True), _row(s["proj"], LANE, AL_GZ // LANE, True),
                              _par(w["gdn_norm_w"], LANE)], [(LANE, BF16)])[0]
    if "late_weights" in w:
        late = w["late_weights"](s["y_gdn"])
        w = {k: v for k, v in {**w, **late}.items() if k != "late_weights"}
    s["w"] = w
    s["p1"] = matmul(f"proj_ssm_l{l}", s["y_ssm"], w["w_proj_ssm"], "nn")
    s["p2"] = matmul(f"proj_gdn_l{l}", s["y_gdn"], w["w_proj_gdn"], "nn")
    s["merged"] = rowwise_fwd(f"merge_l{l}", f_merge, t, rt, 2,
                              [_row(s["proj"], 512, AL_GS // 512, True), _row(s["p1"], 512, 0, True),
                               _row(s["proj"], 512, AL_GG // 512, True), _row(s["p2"], 512, 0, True)],
                              [(512, BF16)])[0]
    s["x1"] = matmul_add(f"out_proj_l{l}", s["merged"], w["w_out"], x)
    s["h2"] = rowwise_fwd(f"norm_ffn_l{l}", f_rmsnorm, t, rt, 1,
                          [_row(s["x1"], D_MODEL), _par(w["norm_ffn_w"], D_MODEL)], [(D_MODEL, BF16)])[0]
    s["gu"] = matmul(f"ffn_in_l{l}", s["h2"], w["w_ffn_in"], "nn")
    s["act"] = rowwise_fwd(f"swiglu_l{l}", f_swiglu, t, rt, FFN_HIDDEN // 256,
                           [_row(s["gu"], 256, 0, True), _row(s["gu"], 256, FFN_HIDDEN // 256, True)],
                           [(256, BF16)])[0]
    x2 = matmul_add(f"ffn_down_l{l}", s["act"], w["w_ffn_down"], s["x1"])
    return x2, s


IN_SHARD = IN_DIM // 4
IN_SHARD_PAD = 2304


def _aligned_to_shards(g):
    orig = jnp.concatenate([g[:, 0:2560], g[:, AL_SMALL:AL_SMALL + 16], g[:, 2560:6656],
                            g[:, AL_SMALL + 16:AL_SMALL + 32], g[:, 6656:8704]], axis=1)
    return jnp.stack([jnp.pad(orig[:, j * IN_SHARD:(j + 1) * IN_SHARD], ((0, 0), (0, IN_SHARD_PAD - IN_SHARD)))
                      for j in range(N_CHIPS)])


def layer_bwd(l, dx2, w, s):
    t = dx2.shape[0]
    rt = min(256, t)
    ct = min(512, t)
    g = {}
    dact = matmul(f"ffn_down_dx_l{l}", dx2, w["w_ffn_down"], "nt")
    g["w_ffn_down"] = matmul(f"ffn_down_dw_l{l}", s["act"], dx2, "tn")
    nf = FFN_HIDDEN // 256
    dgate, dup = rowwise_bwd(f"swiglu_bwd_l{l}", f_swiglu, t, rt, nf,
                             [_row(s["gu"], 256, 0, True), _row(s["gu"], 256, nf, True)], [True, True],
                             [_row(dact, 256, 0, True)], row_dtypes={0: BF16, 1: BF16})
    dgu = jnp.concatenate([dgate, dup], axis=1)
    dh2 = matmul(f"ffn_in_dx_l{l}", dgu, w["w_ffn_in"], "nt")
    g["w_ffn_in"] = matmul(f"ffn_in_dw_l{l}", s["h2"], dgu, "tn", chip_major=True)
    dx1, g["norm_ffn_w"] = rowwise_bwd(f"norm_ffn_bwd_l{l}", f_rmsnorm, t, rt, 1,
                                       [_row(s["x1"], D_MODEL), _par(w["norm_ffn_w"], D_MODEL)], [True, True],
                                       [_row(dh2, D_MODEL)], addends={0: _row(dx2, D_MODEL)})
    dmerged = matmul(f"out_proj_dx_l{l}", dx1, w["w_out"], "nt")
    g["w_out"] = matmul(f"out_proj_dw_l{l}", s["merged"], dx1, "tn")
    dgs, dp1, dgg, dp2 = rowwise_bwd(
        f"merge_bwd_l{l}", f_merge, t, rt, 2,
        [_row(s["proj"], 512, AL_GS // 512, True), _row(s["p1"], 512, 0, True),
         _row(s["proj"], 512, AL_GG // 512, True), _row(s["p2"], 512, 0, True)], [True] * 4,
        [_row(dmerged, 512, 0, True)], row_dtypes={0: BF16, 1: BF16, 2: BF16, 3: BF16})
    dy_ssm = matmul(f"proj_ssm_dx_l{l}", dp1, w["w_proj_ssm"], "nt")
    g["w_proj_ssm"] = matmul(f"proj_ssm_dw_l{l}", s["y_ssm"], dp1, "tn")
    dy_gdn = matmul(f"proj_gdn_dx_l{l}", dp2, w["w_proj_gdn"], "nt")
    g["w_proj_gdn"] = matmul(f"proj_gdn_dw_l{l}", s["y_gdn"], dp2, "tn")
    dy_scan, dz, g["ssm_norm_w"] = rowwise_bwd(
        f"ssd_post_bwd_l{l}", f_ssd_post, t, rt, 2,
        [_row(s["y_scan"], 512, 0, True), _row(s["proj"], 512, AL_Z // 512, True),
         _par(w["ssm_norm_w"], 512, 0, True)], [True] * 3, [_row(dy_ssm, 512, 0, True)], row_dtypes={1: BF16})
    dxbc_act, dsm, g["ssm_dt_bias"], g["ssm_a_log"], g["ssm_d"] = ssd_scan_bwd(
        f"ssd_scan_bwd_l{l}", s["xbc"], s["proj"], w["ssm_dt_bias"], w["ssm_a_log"], w["ssm_d"],
        s["ssd_sin"], dy_scan)
    dxbc, g["ssm_conv_w"], g["ssm_conv_b"] = conv_bwd(
        f"ssm_conv_bwd_l{l}", s["proj"], AL_XBC, SSM_CONV_DIM, w["ssm_conv_w"], w["ssm_conv_b"], dxbc_act, tile=ct)
    do_scan, dgz, g["gdn_norm_w"] = rowwise_bwd(
        f"gdn_post_bwd_l{l}", f_gdn_post, t, rt, GDN_HEADS,
        [_row(s["o_scan"], LANE, 0, True), _row(s["proj"], LANE, AL_GZ // LANE, True),
         _par(w["gdn_norm_w"], LANE)], [True] * 3, [_row(dy_gdn, LANE, 0, True)], row_dtypes={1: BF16})
    dqkv_act, dsm, g["gdn_a_log"], g["gdn_dt_bias"] = gdn_scan_bwd(
        f"gdn_scan_bwd_l{l}", s["qkv"], s["proj"], w["gdn_a_log"], w["gdn_dt_bias"], s["gdn_sin"],
        s["gdn_x"], do_scan, dsm)
    dqkv, g["gdn_conv_w"], _ = conv_bwd(
        f"gdn_conv_bwd_l{l}", s["proj"], AL_QKV, GDN_QKV_DIM, w["gdn_conv_w"], w["gdn_conv_b"], dqkv_act, tile=ct)
    dproj = jnp.concatenate([dz, dxbc, dqkv, dgz, dgs, dgg, dsm.astype(BF16),
                             jnp.zeros((t, AL_DIM - AL_SMALL - LANE), BF16)], axis=1)
    dh = matmul(f"in_proj_dx_l{l}", dproj, w["w_in"], "nt")
    g["w_in"] = matmul(f"in_proj_dw_l{l}", s["h"], dproj, "tn")
    dx0, g["norm_mix_w"] = rowwise_bwd(f"norm_mix_bwd_l{l}", f_rmsnorm, t, rt, 1,
                                       [_row(s["x"], D_MODEL), _par(w["norm_mix_w"], D_MODEL)], [True, True],
                                       [_row(dh, D_MODEL)], addends={0: _row(dx1, D_MODEL)})
    return dx0, g


def _align_w_in(w):
    pad = jnp.zeros((w.shape[0], AL_DIM - AL_SMALL - 32), w.dtype)
    return jnp.concatenate([w[:, 0:2560], w[:, 2576:6672], w[:, 6688:8736],
                            w[:, 2560:2576], w[:, 6672:6688], pad], axis=1)


def _pad_lane(v, at=0):
    return jnp.pad(v[None], ((0, 0), (at, LANE - at - v.shape[0])))


def local_step(x, target, full, gathered_of=None, on_layer_grads=None):
    if gathered_of is None:
        gathered_of = lambda l, after: {n: full[n][l] for n, _ in SHARDED}
    ws, saved = [], []
    h = x
    for l in range(DEPTH):
        gw = gathered_of(l, h)
        ws.append({
            "norm_mix_w": full["norm_mix_w"][l][None], "w_in": _align_w_in(gw["w_in"]),
            "ssm_conv_w": gw["ssm_conv_w"], "ssm_conv_b": full["ssm_conv_b"][l][None],
            "ssm_dt_bias": _pad_lane(full["ssm_dt_bias"][l]), "ssm_a_log": _pad_lane(full["ssm_a_log"][l]),
            "ssm_d": _pad_lane(full["ssm_d"][l]), "ssm_norm_w": full["ssm_norm_w"][l][None],
            "gdn_conv_w": gw["gdn_conv_w"], "gdn_conv_b": jnp.zeros((1, GDN_QKV_DIM), F32),
            "gdn_a_log": _pad_lane(full["gdn_a_log"][l], SM_A),
            "gdn_dt_bias": _pad_lane(full["gdn_dt_bias"][l], SM_A),
            "gdn_norm_w": full["gdn_norm_w"][l][None],
            "norm_ffn_w": full["norm_ffn_w"][l][None],
            **{n: gw[n] for n in ("w_proj_ssm", "w_proj_gdn", "w_out", "w_ffn_in", "w_ffn_down", "late_weights")
               if n in gw},
        })
        h, s = layer_fwd(l, h, ws[l])
        ws[l] = s.pop("w")
        saved.append(s)
    loss, dx, g_final = final_loss("final_loss", h, target, full["final_norm_w"][None], tile=min(256, x.shape[0]))
    per_layer = [None] * DEPTH
    matmul_grads = [None] * DEPTH
    for l in reversed(range(DEPTH)):
        dx, per_layer[l] = layer_bwd(l, dx, ws[l], saved[l])
        matmul_grads[l] = {n: per_layer[l].pop(n) for n, _ in BIG}
        if on_layer_grads is not None:
            dx = on_layer_grads(l, matmul_grads[l], dx)
    grads = {"final_norm_w": g_final[0]}
    if on_layer_grads is None:
        grads.update({n: jnp.stack([matmul_grads[l][n] for l in range(DEPTH)]) for n, _ in BIG})
    for name in per_layer[0]:
        rows = []
        for l in range(DEPTH):
            gl = per_layer[l][name]
            if name in ("ssm_dt_bias", "ssm_a_log", "ssm_d"):
                gl = gl[0, :SSM_HEADS]
            elif name in ("gdn_a_log", "gdn_dt_bias"):
                gl = gl[0, SM_A:SM_A + GDN_HEADS]
            elif name in ("norm_mix_w", "ssm_conv_b", "ssm_norm_w", "gdn_norm_w", "norm_ffn_w"):
                gl = gl[0]
            rows.append(gl)
        grads[name] = jnp.stack(rows)
    return loss, dx, grads


MESH = pl.DeviceIdType.MESH
HBM = pl.BlockSpec(memory_space=pltpu.HBM)
N_DEV = 8


def _pos():
    return lax.axis_index("x"), lax.axis_index("y"), lax.axis_index("c")


def _rcopy(src, dst, send_sem, recv_sem, dev):
    return pltpu.make_async_remote_copy(src_ref=src, dst_ref=dst, send_sem=send_sem, recv_sem=recv_sem,
                                        device_id=dev, device_id_type=MESH)


RELATIONS = (2, 1, 3)


def _related_chip(x, y, mask):
    return (1 - x if mask & 2 else x, 1 - y if mask & 1 else y)


def weights_gather(name, bufs):
    n = len(bufs)

    def body(*refs):
        outs, send_sems, recv_sems = refs[n:2 * n], refs[2 * n], refs[2 * n + 1]
        x, y, c = _pos()
        sib = (x, y, 1 - c)
        sends = []
        for i, a in enumerate(outs):
            for k, m in enumerate(RELATIONS):
                px, py = _related_chip(x, y, m)
                cp = _rcopy(a.at[0, c], a.at[m, c], send_sems.at[6 * i + k], recv_sems.at[6 * i + k], (px, py, c))
                cp.start()
                sends.append(cp)
        for i, a in enumerate(outs):
            for k, m in enumerate(RELATIONS):
                px, py = _related_chip(x, y, m)
                _rcopy(a.at[0, c], a.at[m, c], send_sems.at[6 * i + k], recv_sems.at[6 * i + k],
                       (px, py, c)).wait_recv()
                fw = _rcopy(a.at[m, c], a.at[m, c], send_sems.at[6 * i + 3 + k], recv_sems.at[6 * i + 3 + k], sib)
                fw.start()
                sends.append(fw)
        for i, a in enumerate(outs):
            for k, m in enumerate(RELATIONS):
                _rcopy(a.at[m, 1 - c], a.at[m, 1 - c], send_sems.at[6 * i + 3 + k], recv_sems.at[6 * i + 3 + k],
                       sib).wait_recv()
        for cp in sends:
            cp.wait_send()

    return pl.pallas_call(
        body, name=name, out_shape=[jax.ShapeDtypeStruct(b.shape, b.dtype) for b in bufs],
        in_specs=[HBM] * n, out_specs=[HBM] * n,
        input_output_aliases={i: i for i in range(n)},
        scratch_shapes=[pltpu.SemaphoreType.DMA((6 * n,)), pltpu.SemaphoreType.DMA((6 * n,))],
    )(*bufs)


SEM = pl.BlockSpec(memory_space=pltpu.SEMAPHORE)
DATAFLOW = pltpu.SideEffectType.DATAFLOW_SIDE_EFFECTING


def gather_start(name, bufs, after):
    n = len(bufs)

    def body(*refs):
        ins = refs[:n]
        send_sems, recv_sems, token = refs[n + 1], refs[n + 2], refs[2 * n + 3]
        x, y, c = _pos()
        for i, a in enumerate(ins):
            for k, m in enumerate(RELATIONS):
                px, py = _related_chip(x, y, m)
                _rcopy(a.at[0, c], a.at[m, c], send_sems.at[3 * i + k], recv_sems.at[3 * i + k], (px, py, c)).start()
        token[...] = jnp.zeros_like(token)

    res = pl.pallas_call(
        body, name=name,
        out_shape=(pltpu.SemaphoreType.DMA((3 * n,)), pltpu.SemaphoreType.DMA((3 * n,)),
                   *[pltpu.HBM(b.shape, b.dtype) for b in bufs], jax.ShapeDtypeStruct((8, LANE), F32)),
        in_specs=[HBM] * n + [pl.BlockSpec(memory_space=pl.ANY)],
        out_specs=(SEM, SEM, *[HBM] * n, pl.BlockSpec(memory_space=pltpu.VMEM)),
        input_output_aliases={i: 2 + i for i in range(n)},
        compiler_params=pltpu.CompilerParams(has_side_effects=DATAFLOW),
    )(*[pltpu.with_memory_space_constraint(b, pltpu.HBM) for b in bufs], after)
    return res[0], res[1], list(res[2:2 + n]), res[2 + n]


def gather_wait(name, send_sems, recv_sems, bufs, after):
    n = len(bufs)

    def body(*refs):
        ins, ssem, rsem = refs[:n], refs[n], refs[n + 1]
        x, y, c = _pos()
        for i, a in enumerate(ins):
            for k, m in enumerate(RELATIONS):
                px, py = _related_chip(x, y, m)
                cp = _rcopy(a.at[0, c], a.at[m, c], ssem.at[3 * i + k], rsem.at[3 * i + k], (px, py, c))
                cp.wait_send()
                cp.wait_recv()

    return pl.pallas_call(
        body, name=name,
        out_shape=[pltpu.HBM(b.shape, b.dtype) for b in bufs],
        in_specs=[HBM] * n + [SEM, SEM, pl.BlockSpec(memory_space=pl.ANY)],
        out_specs=[HBM] * n,
        input_output_aliases={i: i for i in range(n)},
        compiler_params=pltpu.CompilerParams(has_side_effects=DATAFLOW),
    )(*bufs, send_sems, recv_sems, after)


def weights_forward(name, bufs):
    n = len(bufs)

    def body(*refs):
        outs, send_sems, recv_sems = refs[n:2 * n], refs[2 * n], refs[2 * n + 1]
        x, y, c = _pos()
        sib = (x, y, 1 - c)
        sends = []
        for i, a in enumerate(outs):
            for k, m in enumerate(RELATIONS):
                fw = _rcopy(a.at[m, c], a.at[m, c], send_sems.at[3 * i + k], recv_sems.at[3 * i + k], sib)
                fw.start()
                sends.append(fw)
        for i, a in enumerate(outs):
            for k, m in enumerate(RELATIONS):
                _rcopy(a.at[m, 1 - c], a.at[m, 1 - c], send_sems.at[3 * i + k], recv_sems.at[3 * i + k],
                       sib).wait_recv()
        for cp in sends:
            cp.wait_send()

    return pl.pallas_call(
        body, name=name, out_shape=[jax.ShapeDtypeStruct(b.shape, b.dtype) for b in bufs],
        in_specs=[HBM] * n, out_specs=[HBM] * n,
        input_output_aliases={i: i for i in range(n)},
        scratch_shapes=[pltpu.SemaphoreType.DMA((3 * n,)), pltpu.SemaphoreType.DMA((3 * n,))],
    )(*bufs)


def pair_swap(name, gs):
    n = len(gs)
    offs = [0]
    for g in gs:
        offs.append(offs[-1] + g.shape[0])

    def body(*refs):
        srcs, outs, send_sems, recv_sems = refs[:n], refs[n:2 * n], refs[2 * n], refs[2 * n + 1]
        x, y, c = _pos()
        cps = [_rcopy(s.at[j, 1 - c], o.at[j], send_sems.at[offs[i] + j], recv_sems.at[offs[i] + j], (x, y, 1 - c))
               for i, (s, o) in enumerate(zip(srcs, outs)) for j in range(s.shape[0])]
        for cp in cps:
            cp.start()
        for cp in cps:
            cp.wait()

    return pl.pallas_call(
        body, name=name, out_shape=[jax.ShapeDtypeStruct(g.shape[:1] + g.shape[2:], g.dtype) for g in gs],
        in_specs=[HBM] * n, out_specs=[HBM] * n,
        scratch_shapes=[pltpu.SemaphoreType.DMA((offs[-1],)), pltpu.SemaphoreType.DMA((offs[-1],))],
    )(*gs)


def scatter_start(name, ss, after):
    n = len(ss)
    lands = [lax.empty((3,) + s.shape[1:], s.dtype) for s in ss]

    def body(*refs):
        srcs, dsts = refs[:n], refs[n:2 * n]
        send_sems, recv_sems, token = refs[2 * n + 1], refs[2 * n + 2], refs[4 * n + 3]
        x, y, c = _pos()
        for i, (s, o) in enumerate(zip(srcs, dsts)):
            for k, m in enumerate(RELATIONS):
                px, py = _related_chip(x, y, m)
                _rcopy(s.at[2 * px + py], o.at[k], send_sems.at[3 * i + k], recv_sems.at[3 * i + k],
                       (px, py, c)).start()
        token[...] = jnp.zeros_like(token)

    both = list(ss) + lands
    res = pl.pallas_call(
        body, name=name,
        out_shape=(pltpu.SemaphoreType.DMA((3 * n,)), pltpu.SemaphoreType.DMA((3 * n,)),
                   *[pltpu.HBM(b.shape, b.dtype) for b in both], jax.ShapeDtypeStruct((8, LANE), F32)),
        in_specs=[HBM] * (2 * n) + [pl.BlockSpec(memory_space=pl.ANY)],
        out_specs=(SEM, SEM, *[HBM] * (2 * n), pl.BlockSpec(memory_space=pltpu.VMEM)),
        input_output_aliases={i: 2 + i for i in range(2 * n)},
        compiler_params=pltpu.CompilerParams(has_side_effects=DATAFLOW),
    )(*[pltpu.with_memory_space_constraint(b, pltpu.HBM) for b in both], after)
    return res[0], res[1], list(res[2:2 + n]), list(res[2 + n:2 + 2 * n]), res[2 + 2 * n]


def scatter_wait(name, send_sems, recv_sems, ss, lands, after):
    n = len(ss)

    def body(*refs):
        srcs, dsts, ssem, rsem = refs[:n], refs[n:2 * n], refs[2 * n], refs[2 * n + 1]
        x, y, c = _pos()
        for i, (s, o) in enumerate(zip(srcs, dsts)):
            for k, m in enumerate(RELATIONS):
                px, py = _related_chip(x, y, m)
                cp = _rcopy(s.at[2 * px + py], o.at[k], ssem.at[3 * i + k], rsem.at[3 * i + k], (px, py, c))
                cp.wait_send()
                cp.wait_recv()

    both = list(ss) + list(lands)
    res = pl.pallas_call(
        body, name=name,
        out_shape=[pltpu.HBM(b.shape, b.dtype) for b in both],
        in_specs=[HBM] * (2 * n) + [SEM, SEM, pl.BlockSpec(memory_space=pl.ANY)],
        out_specs=[HBM] * (2 * n),
        input_output_aliases={i: i for i in range(2 * n)},
        compiler_params=pltpu.CompilerParams(has_side_effects=DATAFLOW),
    )(*both, send_sems, recv_sems, after)
    return list(res[:n]), list(res[n:])


def pair_share(name, bufs):
    n = len(bufs)

    def body(*refs):
        outs, send_sems, recv_sems = refs[n:2 * n], refs[2 * n], refs[2 * n + 1]
        x, y, c = _pos()
        sends = []
        for i, o in enumerate(outs):
            cp = _rcopy(o.at[c], o.at[c], send_sems.at[i], recv_sems.at[i], (x, y, 1 - c))
            cp.start()
            sends.append(cp)
        for i, o in enumerate(outs):
            _rcopy(o.at[1 - c], o.at[1 - c], send_sems.at[i], recv_sems.at[i], (x, y, 1 - c)).wait_recv()
        for cp in sends:
            cp.wait_send()

    return pl.pallas_call(
        body, name=name, out_shape=[jax.ShapeDtypeStruct(b.shape, b.dtype) for b in bufs],
        in_specs=[HBM] * n, out_specs=[HBM] * n,
        input_output_aliases={i: i for i in range(n)},
        scratch_shapes=[pltpu.SemaphoreType.DMA((n,)), pltpu.SemaphoreType.DMA((n,))],
    )(*bufs)


def all_allgather(name, buf):
    r, cd = buf.shape

    def body(src, out, send_sems, recv_sems, lsem):
        x, y, c = _pos()
        me = 4 * x + 2 * y + c
        local = pltpu.make_async_copy(src, out.at[me], lsem)
        local.start()

        def peer(mask):
            px = 1 - x if mask & 4 else x
            py = 1 - y if mask & 2 else y
            pc = 1 - c if mask & 1 else c
            return px, py, pc

        sends = []
        for mask in range(1, N_DEV):
            cp = _rcopy(src, out.at[me], send_sems.at[mask - 1], recv_sems.at[mask - 1], peer(mask))
            cp.start()
            sends.append(cp)
        for mask in range(1, N_DEV):
            px, py, pc = peer(mask)
            _rcopy(src, out.at[4 * px + 2 * py + pc], send_sems.at[mask - 1], recv_sems.at[mask - 1],
                   (px, py, pc)).wait_recv()
        for cp in sends:
            cp.wait_send()
        local.wait()

    return pl.pallas_call(
        body, name=name, out_shape=jax.ShapeDtypeStruct((N_DEV, r, cd), buf.dtype),
        in_specs=[HBM], out_specs=HBM,
        scratch_shapes=[pltpu.SemaphoreType.DMA((N_DEV - 1,)), pltpu.SemaphoreType.DMA((N_DEV - 1,)),
                        pltpu.SemaphoreType.DMA(())],
    )(buf)


ELEMENTWISE_BLOCK_BYTES = 2 << 20


def _row_block(rows, cols):
    for cand in (1024, 512, 256, 128, 64, 32, 16):
        if rows % cand == 0 and cand * cols * 4 <= ELEMENTWISE_BLOCK_BYTES:
            return cand
    return rows


def chip_sum(name, s, r, me, c):
    _, a, b = s.shape
    tr = _row_block(a, b)

    def body(idx_ref, s_ref, r_ref, o_ref):
        del idx_ref
        acc = s_ref[...].astype(F32)
        for k in range(3):
            acc = acc + r_ref[k].astype(F32)
        o_ref[...] = acc

    return pl.pallas_call(
        body, name=name, out_shape=jax.ShapeDtypeStruct((2, a, b), F32),
        grid_spec=pltpu.PrefetchScalarGridSpec(
            num_scalar_prefetch=1, grid=(a // tr,),
            in_specs=[pl.BlockSpec((None, tr, b), lambda i, idx: (idx[0], i, 0)),
                      pl.BlockSpec((3, tr, b), lambda i, idx: (0, i, 0))],
            out_specs=pl.BlockSpec((None, tr, b), lambda i, idx: (idx[1], i, 0))),
        compiler_params=_cparams(("arbitrary",)),
    )(jnp.stack([me, c]).astype(jnp.int32), s, r)


def pair_add(name, p, recv, c):
    nj, _, rh, cd = p.shape
    tr = _row_block(rh, cd)

    def body(c_ref, p_ref, r_ref, o_ref):
        del c_ref
        o_ref[...] = (p_ref[0] + r_ref[...]).astype(o_ref.dtype)

    return pl.pallas_call(
        body, name=name, out_shape=jax.ShapeDtypeStruct((nj, rh, cd), BF16),
        grid_spec=pltpu.PrefetchScalarGridSpec(
            num_scalar_prefetch=1, grid=(nj, rh // tr),
            in_specs=[pl.BlockSpec((1, 1, tr, cd), lambda j, i, c_ref: (j, c_ref[0], i, 0)),
                      pl.BlockSpec((1, tr, cd), lambda j, i, c_ref: (j, i, 0))],
            out_specs=pl.BlockSpec((1, tr, cd), lambda j, i, c_ref: (j, i, 0))),
        compiler_params=_cparams(("arbitrary", "arbitrary")),
    )(jnp.reshape(c, (1,)).astype(jnp.int32), p, recv)


def slab_sum(name, a):
    n, r, cd = a.shape
    tr = _pick(r, (256, 128, 64, 32, 16, 8))

    def body(a_ref, o_ref):
        acc = a_ref[0].astype(F32)
        for j in range(1, n):
            acc = acc + a_ref[j].astype(F32)
        o_ref[...] = acc

    return pl.pallas_call(
        body, name=name, out_shape=jax.ShapeDtypeStruct((r, cd), F32),
        grid=(r // tr,),
        in_specs=[pl.BlockSpec((n, tr, cd), lambda i: (0, i, 0))],
        out_specs=pl.BlockSpec((tr, cd), lambda i: (i, 0)),
        compiler_params=_cparams(("arbitrary",)),
    )(a)


ADAM_C1 = 1.0 - ADAM_B1 ** ADAM_STEP
ADAM_C2 = 1.0 - ADAM_B2 ** ADAM_STEP


def adamw(name, w, g, m, v):
    r, cd = w.shape
    tr = r
    for cand in (512, 256, 128, 64, 32, 16, 8):
        if r % cand == 0 and cand * cd * 4 <= (1 << 20):
            tr = cand
            break

    def body(w_ref, g_ref, m_ref, v_ref, d_ref, nm_ref, nv_ref):
        gv = g_ref[...]
        nm = ADAM_B1 * m_ref[...] + (1.0 - ADAM_B1) * gv
        nv = ADAM_B2 * v_ref[...] + (1.0 - ADAM_B2) * (gv * gv)
        m_hat = nm / ADAM_C1
        v_hat = nv / ADAM_C2
        d_ref[...] = -ADAM_LR * (m_hat / (jnp.sqrt(v_hat) + ADAM_EPS) + ADAM_WD * w_ref[...])
        nm_ref[...] = nm
        nv_ref[...] = nv

    spec = pl.BlockSpec((tr, cd), lambda i: (i, 0))
    sd = jax.ShapeDtypeStruct((r, cd), F32)
    return pl.pallas_call(
        body, name=name, out_shape=[sd, sd, sd], grid=(r // tr,),
        in_specs=[spec] * 4, out_specs=[spec] * 3,
        compiler_params=_cparams(("arbitrary",)),
    )(w, g, m, v)


WEIGHTS = ("norm_mix_w", "w_in", "ssm_conv_w", "ssm_conv_b", "ssm_dt_bias", "ssm_a_log", "ssm_d", "ssm_norm_w",
           "gdn_conv_w", "gdn_a_log", "gdn_dt_bias", "gdn_norm_w", "w_proj_ssm", "w_proj_gdn", "w_out",
           "norm_ffn_w", "w_ffn_in", "w_ffn_down", "final_norm_w")
BIG = (("w_in", 2), ("w_proj_ssm", 1), ("w_proj_gdn", 1), ("w_out", 1), ("w_ffn_in", 2), ("w_ffn_down", 1))
CONVW = (("ssm_conv_w", 2), ("gdn_conv_w", 2))
SHARDED = BIG + CONVW
SMALL = tuple(n for n in WEIGHTS if n not in dict(SHARDED))


def _unpack(buf, shapes, lead=()):
    flat = buf.reshape(lead + (-1,))
    out, o = [], 0
    for shp in shapes:
        n = math.prod(shp)
        out.append(flat[..., o:o + n].reshape(lead + tuple(shp)))
        o += n
    return out


def _pack_rows(arrs, lead=(), mult=8):
    nl = len(lead)
    flat = jnp.concatenate([a.reshape(lead + (-1,)) for a in arrs], axis=nl)
    n = flat.shape[nl]
    rows = -(-n // (mult * LANE)) * mult
    flat = jnp.pad(flat, [(0, 0)] * nl + [(0, rows * LANE - n)])
    return flat.reshape(lead + (rows, LANE))


def _slot_buffer(shard):
    return lax.dynamic_update_slice(lax.empty((N_CHIPS,) + shard.shape, shard.dtype), shard[None],
                                    (0,) * (shard.ndim + 1))


def kernel(x, norm_mix_w, w_in, ssm_conv_w, ssm_conv_b, ssm_dt_bias, ssm_a_log, ssm_d, ssm_norm_w, gdn_conv_w, gdn_a_log, gdn_dt_bias, gdn_norm_w, w_proj_ssm, w_proj_gdn, w_out, norm_ffn_w, w_ffn_in, w_ffn_down, final_norm_w, loss_target, m_norm_mix_w, m_w_in, m_ssm_conv_w, m_ssm_conv_b, m_ssm_dt_bias, m_ssm_a_log, m_ssm_d, m_ssm_norm_w, m_gdn_conv_w, m_gdn_a_log, m_gdn_dt_bias, m_gdn_norm_w, m_w_proj_ssm, m_w_proj_gdn, m_w_out, m_norm_ffn_w, m_w_ffn_in, m_w_ffn_down, m_final_norm_w, v_norm_mix_w, v_w_in, v_ssm_conv_w, v_ssm_conv_b, v_ssm_dt_bias, v_ssm_a_log, v_ssm_d, v_ssm_norm_w, v_gdn_conv_w, v_gdn_a_log, v_gdn_dt_bias, v_gdn_norm_w, v_w_proj_ssm, v_w_proj_gdn, v_w_out, v_norm_ffn_w, v_w_ffn_in, v_w_ffn_down, v_final_norm_w):
    wl = (norm_mix_w, w_in, ssm_conv_w, ssm_conv_b, ssm_dt_bias, ssm_a_log, ssm_d, ssm_norm_w, gdn_conv_w,
          gdn_a_log, gdn_dt_bias, gdn_norm_w, w_proj_ssm, w_proj_gdn, w_out, norm_ffn_w, w_ffn_in, w_ffn_down,
          final_norm_w)
    ml = (m_norm_mix_w, m_w_in, m_ssm_conv_w, m_ssm_conv_b, m_ssm_dt_bias, m_ssm_a_log, m_ssm_d, m_ssm_norm_w,
          m_gdn_conv_w, m_gdn_a_log, m_gdn_dt_bias, m_gdn_norm_w, m_w_proj_ssm, m_w_proj_gdn, m_w_out,
          m_norm_ffn_w, m_w_ffn_in, m_w_ffn_down, m_final_norm_w)
    vl = (v_norm_mix_w, v_w_in, v_ssm_conv_w, v_ssm_conv_b, v_ssm_dt_bias, v_ssm_a_log, v_ssm_d, v_ssm_norm_w,
          v_gdn_conv_w, v_gdn_a_log, v_gdn_dt_bias, v_gdn_norm_w, v_w_proj_ssm, v_w_proj_gdn, v_w_out,
          v_norm_ffn_w, v_w_ffn_in, v_w_ffn_down, v_final_norm_w)
    w = dict(zip(WEIGHTS, wl))
    m = dict(zip(WEIGHTS, ml))
    v = dict(zip(WEIGHTS, vl))
    x_pos, y_pos, c = _pos()
    me = 2 * x_pos + y_pos
    big = [n for n, _ in BIG]

    shards = [w[n].astype(BF16) for n in big]
    shards[0] = jnp.pad(shards[0], ((0, 0), (0, 0), (0, IN_SHARD_PAD - IN_SHARD)))
    conv_shapes = [w[n].shape[1:] for n, _ in CONVW]
    conv_pack = _pack_rows([w[n] for n, _ in CONVW], lead=(DEPTH,), mult=16)

    def slot_buffers(l):
        return [_slot_buffer(s[l].reshape((2, s.shape[1] // 2) + s.shape[2:])) for s in shards + [conv_pack]]

    def assemble(bufs, which):
        out = {}
        for g_, i in zip(bufs, which):
            by_chip = [lax.dynamic_index_in_dim(g_, jnp.bitwise_xor(me, j), 0, keepdims=False)
                       .reshape((-1,) + g_.shape[3:]) for j in range(N_CHIPS)]
            if i < 0:
                conv_parts = [_unpack(p, conv_shapes) for p in by_chip]
                for k, (n, axis) in enumerate(CONVW):
                    out[n] = jnp.concatenate([conv_parts[j][k] for j in range(N_CHIPS)], axis=axis - 1)
            else:
                n, axis = BIG[i]
                cols = IN_SHARD if n == "w_in" else by_chip[0].shape[-1]
                out[n] = jnp.concatenate([p[:, :cols] for p in by_chip], axis=axis - 1)
        return out

    everything = list(range(len(BIG))) + [-1]
    first, rest = [0, -1], everything[1:-1]
    bufs0 = slot_buffers(0)
    landed_first = weights_gather("gather_w_l0_first", [bufs0[i] for i in first])
    rest_sems = gather_start("gather_w_l0_rest_start", [bufs0[i] for i in rest], landed_first[0])
    l1_sems = gather_start("gather_w_l1_start", slot_buffers(1), rest_sems[2][0])

    def land(name, flying, after):
        send_sems, recv_sems, bufs, _ = flying
        return weights_forward(f"{name}_forward", gather_wait(f"{name}_wait", send_sems, recv_sems, bufs, after))

    def gathered_of(l, after):
        if l == 0:
            return {**assemble(landed_first, first),
                    "late_weights": lambda later: assemble(land("gather_w_l0_rest", rest_sems, later), rest)}
        return assemble(land("gather_w_l1", l1_sems, after), everything)

    token = rest_sems[3] + l1_sems[3]

    in_flight = {}

    def start_reduction(l, g_layer, dx):
        halves = [g_layer[n].reshape((1, 2, -1, AL_DIM)) if n == "w_in"
                  else g_layer[n].reshape((N_CHIPS, 2, -1) + g_layer[n].shape[-1:]) for n in big]
        from_pair = pair_swap(f"grad_pair_swap_l{l}", halves)
        chip_part = [pair_add(f"grad_pair_add_{n}_l{l}", g_, r_, c) for n, g_, r_ in zip(big, halves, from_pair)]
        chip_part[0] = _aligned_to_shards(chip_part[0][0])
        *in_flight[l], token_l = scatter_start(f"grad_scatter_start_l{l}", chip_part, dx)
        return dx + token_l[0, 0]

    def finish_reduction(l, after):
        parts, landed = scatter_wait(f"grad_scatter_wait_l{l}", *in_flight[l], after)
        sums = [chip_sum(f"grad_chip_sum_{n}_l{l}", s_, r_, me, c) for n, s_, r_ in zip(big, parts, landed)]
        return [r_.reshape((-1,) + r_.shape[2:]) for r_ in pair_share(f"grad_pair_share_l{l}", sums)]

    full = {n: w[n] for n in SMALL}
    loss_part, grad_x, grads = local_step(x[0] + token[0, 0], loss_target[0], full, gathered_of, start_reduction)
    reduced = [None] * DEPTH
    reduced[1] = finish_reduction(1, grad_x)

    small_names = list(SMALL) + [n for n, _ in CONVW]
    small_all = all_allgather("gather_small", _pack_rows([grads[n] for n in small_names] + [loss_part[0, :1]]))
    small_sum = slab_sum("small_sum", small_all)
    small_vals = _unpack(small_sum, [grads[n].shape for n in small_names] + [(1,)])
    g_small = dict(zip(small_names, small_vals[:-1]))
    loss = small_vals[-1].reshape(())
    out_g, out_d, out_m, out_v = {}, {}, {}, {}
    d_, m_, v_ = adamw("adamw_small", *[_pack_rows([d[n] for n in SMALL]) for d in (w, g_small, m, v)])
    small_shapes = [w[n].shape for n in SMALL]
    for n, dd, mm, vv in zip(SMALL, _unpack(d_, small_shapes), _unpack(m_, small_shapes), _unpack(v_, small_shapes)):
        out_g[n], out_d[n], out_m[n], out_v[n] = g_small[n], dd, mm, vv

    reduced[0] = finish_reduction(0, d_[:1] + reduced[1][0][:1, :LANE])
    g_sharded = {n: jnp.stack([reduced[l][i] for l in range(DEPTH)]) for i, n in enumerate(big)}
    g_sharded["w_in"] = g_sharded["w_in"][:, :, :IN_SHARD]
    for n, axis in CONVW:
        size = w[n].shape[axis]
        g_sharded[n] = lax.dynamic_slice_in_dim(g_small.pop(n), me * size, size, axis=axis)

    for n, _ in SHARDED:
        shp = w[n].shape
        two = lambda a: a.reshape(-1, shp[-1])
        d_, m_, v_ = adamw(f"adamw_{n}", two(w[n]), two(g_sharded[n]), two(m[n]), two(v[n]))
        out_g[n], out_d[n], out_m[n], out_v[n] = g_sharded[n], d_.reshape(shp), m_.reshape(shp), v_.reshape(shp)

    return (loss, grad_x[None], *[out_g[n] for n in WEIGHTS], *[out_d[n] for n in WEIGHTS],
            *[out_m[n] for n in WEIGHTS], *[out_v[n] for n in WEIGHTS])
```

```python
import math

import jax
import jax.numpy as jnp
from jax import lax
from jax.experimental import pallas as pl
from jax.experimental.pallas import tpu as pltpu

F32 = jnp.float32
BF16 = jnp.bfloat16

D_MODEL = 1024
DEPTH = 2
SSM_HEADS = 16
SSM_HEAD_DIM = 64
SSM_D_INNER = 1024
SSM_STATE = 128
SSM_CONV_DIM = 1536
GDN_HEADS = 8
GDN_HEAD = 128
GDN_QKV_DIM = 3072
CONV_K = 4
CHUNK = 128
SSD_CHUNK = 256
FFN_HIDDEN = 2816
EPS = 1e-6
IN_DIM = 8736

ADAM_LR = 0.001
ADAM_B1 = 0.9
ADAM_B2 = 0.999
ADAM_EPS = 1e-08
ADAM_WD = 0.01
ADAM_STEP = 10

LANE = 128
NEG_BIG = -1e30
VMEM_LIMIT = 56 * 1024 * 1024

AL_Z, AL_XBC, AL_QKV, AL_GZ, AL_GS, AL_GG, AL_SMALL, AL_DIM = 0, 1024, 2560, 5632, 6656, 7680, 8704, 9216
SM_DT, SM_A, SM_B = 0, 16, 24

HI = lax.Precision.HIGHEST
NN = (((1,), (0,)), ((), ()))
NT = (((1,), (1,)), ((), ()))
TN = (((0,), (0,)), ((), ()))


def _cparams(sem):
    return pltpu.CompilerParams(dimension_semantics=sem, vmem_limit_bytes=VMEM_LIMIT)


def _pick(n, prefs):
    for p in prefs:
        if n % p == 0:
            return p
    return n


MATMUL_VMEM_BUDGET = 40 << 20
MXU_WIDTH = 256
HBM_BYTES_PER_S = 3.3e12
MXU_FLOPS_PER_S = 9.0e14
GRID_STEP_S = 0.35e-6
N_CHIPS = 4


def _matmul_tiles(m, n, n_dom, k, a_item, b_item, o_item):
    best = None
    def cands(dim, cap):
        return [c for c in range(LANE, min(dim, cap) + 1, LANE) if dim % c == 0] or [dim]

    tms, tns, tks = cands(m, 2048), cands(n_dom, 2304), cands(k, 1 << 30)
    for tm in tms:
        for tn in tns:
            for tk in tks:
                nk = k // tk
                vmem = (2 * (tm * tk * a_item + tk * tn * b_item + tm * tn * o_item) + tm * tn * 4 * (2 if nk > 1 else 1)
                        + (tm * tk * 2 if a_item > 2 else 0) + (tk * tn * 2 if b_item > 2 else 0))
                if vmem > MATMUL_VMEM_BUDGET:
                    continue
                traffic = m * k * a_item * (1 if nk == 1 else n // tn) + k * n * b_item * (m // tm)
                mxu_fill = tn / (-(-tn // MXU_WIDTH) * MXU_WIDTH)
                cost = (max(traffic / HBM_BYTES_PER_S, 2.0 * m * n * k / (MXU_FLOPS_PER_S * mxu_fill))
                        + (m // tm) * (n // tn) * nk * GRID_STEP_S)
                if best is None or cost < best[0]:
                    best = (cost, (tm, tn, tk))
    return best[1]


def _dot(a, b, dims=NN):
    return lax.dot_general(a.astype(BF16), b.astype(BF16), dims, preferred_element_type=F32)


def _dot3(a, b, dims=NN):
    a_hi, b_hi = a.astype(BF16), b.astype(BF16)
    a_lo = (a - a_hi.astype(F32)).astype(BF16)
    b_lo = (b - b_hi.astype(F32)).astype(BF16)

    def dg(u, v):
        return lax.dot_general(u, v, dims, preferred_element_type=F32)

    return dg(a_hi, b_hi) + (dg(a_hi, b_lo) + dg(a_lo, b_hi))


def _dot_hi(a, b, dims=NN):
    return lax.dot_general(a, b, dims, precision=HI, preferred_element_type=F32)


def _sigmoid(x):
    return jax.nn.sigmoid(x)


def _silu(x):
    return x * _sigmoid(x)


def _softplus(x):
    return jnp.maximum(x, 0.0) + jnp.log1p(jnp.exp(-jnp.abs(x)))


def matmul(name, a, b, mode, out_dtype=F32, chip_major=False, stack=None):
    if mode == "nn":
        (m, k), (k2, n) = a.shape, b.shape
    elif mode == "nt":
        (m, k), (n, k2) = a.shape, b.shape
    else:
        (k, m), (k2, n) = a.shape, b.shape
    assert k == k2, (a.shape, b.shape, mode)
    shard = n // N_CHIPS if chip_major else n
    tm, tn, tk = _matmul_tiles(m, n, shard, k, a.dtype.itemsize, b.dtype.itemsize, jnp.dtype(out_dtype).itemsize)
    if chip_major:
        per = shard // tn
        base_shape, base_blk = (N_CHIPS, m, shard), (None, tm, tn)
        base_idx = lambda i, j: (j // per, i, j % per)
    else:
        base_shape, base_blk = (m, n), (tm, tn)
        base_idx = lambda i, j: (i, j)
    nk = k // tk
    dims = {"nn": NN, "nt": NT, "tn": TN}[mode]

    def body_acc(a_ref, b_ref, o_ref, acc_ref):
        kk = pl.program_id(2)

        @pl.when(kk == 0)
        def _():
            acc_ref[...] = jnp.zeros_like(acc_ref)

        acc_ref[...] += _dot(a_ref[...], b_ref[...], dims)

        @pl.when(kk == nk - 1)
        def _():
            o_ref[...] = acc_ref[...].astype(o_ref.dtype)

    def body_one(a_ref, b_ref, o_ref):
        o_ref[...] = _dot(a_ref[...], b_ref[...], dims).astype(o_ref.dtype)

    compute = body_one if nk == 1 else body_acc
    if mode == "tn":
        a_spec = pl.BlockSpec((tk, tm), lambda i, j, kk: (kk, i))
    else:
        a_spec = pl.BlockSpec((tm, tk), lambda i, j, kk: (i, kk))
    if mode == "nt":
        b_spec = pl.BlockSpec((tn, tk), lambda i, j, kk: (j, kk))
    else:
        b_spec = pl.BlockSpec((tk, tn), lambda i, j, kk: (kk, j))
    in_specs, operands, aliases, body = [a_spec, b_spec], [a, b], {}, compute
    if stack is None:
        out_shape, out_blk, out_idx = base_shape, base_blk, (lambda i, j, kk: base_idx(i, j))
    else:
        layer, buf = stack
        out_shape, out_blk = (DEPTH,) + base_shape, (None,) + base_blk
        out_idx = lambda i, j, kk: (layer,) + base_idx(i, j)
        if buf is not None:
            assert buf.shape == out_shape and buf.dtype == out_dtype
            in_specs.append(pl.BlockSpec(memory_space=pl.ANY))
            operands.append(buf)
            aliases = {2: 0}

            def body(a_ref, b_ref, buf_ref, *rest):
                del buf_ref
                compute(a_ref, b_ref, *rest)

    return pl.pallas_call(
        body, name=name,
        out_shape=jax.ShapeDtypeStruct(out_shape, out_dtype),
        grid=(m // tm, n // tn, nk),
        in_specs=in_specs,
        out_specs=pl.BlockSpec(out_blk, out_idx),
        scratch_shapes=[] if nk == 1 else [pltpu.VMEM((tm, tn), F32)],
        input_output_aliases=aliases,
        compiler_params=_cparams(("parallel", "parallel", "arbitrary")),
    )(*operands)


def _row_map(c0, moves):
    return (lambda j, i: (i, c0 + j)) if moves else (lambda j, i: (i, c0))


def _par_map(c0, moves):
    return (lambda j, i: (0, c0 + j)) if moves else (lambda j, i: (0, c0))


def _in_spec(op, tile):
    _, kind, w, c0, moves = op
    if kind == "row":
        return pl.BlockSpec((tile, w), _row_map(c0, moves))
    return pl.BlockSpec((1, w), _par_map(c0, moves))


ROW_BLOCK_ELEMS = 1 << 18


def _row_tile(t, tile, ops):
    width = max(op[2] for op in ops if op[1] == "row")
    return min(t, max(tile, ROW_BLOCK_ELEMS // width))


def rowwise_fwd(name, fn, t, tile, ncol, ins, outs):
    n_in = len(ins)
    tile = _row_tile(t, tile, ins)

    def body(*refs):
        vals = [r[...].astype(F32) for r in refs[:n_in]]
        res = fn(*vals)
        if not isinstance(res, (tuple, list)):
            res = (res,)
        for r, v in zip(refs[n_in:], res):
            r[...] = v.astype(r.dtype)

    res = pl.pallas_call(
        body, name=name,
        out_shape=[jax.ShapeDtypeStruct((t, w * ncol), dt) for w, dt in outs],
        grid=(ncol, t // tile),
        in_specs=[_in_spec(op, tile) for op in ins],
        out_specs=[pl.BlockSpec((tile, w), _row_map(0, True)) for w, _ in outs],
        compiler_params=_cparams(("arbitrary", "arbitrary")),
    )(*[op[0] for op in ins])
    return res


def rowwise_bwd(name, fn, t, tile, ncol, ins, need, cts, addends=None, row_dtypes=None):
    n_in, n_ct = len(ins), len(cts)
    tile = _row_tile(t, tile, ins)
    addends = addends or {}
    row_dtypes = row_dtypes or {}
    didx = [i for i in range(n_in) if need[i]]
    add_ops = [addends[i] for i in didx if i in addends]
    n_add = len(add_ops)

    def body(*refs):
        in_refs = refs[:n_in]
        ct_refs = refs[n_in:n_in + n_ct]
        add_refs = refs[n_in + n_ct:n_in + n_ct + n_add]
        out_refs = refs[n_in + n_ct + n_add:]
        vals = [r[...].astype(F32) for r in in_refs]

        def g(*dv):
            full = list(vals)
            for i, v in zip(didx, dv):
                full[i] = v
            res = fn(*full)
            return tuple(res) if isinstance(res, (tuple, list)) else (res,)

        _, vjp = jax.vjp(g, *[vals[i] for i in didx])
        grads = vjp(tuple(c[...].astype(F32) for c in ct_refs))
        j, i = pl.program_id(0), pl.program_id(1)
        a = 0
        for o_ref, gv, idx in zip(out_refs, grads, didx):
            _, kind, _, _, moves = ins[idx]
            if kind == "row":
                if idx in addends:
                    gv = gv + add_refs[a][...].astype(F32)
                    a += 1
                o_ref[...] = gv.astype(o_ref.dtype)
            else:
                first = (i == 0) if moves else jnp.logical_and(i == 0, j == 0)

                @pl.when(first)
                def _(o_ref=o_ref, gv=gv):
                    o_ref[...] = gv

                @pl.when(jnp.logical_not(first))
                def _(o_ref=o_ref, gv=gv):
                    o_ref[...] += gv

    out_shape, out_specs = [], []
    for idx in didx:
        _, kind, w, _, moves = ins[idx]
        cols = w * (ncol if moves else 1)
        if kind == "row":
            out_shape.append(jax.ShapeDtypeStruct((t, cols), row_dtypes.get(idx, F32)))
            out_specs.append(pl.BlockSpec((tile, w), _row_map(0, moves)))
        else:
            out_shape.append(jax.ShapeDtypeStruct((1, cols), F32))
            out_specs.append(pl.BlockSpec((1, w), _par_map(0, moves)))
    ops = list(ins) + list(cts) + add_ops
    res = pl.pallas_call(
        body, name=name,
        out_shape=out_shape,
        grid=(ncol, t // tile),
        in_specs=[_in_spec(op, tile) for op in ops],
        out_specs=out_specs,
        compiler_params=_cparams(("arbitrary", "arbitrary")),
    )(*[op[0] for op in ops])
    return res


def f_rmsnorm(x, w):
    return x * lax.rsqrt(jnp.mean(x * x, axis=-1, keepdims=True) + EPS) * w


def f_ssd_post(y, z, w):
    y = y * _silu(z)
    return y * lax.rsqrt(jnp.mean(y * y, axis=-1, keepdims=True) + EPS) * w


def f_gdn_post(o, z, w):
    o = o * lax.rsqrt(jnp.mean(o * o, axis=-1, keepdims=True) + EPS) * w
    return o * _silu(z)


def f_merge(gs, p1, gg, p2):
    return _sigmoid(gs) * p1 + _sigmoid(gg) * p2


def f_swiglu(g, u):
    return _silu(g) * u


def final_loss(name, x, tgt, w, tile=256):
    t, d = x.shape

    def body(x_ref, t_ref, w_ref, loss_ref, dx_ref, dw_ref):
        i = pl.program_id(0)
        xv, tv, wv = x_ref[...], t_ref[...], w_ref[...]

        def g(xx, ww):
            err = f_rmsnorm(xx, ww) - tv
            return 0.5 * jnp.sum(jnp.mean(err * err, axis=-1, keepdims=True), axis=0, keepdims=True)

        val, vjp = jax.vjp(g, xv, wv)
        dx, dw = vjp(jnp.ones((1, 1), F32))
        dx_ref[...] = dx
        lv = jnp.broadcast_to(val, (1, LANE))

        @pl.when(i == 0)
        def _():
            loss_ref[...] = lv
            dw_ref[...] = dw

        @pl.when(i != 0)
        def _():
            loss_ref[...] += lv
            dw_ref[...] += dw

    return pl.pallas_call(
        body, name=name,
        out_shape=[jax.ShapeDtypeStruct((1, LANE), F32), jax.ShapeDtypeStruct((t, d), F32),
                   jax.ShapeDtypeStruct((1, d), F32)],
        grid=(t // tile,),
        in_specs=[pl.BlockSpec((tile, d), lambda i: (i, 0)), pl.BlockSpec((tile, d), lambda i: (i, 0)),
                  pl.BlockSpec((1, d), lambda i: (0, 0))],
        out_specs=[pl.BlockSpec((1, LANE), lambda i: (0, 0)), pl.BlockSpec((tile, d), lambda i: (i, 0)),
                   pl.BlockSpec((1, d), lambda i: (0, 0))],
        compiler_params=_cparams(("arbitrary",)),
    )(x, tgt, w)


CONV_W = 512
HALO = 8
STRIPS = 8


def _rows_back(before, cur, d):
    rows = lax.broadcasted_iota(jnp.int32, cur.shape, 0)
    return jnp.where(rows < d, pltpu.roll(before, d, 0), pltpu.roll(cur, d, 0))


def _rows_ahead(cur, after, d):
    rows = lax.broadcasted_iota(jnp.int32, cur.shape, 0)
    return jnp.where(rows < HALO - d, pltpu.roll(cur, HALO - d, 0), pltpu.roll(after, HALO - d, 0))


def _conv_taps(taps, bias, before, cur):
    shifted = [_rows_back(before, cur, CONV_K - 1 - k) for k in range(CONV_K - 1)] + [cur]
    pre = bias + taps[CONV_K - 1] * cur
    for k in range(CONV_K - 1):
        pre = pre + taps[k] * shifted[k]
    return pre, shifted


def conv_fwd(name, src, c0, width, w, b, tile=512):
    t = src.shape[0]
    ncol, nrow = width // CONV_W, t // tile
    cb0 = c0 // CONV_W
    hb = tile // HALO

    def body(prev_ref, cur_ref, w_ref, b_ref, o_ref):
        i = pl.program_id(1)
        taps = [w_ref[k:k + 1, :] for k in range(CONV_K)]
        bias = b_ref[...]

        def strips(g, before):
            for u in range(STRIPS):
                r0 = pl.multiple_of((g * STRIPS + u) * HALO, HALO)
                cur = cur_ref[pl.ds(r0, HALO), :]
                pre, _ = _conv_taps(taps, bias, before, cur)
                o_ref[pl.ds(r0, HALO), :] = _silu(pre)
                before = cur
            return before

        lax.fori_loop(0, tile // (HALO * STRIPS), strips, jnp.where(i == 0, 0.0, prev_ref[...]))

    return pl.pallas_call(
        body, name=name,
        out_shape=jax.ShapeDtypeStruct((t, width), F32),
        grid=(ncol, nrow),
        in_specs=[pl.BlockSpec((HALO, CONV_W), lambda j, i: (jnp.maximum(i * hb - 1, 0), cb0 + j)),
                  pl.BlockSpec((tile, CONV_W), lambda j, i: (i, cb0 + j)),
                  pl.BlockSpec((CONV_K, CONV_W), lambda j, i: (0, j)),
                  pl.BlockSpec((1, CONV_W), lambda j, i: (0, j))],
        out_specs=pl.BlockSpec((tile, CONV_W), lambda j, i: (i, j)),
        compiler_params=_cparams(("arbitrary", "arbitrary")),
    )(src, src, w, b)


def conv_bwd(name, src, c0, width, w, b, dy, tile=512):
    t = src.shape[0]
    ncol, nrow = width // CONV_W, t // tile
    cb0 = c0 // CONV_W
    hb = tile // HALO
    last_hb = t // HALO - 1
    nstrip = tile // HALO

    def body(sprev_ref, scur_ref, snext_ref, w_ref, b_ref, dycur_ref, dynext_ref,
             du_ref, dw_ref, db_ref, dpre_ref):
        i = pl.program_id(1)
        taps = [w_ref[k:k + 1, :] for k in range(CONV_K)]
        bias = b_ref[...]

        def dpre_of(before, cur, dy_strip):
            pre, shifted = _conv_taps(taps, bias, before, cur)
            s = _sigmoid(pre)
            return dy_strip * (s * (1.0 + pre * (1.0 - s))), shifted

        def strips1(g, carry):
            before, dws, dbs = carry
            for u in range(STRIPS):
                r0 = pl.multiple_of((g * STRIPS + u) * HALO, HALO)
                cur = scur_ref[pl.ds(r0, HALO), :]
                dpre, shifted = dpre_of(before, cur, dycur_ref[pl.ds(r0, HALO), :])
                dpre_ref[pl.ds(r0, HALO), :] = dpre
                before, dws, dbs = cur, tuple(a + dpre * v for a, v in zip(dws, shifted)), dbs + dpre
            return before, dws, dbs

        zero = jnp.zeros((HALO, CONV_W), F32)
        before, dws, dbs = lax.fori_loop(0, nstrip // STRIPS, strips1,
                                         (jnp.where(i == 0, 0.0, sprev_ref[...]), (zero,) * CONV_K, zero))
        dpre_next, _ = dpre_of(before, snext_ref[...], dynext_ref[...])
        dpre_ref[pl.ds(tile, HALO), :] = jnp.where(i == nrow - 1, 0.0, dpre_next)

        def strips2(g, _):
            parts = []
            for u in range(STRIPS):
                r0 = pl.multiple_of((g * STRIPS + u) * HALO, HALO)
                cur = dpre_ref[pl.ds(r0, HALO), :]
                after = dpre_ref[pl.ds(r0 + HALO, HALO), :]
                acc = taps[CONV_K - 1] * cur
                for d in range(1, CONV_K):
                    acc = acc + taps[CONV_K - 1 - d] * _rows_ahead(cur, after, d)
                parts.append(acc)
            r0 = pl.multiple_of(g * STRIPS * HALO, STRIPS * HALO)
            du_ref[pl.ds(r0, STRIPS * HALO), :] = jnp.concatenate(parts, axis=0).astype(du_ref.dtype)
            return 0

        lax.fori_loop(0, nstrip // STRIPS, strips2, 0)
        dw_tile = jnp.concatenate([jnp.sum(a, axis=0, keepdims=True) for a in dws], axis=0)
        db_tile = jnp.sum(dbs, axis=0, keepdims=True)
        _acc(dw_ref, dw_tile, i == 0)
        _acc(db_ref, db_tile, i == 0)

    return pl.pallas_call(
        body, name=name,
        out_shape=[jax.ShapeDtypeStruct((t, width), BF16), jax.ShapeDtypeStruct((CONV_K, width), F32),
                   jax.ShapeDtypeStruct((1, width), F32)],
        grid=(ncol, nrow),
        in_specs=[pl.BlockSpec((HALO, CONV_W), lambda j, i: (jnp.maximum(i * hb - 1, 0), cb0 + j)),
                  pl.BlockSpec((tile, CONV_W), lambda j, i: (i, cb0 + j)),
                  pl.BlockSpec((HALO, CONV_W), lambda j, i: (jnp.minimum((i + 1) * hb, last_hb), cb0 + j)),
                  pl.BlockSpec((CONV_K, CONV_W), lambda j, i: (0, j)),
                  pl.BlockSpec((1, CONV_W), lambda j, i: (0, j)),
                  pl.BlockSpec((tile, CONV_W), lambda j, i: (i, j)),
                  pl.BlockSpec((HALO, CONV_W), lambda j, i: (jnp.minimum((i + 1) * hb, last_hb), j))],
        out_specs=[pl.BlockSpec((tile, CONV_W), lambda j, i: (i, j)),
                   pl.BlockSpec((CONV_K, CONV_W), lambda j, i: (0, j)),
                   pl.BlockSpec((1, CONV_W), lambda j, i: (0, j))],
        scratch_shapes=[pltpu.VMEM((tile + HALO, CONV_W), F32)],
        compiler_params=_cparams(("arbitrary", "arbitrary")),
    )(src, src, src, w, b, dy, dy)


def _iota2(q):
    return (lax.broadcasted_iota(jnp.int32, (q, q), 0), lax.broadcasted_iota(jnp.int32, (q, q), 1))


def _lane_pick(blk, idx):
    lane = lax.broadcasted_iota(jnp.int32, (1, LANE), 1)
    return jnp.sum(jnp.where(lane == idx, blk, 0.0), axis=1, keepdims=True)


class _Decay:
    def __init__(self, a):
        q = a.shape[0]
        r, c = _iota2(q)
        self.r, self.c = r, c
        self.cum = _dot_hi((c <= r).astype(F32), a)
        self.cum_t = _dot_hi(a, (r <= c).astype(F32), TN)
        self.tot = self.cum[q - 1:q, :]
        self.e_cum = jnp.exp(self.cum)
        self.e_rest = jnp.exp(self.tot - self.cum)
        self.e_tot = jnp.exp(self.tot)

    def mask(self, lane):
        rows = lax.broadcasted_iota(jnp.int32, (LANE, 1), 0)
        cum_row = jnp.sum(jnp.where(rows == lane, self.cum_t, 0.0), axis=0, keepdims=True)
        return jnp.exp(jnp.where(self.r >= self.c, _lane_pick(self.cum, lane) - cum_row, NEG_BIG))


_SSD_B = SSM_D_INNER
_SSD_C = SSM_D_INNER + 2 * SSM_STATE


def _interleave(gens):
    results = [None] * len(gens)
    live = list(range(len(gens)))
    while live:
        for i in list(live):
            try:
                next(gens[i])
            except StopIteration as stop:
                results[i] = stop.value
                live.remove(i)
    return results


def ssd_chunk(xs, bm, cm, dt_all, dsk, dec, state, p, cb):
    lane = lax.broadcasted_iota(jnp.int32, (1, LANE), 1)
    m0 = lane < SSM_HEAD_DIM
    h0, h1 = 2 * p, 2 * p + 1

    def both(blk):
        return jnp.where(m0, _lane_pick(blk, h0), _lane_pick(blk, h1))

    xdt = xs * both(dt_all)
    l0, l1 = dec.mask(h0), dec.mask(h1)
    yield
    y_diag = _dot(cb * l0, jnp.where(m0, xdt, 0.0)) + _dot(cb * l1, jnp.where(m0, 0.0, xdt))
    y_off = _dot(cm, state, NT) * both(dec.e_cum)
    yield
    rowm = lax.broadcasted_iota(jnp.int32, (LANE, 1), 0) < SSM_HEAD_DIM
    new_state = (state * jnp.where(rowm, _lane_pick(dec.e_tot, h0), _lane_pick(dec.e_tot, h1))
                 + _dot(xdt * both(dec.e_rest), bm, TN))
    y = y_diag + y_off + both(dsk) * xs
    return y, new_state


def ssd_pairs(xs, bms, cms, small, dtb, alog, dsk, states):
    dt_all = _softplus(small + dtb)
    dec = _Decay(dt_all * (-jnp.exp(alog)))
    cbs = [_dot(cm, bm, NT) for cm, bm in zip(cms, bms)]
    res = _interleave([ssd_chunk(x, bms[p // 4], cms[p // 4], dt_all, dsk, dec, st, p, cbs[p // 4])
                       for p, (x, st) in enumerate(zip(xs, states))])
    return tuple(y for y, _ in res), tuple(s for _, s in res)


def tri_inverse(a):
    q = a.shape[0]
    r, c = _iota2(q)
    eye = (r == c).astype(F32)
    diag = (r // 16) == (c // 16)
    bd = jnp.where(diag, a, 0.0)
    off = jnp.where(diag, 0.0, a)
    b2 = _dot3(bd, bd)
    d1 = _dot3(eye - bd, eye + b2)
    yield
    b4 = _dot3(b2, b2)
    yield
    b8 = _dot3(b4, b4)
    d2 = _dot3(d1, eye + b4)
    yield
    dinv = _dot3(d2, eye + b8)
    yield
    n = _dot3(dinv, off)
    yield
    powers = [n]
    while 16 * 2 ** len(powers) < q:
        powers.append(_dot3(powers[-1], powers[-1]))
        yield
    m = dinv
    for pw in reversed(powers[1:]):
        m = _dot3(eye + pw, m)
        yield
    return _dot3(eye - n, m)


@jax.custom_vjp
def _solve_with(xinv, a, rhs):
    del a
    return _dot3(xinv, rhs)


def _solve_with_fwd(xinv, a, rhs):
    t = _dot3(xinv, rhs)
    return t, (xinv, t)


def _solve_with_bwd(res, dt):
    xinv, t = res
    d_rhs = _dot3(xinv, dt, TN)
    d_a = -_dot(d_rhs, t, NT)
    return jnp.zeros_like(xinv), d_a, d_rhs


_solve_with.defvjp(_solve_with_fwd, _solve_with_bwd)


_GDN_K = GDN_HEADS * GDN_HEAD
_GDN_V = 2 * GDN_HEADS * GDN_HEAD


def gdn_chunk(qh, kh, vh, beta_all, dec, state, h, xinv=None):
    r, c = dec.r, dec.c
    qn = qh * lax.rsqrt(jnp.sum(qh * qh, axis=-1, keepdims=True) + EPS) * (GDN_HEAD ** -0.5)
    kn = kh * lax.rsqrt(jnp.sum(kh * kh, axis=-1, keepdims=True) + EPS)
    beta = _lane_pick(beta_all, SM_B + h)
    decay = dec.mask(SM_A + h)
    yield
    kk = _dot(kn, kn, NT)
    qk = _dot(qn, kn, NT) * decay
    amat = jnp.where(r > c, kk * decay * beta, 0.0)
    eg = _lane_pick(dec.e_cum, SM_A + h)
    rhs = jnp.concatenate([vh * beta, kn * (beta * eg)], axis=1)
    qs = _dot(qn * eg, state)
    yield
    if xinv is None:
        xinv = yield from tri_inverse(amat)
        t = _dot3(xinv, rhs)
    else:
        t = _solve_with(xinv, amat, rhs)
    yield
    u, w = t[:, :GDN_HEAD], t[:, GDN_HEAD:]
    v_new = u - _dot(w, state)
    yield
    o = qs + _dot(qk, v_new)
    new_state = (state * _lane_pick(dec.e_tot, SM_A + h)
                 + _dot(kn * _lane_pick(dec.e_rest, SM_A + h), v_new, TN))
    return o, new_state, xinv


def gdn_heads(qs, ks, vs, small, alog, dtb, states, xinvs=None):
    nh = len(qs)
    beta_all = _sigmoid(small)
    dec = _Decay(-jnp.exp(alog) * _softplus(small + dtb))
    res = _interleave([gdn_chunk(qs[h], ks[h], vs[h], beta_all, dec, states[h], h,
                                 None if xinvs is None else xinvs[h]) for h in range(nh)])
    return tuple(o for o, _, _ in res), tuple(s for _, s, _ in res), tuple(x for _, _, x in res)


def _acc(ref, val, first):
    @pl.when(first)
    def _():
        ref[...] = val

    @pl.when(jnp.logical_not(first))
    def _():
        ref[...] += val


def ssd_scan_fwd(name, xbc, proj, dtb, alog, dsk):
    t = xbc.shape[0]
    nc, npair = t // SSD_CHUNK, SSM_HEADS // 2
    small_blk = AL_SMALL // LANE

    def body(xbc_ref, sm_ref, dtb_ref, alog_ref, dsk_ref, y_ref, sin_ref, st_ref):
        ci = pl.program_id(0)

        @pl.when(ci == 0)
        def _():
            st_ref[...] = jnp.zeros_like(st_ref)

        s_in = tuple(st_ref[p] for p in range(npair))
        ys, s_new = ssd_pairs(tuple(xbc_ref[:, p * LANE:(p + 1) * LANE] for p in range(npair)),
                              tuple(xbc_ref[:, _SSD_B + g * LANE:_SSD_B + (g + 1) * LANE] for g in range(2)),
                              tuple(xbc_ref[:, _SSD_C + g * LANE:_SSD_C + (g + 1) * LANE] for g in range(2)),
                              sm_ref[...], dtb_ref[...], alog_ref[...], dsk_ref[...], s_in)
        for p in range(npair):
            sin_ref[0, p] = s_in[p]
            y_ref[:, p * LANE:(p + 1) * LANE] = ys[p]
            st_ref[p] = s_new[p]

    par = pl.BlockSpec((1, LANE), lambda ci: (0, 0))
    return pl.pallas_call(
        body, name=name,
        out_shape=[jax.ShapeDtypeStruct((t, SSM_D_INNER), F32),
                   jax.ShapeDtypeStruct((nc, npair, LANE, LANE), F32)],
        grid=(nc,),
        in_specs=[pl.BlockSpec((SSD_CHUNK, SSM_CONV_DIM), lambda ci: (ci, 0)),
                  pl.BlockSpec((SSD_CHUNK, LANE), lambda ci: (ci, small_blk)),
                  par, par, par],
        out_specs=[pl.BlockSpec((SSD_CHUNK, SSM_D_INNER), lambda ci: (ci, 0)),
                   pl.BlockSpec((1, npair, LANE, LANE), lambda ci: (ci, 0, 0, 0))],
        scratch_shapes=[pltpu.VMEM((npair, LANE, LANE), F32)],
        compiler_params=_cparams(("arbitrary",)),
    )(xbc, proj, dtb, alog, dsk)


def ssd_scan_bwd(name, xbc, proj, dtb, alog, dsk, s_in, dy):
    t = xbc.shape[0]
    nc, npair = t // SSD_CHUNK, SSM_HEADS // 2
    small_blk = AL_SMALL // LANE

    def body(xbc_ref, sm_ref, dtb_ref, alog_ref, dsk_ref, sin_ref, dy_ref,
             dxbc_ref, dsm_ref, ddtb_ref, dalog_ref, ddsk_ref, dst_ref):
        ci = pl.program_id(0)

        @pl.when(ci == 0)
        def _():
            dst_ref[...] = jnp.zeros_like(dst_ref)

        _, vjp = jax.vjp(ssd_pairs, tuple(xbc_ref[:, p * LANE:(p + 1) * LANE] for p in range(npair)),
                         tuple(xbc_ref[:, _SSD_B + g * LANE:_SSD_B + (g + 1) * LANE] for g in range(2)),
                         tuple(xbc_ref[:, _SSD_C + g * LANE:_SSD_C + (g + 1) * LANE] for g in range(2)),
                         sm_ref[...], dtb_ref[...], alog_ref[...], dsk_ref[...],
                         tuple(sin_ref[0, p] for p in range(npair)))
        dxs, dbms, dcms, dsm, ddtb, dalog, ddsk, dsts = vjp(
            (tuple(dy_ref[:, p * LANE:(p + 1) * LANE] for p in range(npair)),
             tuple(dst_ref[p] for p in range(npair))))
        for p in range(npair):
            dxbc_ref[:, p * LANE:(p + 1) * LANE] = dxs[p]
            dst_ref[p] = dsts[p]
        for g in range(2):
            dxbc_ref[:, _SSD_B + g * LANE:_SSD_B + (g + 1) * LANE] = dbms[g]
            dxbc_ref[:, _SSD_C + g * LANE:_SSD_C + (g + 1) * LANE] = dcms[g]
        dsm_ref[...] = dsm
        _acc(ddtb_ref, ddtb, ci == 0)
        _acc(dalog_ref, dalog, ci == 0)
        _acc(ddsk_ref, ddsk, ci == 0)

    par = pl.BlockSpec((1, LANE), lambda ci: (0, 0))
    rev = lambda ci: nc - 1 - ci
    return pl.pallas_call(
        body, name=name,
        out_shape=[jax.ShapeDtypeStruct((t, SSM_CONV_DIM), F32),
                   jax.ShapeDtypeStruct((t, LANE), F32),
                   jax.ShapeDtypeStruct((1, LANE), F32), jax.ShapeDtypeStruct((1, LANE), F32),
                   jax.ShapeDtypeStruct((1, LANE), F32)],
        grid=(nc,),
        in_specs=[pl.BlockSpec((SSD_CHUNK, SSM_CONV_DIM), lambda ci: (rev(ci), 0)),
                  pl.BlockSpec((SSD_CHUNK, LANE), lambda ci: (rev(ci), small_blk)),
                  par, par, par,
                  pl.BlockSpec((1, npair, LANE, LANE), lambda ci: (rev(ci), 0, 0, 0)),
                  pl.BlockSpec((SSD_CHUNK, SSM_D_INNER), lambda ci: (rev(ci), 0))],
        out_specs=[pl.BlockSpec((SSD_CHUNK, SSM_CONV_DIM), lambda ci: (rev(ci), 0)),
                   pl.BlockSpec((SSD_CHUNK, LANE), lambda ci: (rev(ci), 0)),
                   par, par, par],
        scratch_shapes=[pltpu.VMEM((npair, LANE, LANE), F32)],
        compiler_params=_cparams(("arbitrary",)),
    )(xbc, proj, dtb, alog, dsk, s_in, dy)


def gdn_scan_fwd(name, qkv, proj, alog, dtb):
    t = qkv.shape[0]
    nc, nh = t // CHUNK, GDN_HEADS
    small_blk = AL_SMALL // LANE

    def body(qkv_ref, sm_ref, alog_ref, dtb_ref, o_ref, sin_ref, x_ref, st_ref):
        ci = pl.program_id(0)

        @pl.when(ci == 0)
        def _():
            st_ref[...] = jnp.zeros_like(st_ref)

        s_in = tuple(st_ref[h] for h in range(nh))
        os, s_new, xinvs = gdn_heads(
            tuple(qkv_ref[:, h * LANE:(h + 1) * LANE] for h in range(nh)),
            tuple(qkv_ref[:, _GDN_K + h * LANE:_GDN_K + (h + 1) * LANE] for h in range(nh)),
            tuple(qkv_ref[:, _GDN_V + h * LANE:_GDN_V + (h + 1) * LANE] for h in range(nh)),
            sm_ref[...], alog_ref[...], dtb_ref[...], s_in)
        for h in range(nh):
            sin_ref[0, h] = s_in[h]
            o_ref[:, h * LANE:(h + 1) * LANE] = os[h]
            x_ref[0, h] = xinvs[h]
            st_ref[h] = s_new[h]

    par = pl.BlockSpec((1, LANE), lambda ci: (0, 0))
    return pl.pallas_call(
        body, name=name,
        out_shape=[jax.ShapeDtypeStruct((t, GDN_HEADS * GDN_HEAD), F32),
                   jax.ShapeDtypeStruct((nc, nh, LANE, LANE), F32),
                   jax.ShapeDtypeStruct((nc, nh, CHUNK, CHUNK), F32)],
        grid=(nc,),
        in_specs=[pl.BlockSpec((CHUNK, GDN_QKV_DIM), lambda ci: (ci, 0)),
                  pl.BlockSpec((CHUNK, LANE), lambda ci: (ci, small_blk)),
                  par, par],
        out_specs=[pl.BlockSpec((CHUNK, GDN_HEADS * GDN_HEAD), lambda ci: (ci, 0)),
                   pl.BlockSpec((1, nh, LANE, LANE), lambda ci: (ci, 0, 0, 0)),
                   pl.BlockSpec((1, nh, CHUNK, CHUNK), lambda ci: (ci, 0, 0, 0))],
        scratch_shapes=[pltpu.VMEM((nh, LANE, LANE), F32)],
        compiler_params=_cparams(("arbitrary",)),
    )(qkv, proj, alog, dtb)


def gdn_scan_bwd(name, qkv, proj, alog, dtb, s_in, xinv, do, dsm_in):
    t = qkv.shape[0]
    nc, nh = t // CHUNK, GDN_HEADS
    small_blk = AL_SMALL // LANE

    def body(qkv_ref, sm_ref, alog_ref, dtb_ref, sin_ref, x_ref, do_ref, dsmi_ref,
             dqkv_ref, dsm_ref, dalog_ref, ddtb_ref, dst_ref):
        ci = pl.program_id(0)

        @pl.when(ci == 0)
        def _():
            dst_ref[...] = jnp.zeros_like(dst_ref)

        xis = tuple(x_ref[0, h] for h in range(nh))

        def fn(qs, ks, vs, sm, alog_, dtb_, sts):
            os, s_new, _ = gdn_heads(qs, ks, vs, sm, alog_, dtb_, sts, xinvs=xis)
            return os, s_new

        _, vjp = jax.vjp(fn, tuple(qkv_ref[:, h * LANE:(h + 1) * LANE] for h in range(nh)),
                         tuple(qkv_ref[:, _GDN_K + h * LANE:_GDN_K + (h + 1) * LANE] for h in range(nh)),
                         tuple(qkv_ref[:, _GDN_V + h * LANE:_GDN_V + (h + 1) * LANE] for h in range(nh)),
                         sm_ref[...], alog_ref[...], dtb_ref[...], tuple(sin_ref[0, h] for h in range(nh)))
        dqs, dks, dvs, dsm, dalog, ddtb, dsts = vjp(
            (tuple(do_ref[:, h * LANE:(h + 1) * LANE] for h in range(nh)), tuple(dst_ref[h] for h in range(nh))))
        for h in range(nh):
            dqkv_ref[:, h * LANE:(h + 1) * LANE] = dqs[h]
            dqkv_ref[:, _GDN_K + h * LANE:_GDN_K + (h + 1) * LANE] = dks[h]
            dqkv_ref[:, _GDN_V + h * LANE:_GDN_V + (h + 1) * LANE] = dvs[h]
            dst_ref[h] = dsts[h]
        dsm_ref[...] = dsmi_ref[...] + dsm
        _acc(dalog_ref, dalog, ci == 0)
        _acc(ddtb_ref, ddtb, ci == 0)

    par = pl.BlockSpec((1, LANE), lambda ci: (0, 0))
    rev = lambda ci: nc - 1 - ci
    return pl.pallas_call(
        body, name=name,
        out_shape=[jax.ShapeDtypeStruct((t, GDN_QKV_DIM), F32), jax.ShapeDtypeStruct((t, LANE), F32),
                   jax.ShapeDtypeStruct((1, LANE), F32), jax.ShapeDtypeStruct((1, LANE), F32)],
        grid=(nc,),
        in_specs=[pl.BlockSpec((CHUNK, GDN_QKV_DIM), lambda ci: (rev(ci), 0)),
                  pl.BlockSpec((CHUNK, LANE), lambda ci: (rev(ci), small_blk)),
                  par, par,
                  pl.BlockSpec((1, nh, LANE, LANE), lambda ci: (rev(ci), 0, 0, 0)),
                  pl.BlockSpec((1, nh, CHUNK, CHUNK), lambda ci: (rev(ci), 0, 0, 0)),
                  pl.BlockSpec((CHUNK, GDN_HEADS * GDN_HEAD), lambda ci: (rev(ci), 0)),
                  pl.BlockSpec((CHUNK, LANE), lambda ci: (rev(ci), 0))],
        out_specs=[pl.BlockSpec((CHUNK, GDN_QKV_DIM), lambda ci: (rev(ci), 0)),
                   pl.BlockSpec((CHUNK, LANE), lambda ci: (rev(ci), 0)),
                   par, par],
        scratch_shapes=[pltpu.VMEM((nh, LANE, LANE), F32)],
        compiler_params=_cparams(("arbitrary",)),
    )(qkv, proj, alog, dtb, s_in, xinv, do, dsm_in)


def _row(arr, w, c0=0, moves=False):
    return (arr, "row", w, c0, moves)


def _par(arr, w, c0=0, moves=False):
    return (arr, "par", w, c0, moves)


def matmul_add(name, a, b, res):
    (m, k), (_, n) = a.shape, b.shape
    tm, tn, tk = _matmul_tiles(m, n, n, k, a.dtype.itemsize, b.dtype.itemsize, 4 + res.dtype.itemsize)
    nk = k // tk

    def body_acc(a_ref, b_ref, r_ref, o_ref, acc_ref):
        kk = pl.program_id(2)

        @pl.when(kk == 0)
        def _():
            acc_ref[...] = r_ref[...]

        acc_ref[...] += _dot(a_ref[...], b_ref[...])

        @pl.when(kk == nk - 1)
        def _():
            o_ref[...] = acc_ref[...]

    def body_one(a_ref, b_ref, r_ref, o_ref):
        o_ref[...] = r_ref[...] + _dot(a_ref[...], b_ref[...])

    body = body_one if nk == 1 else body_acc
    return pl.pallas_call(
        body, name=name,
        out_shape=jax.ShapeDtypeStruct((m, n), F32),
        grid=(m // tm, n // tn, nk),
        in_specs=[pl.BlockSpec((tm, tk), lambda i, j, kk: (i, kk)),
                  pl.BlockSpec((tk, tn), lambda i, j, kk: (kk, j)),
                  pl.BlockSpec((tm, tn), lambda i, j, kk: (i, j))],
        out_specs=pl.BlockSpec((tm, tn), lambda i, j, kk: (i, j)),
        scratch_shapes=[] if nk == 1 else [pltpu.VMEM((tm, tn), F32)],
        compiler_params=_cparams(("parallel", "parallel", "arbitrary")),
    )(a, b, res)


def layer_fwd(l, x, w):
    t = x.shape[0]
    rt = min(256, t)
    s = {"x": x}
    s["h"] = rowwise_fwd(f"norm_mix_l{l}", f_rmsnorm, t, rt, 1,
                         [_row(x, D_MODEL), _par(w["norm_mix_w"], D_MODEL)], [(D_MODEL, BF16)])[0]
    s["proj"] = matmul(f"in_proj_l{l}", s["h"], w["w_in"], "nn")
    s["xbc"] = conv_fwd(f"ssm_conv_l{l}", s["proj"], AL_XBC, SSM_CONV_DIM, w["ssm_conv_w"], w["ssm_conv_b"],
                        tile=min(512, t))
    s["qkv"] = conv_fwd(f"gdn_conv_l{l}", s["proj"], AL_QKV, GDN_QKV_DIM, w["gdn_conv_w"], w["gdn_conv_b"],
                        tile=min(512, t))
    s["y_scan"], s["ssd_sin"] = ssd_scan_fwd(f"ssd_scan_l{l}", s["xbc"], s["proj"], w["ssm_dt_bias"],
                                             w["ssm_a_log"], w["ssm_d"])
    s["o_scan"], s["gdn_sin"], s["gdn_x"] = gdn_scan_fwd(f"gdn_scan_l{l}", s["qkv"], s["proj"],
                                                         w["gdn_a_log"], w["gdn_dt_bias"])
    s["y_ssm"] = rowwise_fwd(f"ssd_post_l{l}", f_ssd_post, t, rt, 2,
                             [_row(s["y_scan"], 512, 0, True), _row(s["proj"], 512, AL_Z // 512, True),
                              _par(w["ssm_norm_w"], 512, 0, True)], [(512, BF16)])[0]
    s["y_gdn"] = rowwise_fwd(f"gdn_post_l{l}", f_gdn_post, t, rt, GDN_HEADS,
                             [_row(s["o_scan"], LANE, 0, True), _row(s["proj"], LANE, AL_GZ // LANE, True),
                              _par(w["gdn_norm_w"], LANE)], [(LANE, BF16)])[0]
    if "late_weights" in w:
        late = w["late_weights"](s["y_gdn"])
        w = {k: v for k, v in {**w, **late}.items() if k != "late_weights"}
    s["w"] = w
    s["p1"] = matmul(f"proj_ssm_l{l}", s["y_ssm"], w["w_proj_ssm"], "nn")
    s["p2"] = matmul(f"proj_gdn_l{l}", s["y_gdn"], w["w_proj_gdn"], "nn")
    s["merged"] = rowwise_fwd(f"merge_l{l}", f_merge, t, rt, 2,
                              [_row(s["proj"], 512, AL_GS // 512, True), _row(s["p1"], 512, 0, True),
                               _row(s["proj"], 512, AL_GG // 512, True), _row(s["p2"], 512, 0, True)],
                              [(512, BF16)])[0]
    s["x1"] = matmul_add(f"out_proj_l{l}", s["merged"], w["w_out"], x)
    s["h2"] = rowwise_fwd(f"norm_ffn_l{l}", f_rmsnorm, t, rt, 1,
                          [_row(s["x1"], D_MODEL), _par(w["norm_ffn_w"], D_MODEL)], [(D_MODEL, BF16)])[0]
    s["gu"] = matmul(f"ffn_in_l{l}", s["h2"], w["w_ffn_in"], "nn")
    s["act"] = rowwise_fwd(f"swiglu_l{l}", f_swiglu, t, rt, FFN_HIDDEN // 256,
                           [_row(s["gu"], 256, 0, True), _row(s["gu"], 256, FFN_HIDDEN // 256, True)],
                           [(256, BF16)])[0]
    x2 = matmul_add(f"ffn_down_l{l}", s["act"], w["w_ffn_down"], s["x1"])
    return x2, s


IN_SHARD = IN_DIM // 4
IN_SHARD_PAD = 2304


def _aligned_to_shards(g):
    orig = jnp.concatenate([g[:, 0:2560], g[:, AL_SMALL:AL_SMALL + 16], g[:, 2560:6656],
                            g[:, AL_SMALL + 16:AL_SMALL + 32], g[:, 6656:8704]], axis=1)
    return jnp.stack([jnp.pad(orig[:, j * IN_SHARD:(j + 1) * IN_SHARD], ((0, 0), (0, IN_SHARD_PAD - IN_SHARD)))
                      for j in range(N_CHIPS)])


def layer_bwd(l, dx2, w, s):
    t = dx2.shape[0]
    rt = min(256, t)
    ct = min(512, t)
    g = {}
    dact = matmul(f"ffn_down_dx_l{l}", dx2, w["w_ffn_down"], "nt")
    g["w_ffn_down"] = matmul(f"ffn_down_dw_l{l}", s["act"], dx2, "tn")
    nf = FFN_HIDDEN // 256
    dgate, dup = rowwise_bwd(f"swiglu_bwd_l{l}", f_swiglu, t, rt, nf,
                             [_row(s["gu"], 256, 0, True), _row(s["gu"], 256, nf, True)], [True, True],
                             [_row(dact, 256, 0, True)], row_dtypes={0: BF16, 1: BF16})
    dgu = jnp.concatenate([dgate, dup], axis=1)
    dh2 = matmul(f"ffn_in_dx_l{l}", dgu, w["w_ffn_in"], "nt")
    g["w_ffn_in"] = matmul(f"ffn_in_dw_l{l}", s["h2"], dgu, "tn", chip_major=True)
    dx1, g["norm_ffn_w"] = rowwise_bwd(f"norm_ffn_bwd_l{l}", f_rmsnorm, t, rt, 1,
                                       [_row(s["x1"], D_MODEL), _par(w["norm_ffn_w"], D_MODEL)], [True, True],
                                       [_row(dh2, D_MODEL)], addends={0: _row(dx2, D_MODEL)})
    dmerged = matmul(f"out_proj_dx_l{l}", dx1, w["w_out"], "nt")
    g["w_out"] = matmul(f"out_proj_dw_l{l}", s["merged"], dx1, "tn")
    dgs, dp1, dgg, dp2 = rowwise_bwd(
        f"merge_bwd_l{l}", f_merge, t, rt, 2,
        [_row(s["proj"], 512, AL_GS // 512, True), _row(s["p1"], 512, 0, True),
         _row(s["proj"], 512, AL_GG // 512, True), _row(s["p2"], 512, 0, True)], [True] * 4,
        [_row(dmerged, 512, 0, True)], row_dtypes={0: BF16, 1: BF16, 2: BF16, 3: BF16})
    dy_ssm = matmul(f"proj_ssm_dx_l{l}", dp1, w["w_proj_ssm"], "nt")
    g["w_proj_ssm"] = matmul(f"proj_ssm_dw_l{l}", s["y_ssm"], dp1, "tn")
    dy_gdn = matmul(f"proj_gdn_dx_l{l}", dp2, w["w_proj_gdn"], "nt")
    g["w_proj_gdn"] = matmul(f"proj_gdn_dw_l{l}", s["y_gdn"], dp2, "tn")
    dy_scan, dz, g["ssm_norm_w"] = rowwise_bwd(
        f"ssd_post_bwd_l{l}", f_ssd_post, t, rt, 2,
        [_row(s["y_scan"], 512, 0, True), _row(s["proj"], 512, AL_Z // 512, True),
         _par(w["ssm_norm_w"], 512, 0, True)], [True] * 3, [_row(dy_ssm, 512, 0, True)], row_dtypes={1: BF16})
    dxbc_act, dsm, g["ssm_dt_bias"], g["ssm_a_log"], g["ssm_d"] = ssd_scan_bwd(
        f"ssd_scan_bwd_l{l}", s["xbc"], s["proj"], w["ssm_dt_bias"], w["ssm_a_log"], w["ssm_d"],
        s["ssd_sin"], dy_scan)
    dxbc, g["ssm_conv_w"], g["ssm_conv_b"] = conv_bwd(
        f"ssm_conv_bwd_l{l}", s["proj"], AL_XBC, SSM_CONV_DIM, w["ssm_conv_w"], w["ssm_conv_b"], dxbc_act, tile=ct)
    do_scan, dgz, g["gdn_norm_w"] = rowwise_bwd(
        f"gdn_post_bwd_l{l}", f_gdn_post, t, rt, GDN_HEADS,
        [_row(s["o_scan"], LANE, 0, True), _row(s["proj"], LANE, AL_GZ // LANE, True),
         _par(w["gdn_norm_w"], LANE)], [True] * 3, [_row(dy_gdn, LANE, 0, True)], row_dtypes={1: BF16})
    dqkv_act, dsm, g["gdn_a_log"], g["gdn_dt_bias"] = gdn_scan_bwd(
        f"gdn_scan_bwd_l{l}", s["qkv"], s["proj"], w["gdn_a_log"], w["gdn_dt_bias"], s["gdn_sin"],
        s["gdn_x"], do_scan, dsm)
    dqkv, g["gdn_conv_w"], _ = conv_bwd(
        f"gdn_conv_bwd_l{l}", s["proj"], AL_QKV, GDN_QKV_DIM, w["gdn_conv_w"], w["gdn_conv_b"], dqkv_act, tile=ct)
    dproj = jnp.concatenate([dz, dxbc, dqkv, dgz, dgs, dgg, dsm.astype(BF16),
                             jnp.zeros((t, AL_DIM - AL_SMALL - LANE), BF16)], axis=1)
    dh = matmul(f"in_proj_dx_l{l}", dproj, w["w_in"], "nt")
    g["w_in"] = matmul(f"in_proj_dw_l{l}", s["h"], dproj, "tn")
    dx0, g["norm_mix_w"] = rowwise_bwd(f"norm_mix_bwd_l{l}", f_rmsnorm, t, rt, 1,
                                       [_row(s["x"], D_MODEL), _par(w["norm_mix_w"], D_MODEL)], [True, True],
                                       [_row(dh, D_MODEL)], addends={0: _row(dx1, D_MODEL)})
    return dx0, g


def _align_w_in(w):
    pad = jnp.zeros((w.shape[0], AL_DIM - AL_SMALL - 32), w.dtype)
    return jnp.concatenate([w[:, 0:2560], w[:, 2576:6672], w[:, 6688:8736],
                            w[:, 2560:2576], w[:, 6672:6688], pad], axis=1)


def _pad_lane(v, at=0):
    return jnp.pad(v[None], ((0, 0), (at, LANE - at - v.shape[0])))


def local_step(x, target, full, gathered_of=None, on_layer_grads=None):
    if gathered_of is None:
        gathered_of = lambda l, after: {n: full[n][l] for n, _ in SHARDED}
    ws, saved = [], []
    h = x
    for l in range(DEPTH):
        gw = gathered_of(l, h)
        ws.append({
            "norm_mix_w": full["norm_mix_w"][l][None], "w_in": _align_w_in(gw["w_in"]),
            "ssm_conv_w": gw["ssm_conv_w"], "ssm_conv_b": full["ssm_conv_b"][l][None],
            "ssm_dt_bias": _pad_lane(full["ssm_dt_bias"][l]), "ssm_a_log": _pad_lane(full["ssm_a_log"][l]),
            "ssm_d": _pad_lane(full["ssm_d"][l]), "ssm_norm_w": full["ssm_norm_w"][l][None],
            "gdn_conv_w": gw["gdn_conv_w"], "gdn_conv_b": jnp.zeros((1, GDN_QKV_DIM), F32),
            "gdn_a_log": _pad_lane(full["gdn_a_log"][l], SM_A),
            "gdn_dt_bias": _pad_lane(full["gdn_dt_bias"][l], SM_A),
            "gdn_norm_w": full["gdn_norm_w"][l][None],
            "norm_ffn_w": full["norm_ffn_w"][l][None],
            **{n: gw[n] for n in ("w_proj_ssm", "w_proj_gdn", "w_out", "w_ffn_in", "w_ffn_down", "late_weights")
               if n in gw},
        })
        h, s = layer_fwd(l, h, ws[l])
        ws[l] = s.pop("w")
        saved.append(s)
    loss, dx, g_final = final_loss("final_loss", h, target, full["final_norm_w"][None], tile=min(256, x.shape[0]))
    per_layer = [None] * DEPTH
    matmul_grads = [None] * DEPTH
    for l in reversed(range(DEPTH)):
        dx, per_layer[l] = layer_bwd(l, dx, ws[l], saved[l])
        matmul_grads[l] = {n: per_layer[l].pop(n) for n, _ in BIG}
        if on_layer_grads is not None:
            dx = on_layer_grads(l, matmul_grads[l], dx)
    grads = {"final_norm_w": g_final[0]}
    if on_layer_grads is None:
        grads.update({n: jnp.stack([matmul_grads[l][n] for l in range(DEPTH)]) for n, _ in BIG})
    for name in per_layer[0]:
        rows = []
        for l in range(DEPTH):
            gl = per_layer[l][name]
            if name in ("ssm_dt_bias", "ssm_a_log", "ssm_d"):
                gl = gl[0, :SSM_HEADS]
            elif name in ("gdn_a_log", "gdn_dt_bias"):
                gl = gl[0, SM_A:SM_A + GDN_HEADS]
            elif name in ("norm_mix_w", "ssm_conv_b", "ssm_norm_w", "gdn_norm_w", "norm_ffn_w"):
                gl = gl[0]
            rows.append(gl)
        grads[name] = jnp.stack(rows)
    return loss, dx, grads


MESH = pl.DeviceIdType.MESH
HBM = pl.BlockSpec(memory_space=pltpu.HBM)
N_DEV = 8


def _pos():
    return lax.axis_index("x"), lax.axis_index("y"), lax.axis_index("c")


def _rcopy(src, dst, send_sem, recv_sem, dev):
    return pltpu.make_async_remote_copy(src_ref=src, dst_ref=dst, send_sem=send_sem, recv_sem=recv_sem,
                                        device_id=dev, device_id_type=MESH)


RELATIONS = (2, 1, 3)


def _related_chip(x, y, mask):
    return (1 - x if mask & 2 else x, 1 - y if mask & 1 else y)


def weights_gather(name, bufs):
    n = len(bufs)

    def body(*refs):
        outs, send_sems, recv_sems = refs[n:2 * n], refs[2 * n], refs[2 * n + 1]
        x, y, c = _pos()
        sib = (x, y, 1 - c)
        sends = []
        for i, a in enumerate(outs):
            for k, m in enumerate(RELATIONS):
                px, py = _related_chip(x, y, m)
                cp = _rcopy(a.at[0, c], a.at[m, c], send_sems.at[6 * i + k], recv_sems.at[6 * i + k], (px, py, c))
                cp.start()
                sends.append(cp)
        for i, a in enumerate(outs):
            for k, m in enumerate(RELATIONS):
                px, py = _related_chip(x, y, m)
                _rcopy(a.at[0, c], a.at[m, c], send_sems.at[6 * i + k], recv_sems.at[6 * i + k],
                       (px, py, c)).wait_recv()
                fw = _rcopy(a.at[m, c], a.at[m, c], send_sems.at[6 * i + 3 + k], recv_sems.at[6 * i + 3 + k], sib)
                fw.start()
                sends.append(fw)
        for i, a in enumerate(outs):
            for k, m in enumerate(RELATIONS):
                _rcopy(a.at[m, 1 - c], a.at[m, 1 - c], send_sems.at[6 * i + 3 + k], recv_sems.at[6 * i + 3 + k],
                       sib).wait_recv()
        for cp in sends:
            cp.wait_send()

    return pl.pallas_call(
        body, name=name, out_shape=[jax.ShapeDtypeStruct(b.shape, b.dtype) for b in bufs],
        in_specs=[HBM] * n, out_specs=[HBM] * n,
        input_output_aliases={i: i for i in range(n)},
        scratch_shapes=[pltpu.SemaphoreType.DMA((6 * n,)), pltpu.SemaphoreType.DMA((6 * n,))],
    )(*bufs)


SEM = pl.BlockSpec(memory_space=pltpu.SEMAPHORE)
DATAFLOW = pltpu.SideEffectType.DATAFLOW_SIDE_EFFECTING


def gather_start(name, bufs, after):
    n = len(bufs)

    def body(*refs):
        ins = refs[:n]
        send_sems, recv_sems, token = refs[n + 1], refs[n + 2], refs[2 * n + 3]
        x, y, c = _pos()
        for i, a in enumerate(ins):
            for k, m in enumerate(RELATIONS):
                px, py = _related_chip(x, y, m)
                _rcopy(a.at[0, c], a.at[m, c], send_sems.at[3 * i + k], recv_sems.at[3 * i + k], (px, py, c)).start()
        token[...] = jnp.zeros_like(token)

    res = pl.pallas_call(
        body, name=name,
        out_shape=(pltpu.SemaphoreType.DMA((3 * n,)), pltpu.SemaphoreType.DMA((3 * n,)),
                   *[pltpu.HBM(b.shape, b.dtype) for b in bufs], jax.ShapeDtypeStruct((8, LANE), F32)),
        in_specs=[HBM] * n + [pl.BlockSpec(memory_space=pl.ANY)],
        out_specs=(SEM, SEM, *[HBM] * n, pl.BlockSpec(memory_space=pltpu.VMEM)),
        input_output_aliases={i: 2 + i for i in range(n)},
        compiler_params=pltpu.CompilerParams(has_side_effects=DATAFLOW),
    )(*[pltpu.with_memory_space_constraint(b, pltpu.HBM) for b in bufs], after)
    return res[0], res[1], list(res[2:2 + n]), res[2 + n]


def gather_wait(name, send_sems, recv_sems, bufs, after):
    n = len(bufs)

    def body(*refs):
        ins, ssem, rsem = refs[:n], refs[n], refs[n + 1]
        x, y, c = _pos()
        for i, a in enumerate(ins):
            for k, m in enumerate(RELATIONS):
                px, py = _related_chip(x, y, m)
                cp = _rcopy(a.at[0, c], a.at[m, c], ssem.at[3 * i + k], rsem.at[3 * i + k], (px, py, c))
                cp.wait_send()
                cp.wait_recv()

    return pl.pallas_call(
        body, name=name,
        out_shape=[pltpu.HBM(b.shape, b.dtype) for b in bufs],
        in_specs=[HBM] * n + [SEM, SEM, pl.BlockSpec(memory_space=pl.ANY)],
        out_specs=[HBM] * n,
        input_output_aliases={i: i for i in range(n)},
        compiler_params=pltpu.CompilerParams(has_side_effects=DATAFLOW),
    )(*bufs, send_sems, recv_sems, after)


def weights_forward(name, bufs):
    n = len(bufs)

    def body(*refs):
        outs, send_sems, recv_sems = refs[n:2 * n], refs[2 * n], refs[2 * n + 1]
        x, y, c = _pos()
        sib = (x, y, 1 - c)
        sends = []
        for i, a in enumerate(outs):
            for k, m in enumerate(RELATIONS):
                fw = _rcopy(a.at[m, c], a.at[m, c], send_sems.at[3 * i + k], recv_sems.at[3 * i + k], sib)
                fw.start()
                sends.append(fw)
        for i, a in enumerate(outs):
            for k, m in enumerate(RELATIONS):
                _rcopy(a.at[m, 1 - c], a.at[m, 1 - c], send_sems.at[3 * i + k], recv_sems.at[3 * i + k],
                       sib).wait_recv()
        for cp in sends:
            cp.wait_send()

    return pl.pallas_call(
        body, name=name, out_shape=[jax.ShapeDtypeStruct(b.shape, b.dtype) for b in bufs],
        in_specs=[HBM] * n, out_specs=[HBM] * n,
        input_output_aliases={i: i for i in range(n)},
        scratch_shapes=[pltpu.SemaphoreType.DMA((3 * n,)), pltpu.SemaphoreType.DMA((3 * n,))],
    )(*bufs)


def pair_swap(name, gs):
    n = len(gs)
    offs = [0]
    for g in gs:
        offs.append(offs[-1] + g.shape[0])

    def body(*refs):
        srcs, outs, send_sems, recv_sems = refs[:n], refs[n:2 * n], refs[2 * n], refs[2 * n + 1]
        x, y, c = _pos()
        cps = [_rcopy(s.at[j, 1 - c], o.at[j], send_sems.at[offs[i] + j], recv_sems.at[offs[i] + j], (x, y, 1 - c))
               for i, (s, o) in enumerate(zip(srcs, outs)) for j in range(s.shape[0])]
        for cp in cps:
            cp.start()
        for cp in cps:
            cp.wait()

    return pl.pallas_call(
        body, name=name, out_shape=[jax.ShapeDtypeStruct(g.shape[:1] + g.shape[2:], g.dtype) for g in gs],
        in_specs=[HBM] * n, out_specs=[HBM] * n,
        scratch_shapes=[pltpu.SemaphoreType.DMA((offs[-1],)), pltpu.SemaphoreType.DMA((offs[-1],))],
    )(*gs)


def scatter_start(name, ss, after):
    n = len(ss)
    lands = [lax.empty((3,) + s.shape[1:], s.dtype) for s in ss]

    def body(*refs):
        srcs, dsts = refs[:n], refs[n:2 * n]
        send_sems, recv_sems, token = refs[2 * n + 1], refs[2 * n + 2], refs[4 * n + 3]
        x, y, c = _pos()
        for i, (s, o) in enumerate(zip(srcs, dsts)):
            for k, m in enumerate(RELATIONS):
                px, py = _related_chip(x, y, m)
                _rcopy(s.at[2 * px + py], o.at[k], send_sems.at[3 * i + k], recv_sems.at[3 * i + k],
                       (px, py, c)).start()
        token[...] = jnp.zeros_like(token)

    both = list(ss) + lands
    res = pl.pallas_call(
        body, name=name,
        out_shape=(pltpu.SemaphoreType.DMA((3 * n,)), pltpu.SemaphoreType.DMA((3 * n,)),
                   *[pltpu.HBM(b.shape, b.dtype) for b in both], jax.ShapeDtypeStruct((8, LANE), F32)),
        in_specs=[HBM] * (2 * n) + [pl.BlockSpec(memory_space=pl.ANY)],
        out_specs=(SEM, SEM, *[HBM] * (2 * n), pl.BlockSpec(memory_space=pltpu.VMEM)),
        input_output_aliases={i: 2 + i for i in range(2 * n)},
        compiler_params=pltpu.CompilerParams(has_side_effects=DATAFLOW),
    )(*[pltpu.with_memory_space_constraint(b, pltpu.HBM) for b in both], after)
    return res[0], res[1], list(res[2:2 + n]), list(res[2 + n:2 + 2 * n]), res[2 + 2 * n]


def scatter_wait(name, send_sems, recv_sems, ss, lands, after):
    n = len(ss)

    def body(*refs):
        srcs, dsts, ssem, rsem = refs[:n], refs[n:2 * n], refs[2 * n], refs[2 * n + 1]
        x, y, c = _pos()
        for i, (s, o) in enumerate(zip(srcs, dsts)):
            for k, m in enumerate(RELATIONS):
                px, py = _related_chip(x, y, m)
                cp = _rcopy(s.at[2 * px + py], o.at[k], ssem.at[3 * i + k], rsem.at[3 * i + k], (px, py, c))
                cp.wait_send()
                cp.wait_recv()

    both = list(ss) + list(lands)
    res = pl.pallas_call(
        body, name=name,
        out_shape=[pltpu.HBM(b.shape, b.dtype) for b in both],
        in_specs=[HBM] * (2 * n) + [SEM, SEM, pl.BlockSpec(memory_space=pl.ANY)],
        out_specs=[HBM] * (2 * n),
        input_output_aliases={i: i for i in range(2 * n)},
        compiler_params=pltpu.CompilerParams(has_side_effects=DATAFLOW),
    )(*both, send_sems, recv_sems, after)
    return list(res[:n]), list(res[n:])


def pair_share(name, bufs):
    n = len(bufs)

    def body(*refs):
        outs, send_sems, recv_sems = refs[n:2 * n], refs[2 * n], refs[2 * n + 1]
        x, y, c = _pos()
        sends = []
        for i, o in enumerate(outs):
            cp = _rcopy(o.at[c], o.at[c], send_sems.at[i], recv_sems.at[i], (x, y, 1 - c))
            cp.start()
            sends.append(cp)
        for i, o in enumerate(outs):
            _rcopy(o.at[1 - c], o.at[1 - c], send_sems.at[i], recv_sems.at[i], (x, y, 1 - c)).wait_recv()
        for cp in sends:
            cp.wait_send()

    return pl.pallas_call(
        body, name=name, out_shape=[jax.ShapeDtypeStruct(b.shape, b.dtype) for b in bufs],
        in_specs=[HBM] * n, out_specs=[HBM] * n,
        input_output_aliases={i: i for i in range(n)},
        scratch_shapes=[pltpu.SemaphoreType.DMA((n,)), pltpu.SemaphoreType.DMA((n,))],
    )(*bufs)


def all_allgather(name, buf):
    r, cd = buf.shape

    def body(src, out, send_sems, recv_sems, lsem):
        x, y, c = _pos()
        me = 4 * x + 2 * y + c
        local = pltpu.make_async_copy(src, out.at[me], lsem)
        local.start()

        def peer(mask):
            px = 1 - x if mask & 4 else x
            py = 1 - y if mask & 2 else y
            pc = 1 - c if mask & 1 else c
            return px, py, pc

        sends = []
        for mask in range(1, N_DEV):
            cp = _rcopy(src, out.at[me], send_sems.at[mask - 1], recv_sems.at[mask - 1], peer(mask))
            cp.start()
            sends.append(cp)
        for mask in range(1, N_DEV):
            px, py, pc = peer(mask)
            _rcopy(src, out.at[4 * px + 2 * py + pc], send_sems.at[mask - 1], recv_sems.at[mask - 1],
                   (px, py, pc)).wait_recv()
        for cp in sends:
            cp.wait_send()
        local.wait()

    return pl.pallas_call(
        body, name=name, out_shape=jax.ShapeDtypeStruct((N_DEV, r, cd), buf.dtype),
        in_specs=[HBM], out_specs=HBM,
        scratch_shapes=[pltpu.SemaphoreType.DMA((N_DEV - 1,)), pltpu.SemaphoreType.DMA((N_DEV - 1,)),
                        pltpu.SemaphoreType.DMA(())],
    )(buf)


ELEMENTWISE_BLOCK_BYTES = 2 << 20


def _row_block(rows, cols):
    for cand in (1024, 512, 256, 128, 64, 32, 16):
        if rows % cand == 0 and cand * cols * 4 <= ELEMENTWISE_BLOCK_BYTES:
            return cand
    return rows


def chip_sum(name, s, r, me, c):
    _, a, b = s.shape
    tr = _row_block(a, b)

    def body(idx_ref, s_ref, r_ref, o_ref):
        del idx_ref
        acc = s_ref[...].astype(F32)
        for k in range(3):
            acc = acc + r_ref[k].astype(F32)
        o_ref[...] = acc

    return pl.pallas_call(
        body, name=name, out_shape=jax.ShapeDtypeStruct((2, a, b), F32),
        grid_spec=pltpu.PrefetchScalarGridSpec(
            num_scalar_prefetch=1, grid=(a // tr,),
            in_specs=[pl.BlockSpec((None, tr, b), lambda i, idx: (idx[0], i, 0)),
                      pl.BlockSpec((3, tr, b), lambda i, idx: (0, i, 0))],
            out_specs=pl.BlockSpec((None, tr, b), lambda i, idx: (idx[1], i, 0))),
        compiler_params=_cparams(("arbitrary",)),
    )(jnp.stack([me, c]).astype(jnp.int32), s, r)


def pair_add(name, p, recv, c):
    nj, _, rh, cd = p.shape
    tr = _row_block(rh, cd)

    def body(c_ref, p_ref, r_ref, o_ref):
        del c_ref
        o_ref[...] = (p_ref[0] + r_ref[...]).astype(o_ref.dtype)

    return pl.pallas_call(
        body, name=name, out_shape=jax.ShapeDtypeStruct((nj, rh, cd), BF16),
        grid_spec=pltpu.PrefetchScalarGridSpec(
            num_scalar_prefetch=1, grid=(nj, rh // tr),
            in_specs=[pl.BlockSpec((1, 1, tr, cd), lambda j, i, c_ref: (j, c_ref[0], i, 0)),
                      pl.BlockSpec((1, tr, cd), lambda j, i, c_ref: (j, i, 0))],
            out_specs=pl.BlockSpec((1, tr, cd), lambda j, i, c_ref: (j, i, 0))),
        compiler_params=_cparams(("arbitrary", "arbitrary")),
    )(jnp.reshape(c, (1,)).astype(jnp.int32), p, recv)


def slab_sum(name, a):
    n, r, cd = a.shape
    tr = _pick(r, (256, 128, 64, 32, 16, 8))

    def body(a_ref, o_ref):
        acc = a_ref[0].astype(F32)
        for j in range(1, n):
            acc = acc + a_ref[j].astype(F32)
        o_ref[...] = acc

    return pl.pallas_call(
        body, name=name, out_shape=jax.ShapeDtypeStruct((r, cd), F32),
        grid=(r // tr,),
        in_specs=[pl.BlockSpec((n, tr, cd), lambda i: (0, i, 0))],
        out_specs=pl.BlockSpec((tr, cd), lambda i: (i, 0)),
        compiler_params=_cparams(("arbitrary",)),
    )(a)


ADAM_C1 = 1.0 - ADAM_B1 ** ADAM_STEP
ADAM_C2 = 1.0 - ADAM_B2 ** ADAM_STEP


def adamw(name, w, g, m, v):
    r, cd = w.shape
    tr = r
    for cand in (512, 256, 128, 64, 32, 16, 8):
        if r % cand == 0 and cand * cd * 4 <= (1 << 20):
            tr = cand
            break

    def body(w_ref, g_ref, m_ref, v_ref, d_ref, nm_ref, nv_ref):
        gv = g_ref[...]
        nm = ADAM_B1 * m_ref[...] + (1.0 - ADAM_B1) * gv
        nv = ADAM_B2 * v_ref[...] + (1.0 - ADAM_B2) * (gv * gv)
        m_hat = nm / ADAM_C1
        v_hat = nv / ADAM_C2
        d_ref[...] = -ADAM_LR * (m_hat / (jnp.sqrt(v_hat) + ADAM_EPS) + ADAM_WD * w_ref[...])
        nm_ref[...] = nm
        nv_ref[...] = nv

    spec = pl.BlockSpec((tr, cd), lambda i: (i, 0))
    sd = jax.ShapeDtypeStruct((r, cd), F32)
    return pl.pallas_call(
        body, name=name, out_shape=[sd, sd, sd], grid=(r // tr,),
        in_specs=[spec] * 4, out_specs=[spec] * 3,
        compiler_params=_cparams(("arbitrary",)),
    )(w, g, m, v)


WEIGHTS = ("norm_mix_w", "w_in", "ssm_conv_w", "ssm_conv_b", "ssm_dt_bias", "ssm_a_log", "ssm_d", "ssm_norm_w",
           "gdn_conv_w", "gdn_a_log", "gdn_dt_bias", "gdn_norm_w", "w_proj_ssm", "w_proj_gdn", "w_out",
           "norm_ffn_w", "w_ffn_in", "w_ffn_down", "final_norm_w")
BIG = (("w_in", 2), ("w_proj_ssm", 1), ("w_proj_gdn", 1), ("w_out", 1), ("w_ffn_in", 2), ("w_ffn_down", 1))
CONVW = (("ssm_conv_w", 2), ("gdn_conv_w", 2))
SHARDED = BIG + CONVW
SMALL = tuple(n for n in WEIGHTS if n not in dict(SHARDED))


def _unpack(buf, shapes, lead=()):
    flat = buf.reshape(lead + (-1,))
    out, o = [], 0
    for shp in shapes:
        n = math.prod(shp)
        out.append(flat[..., o:o + n].reshape(lead + tuple(shp)))
        o += n
    return out


def _pack_rows(arrs, lead=(), mult=8):
    nl = len(lead)
    flat = jnp.concatenate([a.reshape(lead + (-1,)) for a in arrs], axis=nl)
    n = flat.shape[nl]
    rows = -(-n // (mult * LANE)) * mult
    flat = jnp.pad(flat, [(0, 0)] * nl + [(0, rows * LANE - n)])
    return flat.reshape(lead + (rows, LANE))


def _slot_buffer(shard):
    return lax.dynamic_update_slice(lax.empty((N_CHIPS,) + shard.shape, shard.dtype), shard[None],
                                    (0,) * (shard.ndim + 1))


def kernel(x, norm_mix_w, w_in, ssm_conv_w, ssm_conv_b, ssm_dt_bias, ssm_a_log, ssm_d, ssm_norm_w, gdn_conv_w, gdn_a_log, gdn_dt_bias, gdn_norm_w, w_proj_ssm, w_proj_gdn, w_out, norm_ffn_w, w_ffn_in, w_ffn_down, final_norm_w, loss_target, m_norm_mix_w, m_w_in, m_ssm_conv_w, m_ssm_conv_b, m_ssm_dt_bias, m_ssm_a_log, m_ssm_d, m_ssm_norm_w, m_gdn_conv_w, m_gdn_a_log, m_gdn_dt_bias, m_gdn_norm_w, m_w_proj_ssm, m_w_proj_gdn, m_w_out, m_norm_ffn_w, m_w_ffn_in, m_w_ffn_down, m_final_norm_w, v_norm_mix_w, v_w_in, v_ssm_conv_w, v_ssm_conv_b, v_ssm_dt_bias, v_ssm_a_log, v_ssm_d, v_ssm_norm_w, v_gdn_conv_w, v_gdn_a_log, v_gdn_dt_bias, v_gdn_norm_w, v_w_proj_ssm, v_w_proj_gdn, v_w_out, v_norm_ffn_w, v_w_ffn_in, v_w_ffn_down, v_final_norm_w):
    wl = (norm_mix_w, w_in, ssm_conv_w, ssm_conv_b, ssm_dt_bias, ssm_a_log, ssm_d, ssm_norm_w, gdn_conv_w,
          gdn_a_log, gdn_dt_bias, gdn_norm_w, w_proj_ssm, w_proj_gdn, w_out, norm_ffn_w, w_ffn_in, w_ffn_down,
          final_norm_w)
    ml = (m_norm_mix_w, m_w_in, m_ssm_conv_w, m_ssm_conv_b, m_ssm_dt_bias, m_ssm_a_log, m_ssm_d, m_ssm_norm_w,
          m_gdn_conv_w, m_gdn_a_log, m_gdn_dt_bias, m_gdn_norm_w, m_w_proj_ssm, m_w_proj_gdn, m_w_out,
          m_norm_ffn_w, m_w_ffn_in, m_w_ffn_down, m_final_norm_w)
    vl = (v_norm_mix_w, v_w_in, v_ssm_conv_w, v_ssm_conv_b, v_ssm_dt_bias, v_ssm_a_log, v_ssm_d, v_ssm_norm_w,
          v_gdn_conv_w, v_gdn_a_log, v_gdn_dt_bias, v_gdn_norm_w, v_w_proj_ssm, v_w_proj_gdn, v_w_out,
          v_norm_ffn_w, v_w_ffn_in, v_w_ffn_down, v_final_norm_w)
    w = dict(zip(WEIGHTS, wl))
    m = dict(zip(WEIGHTS, ml))
    v = dict(zip(WEIGHTS, vl))
    x_pos, y_pos, c = _pos()
    me = 2 * x_pos + y_pos
    big = [n for n, _ in BIG]

    shards = [w[n].astype(BF16) for n in big]
    shards[0] = jnp.pad(shards[0], ((0, 0), (0, 0), (0, IN_SHARD_PAD - IN_SHARD)))
    conv_shapes = [w[n].shape[1:] for n, _ in CONVW]
    conv_pack = _pack_rows([w[n] for n, _ in CONVW], lead=(DEPTH,), mult=16)

    def slot_buffers(l):
        return [_slot_buffer(s[l].reshape((2, s.shape[1] // 2) + s.shape[2:])) for s in shards + [conv_pack]]

    def assemble(bufs, which):
        out = {}
        for g_, i in zip(bufs, which):
            by_chip = [lax.dynamic_index_in_dim(g_, jnp.bitwise_xor(me, j), 0, keepdims=False)
                       .reshape((-1,) + g_.shape[3:]) for j in range(N_CHIPS)]
            if i < 0:
                conv_parts = [_unpack(p, conv_shapes) for p in by_chip]
                for k, (n, axis) in enumerate(CONVW):
                    out[n] = jnp.concatenate([conv_parts[j][k] for j in range(N_CHIPS)], axis=axis - 1)
            else:
                n, axis = BIG[i]
                cols = IN_SHARD if n == "w_in" else by_chip[0].shape[-1]
                out[n] = jnp.concatenate([p[:, :cols] for p in by_chip], axis=axis - 1)
        return out

    everything = list(range(len(BIG))) + [-1]
    first, rest = [0, -1], everything[1:-1]
    bufs0 = slot_buffers(0)
    landed_first = weights_gather("gather_w_l0_first", [bufs0[i] for i in first])
    rest_sems = gather_start("gather_w_l0_rest_start", [bufs0[i] for i in rest], landed_first[0])
    l1_sems = gather_start("gather_w_l1_start", slot_buffers(1), rest_sems[2][0])

    def land(name, flying, after):
        send_sems, recv_sems, bufs, _ = flying
        return weights_forward(f"{name}_forward", gather_wait(f"{name}_wait", send_sems, recv_sems, bufs, after))

    def gathered_of(l, after):
        if l == 0:
            return {**assemble(landed_first, first),
                    "late_weights": lambda later: assemble(land("gather_w_l0_rest", rest_sems, later), rest)}
        return assemble(land("gather_w_l1", l1_sems, after), everything)

    token = rest_sems[3] + l1_sems[3]

    in_flight = {}

    def start_reduction(l, g_layer, dx):
        halves = [g_layer[n].reshape((1, 2, -1, AL_DIM)) if n == "w_in"
                  else g_layer[n].reshape((N_CHIPS, 2, -1) + g_layer[n].shape[-1:]) for n in big]
        from_pair = pair_swap(f"grad_pair_swap_l{l}", halves)
        chip_part = [pair_add(f"grad_pair_add_{n}_l{l}", g_, r_, c) for n, g_, r_ in zip(big, halves, from_pair)]
        chip_part[0] = _aligned_to_shards(chip_part[0][0])
        *in_flight[l], token_l = scatter_start(f"grad_scatter_start_l{l}", chip_part, dx)
        return dx + token_l[0, 0]

    def finish_reduction(l, after):
        parts, landed = scatter_wait(f"grad_scatter_wait_l{l}", *in_flight[l], after)
        sums = [chip_sum(f"grad_chip_sum_{n}_l{l}", s_, r_, me, c) for n, s_, r_ in zip(big, parts, landed)]
        return [r_.reshape((-1,) + r_.shape[2:]) for r_ in pair_share(f"grad_pair_share_l{l}", sums)]

    full = {n: w[n] for n in SMALL}
    loss_part, grad_x, grads = local_step(x[0] + token[0, 0], loss_target[0], full, gathered_of, start_reduction)
    reduced = [None] * DEPTH
    reduced[1] = finish_reduction(1, grad_x)

    small_names = list(SMALL) + [n for n, _ in CONVW]
    small_all = all_allgather("gather_small", _pack_rows([grads[n] for n in small_names] + [loss_part[0, :1]]))
    small_sum = slab_sum("small_sum", small_all)
    small_vals = _unpack(small_sum, [grads[n].shape for n in small_names] + [(1,)])
    g_small = dict(zip(small_names, small_vals[:-1]))
    loss = small_vals[-1].reshape(())
    out_g, out_d, out_m, out_v = {}, {}, {}, {}
    d_, m_, v_ = adamw("adamw_small", *[_pack_rows([d[n] for n in SMALL]) for d in (w, g_small, m, v)])
    small_shapes = [w[n].shape for n in SMALL]
    for n, dd, mm, vv in zip(SMALL, _unpack(d_, small_shapes), _unpack(m_, small_shapes), _unpack(v_, small_shapes)):
        out_g[n], out_d[n], out_m[n], out_v[n] = g_small[n], dd, mm, vv

    reduced[0] = finish_reduction(0, d_[:1] + reduced[1][0][:1, :LANE])
    g_sharded = {n: jnp.stack([reduced[l][i] for l in range(DEPTH)]) for i, n in enumerate(big)}
    g_sharded["w_in"] = g_sharded["w_in"][:, :, :IN_SHARD]
    for n, axis in CONVW:
        size = w[n].shape[axis]
        g_sharded[n] = lax.dynamic_slice_in_dim(g_small.pop(n), me * size, size, axis=axis)

    for n, _ in SHARDED:
        shp = w[n].shape
        two = lambda a: a.reshape(-1, shp[-1])
        d_, m_, v_ = adamw(f"adamw_{n}", two(w[n]), two(g_sharded[n]), two(m[n]), two(v[n]))
        out_g[n], out_d[n], out_m[n], out_v[n] = g_sharded[n], d_.reshape(shp), m_.reshape(shp), v_.reshape(shp)

    return (loss, grad_x[None], *[out_g[n] for n in WEIGHTS], *[out_d[n] for n in WEIGHTS],
            *[out_m[n] for n in WEIGHTS], *[out_v[n] for n in WEIGHTS])
```

```python
import math

import jax
import jax.numpy as jnp
from jax import lax
from jax.experimental import pallas as pl
from jax.experimental.pallas import tpu as pltpu

F32 = jnp.float32
BF16 = jnp.bfloat16

D_MODEL = 1024
DEPTH = 2
SSM_HEADS = 16
SSM_HEAD_DIM = 64
SSM_D_INNER = 1024
SSM_STATE = 128
SSM_CONV_DIM = 1536
GDN_HEADS = 8
GDN_HEAD = 128
GDN_QKV_DIM = 3072
CONV_K = 4
CHUNK = 128
SSD_CHUNK = 256
FFN_HIDDEN = 2816
EPS = 1e-6
IN_DIM = 8736

ADAM_LR = 0.001
ADAM_B1 = 0.9
ADAM_B2 = 0.999
ADAM_EPS = 1e-08
ADAM_WD = 0.01
ADAM_STEP = 10

LANE = 128
NEG_BIG = -1e30
VMEM_LIMIT = 56 * 1024 * 1024

AL_Z, AL_XBC, AL_QKV, AL_GZ, AL_GS, AL_GG, AL_SMALL, AL_DIM = 0, 1024, 2560, 5632, 6656, 7680, 8704, 9216
SM_DT, SM_A, SM_B = 0, 16, 24

HI = lax.Precision.HIGHEST
NN = (((1,), (0,)), ((), ()))
NT = (((1,), (1,)), ((), ()))
TN = (((0,), (0,)), ((), ()))


def _cparams(sem):
    return pltpu.CompilerParams(dimension_semantics=sem, vmem_limit_bytes=VMEM_LIMIT)


def _pick(n, prefs):
    for p in prefs:
        if n % p == 0:
            return p
    return n


MATMUL_VMEM_BUDGET = 40 << 20
MXU_WIDTH = 256
HBM_BYTES_PER_S = 3.3e12
MXU_FLOPS_PER_S = 9.0e14
GRID_STEP_S = 0.35e-6
N_CHIPS = 4


def _matmul_tiles(m, n, n_dom, k, a_item, b_item, o_item):
    best = None
    def cands(dim, cap):
        return [c for c in range(LANE, min(dim, cap) + 1, LANE) if dim % c == 0] or [dim]

    tms, tns, tks = cands(m, 2048), cands(n_dom, 2304), cands(k, 1 << 30)
    for tm in tms:
        for tn in tns:
            for tk in tks:
                nk = k // tk
                vmem = (2 * (tm * tk * a_item + tk * tn * b_item + tm * tn * o_item) + tm * tn * 4 * (2 if nk > 1 else 1)
                        + (tm * tk * 2 if a_item > 2 else 0) + (tk * tn * 2 if b_item > 2 else 0))
                if vmem > MATMUL_VMEM_BUDGET:
                    continue
                traffic = m * k * a_item * (1 if nk == 1 else n // tn) + k * n * b_item * (m // tm)
                mxu_fill = tn / (-(-tn // MXU_WIDTH) * MXU_WIDTH)
                cost = (max(traffic / HBM_BYTES_PER_S, 2.0 * m * n * k / (MXU_FLOPS_PER_S * mxu_fill))
                        + (m // tm) * (n // tn) * nk * GRID_STEP_S)
                if best is None or cost < best[0]:
                    best = (cost, (tm, tn, tk))
    return best[1]


def _dot(a, b, dims=NN):
    return lax.dot_general(a.astype(BF16), b.astype(BF16), dims, preferred_element_type=F32)


def _dot3(a, b, dims=NN):
    a_hi, b_hi = a.astype(BF16), b.astype(BF16)
    a_lo = (a - a_hi.astype(F32)).astype(BF16)
    b_lo = (b - b_hi.astype(F32)).astype(BF16)

    def dg(u, v):
        return lax.dot_general(u, v, dims, preferred_element_type=F32)

    return dg(a_hi, b_hi) + (dg(a_hi, b_lo) + dg(a_lo, b_hi))


def _dot_hi(a, b, dims=NN):
    return lax.dot_general(a, b, dims, precision=HI, preferred_element_type=F32)


def _sigmoid(x):
    return jax.nn.sigmoid(x)


def _silu(x):
    return x * _sigmoid(x)


def _softplus(x):
    return jnp.maximum(x, 0.0) + jnp.log1p(jnp.exp(-jnp.abs(x)))


def matmul(name, a, b, mode, out_dtype=F32, chip_major=False, stack=None):
    if mode == "nn":
        (m, k), (k2, n) = a.shape, b.shape
    elif mode == "nt":
        (m, k), (n, k2) = a.shape, b.shape
    else:
        (k, m), (k2, n) = a.shape, b.shape
    assert k == k2, (a.shape, b.shape, mode)
    shard = n // N_CHIPS if chip_major else n
    tm, tn, tk = _matmul_tiles(m, n, shard, k, a.dtype.itemsize, b.dtype.itemsize, jnp.dtype(out_dtype).itemsize)
    if chip_major:
        per = shard // tn
        base_shape, base_blk = (N_CHIPS, m, shard), (None, tm, tn)
        base_idx = lambda i, j: (j // per, i, j % per)
    else:
        base_shape, base_blk = (m, n), (tm, tn)
        base_idx = lambda i, j: (i, j)
    nk = k // tk
    dims = {"nn": NN, "nt": NT, "tn": TN}[mode]

    def body_acc(a_ref, b_ref, o_ref, acc_ref):
        kk = pl.program_id(2)

        @pl.when(kk == 0)
        def _():
            acc_ref[...] = jnp.zeros_like(acc_ref)

        acc_ref[...] += _dot(a_ref[...], b_ref[...], dims)

        @pl.when(kk == nk - 1)
        def _():
            o_ref[...] = acc_ref[...].astype(o_ref.dtype)

    def body_one(a_ref, b_ref, o_ref):
        o_ref[...] = _dot(a_ref[...], b_ref[...], dims).astype(o_ref.dtype)

    compute = body_one if nk == 1 else body_acc
    if mode == "tn":
        a_spec = pl.BlockSpec((tk, tm), lambda i, j, kk: (kk, i))
    else:
        a_spec = pl.BlockSpec((tm, tk), lambda i, j, kk: (i, kk))
    if mode == "nt":
        b_spec = pl.BlockSpec((tn, tk), lambda i, j, kk: (j, kk))
    else:
        b_spec = pl.BlockSpec((tk, tn), lambda i, j, kk: (kk, j))
    in_specs, operands, aliases, body = [a_spec, b_spec], [a, b], {}, compute
    if stack is None:
        out_shape, out_blk, out_idx = base_shape, base_blk, (lambda i, j, kk: base_idx(i, j))
    else:
        layer, buf = stack
        out_shape, out_blk = (DEPTH,) + base_shape, (None,) + base_blk
        out_idx = lambda i, j, kk: (layer,) + base_idx(i, j)
        if buf is not None:
            assert buf.shape == out_shape and buf.dtype == out_dtype
            in_specs.append(pl.BlockSpec(memory_space=pl.ANY))
            operands.append(buf)
            aliases = {2: 0}

            def body(a_ref, b_ref, buf_ref, *rest):
                del buf_ref
                compute(a_ref, b_ref, *rest)

    return pl.pallas_call(
        body, name=name,
        out_shape=jax.ShapeDtypeStruct(out_shape, out_dtype),
        grid=(m // tm, n // tn, nk),
        in_specs=in_specs,
        out_specs=pl.BlockSpec(out_blk, out_idx),
        scratch_shapes=[] if nk == 1 else [pltpu.VMEM((tm, tn), F32)],
        input_output_aliases=aliases,
        compiler_params=_cparams(("parallel", "parallel", "arbitrary")),
    )(*operands)


def _row_map(c0, moves):
    return (lambda j, i: (i, c0 + j)) if moves else (lambda j, i: (i, c0))


def _par_map(c0, moves):
    return (lambda j, i: (0, c0 + j)) if moves else (lambda j, i: (0, c0))


def _in_spec(op, tile):
    _, kind, w, c0, moves = op
    if kind == "row":
        return pl.BlockSpec((tile, w), _row_map(c0, moves))
    return pl.BlockSpec((1, w), _par_map(c0, moves))


ROW_BLOCK_ELEMS = 1 << 18


def _row_tile(t, tile, ops):
    width = max(op[2] for op in ops if op[1] == "row")
    return min(t, max(tile, ROW_BLOCK_ELEMS // width))


def rowwise_fwd(name, fn, t, tile, ncol, ins, outs):
    n_in = len(ins)
    tile = _row_tile(t, tile, ins)

    def body(*refs):
        vals = [r[...].astype(F32) for r in refs[:n_in]]
        res = fn(*vals)
        if not isinstance(res, (tuple, list)):
            res = (res,)
        for r, v in zip(refs[n_in:], res):
            r[...] = v.astype(r.dtype)

    res = pl.pallas_call(
        body, name=name,
        out_shape=[jax.ShapeDtypeStruct((t, w * ncol), dt) for w, dt in outs],
        grid=(ncol, t // tile),
        in_specs=[_in_spec(op, tile) for op in ins],
        out_specs=[pl.BlockSpec((tile, w), _row_map(0, True)) for w, _ in outs],
        compiler_params=_cparams(("arbitrary", "arbitrary")),
    )(*[op[0] for op in ins])
    return res


def rowwise_bwd(name, fn, t, tile, ncol, ins, need, cts, addends=None, row_dtypes=None):
    n_in, n_ct = len(ins), len(cts)
    tile = _row_tile(t, tile, ins)
    addends = addends or {}
    row_dtypes = row_dtypes or {}
    didx = [i for i in range(n_in) if need[i]]
    add_ops = [addends[i] for i in didx if i in addends]
    n_add = len(add_ops)

    def body(*refs):
        in_refs = refs[:n_in]
        ct_refs = refs[n_in:n_in + n_ct]
        add_refs = refs[n_in + n_ct:n_in + n_ct + n_add]
        out_refs = refs[n_in + n_ct + n_add:]
        vals = [r[...].astype(F32) for r in in_refs]

        def g(*dv):
            full = list(vals)
            for i, v in zip(didx, dv):
                full[i] = v
            res = fn(*full)
            return tuple(res) if isinstance(res, (tuple, list)) else (res,)

        _, vjp = jax.vjp(g, *[vals[i] for i in didx])
        grads = vjp(tuple(c[...].astype(F32) for c in ct_refs))
        j, i = pl.program_id(0), pl.program_id(1)
        a = 0
        for o_ref, gv, idx in zip(out_refs, grads, didx):
            _, kind, _, _, moves = ins[idx]
            if kind == "row":
                if idx in addends:
                    gv = gv + add_refs[a][...].astype(F32)
                    a += 1
                o_ref[...] = gv.astype(o_ref.dtype)
            else:
                first = (i == 0) if moves else jnp.logical_and(i == 0, j == 0)

                @pl.when(first)
                def _(o_ref=o_ref, gv=gv):
                    o_ref[...] = gv

                @pl.when(jnp.logical_not(first))
                def _(o_ref=o_ref, gv=gv):
                    o_ref[...] += gv

    out_shape, out_specs = [], []
    for idx in didx:
        _, kind, w, _, moves = ins[idx]
        cols = w * (ncol if moves else 1)
        if kind == "row":
            out_shape.append(jax.ShapeDtypeStruct((t, cols), row_dtypes.get(idx, F32)))
            out_specs.append(pl.BlockSpec((tile, w), _row_map(0, moves)))
        else:
            out_shape.append(jax.ShapeDtypeStruct((1, cols), F32))
            out_specs.append(pl.BlockSpec((1, w), _par_map(0, moves)))
    ops = list(ins) + list(cts) + add_ops
    res = pl.pallas_call(
        body, name=name,
        out_shape=out_shape,
        grid=(ncol, t // tile),
        in_specs=[_in_spec(op, tile) for op in ops],
        out_specs=out_specs,
        compiler_params=_cparams(("arbitrary", "arbitrary")),
    )(*[op[0] for op in ops])
    return res


def f_rmsnorm(x, w):
    return x * lax.rsqrt(jnp.mean(x * x, axis=-1, keepdims=True) + EPS) * w


def f_ssd_post(y, z, w):
    y = y * _silu(z)
    return y * lax.rsqrt(jnp.mean(y * y, axis=-1, keepdims=True) + EPS) * w


def f_gdn_post(o, z, w):
    o = o * lax.rsqrt(jnp.mean(o * o, axis=-1, keepdims=True) + EPS) * w
    return o * _silu(z)


def f_merge(gs, p1, gg, p2):
    return _sigmoid(gs) * p1 + _sigmoid(gg) * p2


def f_swiglu(g, u):
    return _silu(g) * u


def final_loss(name, x, tgt, w, tile=256):
    t, d = x.shape

    def body(x_ref, t_ref, w_ref, loss_ref, dx_ref, dw_ref):
        i = pl.program_id(0)
        xv, tv, wv = x_ref[...], t_ref[...], w_ref[...]

        def g(xx, ww):
            err = f_rmsnorm(xx, ww) - tv
            return 0.5 * jnp.sum(jnp.mean(err * err, axis=-1, keepdims=True), axis=0, keepdims=True)

        val, vjp = jax.vjp(g, xv, wv)
        dx, dw = vjp(jnp.ones((1, 1), F32))
        dx_ref[...] = dx
        lv = jnp.broadcast_to(val, (1, LANE))

        @pl.when(i == 0)
        def _():
            loss_ref[...] = lv
            dw_ref[...] = dw

        @pl.when(i != 0)
        def _():
            loss_ref[...] += lv
            dw_ref[...] += dw

    return pl.pallas_call(
        body, name=name,
        out_shape=[jax.ShapeDtypeStruct((1, LANE), F32), jax.ShapeDtypeStruct((t, d), F32),
                   jax.ShapeDtypeStruct((1, d), F32)],
        grid=(t // tile,),
        in_specs=[pl.BlockSpec((tile, d), lambda i: (i, 0)), pl.BlockSpec((tile, d), lambda i: (i, 0)),
                  pl.BlockSpec((1, d), lambda i: (0, 0))],
        out_specs=[pl.BlockSpec((1, LANE), lambda i: (0, 0)), pl.BlockSpec((tile, d), lambda i: (i, 0)),
                   pl.BlockSpec((1, d), lambda i: (0, 0))],
        compiler_params=_cparams(("arbitrary",)),
    )(x, tgt, w)


CONV_W = 512
HALO = 8
STRIPS = 16


def _rows_back(before, cur, d):
    rows = lax.broadcasted_iota(jnp.int32, cur.shape, 0)
    return jnp.where(rows < d, pltpu.roll(before, d, 0), pltpu.roll(cur, d, 0))


def _rows_ahead(cur, after, d):
    rows = lax.broadcasted_iota(jnp.int32, cur.shape, 0)
    return jnp.where(rows < HALO - d, pltpu.roll(cur, HALO - d, 0), pltpu.roll(after, HALO - d, 0))


def _conv_taps(taps, bias, before, cur):
    shifted = [_rows_back(before, cur, CONV_K - 1 - k) for k in range(CONV_K - 1)] + [cur]
    pre = bias + taps[CONV_K - 1] * cur
    for k in range(CONV_K - 1):
        pre = pre + taps[k] * shifted[k]
    return pre, shifted


def conv_fwd(name, src, c0, width, w, b, tile=512):
    t = src.shape[0]
    ncol, nrow = width // CONV_W, t // tile
    cb0 = c0 // CONV_W
    hb = tile // HALO

    def body(prev_ref, cur_ref, w_ref, b_ref, o_ref):
        i = pl.program_id(1)
        taps = [w_ref[k:k + 1, :] for k in range(CONV_K)]
        bias = b_ref[...]

        def strips(g, before):
            for u in range(STRIPS):
                r0 = pl.multiple_of((g * STRIPS + u) * HALO, HALO)
                cur = cur_ref[pl.ds(r0, HALO), :]
                pre, _ = _conv_taps(taps, bias, before, cur)
                o_ref[pl.ds(r0, HALO), :] = _silu(pre)
                before = cur
            return before

        lax.fori_loop(0, tile // (HALO * STRIPS), strips, jnp.where(i == 0, 0.0, prev_ref[...]))

    return pl.pallas_call(
        body, name=name,
        out_shape=jax.ShapeDtypeStruct((t, width), F32),
        grid=(ncol, nrow),
        in_specs=[pl.BlockSpec((HALO, CONV_W), lambda j, i: (jnp.maximum(i * hb - 1, 0), cb0 + j)),
                  pl.BlockSpec((tile, CONV_W), lambda j, i: (i, cb0 + j)),
                  pl.BlockSpec((CONV_K, CONV_W), lambda j, i: (0, j)),
                  pl.BlockSpec((1, CONV_W), lambda j, i: (0, j))],
        out_specs=pl.BlockSpec((tile, CONV_W), lambda j, i: (i, j)),
        compiler_params=_cparams(("arbitrary", "arbitrary")),
    )(src, src, w, b)


def conv_bwd(name, src, c0, width, w, b, dy, tile=512):
    t = src.shape[0]
    ncol, nrow = width // CONV_W, t // tile
    cb0 = c0 // CONV_W
    hb = tile // HALO
    last_hb = t // HALO - 1
    nstrip = tile // HALO

    def body(sprev_ref, scur_ref, snext_ref, w_ref, b_ref, dycur_ref, dynext_ref,
             du_ref, dw_ref, db_ref, dpre_ref):
        i = pl.program_id(1)
        taps = [w_ref[k:k + 1, :] for k in range(CONV_K)]
        bias = b_ref[...]

        def dpre_of(before, cur, dy_strip):
            pre, shifted = _conv_taps(taps, bias, before, cur)
            s = _sigmoid(pre)
            return dy_strip * (s * (1.0 + pre * (1.0 - s))), shifted

        def strips1(g, carry):
            before, dws, dbs = carry
            for u in range(STRIPS):
                r0 = pl.multiple_of((g * STRIPS + u) * HALO, HALO)
                cur = scur_ref[pl.ds(r0, HALO), :]
                dpre, shifted = dpre_of(before, cur, dycur_ref[pl.ds(r0, HALO), :])
                dpre_ref[pl.ds(r0, HALO), :] = dpre
                before, dws, dbs = cur, tuple(a + dpre * v for a, v in zip(dws, shifted)), dbs + dpre
            return before, dws, dbs

        zero = jnp.zeros((HALO, CONV_W), F32)
        before, dws, dbs = lax.fori_loop(0, nstrip // STRIPS, strips1,
                                         (jnp.where(i == 0, 0.0, sprev_ref[...]), (zero,) * CONV_K, zero))
        dpre_next, _ = dpre_of(before, snext_ref[...], dynext_ref[...])
        dpre_ref[pl.ds(tile, HALO), :] = jnp.where(i == nrow - 1, 0.0, dpre_next)

        def strips2(g, _):
            parts = []
            for u in range(STRIPS):
                r0 = pl.multiple_of((g * STRIPS + u) * HALO, HALO)
                cur = dpre_ref[pl.ds(r0, HALO), :]
                after = dpre_ref[pl.ds(r0 + HALO, HALO), :]
                acc = taps[CONV_K - 1] * cur
                for d in range(1, CONV_K):
                    acc = acc + taps[CONV_K - 1 - d] * _rows_ahead(cur, after, d)
                parts.append(acc)
            r0 = pl.multiple_of(g * STRIPS * HALO, STRIPS * HALO)
            du_ref[pl.ds(r0, STRIPS * HALO), :] = jnp.concatenate(parts, axis=0).astype(du_ref.dtype)
            return 0

        lax.fori_loop(0, nstrip // STRIPS, strips2, 0)
        dw_tile = jnp.concatenate([jnp.sum(a, axis=0, keepdims=True) for a in dws], axis=0)
        db_tile = jnp.sum(dbs, axis=0, keepdims=True)
        _acc(dw_ref, dw_tile, i == 0)
        _acc(db_ref, db_tile, i == 0)

    return pl.pallas_call(
        body, name=name,
        out_shape=[jax.ShapeDtypeStruct((t, width), BF16), jax.ShapeDtypeStruct((CONV_K, width), F32),
                   jax.ShapeDtypeStruct((1, width), F32)],
        grid=(ncol, nrow),
        in_specs=[pl.BlockSpec((HALO, CONV_W), lambda j, i: (jnp.maximum(i * hb - 1, 0), cb0 + j)),
                  pl.BlockSpec((tile, CONV_W), lambda j, i: (i, cb0 + j)),
                  pl.BlockSpec((HALO, CONV_W), lambda j, i: (jnp.minimum((i + 1) * hb, last_hb), cb0 + j)),
                  pl.BlockSpec((CONV_K, CONV_W), lambda j, i: (0, j)),
                  pl.BlockSpec((1, CONV_W), lambda j, i: (0, j)),
                  pl.BlockSpec((tile, CONV_W), lambda j, i: (i, j)),
                  pl.BlockSpec((HALO, CONV_W), lambda j, i: (jnp.minimum((i + 1) * hb, last_hb), j))],
        out_specs=[pl.BlockSpec((tile, CONV_W), lambda j, i: (i, j)),
                   pl.BlockSpec((CONV_K, CONV_W), lambda j, i: (0, j)),
                   pl.BlockSpec((1, CONV_W), lambda j, i: (0, j))],
        scratch_shapes=[pltpu.VMEM((tile + HALO, CONV_W), F32)],
        compiler_params=_cparams(("arbitrary", "arbitrary")),
    )(src, src, src, w, b, dy, dy)


def _iota2(q):
    return (lax.broadcasted_iota(jnp.int32, (q, q), 0), lax.broadcasted_iota(jnp.int32, (q, q), 1))


def _lane_pick(blk, idx):
    lane = lax.broadcasted_iota(jnp.int32, (1, LANE), 1)
    return jnp.sum(jnp.where(lane == idx, blk, 0.0), axis=1, keepdims=True)


class _Decay:
    def __init__(self, a):
        q = a.shape[0]
        r, c = _iota2(q)
        self.r, self.c = r, c
        self.cum = _dot_hi((c <= r).astype(F32), a)
        self.cum_t = _dot_hi(a, (r <= c).astype(F32), TN)
        self.tot = self.cum[q - 1:q, :]
        self.e_cum = jnp.exp(self.cum)
        self.e_rest = jnp.exp(self.tot - self.cum)
        self.e_tot = jnp.exp(self.tot)

    def mask(self, lane):
        rows = lax.broadcasted_iota(jnp.int32, (LANE, 1), 0)
        cum_row = jnp.sum(jnp.where(rows == lane, self.cum_t, 0.0), axis=0, keepdims=True)
        return jnp.exp(jnp.where(self.r >= self.c, _lane_pick(self.cum, lane) - cum_row, NEG_BIG))


_SSD_B = SSM_D_INNER
_SSD_C = SSM_D_INNER + 2 * SSM_STATE


def _interleave(gens):
    results = [None] * len(gens)
    live = list(range(len(gens)))
    while live:
        for i in list(live):
            try:
                next(gens[i])
            except StopIteration as stop:
                results[i] = stop.value
                live.remove(i)
    return results


def ssd_chunk(xs, bm, cm, dt_all, dsk, dec, state, p, cb):
    lane = lax.broadcasted_iota(jnp.int32, (1, LANE), 1)
    m0 = lane < SSM_HEAD_DIM
    h0, h1 = 2 * p, 2 * p + 1

    def both(blk):
        return jnp.where(m0, _lane_pick(blk, h0), _lane_pick(blk, h1))

    xdt = xs * both(dt_all)
    l0, l1 = dec.mask(h0), dec.mask(h1)
    yield
    y_diag = _dot(cb * l0, jnp.where(m0, xdt, 0.0)) + _dot(cb * l1, jnp.where(m0, 0.0, xdt))
    y_off = _dot(cm, state, NT) * both(dec.e_cum)
    yield
    rowm = lax.broadcasted_iota(jnp.int32, (LANE, 1), 0) < SSM_HEAD_DIM
    new_state = (state * jnp.where(rowm, _lane_pick(dec.e_tot, h0), _lane_pick(dec.e_tot, h1))
                 + _dot(xdt * both(dec.e_rest), bm, TN))
    y = y_diag + y_off + both(dsk) * xs
    return y, new_state


def ssd_pairs(xs, bms, cms, small, dtb, alog, dsk, states):
    dt_all = _softplus(small + dtb)
    dec = _Decay(dt_all * (-jnp.exp(alog)))
    cbs = [_dot(cm, bm, NT) for cm, bm in zip(cms, bms)]
    res = _interleave([ssd_chunk(x, bms[p // 4], cms[p // 4], dt_all, dsk, dec, st, p, cbs[p // 4])
                       for p, (x, st) in enumerate(zip(xs, states))])
    return tuple(y for y, _ in res), tuple(s for _, s in res)


def tri_inverse(a):
    q = a.shape[0]
    r, c = _iota2(q)
    eye = (r == c).astype(F32)
    diag = (r // 16) == (c // 16)
    bd = jnp.where(diag, a, 0.0)
    off = jnp.where(diag, 0.0, a)
    b2 = _dot3(bd, bd)
    d1 = _dot3(eye - bd, eye + b2)
    yield
    b4 = _dot3(b2, b2)
    yield
    b8 = _dot3(b4, b4)
    d2 = _dot3(d1, eye + b4)
    yield
    dinv = _dot3(d2, eye + b8)
    yield
    n = _dot3(dinv, off)
    yield
    powers = [n]
    while 16 * 2 ** len(powers) < q:
        powers.append(_dot3(powers[-1], powers[-1]))
        yield
    m = dinv
    for pw in reversed(powers[1:]):
        m = _dot3(eye + pw, m)
        yield
    return _dot3(eye - n, m)


@jax.custom_vjp
def _solve_with(xinv, a, rhs):
    del a
    return _dot3(xinv, rhs)


def _solve_with_fwd(xinv, a, rhs):
    t = _dot3(xinv, rhs)
    return t, (xinv, t)


def _solve_with_bwd(res, dt):
    xinv, t = res
    d_rhs = _dot3(xinv, dt, TN)
    d_a = -_dot(d_rhs, t, NT)
    return jnp.zeros_like(xinv), d_a, d_rhs


_solve_with.defvjp(_solve_with_fwd, _solve_with_bwd)


_GDN_K = GDN_HEADS * GDN_HEAD
_GDN_V = 2 * GDN_HEADS * GDN_HEAD


def gdn_chunk(qh, kh, vh, beta_all, dec, state, h, xinv=None):
    r, c = dec.r, dec.c
    qn = qh * lax.rsqrt(jnp.sum(qh * qh, axis=-1, keepdims=True) + EPS) * (GDN_HEAD ** -0.5)
    kn = kh * lax.rsqrt(jnp.sum(kh * kh, axis=-1, keepdims=True) + EPS)
    beta = _lane_pick(beta_all, SM_B + h)
    decay = dec.mask(SM_A + h)
    yield
    kk = _dot(kn, kn, NT)
    qk = _dot(qn, kn, NT) * decay
    amat = jnp.where(r > c, kk * decay * beta, 0.0)
    eg = _lane_pick(dec.e_cum, SM_A + h)
    rhs = jnp.concatenate([vh * beta, kn * (beta * eg)], axis=1)
    qs = _dot(qn * eg, state)
    yield
    if xinv is None:
        xinv = yield from tri_inverse(amat)
        t = _dot3(xinv, rhs)
    else:
        t = _solve_with(xinv, amat, rhs)
    yield
    u, w = t[:, :GDN_HEAD], t[:, GDN_HEAD:]
    v_new = u - _dot(w, state)
    yield
    o = qs + _dot(qk, v_new)
    new_state = (state * _lane_pick(dec.e_tot, SM_A + h)
                 + _dot(kn * _lane_pick(dec.e_rest, SM_A + h), v_new, TN))
    return o, new_state, xinv


def gdn_heads(qs, ks, vs, small, alog, dtb, states, xinvs=None):
    nh = len(qs)
    beta_all = _sigmoid(small)
    dec = _Decay(-jnp.exp(alog) * _softplus(small + dtb))
    res = _interleave([gdn_chunk(qs[h], ks[h], vs[h], beta_all, dec, states[h], h,
                                 None if xinvs is None else xinvs[h]) for h in range(nh)])
    return tuple(o for o, _, _ in res), tuple(s for _, s, _ in res), tuple(x for _, _, x in res)


def _acc(ref, val, first):
    @pl.when(first)
    def _():
        ref[...] = val

    @pl.when(jnp.logical_not(first))
    def _():
        ref[...] += val


def ssd_scan_fwd(name, xbc, proj, dtb, alog, dsk):
    t = xbc.shape[0]
    nc, npair = t // SSD_CHUNK, SSM_HEADS // 2
    small_blk = AL_SMALL // LANE

    def body(xbc_ref, sm_ref, dtb_ref, alog_ref, dsk_ref, y_ref, sin_ref, st_ref):
        ci = pl.program_id(0)

        @pl.when(ci == 0)
        def _():
            st_ref[...] = jnp.zeros_like(st_ref)

        s_in = tuple(st_ref[p] for p in range(npair))
        ys, s_new = ssd_pairs(tuple(xbc_ref[:, p * LANE:(p + 1) * LANE] for p in range(npair)),
                              tuple(xbc_ref[:, _SSD_B + g * LANE:_SSD_B + (g + 1) * LANE] for g in range(2)),
                              tuple(xbc_ref[:, _SSD_C + g * LANE:_SSD_C + (g + 1) * LANE] for g in range(2)),
                              sm_ref[...], dtb_ref[...], alog_ref[...], dsk_ref[...], s_in)
        for p in range(npair):
            sin_ref[0, p] = s_in[p]
            y_ref[:, p * LANE:(p + 1) * LANE] = ys[p]
            st_ref[p] = s_new[p]

    par = pl.BlockSpec((1, LANE), lambda ci: (0, 0))
    return pl.pallas_call(
        body, name=name,
        out_shape=[jax.ShapeDtypeStruct((t, SSM_D_INNER), F32),
                   jax.ShapeDtypeStruct((nc, npair, LANE, LANE), F32)],
        grid=(nc,),
        in_specs=[pl.BlockSpec((SSD_CHUNK, SSM_CONV_DIM), lambda ci: (ci, 0)),
                  pl.BlockSpec((SSD_CHUNK, LANE), lambda ci: (ci, small_blk)),
                  par, par, par],
        out_specs=[pl.BlockSpec((SSD_CHUNK, SSM_D_INNER), lambda ci: (ci, 0)),
                   pl.BlockSpec((1, npair, LANE, LANE), lambda ci: (ci, 0, 0, 0))],
        scratch_shapes=[pltpu.VMEM((npair, LANE, LANE), F32)],
        compiler_params=_cparams(("arbitrary",)),
    )(xbc, proj, dtb, alog, dsk)


def ssd_scan_bwd(name, xbc, proj, dtb, alog, dsk, s_in, dy):
    t = xbc.shape[0]
    nc, npair = t // SSD_CHUNK, SSM_HEADS // 2
    small_blk = AL_SMALL // LANE

    def body(xbc_ref, sm_ref, dtb_ref, alog_ref, dsk_ref, sin_ref, dy_ref,
             dxbc_ref, dsm_ref, ddtb_ref, dalog_ref, ddsk_ref, dst_ref):
        ci = pl.program_id(0)

        @pl.when(ci == 0)
        def _():
            dst_ref[...] = jnp.zeros_like(dst_ref)

        _, vjp = jax.vjp(ssd_pairs, tuple(xbc_ref[:, p * LANE:(p + 1) * LANE] for p in range(npair)),
                         tuple(xbc_ref[:, _SSD_B + g * LANE:_SSD_B + (g + 1) * LANE] for g in range(2)),
                         tuple(xbc_ref[:, _SSD_C + g * LANE:_SSD_C + (g + 1) * LANE] for g in range(2)),
                         sm_ref[...], dtb_ref[...], alog_ref[...], dsk_ref[...],
                         tuple(sin_ref[0, p] for p in range(npair)))
        dxs, dbms, dcms, dsm, ddtb, dalog, ddsk, dsts = vjp(
            (tuple(dy_ref[:, p * LANE:(p + 1) * LANE] for p in range(npair)),
             tuple(dst_ref[p] for p in range(npair))))
        for p in range(npair):
            dxbc_ref[:, p * LANE:(p + 1) * LANE] = dxs[p]
            dst_ref[p] = dsts[p]
        for g in range(2):
            dxbc_ref[:, _SSD_B + g * LANE:_SSD_B + (g + 1) * LANE] = dbms[g]
            dxbc_ref[:, _SSD_C + g * LANE:_SSD_C + (g + 1) * LANE] = dcms[g]
        dsm_ref[...] = dsm
        _acc(ddtb_ref, ddtb, ci == 0)
        _acc(dalog_ref, dalog, ci == 0)
        _acc(ddsk_ref, ddsk, ci == 0)

    par = pl.BlockSpec((1, LANE), lambda ci: (0, 0))
    rev = lambda ci: nc - 1 - ci
    return pl.pallas_call(
        body, name=name,
        out_shape=[jax.ShapeDtypeStruct((t, SSM_CONV_DIM), F32),
                   jax.ShapeDtypeStruct((t, LANE), F32),
                   jax.ShapeDtypeStruct((1, LANE), F32), jax.ShapeDtypeStruct((1, LANE), F32),
                   jax.ShapeDtypeStruct((1, LANE), F32)],
        grid=(nc,),
        in_specs=[pl.BlockSpec((SSD_CHUNK, SSM_CONV_DIM), lambda ci: (rev(ci), 0)),
                  pl.BlockSpec((SSD_CHUNK, LANE), lambda ci: (rev(ci), small_blk)),
                  par, par, par,
                  pl.BlockSpec((1, npair, LANE, LANE), lambda ci: (rev(ci), 0, 0, 0)),
                  pl.BlockSpec((SSD_CHUNK, SSM_D_INNER), lambda ci: (rev(ci), 0))],
        out_specs=[pl.BlockSpec((SSD_CHUNK, SSM_CONV_DIM), lambda ci: (rev(ci), 0)),
                   pl.BlockSpec((SSD_CHUNK, LANE), lambda ci: (rev(ci), 0)),
                   par, par, par],
        scratch_shapes=[pltpu.VMEM((npair, LANE, LANE), F32)],
        compiler_params=_cparams(("arbitrary",)),
    )(xbc, proj, dtb, alog, dsk, s_in, dy)


def gdn_scan_fwd(name, qkv, proj, alog, dtb):
    t = qkv.shape[0]
    nc, nh = t // CHUNK, GDN_HEADS
    small_blk = AL_SMALL // LANE

    def body(qkv_ref, sm_ref, alog_ref, dtb_ref, o_ref, sin_ref, x_ref, st_ref):
        ci = pl.program_id(0)

        @pl.when(ci == 0)
        def _():
            st_ref[...] = jnp.zeros_like(st_ref)

        s_in = tuple(st_ref[h] for h in range(nh))
        os, s_new, xinvs = gdn_heads(
            tuple(qkv_ref[:, h * LANE:(h + 1) * LANE] for h in range(nh)),
            tuple(qkv_ref[:, _GDN_K + h * LANE:_GDN_K + (h + 1) * LANE] for h in range(nh)),
            tuple(qkv_ref[:, _GDN_V + h * LANE:_GDN_V + (h + 1) * LANE] for h in range(nh)),
            sm_ref[...], alog_ref[...], dtb_ref[...], s_in)
        for h in range(nh):
            sin_ref[0, h] = s_in[h]
            o_ref[:, h * LANE:(h + 1) * LANE] = os[h]
            x_ref[0, h] = xinvs[h]
            st_ref[h] = s_new[h]

    par = pl.BlockSpec((1, LANE), lambda ci: (0, 0))
    return pl.pallas_call(
        body, name=name,
        out_shape=[jax.ShapeDtypeStruct((t, GDN_HEADS * GDN_HEAD), F32),
                   jax.ShapeDtypeStruct((nc, nh, LANE, LANE), F32),
                   jax.ShapeDtypeStruct((nc, nh, CHUNK, CHUNK), F32)],
        grid=(nc,),
        in_specs=[pl.BlockSpec((CHUNK, GDN_QKV_DIM), lambda ci: (ci, 0)),
                  pl.BlockSpec((CHUNK, LANE), lambda ci: (ci, small_blk)),
                  par, par],
        out_specs=[pl.BlockSpec((CHUNK, GDN_HEADS * GDN_HEAD), lambda ci: (ci, 0)),
                   pl.BlockSpec((1, nh, LANE, LANE), lambda ci: (ci, 0, 0, 0)),
                   pl.BlockSpec((1, nh, CHUNK, CHUNK), lambda ci: (ci, 0, 0, 0))],
        scratch_shapes=[pltpu.VMEM((nh, LANE, LANE), F32)],
        compiler_params=_cparams(("arbitrary",)),
    )(qkv, proj, alog, dtb)


def gdn_scan_bwd(name, qkv, proj, alog, dtb, s_in, xinv, do, dsm_in):
    t = qkv.shape[0]
    nc, nh = t // CHUNK, GDN_HEADS
    small_blk = AL_SMALL // LANE

    def body(qkv_ref, sm_ref, alog_ref, dtb_ref, sin_ref, x_ref, do_ref, dsmi_ref,
             dqkv_ref, dsm_ref, dalog_ref, ddtb_ref, dst_ref):
        ci = pl.program_id(0)

        @pl.when(ci == 0)
        def _():
            dst_ref[...] = jnp.zeros_like(dst_ref)

        xis = tuple(x_ref[0, h] for h in range(nh))

        def fn(qs, ks, vs, sm, alog_, dtb_, sts):
            os, s_new, _ = gdn_heads(qs, ks, vs, sm, alog_, dtb_, sts, xinvs=xis)
            return os, s_new

        _, vjp = jax.vjp(fn, tuple(qkv_ref[:, h * LANE:(h + 1) * LANE] for h in range(nh)),
                         tuple(qkv_ref[:, _GDN_K + h * LANE:_GDN_K + (h + 1) * LANE] for h in range(nh)),
                         tuple(qkv_ref[:, _GDN_V + h * LANE:_GDN_V + (h + 1) * LANE] for h in range(nh)),
                         sm_ref[...], alog_ref[...], dtb_ref[...], tuple(sin_ref[0, h] for h in range(nh)))
        dqs, dks, dvs, dsm, dalog, ddtb, dsts = vjp(
            (tuple(do_ref[:, h * LANE:(h + 1) * LANE] for h in range(nh)), tuple(dst_ref[h] for h in range(nh))))
        for h in range(nh):
            dqkv_ref[:, h * LANE:(h + 1) * LANE] = dqs[h]
            dqkv_ref[:, _GDN_K + h * LANE:_GDN_K + (h + 1) * LANE] = dks[h]
            dqkv_ref[:, _GDN_V + h * LANE:_GDN_V + (h + 1) * LANE] = dvs[h]
            dst_ref[h] = dsts[h]
        dsm_ref[...] = dsmi_ref[...] + dsm
        _acc(dalog_ref, dalog, ci == 0)
        _acc(ddtb_ref, ddtb, ci == 0)

    par = pl.BlockSpec((1, LANE), lambda ci: (0, 0))
    rev = lambda ci: nc - 1 - ci
    return pl.pallas_call(
        body, name=name,
        out_shape=[jax.ShapeDtypeStruct((t, GDN_QKV_DIM), F32), jax.ShapeDtypeStruct((t, LANE), F32),
                   jax.ShapeDtypeStruct((1, LANE), F32), jax.ShapeDtypeStruct((1, LANE), F32)],
        grid=(nc,),
        in_specs=[pl.BlockSpec((CHUNK, GDN_QKV_DIM), lambda ci: (rev(ci), 0)),
                  pl.BlockSpec((CHUNK, LANE), lambda ci: (rev(ci), small_blk)),
                  par, par,
                  pl.BlockSpec((1, nh, LANE, LANE), lambda ci: (rev(ci), 0, 0, 0)),
                  pl.BlockSpec((1, nh, CHUNK, CHUNK), lambda ci: (rev(ci), 0, 0, 0)),
                  pl.BlockSpec((CHUNK, GDN_HEADS * GDN_HEAD), lambda ci: (rev(ci), 0)),
                  pl.BlockSpec((CHUNK, LANE), lambda ci: (rev(ci), 0))],
        out_specs=[pl.BlockSpec((CHUNK, GDN_QKV_DIM), lambda ci: (rev(ci), 0)),
                   pl.BlockSpec((CHUNK, LANE), lambda ci: (rev(ci), 0)),
                   par, par],
        scratch_shapes=[pltpu.VMEM((nh, LANE, LANE), F32)],
        compiler_params=_cparams(("arbitrary",)),
    )(qkv, proj, alog, dtb, s_in, xinv, do, dsm_in)


def _row(arr, w, c0=0, moves=False):
    return (arr, "row", w, c0, moves)


def _par(arr, w, c0=0, moves=False):
    return (arr, "par", w, c0, moves)


def matmul_add(name, a, b, res):
    (m, k), (_, n) = a.shape, b.shape
    tm, tn, tk = _matmul_tiles(m, n, n, k, a.dtype.itemsize, b.dtype.itemsize, 4 + res.dtype.itemsize)
    nk = k // tk

    def body_acc(a_ref, b_ref, r_ref, o_ref, acc_ref):
        kk = pl.program_id(2)

        @pl.when(kk == 0)
        def _():
            acc_ref[...] = r_ref[...]

        acc_ref[...] += _dot(a_ref[...], b_ref[...])

        @pl.when(kk == nk - 1)
        def _():
            o_ref[...] = acc_ref[...]

    def body_one(a_ref, b_ref, r_ref, o_ref):
        o_ref[...] = r_ref[...] + _dot(a_ref[...], b_ref[...])

    body = body_one if nk == 1 else body_acc
    return pl.pallas_call(
        body, name=name,
        out_shape=jax.ShapeDtypeStruct((m, n), F32),
        grid=(m // tm, n // tn, nk),
        in_specs=[pl.BlockSpec((tm, tk), lambda i, j, kk: (i, kk)),
                  pl.BlockSpec((tk, tn), lambda i, j, kk: (kk, j)),
                  pl.BlockSpec((tm, tn), lambda i, j, kk: (i, j))],
        out_specs=pl.BlockSpec((tm, tn), lambda i, j, kk: (i, j)),
        scratch_shapes=[] if nk == 1 else [pltpu.VMEM((tm, tn), F32)],
        compiler_params=_cparams(("parallel", "parallel", "arbitrary")),
    )(a, b, res)


def layer_fwd(l, x, w):
    t = x.shape[0]
    rt = min(256, t)
    s = {"x": x}
    s["h"] = rowwise_fwd(f"norm_mix_l{l}", f_rmsnorm, t, rt, 1,
                         [_row(x, D_MODEL), _par(w["norm_mix_w"], D_MODEL)], [(D_MODEL, BF16)])[0]
    s["proj"] = matmul(f"in_proj_l{l}", s["h"], w["w_in"], "nn")
    s["xbc"] = conv_fwd(f"ssm_conv_l{l}", s["proj"], AL_XBC, SSM_CONV_DIM, w["ssm_conv_w"], w["ssm_conv_b"],
                        tile=min(512, t))
    s["qkv"] = conv_fwd(f"gdn_conv_l{l}", s["proj"], AL_QKV, GDN_QKV_DIM, w["gdn_conv_w"], w["gdn_conv_b"],
                        tile=min(512, t))
    s["y_scan"], s["ssd_sin"] = ssd_scan_fwd(f"ssd_scan_l{l}", s["xbc"], s["proj"], w["ssm_dt_bias"],
                                             w["ssm_a_log"], w["ssm_d"])
    s["o_scan"], s["gdn_sin"], s["gdn_x"] = gdn_scan_fwd(f"gdn_scan_l{l}", s["qkv"], s["proj"],
                                                         w["gdn_a_log"], w["gdn_dt_bias"])
    s["y_ssm"] = rowwise_fwd(f"ssd_post_l{l}", f_ssd_post, t, rt, 2,
                             [_row(s["y_scan"], 512, 0, True), _row(s["proj"], 512, AL_Z // 512, True),
                              _par(w["ssm_norm_w"], 512, 0, True)], [(512, BF16)])[0]
    s["y_gdn"] = rowwise_fwd(f"gdn_post_l{l}", f_gdn_post, t, rt, GDN_HEADS,
                             [_row(s["o_scan"], LANE, 0, True), _row(s["proj"], LANE, AL_GZ // LANE, True),
                              _par(w["gdn_norm_w"], LANE)], [(LANE, BF16)])[0]
    if "late_weights" in w:
        late = w["late_weights"](s["y_gdn"])
        w = {k: v for k, v in {**w, **late}.items() if k != "late_weights"}
    s["w"] = w
    s["p1"] = matmul(f"proj_ssm_l{l}", s["y_ssm"], w["w_proj_ssm"], "nn")
    s["p2"] = matmul(f"proj_gdn_l{l}", s["y_gdn"], w["w_proj_gdn"], "nn")
    s["merged"] = rowwise_fwd(f"merge_l{l}", f_merge, t, rt, 2,
                              [_row(s["proj"], 512, AL_GS // 512, True), _row(s["p1"], 512, 0, True),
                               _row(s["proj"], 512, AL_GG // 512, True), _row(s["p2"], 512, 0, True)],
                              [(512, BF16)])[0]
    s["x1"] = matmul_add(f"out_proj_l{l}", s["merged"], w["w_out"], x)
    s["h2"] = rowwise_fwd(f"norm_ffn_l{l}", f_rmsnorm, t, rt, 1,
                          [_row(s["x1"], D_MODEL), _par(w["norm_ffn_w"], D_MODEL)], [(D_MODEL, BF16)])[0]
    s["gu"] = matmul(f"ffn_in_l{l}", s["h2"], w["w_ffn_in"], "nn")
    s["act"] = rowwise_fwd(f"swiglu_l{l}", f_swiglu, t, rt, FFN_HIDDEN // 256,
                           [_row(s["gu"], 256, 0, True), _row(s["gu"], 256, FFN_HIDDEN // 256, True)],
                           [(256, BF16)])[0]
    x2 = matmul_add(f"ffn_down_l{l}", s["act"], w["w_ffn_down"], s["x1"])
    return x2, s


IN_SHARD = IN_DIM // 4
IN_SHARD_PAD = 2304


def _aligned_to_shards(g):
    orig = jnp.concatenate([g[:, 0:2560], g[:, AL_SMALL:AL_SMALL + 16], g[:, 2560:6656],
                            g[:, AL_SMALL + 16:AL_SMALL + 32], g[:, 6656:8704]], axis=1)
    return jnp.stack([jnp.pad(orig[:, j * IN_SHARD:(j + 1) * IN_SHARD], ((0, 0), (0, IN_SHARD_PAD - IN_SHARD)))
                      for j in range(N_CHIPS)])


def layer_bwd(l, dx2, w, s):
    t = dx2.shape[0]
    rt = min(256, t)
    ct = min(512, t)
    g = {}
    dact = matmul(f"ffn_down_dx_l{l}", dx2, w["w_ffn_down"], "nt")
    g["w_ffn_down"] = matmul(f"ffn_down_dw_l{l}", s["act"], dx2, "tn")
    nf = FFN_HIDDEN // 256
    dgate, dup = rowwise_bwd(f"swiglu_bwd_l{l}", f_swiglu, t, rt, nf,
                             [_row(s["gu"], 256, 0, True), _row(s["gu"], 256, nf, True)], [True, True],
                             [_row(dact, 256, 0, True)], row_dtypes={0: BF16, 1: BF16})
    dgu = jnp.concatenate([dgate, dup], axis=1)
    dh2 = matmul(f"ffn_in_dx_l{l}", dgu, w["w_ffn_in"], "nt")
    g["w_ffn_in"] = matmul(f"ffn_in_dw_l{l}", s["h2"], dgu, "tn", chip_major=True)
    dx1, g["norm_ffn_w"] = rowwise_bwd(f"norm_ffn_bwd_l{l}", f_rmsnorm, t, rt, 1,
                                       [_row(s["x1"], D_MODEL), _par(w["norm_ffn_w"], D_MODEL)], [True, True],
                                       [_row(dh2, D_MODEL)], addends={0: _row(dx2, D_MODEL)})
    dmerged = matmul(f"out_proj_dx_l{l}", dx1, w["w_out"], "nt")
    g["w_out"] = matmul(f"out_proj_dw_l{l}", s["merged"], dx1, "tn")
    dgs, dp1, dgg, dp2 = rowwise_bwd(
        f"merge_bwd_l{l}", f_merge, t, rt, 2,
        [_row(s["proj"], 512, AL_GS // 512, True), _row(s["p1"], 512, 0, True),
         _row(s["proj"], 512, AL_GG // 512, True), _row(s["p2"], 512, 0, True)], [True] * 4,
        [_row(dmerged, 512, 0, True)], row_dtypes={0: BF16, 1: BF16, 2: BF16, 3: BF16})
    dy_ssm = matmul(f"proj_ssm_dx_l{l}", dp1, w["w_proj_ssm"], "nt")
    g["w_proj_ssm"] = matmul(f"proj_ssm_dw_l{l}", s["y_ssm"], dp1, "tn")
    dy_gdn = matmul(f"proj_gdn_dx_l{l}", dp2, w["w_proj_gdn"], "nt")
    g["w_proj_gdn"] = matmul(f"proj_gdn_dw_l{l}", s["y_gdn"], dp2, "tn")
    dy_scan, dz, g["ssm_norm_w"] = rowwise_bwd(
        f"ssd_post_bwd_l{l}", f_ssd_post, t, rt, 2,
        [_row(s["y_scan"], 512, 0, True), _row(s["proj"], 512, AL_Z // 512, True),
         _par(w["ssm_norm_w"], 512, 0, True)], [True] * 3, [_row(dy_ssm, 512, 0, True)], row_dtypes={1: BF16})
    dxbc_act, dsm, g["ssm_dt_bias"], g["ssm_a_log"], g["ssm_d"] = ssd_scan_bwd(
        f"ssd_scan_bwd_l{l}", s["xbc"], s["proj"], w["ssm_dt_bias"], w["ssm_a_log"], w["ssm_d"],
        s["ssd_sin"], dy_scan)
    dxbc, g["ssm_conv_w"], g["ssm_conv_b"] = conv_bwd(
        f"ssm_conv_bwd_l{l}", s["proj"], AL_XBC, SSM_CONV_DIM, w["ssm_conv_w"], w["ssm_conv_b"], dxbc_act, tile=ct)
    do_scan, dgz, g["gdn_norm_w"] = rowwise_bwd(
        f"gdn_post_bwd_l{l}", f_gdn_post, t, rt, GDN_HEADS,
        [_row(s["o_scan"], LANE, 0, True), _row(s["proj"], LANE, AL_GZ // LANE, True),
         _par(w["gdn_norm_w"], LANE)], [True] * 3, [_row(dy_gdn, LANE, 0, True)], row_dtypes={1: BF16})
    dqkv_act, dsm, g["gdn_a_log"], g["gdn_dt_bias"] = gdn_scan_bwd(
        f"gdn_scan_bwd_l{l}", s["qkv"], s["proj"], w["gdn_a_log"], w["gdn_dt_bias"], s["gdn_sin"],
        s["gdn_x"], do_scan, dsm)
    dqkv, g["gdn_conv_w"], _ = conv_bwd(
        f"gdn_conv_bwd_l{l}", s["proj"], AL_QKV, GDN_QKV_DIM, w["gdn_conv_w"], w["gdn_conv_b"], dqkv_act, tile=ct)
    dproj = jnp.concatenate([dz, dxbc, dqkv, dgz, dgs, dgg, dsm.astype(BF16),
                             jnp.zeros((t, AL_DIM - AL_SMALL - LANE), BF16)], axis=1)
    dh = matmul(f"in_proj_dx_l{l}", dproj, w["w_in"], "nt")
    g["w_in"] = matmul(f"in_proj_dw_l{l}", s["h"], dproj, "tn")
    dx0, g["norm_mix_w"] = rowwise_bwd(f"norm_mix_bwd_l{l}", f_rmsnorm, t, rt, 1,
                                       [_row(s["x"], D_MODEL), _par(w["norm_mix_w"], D_MODEL)], [True, True],
                                       [_row(dh, D_MODEL)], addends={0: _row(dx1, D_MODEL)})
    return dx0, g


def _align_w_in(w):
    pad = jnp.zeros((w.shape[0], AL_DIM - AL_SMALL - 32), w.dtype)
    return jnp.concatenate([w[:, 0:2560], w[:, 2576:6672], w[:, 6688:8736],
                            w[:, 2560:2576], w[:, 6672:6688], pad], axis=1)


def _pad_lane(v, at=0):
    return jnp.pad(v[None], ((0, 0), (at, LANE - at - v.shape[0])))


def local_step(x, target, full, gathered_of=None, on_layer_grads=None):
    if gathered_of is None:
        gathered_of = lambda l, after: {n: full[n][l] for n, _ in SHARDED}
    ws, saved = [], []
    h = x
    for l in range(DEPTH):
        gw = gathered_of(l, h)
        ws.append({
            "norm_mix_w": full["norm_mix_w"][l][None], "w_in": _align_w_in(gw["w_in"]),
            "ssm_conv_w": gw["ssm_conv_w"], "ssm_conv_b": full["ssm_conv_b"][l][None],
            "ssm_dt_bias": _pad_lane(full["ssm_dt_bias"][l]), "ssm_a_log": _pad_lane(full["ssm_a_log"][l]),
            "ssm_d": _pad_lane(full["ssm_d"][l]), "ssm_norm_w": full["ssm_norm_w"][l][None],
            "gdn_conv_w": gw["gdn_conv_w"], "gdn_conv_b": jnp.zeros((1, GDN_QKV_DIM), F32),
            "gdn_a_log": _pad_lane(full["gdn_a_log"][l], SM_A),
            "gdn_dt_bias": _pad_lane(full["gdn_dt_bias"][l], SM_A),
            "gdn_norm_w": full["gdn_norm_w"][l][None],
            "norm_ffn_w": full["norm_ffn_w"][l][None],
            **{n: gw[n] for n in ("w_proj_ssm", "w_proj_gdn", "w_out", "w_ffn_in", "w_ffn_down", "late_weights")
               if n in gw},
        })
        h, s = layer_fwd(l, h, ws[l])
        ws[l] = s.pop("w")
        saved.append(s)
    loss, dx, g_final = final_loss("final_loss", h, target, full["final_norm_w"][None], tile=min(256, x.shape[0]))
    per_layer = [None] * DEPTH
    matmul_grads = [None] * DEPTH
    for l in reversed(range(DEPTH)):
        dx, per_layer[l] = layer_bwd(l, dx, ws[l], saved[l])
        matmul_grads[l] = {n: per_layer[l].pop(n) for n, _ in BIG}
        if on_layer_grads is not None:
            dx = on_layer_grads(l, matmul_grads[l], dx)
    grads = {"final_norm_w": g_final[0]}
    if on_layer_grads is None:
        grads.update({n: jnp.stack([matmul_grads[l][n] for l in range(DEPTH)]) for n, _ in BIG})
    for name in per_layer[0]:
        rows = []
        for l in range(DEPTH):
            gl = per_layer[l][name]
            if name in ("ssm_dt_bias", "ssm_a_log", "ssm_d"):
                gl = gl[0, :SSM_HEADS]
            elif name in ("gdn_a_log", "gdn_dt_bias"):
                gl = gl[0, SM_A:SM_A + GDN_HEADS]
            elif name in ("norm_mix_w", "ssm_conv_b", "ssm_norm_w", "gdn_norm_w", "norm_ffn_w"):
                gl = gl[0]
            rows.append(gl)
        grads[name] = jnp.stack(rows)
    return loss, dx, grads


MESH = pl.DeviceIdType.MESH
HBM = pl.BlockSpec(memory_space=pltpu.HBM)
N_DEV = 8


def _pos():
    return lax.axis_index("x"), lax.axis_index("y"), lax.axis_index("c")


def _rcopy(src, dst, send_sem, recv_sem, dev):
    return pltpu.make_async_remote_copy(src_ref=src, dst_ref=dst, send_sem=send_sem, recv_sem=recv_sem,
                                        device_id=dev, device_id_type=MESH)


RELATIONS = (2, 1, 3)


def _related_chip(x, y, mask):
    return (1 - x if mask & 2 else x, 1 - y if mask & 1 else y)


def weights_gather(name, bufs):
    n = len(bufs)

    def body(*refs):
        outs, send_sems, recv_sems = refs[n:2 * n], refs[2 * n], refs[2 * n + 1]
        x, y, c = _pos()
        sib = (x, y, 1 - c)
        sends = []
        for i, a in enumerate(outs):
            for k, m in enumerate(RELATIONS):
                px, py = _related_chip(x, y, m)
                cp = _rcopy(a.at[0, c], a.at[m, c], send_sems.at[6 * i + k], recv_sems.at[6 * i + k], (px, py, c))
                cp.start()
                sends.append(cp)
        for i, a in enumerate(outs):
            for k, m in enumerate(RELATIONS):
                px, py = _related_chip(x, y, m)
                _rcopy(a.at[0, c], a.at[m, c], send_sems.at[6 * i + k], recv_sems.at[6 * i + k],
                       (px, py, c)).wait_recv()
                fw = _rcopy(a.at[m, c], a.at[m, c], send_sems.at[6 * i + 3 + k], recv_sems.at[6 * i + 3 + k], sib)
                fw.start()
                sends.append(fw)
        for i, a in enumerate(outs):
            for k, m in enumerate(RELATIONS):
                _rcopy(a.at[m, 1 - c], a.at[m, 1 - c], send_sems.at[6 * i + 3 + k], recv_sems.at[6 * i + 3 + k],
                       sib).wait_recv()
        for cp in sends:
            cp.wait_send()

    return pl.pallas_call(
        body, name=name, out_shape=[jax.ShapeDtypeStruct(b.shape, b.dtype) for b in bufs],
        in_specs=[HBM] * n, out_specs=[HBM] * n,
        input_output_aliases={i: i for i in range(n)},
        scratch_shapes=[pltpu.SemaphoreType.DMA((6 * n,)), pltpu.SemaphoreType.DMA((6 * n,))],
    )(*bufs)


SEM = pl.BlockSpec(memory_space=pltpu.SEMAPHORE)
DATAFLOW = pltpu.SideEffectType.DATAFLOW_SIDE_EFFECTING


def gather_start(name, bufs, after):
    n = len(bufs)

    def body(*refs):
        ins = refs[:n]
        send_sems, recv_sems, token = refs[n + 1], refs[n + 2], refs[2 * n + 3]
        x, y, c = _pos()
        for i, a in enumerate(ins):
            for k, m in enumerate(RELATIONS):
                px, py = _related_chip(x, y, m)
                _rcopy(a.at[0, c], a.at[m, c], send_sems.at[3 * i + k], recv_sems.at[3 * i + k], (px, py, c)).start()
        token[...] = jnp.zeros_like(token)

    res = pl.pallas_call(
        body, name=name,
        out_shape=(pltpu.SemaphoreType.DMA((3 * n,)), pltpu.SemaphoreType.DMA((3 * n,)),
                   *[pltpu.HBM(b.shape, b.dtype) for b in bufs], jax.ShapeDtypeStruct((8, LANE), F32)),
        in_specs=[HBM] * n + [pl.BlockSpec(memory_space=pl.ANY)],
        out_specs=(SEM, SEM, *[HBM] * n, pl.BlockSpec(memory_space=pltpu.VMEM)),
        input_output_aliases={i: 2 + i for i in range(n)},
        compiler_params=pltpu.CompilerParams(has_side_effects=DATAFLOW),
    )(*[pltpu.with_memory_space_constraint(b, pltpu.HBM) for b in bufs], after)
    return res[0], res[1], list(res[2:2 + n]), res[2 + n]


def gather_wait(name, send_sems, recv_sems, bufs, after):
    n = len(bufs)

    def body(*refs):
        ins, ssem, rsem = refs[:n], refs[n], refs[n + 1]
        x, y, c = _pos()
        for i, a in enumerate(ins):
            for k, m in enumerate(RELATIONS):
                px, py = _related_chip(x, y, m)
                cp = _rcopy(a.at[0, c], a.at[m, c], ssem.at[3 * i + k], rsem.at[3 * i + k], (px, py, c))
                cp.wait_send()
                cp.wait_recv()

    return pl.pallas_call(
        body, name=name,
        out_shape=[pltpu.HBM(b.shape, b.dtype) for b in bufs],
        in_specs=[HBM] * n + [SEM, SEM, pl.BlockSpec(memory_space=pl.ANY)],
        out_specs=[HBM] * n,
        input_output_aliases={i: i for i in range(n)},
        compiler_params=pltpu.CompilerParams(has_side_effects=DATAFLOW),
    )(*bufs, send_sems, recv_sems, after)


def weights_forward(name, bufs):
    n = len(bufs)

    def body(*refs):
        outs, send_sems, recv_sems = refs[n:2 * n], refs[2 * n], refs[2 * n + 1]
        x, y, c = _pos()
        sib = (x, y, 1 - c)
        sends = []
        for i, a in enumerate(outs):
            for k, m in enumerate(RELATIONS):
                fw = _rcopy(a.at[m, c], a.at[m, c], send_sems.at[3 * i + k], recv_sems.at[3 * i + k], sib)
                fw.start()
                sends.append(fw)
        for i, a in enumerate(outs):
            for k, m in enumerate(RELATIONS):
                _rcopy(a.at[m, 1 - c], a.at[m, 1 - c], send_sems.at[3 * i + k], recv_sems.at[3 * i + k],
                       sib).wait_recv()
        for cp in sends:
            cp.wait_send()

    return pl.pallas_call(
        body, name=name, out_shape=[jax.ShapeDtypeStruct(b.shape, b.dtype) for b in bufs],
        in_specs=[HBM] * n, out_specs=[HBM] * n,
        input_output_aliases={i: i for i in range(n)},
        scratch_shapes=[pltpu.SemaphoreType.DMA((3 * n,)), pltpu.SemaphoreType.DMA((3 * n,))],
    )(*bufs)


def pair_swap(name, gs):
    n = len(gs)
    offs = [0]
    for g in gs:
        offs.append(offs[-1] + g.shape[0])

    def body(*refs):
        srcs, outs, send_sems, recv_sems = refs[:n], refs[n:2 * n], refs[2 * n], refs[2 * n + 1]
        x, y, c = _pos()
        cps = [_rcopy(s.at[j, 1 - c], o.at[j], send_sems.at[offs[i] + j], recv_sems.at[offs[i] + j], (x, y, 1 - c))
               for i, (s, o) in enumerate(zip(srcs, outs)) for j in range(s.shape[0])]
        for cp in cps:
            cp.start()
        for cp in cps:
            cp.wait()

    return pl.pallas_call(
        body, name=name, out_shape=[jax.ShapeDtypeStruct(g.shape[:1] + g.shape[2:], g.dtype) for g in gs],
        in_specs=[HBM] * n, out_specs=[HBM] * n,
        scratch_shapes=[pltpu.SemaphoreType.DMA((offs[-1],)), pltpu.SemaphoreType.DMA((offs[-1],))],
    )(*gs)


def scatter_start(name, ss, after):
    n = len(ss)
    lands = [lax.empty((3,) + s.shape[1:], s.dtype) for s in ss]

    def body(*refs):
        srcs, dsts = refs[:n], refs[n:2 * n]
        send_sems, recv_sems, token = refs[2 * n + 1], refs[2 * n + 2], refs[4 * n + 3]
        x, y, c = _pos()
        for i, (s, o) in enumerate(zip(srcs, dsts)):
            for k, m in enumerate(RELATIONS):
                px, py = _related_chip(x, y, m)
                _rcopy(s.at[2 * px + py], o.at[k], send_sems.at[3 * i + k], recv_sems.at[3 * i + k],
                       (px, py, c)).start()
        token[...] = jnp.zeros_like(token)

    both = list(ss) + lands
    res = pl.pallas_call(
        body, name=name,
        out_shape=(pltpu.SemaphoreType.DMA((3 * n,)), pltpu.SemaphoreType.DMA((3 * n,)),
                   *[pltpu.HBM(b.shape, b.dtype) for b in both], jax.ShapeDtypeStruct((8, LANE), F32)),
        in_specs=[HBM] * (2 * n) + [pl.BlockSpec(memory_space=pl.ANY)],
        out_specs=(SEM, SEM, *[HBM] * (2 * n), pl.BlockSpec(memory_space=pltpu.VMEM)),
        input_output_aliases={i: 2 + i for i in range(2 * n)},
        compiler_params=pltpu.CompilerParams(has_side_effects=DATAFLOW),
    )(*[pltpu.with_memory_space_constraint(b, pltpu.HBM) for b in both], after)
    return res[0], res[1], list(res[2:2 + n]), list(res[2 + n:2 + 2 * n]), res[2 + 2 * n]


def scatter_wait(name, send_sems, recv_sems, ss, lands, after):
    n = len(ss)

    def body(*refs):
        srcs, dsts, ssem, rsem = refs[:n], refs[n:2 * n], refs[2 * n], refs[2 * n + 1]
        x, y, c = _pos()
        for i, (s, o) in enumerate(zip(srcs, dsts)):
            for k, m in enumerate(RELATIONS):
                px, py = _related_chip(x, y, m)
                cp = _rcopy(s.at[2 * px + py], o.at[k], ssem.at[3 * i + k], rsem.at[3 * i + k], (px, py, c))
                cp.wait_send()
                cp.wait_recv()

    both = list(ss) + list(lands)
    res = pl.pallas_call(
        body, name=name,
        out_shape=[pltpu.HBM(b.shape, b.dtype) for b in both],
        in_specs=[HBM] * (2 * n) + [SEM, SEM, pl.BlockSpec(memory_space=pl.ANY)],
        out_specs=[HBM] * (2 * n),
        input_output_aliases={i: i for i in range(2 * n)},
        compiler_params=pltpu.CompilerParams(has_side_effects=DATAFLOW),
    )(*both, send_sems, recv_sems, after)
    return list(res[:n]), list(res[n:])


def pair_share(name, bufs):
    n = len(bufs)

    def body(*refs):
        outs, send_sems, recv_sems = refs[n:2 * n], refs[2 * n], refs[2 * n + 1]
        x, y, c = _pos()
        sends = []
        for i, o in enumerate(outs):
            cp = _rcopy(o.at[c], o.at[c], send_sems.at[i], recv_sems.at[i], (x, y, 1 - c))
            cp.start()
            sends.append(cp)
        for i, o in enumerate(outs):
            _rcopy(o.at[1 - c], o.at[1 - c], send_sems.at[i], recv_sems.at[i], (x, y, 1 - c)).wait_recv()
        for cp in sends:
            cp.wait_send()

    return pl.pallas_call(
        body, name=name, out_shape=[jax.ShapeDtypeStruct(b.shape, b.dtype) for b in bufs],
        in_specs=[HBM] * n, out_specs=[HBM] * n,
        input_output_aliases={i: i for i in range(n)},
        scratch_shapes=[pltpu.SemaphoreType.DMA((n,)), pltpu.SemaphoreType.DMA((n,))],
    )(*bufs)


def all_allgather(name, buf):
    r, cd = buf.shape

    def body(src, out, send_sems, recv_sems, lsem):
        x, y, c = _pos()
        me = 4 * x + 2 * y + c
        local = pltpu.make_async_copy(src, out.at[me], lsem)
        local.start()

        def peer(mask):
            px = 1 - x if mask & 4 else x
            py = 1 - y if mask & 2 else y
            pc = 1 - c if mask & 1 else c
            return px, py, pc

        sends = []
        for mask in range(1, N_DEV):
            cp = _rcopy(src, out.at[me], send_sems.at[mask - 1], recv_sems.at[mask - 1], peer(mask))
            cp.start()
            sends.append(cp)
        for mask in range(1, N_DEV):
            px, py, pc = peer(mask)
            _rcopy(src, out.at[4 * px + 2 * py + pc], send_sems.at[mask - 1], recv_sems.at[mask - 1],
                   (px, py, pc)).wait_recv()
        for cp in sends:
            cp.wait_send()
        local.wait()

    return pl.pallas_call(
        body, name=name, out_shape=jax.ShapeDtypeStruct((N_DEV, r, cd), buf.dtype),
        in_specs=[HBM], out_specs=HBM,
        scratch_shapes=[pltpu.SemaphoreType.DMA((N_DEV - 1,)), pltpu.SemaphoreType.DMA((N_DEV - 1,)),
                        pltpu.SemaphoreType.DMA(())],
    )(buf)


ELEMENTWISE_BLOCK_BYTES = 2 << 20


def _row_block(rows, cols):
    for cand in (1024, 512, 256, 128, 64, 32, 16):
        if rows % cand == 0 and cand * cols * 4 <= ELEMENTWISE_BLOCK_BYTES:
            return cand
    return rows


def chip_sum(name, s, r, me, c):
    _, a, b = s.shape
    tr = _row_block(a, b)

    def body(idx_ref, s_ref, r_ref, o_ref):
        del idx_ref
        acc = s_ref[...].astype(F32)
        for k in range(3):
            acc = acc + r_ref[k].astype(F32)
        o_ref[...] = acc

    return pl.pallas_call(
        body, name=name, out_shape=jax.ShapeDtypeStruct((2, a, b), F32),
        grid_spec=pltpu.PrefetchScalarGridSpec(
            num_scalar_prefetch=1, grid=(a // tr,),
            in_specs=[pl.BlockSpec((None, tr, b), lambda i, idx: (idx[0], i, 0)),
                      pl.BlockSpec((3, tr, b), lambda i, idx: (0, i, 0))],
            out_specs=pl.BlockSpec((None, tr, b), lambda i, idx: (idx[1], i, 0))),
        compiler_params=_cparams(("arbitrary",)),
    )(jnp.stack([me, c]).astype(jnp.int32), s, r)


def pair_add(name, p, recv, c):
    nj, _, rh, cd = p.shape
    tr = _row_block(rh, cd)

    def body(c_ref, p_ref, r_ref, o_ref):
        del c_ref
        o_ref[...] = (p_ref[0] + r_ref[...]).astype(o_ref.dtype)

    return pl.pallas_call(
        body, name=name, out_shape=jax.ShapeDtypeStruct((nj, rh, cd), BF16),
        grid_spec=pltpu.PrefetchScalarGridSpec(
            num_scalar_prefetch=1, grid=(nj, rh // tr),
            in_specs=[pl.BlockSpec((1, 1, tr, cd), lambda j, i, c_ref: (j, c_ref[0], i, 0)),
                      pl.BlockSpec((1, tr, cd), lambda j, i, c_ref: (j, i, 0))],
            out_specs=pl.BlockSpec((1, tr, cd), lambda j, i, c_ref: (j, i, 0))),
        compiler_params=_cparams(("arbitrary", "arbitrary")),
    )(jnp.reshape(c, (1,)).astype(jnp.int32), p, recv)


def slab_sum(name, a):
    n, r, cd = a.shape
    tr = _pick(r, (256, 128, 64, 32, 16, 8))

    def body(a_ref, o_ref):
        acc = a_ref[0].astype(F32)
        for j in range(1, n):
            acc = acc + a_ref[j].astype(F32)
        o_ref[...] = acc

    return pl.pallas_call(
        body, name=name, out_shape=jax.ShapeDtypeStruct((r, cd), F32),
        grid=(r // tr,),
        in_specs=[pl.BlockSpec((n, tr, cd), lambda i: (0, i, 0))],
        out_specs=pl.BlockSpec((tr, cd), lambda i: (i, 0)),
        compiler_params=_cparams(("arbitrary",)),
    )(a)


ADAM_C1 = 1.0 - ADAM_B1 ** ADAM_STEP
ADAM_C2 = 1.0 - ADAM_B2 ** ADAM_STEP


def adamw(name, w, g, m, v):
    r, cd = w.shape
    tr = r
    for cand in (512, 256, 128, 64, 32, 16, 8):
        if r % cand == 0 and cand * cd * 4 <= (1 << 20):
            tr = cand
            break

    def body(w_ref, g_ref, m_ref, v_ref, d_ref, nm_ref, nv_ref):
        gv = g_ref[...]
        nm = ADAM_B1 * m_ref[...] + (1.0 - ADAM_B1) * gv
        nv = ADAM_B2 * v_ref[...] + (1.0 - ADAM_B2) * (gv * gv)
        m_hat = nm / ADAM_C1
        v_hat = nv / ADAM_C2
        d_ref[...] = -ADAM_LR * (m_hat / (jnp.sqrt(v_hat) + ADAM_EPS) + ADAM_WD * w_ref[...])
        nm_ref[...] = nm
        nv_ref[...] = nv

    spec = pl.BlockSpec((tr, cd), lambda i: (i, 0))
    sd = jax.ShapeDtypeStruct((r, cd), F32)
    return pl.pallas_call(
        body, name=name, out_shape=[sd, sd, sd], grid=(r // tr,),
        in_specs=[spec] * 4, out_specs=[spec] * 3,
        compiler_params=_cparams(("arbitrary",)),
    )(w, g, m, v)


WEIGHTS = ("norm_mix_w", "w_in", "ssm_conv_w", "ssm_conv_b", "ssm_dt_bias", "ssm_a_log", "ssm_d", "ssm_norm_w",
           "gdn_conv_w", "gdn_a_log", "gdn_dt_bias", "gdn_norm_w", "w_proj_ssm", "w_proj_gdn", "w_out",
           "norm_ffn_w", "w_ffn_in", "w_ffn_down", "final_norm_w")
BIG = (("w_in", 2), ("w_proj_ssm", 1), ("w_proj_gdn", 1), ("w_out", 1), ("w_ffn_in", 2), ("w_ffn_down", 1))
CONVW = (("ssm_conv_w", 2), ("gdn_conv_w", 2))
SHARDED = BIG + CONVW
SMALL = tuple(n for n in WEIGHTS if n not in dict(SHARDED))


def _unpack(buf, shapes, lead=()):
    flat = buf.reshape(lead + (-1,))
    out, o = [], 0
    for shp in shapes:
        n = math.prod(shp)
        out.append(flat[..., o:o + n].reshape(lead + tuple(shp)))
        o += n
    return out


def _pack_rows(arrs, lead=(), mult=8):
    nl = len(lead)
    flat = jnp.concatenate([a.reshape(lead + (-1,)) for a in arrs], axis=nl)
    n = flat.shape[nl]
    rows = -(-n // (mult * LANE)) * mult
    flat = jnp.pad(flat, [(0, 0)] * nl + [(0, rows * LANE - n)])
    return flat.reshape(lead + (rows, LANE))


def _slot_buffer(shard):
    return lax.dynamic_update_slice(lax.empty((N_CHIPS,) + shard.shape, shard.dtype), shard[None],
                                    (0,) * (shard.ndim + 1))


def kernel(x, norm_mix_w, w_in, ssm_conv_w, ssm_conv_b, ssm_dt_bias, ssm_a_log, ssm_d, ssm_norm_w, gdn_conv_w, gdn_a_log, gdn_dt_bias, gdn_norm_w, w_proj_ssm, w_proj_gdn, w_out, norm_ffn_w, w_ffn_in, w_ffn_down, final_norm_w, loss_target, m_norm_mix_w, m_w_in, m_ssm_conv_w, m_ssm_conv_b, m_ssm_dt_bias, m_ssm_a_log, m_ssm_d, m_ssm_norm_w, m_gdn_conv_w, m_gdn_a_log, m_gdn_dt_bias, m_gdn_norm_w, m_w_proj_ssm, m_w_proj_gdn, m_w_out, m_norm_ffn_w, m_w_ffn_in, m_w_ffn_down, m_final_norm_w, v_norm_mix_w, v_w_in, v_ssm_conv_w, v_ssm_conv_b, v_ssm_dt_bias, v_ssm_a_log, v_ssm_d, v_ssm_norm_w, v_gdn_conv_w, v_gdn_a_log, v_gdn_dt_bias, v_gdn_norm_w, v_w_proj_ssm, v_w_proj_gdn, v_w_out, v_norm_ffn_w, v_w_ffn_in, v_w_ffn_down, v_final_norm_w):
    wl = (norm_mix_w, w_in, ssm_conv_w, ssm_conv_b, ssm_dt_bias, ssm_a_log, ssm_d, ssm_norm_w, gdn_conv_w,
          gdn_a_log, gdn_dt_bias, gdn_norm_w, w_proj_ssm, w_proj_gdn, w_out, norm_ffn_w, w_ffn_in, w_ffn_down,
          final_norm_w)
    ml = (m_norm_mix_w, m_w_in, m_ssm_conv_w, m_ssm_conv_b, m_ssm_dt_bias, m_ssm_a_log, m_ssm_d, m_ssm_norm_w,
          m_gdn_conv_w, m_gdn_a_log, m_gdn_dt_bias, m_gdn_norm_w, m_w_proj_ssm, m_w_proj_gdn, m_w_out,
          m_norm_ffn_w, m_w_ffn_in, m_w_ffn_down, m_final_norm_w)
    vl = (v_norm_mix_w, v_w_in, v_ssm_conv_w, v_ssm_conv_b, v_ssm_dt_bias, v_ssm_a_log, v_ssm_d, v_ssm_norm_w,
          v_gdn_conv_w, v_gdn_a_log, v_gdn_dt_bias, v_gdn_norm_w, v_w_proj_ssm, v_w_proj_gdn, v_w_out,
          v_norm_ffn_w, v_w_ffn_in, v_w_ffn_down, v_final_norm_w)
    w = dict(zip(WEIGHTS, wl))
    m = dict(zip(WEIGHTS, ml))
    v = dict(zip(WEIGHTS, vl))
    x_pos, y_pos, c = _pos()
    me = 2 * x_pos + y_pos
    big = [n for n, _ in BIG]

    shards = [w[n].astype(BF16) for n in big]
    shards[0] = jnp.pad(shards[0], ((0, 0), (0, 0), (0, IN_SHARD_PAD - IN_SHARD)))
    conv_shapes = [w[n].shape[1:] for n, _ in CONVW]
    conv_pack = _pack_rows([w[n] for n, _ in CONVW], lead=(DEPTH,), mult=16)

    def slot_buffers(l):
        return [_slot_buffer(s[l].reshape((2, s.shape[1] // 2) + s.shape[2:])) for s in shards + [conv_pack]]

    def assemble(bufs, which):
        out = {}
        for g_, i in zip(bufs, which):
            by_chip = [lax.dynamic_index_in_dim(g_, jnp.bitwise_xor(me, j), 0, keepdims=False)
                       .reshape((-1,) + g_.shape[3:]) for j in range(N_CHIPS)]
            if i < 0:
                conv_parts = [_unpack(p, conv_shapes) for p in by_chip]
                for k, (n, axis) in enumerate(CONVW):
                    out[n] = jnp.concatenate([conv_parts[j][k] for j in range(N_CHIPS)], axis=axis - 1)
            else:
                n, axis = BIG[i]
                cols = IN_SHARD if n == "w_in" else by_chip[0].shape[-1]
                out[n] = jnp.concatenate([p[:, :cols] for p in by_chip], axis=axis - 1)
        return out

    everything = list(range(len(BIG))) + [-1]
    first, rest = [0, -1], everything[1:-1]
    bufs0 = slot_buffers(0)
    landed_first = weights_gather("gather_w_l0_first", [bufs0[i] for i in first])
    rest_sems = gather_start("gather_w_l0_rest_start", [bufs0[i] for i in rest], landed_first[0])
    l1_sems = gather_start("gather_w_l1_start", slot_buffers(1), rest_sems[2][0])

    def land(name, flying, after):
        send_sems, recv_sems, bufs, _ = flying
        return weights_forward(f"{name}_forward", gather_wait(f"{name}_wait", send_sems, recv_sems, bufs, after))

    def gathered_of(l, after):
        if l == 0:
            return {**assemble(landed_first, first),
                    "late_weights": lambda later: assemble(land("gather_w_l0_rest", rest_sems, later), rest)}
        return assemble(land("gather_w_l1", l1_sems, after), everything)

    token = rest_sems[3] + l1_sems[3]

    in_flight = {}

    def start_reduction(l, g_layer, dx):
        halves = [g_layer[n].reshape((1, 2, -1, AL_DIM)) if n == "w_in"
                  else g_layer[n].reshape((N_CHIPS, 2, -1) + g_layer[n].shape[-1:]) for n in big]
        from_pair = pair_swap(f"grad_pair_swap_l{l}", halves)
        chip_part = [pair_add(f"grad_pair_add_{n}_l{l}", g_, r_, c) for n, g_, r_ in zip(big, halves, from_pair)]
        chip_part[0] = _aligned_to_shards(chip_part[0][0])
        *in_flight[l], token_l = scatter_start(f"grad_scatter_start_l{l}", chip_part, dx)
        return dx + token_l[0, 0]

    def finish_reduction(l, after):
        parts, landed = scatter_wait(f"grad_scatter_wait_l{l}", *in_flight[l], after)
        sums = [chip_sum(f"grad_chip_sum_{n}_l{l}", s_, r_, me, c) for n, s_, r_ in zip(big, parts, landed)]
        return [r_.reshape((-1,) + r_.shape[2:]) for r_ in pair_share(f"grad_pair_share_l{l}", sums)]

    full = {n: w[n] for n in SMALL}
    loss_part, grad_x, grads = local_step(x[0] + token[0, 0], loss_target[0], full, gathered_of, start_reduction)
    reduced = [None] * DEPTH
    reduced[1] = finish_reduction(1, grad_x)

    small_names = list(SMALL) + [n for n, _ in CONVW]
    small_all = all_allgather("gather_small", _pack_rows([grads[n] for n in small_names] + [loss_part[0, :1]]))
    small_sum = slab_sum("small_sum", small_all)
    small_vals = _unpack(small_sum, [grads[n].shape for n in small_names] + [(1,)])
    g_small = dict(zip(small_names, small_vals[:-1]))
    loss = small_vals[-1].reshape(())
    out_g, out_d, out_m, out_v = {}, {}, {}, {}
    d_, m_, v_ = adamw("adamw_small", *[_pack_rows([d[n] for n in SMALL]) for d in (w, g_small, m, v)])
    small_shapes = [w[n].shape for n in SMALL]
    for n, dd, mm, vv in zip(SMALL, _unpack(d_, small_shapes), _unpack(m_, small_shapes), _unpack(v_, small_shapes)):
        out_g[n], out_d[n], out_m[n], out_v[n] = g_small[n], dd, mm, vv

    reduced[0] = finish_reduction(0, d_[:1] + reduced[1][0][:1, :LANE])
    g_sharded = {n: jnp.stack([reduced[l][i] for l in range(DEPTH)]) for i, n in enumerate(big)}
    g_sharded["w_in"] = g_sharded["w_in"][:, :, :IN_SHARD]
    for n, axis in CONVW:
        size = w[n].shape[axis]
        g_sharded[n] = lax.dynamic_slice_in_dim(g_small.pop(n), me * size, size, axis=axis)

    for n, _ in SHARDED:
        shp = w[n].shape
        two = lambda a: a.reshape(-1, shp[-1])
        d_, m_, v_ = adamw(f"adamw_{n}", two(w[n]), two(g_sharded[n]), two(m[n]), two(v[n]))
        out_g[n], out_d[n], out_m[n], out_v[n] = g_sharded[n], d_.reshape(shp), m_.reshape(shp), v_.reshape(shp)

    return (loss, grad_x[None], *[out_g[n] for n in WEIGHTS], *[out_d[n] for n in WEIGHTS],
            *[out_m[n] for n in WEIGHTS], *[out_v[n] for n in WEIGHTS])
```

```python
import math

import jax
import jax.numpy as jnp
from jax import lax
from jax.experimental import pallas as pl
from jax.experimental.pallas import tpu as pltpu

F32 = jnp.float32
BF16 = jnp.bfloat16

D_MODEL = 1024
DEPTH = 2
SSM_HEADS = 16
SSM_HEAD_DIM = 64
SSM_D_INNER = 1024
SSM_STATE = 128
SSM_CONV_DIM = 1536
GDN_HEADS = 8
GDN_HEAD = 128
GDN_QKV_DIM = 3072
CONV_K = 4
CHUNK = 128
SSD_CHUNK = 256
FFN_HIDDEN = 2816
EPS = 1e-6
IN_DIM = 8736

ADAM_LR = 0.001
ADAM_B1 = 0.9
ADAM_B2 = 0.999
ADAM_EPS = 1e-08
ADAM_WD = 0.01
ADAM_STEP = 10

LANE = 128
NEG_BIG = -1e30
VMEM_LIMIT = 56 * 1024 * 1024

AL_Z, AL_XBC, AL_QKV, AL_GZ, AL_GS, AL_GG, AL_SMALL, AL_DIM = 0, 1024, 2560, 5632, 6656, 7680, 8704, 9216
SM_DT, SM_A, SM_B = 0, 16, 24

HI = lax.Precision.HIGHEST
NN = (((1,), (0,)), ((), ()))
NT = (((1,), (1,)), ((), ()))
TN = (((0,), (0,)), ((), ()))


def _cparams(sem):
    return pltpu.CompilerParams(dimension_semantics=sem, vmem_limit_bytes=VMEM_LIMIT)


def _pick(n, prefs):
    for p in prefs:
        if n % p == 0:
            return p
    return n


MATMUL_VMEM_BUDGET = 40 << 20
MXU_WIDTH = 256
HBM_BYTES_PER_S = 3.3e12
MXU_FLOPS_PER_S = 9.0e14
GRID_STEP_S = 0.35e-6
N_CHIPS = 4


def _matmul_tiles(m, n, n_dom, k, a_item, b_item, o_item):
    best = None
    def cands(dim, cap):
        return [c for c in range(LANE, min(dim, cap) + 1, LANE) if dim % c == 0] or [dim]

    tms, tns, tks = cands(m, 2048), cands(n_dom, 2304), cands(k, 1 << 30)
    for tm in tms:
        for tn in tns:
            for tk in tks:
                nk = k // tk
                vmem = (2 * (tm * tk * a_item + tk * tn * b_item + tm * tn * o_item) + tm * tn * 4 * (2 if nk > 1 else 1)
                        + (tm * tk * 2 if a_item > 2 else 0) + (tk * tn * 2 if b_item > 2 else 0))
                if vmem > MATMUL_VMEM_BUDGET:
                    continue
                traffic = m * k * a_item * (1 if nk == 1 else n // tn) + k * n * b_item * (m // tm)
                mxu_fill = tn / (-(-tn // MXU_WIDTH) * MXU_WIDTH)
                cost = (max(traffic / HBM_BYTES_PER_S, 2.0 * m * n * k / (MXU_FLOPS_PER_S * mxu_fill))
                        + (m // tm) * (n // tn) * nk * GRID_STEP_S)
                if best is None or cost < best[0]:
                    best = (cost, (tm, tn, tk))
    return best[1]


def _dot(a, b, dims=NN):
    return lax.dot_general(a.astype(BF16), b.astype(BF16), dims, preferred_element_type=F32)


def _dot3(a, b, dims=NN):
    a_hi, b_hi = a.astype(BF16), b.astype(BF16)
    a_lo = (a - a_hi.astype(F32)).astype(BF16)
    b_lo = (b - b_hi.astype(F32)).astype(BF16)

    def dg(u, v):
        return lax.dot_general(u, v, dims, preferred_element_type=F32)

    return dg(a_hi, b_hi) + (dg(a_hi, b_lo) + dg(a_lo, b_hi))


def _dot_hi(a, b, dims=NN):
    return lax.dot_general(a, b, dims, precision=HI, preferred_element_type=F32)


def _sigmoid(x):
    return jax.nn.sigmoid(x)


def _silu(x):
    return x * _sigmoid(x)


def _softplus(x):
    return jnp.maximum(x, 0.0) + jnp.log1p(jnp.exp(-jnp.abs(x)))


def matmul(name, a, b, mode, out_dtype=F32, chip_major=False, stack=None):
    if mode == "nn":
        (m, k), (k2, n) = a.shape, b.shape
    elif mode == "nt":
        (m, k), (n, k2) = a.shape, b.shape
    else:
        (k, m), (k2, n) = a.shape, b.shape
    assert k == k2, (a.shape, b.shape, mode)
    shard = n // N_CHIPS if chip_major else n
    tm, tn, tk = _matmul_tiles(m, n, shard, k, a.dtype.itemsize, b.dtype.itemsize, jnp.dtype(out_dtype).itemsize)
    if chip_major:
        per = shard // tn
        base_shape, base_blk = (N_CHIPS, m, shard), (None, tm, tn)
        base_idx = lambda i, j: (j // per, i, j % per)
    else:
        base_shape, base_blk = (m, n), (tm, tn)
        base_idx = lambda i, j: (i, j)
    nk = k // tk
    dims = {"nn": NN, "nt": NT, "tn": TN}[mode]

    def body_acc(a_ref, b_ref, o_ref, acc_ref):
        kk = pl.program_id(2)

        @pl.when(kk == 0)
        def _():
            acc_ref[...] = jnp.zeros_like(acc_ref)

        acc_ref[...] += _dot(a_ref[...], b_ref[...], dims)

        @pl.when(kk == nk - 1)
        def _():
            o_ref[...] = acc_ref[...].astype(o_ref.dtype)

    def body_one(a_ref, b_ref, o_ref):
        o_ref[...] = _dot(a_ref[...], b_ref[...], dims).astype(o_ref.dtype)

    compute = body_one if nk == 1 else body_acc
    if mode == "tn":
        a_spec = pl.BlockSpec((tk, tm), lambda i, j, kk: (kk, i))
    else:
        a_spec = pl.BlockSpec((tm, tk), lambda i, j, kk: (i, kk))
    if mode == "nt":
        b_spec = pl.BlockSpec((tn, tk), lambda i, j, kk: (j, kk))
    else:
        b_spec = pl.BlockSpec((tk, tn), lambda i, j, kk: (kk, j))
    in_specs, operands, aliases, body = [a_spec, b_spec], [a, b], {}, compute
    if stack is None:
        out_shape, out_blk, out_idx = base_shape, base_blk, (lambda i, j, kk: base_idx(i, j))
    else:
        layer, buf = stack
        out_shape, out_blk = (DEPTH,) + base_shape, (None,) + base_blk
        out_idx = lambda i, j, kk: (layer,) + base_idx(i, j)
        if buf is not None:
            assert buf.shape == out_shape and buf.dtype == out_dtype
            in_specs.append(pl.BlockSpec(memory_space=pl.ANY))
            operands.append(buf)
            aliases = {2: 0}

            def body(a_ref, b_ref, buf_ref, *rest):
                del buf_ref
                compute(a_ref, b_ref, *rest)

    return pl.pallas_call(
        body, name=name,
        out_shape=jax.ShapeDtypeStruct(out_shape, out_dtype),
        grid=(m // tm, n // tn, nk),
        in_specs=in_specs,
        out_specs=pl.BlockSpec(out_blk, out_idx),
        scratch_shapes=[] if nk == 1 else [pltpu.VMEM((tm, tn), F32)],
        input_output_aliases=aliases,
        compiler_params=_cparams(("parallel", "parallel", "arbitrary")),
    )(*operands)


def _row_map(c0, moves):
    return (lambda j, i: (i, c0 + j)) if moves else (lambda j, i: (i, c0))


def _par_map(c0, moves):
    return (lambda j, i: (0, c0 + j)) if moves else (lambda j, i: (0, c0))


def _in_spec(op, tile):
    _, kind, w, c0, moves = op
    if kind == "row":
        return pl.BlockSpec((tile, w), _row_map(c0, moves))
    return pl.BlockSpec((1, w), _par_map(c0, moves))


ROW_BLOCK_ELEMS = 1 << 18


def _row_tile(t, tile, ops):
    width = max(op[2] for op in ops if op[1] == "row")
    return min(t, max(tile, ROW_BLOCK_ELEMS // width))


def rowwise_fwd(name, fn, t, tile, ncol, ins, outs):
    n_in = len(ins)
    tile = _row_tile(t, tile, ins)

    def body(*refs):
        vals = [r[...].astype(F32) for r in refs[:n_in]]
        res = fn(*vals)
        if not isinstance(res, (tuple, list)):
            res = (res,)
        for r, v in zip(refs[n_in:], res):
            r[...] = v.astype(r.dtype)

    res = pl.pallas_call(
        body, name=name,
        out_shape=[jax.ShapeDtypeStruct((t, w * ncol), dt) for w, dt in outs],
        grid=(ncol, t // tile),
        in_specs=[_in_spec(op, tile) for op in ins],
        out_specs=[pl.BlockSpec((tile, w), _row_map(0, True)) for w, _ in outs],
        compiler_params=_cparams(("arbitrary", "arbitrary")),
    )(*[op[0] for op in ins])
    return res


def rowwise_bwd(name, fn, t, tile, ncol, ins, need, cts, addends=None, row_dtypes=None):
    n_in, n_ct = len(ins), len(cts)
    tile = _row_tile(t, tile, ins)
    addends = addends or {}
    row_dtypes = row_dtypes or {}
    didx = [i for i in range(n_in) if need[i]]
    add_ops = [addends[i] for i in didx if i in addends]
    n_add = len(add_ops)

    def body(*refs):
        in_refs = refs[:n_in]
        ct_refs = refs[n_in:n_in + n_ct]
        add_refs = refs[n_in + n_ct:n_in + n_ct + n_add]
        out_refs = refs[n_in + n_ct + n_add:]
        vals = [r[...].astype(F32) for r in in_refs]

        def g(*dv):
            full = list(vals)
            for i, v in zip(didx, dv):
                full[i] = v
            res = fn(*full)
            return tuple(res) if isinstance(res, (tuple, list)) else (res,)

        _, vjp = jax.vjp(g, *[vals[i] for i in didx])
        grads = vjp(tuple(c[...].astype(F32) for c in ct_refs))
        j, i = pl.program_id(0), pl.program_id(1)
        a = 0
        for o_ref, gv, idx in zip(out_refs, grads, didx):
            _, kind, _, _, moves = ins[idx]
            if kind == "row":
                if idx in addends:
                    gv = gv + add_refs[a][...].astype(F32)
                    a += 1
                o_ref[...] = gv.astype(o_ref.dtype)
            else:
                first = (i == 0) if moves else jnp.logical_and(i == 0, j == 0)

                @pl.when(first)
                def _(o_ref=o_ref, gv=gv):
                    o_ref[...] = gv

                @pl.when(jnp.logical_not(first))
                def _(o_ref=o_ref, gv=gv):
                    o_ref[...] += gv

    out_shape, out_specs = [], []
    for idx in didx:
        _, kind, w, _, moves = ins[idx]
        cols = w * (ncol if moves else 1)
        if kind == "row":
            out_shape.append(jax.ShapeDtypeStruct((t, cols), row_dtypes.get(idx, F32)))
            out_specs.append(pl.BlockSpec((tile, w), _row_map(0, moves)))
        else:
            out_shape.append(jax.ShapeDtypeStruct((1, cols), F32))
            out_specs.append(pl.BlockSpec((1, w), _par_map(0, moves)))
    ops = list(ins) + list(cts) + add_ops
    res = pl.pallas_call(
        body, name=name,
        out_shape=out_shape,
        grid=(ncol, t // tile),
        in_specs=[_in_spec(op, tile) for op in ops],
        out_specs=out_specs,
        compiler_params=_cparams(("arbitrary", "arbitrary")),
    )(*[op[0] for op in ops])
    return res


def f_rmsnorm(x, w):
    return x * lax.rsqrt(jnp.mean(x * x, axis=-1, keepdims=True) + EPS) * w


def f_ssd_post(y, z, w):
    y = y * _silu(z)
    return y * lax.rsqrt(jnp.mean(y * y, axis=-1, keepdims=True) + EPS) * w


def f_gdn_post(o, z, w):
    o = o * lax.rsqrt(jnp.mean(o * o, axis=-1, keepdims=True) + EPS) * w
    return o * _silu(z)


def f_merge(gs, p1, gg, p2):
    return _sigmoid(gs) * p1 + _sigmoid(gg) * p2


def f_swiglu(g, u):
    return _silu(g) * u


def final_loss(name, x, tgt, w, tile=256):
    t, d = x.shape

    def body(x_ref, t_ref, w_ref, loss_ref, dx_ref, dw_ref):
        i = pl.program_id(0)
        xv, tv, wv = x_ref[...], t_ref[...], w_ref[...]

        def g(xx, ww):
            err = f_rmsnorm(xx, ww) - tv
            return 0.5 * jnp.sum(jnp.mean(err * err, axis=-1, keepdims=True), axis=0, keepdims=True)

        val, vjp = jax.vjp(g, xv, wv)
        dx, dw = vjp(jnp.ones((1, 1), F32))
        dx_ref[...] = dx
        lv = jnp.broadcast_to(val, (1, LANE))

        @pl.when(i == 0)
        def _():
            loss_ref[...] = lv
            dw_ref[...] = dw

        @pl.when(i != 0)
        def _():
            loss_ref[...] += lv
            dw_ref[...] += dw

    return pl.pallas_call(
        body, name=name,
        out_shape=[jax.ShapeDtypeStruct((1, LANE), F32), jax.ShapeDtypeStruct((t, d), F32),
                   jax.ShapeDtypeStruct((1, d), F32)],
        grid=(t // tile,),
        in_specs=[pl.BlockSpec((tile, d), lambda i: (i, 0)), pl.BlockSpec((tile, d), lambda i: (i, 0)),
                  pl.BlockSpec((1, d), lambda i: (0, 0))],
        out_specs=[pl.BlockSpec((1, LANE), lambda i: (0, 0)), pl.BlockSpec((tile, d), lambda i: (i, 0)),
                   pl.BlockSpec((1, d), lambda i: (0, 0))],
        compiler_params=_cparams(("arbitrary",)),
    )(x, tgt, w)


CONV_W = 512
HALO = 8
STRIPS = 16


def _rows_back(before, cur, d):
    rows = lax.broadcasted_iota(jnp.int32, cur.shape, 0)
    return jnp.where(rows < d, pltpu.roll(before, d, 0), pltpu.roll(cur, d, 0))


def _rows_ahead(cur, after, d):
    rows = lax.broadcasted_iota(jnp.int32, cur.shape, 0)
    return jnp.where(rows < HALO - d, pltpu.roll(cur, HALO - d, 0), pltpu.roll(after, HALO - d, 0))


def _conv_taps(taps, bias, before, cur):
    shifted = [_rows_back(before, cur, CONV_K - 1 - k) for k in range(CONV_K - 1)] + [cur]
    pre = bias + taps[CONV_K - 1] * cur
    for k in range(CONV_K - 1):
        pre = pre + taps[k] * shifted[k]
    return pre, shifted


def conv_fwd(name, src, c0, width, w, b, tile=512):
    t = src.shape[0]
    ncol, nrow = width // CONV_W, t // tile
    cb0 = c0 // CONV_W
    hb = tile // HALO

    def body(prev_ref, cur_ref, w_ref, b_ref, o_ref):
        i = pl.program_id(1)
        taps = [w_ref[k:k + 1, :] for k in range(CONV_K)]
        bias = b_ref[...]

        def strips(g, before):
            for u in range(STRIPS):
                r0 = pl.multiple_of((g * STRIPS + u) * HALO, HALO)
                cur = cur_ref[pl.ds(r0, HALO), :]
                pre, _ = _conv_taps(taps, bias, before, cur)
                o_ref[pl.ds(r0, HALO), :] = _silu(pre)
                before = cur
            return before

        lax.fori_loop(0, tile // (HALO * STRIPS), strips, jnp.where(i == 0, 0.0, prev_ref[...]))

    return pl.pallas_call(
        body, name=name,
        out_shape=jax.ShapeDtypeStruct((t, width), F32),
        grid=(ncol, nrow),
        in_specs=[pl.BlockSpec((HALO, CONV_W), lambda j, i: (jnp.maximum(i * hb - 1, 0), cb0 + j)),
                  pl.BlockSpec((tile, CONV_W), lambda j, i: (i, cb0 + j)),
                  pl.BlockSpec((CONV_K, CONV_W), lambda j, i: (0, j)),
                  pl.BlockSpec((1, CONV_W), lambda j, i: (0, j))],
        out_specs=pl.BlockSpec((tile, CONV_W), lambda j, i: (i, j)),
        compiler_params=_cparams(("arbitrary", "arbitrary")),
    )(src, src, w, b)


def conv_bwd(name, src, c0, width, w, b, dy, tile=512):
    t = src.shape[0]
    ncol, nrow = width // CONV_W, t // tile
    cb0 = c0 // CONV_W
    hb = tile // HALO
    last_hb = t // HALO - 1
    nstrip = tile // HALO

    def body(sprev_ref, scur_ref, snext_ref, w_ref, b_ref, dycur_ref, dynext_ref,
             du_ref, dw_ref, db_ref, dpre_ref):
        i = pl.program_id(1)
        taps = [w_ref[k:k + 1, :] for k in range(CONV_K)]
        bias = b_ref[...]

        def dpre_of(before, cur, dy_strip):
            pre, shifted = _conv_taps(taps, bias, before, cur)
            s = _sigmoid(pre)
            return dy_strip * (s * (1.0 + pre * (1.0 - s))), shifted

        def strips1(g, carry):
            before, dws, dbs = carry
            for u in range(STRIPS):
                r0 = pl.multiple_of((g * STRIPS + u) * HALO, HALO)
                cur = scur_ref[pl.ds(r0, HALO), :]
                dpre, shifted = dpre_of(before, cur, dycur_ref[pl.ds(r0, HALO), :])
                dpre_ref[pl.ds(r0, HALO), :] = dpre
                before, dws, dbs = cur, tuple(a + dpre * v for a, v in zip(dws, shifted)), dbs + dpre
            return before, dws, dbs

        zero = jnp.zeros((HALO, CONV_W), F32)
        before, dws, dbs = lax.fori_loop(0, nstrip // STRIPS, strips1,
                                         (jnp.where(i == 0, 0.0, sprev_ref[...]), (zero,) * CONV_K, zero))
        dpre_next, _ = dpre_of(before, snext_ref[...], dynext_ref[...])
        dpre_ref[pl.ds(tile, HALO), :] = jnp.where(i == nrow - 1, 0.0, dpre_next)

        def strips2(g, _):
            parts = []
            for u in range(STRIPS):
                r0 = pl.multiple_of((g * STRIPS + u) * HALO, HALO)
                cur = dpre_ref[pl.ds(r0, HALO), :]
                after = dpre_ref[pl.ds(r0 + HALO, HALO), :]
                acc = taps[CONV_K - 1] * cur
                for d in range(1, CONV_K):
                    acc = acc + taps[CONV_K - 1 - d] * _rows_ahead(cur, after, d)
                parts.append(acc)
            r0 = pl.multiple_of(g * STRIPS * HALO, STRIPS * HALO)
            du_ref[pl.ds(r0, STRIPS * HALO), :] = jnp.concatenate(parts, axis=0).astype(du_ref.dtype)
            return 0

        lax.fori_loop(0, nstrip // STRIPS, strips2, 0)
        dw_tile = jnp.concatenate([jnp.sum(a, axis=0, keepdims=True) for a in dws], axis=0)
        db_tile = jnp.sum(dbs, axis=0, keepdims=True)
        _acc(dw_ref, dw_tile, i == 0)
        _acc(db_ref, db_tile, i == 0)

    return pl.pallas_call(
        body, name=name,
        out_shape=[jax.ShapeDtypeStruct((t, width), BF16), jax.ShapeDtypeStruct((CONV_K, width), F32),
                   jax.ShapeDtypeStruct((1, width), F32)],
        grid=(ncol, nrow),
        in_specs=[pl.BlockSpec((HALO, CONV_W), lambda j, i: (jnp.maximum(i * hb - 1, 0), cb0 + j)),
                  pl.BlockSpec((tile, CONV_W), lambda j, i: (i, cb0 + j)),
                  pl.BlockSpec((HALO, CONV_W), lambda j, i: (jnp.minimum((i + 1) * hb, last_hb), cb0 + j)),
                  pl.BlockSpec((CONV_K, CONV_W), lambda j, i: (0, j)),
                  pl.BlockSpec((1, CONV_W), lambda j, i: (0, j)),
                  pl.BlockSpec((tile, CONV_W), lambda j, i: (i, j)),
                  pl.BlockSpec((HALO, CONV_W), lambda j, i: (jnp.minimum((i + 1) * hb, last_hb), j))],
        out_specs=[pl.BlockSpec((tile, CONV_W), lambda j, i: (i, j)),
                   pl.BlockSpec((CONV_K, CONV_W), lambda j, i: (0, j)),
                   pl.BlockSpec((1, CONV_W), lambda j, i: (0, j))],
        scratch_shapes=[pltpu.VMEM((tile + HALO, CONV_W), F32)],
        compiler_params=_cparams(("arbitrary", "arbitrary")),
    )(src, src, src, w, b, dy, dy)


def _iota2(q):
    return (lax.broadcasted_iota(jnp.int32, (q, q), 0), lax.broadcasted_iota(jnp.int32, (q, q), 1))


def _lane_pick(blk, idx):
    lane = lax.broadcasted_iota(jnp.int32, (1, LANE), 1)
    return jnp.sum(jnp.where(lane == idx, blk, 0.0), axis=1, keepdims=True)


class _Decay:
    def __init__(self, a):
        q = a.shape[0]
        r, c = _iota2(q)
        self.r, self.c = r, c
        self.cum = _dot_hi((c <= r).astype(F32), a)
        self.cum_t = _dot_hi(a, (r <= c).astype(F32), TN)
        self.tot = self.cum[q - 1:q, :]
        self.e_cum = jnp.exp(self.cum)
        self.e_rest = jnp.exp(self.tot - self.cum)
        self.e_tot = jnp.exp(self.tot)

    def mask(self, lane):
        rows = lax.broadcasted_iota(jnp.int32, (LANE, 1), 0)
        cum_row = jnp.sum(jnp.where(rows == lane, self.cum_t, 0.0), axis=0, keepdims=True)
        return jnp.exp(jnp.where(self.r >= self.c, _lane_pick(self.cum, lane) - cum_row, NEG_BIG))


_SSD_B = SSM_D_INNER
_SSD_C = SSM_D_INNER + 2 * SSM_STATE


def _interleave(gens):
    results = [None] * len(gens)
    live = list(range(len(gens)))
    while live:
        for i in list(live):
            try:
                next(gens[i])
            except StopIteration as stop:
                results[i] = stop.value
                live.remove(i)
    return results


def ssd_chunk(xs, bm, cm, dt_all, dsk, dec, state, p, cb):
    lane = lax.broadcasted_iota(jnp.int32, (1, LANE), 1)
    m0 = lane < SSM_HEAD_DIM
    h0, h1 = 2 * p, 2 * p + 1

    def both(blk):
        return jnp.where(m0, _lane_pick(blk, h0), _lane_pick(blk, h1))

    xdt = xs * both(dt_all)
    l0, l1 = dec.mask(h0), dec.mask(h1)
    yield
    y_diag = _dot(cb * l0, jnp.where(m0, xdt, 0.0)) + _dot(cb * l1, jnp.where(m0, 0.0, xdt))
    y_off = _dot(cm, state, NT) * both(dec.e_cum)
    yield
    rowm = lax.broadcasted_iota(jnp.int32, (LANE, 1), 0) < SSM_HEAD_DIM
    new_state = (state * jnp.where(rowm, _lane_pick(dec.e_tot, h0), _lane_pick(dec.e_tot, h1))
                 + _dot(xdt * both(dec.e_rest), bm, TN))
    y = y_diag + y_off + both(dsk) * xs
    return y, new_state


def ssd_pairs(xs, bms, cms, small, dtb, alog, dsk, states):
    dt_all = _softplus(small + dtb)
    dec = _Decay(dt_all * (-jnp.exp(alog)))
    cbs = [_dot(cm, bm, NT) for cm, bm in zip(cms, bms)]
    res = _interleave([ssd_chunk(x, bms[p // 4], cms[p // 4], dt_all, dsk, dec, st, p, cbs[p // 4])
                       for p, (x, st) in enumerate(zip(xs, states))])
    return tuple(y for y, _ in res), tuple(s for _, s in res)


def tri_inverse(a):
    q = a.shape[0]
    r, c = _iota2(q)
    eye = (r == c).astype(F32)
    diag = (r // 16) == (c // 16)
    bd = jnp.where(diag, a, 0.0)
    off = jnp.where(diag, 0.0, a)
    b2 = _dot3(bd, bd)
    d1 = _dot3(eye - bd, eye + b2)
    yield
    b4 = _dot3(b2, b2)
    yield
    b8 = _dot3(b4, b4)
    d2 = _dot3(d1, eye + b4)
    yield
    dinv = _dot3(d2, eye + b8)
    yield
    n = _dot3(dinv, off)
    yield
    powers = [n]
    while 16 * 2 ** len(powers) < q:
        powers.append(_dot3(powers[-1], powers[-1]))
        yield
    m = dinv
    for pw in reversed(powers[1:]):
        m = _dot3(eye + pw, m)
        yield
    return _dot3(eye - n, m)


@jax.custom_vjp
def _solve_with(xinv, a, rhs):
    del a
    return _dot3(xinv, rhs)


def _solve_with_fwd(xinv, a, rhs):
    t = _dot3(xinv, rhs)
    return t, (xinv, t)


def _solve_with_bwd(res, dt):
    xinv, t = res
    d_rhs = _dot3(xinv, dt, TN)
    d_a = -_dot(d_rhs, t, NT)
    return jnp.zeros_like(xinv), d_a, d_rhs


_solve_with.defvjp(_solve_with_fwd, _solve_with_bwd)


_GDN_K = GDN_HEADS * GDN_HEAD
_GDN_V = 2 * GDN_HEADS * GDN_HEAD


def gdn_chunk(qh, kh, vh, beta_all, dec, state, h, xinv=None):
    r, c = dec.r, dec.c
    qn = qh * lax.rsqrt(jnp.sum(qh * qh, axis=-1, keepdims=True) + EPS) * (GDN_HEAD ** -0.5)
    kn = kh * lax.rsqrt(jnp.sum(kh * kh, axis=-1, keepdims=True) + EPS)
    beta = _lane_pick(beta_all, SM_B + h)
    decay = dec.mask(SM_A + h)
    yield
    kk = _dot(kn, kn, NT)
    qk = _dot(qn, kn, NT) * decay
    amat = jnp.where(r > c, kk * decay * beta, 0.0)
    eg = _lane_pick(dec.e_cum, SM_A + h)
    rhs = jnp.concatenate([vh * beta, kn * (beta * eg)], axis=1)
    qs = _dot(qn * eg, state)
    yield
    if xinv is None:
        xinv = yield from tri_inverse(amat)
        t = _dot3(xinv, rhs)
    else:
        t = _solve_with(xinv, amat, rhs)
    yield
    u, w = t[:, :GDN_HEAD], t[:, GDN_HEAD:]
    v_new = u - _dot(w, state)
    yield
    o = qs + _dot(qk, v_new)
    new_state = (state * _lane_pick(dec.e_tot, SM_A + h)
                 + _dot(kn * _lane_pick(dec.e_rest, SM_A + h), v_new, TN))
    return o, new_state, xinv


def gdn_heads(qs, ks, vs, small, alog, dtb, states, xinvs=None):
    nh = len(qs)
    beta_all = _sigmoid(small)
    dec = _Decay(-jnp.exp(alog) * _softplus(small + dtb))
    res = _interleave([gdn_chunk(qs[h], ks[h], vs[h], beta_all, dec, states[h], h,
                                 None if xinvs is None else xinvs[h]) for h in range(nh)])
    return tuple(o for o, _, _ in res), tuple(s for _, s, _ in res), tuple(x for _, _, x in res)


def _acc(ref, val, first):
    @pl.when(first)
    def _():
        ref[...] = val

    @pl.when(jnp.logical_not(first))
    def _():
        ref[...] += val


def ssd_scan_fwd(name, xbc, proj, dtb, alog, dsk):
    t = xbc.shape[0]
    nc, npair = t // SSD_CHUNK, SSM_HEADS // 2
    small_blk = AL_SMALL // LANE

    def body(xbc_ref, sm_ref, dtb_ref, alog_ref, dsk_ref, y_ref, sin_ref, st_ref):
        ci = pl.program_id(0)

        @pl.when(ci == 0)
        def _():
            st_ref[...] = jnp.zeros_like(st_ref)

        s_in = tuple(st_ref[p] for p in range(npair))
        ys, s_new = ssd_pairs(tuple(xbc_ref[:, p * LANE:(p + 1) * LANE] for p in range(npair)),
                              tuple(xbc_ref[:, _SSD_B + g * LANE:_SSD_B + (g + 1) * LANE] for g in range(2)),
                              tuple(xbc_ref[:, _SSD_C + g * LANE:_SSD_C + (g + 1) * LANE] for g in range(2)),
                              sm_ref[...], dtb_ref[...], alog_ref[...], dsk_ref[...], s_in)
        for p in range(npair):
            sin_ref[0, p] = s_in[p]
            y_ref[:, p * LANE:(p + 1) * LANE] = ys[p]
            st_ref[p] = s_new[p]

    par = pl.BlockSpec((1, LANE), lambda ci: (0, 0))
    return pl.pallas_call(
        body, name=name,
        out_shape=[jax.ShapeDtypeStruct((t, SSM_D_INNER), F32),
                   jax.ShapeDtypeStruct((nc, npair, LANE, LANE), F32)],
        grid=(nc,),
        in_specs=[pl.BlockSpec((SSD_CHUNK, SSM_CONV_DIM), lambda ci: (ci, 0)),
                  pl.BlockSpec((SSD_CHUNK, LANE), lambda ci: (ci, small_blk)),
                  par, par, par],
        out_specs=[pl.BlockSpec((SSD_CHUNK, SSM_D_INNER), lambda ci: (ci, 0)),
                   pl.BlockSpec((1, npair, LANE, LANE), lambda ci: (ci, 0, 0, 0))],
        scratch_shapes=[pltpu.VMEM((npair, LANE, LANE), F32)],
        compiler_params=_cparams(("arbitrary",)),
    )(xbc, proj, dtb, alog, dsk)


def ssd_scan_bwd(name, xbc, proj, dtb, alog, dsk, s_in, dy):
    t = xbc.shape[0]
    nc, npair = t // SSD_CHUNK, SSM_HEADS // 2
    small_blk = AL_SMALL // LANE

    def body(xbc_ref, sm_ref, dtb_ref, alog_ref, dsk_ref, sin_ref, dy_ref,
             dxbc_ref, dsm_ref, ddtb_ref, dalog_ref, ddsk_ref, dst_ref):
        ci = pl.program_id(0)

        @pl.when(ci == 0)
        def _():
            dst_ref[...] = jnp.zeros_like(dst_ref)

        _, vjp = jax.vjp(ssd_pairs, tuple(xbc_ref[:, p * LANE:(p + 1) * LANE] for p in range(npair)),
                         tuple(xbc_ref[:, _SSD_B + g * LANE:_SSD_B + (g + 1) * LANE] for g in range(2)),
                         tuple(xbc_ref[:, _SSD_C + g * LANE:_SSD_C + (g + 1) * LANE] for g in range(2)),
                         sm_ref[...], dtb_ref[...], alog_ref[...], dsk_ref[...],
                         tuple(sin_ref[0, p] for p in range(npair)))
        dxs, dbms, dcms, dsm, ddtb, dalog, ddsk, dsts = vjp(
            (tuple(dy_ref[:, p * LANE:(p + 1) * LANE] for p in range(npair)),
             tuple(dst_ref[p] for p in range(npair))))
        for p in range(npair):
            dxbc_ref[:, p * LANE:(p + 1) * LANE] = dxs[p]
            dst_ref[p] = dsts[p]
        for g in range(2):
            dxbc_ref[:, _SSD_B + g * LANE:_SSD_B + (g + 1) * LANE] = dbms[g]
            dxbc_ref[:, _SSD_C + g * LANE:_SSD_C + (g + 1) * LANE] = dcms[g]
        dsm_ref[...] = dsm
        _acc(ddtb_ref, ddtb, ci == 0)
        _acc(dalog_ref, dalog, ci == 0)
        _acc(ddsk_ref, ddsk, ci == 0)

    par = pl.BlockSpec((1, LANE), lambda ci: (0, 0))
    rev = lambda ci: nc - 1 - ci
    return pl.pallas_call(
        body, name=name,
        out_shape=[jax.ShapeDtypeStruct((t, SSM_CONV_DIM), F32),
                   jax.ShapeDtypeStruct((t, LANE), F32),
                   jax.ShapeDtypeStruct((1, LANE), F32), jax.ShapeDtypeStruct((1, LANE), F32),
                   jax.ShapeDtypeStruct((1, LANE), F32)],
        grid=(nc,),
        in_specs=[pl.BlockSpec((SSD_CHUNK, SSM_CONV_DIM), lambda ci: (rev(ci), 0)),
                  pl.BlockSpec((SSD_CHUNK, LANE), lambda ci: (rev(ci), small_blk)),
                  par, par, par,
                  pl.BlockSpec((1, npair, LANE, LANE), lambda ci: (rev(ci), 0, 0, 0)),
                  pl.BlockSpec((SSD_CHUNK, SSM_D_INNER), lambda ci: (rev(ci), 0))],
        out_specs=[pl.BlockSpec((SSD_CHUNK, SSM_CONV_DIM), lambda ci: (rev(ci), 0)),
                   pl.BlockSpec((SSD_CHUNK, LANE), lambda ci: (rev(ci), 0)),
                   par, par, par],
        scratch_shapes=[pltpu.VMEM((npair, LANE, LANE), F32)],
        compiler_params=_cparams(("arbitrary",)),
    )(xbc, proj, dtb, alog, dsk, s_in, dy)


def gdn_scan_fwd(name, qkv, proj, alog, dtb):
    t = qkv.shape[0]
    nc, nh = t // CHUNK, GDN_HEADS
    small_blk = AL_SMALL // LANE

    def body(qkv_ref, sm_ref, alog_ref, dtb_ref, o_ref, sin_ref, x_ref, st_ref):
        ci = pl.program_id(0)

        @pl.when(ci == 0)
        def _():
            st_ref[...] = jnp.zeros_like(st_ref)

        s_in = tuple(st_ref[h] for h in range(nh))
        os, s_new, xinvs = gdn_heads(
            tuple(qkv_ref[:, h * LANE:(h + 1) * LANE] for h in range(nh)),
            tuple(qkv_ref[:, _GDN_K + h * LANE:_GDN_K + (h + 1) * LANE] for h in range(nh)),
            tuple(qkv_ref[:, _GDN_V + h * LANE:_GDN_V + (h + 1) * LANE] for h in range(nh)),
            sm_ref[...], alog_ref[...], dtb_ref[...], s_in)
        for h in range(nh):
            sin_ref[0, h] = s_in[h]
            o_ref[:, h * LANE:(h + 1) * LANE] = os[h]
            x_ref[0, h] = xinvs[h]
            st_ref[h] = s_new[h]

    par = pl.BlockSpec((1, LANE), lambda ci: (0, 0))
    return pl.pallas_call(
        body, name=name,
        out_shape=[jax.ShapeDtypeStruct((t, GDN_HEADS * GDN_HEAD), F32),
                   jax.ShapeDtypeStruct((nc, nh, LANE, LANE), F32),
                   jax.ShapeDtypeStruct((nc, nh, CHUNK, CHUNK), F32)],
        grid=(nc,),
        in_specs=[pl.BlockSpec((CHUNK, GDN_QKV_DIM), lambda ci: (ci, 0)),
                  pl.BlockSpec((CHUNK, LANE), lambda ci: (ci, small_blk)),
                  par, par],
        out_specs=[pl.BlockSpec((CHUNK, GDN_HEADS * GDN_HEAD), lambda ci: (ci, 0)),
                   pl.BlockSpec((1, nh, LANE, LANE), lambda ci: (ci, 0, 0, 0)),
                   pl.BlockSpec((1, nh, CHUNK, CHUNK), lambda ci: (ci, 0, 0, 0))],
        scratch_shapes=[pltpu.VMEM((nh, LANE, LANE), F32)],
        compiler_params=_cparams(("arbitrary",)),
    )(qkv, proj, alog, dtb)


def gdn_scan_bwd(name, qkv, proj, alog, dtb, s_in, xinv, do, dsm_in):
    t = qkv.shape[0]
    nc, nh = t // CHUNK, GDN_HEADS
    small_blk = AL_SMALL // LANE

    def body(qkv_ref, sm_ref, alog_ref, dtb_ref, sin_ref, x_ref, do_ref, dsmi_ref,
             dqkv_ref, dsm_ref, dalog_ref, ddtb_ref, dst_ref):
        ci = pl.program_id(0)

        @pl.when(ci == 0)
        def _():
            dst_ref[...] = jnp.zeros_like(dst_ref)

        xis = tuple(x_ref[0, h] for h in range(nh))

        def fn(qs, ks, vs, sm, alog_, dtb_, sts):
            os, s_new, _ = gdn_heads(qs, ks, vs, sm, alog_, dtb_, sts, xinvs=xis)
            return os, s_new

        _, vjp = jax.vjp(fn, tuple(qkv_ref[:, h * LANE:(h + 1) * LANE] for h in range(nh)),
                         tuple(qkv_ref[:, _GDN_K + h * LANE:_GDN_K + (h + 1) * LANE] for h in range(nh)),
                         tuple(qkv_ref[:, _GDN_V + h * LANE:_GDN_V + (h + 1) * LANE] for h in range(nh)),
                         sm_ref[...], alog_ref[...], dtb_ref[...], tuple(sin_ref[0, h] for h in range(nh)))
        dqs, dks, dvs, dsm, dalog, ddtb, dsts = vjp(
            (tuple(do_ref[:, h * LANE:(h + 1) * LANE] for h in range(nh)), tuple(dst_ref[h] for h in range(nh))))
        for h in range(nh):
            dqkv_ref[:, h * LANE:(h + 1) * LANE] = dqs[h]
            dqkv_ref[:, _GDN_K + h * LANE:_GDN_K + (h + 1) * LANE] = dks[h]
            dqkv_ref[:, _GDN_V + h * LANE:_GDN_V + (h + 1) * LANE] = dvs[h]
            dst_ref[h] = dsts[h]
        dsm_ref[...] = dsmi_ref[...] + dsm
        _acc(dalog_ref, dalog, ci == 0)
        _acc(ddtb_ref, ddtb, ci == 0)

    par = pl.BlockSpec((1, LANE), lambda ci: (0, 0))
    rev = lambda ci: nc - 1 - ci
    return pl.pallas_call(
        body, name=name,
        out_shape=[jax.ShapeDtypeStruct((t, GDN_QKV_DIM), F32), jax.ShapeDtypeStruct((t, LANE), F32),
                   jax.ShapeDtypeStruct((1, LANE), F32), jax.ShapeDtypeStruct((1, LANE), F32)],
        grid=(nc,),
        in_specs=[pl.BlockSpec((CHUNK, GDN_QKV_DIM), lambda ci: (rev(ci), 0)),
                  pl.BlockSpec((CHUNK, LANE), lambda ci: (rev(ci), small_blk)),
                  par, par,
                  pl.BlockSpec((1, nh, LANE, LANE), lambda ci: (rev(ci), 0, 0, 0)),
                  pl.BlockSpec((1, nh, CHUNK, CHUNK), lambda ci: (rev(ci), 0, 0, 0)),
                  pl.BlockSpec((CHUNK, GDN_HEADS * GDN_HEAD), lambda ci: (rev(ci), 0)),
                  pl.BlockSpec((CHUNK, LANE), lambda ci: (rev(ci), 0))],
        out_specs=[pl.BlockSpec((CHUNK, GDN_QKV_DIM), lambda ci: (rev(ci), 0)),
                   pl.BlockSpec((CHUNK, LANE), lambda ci: (rev(ci), 0)),
                   par, par],
        scratch_shapes=[pltpu.VMEM((nh, LANE, LANE), F32)],
        compiler_params=_cparams(("arbitrary",)),
    )(qkv, proj, alog, dtb, s_in, xinv, do, dsm_in)


def _row(arr, w, c0=0, moves=False):
    return (arr, "row", w, c0, moves)


def _par(arr, w, c0=0, moves=False):
    return (arr, "par", w, c0, moves)


def matmul_add(name, a, b, res):
    (m, k), (_, n) = a.shape, b.shape
    tm, tn, tk = _matmul_tiles(m, n, n, k, a.dtype.itemsize, b.dtype.itemsize, 4 + res.dtype.itemsize)
    nk = k // tk

    def body_acc(a_ref, b_ref, r_ref, o_ref, acc_ref):
        kk = pl.program_id(2)

        @pl.when(kk == 0)
        def _():
            acc_ref[...] = r_ref[...]

        acc_ref[...] += _dot(a_ref[...], b_ref[...])

        @pl.when(kk == nk - 1)
        def _():
            o_ref[...] = acc_ref[...]

    def body_one(a_ref, b_ref, r_ref, o_ref):
        o_ref[...] = r_ref[...] + _dot(a_ref[...], b_ref[...])

    body = body_one if nk == 1 else body_acc
    return pl.pallas_call(
        body, name=name,
        out_shape=jax.ShapeDtypeStruct((m, n), F32),
        grid=(m // tm, n // tn, nk),
        in_specs=[pl.BlockSpec((tm, tk), lambda i, j, kk: (i, kk)),
                  pl.BlockSpec((tk, tn), lambda i, j, kk: (kk, j)),
                  pl.BlockSpec((tm, tn), lambda i, j, kk: (i, j))],
        out_specs=pl.BlockSpec((tm, tn), lambda i, j, kk: (i, j)),
        scratch_shapes=[] if nk == 1 else [pltpu.VMEM((tm, tn), F32)],
        compiler_params=_cparams(("parallel", "parallel", "arbitrary")),
    )(a, b, res)


def layer_fwd(l, x, w):
    t = x.shape[0]
    rt = min(256, t)
    s = {"x": x}
    s["h"] = rowwise_fwd(f"norm_mix_l{l}", f_rmsnorm, t, rt, 1,
                         [_row(x, D_MODEL), _par(w["norm_mix_w"], D_MODEL)], [(D_MODEL, BF16)])[0]
    s["proj"] = matmul(f"in_proj_l{l}", s["h"], w["w_in"], "nn")
    s["xbc"] = conv_fwd(f"ssm_conv_l{l}", s["proj"], AL_XBC, SSM_CONV_DIM, w["ssm_conv_w"], w["ssm_conv_b"],
                        tile=min(512, t))
    s["qkv"] = conv_fwd(f"gdn_conv_l{l}", s["proj"], AL_QKV, GDN_QKV_DIM, w["gdn_conv_w"], w["gdn_conv_b"],
                        tile=min(512, t))
    s["y_scan"], s["ssd_sin"] = ssd_scan_fwd(f"ssd_scan_l{l}", s["xbc"], s["proj"], w["ssm_dt_bias"],
                                             w["ssm_a_log"], w["ssm_d"])
    s["o_scan"], s["gdn_sin"], s["gdn_x"] = gdn_scan_fwd(f"gdn_scan_l{l}", s["qkv"], s["proj"],
                                                         w["gdn_a_log"], w["gdn_dt_bias"])
    s["y_ssm"] = rowwise_fwd(f"ssd_post_l{l}", f_ssd_post, t, rt, 2,
                             [_row(s["y_scan"], 512, 0, True), _row(s["proj"], 512, AL_Z // 512, True),
                              _par(w["ssm_norm_w"], 512, 0, True)], [(512, BF16)])[0]
    s["y_gdn"] = rowwise_fwd(f"gdn_post_l{l}", f_gdn_post, t, rt, GDN_HEADS,
                             [_row(s["o_scan"], LANE, 0, True), _row(s["proj"], LANE, AL_GZ // LANE, True),
                              _par(w["gdn_norm_w"], LANE)], [(LANE, BF16)])[0]
    if "late_weights" in w:
        late = w["late_weights"](s["y_gdn"])
        w = {k: v for k, v in {**w, **late}.items() if k != "late_weights"}
    s["w"] = w
    s["p1"] = matmul(f"proj_ssm_l{l}", s["y_ssm"], w["w_proj_ssm"], "nn")
    s["p2"] = matmul(f"proj_gdn_l{l}", s["y_gdn"], w["w_proj_gdn"], "nn")
    s["merged"] = rowwise_fwd(f"merge_l{l}", f_merge, t, rt, 2,
                              [_row(s["proj"], 512, AL_GS // 512, True), _row(s["p1"], 512, 0, True),
                               _row(s["proj"], 512, AL_GG // 512, True), _row(s["p2"], 512, 0, True)],
                              [(512, BF16)])[0]
    s["x1"] = matmul_add(f"out_proj_l{l}", s["merged"], w["w_out"], x)
    s["h2"] = rowwise_fwd(f"norm_ffn_l{l}", f_rmsnorm, t, rt, 1,
                          [_row(s["x1"], D_MODEL), _par(w["norm_ffn_w"], D_MODEL)], [(D_MODEL, BF16)])[0]
    s["gu"] = matmul(f"ffn_in_l{l}", s["h2"], w["w_ffn_in"], "nn")
    s["act"] = rowwise_fwd(f"swiglu_l{l}", f_swiglu, t, rt, FFN_HIDDEN // 256,
                           [_row(s["gu"], 256, 0, True), _row(s["gu"], 256, FFN_HIDDEN // 256, True)],
                           [(256, BF16)])[0]
    x2 = matmul_add(f"ffn_down_l{l}", s["act"], w["w_ffn_down"], s["x1"])
    return x2, s


IN_SHARD = IN_DIM // 4
IN_SHARD_PAD = 2304


def _aligned_to_shards(g):
    orig = jnp.concatenate([g[:, 0:2560], g[:, AL_SMALL:AL_SMALL + 16], g[:, 2560:6656],
                            g[:, AL_SMALL + 16:AL_SMALL + 32], g[:, 6656:8704]], axis=1)
    return jnp.stack([jnp.pad(orig[:, j * IN_SHARD:(j + 1) * IN_SHARD], ((0, 0), (0, IN_SHARD_PAD - IN_SHARD)))
                      for j in range(N_CHIPS)])


def layer_bwd(l, dx2, w, s):
    t = dx2.shape[0]
    rt = min(256, t)
    ct = min(512, t)
    g = {}
    dact = matmul(f"ffn_down_dx_l{l}", dx2, w["w_ffn_down"], "nt")
    g["w_ffn_down"] = matmul(f"ffn_down_dw_l{l}", s["act"], dx2, "tn")
    nf = FFN_HIDDEN // 256
    dgate, dup = rowwise_bwd(f"swiglu_bwd_l{l}", f_swiglu, t, rt, nf,
                             [_row(s["gu"], 256, 0, True), _row(s["gu"], 256, nf, True)], [True, True],
                             [_row(dact, 256, 0, True)], row_dtypes={0: BF16, 1: BF16})
    dgu = jnp.concatenate([dgate, dup], axis=1)
    dh2 = matmul(f"ffn_in_dx_l{l}", dgu, w["w_ffn_in"], "nt")
    g["w_ffn_in"] = matmul(f"ffn_in_dw_l{l}", s["h2"], dgu, "tn", chip_major=True)
    dx1, g["norm_ffn_w"] = rowwise_bwd(f"norm_ffn_bwd_l{l}", f_rmsnorm, t, rt, 1,
                                       [_row(s["x1"], D_MODEL), _par(w["norm_ffn_w"], D_MODEL)], [True, True],
                                       [_row(dh2, D_MODEL)], addends={0: _row(dx2, D_MODEL)})
    dmerged = matmul(f"out_proj_dx_l{l}", dx1, w["w_out"], "nt")
    g["w_out"] = matmul(f"out_proj_dw_l{l}", s["merged"], dx1, "tn")
    dgs, dp1, dgg, dp2 = rowwise_bwd(
        f"merge_bwd_l{l}", f_merge, t, rt, 2,
        [_row(s["proj"], 512, AL_GS // 512, True), _row(s["p1"], 512, 0, True),
         _row(s["proj"], 512, AL_GG // 512, True), _row(s["p2"], 512, 0, True)], [True] * 4,
        [_row(dmerged, 512, 0, True)], row_dtypes={0: BF16, 1: BF16, 2: BF16, 3: BF16})
    dy_ssm = matmul(f"proj_ssm_dx_l{l}", dp1, w["w_proj_ssm"], "nt")
    g["w_proj_ssm"] = matmul(f"proj_ssm_dw_l{l}", s["y_ssm"], dp1, "tn")
    dy_gdn = matmul(f"proj_gdn_dx_l{l}", dp2, w["w_proj_gdn"], "nt")
    g["w_proj_gdn"] = matmul(f"proj_gdn_dw_l{l}", s["y_gdn"], dp2, "tn")
    dy_scan, dz, g["ssm_norm_w"] = rowwise_bwd(
        f"ssd_post_bwd_l{l}", f_ssd_post, t, rt, 2,
        [_row(s["y_scan"], 512, 0, True), _row(s["proj"], 512, AL_Z // 512, True),
         _par(w["ssm_norm_w"], 512, 0, True)], [True] * 3, [_row(dy_ssm, 512, 0, True)], row_dtypes={1: BF16})
    dxbc_act, dsm, g["ssm_dt_bias"], g["ssm_a_log"], g["ssm_d"] = ssd_scan_bwd(
        f"ssd_scan_bwd_l{l}", s["xbc"], s["proj"], w["ssm_dt_bias"], w["ssm_a_log"], w["ssm_d"],
        s["ssd_sin"], dy_scan)
    dxbc, g["ssm_conv_w"], g["ssm_conv_b"] = conv_bwd(
        f"ssm_conv_bwd_l{l}", s["proj"], AL_XBC, SSM_CONV_DIM, w["ssm_conv_w"], w["ssm_conv_b"], dxbc_act, tile=ct)
    do_scan, dgz, g["gdn_norm_w"] = rowwise_bwd(
        f"gdn_post_bwd_l{l}", f_gdn_post, t, rt, GDN_HEADS,
        [_row(s["o_scan"], LANE, 0, True), _row(s["proj"], LANE, AL_GZ // LANE, True),
         _par(w["gdn_norm_w"], LANE)], [True] * 3, [_row(dy_gdn, LANE, 0, True)], row_dtypes={1: BF16})
    dqkv_act, dsm, g["gdn_a_log"], g["gdn_dt_bias"] = gdn_scan_bwd(
        f"gdn_scan_bwd_l{l}", s["qkv"], s["proj"], w["gdn_a_log"], w["gdn_dt_bias"], s["gdn_sin"],
        s["gdn_x"], do_scan, dsm)
    dqkv, g["gdn_conv_w"], _ = conv_bwd(
        f"gdn_conv_bwd_l{l}", s["proj"], AL_QKV, GDN_QKV_DIM, w["gdn_conv_w"], w["gdn_conv_b"], dqkv_act, tile=ct)
    dproj = jnp.concatenate([dz, dxbc, dqkv, dgz, dgs, dgg, dsm.astype(BF16),
                             jnp.zeros((t, AL_DIM - AL_SMALL - LANE), BF16)], axis=1)
    dh = matmul(f"in_proj_dx_l{l}", dproj, w["w_in"], "nt")
    g["w_in"] = matmul(f"in_proj_dw_l{l}", s["h"], dproj, "tn")
    dx0, g["norm_mix_w"] = rowwise_bwd(f"norm_mix_bwd_l{l}", f_rmsnorm, t, rt, 1,
                                       [_row(s["x"], D_MODEL), _par(w["norm_mix_w"], D_MODEL)], [True, True],
                                       [_row(dh, D_MODEL)], addends={0: _row(dx1, D_MODEL)})
    return dx0, g


def _align_w_in(w):
    pad = jnp.zeros((w.shape[0], AL_DIM - AL_SMALL - 32), w.dtype)
    return jnp.concatenate([w[:, 0:2560], w[:, 2576:6672], w[:, 6688:8736],
                            w[:, 2560:2576], w[:, 6672:6688], pad], axis=1)


def _pad_lane(v, at=0):
    return jnp.pad(v[None], ((0, 0), (at, LANE - at - v.shape[0])))


def local_step(x, target, full, gathered_of=None, on_layer_grads=None):
    if gathered_of is None:
        gathered_of = lambda l, after: {n: full[n][l] for n, _ in SHARDED}
    ws, saved = [], []
    h = x
    for l in range(DEPTH):
        gw = gathered_of(l, h)
        ws.append({
            "norm_mix_w": full["norm_mix_w"][l][None], "w_in": _align_w_in(gw["w_in"]),
            "ssm_conv_w": gw["ssm_conv_w"], "ssm_conv_b": full["ssm_conv_b"][l][None],
            "ssm_dt_bias": _pad_lane(full["ssm_dt_bias"][l]), "ssm_a_log": _pad_lane(full["ssm_a_log"][l]),
            "ssm_d": _pad_lane(full["ssm_d"][l]), "ssm_norm_w": full["ssm_norm_w"][l][None],
            "gdn_conv_w": gw["gdn_conv_w"], "gdn_conv_b": jnp.zeros((1, GDN_QKV_DIM), F32),
            "gdn_a_log": _pad_lane(full["gdn_a_log"][l], SM_A),
            "gdn_dt_bias": _pad_lane(full["gdn_dt_bias"][l], SM_A),
            "gdn_norm_w": full["gdn_norm_w"][l][None],
            "norm_ffn_w": full["norm_ffn_w"][l][None],
            **{n: gw[n] for n in ("w_proj_ssm", "w_proj_gdn", "w_out", "w_ffn_in", "w_ffn_down", "late_weights")
               if n in gw},
        })
        h, s = layer_fwd(l, h, ws[l])
        ws[l] = s.pop("w")
        saved.append(s)
    loss, dx, g_final = final_loss("final_loss", h, target, full["final_norm_w"][None], tile=min(256, x.shape[0]))
    per_layer = [None] * DEPTH
    matmul_grads = [None] * DEPTH
    for l in reversed(range(DEPTH)):
        dx, per_layer[l] = layer_bwd(l, dx, ws[l], saved[l])
        matmul_grads[l] = {n: per_layer[l].pop(n) for n, _ in BIG}
        if on_layer_grads is not None:
            dx = on_layer_grads(l, matmul_grads[l], dx)
    grads = {"final_norm_w": g_final[0]}
    if on_layer_grads is None:
        grads.update({n: jnp.stack([matmul_grads[l][n] for l in range(DEPTH)]) for n, _ in BIG})
    for name in per_layer[0]:
        rows = []
        for l in range(DEPTH):
            gl = per_layer[l][name]
            if name in ("ssm_dt_bias", "ssm_a_log", "ssm_d"):
                gl = gl[0, :SSM_HEADS]
            elif name in ("gdn_a_log", "gdn_dt_bias"):
                gl = gl[0, SM_A:SM_A + GDN_HEADS]
            elif name in ("norm_mix_w", "ssm_conv_b", "ssm_norm_w", "gdn_norm_w", "norm_ffn_w"):
                gl = gl[0]
            rows.append(gl)
        grads[name] = jnp.stack(rows)
    return loss, dx, grads


MESH = pl.DeviceIdType.MESH
HBM = pl.BlockSpec(memory_space=pltpu.HBM)
N_DEV = 8


def _pos():
    return lax.axis_index("x"), lax.axis_index("y"), lax.axis_index("c")


def _rcopy(src, dst, send_sem, recv_sem, dev):
    return pltpu.make_async_remote_copy(src_ref=src, dst_ref=dst, send_sem=send_sem, recv_sem=recv_sem,
                                        device_id=dev, device_id_type=MESH)


RELATIONS = (2, 1, 3)


def _related_chip(x, y, mask):
    return (1 - x if mask & 2 else x, 1 - y if mask & 1 else y)


def weights_gather(name, bufs):
    n = len(bufs)

    def body(*refs):
        outs, send_sems, recv_sems = refs[n:2 * n], refs[2 * n], refs[2 * n + 1]
        x, y, c = _pos()
        sib = (x, y, 1 - c)
        sends = []
        for i, a in enumerate(outs):
            for k, m in enumerate(RELATIONS):
                px, py = _related_chip(x, y, m)
                cp = _rcopy(a.at[0, c], a.at[m, c], send_sems.at[6 * i + k], recv_sems.at[6 * i + k], (px, py, c))
                cp.start()
                sends.append(cp)
        for i, a in enumerate(outs):
            for k, m in enumerate(RELATIONS):
                px, py = _related_chip(x, y, m)
                _rcopy(a.at[0, c], a.at[m, c], send_sems.at[6 * i + k], recv_sems.at[6 * i + k],
                       (px, py, c)).wait_recv()
                fw = _rcopy(a.at[m, c], a.at[m, c], send_sems.at[6 * i + 3 + k], recv_sems.at[6 * i + 3 + k], sib)
                fw.start()
                sends.append(fw)
        for i, a in enumerate(outs):
            for k, m in enumerate(RELATIONS):
                _rcopy(a.at[m, 1 - c], a.at[m, 1 - c], send_sems.at[6 * i + 3 + k], recv_sems.at[6 * i + 3 + k],
                       sib).wait_recv()
        for cp in sends:
            cp.wait_send()

    return pl.pallas_call(
        body, name=name, out_shape=[jax.ShapeDtypeStruct(b.shape, b.dtype) for b in bufs],
        in_specs=[HBM] * n, out_specs=[HBM] * n,
        input_output_aliases={i: i for i in range(n)},
        scratch_shapes=[pltpu.SemaphoreType.DMA((6 * n,)), pltpu.SemaphoreType.DMA((6 * n,))],
    )(*bufs)


SEM = pl.BlockSpec(memory_space=pltpu.SEMAPHORE)
DATAFLOW = pltpu.SideEffectType.DATAFLOW_SIDE_EFFECTING


def gather_start(name, bufs, after):
    n = len(bufs)

    def body(*refs):
        ins = refs[:n]
        send_sems, recv_sems, token = refs[n + 1], refs[n + 2], refs[2 * n + 3]
        x, y, c = _pos()
        for i, a in enumerate(ins):
            for k, m in enumerate(RELATIONS):
                px, py = _related_chip(x, y, m)
                _rcopy(a.at[0, c], a.at[m, c], send_sems.at[3 * i + k], recv_sems.at[3 * i + k], (px, py, c)).start()
        token[...] = jnp.zeros_like(token)

    res = pl.pallas_call(
        body, name=name,
        out_shape=(pltpu.SemaphoreType.DMA((3 * n,)), pltpu.SemaphoreType.DMA((3 * n,)),
                   *[pltpu.HBM(b.shape, b.dtype) for b in bufs], jax.ShapeDtypeStruct((8, LANE), F32)),
        in_specs=[HBM] * n + [pl.BlockSpec(memory_space=pl.ANY)],
        out_specs=(SEM, SEM, *[HBM] * n, pl.BlockSpec(memory_space=pltpu.VMEM)),
        input_output_aliases={i: 2 + i for i in range(n)},
        compiler_params=pltpu.CompilerParams(has_side_effects=DATAFLOW),
    )(*[pltpu.with_memory_space_constraint(b, pltpu.HBM) for b in bufs], after)
    return res[0], res[1], list(res[2:2 + n]), res[2 + n]


def gather_wait(name, send_sems, recv_sems, bufs, after):
    n = len(bufs)

    def body(*refs):
        ins, ssem, rsem = refs[:n], refs[n], refs[n + 1]
        x, y, c = _pos()
        for i, a in enumerate(ins):
            for k, m in enumerate(RELATIONS):
                px, py = _related_chip(x, y, m)
                cp = _rcopy(a.at[0, c], a.at[m, c], ssem.at[3 * i + k], rsem.at[3 * i + k], (px, py, c))
                cp.wait_send()
                cp.wait_recv()

    return pl.pallas_call(
        body, name=name,
        out_shape=[pltpu.HBM(b.shape, b.dtype) for b in bufs],
        in_specs=[HBM] * n + [SEM, SEM, pl.BlockSpec(memory_space=pl.ANY)],
        out_specs=[HBM] * n,
        input_output_aliases={i: i for i in range(n)},
        compiler_params=pltpu.CompilerParams(has_side_effects=DATAFLOW),
    )(*bufs, send_sems, recv_sems, after)


def weights_forward(name, bufs):
    n = len(bufs)

    def body(*refs):
        outs, send_sems, recv_sems = refs[n:2 * n], refs[2 * n], refs[2 * n + 1]
        x, y, c = _pos()
        sib = (x, y, 1 - c)
        sends = []
        for i, a in enumerate(outs):
            for k, m in enumerate(RELATIONS):
                fw = _rcopy(a.at[m, c], a.at[m, c], send_sems.at[3 * i + k], recv_sems.at[3 * i + k], sib)
                fw.start()
                sends.append(fw)
        for i, a in enumerate(outs):
            for k, m in enumerate(RELATIONS):
                _rcopy(a.at[m, 1 - c], a.at[m, 1 - c], send_sems.at[3 * i + k], recv_sems.at[3 * i + k],
                       sib).wait_recv()
        for cp in sends:
            cp.wait_send()

    return pl.pallas_call(
        body, name=name, out_shape=[jax.ShapeDtypeStruct(b.shape, b.dtype) for b in bufs],
        in_specs=[HBM] * n, out_specs=[HBM] * n,
        input_output_aliases={i: i for i in range(n)},
        scratch_shapes=[pltpu.SemaphoreType.DMA((3 * n,)), pltpu.SemaphoreType.DMA((3 * n,))],
    )(*bufs)


def pair_swap(name, gs):
    n = len(gs)
    offs = [0]
    for g in gs:
        offs.append(offs[-1] + g.shape[0])

    def body(*refs):
        srcs, outs, send_sems, recv_sems = refs[:n], refs[n:2 * n], refs[2 * n], refs[2 * n + 1]
        x, y, c = _pos()
        cps = [_rcopy(s.at[j, 1 - c], o.at[j], send_sems.at[offs[i] + j], recv_sems.at[offs[i] + j], (x, y, 1 - c))
               for i, (s, o) in enumerate(zip(srcs, outs)) for j in range(s.shape[0])]
        for cp in cps:
            cp.start()
        for cp in cps:
            cp.wait()

    return pl.pallas_call(
        body, name=name, out_shape=[jax.ShapeDtypeStruct(g.shape[:1] + g.shape[2:], g.dtype) for g in gs],
        in_specs=[HBM] * n, out_specs=[HBM] * n,
        scratch_shapes=[pltpu.SemaphoreType.DMA((offs[-1],)), pltpu.SemaphoreType.DMA((offs[-1],))],
    )(*gs)


def scatter_start(name, ss, after):
    n = len(ss)
    lands = [lax.empty((3,) + s.shape[1:], s.dtype) for s in ss]

    def body(*refs):
        srcs, dsts = refs[:n], refs[n:2 * n]
        send_sems, recv_sems, token = refs[2 * n + 1], refs[2 * n + 2], refs[4 * n + 3]
        x, y, c = _pos()
        for i, (s, o) in enumerate(zip(srcs, dsts)):
            for k, m in enumerate(RELATIONS):
                px, py = _related_chip(x, y, m)
                _rcopy(s.at[2 * px + py], o.at[k], send_sems.at[3 * i + k], recv_sems.at[3 * i + k],
                       (px, py, c)).start()
        token[...] = jnp.zeros_like(token)

    both = list(ss) + lands
    res = pl.pallas_call(
        body, name=name,
        out_shape=(pltpu.SemaphoreType.DMA((3 * n,)), pltpu.SemaphoreType.DMA((3 * n,)),
                   *[pltpu.HBM(b.shape, b.dtype) for b in both], jax.ShapeDtypeStruct((8, LANE), F32)),
        in_specs=[HBM] * (2 * n) + [pl.BlockSpec(memory_space=pl.ANY)],
        out_specs=(SEM, SEM, *[HBM] * (2 * n), pl.BlockSpec(memory_space=pltpu.VMEM)),
        input_output_aliases={i: 2 + i for i in range(2 * n)},
        compiler_params=pltpu.CompilerParams(has_side_effects=DATAFLOW),
    )(*[pltpu.with_memory_space_constraint(b, pltpu.HBM) for b in both], after)
    return res[0], res[1], list(res[2:2 + n]), list(res[2 + n:2 + 2 * n]), res[2 + 2 * n]


def scatter_wait(name, send_sems, recv_sems, ss, lands, after):
    n = len(ss)

    def body(*refs):
        srcs, dsts, ssem, rsem = refs[:n], refs[n:2 * n], refs[2 * n], refs[2 * n + 1]
        x, y, c = _pos()
        for i, (s, o) in enumerate(zip(srcs, dsts)):
            for k, m in enumerate(RELATIONS):
                px, py = _related_chip(x, y, m)
                cp = _rcopy(s.at[2 * px + py], o.at[k], ssem.at[3 * i + k], rsem.at[3 * i + k], (px, py, c))
                cp.wait_send()
                cp.wait_recv()

    both = list(ss) + list(lands)
    res = pl.pallas_call(
        body, name=name,
        out_shape=[pltpu.HBM(b.shape, b.dtype) for b in both],
        in_specs=[HBM] * (2 * n) + [SEM, SEM, pl.BlockSpec(memory_space=pl.ANY)],
        out_specs=[HBM] * (2 * n),
        input_output_aliases={i: i for i in range(2 * n)},
        compiler_params=pltpu.CompilerParams(has_side_effects=DATAFLOW),
    )(*both, send_sems, recv_sems, after)
    return list(res[:n]), list(res[n:])


def pair_share(name, bufs):
    n = len(bufs)

    def body(*refs):
        outs, send_sems, recv_sems = refs[n:2 * n], refs[2 * n], refs[2 * n + 1]
        x, y, c = _pos()
        sends = []
        for i, o in enumerate(outs):
            cp = _rcopy(o.at[c], o.at[c], send_sems.at[i], recv_sems.at[i], (x, y, 1 - c))
            cp.start()
            sends.append(cp)
        for i, o in enumerate(outs):
            _rcopy(o.at[1 - c], o.at[1 - c], send_sems.at[i], recv_sems.at[i], (x, y, 1 - c)).wait_recv()
        for cp in sends:
            cp.wait_send()

    return pl.pallas_call(
        body, name=name, out_shape=[jax.ShapeDtypeStruct(b.shape, b.dtype) for b in bufs],
        in_specs=[HBM] * n, out_specs=[HBM] * n,
        input_output_aliases={i: i for i in range(n)},
        scratch_shapes=[pltpu.SemaphoreType.DMA((n,)), pltpu.SemaphoreType.DMA((n,))],
    )(*bufs)


def all_allgather(name, buf):
    r, cd = buf.shape

    def body(src, out, send_sems, recv_sems, lsem):
        x, y, c = _pos()
        me = 4 * x + 2 * y + c
        local = pltpu.make_async_copy(src, out.at[me], lsem)
        local.start()

        def peer(mask):
            px = 1 - x if mask & 4 else x
            py = 1 - y if mask & 2 else y
            pc = 1 - c if mask & 1 else c
            return px, py, pc

        sends = []
        for mask in range(1, N_DEV):
            cp = _rcopy(src, out.at[me], send_sems.at[mask - 1], recv_sems.at[mask - 1], peer(mask))
            cp.start()
            sends.append(cp)
        for mask in range(1, N_DEV):
            px, py, pc = peer(mask)
            _rcopy(src, out.at[4 * px + 2 * py + pc], send_sems.at[mask - 1], recv_sems.at[mask - 1],
                   (px, py, pc)).wait_recv()
        for cp in sends:
            cp.wait_send()
        local.wait()

    return pl.pallas_call(
        body, name=name, out_shape=jax.ShapeDtypeStruct((N_DEV, r, cd), buf.dtype),
        in_specs=[HBM], out_specs=HBM,
        scratch_shapes=[pltpu.SemaphoreType.DMA((N_DEV - 1,)), pltpu.SemaphoreType.DMA((N_DEV - 1,)),
                        pltpu.SemaphoreType.DMA(())],
    )(buf)


ELEMENTWISE_BLOCK_BYTES = 2 << 20


def _row_block(rows, cols):
    for cand in (1024, 512, 256, 128, 64, 32, 16):
        if rows % cand == 0 and cand * cols * 4 <= ELEMENTWISE_BLOCK_BYTES:
            return cand
    return rows


def chip_sum(name, s, r, me, c):
    _, a, b = s.shape
    tr = _row_block(a, b)

    def body(idx_ref, s_ref, r_ref, o_ref):
        del idx_ref
        acc = s_ref[...].astype(F32)
        for k in range(3):
            acc = acc + r_ref[k].astype(F32)
        o_ref[...] = acc

    return pl.pallas_call(
        body, name=name, out_shape=jax.ShapeDtypeStruct((2, a, b), F32),
        grid_spec=pltpu.PrefetchScalarGridSpec(
            num_scalar_prefetch=1, grid=(a // tr,),
            in_specs=[pl.BlockSpec((None, tr, b), lambda i, idx: (idx[0], i, 0)),
                      pl.BlockSpec((3, tr, b), lambda i, idx: (0, i, 0))],
            out_specs=pl.BlockSpec((None, tr, b), lambda i, idx: (idx[1], i, 0))),
        compiler_params=_cparams(("arbitrary",)),
    )(jnp.stack([me, c]).astype(jnp.int32), s, r)


def pair_add(name, p, recv, c):
    nj, _, rh, cd = p.shape
    tr = _row_block(rh, cd)

    def body(c_ref, p_ref, r_ref, o_ref):
        del c_ref
        o_ref[...] = (p_ref[0] + r_ref[...]).astype(o_ref.dtype)

    return pl.pallas_call(
        body, name=name, out_shape=jax.ShapeDtypeStruct((nj, rh, cd), BF16),
        grid_spec=pltpu.PrefetchScalarGridSpec(
            num_scalar_prefetch=1, grid=(nj, rh // tr),
            in_specs=[pl.BlockSpec((1, 1, tr, cd), lambda j, i, c_ref: (j, c_ref[0], i, 0)),
                      pl.BlockSpec((1, tr, cd), lambda j, i, c_ref: (j, i, 0))],
            out_specs=pl.BlockSpec((1, tr, cd), lambda j, i, c_ref: (j, i, 0))),
        compiler_params=_cparams(("arbitrary", "arbitrary")),
    )(jnp.reshape(c, (1,)).astype(jnp.int32), p, recv)


def slab_sum(name, a):
    n, r, cd = a.shape
    tr = _pick(r, (256, 128, 64, 32, 16, 8))

    def body(a_ref, o_ref):
        acc = a_ref[0].astype(F32)
        for j in range(1, n):
            acc = acc + a_ref[j].astype(F32)
        o_ref[...] = acc

    return pl.pallas_call(
        body, name=name, out_shape=jax.ShapeDtypeStruct((r, cd), F32),
        grid=(r // tr,),
        in_specs=[pl.BlockSpec((n, tr, cd), lambda i: (0, i, 0))],
        out_specs=pl.BlockSpec((tr, cd), lambda i: (i, 0)),
        compiler_params=_cparams(("arbitrary",)),
    )(a)


ADAM_C1 = 1.0 - ADAM_B1 ** ADAM_STEP
ADAM_C2 = 1.0 - ADAM_B2 ** ADAM_STEP


def adamw(name, w, g, m, v):
    r, cd = w.shape
    tr = r
    for cand in (512, 256, 128, 64, 32, 16, 8):
        if r % cand == 0 and cand * cd * 4 <= (1 << 20):
            tr = cand
            break

    def body(w_ref, g_ref, m_ref, v_ref, d_ref, nm_ref, nv_ref):
        gv = g_ref[...]
        nm = ADAM_B1 * m_ref[...] + (1.0 - ADAM_B1) * gv
        nv = ADAM_B2 * v_ref[...] + (1.0 - ADAM_B2) * (gv * gv)
        m_hat = nm / ADAM_C1
        v_hat = nv / ADAM_C2
        d_ref[...] = -ADAM_LR * (m_hat / (jnp.sqrt(v_hat) + ADAM_EPS) + ADAM_WD * w_ref[...])
        nm_ref[...] = nm
        nv_ref[...] = nv

    spec = pl.BlockSpec((tr, cd), lambda i: (i, 0))
    sd = jax.ShapeDtypeStruct((r, cd), F32)
    return pl.pallas_call(
        body, name=name, out_shape=[sd, sd, sd], grid=(r // tr,),
        in_specs=[spec] * 4, out_specs=[spec] * 3,
        compiler_params=_cparams(("arbitrary",)),
    )(w, g, m, v)


WEIGHTS = ("norm_mix_w", "w_in", "ssm_conv_w", "ssm_conv_b", "ssm_dt_bias", "ssm_a_log", "ssm_d", "ssm_norm_w",
           "gdn_conv_w", "gdn_a_log", "gdn_dt_bias", "gdn_norm_w", "w_proj_ssm", "w_proj_gdn", "w_out",
           "norm_ffn_w", "w_ffn_in", "w_ffn_down", "final_norm_w")
BIG = (("w_in", 2), ("w_proj_ssm", 1), ("w_proj_gdn", 1), ("w_out", 1), ("w_ffn_in", 2), ("w_ffn_down", 1))
CONVW = (("ssm_conv_w", 2), ("gdn_conv_w", 2))
SHARDED = BIG + CONVW
SMALL = tuple(n for n in WEIGHTS if n not in dict(SHARDED))


def _unpack(buf, shapes, lead=()):
    flat = buf.reshape(lead + (-1,))
    out, o = [], 0
    for shp in shapes:
        n = math.prod(shp)
        out.append(flat[..., o:o + n].reshape(lead + tuple(shp)))
        o += n
    return out


def _pack_rows(arrs, lead=(), mult=8):
    nl = len(lead)
    flat = jnp.concatenate([a.reshape(lead + (-1,)) for a in arrs], axis=nl)
    n = flat.shape[nl]
    rows = -(-n // (mult * LANE)) * mult
    flat = jnp.pad(flat, [(0, 0)] * nl + [(0, rows * LANE - n)])
    return flat.reshape(lead + (rows, LANE))


def _slot_buffer(shard):
    return lax.dynamic_update_slice(lax.empty((N_CHIPS,) + shard.shape, shard.dtype), shard[None],
                                    (0,) * (shard.ndim + 1))


def kernel(x, norm_mix_w, w_in, ssm_conv_w, ssm_conv_b, ssm_dt_bias, ssm_a_log, ssm_d, ssm_norm_w, gdn_conv_w, gdn_a_log, gdn_dt_bias, gdn_norm_w, w_proj_ssm, w_proj_gdn, w_out, norm_ffn_w, w_ffn_in, w_ffn_down, final_norm_w, loss_target, m_norm_mix_w, m_w_in, m_ssm_conv_w, m_ssm_conv_b, m_ssm_dt_bias, m_ssm_a_log, m_ssm_d, m_ssm_norm_w, m_gdn_conv_w, m_gdn_a_log, m_gdn_dt_bias, m_gdn_norm_w, m_w_proj_ssm, m_w_proj_gdn, m_w_out, m_norm_ffn_w, m_w_ffn_in, m_w_ffn_down, m_final_norm_w, v_norm_mix_w, v_w_in, v_ssm_conv_w, v_ssm_conv_b, v_ssm_dt_bias, v_ssm_a_log, v_ssm_d, v_ssm_norm_w, v_gdn_conv_w, v_gdn_a_log, v_gdn_dt_bias, v_gdn_norm_w, v_w_proj_ssm, v_w_proj_gdn, v_w_out, v_norm_ffn_w, v_w_ffn_in, v_w_ffn_down, v_final_norm_w):
    wl = (norm_mix_w, w_in, ssm_conv_w, ssm_conv_b, ssm_dt_bias, ssm_a_log, ssm_d, ssm_norm_w, gdn_conv_w,
          gdn_a_log, gdn_dt_bias, gdn_norm_w, w_proj_ssm, w_proj_gdn, w_out, norm_ffn_w, w_ffn_in, w_ffn_down,
          final_norm_w)
    ml = (m_norm_mix_w, m_w_in, m_ssm_conv_w, m_ssm_conv_b, m_ssm_dt_bias, m_ssm_a_log, m_ssm_d, m_ssm_norm_w,
          m_gdn_conv_w, m_gdn_a_log, m_gdn_dt_bias, m_gdn_norm_w, m_w_proj_ssm, m_w_proj_gdn, m_w_out,
          m_norm_ffn_w, m_w_ffn_in, m_w_ffn_down, m_final_norm_w)
    vl = (v_norm_mix_w, v_w_in, v_ssm_conv_w, v_ssm_conv_b, v_ssm_dt_bias, v_ssm_a_log, v_ssm_d, v_ssm_norm_w,
          v_gdn_conv_w, v_gdn_a_log, v_gdn_dt_bias, v_gdn_norm_w, v_w_proj_ssm, v_w_proj_gdn, v_w_out,
          v_norm_ffn_w, v_w_ffn_in, v_w_ffn_down, v_final_norm_w)
    w = dict(zip(WEIGHTS, wl))
    m = dict(zip(WEIGHTS, ml))
    v = dict(zip(WEIGHTS, vl))
    x_pos, y_pos, c = _pos()
    me = 2 * x_pos + y_pos
    big = [n for n, _ in BIG]

    shards = [w[n].astype(BF16) for n in big]
    shards[0] = jnp.pad(shards[0], ((0, 0), (0, 0), (0, IN_SHARD_PAD - IN_SHARD)))
    conv_shapes = [w[n].shape[1:] for n, _ in CONVW]
    conv_pack = _pack_rows([w[n] for n, _ in CONVW], lead=(DEPTH,), mult=16)

    def slot_buffers(l):
        return [_slot_buffer(s[l].reshape((2, s.shape[1] // 2) + s.shape[2:])) for s in shards + [conv_pack]]

    def assemble(bufs, which):
        out = {}
        for g_, i in zip(bufs, which):
            by_chip = [lax.dynamic_index_in_dim(g_, jnp.bitwise_xor(me, j), 0, keepdims=False)
                       .reshape((-1,) + g_.shape[3:]) for j in range(N_CHIPS)]
            if i < 0:
                conv_parts = [_unpack(p, conv_shapes) for p in by_chip]
                for k, (n, axis) in enumerate(CONVW):
                    out[n] = jnp.concatenate([conv_parts[j][k] for j in range(N_CHIPS)], axis=axis - 1)
            else:
                n, axis = BIG[i]
                cols = IN_SHARD if n == "w_in" else by_chip[0].shape[-1]
                out[n] = jnp.concatenate([p[:, :cols] for p in by_chip], axis=axis - 1)
        return out

    everything = list(range(len(BIG))) + [-1]
    first, rest = [0, -1], everything[1:-1]
    bufs0 = slot_buffers(0)
    landed_first = weights_gather("gather_w_l0_first", [bufs0[i] for i in first])
    rest_sems = gather_start("gather_w_l0_rest_start", [bufs0[i] for i in rest], landed_first[0])
    l1_sems = gather_start("gather_w_l1_start", slot_buffers(1), rest_sems[2][0])

    def land(name, flying, after):
        send_sems, recv_sems, bufs, _ = flying
        return weights_forward(f"{name}_forward", gather_wait(f"{name}_wait", send_sems, recv_sems, bufs, after))

    def gathered_of(l, after):
        if l == 0:
            return {**assemble(landed_first, first),
                    "late_weights": lambda later: assemble(land("gather_w_l0_rest", rest_sems, later), rest)}
        return assemble(land("gather_w_l1", l1_sems, after), everything)

    token = rest_sems[3] + l1_sems[3]

    in_flight = {}

    def start_reduction(l, g_layer, dx):
        halves = [g_layer[n].reshape((1, 2, -1, AL_DIM)) if n == "w_in"
                  else g_layer[n].reshape((N_CHIPS, 2, -1) + g_layer[n].shape[-1:]) for n in big]
        from_pair = pair_swap(f"grad_pair_swap_l{l}", halves)
        chip_part = [pair_add(f"grad_pair_add_{n}_l{l}", g_, r_, c) for n, g_, r_ in zip(big, halves, from_pair)]
        chip_part[0] = _aligned_to_shards(chip_part[0][0])
        *in_flight[l], token_l = scatter_start(f"grad_scatter_start_l{l}", chip_part, dx)
        return dx + token_l[0, 0] if l > 0 else dx

    def finish_reduction(l, after):
        parts, landed = scatter_wait(f"grad_scatter_wait_l{l}", *in_flight[l], after)
        sums = [chip_sum(f"grad_chip_sum_{n}_l{l}", s_, r_, me, c) for n, s_, r_ in zip(big, parts, landed)]
        return [r_.reshape((-1,) + r_.shape[2:]) for r_ in pair_share(f"grad_pair_share_l{l}", sums)]

    full = {n: w[n] for n in SMALL}
    loss_part, grad_x, grads = local_step(x[0] + token[0, 0], loss_target[0], full, gathered_of, start_reduction)
    reduced = [None] * DEPTH
    reduced[1] = finish_reduction(1, grad_x)

    small_names = list(SMALL) + [n for n, _ in CONVW]
    small_all = all_allgather("gather_small", _pack_rows([grads[n] for n in small_names] + [loss_part[0, :1]]))
    small_sum = slab_sum("small_sum", small_all)
    small_vals = _unpack(small_sum, [grads[n].shape for n in small_names] + [(1,)])
    g_small = dict(zip(small_names, small_vals[:-1]))
    loss = small_vals[-1].reshape(())
    out_g, out_d, out_m, out_v = {}, {}, {}, {}
    d_, m_, v_ = adamw("adamw_small", *[_pack_rows([d[n] for n in SMALL]) for d in (w, g_small, m, v)])
    small_shapes = [w[n].shape for n in SMALL]
    for n, dd, mm, vv in zip(SMALL, _unpack(d_, small_shapes), _unpack(m_, small_shapes), _unpack(v_, small_shapes)):
        out_g[n], out_d[n], out_m[n], out_v[n] = g_small[n], dd, mm, vv

    reduced[0] = finish_reduction(0, d_[:1] + reduced[1][0][:1, :LANE])
    g_sharded = {n: jnp.stack([reduced[l][i] for l in range(DEPTH)]) for i, n in enumerate(big)}
    g_sharded["w_in"] = g_sharded["w_in"][:, :, :IN_SHARD]
    for n, axis in CONVW:
        size = w[n].shape[axis]
        g_sharded[n] = lax.dynamic_slice_in_dim(g_small.pop(n), me * size, size, axis=axis)

    for n, _ in SHARDED:
        shp = w[n].shape
        two = lambda a: a.reshape(-1, shp[-1])
        d_, m_, v_ = adamw(f"adamw_{n}", two(w[n]), two(g_sharded[n]), two(m[n]), two(v[n]))
        out_g[n], out_d[n], out_m[n], out_v[n] = g_sharded[n], d_.reshape(shp), m_.reshape(shp), v_.reshape(shp)

    return (loss, grad_x[None], *[out_g[n] for n in WEIGHTS], *[out_d[n] for n in WEIGHTS],
            *[out_m[n] for n in WEIGHTS], *[out_v[n] for n in WEIGHTS])
```
